```python
import jax
import jax.numpy as jnp
from jax import lax
import numpy as np

D_MODEL = 2048
BATCH = 8
SEQ = 2048
DEPTH = 1

CTX_LEN = 256
GRID_W = 64
NORM_EPS = 1e-6
N_DIR = 2
N_BRANCH = 2
HG_WIDTH = 1024
HG_HEAD_DIM = 128
HG_HEADS = HG_WIDTH // HG_HEAD_DIM
HG_CHUNK = 64
RW_WIDTH = 1024
RW_HEAD_DIM = 64
RW_HEADS = RW_WIDTH // RW_HEAD_DIM
RW_DECAY_LORA = 64
RW_A_LORA = 64
RW_GN_EPS = 64e-5
HG_COLS = 5 * HG_WIDTH
RW_SHIFT_COLS = 3 * RW_WIDTH + N_DIR * (RW_DECAY_LORA + RW_A_LORA)
RW_COLS = RW_SHIFT_COLS + RW_WIDTH
GATE_COLS = N_BRANCH * D_MODEL
N_COLS = HG_COLS + RW_COLS + GATE_COLS

kernel_name = "hybrid_hgrn2_rwkv7_prefix_block"


def rmsnorm(x, g):
    x32 = x.astype(jnp.float32)
    return x32 * lax.rsqrt(jnp.mean(x32 * x32, axis=-1, keepdims=True) + NORM_EPS) * g


def ada_modulation(cond, w, b):
    mod = jax.nn.silu(cond.astype(jnp.float32)) @ w + b
    shift, scale, gate = jnp.split(mod, 3, axis=-1)
    return shift[:, None], scale[:, None], gate[:, None]


def grid_shift(p, mu, rows, cols, vertical):
    b, t, ch = p.shape
    p4 = p.reshape(b, rows, cols, ch)
    zc = jnp.zeros_like(p4[:, :, :1])
    left = jnp.concatenate([zc, p4[:, :, :-1]], axis=2)
    right = jnp.concatenate([p4[:, :, 1:], zc], axis=2)
    out = p4 + mu[0] * (left - p4) + mu[1] * (right - p4)
    if vertical:
        zr = jnp.zeros_like(p4[:, :1])
        up = jnp.concatenate([zr, p4[:, :-1]], axis=1)
        down = jnp.concatenate([p4[:, 1:], zr], axis=1)
        out = out + mu[2] * (up - p4) + mu[3] * (down - p4)
    return out.reshape(b, t, ch)


def hgrn2_chunk_scan(q, k, v, g, s0):
    b, t, h, _ = q.shape
    dv = v.shape[-1]
    n = t // HG_CHUNK

    def to_chunks(a):
        return a.reshape(b, n, HG_CHUNK, h, a.shape[-1]).transpose(1, 0, 3, 2, 4)

    tri = jnp.tril(jnp.ones((HG_CHUNK, HG_CHUNK), dtype=bool))

    def step(s, inp):
        qi, ki, vi, gi = inp
        bc = jnp.cumsum(gi, axis=2)
        diff = bc[:, :, :, None, :] - bc[:, :, None, :, :]
        decay = jnp.where(tri[:, :, None], jnp.exp(jnp.minimum(diff, 0.0)), 0.0)
        attn = jnp.einsum('bhtk,bhsk,bhtsk->bhts', qi, ki, decay)
        o = jnp.einsum('bhts,bhsv->bhtv', attn, vi) + jnp.einsum('bhtk,bhkv->bhtv', qi * jnp.exp(bc), s)
        blast = bc[:, :, -1:, :]
        s_new = jnp.exp(blast[:, :, 0, :])[..., None] * s + jnp.einsum('bhsk,bhsv->bhkv', ki * jnp.exp(blast - bc), vi)
        return s_new, o

    s_fin, oc = lax.scan(step, s0, (to_chunks(q), to_chunks(k), to_chunks(v), to_chunks(g)))
    return oc.transpose(1, 0, 3, 2, 4).reshape(b, t, h, dv), s_fin


def hgrn2_mixer(p, lb, norm_g, s0):
    b, t, _ = p.shape
    q_raw, i_in, f_fwd, f_bwd, z = jnp.split(p, 5, axis=-1)
    heads = lambda a: a.reshape(b, t, HG_HEADS, HG_HEAD_DIM)
    flip = lambda a: jnp.flip(a, axis=1)
    q = heads(jax.nn.silu(q_raw))
    v = heads(i_in)
    fg_f = lb[0] + (1.0 - lb[0]) * jax.nn.sigmoid(f_fwd)
    fg_b = lb[1] + (1.0 - lb[1]) * jax.nn.sigmoid(f_bwd)
    o_f, s_f = hgrn2_chunk_scan(q, heads(1.0 - fg_f), v, heads(jnp.log(fg_f)), s0[0])
    o_b, s_b = hgrn2_chunk_scan(flip(q), flip(heads(1.0 - fg_b)), flip(v), flip(heads(jnp.log(fg_b))), s0[1])
    o = o_f + flip(o_b)
    o = o * lax.rsqrt(jnp.mean(o * o, axis=-1, keepdims=True) + NORM_EPS)
    out = o.reshape(b, t, HG_WIDTH) * norm_g * jax.nn.silu(z)
    return out, jnp.stack([s_f, s_b])


def rwkv7_scan(r, w, kk, kb, v, k, s0, reverse):
    tm = lambda a: jnp.moveaxis(a, 1, 0)

    def step(s, inp):
        r_t, w_t, kk_t, b_t, v_t, k_t = inp
        sa = jnp.einsum('bhvk,bhk->bhv', s, kk_t)
        s = s * w_t[:, :, None, :] - sa[..., None] * b_t[:, :, None, :] + v_t[..., None] * k_t[:, :, None, :]
        return s, jnp.einsum('bhvk,bhk->bhv', s, r_t)

    s_fin, y = lax.scan(step, s0, (tm(r), tm(w), tm(kk), tm(kb), tm(v), tm(k)), reverse=reverse)
    return jnp.moveaxis(y, 0, 1), s_fin


def rwkv7_mixer(p, rw, s0, rows, cols, vertical):
    mu, w0, w2, a0, a2, k_k, k_a, r_k, gn_g, gn_b = rw
    b, t, _ = p.shape
    heads = lambda a: a.reshape(b, t, RW_HEADS, RW_HEAD_DIM)
    sh = grid_shift(p[..., :RW_SHIFT_COLS], mu, rows, cols, vertical)
    z = p[..., RW_SHIFT_COLS:]
    r = sh[..., :RW_WIDTH]
    k = sh[..., RW_WIDTH:2 * RW_WIDTH]
    v = sh[..., 2 * RW_WIDTH:3 * RW_WIDTH]
    lora = sh[..., 3 * RW_WIDTH:]
    w_lo = lora[..., :N_DIR * RW_DECAY_LORA].reshape(b, t, N_DIR, RW_DECAY_LORA)
    a_lo = lora[..., N_DIR * RW_DECAY_LORA:].reshape(b, t, N_DIR, RW_A_LORA)
    kk = heads(k * k_k)
    kk = kk * lax.rsqrt(jnp.sum(kk * kk, axis=-1, keepdims=True) + 1e-12)
    y_sum = 0.0
    k_sum = 0.0
    states = []
    for d in range(N_DIR):
        w_log = -jax.nn.softplus(-(w0[d] + jnp.tanh(w_lo[:, :, d]) @ w2[d])) - 0.5
        decay = jnp.exp(-jnp.exp(w_log))
        a = jax.nn.sigmoid(a0[d] + a_lo[:, :, d] @ a2[d])
        k_d = k * (1.0 + (a - 1.0) * k_a)
        y_d, s_d = rwkv7_scan(heads(r), heads(decay), kk, kk * heads(a), heads(v), heads(k_d), s0[d], d == 1)
        y_sum = y_sum + y_d
        k_sum = k_sum + k_d
        states.append(s_d)
    mean = jnp.mean(y_sum, axis=-1, keepdims=True)
    var = jnp.mean(jnp.square(y_sum - mean), axis=-1, keepdims=True)
    y = ((y_sum - mean) * lax.rsqrt(var + RW_GN_EPS)).reshape(b, t, RW_WIDTH) * gn_g + gn_b
    bonus = jnp.sum(heads(r * k_sum * r_k), axis=-1, keepdims=True) * heads(v)
    out = (y + bonus.reshape(b, t, RW_WIDTH)) * jax.nn.silu(z)
    return out, jnp.stack(states)


def mixer_stream(s, cond, ada_w, ada_b, norm_g, w_in, lb, hg_norm_g, rw, hg_s0, rw_s0, rows, cols, vertical):
    shift, scale, gate = ada_modulation(cond, ada_w, ada_b)
    h = rmsnorm(s, norm_g) * (1.0 + scale) + shift
    proj = h @ w_in
    y_hg, hg_s = hgrn2_mixer(proj[..., :HG_COLS], lb, hg_norm_g, hg_s0)
    y_rw, rw_s = rwkv7_mixer(proj[..., HG_COLS:HG_COLS + RW_COLS], rw, rw_s0, rows, cols, vertical)
    return y_hg, y_rw, proj[..., HG_COLS + RW_COLS:], gate, hg_s, rw_s


def merge_branches(gate_p, y_hg, y_rw, w_hg_o, w_rw_o, w_o):
    g_hg, g_rw = jnp.split(gate_p, N_BRANCH, axis=-1)
    merged = jax.nn.sigmoid(g_hg) * (y_hg @ w_hg_o) + jax.nn.sigmoid(g_rw) * (y_rw @ w_rw_o)
    return merged @ w_o


def _fwd_setup_inputs(seed: int = 0) -> dict:
    key = jax.random.key(seed)
    ks = jax.random.split(key, 26)
    L, D = DEPTH, D_MODEL
    nrm = lambda k, shape, s: jax.random.normal(k, shape, jnp.float32) * s
    return {
        'x': nrm(ks[0], (BATCH, SEQ, D), 1.0),
        'c': nrm(ks[1], (BATCH, D), 1.0),
        'ctx': nrm(ks[2], (BATCH, CTX_LEN, D), 1.0),
        'c_ctx': nrm(ks[3], (D,), 1.0),
        'ada_w': nrm(ks[4], (L, D, 3 * D), 0.5 * D ** -0.5),
        'ada_b': nrm(ks[5], (L, 3 * D), 0.02),
        'norm_g': 1.0 + nrm(ks[6], (L, D), 0.02),
        'w_in': nrm(ks[7], (L, D, N_COLS), D ** -0.5),
        'hg_lb': nrm(ks[8], (N_DIR, L + 1, HG_WIDTH), 1.0),
        'hg_norm_g': 1.0 + nrm(ks[9], (L, HG_WIDTH), 0.02),
        'rw_mu': jax.random.uniform(ks[10], (L, 4, RW_SHIFT_COLS), jnp.float32, 0.0, 0.5),
        'rw_w0': jax.random.uniform(ks[11], (L, N_DIR, RW_WIDTH), jnp.float32, -5.0, 0.0),
        'rw_w2': nrm(ks[12], (L, N_DIR, RW_DECAY_LORA, RW_WIDTH), 0.5 * RW_DECAY_LORA ** -0.5),
        'rw_a0': nrm(ks[13], (L, N_DIR, RW_WIDTH), 0.1),
        'rw_a2': nrm(ks[14], (L, N_DIR, RW_A_LORA, RW_WIDTH), 0.5 * RW_A_LORA ** -0.5),
        'rw_kk': 0.85 + nrm(ks[15], (L, RW_WIDTH), 0.1),
        'rw_ka': 1.0 + nrm(ks[16], (L, RW_WIDTH), 0.1),
        'rw_rk': nrm(ks[17], (L, RW_WIDTH), 0.1),
        'rw_gn_g': 1.0 + nrm(ks[18], (L, RW_WIDTH), 0.02),
        'rw_gn_b': nrm(ks[19], (L, RW_WIDTH), 0.02),
        'w_hg_out': nrm(ks[20], (L, HG_WIDTH, D), HG_WIDTH ** -0.5),
        'w_rw_out': nrm(ks[21], (L, RW_WIDTH, D), RW_WIDTH ** -0.5),
        'w_out': nrm(ks[22], (L, D, D), D ** -0.5),
        'final_g': 1.0 + nrm(ks[23], (D,), 0.02),
    }


def _fwd_reference(x, c, ctx, c_ctx, ada_w, ada_b, norm_g, w_in, hg_lb, hg_norm_g, rw_mu, rw_w0, rw_w2, rw_a0,
              rw_a2, rw_kk, rw_ka, rw_rk, rw_gn_g, rw_gn_b, w_hg_out, w_rw_out, w_out, final_g):
    out_dtype = x.dtype
    b, seq, _ = x.shape
    ctx_len = ctx.shape[1]
    rows = seq // GRID_W
    x = x.astype(jnp.float32)
    ctx = ctx.astype(jnp.float32)
    lb_all = jnp.cumsum(jax.nn.softmax(hg_lb.astype(jnp.float32), axis=1), axis=1)
    hg_zero = jnp.zeros((N_DIR, b, HG_HEADS, HG_HEAD_DIM, HG_HEAD_DIM), jnp.float32)
    rw_zero = jnp.zeros((N_DIR, b, RW_HEADS, RW_HEAD_DIM, RW_HEAD_DIM), jnp.float32)
    for l in range(DEPTH):
        lb = lb_all[:, l]
        rw = (rw_mu[l], rw_w0[l], rw_w2[l], rw_a0[l], rw_a2[l], rw_kk[l], rw_ka[l], rw_rk[l], rw_gn_g[l], rw_gn_b[l])
        yc_hg, yc_rw, gc, gate_c, hg_s, rw_s = mixer_stream(
            ctx, c_ctx[None], ada_w[l], ada_b[l], norm_g[l], w_in[l], lb, hg_norm_g[l], rw,
            hg_zero, rw_zero, 1, ctx_len, False)
        y_hg, y_rw, gl, gate_x, _, _ = mixer_stream(
            x, c, ada_w[l], ada_b[l], norm_g[l], w_in[l], lb, hg_norm_g[l], rw,
            hg_s, rw_s, rows, GRID_W, True)
        x = x + gate_x * merge_branches(gl, y_hg, y_rw, w_hg_out[l], w_rw_out[l], w_out[l])
        if l < DEPTH - 1:
            ctx = ctx + gate_c * merge_branches(gc, yc_hg, yc_rw, w_hg_out[l], w_rw_out[l], w_out[l])
    return rmsnorm(x, final_g).astype(out_dtype)


import jax as _jax
import jax.numpy as _jnp

TWIN_FORMAT = 'train_step'
FWD_PARAMS = ['x', 'c', 'ctx', 'c_ctx', 'ada_w', 'ada_b', 'norm_g', 'w_in', 'hg_lb', 'hg_norm_g', 'rw_mu', 'rw_w0', 'rw_w2', 'rw_a0', 'rw_a2', 'rw_kk', 'rw_ka', 'rw_rk', 'rw_gn_g', 'rw_gn_b', 'w_hg_out', 'w_rw_out', 'w_out', 'final_g']
TWIN_WEIGHTS = ['c_ctx', 'ada_w', 'ada_b', 'norm_g', 'w_in', 'hg_lb', 'hg_norm_g', 'rw_mu', 'rw_w0', 'rw_w2', 'rw_a0', 'rw_a2', 'rw_kk', 'rw_ka', 'rw_rk', 'rw_gn_g', 'rw_gn_b', 'w_hg_out', 'w_rw_out', 'w_out', 'final_g']
TWIN_DIFF_INPUT = 'x'
TWIN_INPUTS = ['x', 'c', 'ctx', 'c_ctx', 'ada_w', 'ada_b', 'norm_g', 'w_in', 'hg_lb', 'hg_norm_g', 'rw_mu', 'rw_w0', 'rw_w2', 'rw_a0', 'rw_a2', 'rw_kk', 'rw_ka', 'rw_rk', 'rw_gn_g', 'rw_gn_b', 'w_hg_out', 'w_rw_out', 'w_out', 'final_g', 'loss_target', 'm_c_ctx', 'm_ada_w', 'm_ada_b', 'm_norm_g', 'm_w_in', 'm_hg_lb', 'm_hg_norm_g', 'm_rw_mu', 'm_rw_w0', 'm_rw_w2', 'm_rw_a0', 'm_rw_a2', 'm_rw_kk', 'm_rw_ka', 'm_rw_rk', 'm_rw_gn_g', 'm_rw_gn_b', 'm_w_hg_out', 'm_w_rw_out', 'm_w_out', 'm_final_g', 'v_c_ctx', 'v_ada_w', 'v_ada_b', 'v_norm_g', 'v_w_in', 'v_hg_lb', 'v_hg_norm_g', 'v_rw_mu', 'v_rw_w0', 'v_rw_w2', 'v_rw_a0', 'v_rw_a2', 'v_rw_kk', 'v_rw_ka', 'v_rw_rk', 'v_rw_gn_g', 'v_rw_gn_b', 'v_w_hg_out', 'v_w_rw_out', 'v_w_out', 'v_final_g']
TWIN_OUTPUTS = ['loss', 'grad_x', 'grad_c_ctx', 'grad_ada_w', 'grad_ada_b', 'grad_norm_g', 'grad_w_in', 'grad_hg_lb', 'grad_hg_norm_g', 'grad_rw_mu', 'grad_rw_w0', 'grad_rw_w2', 'grad_rw_a0', 'grad_rw_a2', 'grad_rw_kk', 'grad_rw_ka', 'grad_rw_rk', 'grad_rw_gn_g', 'grad_rw_gn_b', 'grad_w_hg_out', 'grad_w_rw_out', 'grad_w_out', 'grad_final_g', 'delta_c_ctx', 'delta_ada_w', 'delta_ada_b', 'delta_norm_g', 'delta_w_in', 'delta_hg_lb', 'delta_hg_norm_g', 'delta_rw_mu', 'delta_rw_w0', 'delta_rw_w2', 'delta_rw_a0', 'delta_rw_a2', 'delta_rw_kk', 'delta_rw_ka', 'delta_rw_rk', 'delta_rw_gn_g', 'delta_rw_gn_b', 'delta_w_hg_out', 'delta_w_rw_out', 'delta_w_out', 'delta_final_g', 'new_m_c_ctx', 'new_m_ada_w', 'new_m_ada_b', 'new_m_norm_g', 'new_m_w_in', 'new_m_hg_lb', 'new_m_hg_norm_g', 'new_m_rw_mu', 'new_m_rw_w0', 'new_m_rw_w2', 'new_m_rw_a0', 'new_m_rw_a2', 'new_m_rw_kk', 'new_m_rw_ka', 'new_m_rw_rk', 'new_m_rw_gn_g', 'new_m_rw_gn_b', 'new_m_w_hg_out', 'new_m_w_rw_out', 'new_m_w_out', 'new_m_final_g', 'new_v_c_ctx', 'new_v_ada_w', 'new_v_ada_b', 'new_v_norm_g', 'new_v_w_in', 'new_v_hg_lb', 'new_v_hg_norm_g', 'new_v_rw_mu', 'new_v_rw_w0', 'new_v_rw_w2', 'new_v_rw_a0', 'new_v_rw_a2', 'new_v_rw_kk', 'new_v_rw_ka', 'new_v_rw_rk', 'new_v_rw_gn_g', 'new_v_rw_gn_b', 'new_v_w_hg_out', 'new_v_w_rw_out', 'new_v_w_out', 'new_v_final_g']
TWIN_LEAF_KINDS = {'loss': 'loss', 'grad_x': 'grad_x', 'grad_c_ctx': 'grad_w', 'grad_ada_w': 'grad_w', 'grad_ada_b': 'grad_w', 'grad_norm_g': 'grad_w', 'grad_w_in': 'grad_w', 'grad_hg_lb': 'grad_w', 'grad_hg_norm_g': 'grad_w', 'grad_rw_mu': 'grad_w', 'grad_rw_w0': 'grad_w', 'grad_rw_w2': 'grad_w', 'grad_rw_a0': 'grad_w', 'grad_rw_a2': 'grad_w', 'grad_rw_kk': 'grad_w', 'grad_rw_ka': 'grad_w', 'grad_rw_rk': 'grad_w', 'grad_rw_gn_g': 'grad_w', 'grad_rw_gn_b': 'grad_w', 'grad_w_hg_out': 'grad_w', 'grad_w_rw_out': 'grad_w', 'grad_w_out': 'grad_w', 'grad_final_g': 'grad_w', 'delta_c_ctx': 'delta_w', 'delta_ada_w': 'delta_w', 'delta_ada_b': 'delta_w', 'delta_norm_g': 'delta_w', 'delta_w_in': 'delta_w', 'delta_hg_lb': 'delta_w', 'delta_hg_norm_g': 'delta_w', 'delta_rw_mu': 'delta_w', 'delta_rw_w0': 'delta_w', 'delta_rw_w2': 'delta_w', 'delta_rw_a0': 'delta_w', 'delta_rw_a2': 'delta_w', 'delta_rw_kk': 'delta_w', 'delta_rw_ka': 'delta_w', 'delta_rw_rk': 'delta_w', 'delta_rw_gn_g': 'delta_w', 'delta_rw_gn_b': 'delta_w', 'delta_w_hg_out': 'delta_w', 'delta_w_rw_out': 'delta_w', 'delta_w_out': 'delta_w', 'delta_final_g': 'delta_w', 'new_m_c_ctx': 'new_m', 'new_m_ada_w': 'new_m', 'new_m_ada_b': 'new_m', 'new_m_norm_g': 'new_m', 'new_m_w_in': 'new_m', 'new_m_hg_lb': 'new_m', 'new_m_hg_norm_g': 'new_m', 'new_m_rw_mu': 'new_m', 'new_m_rw_w0': 'new_m', 'new_m_rw_w2': 'new_m', 'new_m_rw_a0': 'new_m', 'new_m_rw_a2': 'new_m', 'new_m_rw_kk': 'new_m', 'new_m_rw_ka': 'new_m', 'new_m_rw_rk': 'new_m', 'new_m_rw_gn_g': 'new_m', 'new_m_rw_gn_b': 'new_m', 'new_m_w_hg_out': 'new_m', 'new_m_w_rw_out': 'new_m', 'new_m_w_out': 'new_m', 'new_m_final_g': 'new_m', 'new_v_c_ctx': 'new_v', 'new_v_ada_w': 'new_v', 'new_v_ada_b': 'new_v', 'new_v_norm_g': 'new_v', 'new_v_w_in': 'new_v', 'new_v_hg_lb': 'new_v', 'new_v_hg_norm_g': 'new_v', 'new_v_rw_mu': 'new_v', 'new_v_rw_w0': 'new_v', 'new_v_rw_w2': 'new_v', 'new_v_rw_a0': 'new_v', 'new_v_rw_a2': 'new_v', 'new_v_rw_kk': 'new_v', 'new_v_rw_ka': 'new_v', 'new_v_rw_rk': 'new_v', 'new_v_rw_gn_g': 'new_v', 'new_v_rw_gn_b': 'new_v', 'new_v_w_hg_out': 'new_v', 'new_v_w_rw_out': 'new_v', 'new_v_w_out': 'new_v', 'new_v_final_g': 'new_v'}


def _forward(args):
    return _fwd_reference(*[args[k] for k in FWD_PARAMS])


def _output_shape():
    out = _jax.eval_shape(lambda: _forward(_fwd_setup_inputs(0)))
    return out.shape, out.dtype

N_MICROBATCH = 1
ADAM_LR = 0.001
ADAM_B1 = 0.9
ADAM_B2 = 0.999
ADAM_EPS = 1e-08
ADAM_WD = 0.01
ADAM_STEP = 10
PER_EXAMPLE_BATCH_AXIS = {'x': 0, 'c': 0, 'ctx': 0, 'loss_target': 0}
SHARED_INPUTS = []
_WEIGHT_DTYPES = {'c_ctx': _jnp.float32, 'ada_w': _jnp.float32, 'ada_b': _jnp.float32, 'norm_g': _jnp.float32, 'w_in': _jnp.float32, 'hg_lb': _jnp.float32, 'hg_norm_g': _jnp.float32, 'rw_mu': _jnp.float32, 'rw_w0': _jnp.float32, 'rw_w2': _jnp.float32, 'rw_a0': _jnp.float32, 'rw_a2': _jnp.float32, 'rw_kk': _jnp.float32, 'rw_ka': _jnp.float32, 'rw_rk': _jnp.float32, 'rw_gn_g': _jnp.float32, 'rw_gn_b': _jnp.float32, 'w_hg_out': _jnp.float32, 'w_rw_out': _jnp.float32, 'w_out': _jnp.float32, 'final_g': _jnp.float32}
MOMENT_SCALE = {'c_ctx': 2.405176e-03, 'ada_w': 1.643381e-02, 'ada_b': 2.704977e-02, 'norm_g': 1.748480e-02, 'w_in': 7.211879e-03, 'hg_lb': 3.682063e-04, 'hg_norm_g': 9.858958e-03, 'rw_mu': 1.941267e-02, 'rw_w0': 3.374576e-03, 'rw_w2': 7.980953e-04, 'rw_a0': 2.791590e-03, 'rw_a2': 2.199010e-03, 'rw_kk': 2.151488e-02, 'rw_ka': 2.187845e-02, 'rw_rk': 3.571756e-02, 'rw_gn_g': 1.075601e-02, 'rw_gn_b': 9.786541e-03, 'w_hg_out': 6.718472e-03, 'w_rw_out': 7.172744e-03, 'w_out': 9.819430e-03, 'final_g': 7.995587e+00}


def _to_microbatches(a, axis):
    t = _jnp.moveaxis(a, axis, 0)
    t = t.reshape((N_MICROBATCH, t.shape[0] // N_MICROBATCH) + t.shape[1:])
    return _jnp.moveaxis(t, 1, axis + 1)


def setup_inputs(seed: int = 0) -> dict:
    inp = _fwd_setup_inputs(seed)
    key = _jax.random.fold_in(_jax.random.key(seed), 7919)
    shape, _ = _output_shape()
    out = dict(inp)
    out["loss_target"] = _jax.random.normal(_jax.random.fold_in(key, 0), shape, _jnp.float32)
    for i, name in enumerate(TWIN_WEIGHTS):
        w = inp[name].astype(_jnp.float32)
        if MOMENT_SCALE is None:
            s = _jnp.sqrt(_jnp.mean(_jnp.square(w)) + 1e-30)
        else:
            s = MOMENT_SCALE[name]
        km, kv = _jax.random.split(_jax.random.fold_in(key, i + 1))
        out[name] = w
        out["m_" + name] = s * _jax.random.normal(km, w.shape, _jnp.float32)
        out["v_" + name] = (s * s) * _jax.random.uniform(kv, w.shape, _jnp.float32, 0.5, 1.5)
    if N_MICROBATCH > 1:
        for name, axis in PER_EXAMPLE_BATCH_AXIS.items():
            out[name] = _to_microbatches(out[name], axis)
    return {'x': out['x'], 'c': out['c'], 'ctx': out['ctx'], 'c_ctx': out['c_ctx'], 'ada_w': out['ada_w'], 'ada_b': out['ada_b'], 'norm_g': out['norm_g'], 'w_in': out['w_in'], 'hg_lb': out['hg_lb'], 'hg_norm_g': out['hg_norm_g'], 'rw_mu': out['rw_mu'], 'rw_w0': out['rw_w0'], 'rw_w2': out['rw_w2'], 'rw_a0': out['rw_a0'], 'rw_a2': out['rw_a2'], 'rw_kk': out['rw_kk'], 'rw_ka': out['rw_ka'], 'rw_rk': out['rw_rk'], 'rw_gn_g': out['rw_gn_g'], 'rw_gn_b': out['rw_gn_b'], 'w_hg_out': out['w_hg_out'], 'w_rw_out': out['w_rw_out'], 'w_out': out['w_out'], 'final_g': out['final_g'], 'loss_target': out['loss_target'], 'm_c_ctx': out['m_c_ctx'], 'm_ada_w': out['m_ada_w'], 'm_ada_b': out['m_ada_b'], 'm_norm_g': out['m_norm_g'], 'm_w_in': out['m_w_in'], 'm_hg_lb': out['m_hg_lb'], 'm_hg_norm_g': out['m_hg_norm_g'], 'm_rw_mu': out['m_rw_mu'], 'm_rw_w0': out['m_rw_w0'], 'm_rw_w2': out['m_rw_w2'], 'm_rw_a0': out['m_rw_a0'], 'm_rw_a2': out['m_rw_a2'], 'm_rw_kk': out['m_rw_kk'], 'm_rw_ka': out['m_rw_ka'], 'm_rw_rk': out['m_rw_rk'], 'm_rw_gn_g': out['m_rw_gn_g'], 'm_rw_gn_b': out['m_rw_gn_b'], 'm_w_hg_out': out['m_w_hg_out'], 'm_w_rw_out': out['m_w_rw_out'], 'm_w_out': out['m_w_out'], 'm_final_g': out['m_final_g'], 'v_c_ctx': out['v_c_ctx'], 'v_ada_w': out['v_ada_w'], 'v_ada_b': out['v_ada_b'], 'v_norm_g': out['v_norm_g'], 'v_w_in': out['v_w_in'], 'v_hg_lb': out['v_hg_lb'], 'v_hg_norm_g': out['v_hg_norm_g'], 'v_rw_mu': out['v_rw_mu'], 'v_rw_w0': out['v_rw_w0'], 'v_rw_w2': out['v_rw_w2'], 'v_rw_a0': out['v_rw_a0'], 'v_rw_a2': out['v_rw_a2'], 'v_rw_kk': out['v_rw_kk'], 'v_rw_ka': out['v_rw_ka'], 'v_rw_rk': out['v_rw_rk'], 'v_rw_gn_g': out['v_rw_gn_g'], 'v_rw_gn_b': out['v_rw_gn_b'], 'v_w_hg_out': out['v_w_hg_out'], 'v_w_rw_out': out['v_w_rw_out'], 'v_w_out': out['v_w_out'], 'v_final_g': out['v_final_g']}


def _loss(weights, diff, rest, loss_target):
    with _jax.named_scope("forward"):
        args = {**rest, TWIN_DIFF_INPUT: diff, **{k: w.astype(_WEIGHT_DTYPES[k]) for k, w in weights.items()}}
        y = _forward(args)
    with _jax.named_scope("loss_head"):
        err = _jnp.square(y.astype(_jnp.float32) - loss_target)
        return 0.5 * _jnp.sum(_jnp.mean(err, axis=-1)) if err.ndim else 0.5 * err


def _adamw(w, g, m, v):
    m = ADAM_B1 * m + (1.0 - ADAM_B1) * g
    v = ADAM_B2 * v + (1.0 - ADAM_B2) * _jnp.square(g)
    m_hat = m / (1.0 - ADAM_B1 ** ADAM_STEP)
    v_hat = v / (1.0 - ADAM_B2 ** ADAM_STEP)
    delta = -ADAM_LR * (m_hat / (_jnp.sqrt(v_hat) + ADAM_EPS) + ADAM_WD * w)
    return delta, m, v


def reference(x, c, ctx, c_ctx, ada_w, ada_b, norm_g, w_in, hg_lb, hg_norm_g, rw_mu, rw_w0, rw_w2, rw_a0, rw_a2, rw_kk, rw_ka, rw_rk, rw_gn_g, rw_gn_b, w_hg_out, w_rw_out, w_out, final_g, loss_target, m_c_ctx, m_ada_w, m_ada_b, m_norm_g, m_w_in, m_hg_lb, m_hg_norm_g, m_rw_mu, m_rw_w0, m_rw_w2, m_rw_a0, m_rw_a2, m_rw_kk, m_rw_ka, m_rw_rk, m_rw_gn_g, m_rw_gn_b, m_w_hg_out, m_w_rw_out, m_w_out, m_final_g, v_c_ctx, v_ada_w, v_ada_b, v_norm_g, v_w_in, v_hg_lb, v_hg_norm_g, v_rw_mu, v_rw_w0, v_rw_w2, v_rw_a0, v_rw_a2, v_rw_kk, v_rw_ka, v_rw_rk, v_rw_gn_g, v_rw_gn_b, v_w_hg_out, v_w_rw_out, v_w_out, v_final_g):
    given = dict(x=x, c=c, ctx=ctx, c_ctx=c_ctx, ada_w=ada_w, ada_b=ada_b, norm_g=norm_g, w_in=w_in, hg_lb=hg_lb, hg_norm_g=hg_norm_g, rw_mu=rw_mu, rw_w0=rw_w0, rw_w2=rw_w2, rw_a0=rw_a0, rw_a2=rw_a2, rw_kk=rw_kk, rw_ka=rw_ka, rw_rk=rw_rk, rw_gn_g=rw_gn_g, rw_gn_b=rw_gn_b, w_hg_out=w_hg_out, w_rw_out=w_rw_out, w_out=w_out, final_g=final_g, loss_target=loss_target, m_c_ctx=m_c_ctx, m_ada_w=m_ada_w, m_ada_b=m_ada_b, m_norm_g=m_norm_g, m_w_in=m_w_in, m_hg_lb=m_hg_lb, m_hg_norm_g=m_hg_norm_g, m_rw_mu=m_rw_mu, m_rw_w0=m_rw_w0, m_rw_w2=m_rw_w2, m_rw_a0=m_rw_a0, m_rw_a2=m_rw_a2, m_rw_kk=m_rw_kk, m_rw_ka=m_rw_ka, m_rw_rk=m_rw_rk, m_rw_gn_g=m_rw_gn_g, m_rw_gn_b=m_rw_gn_b, m_w_hg_out=m_w_hg_out, m_w_rw_out=m_w_rw_out, m_w_out=m_w_out, m_final_g=m_final_g, v_c_ctx=v_c_ctx, v_ada_w=v_ada_w, v_ada_b=v_ada_b, v_norm_g=v_norm_g, v_w_in=v_w_in, v_hg_lb=v_hg_lb, v_hg_norm_g=v_hg_norm_g, v_rw_mu=v_rw_mu, v_rw_w0=v_rw_w0, v_rw_w2=v_rw_w2, v_rw_a0=v_rw_a0, v_rw_a2=v_rw_a2, v_rw_kk=v_rw_kk, v_rw_ka=v_rw_ka, v_rw_rk=v_rw_rk, v_rw_gn_g=v_rw_gn_g, v_rw_gn_b=v_rw_gn_b, v_w_hg_out=v_w_hg_out, v_w_rw_out=v_w_rw_out, v_w_out=v_w_out, v_final_g=v_final_g)
    weights = {n: given[n] for n in TWIN_WEIGHTS}
    shared = {n: given[n] for n in SHARED_INPUTS}
    per_example = {n: given[n] for n in ['x', 'c', 'ctx']}
    grad_fn = _jax.value_and_grad(_loss, argnums=(0, 1))

    def one_microbatch(ex, loss_target):
        ex = dict(ex)
        diff = ex.pop(TWIN_DIFF_INPUT)
        return grad_fn(weights, diff, {**shared, **ex}, loss_target)

    if N_MICROBATCH == 1:
        loss, (grad_w, grad_x) = one_microbatch(per_example, given["loss_target"])
    else:
        def body(carry, xs):
            loss_sum, grad_sum = carry
            l_k, (gw_k, gx_k) = one_microbatch(xs[0], xs[1])
            with _jax.named_scope("update"):
                return (loss_sum + l_k, _jax.tree.map(_jnp.add, grad_sum, gw_k)), gx_k

        init = (_jnp.zeros((), _jnp.float32), _jax.tree.map(_jnp.zeros_like, weights))
        (loss, grad_w), grad_x = _jax.lax.scan(body, init, (per_example, given["loss_target"]))
    with _jax.named_scope("update"):
        delta_w, new_m, new_v = {}, {}, {}
        for n in TWIN_WEIGHTS:
            delta_w[n], new_m[n], new_v[n] = _adamw(weights[n], grad_w[n], given["m_" + n], given["v_" + n])
    return (loss, grad_x, *[grad_w[n] for n in TWIN_WEIGHTS], *[delta_w[n] for n in TWIN_WEIGHTS],
            *[new_m[n] for n in TWIN_WEIGHTS], *[new_v[n] for n in TWIN_WEIGHTS])
```

```python
import functools

import jax
import jax.numpy as jnp
from jax import lax
from jax.experimental import pallas as pl
from jax.experimental.pallas import tpu as pltpu

HI = lax.Precision.HIGHEST
F32 = jnp.float32
BF16 = jnp.bfloat16

NORM_EPS = 1e-6
HG_HEAD = 128
RW_HEAD = 64
RW_LORA = 64
RW_GN_EPS = 64e-5
GRID_W = 64
SUB = 16
STEP = 64
N_SHARD = 4
N_DEV = 8
LANE = 128

ADAM_LR = 0.001
ADAM_B1 = 0.9
ADAM_B2 = 0.999
ADAM_EPS = 1e-08
ADAM_WD = 0.01
ADAM_STEP = 10

VMEM_LIMIT = 56 * 1024 * 1024


def _params(sem=None):
    return pltpu.CompilerParams(dimension_semantics=sem, vmem_limit_bytes=VMEM_LIMIT)


def _tile(n, cands):
    for c in cands:
        if n % c == 0:
            return c
    return n


def _iota2(n, m, d):
    return lax.broadcasted_iota(jnp.int32, (n, m), d)


def _before(n, rev, strict):
    t, s = _iota2(n, n, 0), _iota2(n, n, 1)
    if rev:
        return (s > t) if strict else (s >= t)
    return (s < t) if strict else (s <= t)


def _bdot(a, b, spec):
    return jnp.einsum(spec, a, b, precision=HI, preferred_element_type=F32)


def _hg_step(s0, qraw, iin, fin, lb2, rev):
    c, w = qraw.shape
    h = w // HG_HEAD
    nsub = c // SUB
    lb = jax.nn.sigmoid(lb2[0:1] - lb2[1:2])
    q = jax.nn.silu(qraw)
    fg = lb + (1.0 - lb) * jax.nn.sigmoid(fin)
    kk = 1.0 - fg
    g = jnp.log(fg)
    incl = _before(SUB, rev, False).astype(F32)
    rows = lax.broadcasted_iota(jnp.int32, (SUB, 1), 0)
    last = 0 if rev else SUB - 1

    def heads(a):
        return jnp.swapaxes(a.reshape(a.shape[0], h, HG_HEAD), 0, 1)

    s = s0
    outs = [None] * nsub
    order = range(nsub - 1, -1, -1) if rev else range(nsub)
    for j in order:
        sl = slice(j * SUB, (j + 1) * SUB)
        qs, ks, vs, gs = q[sl], kk[sl], iin[sl], g[sl]
        bc = jnp.dot(incl, gs, precision=HI, preferred_element_type=F32)
        o = jnp.zeros((SUB, h, HG_HEAD), F32)
        for si in range(SUB):
            dec = jnp.exp(jnp.minimum(bc - bc[si:si + 1], 0.0))
            a = (qs * ks[si:si + 1] * dec).reshape(SUB, h, HG_HEAD).sum(-1)
            valid = (rows <= si) if rev else (rows >= si)
            a = jnp.where(valid, a, 0.0)
            o = o + a[:, :, None] * vs[si:si + 1].reshape(1, h, HG_HEAD)
        o = o.reshape(SUB, w)
        o = o + jnp.swapaxes(_bdot(heads(qs * jnp.exp(bc)), s, 'htk,hvk->htv'), 0, 1).reshape(SUB, w)
        blast = bc[last:last + 1]
        kdec = ks * jnp.exp(blast - bc)
        s = heads(jnp.exp(blast)) * s + _bdot(heads(vs), heads(kdec), 'hsv,hsk->hvk')
        outs[j] = o
    return jnp.concatenate(outs, axis=0), s


def _tri_solve(lmat, rhs, rev):
    hh, c, _ = lmat.shape
    nb = c // SUB
    eye = (_iota2(SUB, SUB, 0) == _iota2(SUB, SUB, 1)).astype(F32)
    rowid = lax.broadcasted_iota(jnp.int32, (1, SUB, 1), 1)
    diag = jnp.concatenate([lmat[:, i * SUB:(i + 1) * SUB, i * SUB:(i + 1) * SUB] for i in range(nb)], axis=0)
    tinv = jnp.broadcast_to(eye[None], diag.shape)
    order = range(SUB - 2, -1, -1) if rev else range(1, SUB)
    for t in order:
        row = eye[t:t + 1][None] - (diag[:, t, :][:, :, None] * tinv).sum(axis=1, keepdims=True)
        tinv = jnp.where(rowid == t, row, tinv)
    p = [None] * nb
    done = []
    for i in (range(nb - 1, -1, -1) if rev else range(nb)):
        r = rhs[:, i * SUB:(i + 1) * SUB]
        for m in done:
            r = r - _bdot(lmat[:, i * SUB:(i + 1) * SUB, m * SUB:(m + 1) * SUB], p[m], 'hts,hsv->htv')
        p[i] = _bdot(tinv[i * hh:(i + 1) * hh], r, 'hts,hsv->htv')
        done.append(i)
    return jnp.concatenate(p, axis=1)


def _rw_step(s0, r, k, v, wlo, alo, w0h, w2h, a0h, a2h, kkh, kah, rev):
    hh, c, _ = r.shape
    tl = jnp.broadcast_to(jnp.tanh(wlo)[None], (hh, c, wlo.shape[1]))
    al = jnp.broadcast_to(alo[None], (hh, c, alo.shape[1]))
    wlog = -jax.nn.softplus(-(w0h + _bdot(tl, w2h, 'hcl,hlj->hcj'))) - 0.5
    lw = -jnp.exp(wlog)
    a = jax.nn.sigmoid(a0h + _bdot(al, a2h, 'hcl,hlj->hcj'))
    kk = k * kkh
    kk = kk * lax.rsqrt(jnp.sum(kk * kk, axis=-1, keepdims=True) + 1e-12)
    kd = k * (1.0 + (a - 1.0) * kah)
    b = kk * a
    incl = jnp.broadcast_to(_before(c, rev, False).astype(F32)[None], (hh, c, c))
    cum = _bdot(incl, lw, 'hts,hsk->htk')
    ecum, encum = jnp.exp(cum), jnp.exp(-cum)
    alpha = jnp.exp(cum - lw) * kk
    beta = b * encum
    kappa = kd * encum
    rho = r * ecum
    m_lt = _before(c, rev, True)[None]
    m_le = _before(c, rev, False)[None]
    a_kap = jnp.where(m_lt, _bdot(alpha, kappa, 'htk,hsk->hts'), 0.0)
    a_bet = jnp.where(m_lt, _bdot(alpha, beta, 'htk,hsk->hts'), 0.0)
    b_kap = jnp.where(m_le, _bdot(rho, kappa, 'htk,hsk->hts'), 0.0)
    b_bet = jnp.where(m_le, _bdot(rho, beta, 'htk,hsk->hts'), 0.0)
    rhs = _bdot(alpha, s0, 'htk,hvk->htv') + _bdot(a_kap, v, 'hts,hsv->htv')
    p = _tri_solve(a_bet, rhs, rev)
    y = _bdot(rho, s0, 'htk,hvk->htv') + _bdot(b_kap, v, 'hts,hsv->htv') - _bdot(b_bet, p, 'hts,hsv->htv')
    stil = s0 + _bdot(v, kappa, 'hsv,hsk->hvk') - _bdot(p, beta, 'hsv,hsk->hvk')
    last = 0 if rev else c - 1
    return y, stil * ecum[:, last:last + 1, :]


def _fn_h(s, norm_g, scale, shift):
    return s * lax.rsqrt(jnp.mean(s * s, axis=-1, keepdims=True) + NORM_EPS) * norm_g * (1.0 + scale) + shift


def _fn_hgpost(of, ob, z, g):
    tm, w = of.shape
    o = (of + ob).reshape(tm, w // HG_HEAD, HG_HEAD)
    o = o * lax.rsqrt(jnp.mean(o * o, axis=-1, keepdims=True) + NORM_EPS)
    return o.reshape(tm, w) * g * jax.nn.silu(z)


def _fn_rwpost(y0, y1, r, k, v, alo, z, a0, a2, k_a, r_k, gn_g, gn_b):
    tm, w = r.shape
    nh = w // RW_HEAD
    asum = 0.0
    for d in range(2):
        asum = asum + jax.nn.sigmoid(a0[d:d + 1] + jnp.dot(alo[:, d * RW_LORA:(d + 1) * RW_LORA], a2[d],
                                                           precision=HI, preferred_element_type=F32))
    k_sum = k * (2.0 + (asum - 2.0) * k_a)
    ys = (y0 + y1).reshape(tm, nh, RW_HEAD)
    mean = jnp.mean(ys, axis=-1, keepdims=True)
    var = jnp.mean(jnp.square(ys - mean), axis=-1, keepdims=True)
    y = ((ys - mean) * lax.rsqrt(var + RW_GN_EPS)).reshape(tm, w) * gn_g + gn_b
    bonus = jnp.sum((r * k_sum * r_k).reshape(tm, nh, RW_HEAD), axis=-1, keepdims=True) * v.reshape(tm, nh, RW_HEAD)
    return (y + bonus.reshape(tm, w)) * jax.nn.silu(z)


def _fn_merge(a, b, ghg, grw):
    return jax.nn.sigmoid(ghg) * a + jax.nn.sigmoid(grw) * b


def _fn_final(xs, o, gate, final_g, tgt):
    x2 = xs + gate * o
    y = x2 * lax.rsqrt(jnp.mean(x2 * x2, axis=-1, keepdims=True) + NORM_EPS) * final_g
    return 0.5 * jnp.sum(jnp.mean(jnp.square(y - tgt), axis=-1))


def _row_call(name, fn, n_tiles, tm, row_ins, full_ins, row_outs, acc_outs):
    n_ri, n_fi, n_ro = len(row_ins), len(full_ins), len(row_outs)

    def body(*refs):
        i = pl.program_id(0)
        rvals = [r[...] for r in refs[:n_ri]]
        fvals = [r[...] for r in refs[n_ri:n_ri + n_fi]]
        outs = refs[n_ri + n_fi:]
        ro, ao = fn(i, rvals, fvals)
        for ref, val in zip(outs[:n_ro], ro):
            ref[...] = val.astype(ref.dtype)
        for ref, val in zip(outs[n_ro:], ao):
            @pl.when(i == 0)
            def _(ref=ref):
                ref[...] = jnp.zeros_like(ref)
            ref[...] += val.astype(ref.dtype)

    def rspec(width, cb, off):
        return pl.BlockSpec((tm, width), lambda i: (jnp.maximum(i - off, 0), cb))

    def fspec(shape):
        nd = len(shape)
        return pl.BlockSpec(shape, lambda i: (0,) * nd)

    in_specs = [rspec(w, cb, off) for (_, cb, w, off) in row_ins] + [fspec(a.shape) for a in full_ins]
    out_specs = [rspec(w, 0, off) for (_, w, _, off) in row_outs] + [fspec(s) for (s, _) in acc_outs]
    out_shape = [jax.ShapeDtypeStruct((rows, w), dt) for (rows, w, dt, _) in row_outs] + \
                [jax.ShapeDtypeStruct(s, dt) for (s, dt) in acc_outs]
    res = pl.pallas_call(
        body, name=name, grid=(n_tiles,), in_specs=in_specs, out_specs=out_specs, out_shape=out_shape,
        compiler_params=_params(("arbitrary",)),
    )(*[a for (a, _, _, _) in row_ins], *full_ins)
    return list(res)


def _mm(name, a, b, m, n, k_steps, tm, tn, a_block, a_map, b_block, b_map, o_shape, o_block, o_map,
        contract, out_dtype=F32):
    def body(a_ref, b_ref, o_ref, acc_ref):
        kk = pl.program_id(2)

        @pl.when(kk == 0)
        def _():
            acc_ref[...] = jnp.zeros_like(acc_ref)

        acc_ref[...] += lax.dot_general(a_ref[...].astype(BF16), b_ref[...].astype(BF16),
                                        (contract, ((), ())), preferred_element_type=F32)

        @pl.when(kk == k_steps - 1)
        def _():
            o_ref[...] = acc_ref[...].astype(o_ref.dtype)

    return pl.pallas_call(
        body, name=name, grid=(m // tm, n // tn, k_steps),
        in_specs=[pl.BlockSpec(a_block, a_map), pl.BlockSpec(b_block, b_map)],
        out_specs=pl.BlockSpec(o_block, o_map),
        out_shape=jax.ShapeDtypeStruct(o_shape, out_dtype),
        scratch_shapes=[pltpu.VMEM((tm, tn), F32)],
        compiler_params=_params(("parallel", "parallel", "arbitrary")),
    )(a, b)


_TM = (512, 256, 128, 64, 32, 16, 8)
_TN = (512, 256, 128)
_TK = (512, 256, 128)


def _mm_nn(name, a, b, out_dtype=F32):
    m, k = a.shape
    n = b.shape[1]
    tm, tn, tk = _tile(m, _TM), _tile(n, _TN), _tile(k, _TK)
    return _mm(name, a, b, m, n, k // tk, tm, tn, (tm, tk), lambda i, j, s: (i, s), (tk, tn), lambda i, j, s: (s, j),
               (m, n), (tm, tn), lambda i, j, s: (i, j), ((1,), (0,)), out_dtype)


def _mm_nt(name, a, b, out_dtype=F32):
    m, k = a.shape
    n = b.shape[0]
    tm, tn, tk = _tile(m, _TM), _tile(n, _TN), _tile(k, _TK)
    return _mm(name, a, b, m, n, k // tk, tm, tn, (tm, tk), lambda i, j, s: (i, s), (tn, tk), lambda i, j, s: (j, s),
               (m, n), (tm, tn), lambda i, j, s: (i, j), ((1,), (1,)), out_dtype)


def _mm_tn(name, a, b, out_dtype=F32):
    k, m = a.shape
    n = b.shape[1]
    tm, tn, tk = _tile(m, _TM), _tile(n, _TN), _tile(k, _TK)
    return _mm(name, a, b, m, n, k // tk, tm, tn, (tk, tm), lambda i, j, s: (s, i), (tk, tn), lambda i, j, s: (s, j),
               (m, n), (tm, tn), lambda i, j, s: (i, j), ((0,), (0,)), out_dtype)


def _mm_n_st(name, a, bst, out_dtype=F32):
    m, k = a.shape
    ns_, _, ns = bst.shape
    tm, tk = _tile(m, (256, 128, 64, 32, 16, 8)), _tile(k, (256, 128))
    return _mm(name, a, bst, m, ns_ * ns, k // tk, tm, ns,
               (tm, tk), lambda i, j, s: (i, s), (None, tk, ns), lambda i, j, s: (j, s, 0),
               (ns_, m, ns), (None, tm, ns), lambda i, j, s: (j, i, 0), ((1,), (0,)), out_dtype)


def _mm_st_t(name, ast, bst, out_dtype=F32):
    ns_, m, ns = ast.shape
    n = bst.shape[1]
    tm, tn = _tile(m, (256, 128, 64, 32, 16, 8)), _tile(n, (256, 128))
    return _mm(name, ast, bst, m, n, ns_, tm, tn,
               (None, tm, ns), lambda i, j, s: (s, i, 0), (None, tn, ns), lambda i, j, s: (s, j, 0),
               (m, n), (tm, tn), lambda i, j, s: (i, j), ((1,), (1,)), out_dtype)


def _mm_t_st(name, a, bst, out_dtype=F32):
    k, m = a.shape
    ns_, _, ns = bst.shape
    tm, tk = _tile(m, (256, 128, 64, 32, 16, 8)), _tile(k, (256, 128))
    return _mm(name, a, bst, m, ns_ * ns, k // tk, tm, ns,
               (tk, tm), lambda i, j, s: (s, i), (None, tk, ns), lambda i, j, s: (j, s, 0),
               (ns_, m, ns), (None, tm, ns), lambda i, j, s: (j, i, 0), ((0,), (0,)), out_dtype)


def _scan_order(j, n_ctx, n_all, rev):
    if not rev:
        return j
    return jnp.where(j < n_ctx, n_ctx - 1 - j, n_all - 1 - (j - n_ctx))


def _hg_scan_fwd(name, p_hg, lb2, d, n_ctx):
    t, w5 = p_hg.shape
    w = w5 // 5
    h = w // HG_HEAD
    n = t // STEP
    rev = d == 1

    def body(q_ref, i_ref, f_ref, lb_ref, o_ref, st_ref, s_ref):
        j = pl.program_id(0)

        @pl.when(j == 0)
        def _():
            s_ref[...] = jnp.zeros_like(s_ref)

        s0 = s_ref[...]
        st_ref[...] = s0
        o, s1 = _hg_step(s0, q_ref[...], i_ref[...], f_ref[...], lb_ref[...], rev)
        o_ref[...] = o
        s_ref[...] = s1

    def rows(cb):
        return pl.BlockSpec((STEP, w), lambda j: (_scan_order(j, n_ctx, n, rev), cb))

    return pl.pallas_call(
        body, name=name, grid=(n,),
        in_specs=[rows(0), rows(1), rows(2 + d), pl.BlockSpec((2, w), lambda j: (0, 0))],
        out_specs=[rows(0), pl.BlockSpec((None, h, HG_HEAD, HG_HEAD), lambda j: (j, 0, 0, 0))],
        out_shape=[jax.ShapeDtypeStruct((t, w), F32), jax.ShapeDtypeStruct((n, h, HG_HEAD, HG_HEAD), F32)],
        scratch_shapes=[pltpu.VMEM((h, HG_HEAD, HG_HEAD), F32)],
        compiler_params=_params(("arbitrary",)),
    )(p_hg, p_hg, p_hg, lb2)


def _hg_scan_bwd(name, p_hg, lb2, states, do, d, n_ctx):
    t, w5 = p_hg.shape
    w = w5 // 5
    h = w // HG_HEAD
    n = t // STEP
    rev = d == 1

    def body(q_ref, i_ref, f_ref, lb_ref, st_ref, do_ref, dq_ref, di_ref, df_ref, dlb_ref, ds_ref):
        step = pl.program_id(0)

        @pl.when(step == 0)
        def _():
            ds_ref[...] = jnp.zeros_like(ds_ref)
            dlb_ref[...] = jnp.zeros_like(dlb_ref)

        _, vjp = jax.vjp(lambda s0, q, i, f, lb: _hg_step(s0, q, i, f, lb, rev),
                         st_ref[...], q_ref[...], i_ref[...], f_ref[...], lb_ref[...])
        ds0, dq, di, df, dlb = vjp((do_ref[...], ds_ref[...]))
        dq_ref[...] = dq
        di_ref[...] = di
        df_ref[...] = df
        dlb_ref[...] += dlb
        ds_ref[...] = ds0

    def rows(cb):
        return pl.BlockSpec((STEP, w), lambda s: (_scan_order(n - 1 - s, n_ctx, n, rev), cb))

    return pl.pallas_call(
        body, name=name, grid=(n,),
        in_specs=[rows(0), rows(1), rows(2 + d), pl.BlockSpec((2, w), lambda s: (0, 0)),
                  pl.BlockSpec((None, h, HG_HEAD, HG_HEAD), lambda s: (n - 1 - s, 0, 0, 0)), rows(0)],
        out_specs=[rows(0), rows(0), rows(0), pl.BlockSpec((2, w), lambda s: (0, 0))],
        out_shape=[jax.ShapeDtypeStruct((t, w), F32)] * 3 + [jax.ShapeDtypeStruct((2, w), F32)],
        scratch_shapes=[pltpu.VMEM((h, HG_HEAD, HG_HEAD), F32)],
        compiler_params=_params(("arbitrary",)),
    )(p_hg, p_hg, p_hg, lb2, states, do)


def _to_heads(a, nh):
    return jnp.stack([a[:, i * RW_HEAD:(i + 1) * RW_HEAD] for i in range(nh)], axis=0)


def _from_heads(a):
    return jnp.concatenate([a[i] for i in range(a.shape[0])], axis=-1)


def _rw_scan_fwd(name, sh, hp, d, n_ctx):
    t = sh.shape[0]
    w = (sh.shape[1] - 4 * RW_LORA) // 3
    nh = w // RW_HEAD
    n = t // STEP
    rev = d == 1
    lo = 3 * w // LANE

    def body(r_ref, k_ref, v_ref, wl_ref, al_ref, w0_ref, w2_ref, a0_ref, a2_ref, kk_ref, ka_ref,
             y_ref, st_ref, s_ref):
        j = pl.program_id(0)

        @pl.when(j == 0)
        def _():
            s_ref[...] = jnp.zeros_like(s_ref)

        s0 = s_ref[...]
        st_ref[...] = s0
        wl = wl_ref[...][:, d * RW_LORA:(d + 1) * RW_LORA]
        al = al_ref[...][:, d * RW_LORA:(d + 1) * RW_LORA]
        y, s1 = _rw_step(s0, _to_heads(r_ref[...], nh), _to_heads(k_ref[...], nh), _to_heads(v_ref[...], nh), wl, al,
                         w0_ref[...], w2_ref[...], a0_ref[...], a2_ref[...], kk_ref[...], ka_ref[...], rev)
        y_ref[...] = _from_heads(y)
        s_ref[...] = s1

    def rows(cb, width=w):
        return pl.BlockSpec((STEP, width), lambda j: (_scan_order(j, n_ctx, n, rev), cb))

    def whole(a):
        nd = a.ndim
        return pl.BlockSpec(a.shape, lambda j: (0,) * nd)

    return pl.pallas_call(
        body, name=name, grid=(n,),
        in_specs=[rows(0), rows(1), rows(2), rows(lo, LANE), rows(lo + 1, LANE)] + [whole(a) for a in hp],
        out_specs=[rows(0), pl.BlockSpec((None, nh, RW_HEAD, RW_HEAD), lambda j: (j, 0, 0, 0))],
        out_shape=[jax.ShapeDtypeStruct((t, w), F32), jax.ShapeDtypeStruct((n, nh, RW_HEAD, RW_HEAD), F32)],
        scratch_shapes=[pltpu.VMEM((nh, RW_HEAD, RW_HEAD), F32)],
        compiler_params=_params(("arbitrary",)),
    )(sh, sh, sh, sh, sh, *hp)


def _rw_scan_bwd(name, sh, hp, states, dy, d, n_ctx):
    t = sh.shape[0]
    w = (sh.shape[1] - 4 * RW_LORA) // 3
    nh = w // RW_HEAD
    n = t // STEP
    rev = d == 1
    lo = 3 * w // LANE

    def body(r_ref, k_ref, v_ref, wl_ref, al_ref, w0_ref, w2_ref, a0_ref, a2_ref, kk_ref, ka_ref, st_ref, dy_ref,
             dm_ref, dl_ref, dw0_ref, dw2_ref, da0_ref, da2_ref, dkk_ref, dka_ref, ds_ref):
        step = pl.program_id(0)
        pouts = (dw0_ref, dw2_ref, da0_ref, da2_ref, dkk_ref, dka_ref)

        @pl.when(step == 0)
        def _():
            ds_ref[...] = jnp.zeros_like(ds_ref)
            for ref in pouts:
                ref[...] = jnp.zeros_like(ref)

        wl = wl_ref[...][:, d * RW_LORA:(d + 1) * RW_LORA]
        al = al_ref[...][:, d * RW_LORA:(d + 1) * RW_LORA]
        _, vjp = jax.vjp(functools.partial(_rw_step, rev=rev),
                         st_ref[...], _to_heads(r_ref[...], nh), _to_heads(k_ref[...], nh), _to_heads(v_ref[...], nh),
                         wl, al, w0_ref[...], w2_ref[...], a0_ref[...], a2_ref[...], kk_ref[...], ka_ref[...])
        g = vjp((_to_heads(dy_ref[...], nh), ds_ref[...]))
        ds_ref[...] = g[0]
        dm_ref[...] = jnp.concatenate([_from_heads(g[1]), _from_heads(g[2]), _from_heads(g[3])], axis=-1)
        zero = jnp.zeros_like(g[4])
        parts = [zero] * 4
        parts[d], parts[2 + d] = g[4], g[5]
        dl_ref[...] = jnp.concatenate(parts, axis=-1)
        for ref, val in zip(pouts, g[6:]):
            ref[...] += val

    def rows(cb, width=w):
        return pl.BlockSpec((STEP, width), lambda s: (_scan_order(n - 1 - s, n_ctx, n, rev), cb))

    def whole(a):
        nd = a.ndim
        return pl.BlockSpec(a.shape, lambda s: (0,) * nd)

    return pl.pallas_call(
        body, name=name, grid=(n,),
        in_specs=[rows(0), rows(1), rows(2), rows(lo, LANE), rows(lo + 1, LANE)] + [whole(a) for a in hp] +
                 [pl.BlockSpec((None, nh, RW_HEAD, RW_HEAD), lambda s: (n - 1 - s, 0, 0, 0)), rows(0)],
        out_specs=[rows(0, 3 * w), rows(0, 4 * RW_LORA)] + [whole(a) for a in hp],
        out_shape=[jax.ShapeDtypeStruct((t, 3 * w), F32), jax.ShapeDtypeStruct((t, 4 * RW_LORA), F32)] +
                  [jax.ShapeDtypeStruct(a.shape, F32) for a in hp],
        scratch_shapes=[pltpu.VMEM((nh, RW_HEAD, RW_HEAD), F32)],
        compiler_params=_params(("arbitrary",)),
    )(sh, sh, sh, sh, sh, *hp, states, dy)


def _shift_masks(t, n_ctx_rows):
    row = lax.broadcasted_iota(jnp.int32, (t, 1), 0)
    isx = row >= n_ctx_rows
    pos = jnp.where(isx, row - n_ctx_rows, row)
    col = jnp.where(isx, jnp.bitwise_and(pos, GRID_W - 1), pos)
    ncol = jnp.where(isx, GRID_W, n_ctx_rows)
    n_x = t - n_ctx_rows
    ml = col != 0
    mr = col != ncol - 1
    mu = isx & (pos >= GRID_W)
    md = isx & (pos < n_x - GRID_W)
    return ml, mr, mu, md, isx


def _shift_fwd(name, p, mu, n_ctx_rows):
    t, c = p.shape
    cw = LANE

    def body(p_ref, mu_ref, o_ref):
        x = p_ref[...]
        m = mu_ref[...]
        ml, mr, mup, mdn, isx = _shift_masks(t, n_ctx_rows)
        left = jnp.where(ml, pltpu.roll(x, 1, 0), 0.0)
        right = jnp.where(mr, pltpu.roll(x, t - 1, 0), 0.0)
        up = jnp.where(mup, pltpu.roll(x, GRID_W, 0), 0.0)
        down = jnp.where(mdn, pltpu.roll(x, t - GRID_W, 0), 0.0)
        out = x + m[0:1] * (left - x) + m[1:2] * (right - x)
        vert = m[2:3] * (up - x) + m[3:4] * (down - x)
        o_ref[...] = out + jnp.where(isx, vert, 0.0)

    return pl.pallas_call(
        body, name=name, grid=(c // cw,),
        in_specs=[pl.BlockSpec((t, cw), lambda j: (0, j)), pl.BlockSpec((4, cw), lambda j: (0, j))],
        out_specs=pl.BlockSpec((t, cw), lambda j: (0, j)),
        out_shape=jax.ShapeDtypeStruct((t, c), F32),
        compiler_params=_params(("parallel",)),
    )(p, mu)


def _shift_bwd(name, p, mu, dparts, n_ctx_rows):
    t, c = p.shape
    cw = LANE
    npart = len(dparts)

    def body(*refs):
        p_ref, mu_ref = refs[0], refs[1]
        dp_ref, dmu_ref = refs[2 + npart], refs[3 + npart]
        x = p_ref[...]
        m = mu_ref[...]
        g = refs[2][...]
        for r in refs[3:2 + npart]:
            g = g + r[...]
        ml, mr, mup, mdn, isx = _shift_masks(t, n_ctx_rows)
        left = jnp.where(ml, pltpu.roll(x, 1, 0), 0.0)
        right = jnp.where(mr, pltpu.roll(x, t - 1, 0), 0.0)
        up = jnp.where(mup, pltpu.roll(x, GRID_W, 0), 0.0)
        down = jnp.where(mdn, pltpu.roll(x, t - GRID_W, 0), 0.0)
        gx = jnp.where(isx, g, 0.0)
        dmu_ref[...] = jnp.concatenate([
            jnp.sum(g * (left - x), axis=0, keepdims=True), jnp.sum(g * (right - x), axis=0, keepdims=True),
            jnp.sum(gx * (up - x), axis=0, keepdims=True), jnp.sum(gx * (down - x), axis=0, keepdims=True)], axis=0)
        coef = 1.0 - m[0:1] - m[1:2] - jnp.where(isx, m[2:3] + m[3:4], 0.0)
        dp = coef * g
        dp = dp + m[0:1] * pltpu.roll(jnp.where(ml, g, 0.0), t - 1, 0)
        dp = dp + m[1:2] * pltpu.roll(jnp.where(mr, g, 0.0), 1, 0)
        dp = dp + m[2:3] * pltpu.roll(jnp.where(mup, g, 0.0), t - GRID_W, 0)
        dp = dp + m[3:4] * pltpu.roll(jnp.where(mdn, g, 0.0), GRID_W, 0)
        dp_ref[...] = dp

    col = pl.BlockSpec((t, cw), lambda j: (0, j))
    par = pl.BlockSpec((4, cw), lambda j: (0, j))
    return pl.pallas_call(
        body, name=name, grid=(c // cw,),
        in_specs=[col, par] + [col] * npart,
        out_specs=[col, par],
        out_shape=[jax.ShapeDtypeStruct((t, c), F32), jax.ShapeDtypeStruct((4, c), F32)],
        compiler_params=_params(("parallel",)),
    )(p, mu, *dparts)


def _local_step(x, c, ctx, c_ctx, ada_st, ada_b, norm_g, w_in_st, hg_lb, hg_norm_g, rw_mu, rw_w0, rw_w2, rw_a0, rw_a2,
                rw_kk, rw_ka, rw_rk, rw_gn_g, rw_gn_b, w_hg_st, w_rw_st, w_out, final_g, tgt):
    seq, dm = x.shape
    n_ctx_rows = ctx.shape[0]
    t = seq + n_ctx_rows
    hw = hg_norm_g.shape[-1]
    rw = rw_kk.shape[-1]
    nh_rw = rw // RW_HEAD
    n_ctx = n_ctx_rows // STEP
    tm = _tile(n_ctx_rows, (256, 128, 64))
    nt = t // tm
    nct = n_ctx_rows // tm
    n_sh_cols = 3 * rw + 4 * RW_LORA

    xs = jnp.concatenate([ctx, x], axis=0)
    cond = jnp.concatenate([c.reshape(1, dm), c_ctx.reshape(1, dm), jnp.zeros((6, dm), F32)], axis=0)
    final_g2 = final_g.reshape(1, dm)

    def unstack(a_st):
        return jnp.swapaxes(a_st, 0, 1).reshape(a_st.shape[1], -1)

    def restack(a, ns=N_SHARD):
        return jnp.swapaxes(a.reshape(a.shape[0], ns, -1), 0, 1)

    (sc,) = _row_call("cond_silu", lambda i, r, f: ([jax.nn.silu(r[0])], []), 1, 8, [(cond, 0, dm, 0)], [],
                      [(8, dm, F32, 0)], [])
    mod_st = _mm_n_st("mod_mm", sc, ada_st)
    mod = unstack(mod_st) + ada_b
    mod3 = mod.reshape(8, 3, dm)

    def pick(i, m3):
        r = jnp.where(i < nct, m3[1], m3[0])
        return r[0:1], r[1:2]

    def h_fn(i, r, f):
        shift, scale = pick(i, f[1])
        return [_fn_h(r[0], f[0], scale, shift)], []

    (h,) = _row_call("h_fwd", h_fn, nt, tm, [(xs, 0, dm, 0)], [norm_g, mod3], [(t, dm, BF16, 0)], [])
    proj = unstack(_mm_n_st("proj_mm", h, w_in_st))
    p_hg = proj[:, :5 * hw]
    p_rs = proj[:, 5 * hw:5 * hw + n_sh_cols]
    p_zr = proj[:, 5 * hw + n_sh_cols:5 * hw + n_sh_cols + rw]
    p_gt = proj[:, 5 * hw + n_sh_cols + rw:]

    o_hg, st_hg = [], []
    for d in range(2):
        o, st = _hg_scan_fwd(f"hg_scan_fwd{d}", p_hg, hg_lb[d], d, n_ctx)
        o_hg.append(o)
        st_hg.append(st)

    def hgpost_fn(i, r, f):
        return [_fn_hgpost(r[0], r[1], r[2], f[0])], []

    hg_in = [(o_hg[0], 0, hw, 0), (o_hg[1], 0, hw, 0), (p_hg, 4, hw, 0)]
    (y_hg,) = _row_call("hg_post", hgpost_fn, nt, tm, hg_in, [hg_norm_g], [(t, hw, BF16, 0)], [])

    sh = _shift_fwd("rw_shift", p_rs, rw_mu, n_ctx_rows)
    hps = []
    for d in range(2):
        hps.append([rw_w0[d].reshape(nh_rw, 1, RW_HEAD), jnp.swapaxes(rw_w2[d].reshape(RW_LORA, nh_rw, RW_HEAD), 0, 1),
                    rw_a0[d].reshape(nh_rw, 1, RW_HEAD), jnp.swapaxes(rw_a2[d].reshape(RW_LORA, nh_rw, RW_HEAD), 0, 1),
                    rw_kk.reshape(nh_rw, 1, RW_HEAD), rw_ka.reshape(nh_rw, 1, RW_HEAD)])
    y_rw_d, st_rw = [], []
    for d in range(2):
        y, st = _rw_scan_fwd(f"rw_scan_fwd{d}", sh, hps[d], d, n_ctx)
        y_rw_d.append(y)
        st_rw.append(st)

    rw_full = [rw_a0, rw_a2, rw_ka, rw_rk, rw_gn_g, rw_gn_b]
    lo = 3 * rw // LANE
    rw_in = [(y_rw_d[0], 0, rw, 0), (y_rw_d[1], 0, rw, 0), (sh, 0, rw, 0), (sh, 1, rw, 0), (sh, 2, rw, 0),
             (sh, lo + 1, LANE, 0), (p_zr, 0, rw, 0)]

    def rwpost_fn(i, r, f):
        return [_fn_rwpost(*r, *f)], []

    (y_rw,) = _row_call("rw_post", rwpost_fn, nt, tm, rw_in, rw_full, [(t, rw, BF16, 0)], [])

    a_hg = unstack(_mm_n_st("hg_out_mm", y_hg, w_hg_st))
    a_rw = unstack(_mm_n_st("rw_out_mm", y_rw, w_rw_st))
    mg_in = [(a_hg, 0, dm, 0), (a_rw, 0, dm, 0), (p_gt, 0, dm, 0), (p_gt, 1, dm, 0)]
    (merged,) = _row_call("merge", lambda i, r, f: ([_fn_merge(*r)], []), nt, tm, mg_in, [], [(t, dm, BF16, 0)], [])
    o_out = _mm_nn("out_mm", merged, w_out)

    def final_fn(i, r, f):
        gate = f[0][0][2:3]
        loss, vjp = jax.vjp(_fn_final, r[0], r[1], gate, f[1], r[2])
        dx, do, dgate, dfg, _ = vjp(jnp.ones((), F32))
        live = i >= nct
        zero = lambda a: jnp.where(live, a, 0.0)
        dmod = jnp.concatenate([jnp.concatenate([jnp.zeros((1, 2 * dm), F32), zero(dgate)], axis=1),
                                jnp.zeros((7, 3 * dm), F32)], axis=0)
        return [zero(dx), zero(do)], [jnp.broadcast_to(zero(loss), (8, LANE)), dmod, zero(dfg)]

    fin_in = [(xs, 0, dm, 0), (o_out, 0, dm, 0), (tgt, 0, dm, nct)]
    dx_res, d_o, loss_acc, dmod_gate, d_final_g = _row_call(
        "final", final_fn, nt, tm, fin_in, [mod3, final_g2], [(t, dm, F32, 0), (t, dm, BF16, 0)],
        [((8, LANE), F32), ((8, 3 * dm), F32), ((1, dm), F32)])

    g_w_out = _mm_tn("d_w_out", merged, d_o)
    d_merged = _mm_nt("d_merged", d_o, w_out)

    def merge_bwd(i, r, f):
        _, vjp = jax.vjp(_fn_merge, r[0], r[1], r[2], r[3])
        da, db, dgh, dgr = vjp(r[4])
        return [da, db, jnp.concatenate([dgh, dgr], axis=1)], []

    da_hg, da_rw, dp_gt = _row_call("merge_bwd", merge_bwd, nt, tm, mg_in + [(d_merged, 0, dm, 0)], [],
                                    [(t, dm, BF16, 0), (t, dm, BF16, 0), (t, 2 * dm, F32, 0)], [])
    g_w_hg_st = _mm_t_st("d_w_hg", y_hg, restack(da_hg))
    g_w_rw_st = _mm_t_st("d_w_rw", y_rw, restack(da_rw))
    dy_hg = _mm_st_t("d_y_hg", restack(da_hg), w_hg_st)
    dy_rw = _mm_st_t("d_y_rw", restack(da_rw), w_rw_st)

    def hgpost_bwd(i, r, f):
        _, vjp = jax.vjp(_fn_hgpost, r[0], r[1], r[2], f[0])
        dof, _, dz, dg = vjp(r[3])
        return [dof, dz], [dg]

    do_hg, dz_hg, g_hg_norm = _row_call("hg_post_bwd", hgpost_bwd, nt, tm, hg_in + [(dy_hg, 0, hw, 0)], [hg_norm_g],
                                        [(t, hw, F32, 0), (t, hw, F32, 0)], [((1, hw), F32)])
    dqs, dis, dfs, g_lb = [], [], [], []
    for d in range(2):
        dq, di, df, dlb = _hg_scan_bwd(f"hg_scan_bwd{d}", p_hg, hg_lb[d], st_hg[d], do_hg, d, n_ctx)
        dqs.append(dq)
        dis.append(di)
        dfs.append(df)
        g_lb.append(dlb)
    (dqi,) = _row_call("hg_dsum", lambda i, r, f: ([jnp.concatenate([r[0] + r[1], r[2] + r[3]], axis=1)], []), nt, tm,
                       [(dqs[0], 0, hw, 0), (dqs[1], 0, hw, 0), (dis[0], 0, hw, 0), (dis[1], 0, hw, 0)], [],
                       [(t, 2 * hw, F32, 0)], [])
    g_hg_lb = jnp.stack(g_lb, axis=0)

    def rwpost_bwd(i, r, f):
        _, vjp = jax.vjp(_fn_rwpost, *r[:7], *f)
        g = vjp(r[7])
        zl = jnp.zeros((g[5].shape[0], 2 * RW_LORA), F32)
        dmain = jnp.concatenate([g[2], g[3], g[4]], axis=1)
        return [g[0], dmain, jnp.concatenate([zl, g[5]], axis=1), g[6]], list(g[7:])

    dy_sum, dsh_p, dsl_p, dz_rw, g_a0_p, g_a2_p, g_ka_p, g_rk, g_gn_g, g_gn_b = _row_call(
        "rw_post_bwd", rwpost_bwd, nt, tm, rw_in + [(dy_rw, 0, rw, 0)], rw_full,
        [(t, rw, F32, 0), (t, 3 * rw, F32, 0), (t, 4 * RW_LORA, F32, 0), (t, rw, F32, 0)],
        [(a.shape, F32) for a in rw_full])
    dmains, dloras, hp_grads = [dsh_p], [dsl_p], []
    for d in range(2):
        res = _rw_scan_bwd(f"rw_scan_bwd{d}", sh, hps[d], st_rw[d], dy_sum, d, n_ctx)
        dmains.append(res[0])
        dloras.append(res[1])
        hp_grads.append(res[2:])
    dsh_parts = [jnp.concatenate([m, l], axis=1) for m, l in zip(dmains, dloras)]
    dp_rs, g_mu = _shift_bwd("rw_shift_bwd", p_rs, rw_mu, dsh_parts, n_ctx_rows)

    def flat(a):
        if a.shape[1] == 1:
            return a.reshape(rw)
        return jnp.swapaxes(a, 0, 1).reshape(RW_LORA, rw)

    g_w0 = jnp.stack([flat(hp_grads[d][0]) for d in range(2)], axis=0)
    g_w2 = jnp.stack([flat(hp_grads[d][1]) for d in range(2)], axis=0)
    g_a0 = jnp.stack([flat(hp_grads[d][2]) for d in range(2)], axis=0) + g_a0_p
    g_a2 = jnp.stack([flat(hp_grads[d][3]) for d in range(2)], axis=0) + g_a2_p
    g_kk = (flat(hp_grads[0][4]) + flat(hp_grads[1][4])).reshape(1, rw)
    g_ka = (flat(hp_grads[0][5]) + flat(hp_grads[1][5])).reshape(1, rw) + g_ka_p

    dproj = jnp.concatenate([dqi, dfs[0], dfs[1], dz_hg, dp_rs, dz_rw, dp_gt], axis=1).astype(BF16)
    dproj_st = restack(dproj)
    g_w_in_st = _mm_t_st("d_w_in", h, dproj_st)
    dh = _mm_st_t("d_h", dproj_st, w_in_st)

    def h_bwd(i, r, f):
        shift, scale = pick(i, f[1])
        _, vjp = jax.vjp(_fn_h, r[0], f[0], scale, shift)
        ds, dg, dscale, dshift = vjp(r[1])
        row = jnp.concatenate([dshift, dscale, jnp.zeros((1, dm), F32)], axis=1)
        z = jnp.zeros_like(row)
        is_ctx = i < nct
        dmod = jnp.concatenate([jnp.where(is_ctx, z, row), jnp.where(is_ctx, row, z), jnp.zeros((6, 3 * dm), F32)], axis=0)
        return [ds + r[2]], [dg, dmod]

    grad_x, g_norm_g, dmod_h = _row_call(
        "h_bwd", h_bwd, nt, tm, [(xs, 0, dm, 0), (dh, 0, dm, 0), (dx_res, 0, dm, 0)], [norm_g, mod3],
        [(seq, dm, F32, nct)], [((1, dm), F32), ((8, 3 * dm), F32)])
    dmod = dmod_h + dmod_gate
    g_ada_b = (dmod[0] + dmod[1]).reshape(1, 3 * dm)
    g_ada_st = _mm_t_st("d_ada_w", sc, restack(dmod))
    d_sc = _mm_st_t("d_cond", restack(dmod), ada_st)

    def cond_bwd(i, r, f):
        _, vjp = jax.vjp(jax.nn.silu, r[0])
        return [vjp(r[1])[0]], []

    (d_cond,) = _row_call("cond_bwd", cond_bwd, 1, 8, [(cond, 0, dm, 0), (d_sc, 0, dm, 0)], [], [(8, dm, F32, 0)], [])

    grads = dict(
        c_ctx=d_cond[1], ada_w=g_ada_st, ada_b=g_ada_b, norm_g=g_norm_g, w_in=g_w_in_st, hg_lb=g_hg_lb,
        hg_norm_g=g_hg_norm, rw_mu=g_mu, rw_w0=g_w0, rw_w2=g_w2, rw_a0=g_a0, rw_a2=g_a2, rw_kk=g_kk, rw_ka=g_ka,
        rw_rk=g_rk, rw_gn_g=g_gn_g, rw_gn_b=g_gn_b, w_hg_out=g_w_hg_st, w_rw_out=g_w_rw_st, w_out=g_w_out,
        final_g=d_final_g.reshape(dm))
    return loss_acc[0:1, 0:1], grad_x, grads


def _my_place():
    return lax.axis_index("x"), lax.axis_index("y"), lax.axis_index("c")


def _chip_exchange(name, arrays, gather):
    na = len(arrays)

    def body(*refs):
        ins, outs = refs[:na], refs[na:2 * na]
        send_sems, recv_sems, local_sems = refs[2 * na:]
        x, y, c = _my_place()
        me = 2 * x + y
        peers = [(1 - x, y), (x, 1 - y), (1 - x, 1 - y)]

        def src(a, chip):
            return ins[a] if gather else ins[a].at[chip]

        def remote(a, k, land):
            px, py = peers[k]
            return pltpu.make_async_remote_copy(
                src_ref=src(a, 2 * px + py), dst_ref=outs[a].at[land], send_sem=send_sems.at[a, k],
                recv_sem=recv_sems.at[a, k], device_id=(px, py, c), device_id_type=pl.DeviceIdType.MESH)

        local = [pltpu.make_async_copy(src(a, me), outs[a].at[me], local_sems.at[a]) for a in range(na)]
        for cp in local:
            cp.start()
        for a in range(na):
            for k in range(3):
                remote(a, k, me).start()
        for a in range(na):
            for k in range(3):
                px, py = peers[k]
                remote(a, k, 2 * px + py).wait_recv()
        for a in range(na):
            for k in range(3):
                remote(a, k, me).wait_send()
        for cp in local:
            cp.wait()

    def oshape(a):
        return (N_SHARD,) + a.shape if gather else a.shape

    hbm = pl.BlockSpec(memory_space=pl.ANY)
    return pl.pallas_call(
        body, name=name, in_specs=[hbm] * na, out_specs=[hbm] * na,
        out_shape=[jax.ShapeDtypeStruct(oshape(a), a.dtype) for a in arrays],
        scratch_shapes=[pltpu.SemaphoreType.DMA((na, 3)), pltpu.SemaphoreType.DMA((na, 3)), pltpu.SemaphoreType.DMA((na,))],
    )(*arrays)


def _sibling_exchange(name, arrays):
    na = len(arrays)

    def body(*refs):
        ins, outs = refs[:na], refs[na:2 * na]
        send_sems, recv_sems, local_sems = refs[2 * na:]
        x, y, c = _my_place()

        def remote(a, land):
            return pltpu.make_async_remote_copy(
                src_ref=ins[a], dst_ref=outs[a].at[land], send_sem=send_sems.at[a], recv_sem=recv_sems.at[a],
                device_id=(x, y, 1 - c), device_id_type=pl.DeviceIdType.MESH)

        local = [pltpu.make_async_copy(ins[a], outs[a].at[c], local_sems.at[a]) for a in range(na)]
        for cp in local:
            cp.start()
        for a in range(na):
            remote(a, c).start()
        for a in range(na):
            remote(a, 1 - c).wait_recv()
        for a in range(na):
            remote(a, c).wait_send()
        for cp in local:
            cp.wait()

    hbm = pl.BlockSpec(memory_space=pl.ANY)
    return pl.pallas_call(
        body, name=name, in_specs=[hbm] * na, out_specs=[hbm] * na,
        out_shape=[jax.ShapeDtypeStruct((2,) + a.shape, a.dtype) for a in arrays],
        scratch_shapes=[pltpu.SemaphoreType.DMA((na,)), pltpu.SemaphoreType.DMA((na,)), pltpu.SemaphoreType.DMA((na,))],
    )(*arrays)


def _gather_all(name, a):
    def body(in_ref, out_ref, send_sems, recv_sems, local_sem):
        x, y, c = _my_place()
        me = 4 * x + 2 * y + c

        def peer(k):
            return (x ^ (k >> 2), y ^ ((k >> 1) & 1), c ^ (k & 1))

        def remote(k, land):
            return pltpu.make_async_remote_copy(
                src_ref=in_ref, dst_ref=out_ref.at[land], send_sem=send_sems.at[k - 1], recv_sem=recv_sems.at[k - 1],
                device_id=peer(k), device_id_type=pl.DeviceIdType.MESH)

        local = pltpu.make_async_copy(in_ref, out_ref.at[me], local_sem)
        local.start()
        for k in range(1, N_DEV):
            remote(k, me).start()
        for k in range(1, N_DEV):
            px, py, pc = peer(k)
            remote(k, 4 * px + 2 * py + pc).wait_recv()
        for k in range(1, N_DEV):
            remote(k, me).wait_send()
        local.wait()

    hbm = pl.BlockSpec(memory_space=pl.ANY)
    return pl.pallas_call(
        body, name=name, in_specs=[hbm], out_specs=hbm,
        out_shape=jax.ShapeDtypeStruct((N_DEV,) + a.shape, a.dtype),
        scratch_shapes=[pltpu.SemaphoreType.DMA((N_DEV - 1,)), pltpu.SemaphoreType.DMA((N_DEV - 1,)), pltpu.SemaphoreType.DMA],
    )(a)


def _row_tile_for(rows, cols, budget=1 << 20):
    for tm in (1024, 512, 256, 128, 64, 32, 16, 8):
        if rows % tm == 0 and tm * cols * 4 <= budget:
            return tm
    return rows


def _slot_sum(vals):
    g = vals[0]
    for v in vals[1:]:
        g = g + v
    return g


def _sum_slots(name, st):
    ns, rows, cols = st.shape
    tm = _row_tile_for(rows, cols)

    def body(s_ref, o_ref):
        o_ref[...] = _slot_sum([s_ref[j] for j in range(ns)])

    return pl.pallas_call(
        body, name=name, grid=(rows // tm,),
        in_specs=[pl.BlockSpec((ns, tm, cols), lambda i: (0, i, 0))],
        out_specs=pl.BlockSpec((tm, cols), lambda i: (i, 0)),
        out_shape=jax.ShapeDtypeStruct((rows, cols), F32),
        compiler_params=_params(("parallel",)),
    )(st)


def _adamw(name, p, m, v, gst):
    rows, cols = p.shape
    ns = gst.shape[0]
    tm = _row_tile_for(rows, cols, budget=(1 << 20) // 2)

    def body(p_ref, m_ref, v_ref, g_ref, go_ref, d_ref, mo_ref, vo_ref):
        g = _slot_sum([g_ref[j] for j in range(ns)])
        mn = ADAM_B1 * m_ref[...] + (1.0 - ADAM_B1) * g
        vn = ADAM_B2 * v_ref[...] + (1.0 - ADAM_B2) * jnp.square(g)
        m_hat = mn / (1.0 - ADAM_B1 ** ADAM_STEP)
        v_hat = vn / (1.0 - ADAM_B2 ** ADAM_STEP)
        go_ref[...] = g
        d_ref[...] = -ADAM_LR * (m_hat / (jnp.sqrt(v_hat) + ADAM_EPS) + ADAM_WD * p_ref[...])
        mo_ref[...] = mn
        vo_ref[...] = vn

    blk = pl.BlockSpec((tm, cols), lambda i: (i, 0))
    return pl.pallas_call(
        body, name=name, grid=(rows // tm,),
        in_specs=[blk, blk, blk, pl.BlockSpec((ns, tm, cols), lambda i: (0, i, 0))],
        out_specs=[blk] * 4, out_shape=[jax.ShapeDtypeStruct((rows, cols), F32)] * 4,
        compiler_params=_params(("parallel",)),
    )(p, m, v, gst)


def _pack(parts, width=LANE, mult=8):
    flat = jnp.concatenate([a.reshape(-1) for a in parts])
    n = flat.shape[0]
    per = width * mult
    total = -(-n // per) * per
    return jnp.pad(flat, (0, total - n)).reshape(total // width, width)


def _unpack(packed, shapes):
    flat = packed.reshape(-1)
    out, off = [], 0
    for s in shapes:
        n = 1
        for d in s:
            n *= d
        out.append(flat[off:off + n].reshape(s))
        off += n
    return out


_SMALL_SHARDED = ("hg_lb", "rw_mu", "rw_w0", "rw_w2", "rw_a0", "rw_a2")
_REPLICATED = ("c_ctx", "ada_b", "norm_g", "hg_norm_g", "rw_kk", "rw_ka", "rw_rk", "rw_gn_g", "rw_gn_b", "final_g")
_BIG = ("ada_w", "w_in", "w_hg_out", "w_rw_out", "w_out")
_WEIGHTS = ("c_ctx", "ada_w", "ada_b", "norm_g", "w_in", "hg_lb", "hg_norm_g", "rw_mu", "rw_w0", "rw_w2", "rw_a0", "rw_a2",
            "rw_kk", "rw_ka", "rw_rk", "rw_gn_g", "rw_gn_b", "w_hg_out", "w_rw_out", "w_out", "final_g")


def _join_shards(st):
    a = jnp.moveaxis(st, 0, -2)
    return a.reshape(a.shape[:-2] + (a.shape[-2] * a.shape[-1],))


def _split_shards(a):
    s = a.reshape(a.shape[:-1] + (N_SHARD, a.shape[-1] // N_SHARD))
    return jnp.moveaxis(s, -2, 0)


def kernel(x, c, ctx, c_ctx, ada_w, ada_b, norm_g, w_in, hg_lb, hg_norm_g, rw_mu, rw_w0, rw_w2, rw_a0, rw_a2, rw_kk, rw_ka, rw_rk, rw_gn_g, rw_gn_b, w_hg_out, w_rw_out, w_out, final_g, loss_target, m_c_ctx, m_ada_w, m_ada_b, m_norm_g, m_w_in, m_hg_lb, m_hg_norm_g, m_rw_mu, m_rw_w0, m_rw_w2, m_rw_a0, m_rw_a2, m_rw_kk, m_rw_ka, m_rw_rk, m_rw_gn_g, m_rw_gn_b, m_w_hg_out, m_w_rw_out, m_w_out, m_final_g, v_c_ctx, v_ada_w, v_ada_b, v_norm_g, v_w_in, v_hg_lb, v_hg_norm_g, v_rw_mu, v_rw_w0, v_rw_w2, v_rw_a0, v_rw_a2, v_rw_kk, v_rw_ka, v_rw_rk, v_rw_gn_g, v_rw_gn_b, v_w_hg_out, v_w_rw_out, v_w_out, v_final_g):
    w = dict(c_ctx=c_ctx, ada_w=ada_w, ada_b=ada_b, norm_g=norm_g, w_in=w_in, hg_lb=hg_lb, hg_norm_g=hg_norm_g, rw_mu=rw_mu,
             rw_w0=rw_w0, rw_w2=rw_w2, rw_a0=rw_a0, rw_a2=rw_a2, rw_kk=rw_kk, rw_ka=rw_ka, rw_rk=rw_rk, rw_gn_g=rw_gn_g,
             rw_gn_b=rw_gn_b, w_hg_out=w_hg_out, w_rw_out=w_rw_out, w_out=w_out, final_g=final_g)
    m = dict(c_ctx=m_c_ctx, ada_w=m_ada_w, ada_b=m_ada_b, norm_g=m_norm_g, w_in=m_w_in, hg_lb=m_hg_lb, hg_norm_g=m_hg_norm_g,
             rw_mu=m_rw_mu, rw_w0=m_rw_w0, rw_w2=m_rw_w2, rw_a0=m_rw_a0, rw_a2=m_rw_a2, rw_kk=m_rw_kk, rw_ka=m_rw_ka,
             rw_rk=m_rw_rk, rw_gn_g=m_rw_gn_g, rw_gn_b=m_rw_gn_b, w_hg_out=m_w_hg_out, w_rw_out=m_w_rw_out, w_out=m_w_out,
             final_g=m_final_g)
    v = dict(c_ctx=v_c_ctx, ada_w=v_ada_w, ada_b=v_ada_b, norm_g=v_norm_g, w_in=v_w_in, hg_lb=v_hg_lb, hg_norm_g=v_hg_norm_g,
             rw_mu=v_rw_mu, rw_w0=v_rw_w0, rw_w2=v_rw_w2, rw_a0=v_rw_a0, rw_a2=v_rw_a2, rw_kk=v_rw_kk, rw_ka=v_rw_ka,
             rw_rk=v_rw_rk, rw_gn_g=v_rw_gn_g, rw_gn_b=v_rw_gn_b, w_hg_out=v_w_hg_out, w_rw_out=v_w_rw_out, w_out=v_w_out,
             final_g=v_final_g)

    def mat(a):
        return a.reshape(a.shape[-2], a.shape[-1])

    small_shapes = [w[n].shape for n in _SMALL_SHARDED]
    gathered = _chip_exchange("weights_gather", [mat(w[n]) for n in _BIG] + [_pack([w[n] for n in _SMALL_SHARDED])], True)
    ada_st, w_in_st, w_hg_st, w_rw_st, w_out_st, small_st = gathered
    full_small = {}
    per_chip = [_unpack(small_st[j], small_shapes) for j in range(N_SHARD)]
    for i, n in enumerate(_SMALL_SHARDED):
        full_small[n] = _join_shards(jnp.stack([per_chip[j][i] for j in range(N_SHARD)], axis=0))
    dm = x.shape[-1]
    w_out_full = w_out_st.reshape(dm, dm)

    loss_b, grad_x, g = _local_step(
        x[0], c, ctx[0], c_ctx, ada_st, ada_b, norm_g, w_in_st, full_small["hg_lb"], hg_norm_g, full_small["rw_mu"][0],
        full_small["rw_w0"][0], full_small["rw_w2"][0], full_small["rw_a0"][0], full_small["rw_a2"][0], rw_kk, rw_ka, rw_rk,
        rw_gn_g, rw_gn_b, w_hg_st, w_rw_st, w_out_full, final_g, loss_target[0])
    loss = lax.psum(loss_b[0, 0], ("x", "y", "c"))

    g_small = {"hg_lb": g["hg_lb"], "rw_mu": g["rw_mu"][None], "rw_w0": g["rw_w0"][None], "rw_w2": g["rw_w2"][None],
               "rw_a0": g["rw_a0"][None], "rw_a2": g["rw_a2"][None]}
    split = {n: _split_shards(g_small[n]) for n in _SMALL_SHARDED}
    small_parts = jnp.stack([_pack([split[n][j] for n in _SMALL_SHARDED]) for j in range(N_SHARD)], axis=0)
    partial = [g["ada_w"], g["w_in"], g["w_hg_out"], g["w_rw_out"], g["w_out"].reshape(N_SHARD, dm // N_SHARD, dm), small_parts]
    landed = _chip_exchange("grads_scatter", partial, False)
    core_sums = [_sum_slots(f"grads_sum{i}", a) for i, a in enumerate(landed)]
    pairs = _sibling_exchange("grads_sibling", core_sums)
    rep_shapes = [w[n].shape for n in _REPLICATED]
    rep_all = _gather_all("grads_replicated", _pack([g[n].reshape(w[n].shape) for n in _REPLICATED]))

    res = {}
    for i, n in enumerate(_BIG):
        outs = _adamw(f"adamw_{n}", mat(w[n]), mat(m[n]), mat(v[n]), pairs[i])
        res[n] = [o.reshape(w[n].shape) for o in outs]
    outs = _adamw("adamw_small", _pack([w[n] for n in _SMALL_SHARDED]), _pack([m[n] for n in _SMALL_SHARDED]),
                  _pack([v[n] for n in _SMALL_SHARDED]), pairs[len(_BIG)])
    for i, vals in enumerate(zip(*[_unpack(o, small_shapes) for o in outs])):
        res[_SMALL_SHARDED[i]] = list(vals)
    outs = _adamw("adamw_replicated", _pack([w[n] for n in _REPLICATED]), _pack([m[n] for n in _REPLICATED]),
                  _pack([v[n] for n in _REPLICATED]), rep_all)
    for i, vals in enumerate(zip(*[_unpack(o, rep_shapes) for o in outs])):
        res[_REPLICATED[i]] = list(vals)

    return (loss, grad_x[None], *[res[n][0] for n in _WEIGHTS], *[res[n][1] for n in _WEIGHTS],
            *[res[n][2] for n in _WEIGHTS], *[res[n][3] for n in _WEIGHTS])
```

```python
import functools

import jax
import jax.numpy as jnp
from jax import lax
from jax.experimental import pallas as pl
from jax.experimental.pallas import tpu as pltpu

HI = lax.Precision.HIGHEST
F32 = jnp.float32
BF16 = jnp.bfloat16

NORM_EPS = 1e-6
HG_HEAD = 128
RW_HEAD = 64
RW_LORA = 64
RW_GN_EPS = 64e-5
GRID_W = 64
SUB = 16
STEP = 64
N_SHARD = 4
N_DEV = 8
LANE = 128

ADAM_LR = 0.001
ADAM_B1 = 0.9
ADAM_B2 = 0.999
ADAM_EPS = 1e-08
ADAM_WD = 0.01
ADAM_STEP = 10

VMEM_LIMIT = 56 * 1024 * 1024


def _params(sem=None):
    return pltpu.CompilerParams(dimension_semantics=sem, vmem_limit_bytes=VMEM_LIMIT)


def _tile(n, cands):
    for c in cands:
        if n % c == 0:
            return c
    return n


def _iota2(n, m, d):
    return lax.broadcasted_iota(jnp.int32, (n, m), d)


def _before(n, rev, strict):
    t, s = _iota2(n, n, 0), _iota2(n, n, 1)
    if rev:
        return (s > t) if strict else (s >= t)
    return (s < t) if strict else (s <= t)


def _bdot(a, b, spec):
    return jnp.einsum(spec, a, b, precision=HI, preferred_element_type=F32)


def _hg_step(s0, qraw, iin, fin, lb2, rev):
    c, w = qraw.shape
    h = w // HG_HEAD
    nsub = c // SUB
    lb = jax.nn.sigmoid(lb2[0:1] - lb2[1:2])
    q = jax.nn.silu(qraw)
    fg = lb + (1.0 - lb) * jax.nn.sigmoid(fin)
    kk = 1.0 - fg
    g = jnp.log(fg)
    incl = _before(SUB, rev, False).astype(F32)
    rows = lax.broadcasted_iota(jnp.int32, (SUB, 1), 0)
    last = 0 if rev else SUB - 1

    def heads(a):
        return jnp.swapaxes(a.reshape(a.shape[0], h, HG_HEAD), 0, 1)

    s = s0
    outs = [None] * nsub
    order = range(nsub - 1, -1, -1) if rev else range(nsub)
    for j in order:
        sl = slice(j * SUB, (j + 1) * SUB)
        qs, ks, vs, gs = q[sl], kk[sl], iin[sl], g[sl]
        bc = jnp.dot(incl, gs, precision=HI, preferred_element_type=F32)
        o = jnp.zeros((SUB, h, HG_HEAD), F32)
        for si in range(SUB):
            dec = jnp.exp(jnp.minimum(bc - bc[si:si + 1], 0.0))
            a = (qs * ks[si:si + 1] * dec).reshape(SUB, h, HG_HEAD).sum(-1)
            valid = (rows <= si) if rev else (rows >= si)
            a = jnp.where(valid, a, 0.0)
            o = o + a[:, :, None] * vs[si:si + 1].reshape(1, h, HG_HEAD)
        o = o.reshape(SUB, w)
        o = o + jnp.swapaxes(_bdot(heads(qs * jnp.exp(bc)), s, 'htk,hvk->htv'), 0, 1).reshape(SUB, w)
        blast = bc[last:last + 1]
        kdec = ks * jnp.exp(blast - bc)
        s = heads(jnp.exp(blast)) * s + _bdot(heads(vs), heads(kdec), 'hsv,hsk->hvk')
        outs[j] = o
    return jnp.concatenate(outs, axis=0), s


def _tri_solve(lmat, rhs, rev):
    hh, c, _ = lmat.shape
    nb = c // SUB
    eye = (_iota2(SUB, SUB, 0) == _iota2(SUB, SUB, 1)).astype(F32)
    rowid = lax.broadcasted_iota(jnp.int32, (1, SUB, 1), 1)
    diag = jnp.concatenate([lmat[:, i * SUB:(i + 1) * SUB, i * SUB:(i + 1) * SUB] for i in range(nb)], axis=0)
    tinv = jnp.broadcast_to(eye[None], diag.shape)
    order = range(SUB - 2, -1, -1) if rev else range(1, SUB)
    for t in order:
        row = eye[t:t + 1][None] - (diag[:, t, :][:, :, None] * tinv).sum(axis=1, keepdims=True)
        tinv = jnp.where(rowid == t, row, tinv)
    p = [None] * nb
    done = []
    for i in (range(nb - 1, -1, -1) if rev else range(nb)):
        r = rhs[:, i * SUB:(i + 1) * SUB]
        for m in done:
            r = r - _bdot(lmat[:, i * SUB:(i + 1) * SUB, m * SUB:(m + 1) * SUB], p[m], 'hts,hsv->htv')
        p[i] = _bdot(tinv[i * hh:(i + 1) * hh], r, 'hts,hsv->htv')
        done.append(i)
    return jnp.concatenate(p, axis=1)


def _rw_step(s0, r, k, v, wlo, alo, w0h, w2h, a0h, a2h, kkh, kah, rev):
    hh, c, _ = r.shape
    tl = jnp.broadcast_to(jnp.tanh(wlo)[None], (hh, c, wlo.shape[1]))
    al = jnp.broadcast_to(alo[None], (hh, c, alo.shape[1]))
    wlog = -jax.nn.softplus(-(w0h + _bdot(tl, w2h, 'hcl,hlj->hcj'))) - 0.5
    lw = -jnp.exp(wlog)
    a = jax.nn.sigmoid(a0h + _bdot(al, a2h, 'hcl,hlj->hcj'))
    kk = k * kkh
    kk = kk * lax.rsqrt(jnp.sum(kk * kk, axis=-1, keepdims=True) + 1e-12)
    kd = k * (1.0 + (a - 1.0) * kah)
    b = kk * a
    incl = jnp.broadcast_to(_before(c, rev, False).astype(F32)[None], (hh, c, c))
    cum = _bdot(incl, lw, 'hts,hsk->htk')
    ecum, encum = jnp.exp(cum), jnp.exp(-cum)
    alpha = jnp.exp(cum - lw) * kk
    beta = b * encum
    kappa = kd * encum
    rho = r * ecum
    m_lt = _before(c, rev, True)[None]
    m_le = _before(c, rev, False)[None]
    a_kap = jnp.where(m_lt, _bdot(alpha, kappa, 'htk,hsk->hts'), 0.0)
    a_bet = jnp.where(m_lt, _bdot(alpha, beta, 'htk,hsk->hts'), 0.0)
    b_kap = jnp.where(m_le, _bdot(rho, kappa, 'htk,hsk->hts'), 0.0)
    b_bet = jnp.where(m_le, _bdot(rho, beta, 'htk,hsk->hts'), 0.0)
    rhs = _bdot(alpha, s0, 'htk,hvk->htv') + _bdot(a_kap, v, 'hts,hsv->htv')
    p = _tri_solve(a_bet, rhs, rev)
    y = _bdot(rho, s0, 'htk,hvk->htv') + _bdot(b_kap, v, 'hts,hsv->htv') - _bdot(b_bet, p, 'hts,hsv->htv')
    stil = s0 + _bdot(v, kappa, 'hsv,hsk->hvk') - _bdot(p, beta, 'hsv,hsk->hvk')
    last = 0 if rev else c - 1
    return y, stil * ecum[:, last:last + 1, :]


def _fn_h(s, norm_g, scale, shift):
    return s * lax.rsqrt(jnp.mean(s * s, axis=-1, keepdims=True) + NORM_EPS) * norm_g * (1.0 + scale) + shift


def _fn_hgpost(of, ob, z, g):
    tm, w = of.shape
    o = (of + ob).reshape(tm, w // HG_HEAD, HG_HEAD)
    o = o * lax.rsqrt(jnp.mean(o * o, axis=-1, keepdims=True) + NORM_EPS)
    return o.reshape(tm, w) * g * jax.nn.silu(z)


def _fn_rwpost(y0, y1, r, k, v, alo, z, a0, a2, k_a, r_k, gn_g, gn_b):
    tm, w = r.shape
    nh = w // RW_HEAD
    asum = 0.0
    for d in range(2):
        asum = asum + jax.nn.sigmoid(a0[d:d + 1] + jnp.dot(alo[:, d * RW_LORA:(d + 1) * RW_LORA], a2[d],
                                                           precision=HI, preferred_element_type=F32))
    k_sum = k * (2.0 + (asum - 2.0) * k_a)
    ys = (y0 + y1).reshape(tm, nh, RW_HEAD)
    mean = jnp.mean(ys, axis=-1, keepdims=True)
    var = jnp.mean(jnp.square(ys - mean), axis=-1, keepdims=True)
    y = ((ys - mean) * lax.rsqrt(var + RW_GN_EPS)).reshape(tm, w) * gn_g + gn_b
    bonus = jnp.sum((r * k_sum * r_k).reshape(tm, nh, RW_HEAD), axis=-1, keepdims=True) * v.reshape(tm, nh, RW_HEAD)
    return (y + bonus.reshape(tm, w)) * jax.nn.silu(z)


def _fn_merge(a, b, ghg, grw):
    return jax.nn.sigmoid(ghg) * a + jax.nn.sigmoid(grw) * b


def _fn_final(xs, o, gate, final_g, tgt):
    x2 = xs + gate * o
    y = x2 * lax.rsqrt(jnp.mean(x2 * x2, axis=-1, keepdims=True) + NORM_EPS) * final_g
    return 0.5 * jnp.sum(jnp.mean(jnp.square(y - tgt), axis=-1))


def _row_call(name, fn, n_tiles, tm, row_ins, full_ins, row_outs, acc_outs):
    n_ri, n_fi, n_ro = len(row_ins), len(full_ins), len(row_outs)

    def body(*refs):
        i = pl.program_id(0)
        rvals = [r[...] for r in refs[:n_ri]]
        fvals = [r[...] for r in refs[n_ri:n_ri + n_fi]]
        outs = refs[n_ri + n_fi:]
        ro, ao = fn(i, rvals, fvals)
        for ref, val in zip(outs[:n_ro], ro):
            ref[...] = val.astype(ref.dtype)
        for ref, val in zip(outs[n_ro:], ao):
            @pl.when(i == 0)
            def _(ref=ref):
                ref[...] = jnp.zeros_like(ref)
            ref[...] += val.astype(ref.dtype)

    def rspec(width, cb, off):
        return pl.BlockSpec((tm, width), lambda i: (jnp.maximum(i - off, 0), cb))

    def fspec(shape):
        nd = len(shape)
        return pl.BlockSpec(shape, lambda i: (0,) * nd)

    in_specs = [rspec(w, cb, off) for (_, cb, w, off) in row_ins] + [fspec(a.shape) for a in full_ins]
    out_specs = [rspec(w, 0, off) for (_, w, _, off) in row_outs] + [fspec(s) for (s, _) in acc_outs]
    out_shape = [jax.ShapeDtypeStruct((rows, w), dt) for (rows, w, dt, _) in row_outs] + \
                [jax.ShapeDtypeStruct(s, dt) for (s, dt) in acc_outs]
    res = pl.pallas_call(
        body, name=name, grid=(n_tiles,), in_specs=in_specs, out_specs=out_specs, out_shape=out_shape,
        compiler_params=_params(("arbitrary",)),
    )(*[a for (a, _, _, _) in row_ins], *full_ins)
    return list(res)


def _mm(name, a, b, m, n, k_steps, tm, tn, a_block, a_map, b_block, b_map, o_shape, o_block, o_map,
        contract, out_dtype=F32):
    def body(a_ref, b_ref, o_ref, acc_ref):
        kk = pl.program_id(2)

        @pl.when(kk == 0)
        def _():
            acc_ref[...] = jnp.zeros_like(acc_ref)

        acc_ref[...] += lax.dot_general(a_ref[...].astype(BF16), b_ref[...].astype(BF16),
                                        (contract, ((), ())), preferred_element_type=F32)

        @pl.when(kk == k_steps - 1)
        def _():
            o_ref[...] = acc_ref[...].astype(o_ref.dtype)

    return pl.pallas_call(
        body, name=name, grid=(m // tm, n // tn, k_steps),
        in_specs=[pl.BlockSpec(a_block, a_map), pl.BlockSpec(b_block, b_map)],
        out_specs=pl.BlockSpec(o_block, o_map),
        out_shape=jax.ShapeDtypeStruct(o_shape, out_dtype),
        scratch_shapes=[pltpu.VMEM((tm, tn), F32)],
        compiler_params=_params(("parallel", "parallel", "arbitrary")),
    )(a, b)


_TM = (512, 256, 128, 64, 32, 16, 8)
_TN = (512, 256, 128)
_TK = (512, 256, 128)


def _mm_nn(name, a, b, out_dtype=F32):
    m, k = a.shape
    n = b.shape[1]
    tm, tn, tk = _tile(m, _TM), _tile(n, _TN), _tile(k, _TK)
    return _mm(name, a, b, m, n, k // tk, tm, tn, (tm, tk), lambda i, j, s: (i, s), (tk, tn), lambda i, j, s: (s, j),
               (m, n), (tm, tn), lambda i, j, s: (i, j), ((1,), (0,)), out_dtype)


def _mm_nt(name, a, b, out_dtype=F32):
    m, k = a.shape
    n = b.shape[0]
    tm, tn, tk = _tile(m, _TM), _tile(n, _TN), _tile(k, _TK)
    return _mm(name, a, b, m, n, k // tk, tm, tn, (tm, tk), lambda i, j, s: (i, s), (tn, tk), lambda i, j, s: (j, s),
               (m, n), (tm, tn), lambda i, j, s: (i, j), ((1,), (1,)), out_dtype)


def _mm_tn(name, a, b, out_dtype=F32):
    k, m = a.shape
    n = b.shape[1]
    tm, tn, tk = _tile(m, _TM), _tile(n, _TN), _tile(k, _TK)
    return _mm(name, a, b, m, n, k // tk, tm, tn, (tk, tm), lambda i, j, s: (s, i), (tk, tn), lambda i, j, s: (s, j),
               (m, n), (tm, tn), lambda i, j, s: (i, j), ((0,), (0,)), out_dtype)


def _mm_n_st(name, a, bst, out_dtype=F32):
    m, k = a.shape
    ns_, _, ns = bst.shape
    tm, tk = _tile(m, (256, 128, 64, 32, 16, 8)), _tile(k, (256, 128))
    return _mm(name, a, bst, m, ns_ * ns, k // tk, tm, ns,
               (tm, tk), lambda i, j, s: (i, s), (None, tk, ns), lambda i, j, s: (j, s, 0),
               (ns_, m, ns), (None, tm, ns), lambda i, j, s: (j, i, 0), ((1,), (0,)), out_dtype)


def _mm_st_t(name, ast, bst, out_dtype=F32):
    ns_, m, ns = ast.shape
    n = bst.shape[1]
    tm, tn = _tile(m, (256, 128, 64, 32, 16, 8)), _tile(n, (256, 128))
    return _mm(name, ast, bst, m, n, ns_, tm, tn,
               (None, tm, ns), lambda i, j, s: (s, i, 0), (None, tn, ns), lambda i, j, s: (s, j, 0),
               (m, n), (tm, tn), lambda i, j, s: (i, j), ((1,), (1,)), out_dtype)


def _mm_t_st(name, a, bst, out_dtype=F32):
    k, m = a.shape
    ns_, _, ns = bst.shape
    tm, tk = _tile(m, (256, 128, 64, 32, 16, 8)), _tile(k, (256, 128))
    return _mm(name, a, bst, m, ns_ * ns, k // tk, tm, ns,
               (tk, tm), lambda i, j, s: (s, i), (None, tk, ns), lambda i, j, s: (j, s, 0),
               (ns_, m, ns), (None, tm, ns), lambda i, j, s: (j, i, 0), ((0,), (0,)), out_dtype)


def _scan_order(j, n_ctx, n_all, rev):
    if not rev:
        return j
    return jnp.where(j < n_ctx, n_ctx - 1 - j, n_all - 1 - (j - n_ctx))


def _hg_scan_fwd(name, p_hg, lb2, d, n_ctx):
    t, w5 = p_hg.shape
    w = w5 // 5
    h = w // HG_HEAD
    n = t // STEP
    rev = d == 1

    def body(q_ref, i_ref, f_ref, lb_ref, o_ref, st_ref, s_ref):
        j = pl.program_id(0)

        @pl.when(j == 0)
        def _():
            s_ref[...] = jnp.zeros_like(s_ref)

        s0 = s_ref[...]
        st_ref[...] = s0
        o, s1 = _hg_step(s0, q_ref[...], i_ref[...], f_ref[...], lb_ref[...], rev)
        o_ref[...] = o
        s_ref[...] = s1

    def rows(cb):
        return pl.BlockSpec((STEP, w), lambda j: (_scan_order(j, n_ctx, n, rev), cb))

    return pl.pallas_call(
        body, name=name, grid=(n,),
        in_specs=[rows(0), rows(1), rows(2 + d), pl.BlockSpec((2, w), lambda j: (0, 0))],
        out_specs=[rows(0), pl.BlockSpec((None, h, HG_HEAD, HG_HEAD), lambda j: (j, 0, 0, 0))],
        out_shape=[jax.ShapeDtypeStruct((t, w), F32), jax.ShapeDtypeStruct((n, h, HG_HEAD, HG_HEAD), F32)],
        scratch_shapes=[pltpu.VMEM((h, HG_HEAD, HG_HEAD), F32)],
        compiler_params=_params(("arbitrary",)),
    )(p_hg, p_hg, p_hg, lb2)


def _hg_scan_bwd(name, p_hg, lb2, states, do, d, n_ctx):
    t, w5 = p_hg.shape
    w = w5 // 5
    h = w // HG_HEAD
    n = t // STEP
    rev = d == 1

    def body(q_ref, i_ref, f_ref, lb_ref, st_ref, do_ref, dq_ref, di_ref, df_ref, dlb_ref, ds_ref):
        step = pl.program_id(0)

        @pl.when(step == 0)
        def _():
            ds_ref[...] = jnp.zeros_like(ds_ref)
            dlb_ref[...] = jnp.zeros_like(dlb_ref)

        _, vjp = jax.vjp(lambda s0, q, i, f, lb: _hg_step(s0, q, i, f, lb, rev),
                         st_ref[...], q_ref[...], i_ref[...], f_ref[...], lb_ref[...])
        ds0, dq, di, df, dlb = vjp((do_ref[...], ds_ref[...]))
        dq_ref[...] = dq
        di_ref[...] = di
        df_ref[...] = df
        dlb_ref[...] += dlb
        ds_ref[...] = ds0

    def rows(cb):
        return pl.BlockSpec((STEP, w), lambda s: (_scan_order(n - 1 - s, n_ctx, n, rev), cb))

    return pl.pallas_call(
        body, name=name, grid=(n,),
        in_specs=[rows(0), rows(1), rows(2 + d), pl.BlockSpec((2, w), lambda s: (0, 0)),
                  pl.BlockSpec((None, h, HG_HEAD, HG_HEAD), lambda s: (n - 1 - s, 0, 0, 0)), rows(0)],
        out_specs=[rows(0), rows(0), rows(0), pl.BlockSpec((2, w), lambda s: (0, 0))],
        out_shape=[jax.ShapeDtypeStruct((t, w), F32)] * 3 + [jax.ShapeDtypeStruct((2, w), F32)],
        scratch_shapes=[pltpu.VMEM((h, HG_HEAD, HG_HEAD), F32)],
        compiler_params=_params(("arbitrary",)),
    )(p_hg, p_hg, p_hg, lb2, states, do)


def _to_heads(a, nh):
    return jnp.stack([a[:, i * RW_HEAD:(i + 1) * RW_HEAD] for i in range(nh)], axis=0)


def _from_heads(a):
    return jnp.concatenate([a[i] for i in range(a.shape[0])], axis=-1)


def _rw_scan_fwd(name, sh, hp, d, n_ctx):
    t = sh.shape[0]
    w = (sh.shape[1] - 4 * RW_LORA) // 3
    nh = w // RW_HEAD
    n = t // STEP
    rev = d == 1
    lo = 3 * w // LANE

    def body(r_ref, k_ref, v_ref, wl_ref, al_ref, w0_ref, w2_ref, a0_ref, a2_ref, kk_ref, ka_ref,
             y_ref, st_ref, s_ref):
        j = pl.program_id(0)

        @pl.when(j == 0)
        def _():
            s_ref[...] = jnp.zeros_like(s_ref)

        s0 = s_ref[...]
        st_ref[...] = s0
        wl = wl_ref[...][:, d * RW_LORA:(d + 1) * RW_LORA]
        al = al_ref[...][:, d * RW_LORA:(d + 1) * RW_LORA]
        y, s1 = _rw_step(s0, _to_heads(r_ref[...], nh), _to_heads(k_ref[...], nh), _to_heads(v_ref[...], nh), wl, al,
                         w0_ref[...], w2_ref[...], a0_ref[...], a2_ref[...], kk_ref[...], ka_ref[...], rev)
        y_ref[...] = _from_heads(y)
        s_ref[...] = s1

    def rows(cb, width=w):
        return pl.BlockSpec((STEP, width), lambda j: (_scan_order(j, n_ctx, n, rev), cb))

    def whole(a):
        nd = a.ndim
        return pl.BlockSpec(a.shape, lambda j: (0,) * nd)

    return pl.pallas_call(
        body, name=name, grid=(n,),
        in_specs=[rows(0), rows(1), rows(2), rows(lo, LANE), rows(lo + 1, LANE)] + [whole(a) for a in hp],
        out_specs=[rows(0), pl.BlockSpec((None, nh, RW_HEAD, RW_HEAD), lambda j: (j, 0, 0, 0))],
        out_shape=[jax.ShapeDtypeStruct((t, w), F32), jax.ShapeDtypeStruct((n, nh, RW_HEAD, RW_HEAD), F32)],
        scratch_shapes=[pltpu.VMEM((nh, RW_HEAD, RW_HEAD), F32)],
        compiler_params=_params(("arbitrary",)),
    )(sh, sh, sh, sh, sh, *hp)


def _rw_scan_bwd(name, sh, hp, states, dy, d, n_ctx):
    t = sh.shape[0]
    w = (sh.shape[1] - 4 * RW_LORA) // 3
    nh = w // RW_HEAD
    n = t // STEP
    rev = d == 1
    lo = 3 * w // LANE

    def body(r_ref, k_ref, v_ref, wl_ref, al_ref, w0_ref, w2_ref, a0_ref, a2_ref, kk_ref, ka_ref, st_ref, dy_ref,
             dm_ref, dl_ref, dw0_ref, dw2_ref, da0_ref, da2_ref, dkk_ref, dka_ref, ds_ref):
        step = pl.program_id(0)
        pouts = (dw0_ref, dw2_ref, da0_ref, da2_ref, dkk_ref, dka_ref)

        @pl.when(step == 0)
        def _():
            ds_ref[...] = jnp.zeros_like(ds_ref)
            for ref in pouts:
                ref[...] = jnp.zeros_like(ref)

        wl = wl_ref[...][:, d * RW_LORA:(d + 1) * RW_LORA]
        al = al_ref[...][:, d * RW_LORA:(d + 1) * RW_LORA]
        _, vjp = jax.vjp(functools.partial(_rw_step, rev=rev),
                         st_ref[...], _to_heads(r_ref[...], nh), _to_heads(k_ref[...], nh), _to_heads(v_ref[...], nh),
                         wl, al, w0_ref[...], w2_ref[...], a0_ref[...], a2_ref[...], kk_ref[...], ka_ref[...])
        g = vjp((_to_heads(dy_ref[...], nh), ds_ref[...]))
        ds_ref[...] = g[0]
        dm_ref[...] = jnp.concatenate([_from_heads(g[1]), _from_heads(g[2]), _from_heads(g[3])], axis=-1)
        zero = jnp.zeros_like(g[4])
        parts = [zero] * 4
        parts[d], parts[2 + d] = g[4], g[5]
        dl_ref[...] = jnp.concatenate(parts, axis=-1)
        for ref, val in zip(pouts, g[6:]):
            ref[...] += val

    def rows(cb, width=w):
        return pl.BlockSpec((STEP, width), lambda s: (_scan_order(n - 1 - s, n_ctx, n, rev), cb))

    def whole(a):
        nd = a.ndim
        return pl.BlockSpec(a.shape, lambda s: (0,) * nd)

    return pl.pallas_call(
        body, name=name, grid=(n,),
        in_specs=[rows(0), rows(1), rows(2), rows(lo, LANE), rows(lo + 1, LANE)] + [whole(a) for a in hp] +
                 [pl.BlockSpec((None, nh, RW_HEAD, RW_HEAD), lambda s: (n - 1 - s, 0, 0, 0)), rows(0)],
        out_specs=[rows(0, 3 * w), rows(0, 4 * RW_LORA)] + [whole(a) for a in hp],
        out_shape=[jax.ShapeDtypeStruct((t, 3 * w), F32), jax.ShapeDtypeStruct((t, 4 * RW_LORA), F32)] +
                  [jax.ShapeDtypeStruct(a.shape, F32) for a in hp],
        scratch_shapes=[pltpu.VMEM((nh, RW_HEAD, RW_HEAD), F32)],
        compiler_params=_params(("arbitrary",)),
    )(sh, sh, sh, sh, sh, *hp, states, dy)


def _shift_masks(t, n_ctx_rows):
    row = lax.broadcasted_iota(jnp.int32, (t, 1), 0)
    isx = row >= n_ctx_rows
    pos = jnp.where(isx, row - n_ctx_rows, row)
    col = jnp.where(isx, jnp.bitwise_and(pos, GRID_W - 1), pos)
    ncol = jnp.where(isx, GRID_W, n_ctx_rows)
    n_x = t - n_ctx_rows
    ml = col != 0
    mr = col != ncol - 1
    mu = isx & (pos >= GRID_W)
    md = isx & (pos < n_x - GRID_W)
    return ml, mr, mu, md, isx


def _shift_fwd(name, p, mu, n_ctx_rows):
    t, c = p.shape
    cw = LANE

    def body(p_ref, mu_ref, o_ref):
        x = p_ref[...]
        m = mu_ref[...]
        ml, mr, mup, mdn, isx = _shift_masks(t, n_ctx_rows)
        left = jnp.where(ml, pltpu.roll(x, 1, 0), 0.0)
        right = jnp.where(mr, pltpu.roll(x, t - 1, 0), 0.0)
        up = jnp.where(mup, pltpu.roll(x, GRID_W, 0), 0.0)
        down = jnp.where(mdn, pltpu.roll(x, t - GRID_W, 0), 0.0)
        out = x + m[0:1] * (left - x) + m[1:2] * (right - x)
        vert = m[2:3] * (up - x) + m[3:4] * (down - x)
        o_ref[...] = out + jnp.where(isx, vert, 0.0)

    return pl.pallas_call(
        body, name=name, grid=(c // cw,),
        in_specs=[pl.BlockSpec((t, cw), lambda j: (0, j)), pl.BlockSpec((4, cw), lambda j: (0, j))],
        out_specs=pl.BlockSpec((t, cw), lambda j: (0, j)),
        out_shape=jax.ShapeDtypeStruct((t, c), F32),
        compiler_params=_params(("parallel",)),
    )(p, mu)


def _shift_bwd(name, p, mu, dparts, n_ctx_rows):
    t, c = p.shape
    cw = LANE
    npart = len(dparts)

    def body(*refs):
        p_ref, mu_ref = refs[0], refs[1]
        dp_ref, dmu_ref = refs[2 + npart], refs[3 + npart]
        x = p_ref[...]
        m = mu_ref[...]
        g = refs[2][...]
        for r in refs[3:2 + npart]:
            g = g + r[...]
        ml, mr, mup, mdn, isx = _shift_masks(t, n_ctx_rows)
        left = jnp.where(ml, pltpu.roll(x, 1, 0), 0.0)
        right = jnp.where(mr, pltpu.roll(x, t - 1, 0), 0.0)
        up = jnp.where(mup, pltpu.roll(x, GRID_W, 0), 0.0)
        down = jnp.where(mdn, pltpu.roll(x, t - GRID_W, 0), 0.0)
        gx = jnp.where(isx, g, 0.0)
        dmu_ref[...] = jnp.concatenate([
            jnp.sum(g * (left - x), axis=0, keepdims=True), jnp.sum(g * (right - x), axis=0, keepdims=True),
            jnp.sum(gx * (up - x), axis=0, keepdims=True), jnp.sum(gx * (down - x), axis=0, keepdims=True)], axis=0)
        coef = 1.0 - m[0:1] - m[1:2] - jnp.where(isx, m[2:3] + m[3:4], 0.0)
        dp = coef * g
        dp = dp + m[0:1] * pltpu.roll(jnp.where(ml, g, 0.0), t - 1, 0)
        dp = dp + m[1:2] * pltpu.roll(jnp.where(mr, g, 0.0), 1, 0)
        dp = dp + m[2:3] * pltpu.roll(jnp.where(mup, g, 0.0), t - GRID_W, 0)
        dp = dp + m[3:4] * pltpu.roll(jnp.where(mdn, g, 0.0), GRID_W, 0)
        dp_ref[...] = dp

    col = pl.BlockSpec((t, cw), lambda j: (0, j))
    par = pl.BlockSpec((4, cw), lambda j: (0, j))
    return pl.pallas_call(
        body, name=name, grid=(c // cw,),
        in_specs=[col, par] + [col] * npart,
        out_specs=[col, par],
        out_shape=[jax.ShapeDtypeStruct((t, c), F32), jax.ShapeDtypeStruct((4, c), F32)],
        compiler_params=_params(("parallel",)),
    )(p, mu, *dparts)


def _local_step(x, c, ctx, c_ctx, ada_st, ada_b, norm_g, w_in_st, hg_lb, hg_norm_g, rw_mu, rw_w0, rw_w2, rw_a0, rw_a2,
                rw_kk, rw_ka, rw_rk, rw_gn_g, rw_gn_b, w_hg_st, w_rw_st, w_out, final_g, tgt):
    seq, dm = x.shape
    n_ctx_rows = ctx.shape[0]
    t = seq + n_ctx_rows
    hw = hg_norm_g.shape[-1]
    rw = rw_kk.shape[-1]
    nh_rw = rw // RW_HEAD
    n_ctx = n_ctx_rows // STEP
    tm = _tile(n_ctx_rows, (256, 128, 64))
    nt = t // tm
    nct = n_ctx_rows // tm
    n_sh_cols = 3 * rw + 4 * RW_LORA

    xs = jnp.concatenate([ctx, x], axis=0)
    cond = jnp.concatenate([c.reshape(1, dm), c_ctx.reshape(1, dm), jnp.zeros((6, dm), F32)], axis=0)
    final_g2 = final_g.reshape(1, dm)

    def unstack(a_st):
        return jnp.swapaxes(a_st, 0, 1).reshape(a_st.shape[1], -1)

    def restack(a, ns=N_SHARD):
        return jnp.swapaxes(a.reshape(a.shape[0], ns, -1), 0, 1)

    (sc,) = _row_call("cond_silu", lambda i, r, f: ([jax.nn.silu(r[0])], []), 1, 8, [(cond, 0, dm, 0)], [],
                      [(8, dm, F32, 0)], [])
    mod_st = _mm_n_st("mod_mm", sc, ada_st)
    mod = unstack(mod_st) + ada_b
    mod3 = mod.reshape(8, 3, dm)

    def pick(i, m3):
        r = jnp.where(i < nct, m3[1], m3[0])
        return r[0:1], r[1:2]

    def h_fn(i, r, f):
        shift, scale = pick(i, f[1])
        return [_fn_h(r[0], f[0], scale, shift)], []

    (h,) = _row_call("h_fwd", h_fn, nt, tm, [(xs, 0, dm, 0)], [norm_g, mod3], [(t, dm, BF16, 0)], [])
    proj = unstack(_mm_n_st("proj_mm", h, w_in_st))
    p_hg = proj[:, :5 * hw]
    p_rs = proj[:, 5 * hw:5 * hw + n_sh_cols]
    p_zr = proj[:, 5 * hw + n_sh_cols:5 * hw + n_sh_cols + rw]
    p_gt = proj[:, 5 * hw + n_sh_cols + rw:]

    o_hg, st_hg = [], []
    for d in range(2):
        o, st = _hg_scan_fwd(f"hg_scan_fwd{d}", p_hg, hg_lb[d], d, n_ctx)
        o_hg.append(o)
        st_hg.append(st)

    def hgpost_fn(i, r, f):
        return [_fn_hgpost(r[0], r[1], r[2], f[0])], []

    hg_in = [(o_hg[0], 0, hw, 0), (o_hg[1], 0, hw, 0), (p_hg, 4, hw, 0)]
    (y_hg,) = _row_call("hg_post", hgpost_fn, nt, tm, hg_in, [hg_norm_g], [(t, hw, BF16, 0)], [])

    sh = _shift_fwd("rw_shift", p_rs, rw_mu, n_ctx_rows)
    hps = []
    for d in range(2):
        hps.append([rw_w0[d].reshape(nh_rw, 1, RW_HEAD), jnp.swapaxes(rw_w2[d].reshape(RW_LORA, nh_rw, RW_HEAD), 0, 1),
                    rw_a0[d].reshape(nh_rw, 1, RW_HEAD), jnp.swapaxes(rw_a2[d].reshape(RW_LORA, nh_rw, RW_HEAD), 0, 1),
                    rw_kk.reshape(nh_rw, 1, RW_HEAD), rw_ka.reshape(nh_rw, 1, RW_HEAD)])
    y_rw_d, st_rw = [], []
    for d in range(2):
        y, st = _rw_scan_fwd(f"rw_scan_fwd{d}", sh, hps[d], d, n_ctx)
        y_rw_d.append(y)
        st_rw.append(st)

    rw_full = [rw_a0, rw_a2, rw_ka, rw_rk, rw_gn_g, rw_gn_b]
    lo = 3 * rw // LANE
    rw_in = [(y_rw_d[0], 0, rw, 0), (y_rw_d[1], 0, rw, 0), (sh, 0, rw, 0), (sh, 1, rw, 0), (sh, 2, rw, 0),
             (sh, lo + 1, LANE, 0), (p_zr, 0, rw, 0)]

    def rwpost_fn(i, r, f):
        return [_fn_rwpost(*r, *f)], []

    (y_rw,) = _row_call("rw_post", rwpost_fn, nt, tm, rw_in, rw_full, [(t, rw, BF16, 0)], [])

    a_hg = unstack(_mm_n_st("hg_out_mm", y_hg, w_hg_st))
    a_rw = unstack(_mm_n_st("rw_out_mm", y_rw, w_rw_st))
    mg_in = [(a_hg, 0, dm, 0), (a_rw, 0, dm, 0), (p_gt, 0, dm, 0), (p_gt, 1, dm, 0)]
    (merged,) = _row_call("merge", lambda i, r, f: ([_fn_merge(*r)], []), nt, tm, mg_in, [], [(t, dm, BF16, 0)], [])
    o_out = _mm_nn("out_mm", merged, w_out)

    def final_fn(i, r, f):
        gate = f[0][0][2:3]
        loss, vjp = jax.vjp(_fn_final, r[0], r[1], gate, f[1], r[2])
        dx, do, dgate, dfg, _ = vjp(jnp.ones((), F32))
        live = i >= nct
        zero = lambda a: jnp.where(live, a, 0.0)
        dmod = jnp.concatenate([jnp.concatenate([jnp.zeros((1, 2 * dm), F32), zero(dgate)], axis=1),
                                jnp.zeros((7, 3 * dm), F32)], axis=0)
        return [zero(dx), zero(do)], [jnp.broadcast_to(zero(loss), (8, LANE)), dmod, zero(dfg)]

    fin_in = [(xs, 0, dm, 0), (o_out, 0, dm, 0), (tgt, 0, dm, nct)]
    dx_res, d_o, loss_acc, dmod_gate, d_final_g = _row_call(
        "final", final_fn, nt, tm, fin_in, [mod3, final_g2], [(t, dm, F32, 0), (t, dm, BF16, 0)],
        [((8, LANE), F32), ((8, 3 * dm), F32), ((1, dm), F32)])

    g_w_out = _mm_tn("d_w_out", merged, d_o)
    d_merged = _mm_nt("d_merged", d_o, w_out)

    def merge_bwd(i, r, f):
        _, vjp = jax.vjp(_fn_merge, r[0], r[1], r[2], r[3])
        da, db, dgh, dgr = vjp(r[4])
        return [da, db, jnp.concatenate([dgh, dgr], axis=1)], []

    da_hg, da_rw, dp_gt = _row_call("merge_bwd", merge_bwd, nt, tm, mg_in + [(d_merged, 0, dm, 0)], [],
                                    [(t, dm, BF16, 0), (t, dm, BF16, 0), (t, 2 * dm, F32, 0)], [])
    g_w_hg_st = _mm_t_st("d_w_hg", y_hg, restack(da_hg))
    g_w_rw_st = _mm_t_st("d_w_rw", y_rw, restack(da_rw))
    dy_hg = _mm_st_t("d_y_hg", restack(da_hg), w_hg_st)
    dy_rw = _mm_st_t("d_y_rw", restack(da_rw), w_rw_st)

    def hgpost_bwd(i, r, f):
        _, vjp = jax.vjp(_fn_hgpost, r[0], r[1], r[2], f[0])
        dof, _, dz, dg = vjp(r[3])
        return [dof, dz], [dg]

    do_hg, dz_hg, g_hg_norm = _row_call("hg_post_bwd", hgpost_bwd, nt, tm, hg_in + [(dy_hg, 0, hw, 0)], [hg_norm_g],
                                        [(t, hw, F32, 0), (t, hw, F32, 0)], [((1, hw), F32)])
    dqs, dis, dfs, g_lb = [], [], [], []
    for d in range(2):
        dq, di, df, dlb = _hg_scan_bwd(f"hg_scan_bwd{d}", p_hg, hg_lb[d], st_hg[d], do_hg, d, n_ctx)
        dqs.append(dq)
        dis.append(di)
        dfs.append(df)
        g_lb.append(dlb)
    (dqi,) = _row_call("hg_dsum", lambda i, r, f: ([jnp.concatenate([r[0] + r[1], r[2] + r[3]], axis=1)], []), nt, tm,
                       [(dqs[0], 0, hw, 0), (dqs[1], 0, hw, 0), (dis[0], 0, hw, 0), (dis[1], 0, hw, 0)], [],
                       [(t, 2 * hw, F32, 0)], [])
    g_hg_lb = jnp.stack(g_lb, axis=0)

    def rwpost_bwd(i, r, f):
        _, vjp = jax.vjp(_fn_rwpost, *r[:7], *f)
        g = vjp(r[7])
        zl = jnp.zeros((g[5].shape[0], 2 * RW_LORA), F32)
        dmain = jnp.concatenate([g[2], g[3], g[4]], axis=1)
        return [g[0], dmain, jnp.concatenate([zl, g[5]], axis=1), g[6]], list(g[7:])

    dy_sum, dsh_p, dsl_p, dz_rw, g_a0_p, g_a2_p, g_ka_p, g_rk, g_gn_g, g_gn_b = _row_call(
        "rw_post_bwd", rwpost_bwd, nt, tm, rw_in + [(dy_rw, 0, rw, 0)], rw_full,
        [(t, rw, F32, 0), (t, 3 * rw, F32, 0), (t, 4 * RW_LORA, F32, 0), (t, rw, F32, 0)],
        [(a.shape, F32) for a in rw_full])
    dmains, dloras, hp_grads = [dsh_p], [dsl_p], []
    for d in range(2):
        res = _rw_scan_bwd(f"rw_scan_bwd{d}", sh, hps[d], st_rw[d], dy_sum, d, n_ctx)
        dmains.append(res[0])
        dloras.append(res[1])
        hp_grads.append(res[2:])
    dsh_parts = [jnp.concatenate([m, l], axis=1) for m, l in zip(dmains, dloras)]
    dp_rs, g_mu = _shift_bwd("rw_shift_bwd", p_rs, rw_mu, dsh_parts, n_ctx_rows)

    def flat(a):
        if a.shape[1] == 1:
            return a.reshape(rw)
        return jnp.swapaxes(a, 0, 1).reshape(RW_LORA, rw)

    g_w0 = jnp.stack([flat(hp_grads[d][0]) for d in range(2)], axis=0)
    g_w2 = jnp.stack([flat(hp_grads[d][1]) for d in range(2)], axis=0)
    g_a0 = jnp.stack([flat(hp_grads[d][2]) for d in range(2)], axis=0) + g_a0_p
    g_a2 = jnp.stack([flat(hp_grads[d][3]) for d in range(2)], axis=0) + g_a2_p
    g_kk = (flat(hp_grads[0][4]) + flat(hp_grads[1][4])).reshape(1, rw)
    g_ka = (flat(hp_grads[0][5]) + flat(hp_grads[1][5])).reshape(1, rw) + g_ka_p

    dproj = jnp.concatenate([dqi, dfs[0], dfs[1], dz_hg, dp_rs, dz_rw, dp_gt], axis=1).astype(BF16)
    dproj_st = restack(dproj)
    g_w_in_st = _mm_t_st("d_w_in", h, dproj_st)
    dh = _mm_st_t("d_h", dproj_st, w_in_st)

    def h_bwd(i, r, f):
        shift, scale = pick(i, f[1])
        _, vjp = jax.vjp(_fn_h, r[0], f[0], scale, shift)
        ds, dg, dscale, dshift = vjp(r[1])
        row = jnp.concatenate([dshift, dscale, jnp.zeros((1, dm), F32)], axis=1)
        z = jnp.zeros_like(row)
        is_ctx = i < nct
        dmod = jnp.concatenate([jnp.where(is_ctx, z, row), jnp.where(is_ctx, row, z), jnp.zeros((6, 3 * dm), F32)], axis=0)
        return [ds + r[2]], [dg, dmod]

    grad_x, g_norm_g, dmod_h = _row_call(
        "h_bwd", h_bwd, nt, tm, [(xs, 0, dm, 0), (dh, 0, dm, 0), (dx_res, 0, dm, 0)], [norm_g, mod3],
        [(seq, dm, F32, nct)], [((1, dm), F32), ((8, 3 * dm), F32)])
    dmod = dmod_h + dmod_gate
    g_ada_b = (dmod[0] + dmod[1]).reshape(1, 3 * dm)
    g_ada_st = _mm_t_st("d_ada_w", sc, restack(dmod))
    d_sc = _mm_st_t("d_cond", restack(dmod), ada_st)

    def cond_bwd(i, r, f):
        _, vjp = jax.vjp(jax.nn.silu, r[0])
        return [vjp(r[1])[0]], []

    (d_cond,) = _row_call("cond_bwd", cond_bwd, 1, 8, [(cond, 0, dm, 0), (d_sc, 0, dm, 0)], [], [(8, dm, F32, 0)], [])

    grads = dict(
        c_ctx=d_cond[1], ada_w=g_ada_st, ada_b=g_ada_b, norm_g=g_norm_g, w_in=g_w_in_st, hg_lb=g_hg_lb,
        hg_norm_g=g_hg_norm, rw_mu=g_mu, rw_w0=g_w0, rw_w2=g_w2, rw_a0=g_a0, rw_a2=g_a2, rw_kk=g_kk, rw_ka=g_ka,
        rw_rk=g_rk, rw_gn_g=g_gn_g, rw_gn_b=g_gn_b, w_hg_out=g_w_hg_st, w_rw_out=g_w_rw_st, w_out=g_w_out,
        final_g=d_final_g.reshape(dm))
    return loss_acc[0:1, 0:1], grad_x, grads


def _my_place():
    return lax.axis_index("x"), lax.axis_index("y"), lax.axis_index("c")


MIN_CHUNK_BYTES = 1 << 18
ROW_ALIGN = 16


def _n_chunks(rows, row_bytes):
    for n in (8, 4, 2):
        if rows % (n * ROW_ALIGN) == 0 and rows // n * row_bytes >= MIN_CHUNK_BYTES:
            return n
    return 1


def _row_bytes(a, lead=1):
    n = a.dtype.itemsize
    for d in a.shape[lead:]:
        n *= d
    return n


def _rows(ref, start, size):
    return ref.at[pl.ds(start, size)]


def _chunked(make, start, size, n):
    cs = size // n
    return [make(start + j * cs, cs) for j in range(n)]


_PEER_CHIPS = 3


def _weights_gather(name, big, small):
    nb, na = len(big), len(big) + len(small)
    arrays = list(big) + list(small)

    def body(*refs):
        ins, outs = refs[:na], refs[na:2 * na]
        send_sems, recv_sems, fsend_sems, frecv_sems, local_sems = refs[2 * na:]
        x, y, c = _my_place()
        me = 2 * x + y
        chips = [(1 - x, y), (x, 1 - y), (1 - x, 1 - y)]

        def over_ici(a, k, slot, r0, nr):
            px, py = chips[k]
            return pltpu.make_async_remote_copy(
                src_ref=_rows(ins[a], r0, nr), dst_ref=_rows(outs[a].at[slot], r0, nr), send_sem=send_sems.at[a, k],
                recv_sem=recv_sems.at[a, k], device_id=(px, py, c), device_id_type=pl.DeviceIdType.MESH)

        def to_sibling(a, k, r0, nr):
            px, py = chips[k]
            rows = _rows(outs[a].at[2 * px + py], r0, nr)
            return pltpu.make_async_remote_copy(
                src_ref=rows, dst_ref=rows, send_sem=fsend_sems.at[a, k], recv_sem=frecv_sems.at[a, k],
                device_id=(x, y, 1 - c), device_id_type=pl.DeviceIdType.MESH)

        local = [pltpu.make_async_copy(ins[a], outs[a].at[me], local_sems.at[a]) for a in range(na)]
        for cp in local:
            cp.start()
        span = []
        for a in range(na):
            rows = arrays[a].shape[0]
            if a < nb:
                half = rows // 2
                span.append((pl.multiple_of(c * half, ROW_ALIGN), pl.multiple_of((1 - c) * half, ROW_ALIGN), half,
                             _n_chunks(half, _row_bytes(arrays[a]))))
            else:
                span.append((0, 0, rows, 1))
        for a in range(na):
            mine, _, nr, n = span[a]
            for k in range(_PEER_CHIPS):
                for cp in _chunked(lambda r0, cs: over_ici(a, k, me, r0, cs), mine, nr, n):
                    cp.start()
        for k in range(_PEER_CHIPS):
            px, py = chips[k]
            for a in range(na):
                mine, _, nr, n = span[a]
                over_ici(a, k, 2 * px + py, mine, nr).wait_recv()
                if a < nb:
                    for cp in _chunked(lambda r0, cs: to_sibling(a, k, r0, cs), mine, nr, n):
                        cp.start()
        for k in range(_PEER_CHIPS):
            for a in range(nb):
                _, theirs, nr, _ = span[a]
                to_sibling(a, k, theirs, nr).wait_recv()
        for a in range(na):
            mine, _, nr, _ = span[a]
            for k in range(_PEER_CHIPS):
                over_ici(a, k, me, mine, nr).wait_send()
                if a < nb:
                    to_sibling(a, k, mine, nr).wait_send()
        for cp in local:
            cp.wait()

    hbm = pl.BlockSpec(memory_space=pl.ANY)
    sems = pltpu.SemaphoreType.DMA((na, _PEER_CHIPS))
    return pl.pallas_call(
        body, name=name, in_specs=[hbm] * na, out_specs=[hbm] * na,
        out_shape=[jax.ShapeDtypeStruct((N_SHARD,) + a.shape, a.dtype) for a in arrays],
        scratch_shapes=[sems, sems, sems, sems, pltpu.SemaphoreType.DMA((na,))],
    )(*arrays)


def _chip_scatter(name, arrays):
    na = len(arrays)

    def body(*refs):
        ins, outs = refs[:na], refs[na:2 * na]
        send_sems, recv_sems, local_sems = refs[2 * na:]
        x, y, c = _my_place()
        me = 2 * x + y
        chips = [(1 - x, y), (x, 1 - y), (1 - x, 1 - y)]

        def remote(a, k, slot, r0, nr):
            px, py = chips[k]
            return pltpu.make_async_remote_copy(
                src_ref=_rows(ins[a].at[2 * px + py], r0, nr), dst_ref=_rows(outs[a].at[slot], r0, nr),
                send_sem=send_sems.at[a, k], recv_sem=recv_sems.at[a, k], device_id=(px, py, c),
                device_id_type=pl.DeviceIdType.MESH)

        local = [pltpu.make_async_copy(ins[a].at[me], outs[a].at[me], local_sems.at[a]) for a in range(na)]
        for cp in local:
            cp.start()
        for a in range(na):
            rows = arrays[a].shape[1]
            for k in range(_PEER_CHIPS):
                for cp in _chunked(lambda r0, cs: remote(a, k, me, r0, cs), 0, rows, _n_chunks(rows, _row_bytes(arrays[a], 2))):
                    cp.start()
        for k in range(_PEER_CHIPS):
            px, py = chips[k]
            for a in range(na):
                remote(a, k, 2 * px + py, 0, arrays[a].shape[1]).wait_recv()
        for a in range(na):
            for k in range(_PEER_CHIPS):
                remote(a, k, me, 0, arrays[a].shape[1]).wait_send()
        for cp in local:
            cp.wait()

    hbm = pl.BlockSpec(memory_space=pl.ANY)
    sems = pltpu.SemaphoreType.DMA((na, _PEER_CHIPS))
    return pl.pallas_call(
        body, name=name, in_specs=[hbm] * na, out_specs=[hbm] * na,
        out_shape=[jax.ShapeDtypeStruct(a.shape, a.dtype) for a in arrays],
        scratch_shapes=[sems, sems, pltpu.SemaphoreType.DMA((na,))],
    )(*arrays)


def _pair_split(name, arrays):
    na = len(arrays)

    def body(*refs):
        ins, mine, theirs = refs[:na], refs[na:2 * na], refs[2 * na:3 * na]
        send_sems, recv_sems, local_sems = refs[3 * na:]
        x, y, c = _my_place()

        def remote(a, j, r0, nr, base):
            return pltpu.make_async_remote_copy(
                src_ref=_rows(ins[a].at[j], base + r0, nr), dst_ref=_rows(theirs[a].at[j], r0, nr),
                send_sem=send_sems.at[a], recv_sem=recv_sems.at[a], device_id=(x, y, 1 - c),
                device_id_type=pl.DeviceIdType.MESH)

        keep = []
        for a in range(na):
            half = arrays[a].shape[1] // 2
            n = _n_chunks(half, _row_bytes(arrays[a], 2))
            kept = pl.multiple_of(c * half, ROW_ALIGN)
            sent = pl.multiple_of((1 - c) * half, ROW_ALIGN)
            for j in range(N_SHARD):
                for cp in _chunked(lambda r0, cs: remote(a, j, r0, cs, sent), 0, half, n):
                    cp.start()
                keep += _chunked(lambda r0, cs: pltpu.make_async_copy(
                    _rows(ins[a].at[j], kept + r0, cs), _rows(mine[a].at[j], r0, cs), local_sems.at[a]), 0, half, n)
        for cp in keep:
            cp.start()
        for a in range(na):
            whole = pltpu.make_async_remote_copy(
                src_ref=mine[a], dst_ref=theirs[a], send_sem=send_sems.at[a], recv_sem=recv_sems.at[a],
                device_id=(x, y, 1 - c), device_id_type=pl.DeviceIdType.MESH)
            whole.wait_recv()
            whole.wait_send()
            pltpu.make_async_copy(theirs[a], mine[a], local_sems.at[a]).wait()

    def half_shape(a):
        return jax.ShapeDtypeStruct((a.shape[0], a.shape[1] // 2) + a.shape[2:], a.dtype)

    hbm = pl.BlockSpec(memory_space=pl.ANY)
    sems = pltpu.SemaphoreType.DMA((na,))
    res = pl.pallas_call(
        body, name=name, in_specs=[hbm] * na, out_specs=[hbm] * (2 * na),
        out_shape=[half_shape(a) for a in arrays] * 2, scratch_shapes=[sems, sems, sems],
    )(*arrays)
    return res[:na], res[na:]


def _pair_join(name, arrays):
    na = len(arrays)

    def body(*refs):
        ins, outs = refs[:na], refs[na:2 * na]
        send_sems, recv_sems, local_sems = refs[2 * na:]
        x, y, c = _my_place()

        def remote(a, slot, r0, nr):
            return pltpu.make_async_remote_copy(
                src_ref=_rows(ins[a], r0, nr), dst_ref=_rows(outs[a].at[slot], r0, nr), send_sem=send_sems.at[a],
                recv_sem=recv_sems.at[a], device_id=(x, y, 1 - c), device_id_type=pl.DeviceIdType.MESH)

        local = [pltpu.make_async_copy(ins[a], outs[a].at[c], local_sems.at[a]) for a in range(na)]
        for cp in local:
            cp.start()
        for a in range(na):
            rows = arrays[a].shape[0]
            for cp in _chunked(lambda r0, cs: remote(a, c, r0, cs), 0, rows, _n_chunks(rows, _row_bytes(arrays[a]))):
                cp.start()
        for a in range(na):
            remote(a, 1 - c, 0, arrays[a].shape[0]).wait_recv()
        for a in range(na):
            remote(a, c, 0, arrays[a].shape[0]).wait_send()
        for cp in local:
            cp.wait()

    hbm = pl.BlockSpec(memory_space=pl.ANY)
    sems = pltpu.SemaphoreType.DMA((na,))
    return pl.pallas_call(
        body, name=name, in_specs=[hbm] * na, out_specs=[hbm] * na,
        out_shape=[jax.ShapeDtypeStruct((2,) + a.shape, a.dtype) for a in arrays], scratch_shapes=[sems, sems, sems],
    )(*arrays)


def _gather_all(name, a):
    def body(in_ref, out_ref, send_sems, recv_sems, local_sem):
        x, y, c = _my_place()
        me = 4 * x + 2 * y + c

        def peer(k):
            return (x ^ (k >> 2), y ^ ((k >> 1) & 1), c ^ (k & 1))

        def remote(k, land):
            return pltpu.make_async_remote_copy(
                src_ref=in_ref, dst_ref=out_ref.at[land], send_sem=send_sems.at[k - 1], recv_sem=recv_sems.at[k - 1],
                device_id=peer(k), device_id_type=pl.DeviceIdType.MESH)

        local = pltpu.make_async_copy(in_ref, out_ref.at[me], local_sem)
        local.start()
        for k in range(1, N_DEV):
            remote(k, me).start()
        for k in range(1, N_DEV):
            px, py, pc = peer(k)
            remote(k, 4 * px + 2 * py + pc).wait_recv()
        for k in range(1, N_DEV):
            remote(k, me).wait_send()
        local.wait()

    hbm = pl.BlockSpec(memory_space=pl.ANY)
    return pl.pallas_call(
        body, name=name, in_specs=[hbm], out_specs=hbm,
        out_shape=jax.ShapeDtypeStruct((N_DEV,) + a.shape, a.dtype),
        scratch_shapes=[pltpu.SemaphoreType.DMA((N_DEV - 1,)), pltpu.SemaphoreType.DMA((N_DEV - 1,)), pltpu.SemaphoreType.DMA],
    )(a)


def _row_tile_for(rows, cols, budget=1 << 20):
    for tm in (1024, 512, 256, 128, 64, 32, 16, 8):
        if rows % tm == 0 and tm * cols * 4 <= budget:
            return tm
    return rows


def _slot_sum(vals):
    g = vals[0]
    for v in vals[1:]:
        g = g + v
    return g


def _rowwise(name, fn, arrays, out_dtype):
    rows, cols = arrays[0].shape
    tm = _row_tile_for(rows, cols)

    def body(*refs):
        refs[-1][...] = fn(*[r[...] for r in refs[:-1]]).astype(out_dtype)

    blk = pl.BlockSpec((tm, cols), lambda i: (i, 0))
    return pl.pallas_call(
        body, name=name, grid=(rows // tm,), in_specs=[blk] * len(arrays), out_specs=blk,
        out_shape=jax.ShapeDtypeStruct((rows, cols), out_dtype), compiler_params=_params(("parallel",)),
    )(*arrays)


def _sum_slots(name, st):
    ns, rows, cols = st.shape
    tm = _row_tile_for(rows, cols)

    def body(s_ref, o_ref):
        o_ref[...] = _slot_sum([s_ref[j].astype(F32) for j in range(ns)])

    return pl.pallas_call(
        body, name=name, grid=(rows // tm,),
        in_specs=[pl.BlockSpec((ns, tm, cols), lambda i: (0, i, 0))],
        out_specs=pl.BlockSpec((tm, cols), lambda i: (i, 0)),
        out_shape=jax.ShapeDtypeStruct((rows, cols), F32),
        compiler_params=_params(("parallel",)),
    )(st)


def _adamw(name, p, m, v, gst):
    rows, cols = p.shape
    ns = gst.shape[0]
    tm = _row_tile_for(rows, cols, budget=(1 << 20) // 2)

    def body(p_ref, m_ref, v_ref, g_ref, go_ref, d_ref, mo_ref, vo_ref):
        g = _slot_sum([g_ref[j] for j in range(ns)])
        mn = ADAM_B1 * m_ref[...] + (1.0 - ADAM_B1) * g
        vn = ADAM_B2 * v_ref[...] + (1.0 - ADAM_B2) * jnp.square(g)
        m_hat = mn / (1.0 - ADAM_B1 ** ADAM_STEP)
        v_hat = vn / (1.0 - ADAM_B2 ** ADAM_STEP)
        go_ref[...] = g
        d_ref[...] = -ADAM_LR * (m_hat / (jnp.sqrt(v_hat) + ADAM_EPS) + ADAM_WD * p_ref[...])
        mo_ref[...] = mn
        vo_ref[...] = vn

    blk = pl.BlockSpec((tm, cols), lambda i: (i, 0))
    return pl.pallas_call(
        body, name=name, grid=(rows // tm,),
        in_specs=[blk, blk, blk, pl.BlockSpec((ns, tm, cols), lambda i: (0, i, 0))],
        out_specs=[blk] * 4, out_shape=[jax.ShapeDtypeStruct((rows, cols), F32)] * 4,
        compiler_params=_params(("parallel",)),
    )(p, m, v, gst)


def _pack(parts, width=LANE, mult=8):
    flat = jnp.concatenate([a.reshape(-1) for a in parts])
    n = flat.shape[0]
    per = width * mult
    total = -(-n // per) * per
    return jnp.pad(flat, (0, total - n)).reshape(total // width, width)


def _unpack(packed, shapes):
    flat = packed.reshape(-1)
    out, off = [], 0
    for s in shapes:
        n = 1
        for d in s:
            n *= d
        out.append(flat[off:off + n].reshape(s))
        off += n
    return out


_SMALL_SHARDED = ("hg_lb", "rw_mu", "rw_w0", "rw_w2", "rw_a0", "rw_a2")
_REPLICATED = ("c_ctx", "ada_b", "norm_g", "hg_norm_g", "rw_kk", "rw_ka", "rw_rk", "rw_gn_g", "rw_gn_b", "final_g")
_BIG = ("ada_w", "w_in", "w_hg_out", "w_rw_out", "w_out")
_WEIGHTS = ("c_ctx", "ada_w", "ada_b", "norm_g", "w_in", "hg_lb", "hg_norm_g", "rw_mu", "rw_w0", "rw_w2", "rw_a0", "rw_a2",
            "rw_kk", "rw_ka", "rw_rk", "rw_gn_g", "rw_gn_b", "w_hg_out", "w_rw_out", "w_out", "final_g")


def _join_shards(st):
    a = jnp.moveaxis(st, 0, -2)
    return a.reshape(a.shape[:-2] + (a.shape[-2] * a.shape[-1],))


def _split_shards(a):
    s = a.reshape(a.shape[:-1] + (N_SHARD, a.shape[-1] // N_SHARD))
    return jnp.moveaxis(s, -2, 0)


def kernel(x, c, ctx, c_ctx, ada_w, ada_b, norm_g, w_in, hg_lb, hg_norm_g, rw_mu, rw_w0, rw_w2, rw_a0, rw_a2, rw_kk, rw_ka, rw_rk, rw_gn_g, rw_gn_b, w_hg_out, w_rw_out, w_out, final_g, loss_target, m_c_ctx, m_ada_w, m_ada_b, m_norm_g, m_w_in, m_hg_lb, m_hg_norm_g, m_rw_mu, m_rw_w0, m_rw_w2, m_rw_a0, m_rw_a2, m_rw_kk, m_rw_ka, m_rw_rk, m_rw_gn_g, m_rw_gn_b, m_w_hg_out, m_w_rw_out, m_w_out, m_final_g, v_c_ctx, v_ada_w, v_ada_b, v_norm_g, v_w_in, v_hg_lb, v_hg_norm_g, v_rw_mu, v_rw_w0, v_rw_w2, v_rw_a0, v_rw_a2, v_rw_kk, v_rw_ka, v_rw_rk, v_rw_gn_g, v_rw_gn_b, v_w_hg_out, v_w_rw_out, v_w_out, v_final_g):
    w = dict(c_ctx=c_ctx, ada_w=ada_w, ada_b=ada_b, norm_g=norm_g, w_in=w_in, hg_lb=hg_lb, hg_norm_g=hg_norm_g, rw_mu=rw_mu,
             rw_w0=rw_w0, rw_w2=rw_w2, rw_a0=rw_a0, rw_a2=rw_a2, rw_kk=rw_kk, rw_ka=rw_ka, rw_rk=rw_rk, rw_gn_g=rw_gn_g,
             rw_gn_b=rw_gn_b, w_hg_out=w_hg_out, w_rw_out=w_rw_out, w_out=w_out, final_g=final_g)
    m = dict(c_ctx=m_c_ctx, ada_w=m_ada_w, ada_b=m_ada_b, norm_g=m_norm_g, w_in=m_w_in, hg_lb=m_hg_lb, hg_norm_g=m_hg_norm_g,
             rw_mu=m_rw_mu, rw_w0=m_rw_w0, rw_w2=m_rw_w2, rw_a0=m_rw_a0, rw_a2=m_rw_a2, rw_kk=m_rw_kk, rw_ka=m_rw_ka,
             rw_rk=m_rw_rk, rw_gn_g=m_rw_gn_g, rw_gn_b=m_rw_gn_b, w_hg_out=m_w_hg_out, w_rw_out=m_w_rw_out, w_out=m_w_out,
             final_g=m_final_g)
    v = dict(c_ctx=v_c_ctx, ada_w=v_ada_w, ada_b=v_ada_b, norm_g=v_norm_g, w_in=v_w_in, hg_lb=v_hg_lb, hg_norm_g=v_hg_norm_g,
             rw_mu=v_rw_mu, rw_w0=v_rw_w0, rw_w2=v_rw_w2, rw_a0=v_rw_a0, rw_a2=v_rw_a2, rw_kk=v_rw_kk, rw_ka=v_rw_ka,
             rw_rk=v_rw_rk, rw_gn_g=v_rw_gn_g, rw_gn_b=v_rw_gn_b, w_hg_out=v_w_hg_out, w_rw_out=v_w_rw_out, w_out=v_w_out,
             final_g=v_final_g)

    def mat(a):
        return a.reshape(a.shape[-2], a.shape[-1])

    def pack_small(d):
        return _pack([d[n] for n in _SMALL_SHARDED], mult=2 * ROW_ALIGN)

    small_shapes = [w[n].shape for n in _SMALL_SHARDED]
    big_bf = [_rowwise(f"to_bf16_{n}", lambda a: a, [mat(w[n])], BF16) for n in _BIG]
    gathered = _weights_gather("weights_gather", big_bf, [pack_small(w)])
    ada_st, w_in_st, w_hg_st, w_rw_st, w_out_st, small_st = gathered
    full_small = {}
    per_chip = [_unpack(small_st[j], small_shapes) for j in range(N_SHARD)]
    for i, n in enumerate(_SMALL_SHARDED):
        full_small[n] = _join_shards(jnp.stack([per_chip[j][i] for j in range(N_SHARD)], axis=0))
    dm = x.shape[-1]
    w_out_full = w_out_st.reshape(dm, dm)

    loss_b, grad_x, g = _local_step(
        x[0], c, ctx[0], c_ctx, ada_st, ada_b, norm_g, w_in_st, full_small["hg_lb"], hg_norm_g, full_small["rw_mu"][0],
        full_small["rw_w0"][0], full_small["rw_w2"][0], full_small["rw_a0"][0], full_small["rw_a2"][0], rw_kk, rw_ka, rw_rk,
        rw_gn_g, rw_gn_b, w_hg_st, w_rw_st, w_out_full, final_g, loss_target[0])
    loss = lax.psum(loss_b[0, 0], ("x", "y", "c"))

    g_small = {"hg_lb": g["hg_lb"], "rw_mu": g["rw_mu"][None], "rw_w0": g["rw_w0"][None], "rw_w2": g["rw_w2"][None],
               "rw_a0": g["rw_a0"][None], "rw_a2": g["rw_a2"][None]}
    split = {n: _split_shards(g_small[n]) for n in _SMALL_SHARDED}
    small_parts = jnp.stack([pack_small({n: split[n][j] for n in _SMALL_SHARDED}) for j in range(N_SHARD)], axis=0)
    partial = [g["ada_w"], g["w_in"], g["w_hg_out"], g["w_rw_out"], g["w_out"].reshape(N_SHARD, dm // N_SHARD, dm), small_parts]
    mine, theirs = _pair_split("grads_pair_split", partial)
    chip_sums = []
    for i, (a, b) in enumerate(zip(mine, theirs)):
        flat = (a.shape[0] * a.shape[1], a.shape[2])
        chip_sums.append(_rowwise(f"grads_pair_sum{i}", lambda p, q: p + q, [a.reshape(flat), b.reshape(flat)], BF16).reshape(a.shape))
    landed = _chip_scatter("grads_scatter", chip_sums)
    half_sums = [_sum_slots(f"grads_sum{i}", a) for i, a in enumerate(landed)]
    pairs = [p.reshape(1, 2 * p.shape[1], p.shape[2]) for p in _pair_join("grads_pair_join", half_sums)]
    rep_shapes = [w[n].shape for n in _REPLICATED]
    rep_all = _gather_all("grads_replicated", _pack([g[n].reshape(w[n].shape) for n in _REPLICATED]))

    res = {}
    for i, n in enumerate(_BIG):
        outs = _adamw(f"adamw_{n}", mat(w[n]), mat(m[n]), mat(v[n]), pairs[i])
        res[n] = [o.reshape(w[n].shape) for o in outs]
    outs = _adamw("adamw_small", pack_small(w), pack_small(m), pack_small(v), pairs[len(_BIG)])
    for i, vals in enumerate(zip(*[_unpack(o, small_shapes) for o in outs])):
        res[_SMALL_SHARDED[i]] = list(vals)
    outs = _adamw("adamw_replicated", _pack([w[n] for n in _REPLICATED]), _pack([m[n] for n in _REPLICATED]),
                  _pack([v[n] for n in _REPLICATED]), rep_all)
    for i, vals in enumerate(zip(*[_unpack(o, rep_shapes) for o in outs])):
        res[_REPLICATED[i]] = list(vals)

    return (loss, grad_x[None], *[res[n][0] for n in _WEIGHTS], *[res[n][1] for n in _WEIGHTS],
            *[res[n][2] for n in _WEIGHTS], *[res[n][3] for n in _WEIGHTS])
```

```python
import functools

import jax
import jax.numpy as jnp
from jax import lax
from jax.experimental import pallas as pl
from jax.experimental.pallas import tpu as pltpu

HI = lax.Precision.HIGHEST
F32 = jnp.float32
BF16 = jnp.bfloat16

NORM_EPS = 1e-6
HG_HEAD = 128
RW_HEAD = 64
RW_LORA = 64
RW_GN_EPS = 64e-5
GRID_W = 64
SUB = 16
STEP = 64
N_SHARD = 4
N_DEV = 8
LANE = 128

ADAM_LR = 0.001
ADAM_B1 = 0.9
ADAM_B2 = 0.999
ADAM_EPS = 1e-08
ADAM_WD = 0.01
ADAM_STEP = 10

VMEM_LIMIT = 56 * 1024 * 1024


def _params(sem=None):
    return pltpu.CompilerParams(dimension_semantics=sem, vmem_limit_bytes=VMEM_LIMIT)


def _tile(n, cands):
    for c in cands:
        if n % c == 0:
            return c
    return n


def _iota2(n, m, d):
    return lax.broadcasted_iota(jnp.int32, (n, m), d)


def _before(n, rev, strict):
    t, s = _iota2(n, n, 0), _iota2(n, n, 1)
    if rev:
        return (s > t) if strict else (s >= t)
    return (s < t) if strict else (s <= t)


def _bdot(a, b, spec):
    return jnp.einsum(spec, a, b, precision=HI, preferred_element_type=F32)


def _hg_step(s0, qraw, iin, fin, lb2, rev):
    c, w = qraw.shape
    h = w // HG_HEAD
    nsub = c // SUB
    lb = jax.nn.sigmoid(lb2[0:1] - lb2[1:2])
    q = jax.nn.silu(qraw)
    fg = lb + (1.0 - lb) * jax.nn.sigmoid(fin)
    kk = 1.0 - fg
    g = jnp.log(fg)
    incl = _before(SUB, rev, False).astype(F32)
    rows = lax.broadcasted_iota(jnp.int32, (SUB, 1), 0)
    last = 0 if rev else SUB - 1

    def heads(a):
        return jnp.swapaxes(a.reshape(a.shape[0], h, HG_HEAD), 0, 1)

    s = s0
    outs = [None] * nsub
    order = range(nsub - 1, -1, -1) if rev else range(nsub)
    for j in order:
        sl = slice(j * SUB, (j + 1) * SUB)
        qs, ks, vs, gs = q[sl], kk[sl], iin[sl], g[sl]
        bc = jnp.dot(incl, gs, precision=HI, preferred_element_type=F32)
        o = jnp.zeros((SUB, h, HG_HEAD), F32)
        for si in range(SUB):
            dec = jnp.exp(jnp.minimum(bc - bc[si:si + 1], 0.0))
            a = (qs * ks[si:si + 1] * dec).reshape(SUB, h, HG_HEAD).sum(-1)
            valid = (rows <= si) if rev else (rows >= si)
            a = jnp.where(valid, a, 0.0)
            o = o + a[:, :, None] * vs[si:si + 1].reshape(1, h, HG_HEAD)
        o = o.reshape(SUB, w)
        o = o + jnp.swapaxes(_bdot(heads(qs * jnp.exp(bc)), s, 'htk,hvk->htv'), 0, 1).reshape(SUB, w)
        blast = bc[last:last + 1]
        kdec = ks * jnp.exp(blast - bc)
        s = heads(jnp.exp(blast)) * s + _bdot(heads(vs), heads(kdec), 'hsv,hsk->hvk')
        outs[j] = o
    return jnp.concatenate(outs, axis=0), s


def _tri_solve(lmat, rhs, rev):
    hh, c, _ = lmat.shape
    nb = c // SUB
    eye = (_iota2(SUB, SUB, 0) == _iota2(SUB, SUB, 1)).astype(F32)
    rowid = lax.broadcasted_iota(jnp.int32, (1, SUB, 1), 1)
    diag = jnp.concatenate([lmat[:, i * SUB:(i + 1) * SUB, i * SUB:(i + 1) * SUB] for i in range(nb)], axis=0)
    tinv = jnp.broadcast_to(eye[None], diag.shape)
    order = range(SUB - 2, -1, -1) if rev else range(1, SUB)
    for t in order:
        row = eye[t:t + 1][None] - (diag[:, t, :][:, :, None] * tinv).sum(axis=1, keepdims=True)
        tinv = jnp.where(rowid == t, row, tinv)
    p = [None] * nb
    done = []
    for i in (range(nb - 1, -1, -1) if rev else range(nb)):
        r = rhs[:, i * SUB:(i + 1) * SUB]
        for m in done:
            r = r - _bdot(lmat[:, i * SUB:(i + 1) * SUB, m * SUB:(m + 1) * SUB], p[m], 'hts,hsv->htv')
        p[i] = _bdot(tinv[i * hh:(i + 1) * hh], r, 'hts,hsv->htv')
        done.append(i)
    return jnp.concatenate(p, axis=1)


def _rw_step(s0, r, k, v, wlo, alo, w0h, w2h, a0h, a2h, kkh, kah, rev):
    hh, c, _ = r.shape
    tl = jnp.broadcast_to(jnp.tanh(wlo)[None], (hh, c, wlo.shape[1]))
    al = jnp.broadcast_to(alo[None], (hh, c, alo.shape[1]))
    wlog = -jax.nn.softplus(-(w0h + _bdot(tl, w2h, 'hcl,hlj->hcj'))) - 0.5
    lw = -jnp.exp(wlog)
    a = jax.nn.sigmoid(a0h + _bdot(al, a2h, 'hcl,hlj->hcj'))
    kk = k * kkh
    kk = kk * lax.rsqrt(jnp.sum(kk * kk, axis=-1, keepdims=True) + 1e-12)
    kd = k * (1.0 + (a - 1.0) * kah)
    b = kk * a
    incl = jnp.broadcast_to(_before(c, rev, False).astype(F32)[None], (hh, c, c))
    cum = _bdot(incl, lw, 'hts,hsk->htk')
    ecum, encum = jnp.exp(cum), jnp.exp(-cum)
    alpha = jnp.exp(cum - lw) * kk
    beta = b * encum
    kappa = kd * encum
    rho = r * ecum
    m_lt = _before(c, rev, True)[None]
    m_le = _before(c, rev, False)[None]
    a_kap = jnp.where(m_lt, _bdot(alpha, kappa, 'htk,hsk->hts'), 0.0)
    a_bet = jnp.where(m_lt, _bdot(alpha, beta, 'htk,hsk->hts'), 0.0)
    b_kap = jnp.where(m_le, _bdot(rho, kappa, 'htk,hsk->hts'), 0.0)
    b_bet = jnp.where(m_le, _bdot(rho, beta, 'htk,hsk->hts'), 0.0)
    rhs = _bdot(alpha, s0, 'htk,hvk->htv') + _bdot(a_kap, v, 'hts,hsv->htv')
    p = _tri_solve(a_bet, rhs, rev)
    y = _bdot(rho, s0, 'htk,hvk->htv') + _bdot(b_kap, v, 'hts,hsv->htv') - _bdot(b_bet, p, 'hts,hsv->htv')
    stil = s0 + _bdot(v, kappa, 'hsv,hsk->hvk') - _bdot(p, beta, 'hsv,hsk->hvk')
    last = 0 if rev else c - 1
    return y, stil * ecum[:, last:last + 1, :]


def _fn_h(s, norm_g, scale, shift):
    return s * lax.rsqrt(jnp.mean(s * s, axis=-1, keepdims=True) + NORM_EPS) * norm_g * (1.0 + scale) + shift


def _fn_hgpost(of, ob, z, g):
    tm, w = of.shape
    o = (of + ob).reshape(tm, w // HG_HEAD, HG_HEAD)
    o = o * lax.rsqrt(jnp.mean(o * o, axis=-1, keepdims=True) + NORM_EPS)
    return o.reshape(tm, w) * g * jax.nn.silu(z)


def _fn_rwpost(y0, y1, r, k, v, alo, z, a0, a2, k_a, r_k, gn_g, gn_b):
    tm, w = r.shape
    nh = w // RW_HEAD
    asum = 0.0
    for d in range(2):
        asum = asum + jax.nn.sigmoid(a0[d:d + 1] + jnp.dot(alo[:, d * RW_LORA:(d + 1) * RW_LORA], a2[d],
                                                           precision=HI, preferred_element_type=F32))
    k_sum = k * (2.0 + (asum - 2.0) * k_a)
    ys = (y0 + y1).reshape(tm, nh, RW_HEAD)
    mean = jnp.mean(ys, axis=-1, keepdims=True)
    var = jnp.mean(jnp.square(ys - mean), axis=-1, keepdims=True)
    y = ((ys - mean) * lax.rsqrt(var + RW_GN_EPS)).reshape(tm, w) * gn_g + gn_b
    bonus = jnp.sum((r * k_sum * r_k).reshape(tm, nh, RW_HEAD), axis=-1, keepdims=True) * v.reshape(tm, nh, RW_HEAD)
    return (y + bonus.reshape(tm, w)) * jax.nn.silu(z)


def _fn_merge(a, b, ghg, grw):
    return jax.nn.sigmoid(ghg) * a + jax.nn.sigmoid(grw) * b


def _fn_final(xs, o, gate, final_g, tgt):
    x2 = xs + gate * o
    y = x2 * lax.rsqrt(jnp.mean(x2 * x2, axis=-1, keepdims=True) + NORM_EPS) * final_g
    return 0.5 * jnp.sum(jnp.mean(jnp.square(y - tgt), axis=-1))


def _row_call(name, fn, n_tiles, tm, row_ins, full_ins, row_outs, acc_outs):
    n_ri, n_fi, n_ro = len(row_ins), len(full_ins), len(row_outs)

    def body(*refs):
        i = pl.program_id(0)
        rvals = [r[...] for r in refs[:n_ri]]
        fvals = [r[...] for r in refs[n_ri:n_ri + n_fi]]
        outs = refs[n_ri + n_fi:]
        ro, ao = fn(i, rvals, fvals)
        for ref, val in zip(outs[:n_ro], ro):
            ref[...] = val.astype(ref.dtype)
        for ref, val in zip(outs[n_ro:], ao):
            @pl.when(i == 0)
            def _(ref=ref):
                ref[...] = jnp.zeros_like(ref)
            ref[...] += val.astype(ref.dtype)

    def rspec(width, cb, off):
        return pl.BlockSpec((tm, width), lambda i: (jnp.maximum(i - off, 0), cb))

    def fspec(shape):
        nd = len(shape)
        return pl.BlockSpec(shape, lambda i: (0,) * nd)

    in_specs = [rspec(w, cb, off) for (_, cb, w, off) in row_ins] + [fspec(a.shape) for a in full_ins]
    out_specs = [rspec(w, 0, off) for (_, w, _, off) in row_outs] + [fspec(s) for (s, _) in acc_outs]
    out_shape = [jax.ShapeDtypeStruct((rows, w), dt) for (rows, w, dt, _) in row_outs] + \
                [jax.ShapeDtypeStruct(s, dt) for (s, dt) in acc_outs]
    res = pl.pallas_call(
        body, name=name, grid=(n_tiles,), in_specs=in_specs, out_specs=out_specs, out_shape=out_shape,
        compiler_params=_params(("arbitrary",)),
    )(*[a for (a, _, _, _) in row_ins], *full_ins)
    return list(res)


def _mm(name, a, b, m, n, k_steps, tm, tn, a_block, a_map, b_block, b_map, o_shape, o_block, o_map,
        contract, out_dtype=F32):
    def body(a_ref, b_ref, o_ref, acc_ref):
        kk = pl.program_id(2)

        @pl.when(kk == 0)
        def _():
            acc_ref[...] = jnp.zeros_like(acc_ref)

        acc_ref[...] += lax.dot_general(a_ref[...].astype(BF16), b_ref[...].astype(BF16),
                                        (contract, ((), ())), preferred_element_type=F32)

        @pl.when(kk == k_steps - 1)
        def _():
            o_ref[...] = acc_ref[...].astype(o_ref.dtype)

    return pl.pallas_call(
        body, name=name, grid=(m // tm, n // tn, k_steps),
        in_specs=[pl.BlockSpec(a_block, a_map), pl.BlockSpec(b_block, b_map)],
        out_specs=pl.BlockSpec(o_block, o_map),
        out_shape=jax.ShapeDtypeStruct(o_shape, out_dtype),
        scratch_shapes=[pltpu.VMEM((tm, tn), F32)],
        compiler_params=_params(("parallel", "parallel", "arbitrary")),
    )(a, b)


_TM = (512, 256, 128, 64, 32, 16, 8)
_TN = (512, 256, 128)
_TK = (512, 256, 128)


def _mm_nn(name, a, b, out_dtype=F32):
    m, k = a.shape
    n = b.shape[1]
    tm, tn, tk = _tile(m, _TM), _tile(n, _TN), _tile(k, _TK)
    return _mm(name, a, b, m, n, k // tk, tm, tn, (tm, tk), lambda i, j, s: (i, s), (tk, tn), lambda i, j, s: (s, j),
               (m, n), (tm, tn), lambda i, j, s: (i, j), ((1,), (0,)), out_dtype)


def _mm_nt(name, a, b, out_dtype=F32):
    m, k = a.shape
    n = b.shape[0]
    tm, tn, tk = _tile(m, _TM), _tile(n, _TN), _tile(k, _TK)
    return _mm(name, a, b, m, n, k // tk, tm, tn, (tm, tk), lambda i, j, s: (i, s), (tn, tk), lambda i, j, s: (j, s),
               (m, n), (tm, tn), lambda i, j, s: (i, j), ((1,), (1,)), out_dtype)


def _mm_tn(name, a, b, out_dtype=F32):
    k, m = a.shape
    n = b.shape[1]
    tm, tn, tk = _tile(m, _TM), _tile(n, _TN), _tile(k, _TK)
    return _mm(name, a, b, m, n, k // tk, tm, tn, (tk, tm), lambda i, j, s: (s, i), (tk, tn), lambda i, j, s: (s, j),
               (m, n), (tm, tn), lambda i, j, s: (i, j), ((0,), (0,)), out_dtype)


def _mm_n_st(name, a, bst, out_dtype=F32):
    m, k = a.shape
    ns_, _, ns = bst.shape
    tm, tk = _tile(m, (256, 128, 64, 32, 16, 8)), _tile(k, (256, 128))
    return _mm(name, a, bst, m, ns_ * ns, k // tk, tm, ns,
               (tm, tk), lambda i, j, s: (i, s), (None, tk, ns), lambda i, j, s: (j, s, 0),
               (ns_, m, ns), (None, tm, ns), lambda i, j, s: (j, i, 0), ((1,), (0,)), out_dtype)


def _mm_st_t(name, ast, bst, out_dtype=F32):
    ns_, m, ns = ast.shape
    n = bst.shape[1]
    tm, tn = _tile(m, (256, 128, 64, 32, 16, 8)), _tile(n, (256, 128))
    return _mm(name, ast, bst, m, n, ns_, tm, tn,
               (None, tm, ns), lambda i, j, s: (s, i, 0), (None, tn, ns), lambda i, j, s: (s, j, 0),
               (m, n), (tm, tn), lambda i, j, s: (i, j), ((1,), (1,)), out_dtype)


def _mm_t_st(name, a, bst, out_dtype=F32):
    k, m = a.shape
    ns_, _, ns = bst.shape
    tm, tk = _tile(m, (256, 128, 64, 32, 16, 8)), _tile(k, (256, 128))
    return _mm(name, a, bst, m, ns_ * ns, k // tk, tm, ns,
               (tk, tm), lambda i, j, s: (s, i), (None, tk, ns), lambda i, j, s: (j, s, 0),
               (ns_, m, ns), (None, tm, ns), lambda i, j, s: (j, i, 0), ((0,), (0,)), out_dtype)


def _scan_order(j, n_ctx, n_all, rev):
    if not rev:
        return j
    return jnp.where(j < n_ctx, n_ctx - 1 - j, n_all - 1 - (j - n_ctx))


def _hg_scan_fwd(name, p_hg, lb2, d, n_ctx):
    t, w5 = p_hg.shape
    w = w5 // 5
    h = w // HG_HEAD
    n = t // STEP
    rev = d == 1

    def body(q_ref, i_ref, f_ref, lb_ref, o_ref, st_ref, s_ref):
        j = pl.program_id(0)

        @pl.when(j == 0)
        def _():
            s_ref[...] = jnp.zeros_like(s_ref)

        s0 = s_ref[...]
        st_ref[...] = s0
        o, s1 = _hg_step(s0, q_ref[...], i_ref[...], f_ref[...], lb_ref[...], rev)
        o_ref[...] = o
        s_ref[...] = s1

    def rows(cb):
        return pl.BlockSpec((STEP, w), lambda j: (_scan_order(j, n_ctx, n, rev), cb))

    return pl.pallas_call(
        body, name=name, grid=(n,),
        in_specs=[rows(0), rows(1), rows(2 + d), pl.BlockSpec((2, w), lambda j: (0, 0))],
        out_specs=[rows(0), pl.BlockSpec((None, h, HG_HEAD, HG_HEAD), lambda j: (j, 0, 0, 0))],
        out_shape=[jax.ShapeDtypeStruct((t, w), F32), jax.ShapeDtypeStruct((n, h, HG_HEAD, HG_HEAD), F32)],
        scratch_shapes=[pltpu.VMEM((h, HG_HEAD, HG_HEAD), F32)],
        compiler_params=_params(("arbitrary",)),
    )(p_hg, p_hg, p_hg, lb2)


def _hg_scan_bwd(name, p_hg, lb2, states, do, d, n_ctx):
    t, w5 = p_hg.shape
    w = w5 // 5
    h = w // HG_HEAD
    n = t // STEP
    rev = d == 1

    def body(q_ref, i_ref, f_ref, lb_ref, st_ref, do_ref, dq_ref, di_ref, df_ref, dlb_ref, ds_ref):
        step = pl.program_id(0)

        @pl.when(step == 0)
        def _():
            ds_ref[...] = jnp.zeros_like(ds_ref)
            dlb_ref[...] = jnp.zeros_like(dlb_ref)

        _, vjp = jax.vjp(lambda s0, q, i, f, lb: _hg_step(s0, q, i, f, lb, rev),
                         st_ref[...], q_ref[...], i_ref[...], f_ref[...], lb_ref[...])
        ds0, dq, di, df, dlb = vjp((do_ref[...], ds_ref[...]))
        dq_ref[...] = dq
        di_ref[...] = di
        df_ref[...] = df
        dlb_ref[...] += dlb
        ds_ref[...] = ds0

    def rows(cb):
        return pl.BlockSpec((STEP, w), lambda s: (_scan_order(n - 1 - s, n_ctx, n, rev), cb))

    return pl.pallas_call(
        body, name=name, grid=(n,),
        in_specs=[rows(0), rows(1), rows(2 + d), pl.BlockSpec((2, w), lambda s: (0, 0)),
                  pl.BlockSpec((None, h, HG_HEAD, HG_HEAD), lambda s: (n - 1 - s, 0, 0, 0)), rows(0)],
        out_specs=[rows(0), rows(0), rows(0), pl.BlockSpec((2, w), lambda s: (0, 0))],
        out_shape=[jax.ShapeDtypeStruct((t, w), F32)] * 3 + [jax.ShapeDtypeStruct((2, w), F32)],
        scratch_shapes=[pltpu.VMEM((h, HG_HEAD, HG_HEAD), F32)],
        compiler_params=_params(("arbitrary",)),
    )(p_hg, p_hg, p_hg, lb2, states, do)


def _to_heads(a, nh):
    return jnp.stack([a[:, i * RW_HEAD:(i + 1) * RW_HEAD] for i in range(nh)], axis=0)


def _from_heads(a):
    return jnp.concatenate([a[i] for i in range(a.shape[0])], axis=-1)


def _rw_scan_fwd(name, sh, hp, d, n_ctx):
    t = sh.shape[0]
    w = (sh.shape[1] - 4 * RW_LORA) // 3
    nh = w // RW_HEAD
    n = t // STEP
    rev = d == 1
    lo = 3 * w // LANE

    def body(r_ref, k_ref, v_ref, wl_ref, al_ref, w0_ref, w2_ref, a0_ref, a2_ref, kk_ref, ka_ref,
             y_ref, st_ref, s_ref):
        j = pl.program_id(0)

        @pl.when(j == 0)
        def _():
            s_ref[...] = jnp.zeros_like(s_ref)

        s0 = s_ref[...]
        st_ref[...] = s0
        wl = wl_ref[...][:, d * RW_LORA:(d + 1) * RW_LORA]
        al = al_ref[...][:, d * RW_LORA:(d + 1) * RW_LORA]
        y, s1 = _rw_step(s0, _to_heads(r_ref[...], nh), _to_heads(k_ref[...], nh), _to_heads(v_ref[...], nh), wl, al,
                         w0_ref[...], w2_ref[...], a0_ref[...], a2_ref[...], kk_ref[...], ka_ref[...], rev)
        y_ref[...] = _from_heads(y)
        s_ref[...] = s1

    def rows(cb, width=w):
        return pl.BlockSpec((STEP, width), lambda j: (_scan_order(j, n_ctx, n, rev), cb))

    def whole(a):
        nd = a.ndim
        return pl.BlockSpec(a.shape, lambda j: (0,) * nd)

    return pl.pallas_call(
        body, name=name, grid=(n,),
        in_specs=[rows(0), rows(1), rows(2), rows(lo, LANE), rows(lo + 1, LANE)] + [whole(a) for a in hp],
        out_specs=[rows(0), pl.BlockSpec((None, nh, RW_HEAD, RW_HEAD), lambda j: (j, 0, 0, 0))],
        out_shape=[jax.ShapeDtypeStruct((t, w), F32), jax.ShapeDtypeStruct((n, nh, RW_HEAD, RW_HEAD), F32)],
        scratch_shapes=[pltpu.VMEM((nh, RW_HEAD, RW_HEAD), F32)],
        compiler_params=_params(("arbitrary",)),
    )(sh, sh, sh, sh, sh, *hp)


def _rw_scan_bwd(name, sh, hp, states, dy, d, n_ctx):
    t = sh.shape[0]
    w = (sh.shape[1] - 4 * RW_LORA) // 3
    nh = w // RW_HEAD
    n = t // STEP
    rev = d == 1
    lo = 3 * w // LANE

    def body(r_ref, k_ref, v_ref, wl_ref, al_ref, w0_ref, w2_ref, a0_ref, a2_ref, kk_ref, ka_ref, st_ref, dy_ref,
             dm_ref, dl_ref, dw0_ref, dw2_ref, da0_ref, da2_ref, dkk_ref, dka_ref, ds_ref):
        step = pl.program_id(0)
        pouts = (dw0_ref, dw2_ref, da0_ref, da2_ref, dkk_ref, dka_ref)

        @pl.when(step == 0)
        def _():
            ds_ref[...] = jnp.zeros_like(ds_ref)
            for ref in pouts:
                ref[...] = jnp.zeros_like(ref)

        wl = wl_ref[...][:, d * RW_LORA:(d + 1) * RW_LORA]
        al = al_ref[...][:, d * RW_LORA:(d + 1) * RW_LORA]
        _, vjp = jax.vjp(functools.partial(_rw_step, rev=rev),
                         st_ref[...], _to_heads(r_ref[...], nh), _to_heads(k_ref[...], nh), _to_heads(v_ref[...], nh),
                         wl, al, w0_ref[...], w2_ref[...], a0_ref[...], a2_ref[...], kk_ref[...], ka_ref[...])
        g = vjp((_to_heads(dy_ref[...], nh), ds_ref[...]))
        ds_ref[...] = g[0]
        dm_ref[...] = jnp.concatenate([_from_heads(g[1]), _from_heads(g[2]), _from_heads(g[3])], axis=-1)
        zero = jnp.zeros_like(g[4])
        parts = [zero] * 4
        parts[d], parts[2 + d] = g[4], g[5]
        dl_ref[...] = jnp.concatenate(parts, axis=-1)
        for ref, val in zip(pouts, g[6:]):
            ref[...] += val

    def rows(cb, width=w):
        return pl.BlockSpec((STEP, width), lambda s: (_scan_order(n - 1 - s, n_ctx, n, rev), cb))

    def whole(a):
        nd = a.ndim
        return pl.BlockSpec(a.shape, lambda s: (0,) * nd)

    return pl.pallas_call(
        body, name=name, grid=(n,),
        in_specs=[rows(0), rows(1), rows(2), rows(lo, LANE), rows(lo + 1, LANE)] + [whole(a) for a in hp] +
                 [pl.BlockSpec((None, nh, RW_HEAD, RW_HEAD), lambda s: (n - 1 - s, 0, 0, 0)), rows(0)],
        out_specs=[rows(0, 3 * w), rows(0, 4 * RW_LORA)] + [whole(a) for a in hp],
        out_shape=[jax.ShapeDtypeStruct((t, 3 * w), F32), jax.ShapeDtypeStruct((t, 4 * RW_LORA), F32)] +
                  [jax.ShapeDtypeStruct(a.shape, F32) for a in hp],
        scratch_shapes=[pltpu.VMEM((nh, RW_HEAD, RW_HEAD), F32)],
        compiler_params=_params(("arbitrary",)),
    )(sh, sh, sh, sh, sh, *hp, states, dy)


def _shift_masks(t, n_ctx_rows):
    row = lax.broadcasted_iota(jnp.int32, (t, 1), 0)
    isx = row >= n_ctx_rows
    pos = jnp.where(isx, row - n_ctx_rows, row)
    col = jnp.where(isx, jnp.bitwise_and(pos, GRID_W - 1), pos)
    ncol = jnp.where(isx, GRID_W, n_ctx_rows)
    n_x = t - n_ctx_rows
    ml = col != 0
    mr = col != ncol - 1
    mu = isx & (pos >= GRID_W)
    md = isx & (pos < n_x - GRID_W)
    return ml, mr, mu, md, isx


def _shift_fwd(name, p, mu, n_ctx_rows):
    t, c = p.shape
    cw = LANE

    def body(p_ref, mu_ref, o_ref):
        x = p_ref[...]
        m = mu_ref[...]
        ml, mr, mup, mdn, isx = _shift_masks(t, n_ctx_rows)
        left = jnp.where(ml, pltpu.roll(x, 1, 0), 0.0)
        right = jnp.where(mr, pltpu.roll(x, t - 1, 0), 0.0)
        up = jnp.where(mup, pltpu.roll(x, GRID_W, 0), 0.0)
        down = jnp.where(mdn, pltpu.roll(x, t - GRID_W, 0), 0.0)
        out = x + m[0:1] * (left - x) + m[1:2] * (right - x)
        vert = m[2:3] * (up - x) + m[3:4] * (down - x)
        o_ref[...] = out + jnp.where(isx, vert, 0.0)

    return pl.pallas_call(
        body, name=name, grid=(c // cw,),
        in_specs=[pl.BlockSpec((t, cw), lambda j: (0, j)), pl.BlockSpec((4, cw), lambda j: (0, j))],
        out_specs=pl.BlockSpec((t, cw), lambda j: (0, j)),
        out_shape=jax.ShapeDtypeStruct((t, c), F32),
        compiler_params=_params(("parallel",)),
    )(p, mu)


def _shift_bwd(name, p, mu, dparts, n_ctx_rows):
    t, c = p.shape
    cw = LANE
    npart = len(dparts)

    def body(*refs):
        p_ref, mu_ref = refs[0], refs[1]
        dp_ref, dmu_ref = refs[2 + npart], refs[3 + npart]
        x = p_ref[...]
        m = mu_ref[...]
        g = refs[2][...]
        for r in refs[3:2 + npart]:
            g = g + r[...]
        ml, mr, mup, mdn, isx = _shift_masks(t, n_ctx_rows)
        left = jnp.where(ml, pltpu.roll(x, 1, 0), 0.0)
        right = jnp.where(mr, pltpu.roll(x, t - 1, 0), 0.0)
        up = jnp.where(mup, pltpu.roll(x, GRID_W, 0), 0.0)
        down = jnp.where(mdn, pltpu.roll(x, t - GRID_W, 0), 0.0)
        gx = jnp.where(isx, g, 0.0)
        dmu_ref[...] = jnp.concatenate([
            jnp.sum(g * (left - x), axis=0, keepdims=True), jnp.sum(g * (right - x), axis=0, keepdims=True),
            jnp.sum(gx * (up - x), axis=0, keepdims=True), jnp.sum(gx * (down - x), axis=0, keepdims=True)], axis=0)
        coef = 1.0 - m[0:1] - m[1:2] - jnp.where(isx, m[2:3] + m[3:4], 0.0)
        dp = coef * g
        dp = dp + m[0:1] * pltpu.roll(jnp.where(ml, g, 0.0), t - 1, 0)
        dp = dp + m[1:2] * pltpu.roll(jnp.where(mr, g, 0.0), 1, 0)
        dp = dp + m[2:3] * pltpu.roll(jnp.where(mup, g, 0.0), t - GRID_W, 0)
        dp = dp + m[3:4] * pltpu.roll(jnp.where(mdn, g, 0.0), GRID_W, 0)
        dp_ref[...] = dp

    col = pl.BlockSpec((t, cw), lambda j: (0, j))
    par = pl.BlockSpec((4, cw), lambda j: (0, j))
    return pl.pallas_call(
        body, name=name, grid=(c // cw,),
        in_specs=[col, par] + [col] * npart,
        out_specs=[col, par],
        out_shape=[jax.ShapeDtypeStruct((t, c), F32), jax.ShapeDtypeStruct((4, c), F32)],
        compiler_params=_params(("parallel",)),
    )(p, mu, *dparts)


def _local_step(x, c, ctx, c_ctx, ada_st, ada_b, norm_g, w_in_st, hg_lb, hg_norm_g, rw_mu, rw_w0, rw_w2, rw_a0, rw_a2,
                rw_kk, rw_ka, rw_rk, rw_gn_g, rw_gn_b, w_hg_st, w_rw_st, w_out, final_g, tgt):
    seq, dm = x.shape
    n_ctx_rows = ctx.shape[0]
    t = seq + n_ctx_rows
    hw = hg_norm_g.shape[-1]
    rw = rw_kk.shape[-1]
    nh_rw = rw // RW_HEAD
    n_ctx = n_ctx_rows // STEP
    tm = _tile(n_ctx_rows, (256, 128, 64))
    nt = t // tm
    nct = n_ctx_rows // tm
    n_sh_cols = 3 * rw + 4 * RW_LORA

    xs = jnp.concatenate([ctx, x], axis=0)
    cond = jnp.concatenate([c.reshape(1, dm), c_ctx.reshape(1, dm), jnp.zeros((6, dm), F32)], axis=0)
    final_g2 = final_g.reshape(1, dm)

    def unstack(a_st):
        return jnp.swapaxes(a_st, 0, 1).reshape(a_st.shape[1], -1)

    def restack(a, ns=N_SHARD):
        return jnp.swapaxes(a.reshape(a.shape[0], ns, -1), 0, 1)

    (sc,) = _row_call("cond_silu", lambda i, r, f: ([jax.nn.silu(r[0])], []), 1, 8, [(cond, 0, dm, 0)], [],
                      [(8, dm, F32, 0)], [])
    mod_st = _mm_n_st("mod_mm", sc, ada_st)
    mod = unstack(mod_st) + ada_b
    mod3 = mod.reshape(8, 3, dm)

    def pick(i, m3):
        r = jnp.where(i < nct, m3[1], m3[0])
        return r[0:1], r[1:2]

    def h_fn(i, r, f):
        shift, scale = pick(i, f[1])
        return [_fn_h(r[0], f[0], scale, shift)], []

    (h,) = _row_call("h_fwd", h_fn, nt, tm, [(xs, 0, dm, 0)], [norm_g, mod3], [(t, dm, BF16, 0)], [])
    proj = unstack(_mm_n_st("proj_mm", h, w_in_st))
    p_hg = proj[:, :5 * hw]
    p_rs = proj[:, 5 * hw:5 * hw + n_sh_cols]
    p_zr = proj[:, 5 * hw + n_sh_cols:5 * hw + n_sh_cols + rw]
    p_gt = proj[:, 5 * hw + n_sh_cols + rw:]

    o_hg, st_hg = [], []
    for d in range(2):
        o, st = _hg_scan_fwd(f"hg_scan_fwd{d}", p_hg, hg_lb[d], d, n_ctx)
        o_hg.append(o)
        st_hg.append(st)

    def hgpost_fn(i, r, f):
        return [_fn_hgpost(r[0], r[1], r[2], f[0])], []

    hg_in = [(o_hg[0], 0, hw, 0), (o_hg[1], 0, hw, 0), (p_hg, 4, hw, 0)]
    (y_hg,) = _row_call("hg_post", hgpost_fn, nt, tm, hg_in, [hg_norm_g], [(t, hw, BF16, 0)], [])

    sh = _shift_fwd("rw_shift", p_rs, rw_mu, n_ctx_rows)
    hps = []
    for d in range(2):
        hps.append([rw_w0[d].reshape(nh_rw, 1, RW_HEAD), jnp.swapaxes(rw_w2[d].reshape(RW_LORA, nh_rw, RW_HEAD), 0, 1),
                    rw_a0[d].reshape(nh_rw, 1, RW_HEAD), jnp.swapaxes(rw_a2[d].reshape(RW_LORA, nh_rw, RW_HEAD), 0, 1),
                    rw_kk.reshape(nh_rw, 1, RW_HEAD), rw_ka.reshape(nh_rw, 1, RW_HEAD)])
    y_rw_d, st_rw = [], []
    for d in range(2):
        y, st = _rw_scan_fwd(f"rw_scan_fwd{d}", sh, hps[d], d, n_ctx)
        y_rw_d.append(y)
        st_rw.append(st)

    rw_full = [rw_a0, rw_a2, rw_ka, rw_rk, rw_gn_g, rw_gn_b]
    lo = 3 * rw // LANE
    rw_in = [(y_rw_d[0], 0, rw, 0), (y_rw_d[1], 0, rw, 0), (sh, 0, rw, 0), (sh, 1, rw, 0), (sh, 2, rw, 0),
             (sh, lo + 1, LANE, 0), (p_zr, 0, rw, 0)]

    def rwpost_fn(i, r, f):
        return [_fn_rwpost(*r, *f)], []

    (y_rw,) = _row_call("rw_post", rwpost_fn, nt, tm, rw_in, rw_full, [(t, rw, BF16, 0)], [])

    a_hg = unstack(_mm_n_st("hg_out_mm", y_hg, w_hg_st))
    a_rw = unstack(_mm_n_st("rw_out_mm", y_rw, w_rw_st))
    mg_in = [(a_hg, 0, dm, 0), (a_rw, 0, dm, 0), (p_gt, 0, dm, 0), (p_gt, 1, dm, 0)]
    (merged,) = _row_call("merge", lambda i, r, f: ([_fn_merge(*r)], []), nt, tm, mg_in, [], [(t, dm, BF16, 0)], [])
    o_out = _mm_nn("out_mm", merged, w_out)

    def final_fn(i, r, f):
        gate = f[0][0][2:3]
        loss, vjp = jax.vjp(_fn_final, r[0], r[1], gate, f[1], r[2])
        dx, do, dgate, dfg, _ = vjp(jnp.ones((), F32))
        live = i >= nct
        zero = lambda a: jnp.where(live, a, 0.0)
        dmod = jnp.concatenate([jnp.concatenate([jnp.zeros((1, 2 * dm), F32), zero(dgate)], axis=1),
                                jnp.zeros((7, 3 * dm), F32)], axis=0)
        return [zero(dx), zero(do)], [jnp.broadcast_to(zero(loss), (8, LANE)), dmod, zero(dfg)]

    fin_in = [(xs, 0, dm, 0), (o_out, 0, dm, 0), (tgt, 0, dm, nct)]
    dx_res, d_o, loss_acc, dmod_gate, d_final_g = _row_call(
        "final", final_fn, nt, tm, fin_in, [mod3, final_g2], [(t, dm, F32, 0), (t, dm, BF16, 0)],
        [((8, LANE), F32), ((8, 3 * dm), F32), ((1, dm), F32)])

    g_w_out = _mm_tn("d_w_out", merged, d_o)
    d_merged = _mm_nt("d_merged", d_o, w_out)

    def merge_bwd(i, r, f):
        _, vjp = jax.vjp(_fn_merge, r[0], r[1], r[2], r[3])
        da, db, dgh, dgr = vjp(r[4])
        return [da, db, jnp.concatenate([dgh, dgr], axis=1)], []

    da_hg, da_rw, dp_gt = _row_call("merge_bwd", merge_bwd, nt, tm, mg_in + [(d_merged, 0, dm, 0)], [],
                                    [(t, dm, BF16, 0), (t, dm, BF16, 0), (t, 2 * dm, F32, 0)], [])
    g_w_hg_st = _mm_t_st("d_w_hg", y_hg, restack(da_hg))
    g_w_rw_st = _mm_t_st("d_w_rw", y_rw, restack(da_rw))
    dy_hg = _mm_st_t("d_y_hg", restack(da_hg), w_hg_st)
    dy_rw = _mm_st_t("d_y_rw", restack(da_rw), w_rw_st)

    def hgpost_bwd(i, r, f):
        _, vjp = jax.vjp(_fn_hgpost, r[0], r[1], r[2], f[0])
        dof, _, dz, dg = vjp(r[3])
        return [dof, dz], [dg]

    do_hg, dz_hg, g_hg_norm = _row_call("hg_post_bwd", hgpost_bwd, nt, tm, hg_in + [(dy_hg, 0, hw, 0)], [hg_norm_g],
                                        [(t, hw, F32, 0), (t, hw, F32, 0)], [((1, hw), F32)])
    dqs, dis, dfs, g_lb = [], [], [], []
    for d in range(2):
        dq, di, df, dlb = _hg_scan_bwd(f"hg_scan_bwd{d}", p_hg, hg_lb[d], st_hg[d], do_hg, d, n_ctx)
        dqs.append(dq)
        dis.append(di)
        dfs.append(df)
        g_lb.append(dlb)
    (dqi,) = _row_call("hg_dsum", lambda i, r, f: ([jnp.concatenate([r[0] + r[1], r[2] + r[3]], axis=1)], []), nt, tm,
                       [(dqs[0], 0, hw, 0), (dqs[1], 0, hw, 0), (dis[0], 0, hw, 0), (dis[1], 0, hw, 0)], [],
                       [(t, 2 * hw, F32, 0)], [])
    g_hg_lb = jnp.stack(g_lb, axis=0)

    def rwpost_bwd(i, r, f):
        _, vjp = jax.vjp(_fn_rwpost, *r[:7], *f)
        g = vjp(r[7])
        zl = jnp.zeros((g[5].shape[0], 2 * RW_LORA), F32)
        dmain = jnp.concatenate([g[2], g[3], g[4]], axis=1)
        return [g[0], dmain, jnp.concatenate([zl, g[5]], axis=1), g[6]], list(g[7:])

    dy_sum, dsh_p, dsl_p, dz_rw, g_a0_p, g_a2_p, g_ka_p, g_rk, g_gn_g, g_gn_b = _row_call(
        "rw_post_bwd", rwpost_bwd, nt, tm, rw_in + [(dy_rw, 0, rw, 0)], rw_full,
        [(t, rw, F32, 0), (t, 3 * rw, F32, 0), (t, 4 * RW_LORA, F32, 0), (t, rw, F32, 0)],
        [(a.shape, F32) for a in rw_full])
    dmains, dloras, hp_grads = [dsh_p], [dsl_p], []
    for d in range(2):
        res = _rw_scan_bwd(f"rw_scan_bwd{d}", sh, hps[d], st_rw[d], dy_sum, d, n_ctx)
        dmains.append(res[0])
        dloras.append(res[1])
        hp_grads.append(res[2:])
    dsh_parts = [jnp.concatenate([m, l], axis=1) for m, l in zip(dmains, dloras)]
    dp_rs, g_mu = _shift_bwd("rw_shift_bwd", p_rs, rw_mu, dsh_parts, n_ctx_rows)

    def flat(a):
        if a.shape[1] == 1:
            return a.reshape(rw)
        return jnp.swapaxes(a, 0, 1).reshape(RW_LORA, rw)

    g_w0 = jnp.stack([flat(hp_grads[d][0]) for d in range(2)], axis=0)
    g_w2 = jnp.stack([flat(hp_grads[d][1]) for d in range(2)], axis=0)
    g_a0 = jnp.stack([flat(hp_grads[d][2]) for d in range(2)], axis=0) + g_a0_p
    g_a2 = jnp.stack([flat(hp_grads[d][3]) for d in range(2)], axis=0) + g_a2_p
    g_kk = (flat(hp_grads[0][4]) + flat(hp_grads[1][4])).reshape(1, rw)
    g_ka = (flat(hp_grads[0][5]) + flat(hp_grads[1][5])).reshape(1, rw) + g_ka_p

    dproj = jnp.concatenate([dqi, dfs[0], dfs[1], dz_hg, dp_rs, dz_rw, dp_gt], axis=1).astype(BF16)
    dproj_st = restack(dproj)
    g_w_in_st = _mm_t_st("d_w_in", h, dproj_st)
    dh = _mm_st_t("d_h", dproj_st, w_in_st)

    def h_bwd(i, r, f):
        shift, scale = pick(i, f[1])
        _, vjp = jax.vjp(_fn_h, r[0], f[0], scale, shift)
        ds, dg, dscale, dshift = vjp(r[1])
        row = jnp.concatenate([dshift, dscale, jnp.zeros((1, dm), F32)], axis=1)
        z = jnp.zeros_like(row)
        is_ctx = i < nct
        dmod = jnp.concatenate([jnp.where(is_ctx, z, row), jnp.where(is_ctx, row, z), jnp.zeros((6, 3 * dm), F32)], axis=0)
        return [ds + r[2]], [dg, dmod]

    grad_x, g_norm_g, dmod_h = _row_call(
        "h_bwd", h_bwd, nt, tm, [(xs, 0, dm, 0), (dh, 0, dm, 0), (dx_res, 0, dm, 0)], [norm_g, mod3],
        [(seq, dm, F32, nct)], [((1, dm), F32), ((8, 3 * dm), F32)])
    dmod = dmod_h + dmod_gate
    g_ada_b = (dmod[0] + dmod[1]).reshape(1, 3 * dm)
    g_ada_st = _mm_t_st("d_ada_w", sc, restack(dmod))
    d_sc = _mm_st_t("d_cond", restack(dmod), ada_st)

    def cond_bwd(i, r, f):
        _, vjp = jax.vjp(jax.nn.silu, r[0])
        return [vjp(r[1])[0]], []

    (d_cond,) = _row_call("cond_bwd", cond_bwd, 1, 8, [(cond, 0, dm, 0), (d_sc, 0, dm, 0)], [], [(8, dm, F32, 0)], [])

    grads = dict(
        c_ctx=d_cond[1], ada_w=g_ada_st, ada_b=g_ada_b, norm_g=g_norm_g, w_in=g_w_in_st, hg_lb=g_hg_lb,
        hg_norm_g=g_hg_norm, rw_mu=g_mu, rw_w0=g_w0, rw_w2=g_w2, rw_a0=g_a0, rw_a2=g_a2, rw_kk=g_kk, rw_ka=g_ka,
        rw_rk=g_rk, rw_gn_g=g_gn_g, rw_gn_b=g_gn_b, w_hg_out=g_w_hg_st, w_rw_out=g_w_rw_st, w_out=g_w_out,
        final_g=d_final_g.reshape(dm))
    return loss_acc[0:1, 0:1], grad_x, grads


def _my_place():
    return lax.axis_index("x"), lax.axis_index("y"), lax.axis_index("c")


MIN_CHUNK_BYTES = 1 << 18
ROW_ALIGN = 16


def _n_chunks(rows, row_bytes):
    for n in (8, 4, 2):
        if rows % (n * ROW_ALIGN) == 0 and rows // n * row_bytes >= MIN_CHUNK_BYTES:
            return n
    return 1


def _row_bytes(a, lead=1):
    n = a.dtype.itemsize
    for d in a.shape[lead:]:
        n *= d
    return n


def _rows(ref, start, size):
    return ref.at[pl.ds(start, size)]


def _chunked(make, start, size, n):
    cs = size // n
    return [make(start + j * cs, cs) for j in range(n)]


_PEER_CHIPS = 3


def _weights_gather(name, big, small):
    nb, na = len(big), len(big) + len(small)
    arrays = list(big) + list(small)

    def body(*refs):
        outs = refs[na:2 * na]
        send_sems, recv_sems, fsend_sems, frecv_sems = refs[2 * na:]
        x, y, c = _my_place()
        me = 2 * x + y
        chips = [(1 - x, y), (x, 1 - y), (1 - x, 1 - y)]

        def over_ici(a, k, slot, r0, nr):
            px, py = chips[k]
            return pltpu.make_async_remote_copy(
                src_ref=_rows(outs[a].at[me], r0, nr), dst_ref=_rows(outs[a].at[slot], r0, nr), send_sem=send_sems.at[a, k],
                recv_sem=recv_sems.at[a, k], device_id=(px, py, c), device_id_type=pl.DeviceIdType.MESH)

        def to_sibling(a, k, r0, nr):
            px, py = chips[k]
            rows = _rows(outs[a].at[2 * px + py], r0, nr)
            return pltpu.make_async_remote_copy(
                src_ref=rows, dst_ref=rows, send_sem=fsend_sems.at[a, k], recv_sem=frecv_sems.at[a, k],
                device_id=(x, y, 1 - c), device_id_type=pl.DeviceIdType.MESH)

        span = []
        for a in range(na):
            rows = arrays[a].shape[1]
            if a < nb:
                half = rows // 2
                span.append((pl.multiple_of(c * half, ROW_ALIGN), pl.multiple_of((1 - c) * half, ROW_ALIGN), half,
                             _n_chunks(half, _row_bytes(arrays[a], 2))))
            else:
                span.append((0, 0, rows, 1))
        for a in range(na):
            mine, _, nr, n = span[a]
            for k in range(_PEER_CHIPS):
                for cp in _chunked(lambda r0, cs: over_ici(a, k, me, r0, cs), mine, nr, n):
                    cp.start()
        for k in range(_PEER_CHIPS):
            px, py = chips[k]
            for a in range(na):
                mine, _, nr, n = span[a]
                over_ici(a, k, 2 * px + py, mine, nr).wait_recv()
                if a < nb:
                    for cp in _chunked(lambda r0, cs: to_sibling(a, k, r0, cs), mine, nr, n):
                        cp.start()
        for k in range(_PEER_CHIPS):
            for a in range(nb):
                _, theirs, nr, _ = span[a]
                to_sibling(a, k, theirs, nr).wait_recv()
        for a in range(na):
            mine, _, nr, _ = span[a]
            for k in range(_PEER_CHIPS):
                over_ici(a, k, me, mine, nr).wait_send()
                if a < nb:
                    to_sibling(a, k, mine, nr).wait_send()

    hbm = pl.BlockSpec(memory_space=pl.ANY)
    sems = pltpu.SemaphoreType.DMA((na, _PEER_CHIPS))
    return pl.pallas_call(
        body, name=name, in_specs=[hbm] * na, out_specs=[hbm] * na,
        out_shape=[jax.ShapeDtypeStruct(a.shape, a.dtype) for a in arrays],
        input_output_aliases={a: a for a in range(na)}, scratch_shapes=[sems, sems, sems, sems],
    )(*arrays)


def _chip_scatter(name, arrays):
    na = len(arrays)

    def body(*refs):
        ins, outs = refs[:na], refs[na:2 * na]
        send_sems, recv_sems = refs[2 * na:]
        x, y, c = _my_place()
        me = 2 * x + y
        chips = [(1 - x, y), (x, 1 - y), (1 - x, 1 - y)]

        def remote(a, k, slot, r0, nr):
            px, py = chips[k]
            return pltpu.make_async_remote_copy(
                src_ref=_rows(ins[a].at[2 * px + py], r0, nr), dst_ref=_rows(outs[a].at[slot], r0, nr),
                send_sem=send_sems.at[a, k], recv_sem=recv_sems.at[a, k], device_id=(px, py, c),
                device_id_type=pl.DeviceIdType.MESH)

        for a in range(na):
            rows = arrays[a].shape[1]
            for k in range(_PEER_CHIPS):
                for cp in _chunked(lambda r0, cs: remote(a, k, me, r0, cs), 0, rows, _n_chunks(rows, _row_bytes(arrays[a], 2))):
                    cp.start()
        for k in range(_PEER_CHIPS):
            px, py = chips[k]
            for a in range(na):
                remote(a, k, 2 * px + py, 0, arrays[a].shape[1]).wait_recv()
        for a in range(na):
            for k in range(_PEER_CHIPS):
                remote(a, k, me, 0, arrays[a].shape[1]).wait_send()

    hbm = pl.BlockSpec(memory_space=pl.ANY)
    sems = pltpu.SemaphoreType.DMA((na, _PEER_CHIPS))
    return pl.pallas_call(
        body, name=name, in_specs=[hbm] * na, out_specs=[hbm] * na,
        out_shape=[jax.ShapeDtypeStruct(a.shape, a.dtype) for a in arrays], scratch_shapes=[sems, sems],
    )(*arrays)


PAIR_TILE_BYTES = 2 << 20


def _pair_exchange(name, a, place, reduce, out_dtype):
    rows, cols = a.shape[-2], a.shape[-1]
    half = rows // 2 if reduce else rows
    tr = _row_tile_for(half, cols, budget=PAIR_TILE_BYTES)
    nh = half // tr
    n_steps = (N_SHARD if reduce else 1) * nh

    def body(pc_ref, *refs):
        if reduce:
            keep_ref, send_ref, o_ref, land, send_sems, recv_sems, credit = refs
        else:
            send_ref, o_ref, land, send_sems, recv_sems, credit = refs
        x, y, c = _my_place()
        other = (x, y, 1 - c)
        t = pl.program_id(0) * nh + pl.program_id(1) if reduce else pl.program_id(0)
        slot = t % 2

        @pl.when(t >= 2)
        def _():
            pl.semaphore_wait(credit, 1)

        copy = pltpu.make_async_remote_copy(
            src_ref=send_ref, dst_ref=land.at[pl.ds(slot, 1)] if reduce else land.at[slot], send_sem=send_sems.at[slot],
            recv_sem=recv_sems.at[slot], device_id=other, device_id_type=pl.DeviceIdType.MESH)
        copy.start()
        copy.wait_recv()
        got = land[slot]
        o_ref[...] = ((keep_ref[...] + got) if reduce else got).astype(out_dtype)
        copy.wait_send()

        @pl.when(t < n_steps - 2)
        def _():
            pl.semaphore_signal(credit, inc=1, device_id=other, device_id_type=pl.DeviceIdType.MESH)

    if reduce:
        grid = (N_SHARD, nh)
        in_specs = [pl.BlockSpec((None, tr, cols), lambda j, i, pc: (j, pc[0] * nh + i, 0)),
                    pl.BlockSpec((1, tr, cols), lambda j, i, pc: (j, (1 - pc[0]) * nh + i, 0))]
        out_spec = pl.BlockSpec((None, tr, cols), lambda j, i, pc: (j, i, 0))
        out_shape = jax.ShapeDtypeStruct((N_SHARD, half, cols), out_dtype)
        operands = (a, a)
        sem = ("arbitrary", "arbitrary")
    else:
        grid = (nh,)
        in_specs = [pl.BlockSpec((tr, cols), lambda i, pc: (i, 0))]
        out_spec = pl.BlockSpec((tr, cols), lambda i, pc: (i, 0))
        out_shape = jax.ShapeDtypeStruct((half, cols), out_dtype)
        operands = (a,)
        sem = ("arbitrary",)
    return pl.pallas_call(
        body, name=name,
        grid_spec=pltpu.PrefetchScalarGridSpec(
            num_scalar_prefetch=1, grid=grid, in_specs=in_specs, out_specs=out_spec,
            scratch_shapes=[pltpu.VMEM((2, tr, cols), a.dtype), pltpu.SemaphoreType.DMA((2,)),
                            pltpu.SemaphoreType.DMA((2,)), pltpu.SemaphoreType.REGULAR]),
        out_shape=out_shape, compiler_params=_params(sem),
    )(place, *operands)


def _cast_into_slot(name, a, chip):
    rows, cols = a.shape
    tm = _row_tile_for(rows, cols)

    def body(pc_ref, a_ref, o_ref):
        o_ref[...] = a_ref[...].astype(BF16)

    return pl.pallas_call(
        body, name=name,
        grid_spec=pltpu.PrefetchScalarGridSpec(
            num_scalar_prefetch=1, grid=(rows // tm,), in_specs=[pl.BlockSpec((tm, cols), lambda i, pc: (i, 0))],
            out_specs=pl.BlockSpec((None, tm, cols), lambda i, pc: (pc[0], i, 0))),
        out_shape=jax.ShapeDtypeStruct((N_SHARD, rows, cols), BF16), compiler_params=_params(("parallel",)),
    )(chip, a)


def _sum_landed(name, landed, sent, chip):
    ns, rows, cols = landed.shape
    tm = _row_tile_for(rows, cols)

    def body(pc_ref, *refs):
        own_ref, o_ref = refs[ns], refs[ns + 1]
        me = pc_ref[0]
        terms = [jnp.where(me == j, own_ref[...], refs[j][...]).astype(F32) for j in range(ns)]
        o_ref[...] = _slot_sum(terms)

    def landed_spec(j):
        return pl.BlockSpec((None, tm, cols), lambda i, pc: (jnp.where(pc[0] == j, (j + 1) % ns, j), i, 0))

    return pl.pallas_call(
        body, name=name,
        grid_spec=pltpu.PrefetchScalarGridSpec(
            num_scalar_prefetch=1, grid=(rows // tm,),
            in_specs=[landed_spec(j) for j in range(ns)] + [pl.BlockSpec((None, tm, cols), lambda i, pc: (pc[0], i, 0))],
            out_specs=pl.BlockSpec((tm, cols), lambda i, pc: (i, 0))),
        out_shape=jax.ShapeDtypeStruct((rows, cols), F32), compiler_params=_params(("parallel",)),
    )(chip, *([landed] * ns), sent)


def _gather_all(name, a):
    def body(in_ref, out_ref, send_sems, recv_sems, local_sem):
        x, y, c = _my_place()
        me = 4 * x + 2 * y + c

        def peer(k):
            return (x ^ (k >> 2), y ^ ((k >> 1) & 1), c ^ (k & 1))

        def remote(k, land):
            return pltpu.make_async_remote_copy(
                src_ref=in_ref, dst_ref=out_ref.at[land], send_sem=send_sems.at[k - 1], recv_sem=recv_sems.at[k - 1],
                device_id=peer(k), device_id_type=pl.DeviceIdType.MESH)

        local = pltpu.make_async_copy(in_ref, out_ref.at[me], local_sem)
        local.start()
        for k in range(1, N_DEV):
            remote(k, me).start()
        for k in range(1, N_DEV):
            px, py, pc = peer(k)
            remote(k, 4 * px + 2 * py + pc).wait_recv()
        for k in range(1, N_DEV):
            remote(k, me).wait_send()
        local.wait()

    hbm = pl.BlockSpec(memory_space=pl.ANY)
    return pl.pallas_call(
        body, name=name, in_specs=[hbm], out_specs=hbm,
        out_shape=jax.ShapeDtypeStruct((N_DEV,) + a.shape, a.dtype),
        scratch_shapes=[pltpu.SemaphoreType.DMA((N_DEV - 1,)), pltpu.SemaphoreType.DMA((N_DEV - 1,)), pltpu.SemaphoreType.DMA],
    )(a)


def _row_tile_for(rows, cols, budget=1 << 20):
    for tm in (1024, 512, 256, 128, 64, 32, 16, 8):
        if rows % tm == 0 and tm * cols * 4 <= budget:
            return tm
    return rows


def _slot_sum(vals):
    g = vals[0]
    for v in vals[1:]:
        g = g + v
    return g


def _rowwise(name, fn, arrays, out_dtype):
    rows, cols = arrays[0].shape
    tm = _row_tile_for(rows, cols)

    def body(*refs):
        refs[-1][...] = fn(*[r[...] for r in refs[:-1]]).astype(out_dtype)

    blk = pl.BlockSpec((tm, cols), lambda i: (i, 0))
    return pl.pallas_call(
        body, name=name, grid=(rows // tm,), in_specs=[blk] * len(arrays), out_specs=blk,
        out_shape=jax.ShapeDtypeStruct((rows, cols), out_dtype), compiler_params=_params(("parallel",)),
    )(*arrays)


def _sum_slots(name, st):
    ns, rows, cols = st.shape
    tm = _row_tile_for(rows, cols)

    def body(s_ref, o_ref):
        o_ref[...] = _slot_sum([s_ref[j].astype(F32) for j in range(ns)])

    return pl.pallas_call(
        body, name=name, grid=(rows // tm,),
        in_specs=[pl.BlockSpec((ns, tm, cols), lambda i: (0, i, 0))],
        out_specs=pl.BlockSpec((tm, cols), lambda i: (i, 0)),
        out_shape=jax.ShapeDtypeStruct((rows, cols), F32),
        compiler_params=_params(("parallel",)),
    )(st)


ADAM_TILE_BYTES = 1 << 19


def _adam_update(g, p_ref, m_ref, v_ref, go_ref, d_ref, mo_ref, vo_ref):
    mn = ADAM_B1 * m_ref[...] + (1.0 - ADAM_B1) * g
    vn = ADAM_B2 * v_ref[...] + (1.0 - ADAM_B2) * jnp.square(g)
    m_hat = mn / (1.0 - ADAM_B1 ** ADAM_STEP)
    v_hat = vn / (1.0 - ADAM_B2 ** ADAM_STEP)
    go_ref[...] = g
    d_ref[...] = -ADAM_LR * (m_hat / (jnp.sqrt(v_hat) + ADAM_EPS) + ADAM_WD * p_ref[...])
    mo_ref[...] = mn
    vo_ref[...] = vn


def _adamw(name, p, m, v, gst):
    rows, cols = p.shape
    ns = gst.shape[0]
    tm = _row_tile_for(rows, cols, budget=ADAM_TILE_BYTES)

    def body(p_ref, m_ref, v_ref, g_ref, *outs):
        _adam_update(_slot_sum([g_ref[j] for j in range(ns)]), p_ref, m_ref, v_ref, *outs)

    blk = pl.BlockSpec((tm, cols), lambda i: (i, 0))
    return pl.pallas_call(
        body, name=name, grid=(rows // tm,),
        in_specs=[blk, blk, blk, pl.BlockSpec((ns, tm, cols), lambda i: (0, i, 0))],
        out_specs=[blk] * 4, out_shape=[jax.ShapeDtypeStruct((rows, cols), F32)] * 4,
        compiler_params=_params(("parallel",)),
    )(p, m, v, gst)


def _adamw_halves(name, p, m, v, mine, theirs, place):
    rows, cols = p.shape
    half = rows // 2
    tm = _row_tile_for(half, cols, budget=ADAM_TILE_BYTES)
    nh = half // tm

    def body(pc_ref, p_ref, m_ref, v_ref, mine_ref, theirs_ref, *outs):
        g = jnp.where(pl.program_id(0) == pc_ref[0], mine_ref[...], theirs_ref[...])
        _adam_update(g, p_ref, m_ref, v_ref, *outs)

    blk = pl.BlockSpec((tm, cols), lambda h, i, pc: (h * nh + i, 0))
    hblk = pl.BlockSpec((tm, cols), lambda h, i, pc: (i, 0))
    return pl.pallas_call(
        body, name=name,
        grid_spec=pltpu.PrefetchScalarGridSpec(
            num_scalar_prefetch=1, grid=(2, nh), in_specs=[blk, blk, blk, hblk, hblk], out_specs=[blk] * 4),
        out_shape=[jax.ShapeDtypeStruct((rows, cols), F32)] * 4, compiler_params=_params(("parallel", "parallel")),
    )(place, p, m, v, mine, theirs)


def _pack(parts, width=LANE, mult=8):
    flat = jnp.concatenate([a.reshape(-1) for a in parts])
    n = flat.shape[0]
    per = width * mult
    total = -(-n // per) * per
    return jnp.pad(flat, (0, total - n)).reshape(total // width, width)


def _unpack(packed, shapes):
    flat = packed.reshape(-1)
    out, off = [], 0
    for s in shapes:
        n = 1
        for d in s:
            n *= d
        out.append(flat[off:off + n].reshape(s))
        off += n
    return out


_SMALL_SHARDED = ("hg_lb", "rw_mu", "rw_w0", "rw_w2", "rw_a0", "rw_a2")
_REPLICATED = ("c_ctx", "ada_b", "norm_g", "hg_norm_g", "rw_kk", "rw_ka", "rw_rk", "rw_gn_g", "rw_gn_b", "final_g")
_BIG = ("ada_w", "w_in", "w_hg_out", "w_rw_out", "w_out")
_WEIGHTS = ("c_ctx", "ada_w", "ada_b", "norm_g", "w_in", "hg_lb", "hg_norm_g", "rw_mu", "rw_w0", "rw_w2", "rw_a0", "rw_a2",
            "rw_kk", "rw_ka", "rw_rk", "rw_gn_g", "rw_gn_b", "w_hg_out", "w_rw_out", "w_out", "final_g")


def _join_shards(st):
    a = jnp.moveaxis(st, 0, -2)
    return a.reshape(a.shape[:-2] + (a.shape[-2] * a.shape[-1],))


def _split_shards(a):
    s = a.reshape(a.shape[:-1] + (N_SHARD, a.shape[-1] // N_SHARD))
    return jnp.moveaxis(s, -2, 0)


def kernel(x, c, ctx, c_ctx, ada_w, ada_b, norm_g, w_in, hg_lb, hg_norm_g, rw_mu, rw_w0, rw_w2, rw_a0, rw_a2, rw_kk, rw_ka, rw_rk, rw_gn_g, rw_gn_b, w_hg_out, w_rw_out, w_out, final_g, loss_target, m_c_ctx, m_ada_w, m_ada_b, m_norm_g, m_w_in, m_hg_lb, m_hg_norm_g, m_rw_mu, m_rw_w0, m_rw_w2, m_rw_a0, m_rw_a2, m_rw_kk, m_rw_ka, m_rw_rk, m_rw_gn_g, m_rw_gn_b, m_w_hg_out, m_w_rw_out, m_w_out, m_final_g, v_c_ctx, v_ada_w, v_ada_b, v_norm_g, v_w_in, v_hg_lb, v_hg_norm_g, v_rw_mu, v_rw_w0, v_rw_w2, v_rw_a0, v_rw_a2, v_rw_kk, v_rw_ka, v_rw_rk, v_rw_gn_g, v_rw_gn_b, v_w_hg_out, v_w_rw_out, v_w_out, v_final_g):
    w = dict(c_ctx=c_ctx, ada_w=ada_w, ada_b=ada_b, norm_g=norm_g, w_in=w_in, hg_lb=hg_lb, hg_norm_g=hg_norm_g, rw_mu=rw_mu,
             rw_w0=rw_w0, rw_w2=rw_w2, rw_a0=rw_a0, rw_a2=rw_a2, rw_kk=rw_kk, rw_ka=rw_ka, rw_rk=rw_rk, rw_gn_g=rw_gn_g,
             rw_gn_b=rw_gn_b, w_hg_out=w_hg_out, w_rw_out=w_rw_out, w_out=w_out, final_g=final_g)
    m = dict(c_ctx=m_c_ctx, ada_w=m_ada_w, ada_b=m_ada_b, norm_g=m_norm_g, w_in=m_w_in, hg_lb=m_hg_lb, hg_norm_g=m_hg_norm_g,
             rw_mu=m_rw_mu, rw_w0=m_rw_w0, rw_w2=m_rw_w2, rw_a0=m_rw_a0, rw_a2=m_rw_a2, rw_kk=m_rw_kk, rw_ka=m_rw_ka,
             rw_rk=m_rw_rk, rw_gn_g=m_rw_gn_g, rw_gn_b=m_rw_gn_b, w_hg_out=m_w_hg_out, w_rw_out=m_w_rw_out, w_out=m_w_out,
             final_g=m_final_g)
    v = dict(c_ctx=v_c_ctx, ada_w=v_ada_w, ada_b=v_ada_b, norm_g=v_norm_g, w_in=v_w_in, hg_lb=v_hg_lb, hg_norm_g=v_hg_norm_g,
             rw_mu=v_rw_mu, rw_w0=v_rw_w0, rw_w2=v_rw_w2, rw_a0=v_rw_a0, rw_a2=v_rw_a2, rw_kk=v_rw_kk, rw_ka=v_rw_ka,
             rw_rk=v_rw_rk, rw_gn_g=v_rw_gn_g, rw_gn_b=v_rw_gn_b, w_hg_out=v_w_hg_out, w_rw_out=v_w_rw_out, w_out=v_w_out,
             final_g=v_final_g)

    def mat(a):
        return a.reshape(a.shape[-2], a.shape[-1])

    def pack_small(d):
        return _pack([d[n] for n in _SMALL_SHARDED], mult=2 * ROW_ALIGN)

    my_core = lax.axis_index("c").astype(jnp.int32).reshape(1)
    my_chip = (2 * lax.axis_index("x") + lax.axis_index("y")).astype(jnp.int32).reshape(1)

    small_shapes = [w[n].shape for n in _SMALL_SHARDED]
    big_bf = [_cast_into_slot(f"to_bf16_{n}", mat(w[n]), my_chip) for n in _BIG]
    small_mine = pack_small(w)
    small_slots = lax.dynamic_update_slice(jnp.zeros((N_SHARD,) + small_mine.shape, F32), small_mine[None], (my_chip[0], 0, 0))
    gathered = _weights_gather("weights_gather", big_bf, [small_slots])
    ada_st, w_in_st, w_hg_st, w_rw_st, w_out_st, small_st = gathered
    full_small = {}
    per_chip = [_unpack(small_st[j], small_shapes) for j in range(N_SHARD)]
    for i, n in enumerate(_SMALL_SHARDED):
        full_small[n] = _join_shards(jnp.stack([per_chip[j][i] for j in range(N_SHARD)], axis=0))
    dm = x.shape[-1]
    w_out_full = w_out_st.reshape(dm, dm)

    loss_b, grad_x, g = _local_step(
        x[0], c, ctx[0], c_ctx, ada_st, ada_b, norm_g, w_in_st, full_small["hg_lb"], hg_norm_g, full_small["rw_mu"][0],
        full_small["rw_w0"][0], full_small["rw_w2"][0], full_small["rw_a0"][0], full_small["rw_a2"][0], rw_kk, rw_ka, rw_rk,
        rw_gn_g, rw_gn_b, w_hg_st, w_rw_st, w_out_full, final_g, loss_target[0])
    loss = lax.psum(loss_b[0, 0], ("x", "y", "c"))

    g_small = {"hg_lb": g["hg_lb"], "rw_mu": g["rw_mu"][None], "rw_w0": g["rw_w0"][None], "rw_w2": g["rw_w2"][None],
               "rw_a0": g["rw_a0"][None], "rw_a2": g["rw_a2"][None]}
    split = {n: _split_shards(g_small[n]) for n in _SMALL_SHARDED}
    small_parts = jnp.stack([pack_small({n: split[n][j] for n in _SMALL_SHARDED}) for j in range(N_SHARD)], axis=0)
    partial = [g["ada_w"], g["w_in"], g["w_hg_out"], g["w_rw_out"], g["w_out"].reshape(N_SHARD, dm // N_SHARD, dm), small_parts]
    chip_sums = [_pair_exchange(f"grads_pair_sum{i}", a, my_core, True, BF16) for i, a in enumerate(partial)]
    landed = _chip_scatter("grads_scatter", chip_sums)
    mine = [_sum_landed(f"grads_sum{i}", a, s, my_chip) for i, (a, s) in enumerate(zip(landed, chip_sums))]
    theirs = [_pair_exchange(f"grads_pair_swap{i}", a, my_core, False, F32) for i, a in enumerate(mine)]
    rep_shapes = [w[n].shape for n in _REPLICATED]
    rep_all = _gather_all("grads_replicated", _pack([g[n].reshape(w[n].shape) for n in _REPLICATED]))

    res = {}
    for i, n in enumerate(_BIG):
        outs = _adamw_halves(f"adamw_{n}", mat(w[n]), mat(m[n]), mat(v[n]), mine[i], theirs[i], my_core)
        res[n] = [o.reshape(w[n].shape) for o in outs]
    outs = _adamw_halves("adamw_small", small_mine, pack_small(m), pack_small(v), mine[len(_BIG)], theirs[len(_BIG)], my_core)
    for i, vals in enumerate(zip(*[_unpack(o, small_shapes) for o in outs])):
        res[_SMALL_SHARDED[i]] = list(vals)
    outs = _adamw("adamw_replicated", _pack([w[n] for n in _REPLICATED]), _pack([m[n] for n in _REPLICATED]),
                  _pack([v[n] for n in _REPLICATED]), rep_all)
    for i, vals in enumerate(zip(*[_unpack(o, rep_shapes) for o in outs])):
        res[_REPLICATED[i]] = list(vals)

    return (loss, grad_x[None], *[res[n][0] for n in _WEIGHTS], *[res[n][1] for n in _WEIGHTS],
            *[res[n][2] for n in _WEIGHTS], *[res[n][3] for n in _WEIGHTS])
```

```python
import functools

import jax
import jax.numpy as jnp
from jax import lax
from jax.experimental import pallas as pl
from jax.experimental.pallas import tpu as pltpu

HI = lax.Precision.HIGHEST
F32 = jnp.float32
BF16 = jnp.bfloat16

NORM_EPS = 1e-6
HG_HEAD = 128
RW_HEAD = 64
RW_LORA = 64
RW_GN_EPS = 64e-5
GRID_W = 64
SUB = 16
STEP = 64
N_SHARD = 4
N_DEV = 8
LANE = 128

ADAM_LR = 0.001
ADAM_B1 = 0.9
ADAM_B2 = 0.999
ADAM_EPS = 1e-08
ADAM_WD = 0.01
ADAM_STEP = 10

VMEM_LIMIT = 56 * 1024 * 1024


def _params(sem=None):
    return pltpu.CompilerParams(dimension_semantics=sem, vmem_limit_bytes=VMEM_LIMIT)


def _tile(n, cands):
    for c in cands:
        if n % c == 0:
            return c
    return n


def _iota2(n, m, d):
    return lax.broadcasted_iota(jnp.int32, (n, m), d)


def _before(n, rev, strict):
    t, s = _iota2(n, n, 0), _iota2(n, n, 1)
    if rev:
        return (s > t) if strict else (s >= t)
    return (s < t) if strict else (s <= t)


def _bdot(a, b, spec):
    return jnp.einsum(spec, a, b, precision=HI, preferred_element_type=F32)


def _sdot(a, b, spec):
    return jnp.einsum(spec, a, b, precision=lax.Precision.DEFAULT, preferred_element_type=F32)


def _hg_step(s0, qraw, iin, fin, lb2, rev):
    c, w = qraw.shape
    h = w // HG_HEAD
    nsub = c // SUB
    lb = jax.nn.sigmoid(lb2[0:1] - lb2[1:2])
    q = jax.nn.silu(qraw)
    fg = lb + (1.0 - lb) * jax.nn.sigmoid(fin)
    kk = 1.0 - fg
    g = jnp.log(fg)
    incl = _before(SUB, rev, False).astype(F32)
    rows = lax.broadcasted_iota(jnp.int32, (SUB, 1), 0)
    last = 0 if rev else SUB - 1

    def heads(a):
        return jnp.swapaxes(a.reshape(a.shape[0], h, HG_HEAD), 0, 1)

    s = s0
    outs = [None] * nsub
    order = range(nsub - 1, -1, -1) if rev else range(nsub)
    for j in order:
        sl = slice(j * SUB, (j + 1) * SUB)
        qs, ks, vs, gs = q[sl], kk[sl], iin[sl], g[sl]
        bc = jnp.dot(incl, gs, precision=HI, preferred_element_type=F32)
        o = jnp.zeros((SUB, h, HG_HEAD), F32)
        for si in range(SUB):
            dec = jnp.exp(jnp.minimum(bc - bc[si:si + 1], 0.0))
            a = (qs * ks[si:si + 1] * dec).reshape(SUB, h, HG_HEAD).sum(-1)
            valid = (rows <= si) if rev else (rows >= si)
            a = jnp.where(valid, a, 0.0)
            o = o + a[:, :, None] * vs[si:si + 1].reshape(1, h, HG_HEAD)
        o = o.reshape(SUB, w)
        o = o + jnp.swapaxes(_bdot(heads(qs * jnp.exp(bc)), s, 'htk,hvk->htv'), 0, 1).reshape(SUB, w)
        blast = bc[last:last + 1]
        kdec = ks * jnp.exp(blast - bc)
        s = heads(jnp.exp(blast)) * s + _bdot(heads(vs), heads(kdec), 'hsv,hsk->hvk')
        outs[j] = o
    return jnp.concatenate(outs, axis=0), s


def _tri_solve(lmat, rhs, rev):
    hh, c, _ = lmat.shape
    nb = c // SUB
    eye = (_iota2(SUB, SUB, 0) == _iota2(SUB, SUB, 1)).astype(F32)
    rowid = lax.broadcasted_iota(jnp.int32, (1, SUB, 1), 1)
    diag = jnp.concatenate([lmat[:, i * SUB:(i + 1) * SUB, i * SUB:(i + 1) * SUB] for i in range(nb)], axis=0)
    tinv = jnp.broadcast_to(eye[None], diag.shape)
    order = range(SUB - 2, -1, -1) if rev else range(1, SUB)
    for t in order:
        row = eye[t:t + 1][None] - (diag[:, t, :][:, :, None] * tinv).sum(axis=1, keepdims=True)
        tinv = jnp.where(rowid == t, row, tinv)
    p = [None] * nb
    done = []
    for i in (range(nb - 1, -1, -1) if rev else range(nb)):
        r = rhs[:, i * SUB:(i + 1) * SUB]
        for m in done:
            r = r - _sdot(lmat[:, i * SUB:(i + 1) * SUB, m * SUB:(m + 1) * SUB], p[m], 'hts,hsv->htv')
        p[i] = _sdot(tinv[i * hh:(i + 1) * hh], r, 'hts,hsv->htv')
        done.append(i)
    return jnp.concatenate(p, axis=1)


def _rw_step(s0, r, k, v, wlo, alo, w0h, w2h, a0h, a2h, kkh, kah, rev):
    hh, c, _ = r.shape
    tl = jnp.broadcast_to(jnp.tanh(wlo)[None], (hh, c, wlo.shape[1]))
    al = jnp.broadcast_to(alo[None], (hh, c, alo.shape[1]))
    wlog = -jax.nn.softplus(-(w0h + _sdot(tl, w2h, 'hcl,hlj->hcj'))) - 0.5
    lw = -jnp.exp(wlog)
    a = jax.nn.sigmoid(a0h + _sdot(al, a2h, 'hcl,hlj->hcj'))
    kk = k * kkh
    kk = kk * lax.rsqrt(jnp.sum(kk * kk, axis=-1, keepdims=True) + 1e-12)
    kd = k * (1.0 + (a - 1.0) * kah)
    b = kk * a
    incl = jnp.broadcast_to(_before(c, rev, False).astype(F32)[None], (hh, c, c))
    cum = _bdot(incl, lw, 'hts,hsk->htk')
    ecum, encum = jnp.exp(cum), jnp.exp(-cum)
    alpha = jnp.exp(cum - lw) * kk
    beta = b * encum
    kappa = kd * encum
    rho = r * ecum
    m_lt = _before(c, rev, True)[None]
    m_le = _before(c, rev, False)[None]
    a_kap = jnp.where(m_lt, _sdot(alpha, kappa, 'htk,hsk->hts'), 0.0)
    a_bet = jnp.where(m_lt, _sdot(alpha, beta, 'htk,hsk->hts'), 0.0)
    b_kap = jnp.where(m_le, _sdot(rho, kappa, 'htk,hsk->hts'), 0.0)
    b_bet = jnp.where(m_le, _sdot(rho, beta, 'htk,hsk->hts'), 0.0)
    rhs = _sdot(alpha, s0, 'htk,hvk->htv') + _sdot(a_kap, v, 'hts,hsv->htv')
    p = _tri_solve(a_bet, rhs, rev)
    y = _sdot(rho, s0, 'htk,hvk->htv') + _sdot(b_kap, v, 'hts,hsv->htv') - _sdot(b_bet, p, 'hts,hsv->htv')
    stil = s0 + _sdot(v, kappa, 'hsv,hsk->hvk') - _sdot(p, beta, 'hsv,hsk->hvk')
    last = 0 if rev else c - 1
    return y, stil * ecum[:, last:last + 1, :]


def _fn_h(s, norm_g, scale, shift):
    return s * lax.rsqrt(jnp.mean(s * s, axis=-1, keepdims=True) + NORM_EPS) * norm_g * (1.0 + scale) + shift


def _fn_hgpost(of, ob, z, g):
    tm, w = of.shape
    o = (of + ob).reshape(tm, w // HG_HEAD, HG_HEAD)
    o = o * lax.rsqrt(jnp.mean(o * o, axis=-1, keepdims=True) + NORM_EPS)
    return o.reshape(tm, w) * g * jax.nn.silu(z)


def _fn_rwpost(y0, y1, r, k, v, alo, z, a0, a2, k_a, r_k, gn_g, gn_b):
    tm, w = r.shape
    nh = w // RW_HEAD
    asum = 0.0
    for d in range(2):
        asum = asum + jax.nn.sigmoid(a0[d:d + 1] + jnp.dot(alo[:, d * RW_LORA:(d + 1) * RW_LORA], a2[d],
                                                           precision=HI, preferred_element_type=F32))
    k_sum = k * (2.0 + (asum - 2.0) * k_a)
    ys = (y0 + y1).reshape(tm, nh, RW_HEAD)
    mean = jnp.mean(ys, axis=-1, keepdims=True)
    var = jnp.mean(jnp.square(ys - mean), axis=-1, keepdims=True)
    y = ((ys - mean) * lax.rsqrt(var + RW_GN_EPS)).reshape(tm, w) * gn_g + gn_b
    bonus = jnp.sum((r * k_sum * r_k).reshape(tm, nh, RW_HEAD), axis=-1, keepdims=True) * v.reshape(tm, nh, RW_HEAD)
    return (y + bonus.reshape(tm, w)) * jax.nn.silu(z)


def _fn_merge(a, b, ghg, grw):
    return jax.nn.sigmoid(ghg) * a + jax.nn.sigmoid(grw) * b


def _fn_final(xs, o, gate, final_g, tgt):
    x2 = xs + gate * o
    y = x2 * lax.rsqrt(jnp.mean(x2 * x2, axis=-1, keepdims=True) + NORM_EPS) * final_g
    return 0.5 * jnp.sum(jnp.mean(jnp.square(y - tgt), axis=-1))


def _row_call(name, fn, n_tiles, tm, row_ins, full_ins, row_outs, acc_outs):
    n_ri, n_fi, n_ro = len(row_ins), len(full_ins), len(row_outs)

    def body(*refs):
        i = pl.program_id(0)
        rvals = [r[...] for r in refs[:n_ri]]
        fvals = [r[...] for r in refs[n_ri:n_ri + n_fi]]
        outs = refs[n_ri + n_fi:]
        ro, ao = fn(i, rvals, fvals)
        for ref, val in zip(outs[:n_ro], ro):
            ref[...] = val.astype(ref.dtype)
        for ref, val in zip(outs[n_ro:], ao):
            @pl.when(i == 0)
            def _(ref=ref):
                ref[...] = jnp.zeros_like(ref)
            ref[...] += val.astype(ref.dtype)

    def rspec(width, cb, off):
        return pl.BlockSpec((tm, width), lambda i: (jnp.maximum(i - off, 0), cb))

    def fspec(shape):
        nd = len(shape)
        return pl.BlockSpec(shape, lambda i: (0,) * nd)

    in_specs = [rspec(w, cb, off) for (_, cb, w, off) in row_ins] + [fspec(a.shape) for a in full_ins]
    out_specs = [rspec(w, 0, off) for (_, w, _, off) in row_outs] + [fspec(s) for (s, _) in acc_outs]
    out_shape = [jax.ShapeDtypeStruct((rows, w), dt) for (rows, w, dt, _) in row_outs] + \
                [jax.ShapeDtypeStruct(s, dt) for (s, dt) in acc_outs]
    res = pl.pallas_call(
        body, name=name, grid=(n_tiles,), in_specs=in_specs, out_specs=out_specs, out_shape=out_shape,
        compiler_params=_params(("arbitrary",)),
    )(*[a for (a, _, _, _) in row_ins], *full_ins)
    return list(res)


def _mm(name, a, b, m, n, k_steps, tm, tn, a_block, a_map, b_block, b_map, o_shape, o_block, o_map,
        contract, out_dtype=F32):
    def body(a_ref, b_ref, o_ref, acc_ref):
        kk = pl.program_id(2)

        @pl.when(kk == 0)
        def _():
            acc_ref[...] = jnp.zeros_like(acc_ref)

        acc_ref[...] += lax.dot_general(a_ref[...].astype(BF16), b_ref[...].astype(BF16),
                                        (contract, ((), ())), preferred_element_type=F32)

        @pl.when(kk == k_steps - 1)
        def _():
            o_ref[...] = acc_ref[...].astype(o_ref.dtype)

    return pl.pallas_call(
        body, name=name, grid=(m // tm, n // tn, k_steps),
        in_specs=[pl.BlockSpec(a_block, a_map), pl.BlockSpec(b_block, b_map)],
        out_specs=pl.BlockSpec(o_block, o_map),
        out_shape=jax.ShapeDtypeStruct(o_shape, out_dtype),
        scratch_shapes=[pltpu.VMEM((tm, tn), F32)],
        compiler_params=_params(("parallel", "parallel", "arbitrary")),
    )(a, b)


_TM = (768, 512, 256, 128, 64, 32, 16, 8)
_TN = (512, 256, 128)
_TK = (1024, 768, 512, 256, 128)
_TK_WIDE = (768, 512, 256, 128)
WIDE_OUT_BYTES = 32 << 20


def _tm_wide(m, ns):
    for tm in _TM:
        if m % tm == 0 and 3 * 4 * tm * ns <= WIDE_OUT_BYTES:
            return tm
    return m


def _mm_nn(name, a, b, out_dtype=F32):
    m, k = a.shape
    n = b.shape[1]
    tm, tn, tk = _tile(m, _TM), _tile(n, _TN), _tile(k, _TK)
    return _mm(name, a, b, m, n, k // tk, tm, tn, (tm, tk), lambda i, j, s: (i, s), (tk, tn), lambda i, j, s: (s, j),
               (m, n), (tm, tn), lambda i, j, s: (i, j), ((1,), (0,)), out_dtype)


def _mm_nt(name, a, b, out_dtype=F32):
    m, k = a.shape
    n = b.shape[0]
    tm, tn, tk = _tile(m, _TM), _tile(n, _TN), _tile(k, _TK)
    return _mm(name, a, b, m, n, k // tk, tm, tn, (tm, tk), lambda i, j, s: (i, s), (tn, tk), lambda i, j, s: (j, s),
               (m, n), (tm, tn), lambda i, j, s: (i, j), ((1,), (1,)), out_dtype)


def _mm_tn(name, a, b, out_dtype=F32):
    k, m = a.shape
    n = b.shape[1]
    tm, tn, tk = _tile(m, _TM), _tile(n, _TN), _tile(k, _TK)
    return _mm(name, a, b, m, n, k // tk, tm, tn, (tk, tm), lambda i, j, s: (s, i), (tk, tn), lambda i, j, s: (s, j),
               (m, n), (tm, tn), lambda i, j, s: (i, j), ((0,), (0,)), out_dtype)


def _mm_n_st(name, a, bst, out_dtype=F32):
    m, k = a.shape
    ns_, _, ns = bst.shape
    tm, tk = _tm_wide(m, ns), _tile(k, (512, 256, 128))
    return _mm(name, a, bst, m, ns_ * ns, k // tk, tm, ns,
               (tm, tk), lambda i, j, s: (i, s), (None, tk, ns), lambda i, j, s: (j, s, 0),
               (ns_, m, ns), (None, tm, ns), lambda i, j, s: (j, i, 0), ((1,), (0,)), out_dtype)


def _mm_st_t(name, ast, bst, out_dtype=F32):
    ns_, m, ns = ast.shape
    n = bst.shape[1]
    tm, tn = _tile(m, _TM), _tile(n, _TN)
    return _mm(name, ast, bst, m, n, ns_, tm, tn,
               (None, tm, ns), lambda i, j, s: (s, i, 0), (None, tn, ns), lambda i, j, s: (s, j, 0),
               (m, n), (tm, tn), lambda i, j, s: (i, j), ((1,), (1,)), out_dtype)


def _mm_t_st(name, a, bst, out_dtype=F32):
    k, m = a.shape
    ns_, _, ns = bst.shape
    tm, tk = _tile(m, _TN), _tile(k, _TK_WIDE)
    return _mm(name, a, bst, m, ns_ * ns, k // tk, tm, ns,
               (tk, tm), lambda i, j, s: (s, i), (None, tk, ns), lambda i, j, s: (j, s, 0),
               (ns_, m, ns), (None, tm, ns), lambda i, j, s: (j, i, 0), ((0,), (0,)), out_dtype)


def _scan_order(j, n_ctx, n_all, rev):
    if not rev:
        return j
    return jnp.where(j < n_ctx, n_ctx - 1 - j, n_all - 1 - (j - n_ctx))


def _hg_scan_fwd(name, p_hg, lb2, d, n_ctx):
    t, w5 = p_hg.shape
    w = w5 // 5
    h = w // HG_HEAD
    n = t // STEP
    rev = d == 1

    def body(q_ref, i_ref, f_ref, lb_ref, o_ref, st_ref, s_ref):
        j = pl.program_id(0)

        @pl.when(j == 0)
        def _():
            s_ref[...] = jnp.zeros_like(s_ref)

        s0 = s_ref[...]
        st_ref[...] = s0
        o, s1 = _hg_step(s0, q_ref[...], i_ref[...], f_ref[...], lb_ref[...], rev)
        o_ref[...] = o
        s_ref[...] = s1

    def rows(cb):
        return pl.BlockSpec((STEP, w), lambda j: (_scan_order(j, n_ctx, n, rev), cb))

    return pl.pallas_call(
        body, name=name, grid=(n,),
        in_specs=[rows(0), rows(1), rows(2 + d), pl.BlockSpec((2, w), lambda j: (0, 0))],
        out_specs=[rows(0), pl.BlockSpec((None, h, HG_HEAD, HG_HEAD), lambda j: (j, 0, 0, 0))],
        out_shape=[jax.ShapeDtypeStruct((t, w), F32), jax.ShapeDtypeStruct((n, h, HG_HEAD, HG_HEAD), F32)],
        scratch_shapes=[pltpu.VMEM((h, HG_HEAD, HG_HEAD), F32)],
        compiler_params=_params(("arbitrary",)),
    )(p_hg, p_hg, p_hg, lb2)


def _hg_scan_bwd(name, p_hg, lb2, states, do, d, n_ctx):
    t, w5 = p_hg.shape
    w = w5 // 5
    h = w // HG_HEAD
    n = t // STEP
    rev = d == 1

    def body(q_ref, i_ref, f_ref, lb_ref, st_ref, do_ref, dq_ref, di_ref, df_ref, dlb_ref, ds_ref):
        step = pl.program_id(0)

        @pl.when(step == 0)
        def _():
            ds_ref[...] = jnp.zeros_like(ds_ref)
            dlb_ref[...] = jnp.zeros_like(dlb_ref)

        _, vjp = jax.vjp(lambda s0, q, i, f, lb: _hg_step(s0, q, i, f, lb, rev),
                         st_ref[...], q_ref[...], i_ref[...], f_ref[...], lb_ref[...])
        ds0, dq, di, df, dlb = vjp((do_ref[...], ds_ref[...]))
        dq_ref[...] = dq
        di_ref[...] = di
        df_ref[...] = df
        dlb_ref[...] += dlb
        ds_ref[...] = ds0

    def rows(cb):
        return pl.BlockSpec((STEP, w), lambda s: (_scan_order(n - 1 - s, n_ctx, n, rev), cb))

    return pl.pallas_call(
        body, name=name, grid=(n,),
        in_specs=[rows(0), rows(1), rows(2 + d), pl.BlockSpec((2, w), lambda s: (0, 0)),
                  pl.BlockSpec((None, h, HG_HEAD, HG_HEAD), lambda s: (n - 1 - s, 0, 0, 0)), rows(0)],
        out_specs=[rows(0), rows(0), rows(0), pl.BlockSpec((2, w), lambda s: (0, 0))],
        out_shape=[jax.ShapeDtypeStruct((t, w), F32)] * 3 + [jax.ShapeDtypeStruct((2, w), F32)],
        scratch_shapes=[pltpu.VMEM((h, HG_HEAD, HG_HEAD), F32)],
        compiler_params=_params(("arbitrary",)),
    )(p_hg, p_hg, p_hg, lb2, states, do)


def _to_heads(a, nh):
    return jnp.stack([a[:, i * RW_HEAD:(i + 1) * RW_HEAD] for i in range(nh)], axis=0)


def _from_heads(a):
    return jnp.concatenate([a[i] for i in range(a.shape[0])], axis=-1)


def _rw_scan_fwd(name, sh, hp, d, n_ctx):
    t = sh.shape[0]
    w = (sh.shape[1] - 4 * RW_LORA) // 3
    nh = w // RW_HEAD
    n = t // STEP
    rev = d == 1
    lo = 3 * w // LANE

    def body(r_ref, k_ref, v_ref, wl_ref, al_ref, w0_ref, w2_ref, a0_ref, a2_ref, kk_ref, ka_ref,
             y_ref, st_ref, s_ref):
        j = pl.program_id(0)

        @pl.when(j == 0)
        def _():
            s_ref[...] = jnp.zeros_like(s_ref)

        s0 = s_ref[...]
        st_ref[...] = s0
        wl = wl_ref[...][:, d * RW_LORA:(d + 1) * RW_LORA]
        al = al_ref[...][:, d * RW_LORA:(d + 1) * RW_LORA]
        y, s1 = _rw_step(s0, _to_heads(r_ref[...], nh), _to_heads(k_ref[...], nh), _to_heads(v_ref[...], nh), wl, al,
                         w0_ref[...], w2_ref[...], a0_ref[...], a2_ref[...], kk_ref[...], ka_ref[...], rev)
        y_ref[...] = _from_heads(y)
        s_ref[...] = s1

    def rows(cb, width=w):
        return pl.BlockSpec((STEP, width), lambda j: (_scan_order(j, n_ctx, n, rev), cb))

    def whole(a):
        nd = a.ndim
        return pl.BlockSpec(a.shape, lambda j: (0,) * nd)

    return pl.pallas_call(
        body, name=name, grid=(n,),
        in_specs=[rows(0), rows(1), rows(2), rows(lo, LANE), rows(lo + 1, LANE)] + [whole(a) for a in hp],
        out_specs=[rows(0), pl.BlockSpec((None, nh, RW_HEAD, RW_HEAD), lambda j: (j, 0, 0, 0))],
        out_shape=[jax.ShapeDtypeStruct((t, w), F32), jax.ShapeDtypeStruct((n, nh, RW_HEAD, RW_HEAD), F32)],
        scratch_shapes=[pltpu.VMEM((nh, RW_HEAD, RW_HEAD), F32)],
        compiler_params=_params(("arbitrary",)),
    )(sh, sh, sh, sh, sh, *hp)


def _rw_scan_bwd(name, sh, hp, states, dy, d, n_ctx):
    t = sh.shape[0]
    w = (sh.shape[1] - 4 * RW_LORA) // 3
    nh = w // RW_HEAD
    n = t // STEP
    rev = d == 1
    lo = 3 * w // LANE

    def body(r_ref, k_ref, v_ref, wl_ref, al_ref, w0_ref, w2_ref, a0_ref, a2_ref, kk_ref, ka_ref, st_ref, dy_ref,
             dm_ref, dl_ref, dw0_ref, dw2_ref, da0_ref, da2_ref, dkk_ref, dka_ref, ds_ref):
        step = pl.program_id(0)
        pouts = (dw0_ref, dw2_ref, da0_ref, da2_ref, dkk_ref, dka_ref)

        @pl.when(step == 0)
        def _():
            ds_ref[...] = jnp.zeros_like(ds_ref)
            for ref in pouts:
                ref[...] = jnp.zeros_like(ref)

        wl = wl_ref[...][:, d * RW_LORA:(d + 1) * RW_LORA]
        al = al_ref[...][:, d * RW_LORA:(d + 1) * RW_LORA]
        _, vjp = jax.vjp(functools.partial(_rw_step, rev=rev),
                         st_ref[...], _to_heads(r_ref[...], nh), _to_heads(k_ref[...], nh), _to_heads(v_ref[...], nh),
                         wl, al, w0_ref[...], w2_ref[...], a0_ref[...], a2_ref[...], kk_ref[...], ka_ref[...])
        g = vjp((_to_heads(dy_ref[...], nh), ds_ref[...]))
        ds_ref[...] = g[0]
        dm_ref[...] = jnp.concatenate([_from_heads(g[1]), _from_heads(g[2]), _from_heads(g[3])], axis=-1)
        zero = jnp.zeros_like(g[4])
        parts = [zero] * 4
        parts[d], parts[2 + d] = g[4], g[5]
        dl_ref[...] = jnp.concatenate(parts, axis=-1)
        for ref, val in zip(pouts, g[6:]):
            ref[...] += val

    def rows(cb, width=w):
        return pl.BlockSpec((STEP, width), lambda s: (_scan_order(n - 1 - s, n_ctx, n, rev), cb))

    def whole(a):
        nd = a.ndim
        return pl.BlockSpec(a.shape, lambda s: (0,) * nd)

    return pl.pallas_call(
        body, name=name, grid=(n,),
        in_specs=[rows(0), rows(1), rows(2), rows(lo, LANE), rows(lo + 1, LANE)] + [whole(a) for a in hp] +
                 [pl.BlockSpec((None, nh, RW_HEAD, RW_HEAD), lambda s: (n - 1 - s, 0, 0, 0)), rows(0)],
        out_specs=[rows(0, 3 * w), rows(0, 4 * RW_LORA)] + [whole(a) for a in hp],
        out_shape=[jax.ShapeDtypeStruct((t, 3 * w), F32), jax.ShapeDtypeStruct((t, 4 * RW_LORA), F32)] +
                  [jax.ShapeDtypeStruct(a.shape, F32) for a in hp],
        scratch_shapes=[pltpu.VMEM((nh, RW_HEAD, RW_HEAD), F32)],
        compiler_params=_params(("arbitrary",)),
    )(sh, sh, sh, sh, sh, *hp, states, dy)


def _shift_masks(t, n_ctx_rows):
    row = lax.broadcasted_iota(jnp.int32, (t, 1), 0)
    isx = row >= n_ctx_rows
    pos = jnp.where(isx, row - n_ctx_rows, row)
    col = jnp.where(isx, jnp.bitwise_and(pos, GRID_W - 1), pos)
    ncol = jnp.where(isx, GRID_W, n_ctx_rows)
    n_x = t - n_ctx_rows
    ml = col != 0
    mr = col != ncol - 1
    mu = isx & (pos >= GRID_W)
    md = isx & (pos < n_x - GRID_W)
    return ml, mr, mu, md, isx


def _shift_fwd(name, p, mu, n_ctx_rows):
    t, c = p.shape
    cw = LANE

    def body(p_ref, mu_ref, o_ref):
        x = p_ref[...]
        m = mu_ref[...]
        ml, mr, mup, mdn, isx = _shift_masks(t, n_ctx_rows)
        left = jnp.where(ml, pltpu.roll(x, 1, 0), 0.0)
        right = jnp.where(mr, pltpu.roll(x, t - 1, 0), 0.0)
        up = jnp.where(mup, pltpu.roll(x, GRID_W, 0), 0.0)
        down = jnp.where(mdn, pltpu.roll(x, t - GRID_W, 0), 0.0)
        out = x + m[0:1] * (left - x) + m[1:2] * (right - x)
        vert = m[2:3] * (up - x) + m[3:4] * (down - x)
        o_ref[...] = out + jnp.where(isx, vert, 0.0)

    return pl.pallas_call(
        body, name=name, grid=(c // cw,),
        in_specs=[pl.BlockSpec((t, cw), lambda j: (0, j)), pl.BlockSpec((4, cw), lambda j: (0, j))],
        out_specs=pl.BlockSpec((t, cw), lambda j: (0, j)),
        out_shape=jax.ShapeDtypeStruct((t, c), F32),
        compiler_params=_params(("parallel",)),
    )(p, mu)


def _shift_bwd(name, p, mu, dparts, n_ctx_rows):
    t, c = p.shape
    cw = LANE
    npart = len(dparts)

    def body(*refs):
        p_ref, mu_ref = refs[0], refs[1]
        dp_ref, dmu_ref = refs[2 + npart], refs[3 + npart]
        x = p_ref[...]
        m = mu_ref[...]
        g = refs[2][...]
        for r in refs[3:2 + npart]:
            g = g + r[...]
        ml, mr, mup, mdn, isx = _shift_masks(t, n_ctx_rows)
        left = jnp.where(ml, pltpu.roll(x, 1, 0), 0.0)
        right = jnp.where(mr, pltpu.roll(x, t - 1, 0), 0.0)
        up = jnp.where(mup, pltpu.roll(x, GRID_W, 0), 0.0)
        down = jnp.where(mdn, pltpu.roll(x, t - GRID_W, 0), 0.0)
        gx = jnp.where(isx, g, 0.0)
        dmu_ref[...] = jnp.concatenate([
            jnp.sum(g * (left - x), axis=0, keepdims=True), jnp.sum(g * (right - x), axis=0, keepdims=True),
            jnp.sum(gx * (up - x), axis=0, keepdims=True), jnp.sum(gx * (down - x), axis=0, keepdims=True)], axis=0)
        coef = 1.0 - m[0:1] - m[1:2] - jnp.where(isx, m[2:3] + m[3:4], 0.0)
        dp = coef * g
        dp = dp + m[0:1] * pltpu.roll(jnp.where(ml, g, 0.0), t - 1, 0)
        dp = dp + m[1:2] * pltpu.roll(jnp.where(mr, g, 0.0), 1, 0)
        dp = dp + m[2:3] * pltpu.roll(jnp.where(mup, g, 0.0), t - GRID_W, 0)
        dp = dp + m[3:4] * pltpu.roll(jnp.where(mdn, g, 0.0), GRID_W, 0)
        dp_ref[...] = dp

    col = pl.BlockSpec((t, cw), lambda j: (0, j))
    par = pl.BlockSpec((4, cw), lambda j: (0, j))
    return pl.pallas_call(
        body, name=name, grid=(c // cw,),
        in_specs=[col, par] + [col] * npart,
        out_specs=[col, par],
        out_shape=[jax.ShapeDtypeStruct((t, c), F32), jax.ShapeDtypeStruct((4, c), F32)],
        compiler_params=_params(("parallel",)),
    )(p, mu, *dparts)


def _local_step(x, c, ctx, c_ctx, ada_st, ada_b, norm_g, w_in_st, hg_lb, hg_norm_g, rw_mu, rw_w0, rw_w2, rw_a0, rw_a2,
                rw_kk, rw_ka, rw_rk, rw_gn_g, rw_gn_b, w_hg_st, w_rw_st, w_out, final_g, tgt):
    seq, dm = x.shape
    n_ctx_rows = ctx.shape[0]
    t = seq + n_ctx_rows
    hw = hg_norm_g.shape[-1]
    rw = rw_kk.shape[-1]
    nh_rw = rw // RW_HEAD
    n_ctx = n_ctx_rows // STEP
    tm = _tile(n_ctx_rows, (256, 128, 64))
    nt = t // tm
    nct = n_ctx_rows // tm
    n_sh_cols = 3 * rw + 4 * RW_LORA

    xs = jnp.concatenate([ctx, x], axis=0)
    cond = jnp.concatenate([c.reshape(1, dm), c_ctx.reshape(1, dm), jnp.zeros((6, dm), F32)], axis=0)
    final_g2 = final_g.reshape(1, dm)

    def unstack(a_st):
        return jnp.swapaxes(a_st, 0, 1).reshape(a_st.shape[1], -1)

    def restack(a, ns=N_SHARD):
        return jnp.swapaxes(a.reshape(a.shape[0], ns, -1), 0, 1)

    (sc,) = _row_call("cond_silu", lambda i, r, f: ([jax.nn.silu(r[0])], []), 1, 8, [(cond, 0, dm, 0)], [],
                      [(8, dm, F32, 0)], [])
    mod_st = _mm_n_st("mod_mm", sc, ada_st)
    mod = unstack(mod_st) + ada_b
    mod3 = mod.reshape(8, 3, dm)

    def pick(i, m3):
        r = jnp.where(i < nct, m3[1], m3[0])
        return r[0:1], r[1:2]

    def h_fn(i, r, f):
        shift, scale = pick(i, f[1])
        return [_fn_h(r[0], f[0], scale, shift)], []

    (h,) = _row_call("h_fwd", h_fn, nt, tm, [(xs, 0, dm, 0)], [norm_g, mod3], [(t, dm, BF16, 0)], [])
    proj = unstack(_mm_n_st("proj_mm", h, w_in_st))
    p_hg = proj[:, :5 * hw]
    p_rs = proj[:, 5 * hw:5 * hw + n_sh_cols]
    p_zr = proj[:, 5 * hw + n_sh_cols:5 * hw + n_sh_cols + rw]
    p_gt = proj[:, 5 * hw + n_sh_cols + rw:]

    o_hg, st_hg = [], []
    for d in range(2):
        o, st = _hg_scan_fwd(f"hg_scan_fwd{d}", p_hg, hg_lb[d], d, n_ctx)
        o_hg.append(o)
        st_hg.append(st)

    def hgpost_fn(i, r, f):
        return [_fn_hgpost(r[0], r[1], r[2], f[0])], []

    hg_in = [(o_hg[0], 0, hw, 0), (o_hg[1], 0, hw, 0), (p_hg, 4, hw, 0)]
    (y_hg,) = _row_call("hg_post", hgpost_fn, nt, tm, hg_in, [hg_norm_g], [(t, hw, BF16, 0)], [])

    sh = _shift_fwd("rw_shift", p_rs, rw_mu, n_ctx_rows)
    hps = []
    for d in range(2):
        hps.append([rw_w0[d].reshape(nh_rw, 1, RW_HEAD), jnp.swapaxes(rw_w2[d].reshape(RW_LORA, nh_rw, RW_HEAD), 0, 1),
                    rw_a0[d].reshape(nh_rw, 1, RW_HEAD), jnp.swapaxes(rw_a2[d].reshape(RW_LORA, nh_rw, RW_HEAD), 0, 1),
                    rw_kk.reshape(nh_rw, 1, RW_HEAD), rw_ka.reshape(nh_rw, 1, RW_HEAD)])
    y_rw_d, st_rw = [], []
    for d in range(2):
        y, st = _rw_scan_fwd(f"rw_scan_fwd{d}", sh, hps[d], d, n_ctx)
        y_rw_d.append(y)
        st_rw.append(st)

    rw_full = [rw_a0, rw_a2, rw_ka, rw_rk, rw_gn_g, rw_gn_b]
    lo = 3 * rw // LANE
    rw_in = [(y_rw_d[0], 0, rw, 0), (y_rw_d[1], 0, rw, 0), (sh, 0, rw, 0), (sh, 1, rw, 0), (sh, 2, rw, 0),
             (sh, lo + 1, LANE, 0), (p_zr, 0, rw, 0)]

    def rwpost_fn(i, r, f):
        return [_fn_rwpost(*r, *f)], []

    (y_rw,) = _row_call("rw_post", rwpost_fn, nt, tm, rw_in, rw_full, [(t, rw, BF16, 0)], [])

    a_hg = unstack(_mm_n_st("hg_out_mm", y_hg, w_hg_st))
    a_rw = unstack(_mm_n_st("rw_out_mm", y_rw, w_rw_st))
    mg_in = [(a_hg, 0, dm, 0), (a_rw, 0, dm, 0), (p_gt, 0, dm, 0), (p_gt, 1, dm, 0)]
    (merged,) = _row_call("merge", lambda i, r, f: ([_fn_merge(*r)], []), nt, tm, mg_in, [], [(t, dm, BF16, 0)], [])
    o_out = _mm_nn("out_mm", merged, w_out)

    def final_fn(i, r, f):
        gate = f[0][0][2:3]
        loss, vjp = jax.vjp(_fn_final, r[0], r[1], gate, f[1], r[2])
        dx, do, dgate, dfg, _ = vjp(jnp.ones((), F32))
        live = i >= nct
        zero = lambda a: jnp.where(live, a, 0.0)
        dmod = jnp.concatenate([jnp.concatenate([jnp.zeros((1, 2 * dm), F32), zero(dgate)], axis=1),
                                jnp.zeros((7, 3 * dm), F32)], axis=0)
        return [zero(dx), zero(do)], [jnp.broadcast_to(zero(loss), (8, LANE)), dmod, zero(dfg)]

    fin_in = [(xs, 0, dm, 0), (o_out, 0, dm, 0), (tgt, 0, dm, nct)]
    dx_res, d_o, loss_acc, dmod_gate, d_final_g = _row_call(
        "final", final_fn, nt, tm, fin_in, [mod3, final_g2], [(t, dm, F32, 0), (t, dm, BF16, 0)],
        [((8, LANE), F32), ((8, 3 * dm), F32), ((1, dm), F32)])

    g_w_out = _mm_tn("d_w_out", merged, d_o)
    d_merged = _mm_nt("d_merged", d_o, w_out)

    def merge_bwd(i, r, f):
        _, vjp = jax.vjp(_fn_merge, r[0], r[1], r[2], r[3])
        da, db, dgh, dgr = vjp(r[4])
        return [da, db, jnp.concatenate([dgh, dgr], axis=1)], []

    da_hg, da_rw, dp_gt = _row_call("merge_bwd", merge_bwd, nt, tm, mg_in + [(d_merged, 0, dm, 0)], [],
                                    [(t, dm, BF16, 0), (t, dm, BF16, 0), (t, 2 * dm, F32, 0)], [])
    g_w_hg_st = _mm_t_st("d_w_hg", y_hg, restack(da_hg))
    g_w_rw_st = _mm_t_st("d_w_rw", y_rw, restack(da_rw))
    dy_hg = _mm_st_t("d_y_hg", restack(da_hg), w_hg_st)
    dy_rw = _mm_st_t("d_y_rw", restack(da_rw), w_rw_st)

    def hgpost_bwd(i, r, f):
        _, vjp = jax.vjp(_fn_hgpost, r[0], r[1], r[2], f[0])
        dof, _, dz, dg = vjp(r[3])
        return [dof, dz], [dg]

    do_hg, dz_hg, g_hg_norm = _row_call("hg_post_bwd", hgpost_bwd, nt, tm, hg_in + [(dy_hg, 0, hw, 0)], [hg_norm_g],
                                        [(t, hw, F32, 0), (t, hw, F32, 0)], [((1, hw), F32)])
    dqs, dis, dfs, g_lb = [], [], [], []
    for d in range(2):
        dq, di, df, dlb = _hg_scan_bwd(f"hg_scan_bwd{d}", p_hg, hg_lb[d], st_hg[d], do_hg, d, n_ctx)
        dqs.append(dq)
        dis.append(di)
        dfs.append(df)
        g_lb.append(dlb)
    (dqi,) = _row_call("hg_dsum", lambda i, r, f: ([jnp.concatenate([r[0] + r[1], r[2] + r[3]], axis=1)], []), nt, tm,
                       [(dqs[0], 0, hw, 0), (dqs[1], 0, hw, 0), (dis[0], 0, hw, 0), (dis[1], 0, hw, 0)], [],
                       [(t, 2 * hw, F32, 0)], [])
    g_hg_lb = jnp.stack(g_lb, axis=0)

    def rwpost_bwd(i, r, f):
        _, vjp = jax.vjp(_fn_rwpost, *r[:7], *f)
        g = vjp(r[7])
        zl = jnp.zeros((g[5].shape[0], 2 * RW_LORA), F32)
        dmain = jnp.concatenate([g[2], g[3], g[4]], axis=1)
        return [g[0], dmain, jnp.concatenate([zl, g[5]], axis=1), g[6]], list(g[7:])

    dy_sum, dsh_p, dsl_p, dz_rw, g_a0_p, g_a2_p, g_ka_p, g_rk, g_gn_g, g_gn_b = _row_call(
        "rw_post_bwd", rwpost_bwd, nt, tm, rw_in + [(dy_rw, 0, rw, 0)], rw_full,
        [(t, rw, F32, 0), (t, 3 * rw, F32, 0), (t, 4 * RW_LORA, F32, 0), (t, rw, F32, 0)],
        [(a.shape, F32) for a in rw_full])
    dmains, dloras, hp_grads = [dsh_p], [dsl_p], []
    for d in range(2):
        res = _rw_scan_bwd(f"rw_scan_bwd{d}", sh, hps[d], st_rw[d], dy_sum, d, n_ctx)
        dmains.append(res[0])
        dloras.append(res[1])
        hp_grads.append(res[2:])
    dsh_parts = [jnp.concatenate([m, l], axis=1) for m, l in zip(dmains, dloras)]
    dp_rs, g_mu = _shift_bwd("rw_shift_bwd", p_rs, rw_mu, dsh_parts, n_ctx_rows)

    def flat(a):
        if a.shape[1] == 1:
            return a.reshape(rw)
        return jnp.swapaxes(a, 0, 1).reshape(RW_LORA, rw)

    g_w0 = jnp.stack([flat(hp_grads[d][0]) for d in range(2)], axis=0)
    g_w2 = jnp.stack([flat(hp_grads[d][1]) for d in range(2)], axis=0)
    g_a0 = jnp.stack([flat(hp_grads[d][2]) for d in range(2)], axis=0) + g_a0_p
    g_a2 = jnp.stack([flat(hp_grads[d][3]) for d in range(2)], axis=0) + g_a2_p
    g_kk = (flat(hp_grads[0][4]) + flat(hp_grads[1][4])).reshape(1, rw)
    g_ka = (flat(hp_grads[0][5]) + flat(hp_grads[1][5])).reshape(1, rw) + g_ka_p

    dproj = jnp.concatenate([dqi, dfs[0], dfs[1], dz_hg, dp_rs, dz_rw, dp_gt], axis=1).astype(BF16)
    dproj_st = restack(dproj)
    g_w_in_st = _mm_t_st("d_w_in", h, dproj_st)
    dh = _mm_st_t("d_h", dproj_st, w_in_st)

    def h_bwd(i, r, f):
        shift, scale = pick(i, f[1])
        _, vjp = jax.vjp(_fn_h, r[0], f[0], scale, shift)
        ds, dg, dscale, dshift = vjp(r[1])
        row = jnp.concatenate([dshift, dscale, jnp.zeros((1, dm), F32)], axis=1)
        z = jnp.zeros_like(row)
        is_ctx = i < nct
        dmod = jnp.concatenate([jnp.where(is_ctx, z, row), jnp.where(is_ctx, row, z), jnp.zeros((6, 3 * dm), F32)], axis=0)
        return [ds + r[2]], [dg, dmod]

    grad_x, g_norm_g, dmod_h = _row_call(
        "h_bwd", h_bwd, nt, tm, [(xs, 0, dm, 0), (dh, 0, dm, 0), (dx_res, 0, dm, 0)], [norm_g, mod3],
        [(seq, dm, F32, nct)], [((1, dm), F32), ((8, 3 * dm), F32)])
    dmod = dmod_h + dmod_gate
    g_ada_b = (dmod[0] + dmod[1]).reshape(1, 3 * dm)
    g_ada_st = _mm_t_st("d_ada_w", sc, restack(dmod))
    d_sc = _mm_st_t("d_cond", restack(dmod), ada_st)

    def cond_bwd(i, r, f):
        _, vjp = jax.vjp(jax.nn.silu, r[0])
        return [vjp(r[1])[0]], []

    (d_cond,) = _row_call("cond_bwd", cond_bwd, 1, 8, [(cond, 0, dm, 0), (d_sc, 0, dm, 0)], [], [(8, dm, F32, 0)], [])

    grads = dict(
        c_ctx=d_cond[1], ada_w=g_ada_st, ada_b=g_ada_b, norm_g=g_norm_g, w_in=g_w_in_st, hg_lb=g_hg_lb,
        hg_norm_g=g_hg_norm, rw_mu=g_mu, rw_w0=g_w0, rw_w2=g_w2, rw_a0=g_a0, rw_a2=g_a2, rw_kk=g_kk, rw_ka=g_ka,
        rw_rk=g_rk, rw_gn_g=g_gn_g, rw_gn_b=g_gn_b, w_hg_out=g_w_hg_st, w_rw_out=g_w_rw_st, w_out=g_w_out,
        final_g=d_final_g.reshape(dm))
    return loss_acc[0:1, 0:1], grad_x, grads


def _my_place():
    return lax.axis_index("x"), lax.axis_index("y"), lax.axis_index("c")


MIN_CHUNK_BYTES = 1 << 18
ROW_ALIGN = 16


def _n_chunks(rows, row_bytes):
    for n in (8, 4, 2):
        if rows % (n * ROW_ALIGN) == 0 and rows // n * row_bytes >= MIN_CHUNK_BYTES:
            return n
    return 1


def _row_bytes(a, lead=1):
    n = a.dtype.itemsize
    for d in a.shape[lead:]:
        n *= d
    return n


def _rows(ref, start, size):
    return ref.at[pl.ds(start, size)]


def _chunked(make, start, size, n):
    cs = size // n
    return [make(start + j * cs, cs) for j in range(n)]


_PEER_CHIPS = 3


def _weights_gather(name, big, small):
    nb, na = len(big), len(big) + len(small)
    arrays = list(big) + list(small)

    def body(*refs):
        outs = refs[na:2 * na]
        send_sems, recv_sems, fsend_sems, frecv_sems = refs[2 * na:]
        x, y, c = _my_place()
        me = 2 * x + y
        chips = [(1 - x, y), (x, 1 - y), (1 - x, 1 - y)]

        def over_ici(a, k, slot, r0, nr):
            px, py = chips[k]
            return pltpu.make_async_remote_copy(
                src_ref=_rows(outs[a].at[me], r0, nr), dst_ref=_rows(outs[a].at[slot], r0, nr), send_sem=send_sems.at[a, k],
                recv_sem=recv_sems.at[a, k], device_id=(px, py, c), device_id_type=pl.DeviceIdType.MESH)

        def to_sibling(a, k, r0, nr):
            px, py = chips[k]
            rows = _rows(outs[a].at[2 * px + py], r0, nr)
            return pltpu.make_async_remote_copy(
                src_ref=rows, dst_ref=rows, send_sem=fsend_sems.at[a, k], recv_sem=frecv_sems.at[a, k],
                device_id=(x, y, 1 - c), device_id_type=pl.DeviceIdType.MESH)

        span = []
        for a in range(na):
            rows = arrays[a].shape[1]
            if a < nb:
                half = rows // 2
                span.append((pl.multiple_of(c * half, ROW_ALIGN), pl.multiple_of((1 - c) * half, ROW_ALIGN), half,
                             _n_chunks(half, _row_bytes(arrays[a], 2))))
            else:
                span.append((0, 0, rows, 1))
        for a in range(na):
            mine, _, nr, n = span[a]
            for k in range(_PEER_CHIPS):
                for cp in _chunked(lambda r0, cs: over_ici(a, k, me, r0, cs), mine, nr, n):
                    cp.start()
        for k in range(_PEER_CHIPS):
            px, py = chips[k]
            for a in range(na):
                mine, _, nr, n = span[a]
                over_ici(a, k, 2 * px + py, mine, nr).wait_recv()
                if a < nb:
                    for cp in _chunked(lambda r0, cs: to_sibling(a, k, r0, cs), mine, nr, n):
                        cp.start()
        for k in range(_PEER_CHIPS):
            for a in range(nb):
                _, theirs, nr, _ = span[a]
                to_sibling(a, k, theirs, nr).wait_recv()
        for a in range(na):
            mine, _, nr, _ = span[a]
            for k in range(_PEER_CHIPS):
                over_ici(a, k, me, mine, nr).wait_send()
                if a < nb:
                    to_sibling(a, k, mine, nr).wait_send()

    hbm = pl.BlockSpec(memory_space=pl.ANY)
    sems = pltpu.SemaphoreType.DMA((na, _PEER_CHIPS))
    return pl.pallas_call(
        body, name=name, in_specs=[hbm] * na, out_specs=[hbm] * na,
        out_shape=[jax.ShapeDtypeStruct(a.shape, a.dtype) for a in arrays],
        input_output_aliases={a: a for a in range(na)}, scratch_shapes=[sems, sems, sems, sems],
    )(*arrays)


def _chip_scatter(name, arrays):
    na = len(arrays)

    def body(*refs):
        ins, outs = refs[:na], refs[na:2 * na]
        send_sems, recv_sems = refs[2 * na:]
        x, y, c = _my_place()
        me = 2 * x + y
        chips = [(1 - x, y), (x, 1 - y), (1 - x, 1 - y)]

        def remote(a, k, slot, r0, nr):
            px, py = chips[k]
            return pltpu.make_async_remote_copy(
                src_ref=_rows(ins[a].at[2 * px + py], r0, nr), dst_ref=_rows(outs[a].at[slot], r0, nr),
                send_sem=send_sems.at[a, k], recv_sem=recv_sems.at[a, k], device_id=(px, py, c),
                device_id_type=pl.DeviceIdType.MESH)

        for a in range(na):
            rows = arrays[a].shape[1]
            for k in range(_PEER_CHIPS):
                for cp in _chunked(lambda r0, cs: remote(a, k, me, r0, cs), 0, rows, _n_chunks(rows, _row_bytes(arrays[a], 2))):
                    cp.start()
        for k in range(_PEER_CHIPS):
            px, py = chips[k]
            for a in range(na):
                remote(a, k, 2 * px + py, 0, arrays[a].shape[1]).wait_recv()
        for a in range(na):
            for k in range(_PEER_CHIPS):
                remote(a, k, me, 0, arrays[a].shape[1]).wait_send()

    hbm = pl.BlockSpec(memory_space=pl.ANY)
    sems = pltpu.SemaphoreType.DMA((na, _PEER_CHIPS))
    return pl.pallas_call(
        body, name=name, in_specs=[hbm] * na, out_specs=[hbm] * na,
        out_shape=[jax.ShapeDtypeStruct(a.shape, a.dtype) for a in arrays], scratch_shapes=[sems, sems],
    )(*arrays)


PAIR_TILE_BYTES = 2 << 20


def _pair_exchange(name, a, place, reduce, out_dtype):
    rows, cols = a.shape[-2], a.shape[-1]
    half = rows // 2 if reduce else rows
    tr = _row_tile_for(half, cols, budget=PAIR_TILE_BYTES)
    nh = half // tr
    n_steps = (N_SHARD if reduce else 1) * nh

    def body(pc_ref, *refs):
        if reduce:
            keep_ref, send_ref, o_ref, land, send_sems, recv_sems, credit = refs
        else:
            send_ref, o_ref, land, send_sems, recv_sems, credit = refs
        x, y, c = _my_place()
        other = (x, y, 1 - c)
        t = pl.program_id(0) * nh + pl.program_id(1) if reduce else pl.program_id(0)
        slot = t % 2

        @pl.when(t >= 2)
        def _():
            pl.semaphore_wait(credit, 1)

        copy = pltpu.make_async_remote_copy(
            src_ref=send_ref, dst_ref=land.at[pl.ds(slot, 1)] if reduce else land.at[slot], send_sem=send_sems.at[slot],
            recv_sem=recv_sems.at[slot], device_id=other, device_id_type=pl.DeviceIdType.MESH)
        copy.start()
        copy.wait_recv()
        got = land[slot]
        o_ref[...] = ((keep_ref[...] + got) if reduce else got).astype(out_dtype)
        copy.wait_send()

        @pl.when(t < n_steps - 2)
        def _():
            pl.semaphore_signal(credit, inc=1, device_id=other, device_id_type=pl.DeviceIdType.MESH)

    if reduce:
        grid = (N_SHARD, nh)
        in_specs = [pl.BlockSpec((None, tr, cols), lambda j, i, pc: (j, pc[0] * nh + i, 0)),
                    pl.BlockSpec((1, tr, cols), lambda j, i, pc: (j, (1 - pc[0]) * nh + i, 0))]
        out_spec = pl.BlockSpec((None, tr, cols), lambda j, i, pc: (j, i, 0))
        out_shape = jax.ShapeDtypeStruct((N_SHARD, half, cols), out_dtype)
        operands = (a, a)
        sem = ("arbitrary", "arbitrary")
    else:
        grid = (nh,)
        in_specs = [pl.BlockSpec((tr, cols), lambda i, pc: (i, 0))]
        out_spec = pl.BlockSpec((tr, cols), lambda i, pc: (i, 0))
        out_shape = jax.ShapeDtypeStruct((half, cols), out_dtype)
        operands = (a,)
        sem = ("arbitrary",)
    return pl.pallas_call(
        body, name=name,
        grid_spec=pltpu.PrefetchScalarGridSpec(
            num_scalar_prefetch=1, grid=grid, in_specs=in_specs, out_specs=out_spec,
            scratch_shapes=[pltpu.VMEM((2, tr, cols), a.dtype), pltpu.SemaphoreType.DMA((2,)),
                            pltpu.SemaphoreType.DMA((2,)), pltpu.SemaphoreType.REGULAR]),
        out_shape=out_shape, compiler_params=_params(sem),
    )(place, *operands)


def _cast_into_slot(name, a, chip):
    rows, cols = a.shape
    tm = _row_tile_for(rows, cols)

    def body(pc_ref, a_ref, o_ref):
        o_ref[...] = a_ref[...].astype(BF16)

    return pl.pallas_call(
        body, name=name,
        grid_spec=pltpu.PrefetchScalarGridSpec(
            num_scalar_prefetch=1, grid=(rows // tm,), in_specs=[pl.BlockSpec((tm, cols), lambda i, pc: (i, 0))],
            out_specs=pl.BlockSpec((None, tm, cols), lambda i, pc: (pc[0], i, 0))),
        out_shape=jax.ShapeDtypeStruct((N_SHARD, rows, cols), BF16), compiler_params=_params(("parallel",)),
    )(chip, a)


def _sum_landed(name, landed, sent, chip):
    ns, rows, cols = landed.shape
    tm = _row_tile_for(rows, cols)

    def body(pc_ref, *refs):
        own_ref, o_ref = refs[ns], refs[ns + 1]
        me = pc_ref[0]
        terms = [jnp.where(me == j, own_ref[...], refs[j][...]).astype(F32) for j in range(ns)]
        o_ref[...] = _slot_sum(terms)

    def landed_spec(j):
        return pl.BlockSpec((None, tm, cols), lambda i, pc: (jnp.where(pc[0] == j, (j + 1) % ns, j), i, 0))

    return pl.pallas_call(
        body, name=name,
        grid_spec=pltpu.PrefetchScalarGridSpec(
            num_scalar_prefetch=1, grid=(rows // tm,),
            in_specs=[landed_spec(j) for j in range(ns)] + [pl.BlockSpec((None, tm, cols), lambda i, pc: (pc[0], i, 0))],
            out_specs=pl.BlockSpec((tm, cols), lambda i, pc: (i, 0))),
        out_shape=jax.ShapeDtypeStruct((rows, cols), F32), compiler_params=_params(("parallel",)),
    )(chip, *([landed] * ns), sent)


def _gather_all(name, a):
    def body(in_ref, out_ref, send_sems, recv_sems, local_sem):
        x, y, c = _my_place()
        me = 4 * x + 2 * y + c

        def peer(k):
            return (x ^ (k >> 2), y ^ ((k >> 1) & 1), c ^ (k & 1))

        def remote(k, land):
            return pltpu.make_async_remote_copy(
                src_ref=in_ref, dst_ref=out_ref.at[land], send_sem=send_sems.at[k - 1], recv_sem=recv_sems.at[k - 1],
                device_id=peer(k), device_id_type=pl.DeviceIdType.MESH)

        local = pltpu.make_async_copy(in_ref, out_ref.at[me], local_sem)
        local.start()
        for k in range(1, N_DEV):
            remote(k, me).start()
        for k in range(1, N_DEV):
            px, py, pc = peer(k)
            remote(k, 4 * px + 2 * py + pc).wait_recv()
        for k in range(1, N_DEV):
            remote(k, me).wait_send()
        local.wait()

    hbm = pl.BlockSpec(memory_space=pl.ANY)
    return pl.pallas_call(
        body, name=name, in_specs=[hbm], out_specs=hbm,
        out_shape=jax.ShapeDtypeStruct((N_DEV,) + a.shape, a.dtype),
        scratch_shapes=[pltpu.SemaphoreType.DMA((N_DEV - 1,)), pltpu.SemaphoreType.DMA((N_DEV - 1,)), pltpu.SemaphoreType.DMA],
    )(a)


def _row_tile_for(rows, cols, budget=1 << 20):
    for tm in (1024, 512, 256, 128, 64, 32, 16, 8):
        if rows % tm == 0 and tm * cols * 4 <= budget:
            return tm
    return rows


def _slot_sum(vals):
    g = vals[0]
    for v in vals[1:]:
        g = g + v
    return g


def _rowwise(name, fn, arrays, out_dtype):
    rows, cols = arrays[0].shape
    tm = _row_tile_for(rows, cols)

    def body(*refs):
        refs[-1][...] = fn(*[r[...] for r in refs[:-1]]).astype(out_dtype)

    blk = pl.BlockSpec((tm, cols), lambda i: (i, 0))
    return pl.pallas_call(
        body, name=name, grid=(rows // tm,), in_specs=[blk] * len(arrays), out_specs=blk,
        out_shape=jax.ShapeDtypeStruct((rows, cols), out_dtype), compiler_params=_params(("parallel",)),
    )(*arrays)


def _sum_slots(name, st):
    ns, rows, cols = st.shape
    tm = _row_tile_for(rows, cols)

    def body(s_ref, o_ref):
        o_ref[...] = _slot_sum([s_ref[j].astype(F32) for j in range(ns)])

    return pl.pallas_call(
        body, name=name, grid=(rows // tm,),
        in_specs=[pl.BlockSpec((ns, tm, cols), lambda i: (0, i, 0))],
        out_specs=pl.BlockSpec((tm, cols), lambda i: (i, 0)),
        out_shape=jax.ShapeDtypeStruct((rows, cols), F32),
        compiler_params=_params(("parallel",)),
    )(st)


ADAM_TILE_BYTES = 1 << 19


def _adam_update(g, p_ref, m_ref, v_ref, go_ref, d_ref, mo_ref, vo_ref):
    mn = ADAM_B1 * m_ref[...] + (1.0 - ADAM_B1) * g
    vn = ADAM_B2 * v_ref[...] + (1.0 - ADAM_B2) * jnp.square(g)
    m_hat = mn / (1.0 - ADAM_B1 ** ADAM_STEP)
    v_hat = vn / (1.0 - ADAM_B2 ** ADAM_STEP)
    go_ref[...] = g
    d_ref[...] = -ADAM_LR * (m_hat / (jnp.sqrt(v_hat) + ADAM_EPS) + ADAM_WD * p_ref[...])
    mo_ref[...] = mn
    vo_ref[...] = vn


def _adamw(name, p, m, v, gst):
    rows, cols = p.shape
    ns = gst.shape[0]
    tm = _row_tile_for(rows, cols, budget=ADAM_TILE_BYTES)

    def body(p_ref, m_ref, v_ref, g_ref, *outs):
        _adam_update(_slot_sum([g_ref[j] for j in range(ns)]), p_ref, m_ref, v_ref, *outs)

    blk = pl.BlockSpec((tm, cols), lambda i: (i, 0))
    return pl.pallas_call(
        body, name=name, grid=(rows // tm,),
        in_specs=[blk, blk, blk, pl.BlockSpec((ns, tm, cols), lambda i: (0, i, 0))],
        out_specs=[blk] * 4, out_shape=[jax.ShapeDtypeStruct((rows, cols), F32)] * 4,
        compiler_params=_params(("parallel",)),
    )(p, m, v, gst)


def _adamw_halves(name, p, m, v, mine, theirs, place):
    rows, cols = p.shape
    half = rows // 2
    tm = _row_tile_for(half, cols, budget=ADAM_TILE_BYTES)
    nh = half // tm

    def body(pc_ref, p_ref, m_ref, v_ref, mine_ref, theirs_ref, *outs):
        g = jnp.where(pl.program_id(0) == pc_ref[0], mine_ref[...], theirs_ref[...])
        _adam_update(g, p_ref, m_ref, v_ref, *outs)

    blk = pl.BlockSpec((tm, cols), lambda h, i, pc: (h * nh + i, 0))
    hblk = pl.BlockSpec((tm, cols), lambda h, i, pc: (i, 0))
    return pl.pallas_call(
        body, name=name,
        grid_spec=pltpu.PrefetchScalarGridSpec(
            num_scalar_prefetch=1, grid=(2, nh), in_specs=[blk, blk, blk, hblk, hblk], out_specs=[blk] * 4),
        out_shape=[jax.ShapeDtypeStruct((rows, cols), F32)] * 4, compiler_params=_params(("parallel", "parallel")),
    )(place, p, m, v, mine, theirs)


def _pack(parts, width=LANE, mult=8):
    flat = jnp.concatenate([a.reshape(-1) for a in parts])
    n = flat.shape[0]
    per = width * mult
    total = -(-n // per) * per
    return jnp.pad(flat, (0, total - n)).reshape(total // width, width)


def _unpack(packed, shapes):
    flat = packed.reshape(-1)
    out, off = [], 0
    for s in shapes:
        n = 1
        for d in s:
            n *= d
        out.append(flat[off:off + n].reshape(s))
        off += n
    return out


_SMALL_SHARDED = ("hg_lb", "rw_mu", "rw_w0", "rw_w2", "rw_a0", "rw_a2")
_REPLICATED = ("c_ctx", "ada_b", "norm_g", "hg_norm_g", "rw_kk", "rw_ka", "rw_rk", "rw_gn_g", "rw_gn_b", "final_g")
_BIG = ("ada_w", "w_in", "w_hg_out", "w_rw_out", "w_out")
_WEIGHTS = ("c_ctx", "ada_w", "ada_b", "norm_g", "w_in", "hg_lb", "hg_norm_g", "rw_mu", "rw_w0", "rw_w2", "rw_a0", "rw_a2",
            "rw_kk", "rw_ka", "rw_rk", "rw_gn_g", "rw_gn_b", "w_hg_out", "w_rw_out", "w_out", "final_g")


def _join_shards(st):
    a = jnp.moveaxis(st, 0, -2)
    return a.reshape(a.shape[:-2] + (a.shape[-2] * a.shape[-1],))


def _split_shards(a):
    s = a.reshape(a.shape[:-1] + (N_SHARD, a.shape[-1] // N_SHARD))
    return jnp.moveaxis(s, -2, 0)


def kernel(x, c, ctx, c_ctx, ada_w, ada_b, norm_g, w_in, hg_lb, hg_norm_g, rw_mu, rw_w0, rw_w2, rw_a0, rw_a2, rw_kk, rw_ka, rw_rk, rw_gn_g, rw_gn_b, w_hg_out, w_rw_out, w_out, final_g, loss_target, m_c_ctx, m_ada_w, m_ada_b, m_norm_g, m_w_in, m_hg_lb, m_hg_norm_g, m_rw_mu, m_rw_w0, m_rw_w2, m_rw_a0, m_rw_a2, m_rw_kk, m_rw_ka, m_rw_rk, m_rw_gn_g, m_rw_gn_b, m_w_hg_out, m_w_rw_out, m_w_out, m_final_g, v_c_ctx, v_ada_w, v_ada_b, v_norm_g, v_w_in, v_hg_lb, v_hg_norm_g, v_rw_mu, v_rw_w0, v_rw_w2, v_rw_a0, v_rw_a2, v_rw_kk, v_rw_ka, v_rw_rk, v_rw_gn_g, v_rw_gn_b, v_w_hg_out, v_w_rw_out, v_w_out, v_final_g):
    w = dict(c_ctx=c_ctx, ada_w=ada_w, ada_b=ada_b, norm_g=norm_g, w_in=w_in, hg_lb=hg_lb, hg_norm_g=hg_norm_g, rw_mu=rw_mu,
             rw_w0=rw_w0, rw_w2=rw_w2, rw_a0=rw_a0, rw_a2=rw_a2, rw_kk=rw_kk, rw_ka=rw_ka, rw_rk=rw_rk, rw_gn_g=rw_gn_g,
             rw_gn_b=rw_gn_b, w_hg_out=w_hg_out, w_rw_out=w_rw_out, w_out=w_out, final_g=final_g)
    m = dict(c_ctx=m_c_ctx, ada_w=m_ada_w, ada_b=m_ada_b, norm_g=m_norm_g, w_in=m_w_in, hg_lb=m_hg_lb, hg_norm_g=m_hg_norm_g,
             rw_mu=m_rw_mu, rw_w0=m_rw_w0, rw_w2=m_rw_w2, rw_a0=m_rw_a0, rw_a2=m_rw_a2, rw_kk=m_rw_kk, rw_ka=m_rw_ka,
             rw_rk=m_rw_rk, rw_gn_g=m_rw_gn_g, rw_gn_b=m_rw_gn_b, w_hg_out=m_w_hg_out, w_rw_out=m_w_rw_out, w_out=m_w_out,
             final_g=m_final_g)
    v = dict(c_ctx=v_c_ctx, ada_w=v_ada_w, ada_b=v_ada_b, norm_g=v_norm_g, w_in=v_w_in, hg_lb=v_hg_lb, hg_norm_g=v_hg_norm_g,
             rw_mu=v_rw_mu, rw_w0=v_rw_w0, rw_w2=v_rw_w2, rw_a0=v_rw_a0, rw_a2=v_rw_a2, rw_kk=v_rw_kk, rw_ka=v_rw_ka,
             rw_rk=v_rw_rk, rw_gn_g=v_rw_gn_g, rw_gn_b=v_rw_gn_b, w_hg_out=v_w_hg_out, w_rw_out=v_w_rw_out, w_out=v_w_out,
             final_g=v_final_g)

    def mat(a):
        return a.reshape(a.shape[-2], a.shape[-1])

    def pack_small(d):
        return _pack([d[n] for n in _SMALL_SHARDED], mult=2 * ROW_ALIGN)

    my_core = lax.axis_index("c").astype(jnp.int32).reshape(1)
    my_chip = (2 * lax.axis_index("x") + lax.axis_index("y")).astype(jnp.int32).reshape(1)

    small_shapes = [w[n].shape for n in _SMALL_SHARDED]
    big_bf = [_cast_into_slot(f"to_bf16_{n}", mat(w[n]), my_chip) for n in _BIG]
    small_mine = pack_small(w)
    small_slots = lax.dynamic_update_slice(jnp.zeros((N_SHARD,) + small_mine.shape, F32), small_mine[None], (my_chip[0], 0, 0))
    gathered = _weights_gather("weights_gather", big_bf, [small_slots])
    ada_st, w_in_st, w_hg_st, w_rw_st, w_out_st, small_st = gathered
    full_small = {}
    per_chip = [_unpack(small_st[j], small_shapes) for j in range(N_SHARD)]
    for i, n in enumerate(_SMALL_SHARDED):
        full_small[n] = _join_shards(jnp.stack([per_chip[j][i] for j in range(N_SHARD)], axis=0))
    dm = x.shape[-1]
    w_out_full = w_out_st.reshape(dm, dm)

    loss_b, grad_x, g = _local_step(
        x[0], c, ctx[0], c_ctx, ada_st, ada_b, norm_g, w_in_st, full_small["hg_lb"], hg_norm_g, full_small["rw_mu"][0],
        full_small["rw_w0"][0], full_small["rw_w2"][0], full_small["rw_a0"][0], full_small["rw_a2"][0], rw_kk, rw_ka, rw_rk,
        rw_gn_g, rw_gn_b, w_hg_st, w_rw_st, w_out_full, final_g, loss_target[0])
    loss = lax.psum(loss_b[0, 0], ("x", "y", "c"))

    g_small = {"hg_lb": g["hg_lb"], "rw_mu": g["rw_mu"][None], "rw_w0": g["rw_w0"][None], "rw_w2": g["rw_w2"][None],
               "rw_a0": g["rw_a0"][None], "rw_a2": g["rw_a2"][None]}
    split = {n: _split_shards(g_small[n]) for n in _SMALL_SHARDED}
    small_parts = jnp.stack([pack_small({n: split[n][j] for n in _SMALL_SHARDED}) for j in range(N_SHARD)], axis=0)
    partial = [g["ada_w"], g["w_in"], g["w_hg_out"], g["w_rw_out"], g["w_out"].reshape(N_SHARD, dm // N_SHARD, dm), small_parts]
    chip_sums = [_pair_exchange(f"grads_pair_sum{i}", a, my_core, True, BF16) for i, a in enumerate(partial)]
    landed = _chip_scatter("grads_scatter", chip_sums)
    mine = [_sum_landed(f"grads_sum{i}", a, s, my_chip) for i, (a, s) in enumerate(zip(landed, chip_sums))]
    theirs = [_pair_exchange(f"grads_pair_swap{i}", a, my_core, False, F32) for i, a in enumerate(mine)]
    rep_shapes = [w[n].shape for n in _REPLICATED]
    rep_all = _gather_all("grads_replicated", _pack([g[n].reshape(w[n].shape) for n in _REPLICATED]))

    res = {}
    for i, n in enumerate(_BIG):
        outs = _adamw_halves(f"adamw_{n}", mat(w[n]), mat(m[n]), mat(v[n]), mine[i], theirs[i], my_core)
        res[n] = [o.reshape(w[n].shape) for o in outs]
    outs = _adamw_halves("adamw_small", small_mine, pack_small(m), pack_small(v), mine[len(_BIG)], theirs[len(_BIG)], my_core)
    for i, vals in enumerate(zip(*[_unpack(o, small_shapes) for o in outs])):
        res[_SMALL_SHARDED[i]] = list(vals)
    outs = _adamw("adamw_replicated", _pack([w[n] for n in _REPLICATED]), _pack([m[n] for n in _REPLICATED]),
                  _pack([v[n] for n in _REPLICATED]), rep_all)
    for i, vals in enumerate(zip(*[_unpack(o, rep_shapes) for o in outs])):
        res[_REPLICATED[i]] = list(vals)

    return (loss, grad_x[None], *[res[n][0] for n in _WEIGHTS], *[res[n][1] for n in _WEIGHTS],
            *[res[n][2] for n in _WEIGHTS], *[res[n][3] for n in _WEIGHTS])
```

```python
import functools

import jax
import jax.numpy as jnp
from jax import lax
from jax.experimental import pallas as pl
from jax.experimental.pallas import tpu as pltpu

HI = lax.Precision.HIGHEST
F32 = jnp.float32
BF16 = jnp.bfloat16

NORM_EPS = 1e-6
HG_HEAD = 128
RW_HEAD = 64
RW_LORA = 64
RW_GN_EPS = 64e-5
GRID_W = 64
SUB = 16
STEP = 64
N_SHARD = 4
N_DEV = 8
LANE = 128

ADAM_LR = 0.001
ADAM_B1 = 0.9
ADAM_B2 = 0.999
ADAM_EPS = 1e-08
ADAM_WD = 0.01
ADAM_STEP = 10

VMEM_LIMIT = 56 * 1024 * 1024


def _params(sem=None):
    return pltpu.CompilerParams(dimension_semantics=sem, vmem_limit_bytes=VMEM_LIMIT)


def _tile(n, cands):
    for c in cands:
        if n % c == 0:
            return c
    return n


def _iota2(n, m, d):
    return lax.broadcasted_iota(jnp.int32, (n, m), d)


def _before(n, rev, strict):
    t, s = _iota2(n, n, 0), _iota2(n, n, 1)
    if rev:
        return (s > t) if strict else (s >= t)
    return (s < t) if strict else (s <= t)


def _bdot(a, b, spec):
    return jnp.einsum(spec, a, b, precision=HI, preferred_element_type=F32)


def _sdot(a, b, spec):
    return jnp.einsum(spec, a, b, precision=lax.Precision.DEFAULT, preferred_element_type=F32)


def _hg_step(s0, qraw, iin, fin, lb2, rev):
    c, w = qraw.shape
    h = w // HG_HEAD
    nsub = c // SUB
    lb = jax.nn.sigmoid(lb2[0:1] - lb2[1:2])
    q = jax.nn.silu(qraw)
    fg = lb + (1.0 - lb) * jax.nn.sigmoid(fin)
    kk = 1.0 - fg
    g = jnp.log(fg)
    bcum = jnp.dot(_before(c, rev, False).astype(F32), g, precision=HI, preferred_element_type=F32)
    rows = lax.broadcasted_iota(jnp.int32, (SUB, 1, 1), 0)

    def heads(a):
        return jnp.swapaxes(a.reshape(a.shape[0], h, HG_HEAD), 0, 1)

    def unheads(a):
        return jnp.swapaxes(a, 0, 1).reshape(a.shape[1], w)

    blocks = [slice(j * SUB, (j + 1) * SUB) for j in range(nsub)]
    outs = []
    for sl in blocks:
        qs, ks, vs, bc = [a[sl].reshape(SUB, h, HG_HEAD) for a in (q, kk, iin, bcum)]
        o = jnp.zeros((SUB, h, HG_HEAD), F32)
        for si in range(SUB):
            dec = jnp.exp(jnp.minimum(bc - bc[si:si + 1], 0.0))
            a = jnp.sum(qs * ks[si:si + 1] * dec, axis=-1, keepdims=True)
            valid = (rows <= si) if rev else (rows >= si)
            o = o + jnp.where(valid, a, 0.0) * vs[si:si + 1]
        outs.append(o.reshape(SUB, w))
    order = list(range(nsub - 1, -1, -1)) if rev else list(range(nsub))
    for pos in range(1, nsub):
        j, before = order[pos], order[:pos]
        first = (j + 1) * SUB - 1 if rev else j * SUB
        bstart = bcum[first:first + 1] - g[first:first + 1]
        qp = heads(q[blocks[j]] * jnp.exp(bcum[blocks[j]] - bstart))
        kp = heads(jnp.concatenate([kk[blocks[p]] * jnp.exp(bstart - bcum[blocks[p]]) for p in before], axis=0))
        vp = heads(jnp.concatenate([iin[blocks[p]] for p in before], axis=0))
        outs[j] = outs[j] + unheads(_sdot(_sdot(qp, kp, 'htk,hsk->hts'), vp, 'hts,hsv->htv'))
    o_state = unheads(_sdot(heads(q * jnp.exp(bcum)), s0, 'htk,hvk->htv'))
    last = 0 if rev else c - 1
    blast = bcum[last:last + 1]
    s_new = heads(jnp.exp(blast)) * s0 + _sdot(heads(iin), heads(kk * jnp.exp(blast - bcum)), 'hsv,hsk->hvk')
    return jnp.concatenate(outs, axis=0) + o_state, s_new


def _tri_solve(lmat, rhs, rev):
    hh, c, _ = lmat.shape
    nb = c // SUB
    diag = jnp.concatenate([lmat[:, i * SUB:(i + 1) * SUB, i * SUB:(i + 1) * SUB] for i in range(nb)], axis=0)
    dt = jnp.transpose(diag, (1, 2, 0))
    col = lax.broadcasted_iota(jnp.int32, (SUB, 1), 0)
    inv_rows = [None] * SUB
    order = list(range(SUB - 1, -1, -1)) if rev else list(range(SUB))
    for pos, t in enumerate(order):
        row = jnp.broadcast_to((col == t).astype(F32), (SUB, dt.shape[2]))
        for s in order[:pos]:
            row = row - dt[t, s:s + 1, :] * inv_rows[s]
        inv_rows[t] = row
    tinv = jnp.transpose(jnp.stack(inv_rows, axis=0), (2, 0, 1))
    p = [None] * nb
    done = []
    for i in (range(nb - 1, -1, -1) if rev else range(nb)):
        r = rhs[:, i * SUB:(i + 1) * SUB]
        for m in done:
            r = r - _sdot(lmat[:, i * SUB:(i + 1) * SUB, m * SUB:(m + 1) * SUB], p[m], 'hts,hsv->htv')
        p[i] = _sdot(tinv[i * hh:(i + 1) * hh], r, 'hts,hsv->htv')
        done.append(i)
    return jnp.concatenate(p, axis=1)


def _rw_step(s0, r, k, v, wlo, alo, w0h, w2h, a0h, a2h, kkh, kah, rev):
    hh, c, _ = r.shape
    tl = jnp.broadcast_to(jnp.tanh(wlo)[None], (hh, c, wlo.shape[1]))
    al = jnp.broadcast_to(alo[None], (hh, c, alo.shape[1]))
    wlog = -jax.nn.softplus(-(w0h + _sdot(tl, w2h, 'hcl,hlj->hcj'))) - 0.5
    lw = -jnp.exp(wlog)
    a = jax.nn.sigmoid(a0h + _sdot(al, a2h, 'hcl,hlj->hcj'))
    kk = k * kkh
    kk = kk * lax.rsqrt(jnp.sum(kk * kk, axis=-1, keepdims=True) + 1e-12)
    kd = k * (1.0 + (a - 1.0) * kah)
    b = kk * a
    incl = jnp.broadcast_to(_before(c, rev, False).astype(F32)[None], (hh, c, c))
    cum = _bdot(incl, lw, 'hts,hsk->htk')
    ecum, encum = jnp.exp(cum), jnp.exp(-cum)
    alpha = jnp.exp(cum - lw) * kk
    beta = b * encum
    kappa = kd * encum
    rho = r * ecum
    m_lt = _before(c, rev, True)[None]
    m_le = _before(c, rev, False)[None]
    a_kap = jnp.where(m_lt, _sdot(alpha, kappa, 'htk,hsk->hts'), 0.0)
    a_bet = jnp.where(m_lt, _sdot(alpha, beta, 'htk,hsk->hts'), 0.0)
    b_kap = jnp.where(m_le, _sdot(rho, kappa, 'htk,hsk->hts'), 0.0)
    b_bet = jnp.where(m_le, _sdot(rho, beta, 'htk,hsk->hts'), 0.0)
    rhs = _sdot(alpha, s0, 'htk,hvk->htv') + _sdot(a_kap, v, 'hts,hsv->htv')
    p = _tri_solve(a_bet, rhs, rev)
    y = _sdot(rho, s0, 'htk,hvk->htv') + _sdot(b_kap, v, 'hts,hsv->htv') - _sdot(b_bet, p, 'hts,hsv->htv')
    stil = s0 + _sdot(v, kappa, 'hsv,hsk->hvk') - _sdot(p, beta, 'hsv,hsk->hvk')
    last = 0 if rev else c - 1
    return y, stil * ecum[:, last:last + 1, :]


def _fn_h(s, norm_g, scale, shift):
    return s * lax.rsqrt(jnp.mean(s * s, axis=-1, keepdims=True) + NORM_EPS) * norm_g * (1.0 + scale) + shift


def _fn_hgpost(of, ob, z, g):
    tm, w = of.shape
    o = (of + ob).reshape(tm, w // HG_HEAD, HG_HEAD)
    o = o * lax.rsqrt(jnp.mean(o * o, axis=-1, keepdims=True) + NORM_EPS)
    return o.reshape(tm, w) * g * jax.nn.silu(z)


def _fn_rwpost(y0, y1, r, k, v, alo, z, a0, a2, k_a, r_k, gn_g, gn_b):
    tm, w = r.shape
    nh = w // RW_HEAD
    asum = 0.0
    for d in range(2):
        asum = asum + jax.nn.sigmoid(a0[d:d + 1] + jnp.dot(alo[:, d * RW_LORA:(d + 1) * RW_LORA], a2[d],
                                                           precision=HI, preferred_element_type=F32))
    k_sum = k * (2.0 + (asum - 2.0) * k_a)
    ys = (y0 + y1).reshape(tm, nh, RW_HEAD)
    mean = jnp.mean(ys, axis=-1, keepdims=True)
    var = jnp.mean(jnp.square(ys - mean), axis=-1, keepdims=True)
    y = ((ys - mean) * lax.rsqrt(var + RW_GN_EPS)).reshape(tm, w) * gn_g + gn_b
    bonus = jnp.sum((r * k_sum * r_k).reshape(tm, nh, RW_HEAD), axis=-1, keepdims=True) * v.reshape(tm, nh, RW_HEAD)
    return (y + bonus.reshape(tm, w)) * jax.nn.silu(z)


def _fn_merge(a, b, ghg, grw):
    return jax.nn.sigmoid(ghg) * a + jax.nn.sigmoid(grw) * b


def _fn_final(xs, o, gate, final_g, tgt):
    x2 = xs + gate * o
    y = x2 * lax.rsqrt(jnp.mean(x2 * x2, axis=-1, keepdims=True) + NORM_EPS) * final_g
    return 0.5 * jnp.sum(jnp.mean(jnp.square(y - tgt), axis=-1))


def _row_call(name, fn, n_tiles, tm, row_ins, full_ins, row_outs, acc_outs):
    n_ri, n_fi, n_ro = len(row_ins), len(full_ins), len(row_outs)

    def body(*refs):
        i = pl.program_id(0)
        rvals = [r[...] for r in refs[:n_ri]]
        fvals = [r[...] for r in refs[n_ri:n_ri + n_fi]]
        outs = refs[n_ri + n_fi:]
        ro, ao = fn(i, rvals, fvals)
        for ref, val in zip(outs[:n_ro], ro):
            ref[...] = val.astype(ref.dtype)
        for ref, val in zip(outs[n_ro:], ao):
            @pl.when(i == 0)
            def _(ref=ref):
                ref[...] = jnp.zeros_like(ref)
            ref[...] += val.astype(ref.dtype)

    def rspec(width, cb, off):
        return pl.BlockSpec((tm, width), lambda i: (jnp.maximum(i - off, 0), cb))

    def fspec(shape):
        nd = len(shape)
        return pl.BlockSpec(shape, lambda i: (0,) * nd)

    in_specs = [rspec(w, cb, off) for (_, cb, w, off) in row_ins] + [fspec(a.shape) for a in full_ins]
    out_specs = [rspec(w, 0, off) for (_, w, _, off) in row_outs] + [fspec(s) for (s, _) in acc_outs]
    out_shape = [jax.ShapeDtypeStruct((rows, w), dt) for (rows, w, dt, _) in row_outs] + \
                [jax.ShapeDtypeStruct(s, dt) for (s, dt) in acc_outs]
    res = pl.pallas_call(
        body, name=name, grid=(n_tiles,), in_specs=in_specs, out_specs=out_specs, out_shape=out_shape,
        compiler_params=_params(("arbitrary",)),
    )(*[a for (a, _, _, _) in row_ins], *full_ins)
    return list(res)


def _mm(name, a, b, m, n, k_steps, tm, tn, a_block, a_map, b_block, b_map, o_shape, o_block, o_map,
        contract, out_dtype=F32):
    def body(a_ref, b_ref, o_ref, acc_ref):
        kk = pl.program_id(2)

        @pl.when(kk == 0)
        def _():
            acc_ref[...] = jnp.zeros_like(acc_ref)

        acc_ref[...] += lax.dot_general(a_ref[...].astype(BF16), b_ref[...].astype(BF16),
                                        (contract, ((), ())), preferred_element_type=F32)

        @pl.when(kk == k_steps - 1)
        def _():
            o_ref[...] = acc_ref[...].astype(o_ref.dtype)

    return pl.pallas_call(
        body, name=name, grid=(m // tm, n // tn, k_steps),
        in_specs=[pl.BlockSpec(a_block, a_map), pl.BlockSpec(b_block, b_map)],
        out_specs=pl.BlockSpec(o_block, o_map),
        out_shape=jax.ShapeDtypeStruct(o_shape, out_dtype),
        scratch_shapes=[pltpu.VMEM((tm, tn), F32)],
        compiler_params=_params(("parallel", "parallel", "arbitrary")),
    )(a, b)


_TM = (768, 512, 256, 128, 64, 32, 16, 8)
_TN = (512, 256, 128)
_TK = (1024, 768, 512, 256, 128)
_TK_WIDE = (768, 512, 256, 128)
WIDE_OUT_BYTES = 32 << 20


def _tm_wide(m, ns):
    for tm in _TM:
        if m % tm == 0 and 3 * 4 * tm * ns <= WIDE_OUT_BYTES:
            return tm
    return m


def _mm_nn(name, a, b, out_dtype=F32):
    m, k = a.shape
    n = b.shape[1]
    tm, tn, tk = _tile(m, _TM), _tile(n, _TN), _tile(k, _TK)
    return _mm(name, a, b, m, n, k // tk, tm, tn, (tm, tk), lambda i, j, s: (i, s), (tk, tn), lambda i, j, s: (s, j),
               (m, n), (tm, tn), lambda i, j, s: (i, j), ((1,), (0,)), out_dtype)


def _mm_nt(name, a, b, out_dtype=F32):
    m, k = a.shape
    n = b.shape[0]
    tm, tn, tk = _tile(m, _TM), _tile(n, _TN), _tile(k, _TK)
    return _mm(name, a, b, m, n, k // tk, tm, tn, (tm, tk), lambda i, j, s: (i, s), (tn, tk), lambda i, j, s: (j, s),
               (m, n), (tm, tn), lambda i, j, s: (i, j), ((1,), (1,)), out_dtype)


def _mm_tn(name, a, b, out_dtype=F32):
    k, m = a.shape
    n = b.shape[1]
    tm, tn, tk = _tile(m, _TM), _tile(n, _TN), _tile(k, _TK)
    return _mm(name, a, b, m, n, k // tk, tm, tn, (tk, tm), lambda i, j, s: (s, i), (tk, tn), lambda i, j, s: (s, j),
               (m, n), (tm, tn), lambda i, j, s: (i, j), ((0,), (0,)), out_dtype)


def _mm_n_st(name, a, bst, out_dtype=F32):
    m, k = a.shape
    ns_, _, ns = bst.shape
    tm, tk = _tm_wide(m, ns), _tile(k, (512, 256, 128))
    return _mm(name, a, bst, m, ns_ * ns, k // tk, tm, ns,
               (tm, tk), lambda i, j, s: (i, s), (None, tk, ns), lambda i, j, s: (j, s, 0),
               (ns_, m, ns), (None, tm, ns), lambda i, j, s: (j, i, 0), ((1,), (0,)), out_dtype)


def _mm_st_t(name, ast, bst, out_dtype=F32):
    ns_, m, ns = ast.shape
    n = bst.shape[1]
    tm, tn = _tile(m, _TM), _tile(n, _TN)
    return _mm(name, ast, bst, m, n, ns_, tm, tn,
               (None, tm, ns), lambda i, j, s: (s, i, 0), (None, tn, ns), lambda i, j, s: (s, j, 0),
               (m, n), (tm, tn), lambda i, j, s: (i, j), ((1,), (1,)), out_dtype)


def _mm_t_st(name, a, bst, out_dtype=F32):
    k, m = a.shape
    ns_, _, ns = bst.shape
    tm, tk = _tile(m, _TN), _tile(k, _TK_WIDE)
    return _mm(name, a, bst, m, ns_ * ns, k // tk, tm, ns,
               (tk, tm), lambda i, j, s: (s, i), (None, tk, ns), lambda i, j, s: (j, s, 0),
               (ns_, m, ns), (None, tm, ns), lambda i, j, s: (j, i, 0), ((0,), (0,)), out_dtype)


def _scan_order(j, n_ctx, n_all, rev):
    if not rev:
        return j
    return jnp.where(j < n_ctx, n_ctx - 1 - j, n_all - 1 - (j - n_ctx))


def _hg_scan_fwd(name, p_hg, lb2, d, n_ctx):
    t, w5 = p_hg.shape
    w = w5 // 5
    h = w // HG_HEAD
    n = t // STEP
    rev = d == 1

    def body(q_ref, i_ref, f_ref, lb_ref, o_ref, st_ref, s_ref):
        j = pl.program_id(0)

        @pl.when(j == 0)
        def _():
            s_ref[...] = jnp.zeros_like(s_ref)

        s0 = s_ref[...]
        st_ref[...] = s0
        o, s1 = _hg_step(s0, q_ref[...], i_ref[...], f_ref[...], lb_ref[...], rev)
        o_ref[...] = o
        s_ref[...] = s1

    def rows(cb):
        return pl.BlockSpec((STEP, w), lambda j: (_scan_order(j, n_ctx, n, rev), cb))

    return pl.pallas_call(
        body, name=name, grid=(n,),
        in_specs=[rows(0), rows(1), rows(2 + d), pl.BlockSpec((2, w), lambda j: (0, 0))],
        out_specs=[rows(0), pl.BlockSpec((None, h, HG_HEAD, HG_HEAD), lambda j: (j, 0, 0, 0))],
        out_shape=[jax.ShapeDtypeStruct((t, w), F32), jax.ShapeDtypeStruct((n, h, HG_HEAD, HG_HEAD), F32)],
        scratch_shapes=[pltpu.VMEM((h, HG_HEAD, HG_HEAD), F32)],
        compiler_params=_params(("arbitrary",)),
    )(p_hg, p_hg, p_hg, lb2)


def _hg_scan_bwd(name, p_hg, lb2, states, do, d, n_ctx):
    t, w5 = p_hg.shape
    w = w5 // 5
    h = w // HG_HEAD
    n = t // STEP
    rev = d == 1

    def body(q_ref, i_ref, f_ref, lb_ref, st_ref, do_ref, dq_ref, di_ref, df_ref, dlb_ref, ds_ref):
        step = pl.program_id(0)

        @pl.when(step == 0)
        def _():
            ds_ref[...] = jnp.zeros_like(ds_ref)
            dlb_ref[...] = jnp.zeros_like(dlb_ref)

        _, vjp = jax.vjp(lambda s0, q, i, f, lb: _hg_step(s0, q, i, f, lb, rev),
                         st_ref[...], q_ref[...], i_ref[...], f_ref[...], lb_ref[...])
        ds0, dq, di, df, dlb = vjp((do_ref[...], ds_ref[...]))
        dq_ref[...] = dq
        di_ref[...] = di
        df_ref[...] = df
        dlb_ref[...] += dlb
        ds_ref[...] = ds0

    def rows(cb):
        return pl.BlockSpec((STEP, w), lambda s: (_scan_order(n - 1 - s, n_ctx, n, rev), cb))

    return pl.pallas_call(
        body, name=name, grid=(n,),
        in_specs=[rows(0), rows(1), rows(2 + d), pl.BlockSpec((2, w), lambda s: (0, 0)),
                  pl.BlockSpec((None, h, HG_HEAD, HG_HEAD), lambda s: (n - 1 - s, 0, 0, 0)), rows(0)],
        out_specs=[rows(0), rows(0), rows(0), pl.BlockSpec((2, w), lambda s: (0, 0))],
        out_shape=[jax.ShapeDtypeStruct((t, w), F32)] * 3 + [jax.ShapeDtypeStruct((2, w), F32)],
        scratch_shapes=[pltpu.VMEM((h, HG_HEAD, HG_HEAD), F32)],
        compiler_params=_params(("arbitrary",)),
    )(p_hg, p_hg, p_hg, lb2, states, do)


def _to_heads(a, nh):
    return jnp.stack([a[:, i * RW_HEAD:(i + 1) * RW_HEAD] for i in range(nh)], axis=0)


def _from_heads(a):
    return jnp.concatenate([a[i] for i in range(a.shape[0])], axis=-1)


def _rw_scan_fwd(name, sh, hp, d, n_ctx):
    t = sh.shape[0]
    w = (sh.shape[1] - 4 * RW_LORA) // 3
    nh = w // RW_HEAD
    n = t // STEP
    rev = d == 1
    lo = 3 * w // LANE

    def body(r_ref, k_ref, v_ref, wl_ref, al_ref, w0_ref, w2_ref, a0_ref, a2_ref, kk_ref, ka_ref,
             y_ref, st_ref, s_ref):
        j = pl.program_id(0)

        @pl.when(j == 0)
        def _():
            s_ref[...] = jnp.zeros_like(s_ref)

        s0 = s_ref[...]
        st_ref[...] = s0
        wl = wl_ref[...][:, d * RW_LORA:(d + 1) * RW_LORA]
        al = al_ref[...][:, d * RW_LORA:(d + 1) * RW_LORA]
        y, s1 = _rw_step(s0, _to_heads(r_ref[...], nh), _to_heads(k_ref[...], nh), _to_heads(v_ref[...], nh), wl, al,
                         w0_ref[...], w2_ref[...], a0_ref[...], a2_ref[...], kk_ref[...], ka_ref[...], rev)
        y_ref[...] = _from_heads(y)
        s_ref[...] = s1

    def rows(cb, width=w):
        return pl.BlockSpec((STEP, width), lambda j: (_scan_order(j, n_ctx, n, rev), cb))

    def whole(a):
        nd = a.ndim
        return pl.BlockSpec(a.shape, lambda j: (0,) * nd)

    return pl.pallas_call(
        body, name=name, grid=(n,),
        in_specs=[rows(0), rows(1), rows(2), rows(lo, LANE), rows(lo + 1, LANE)] + [whole(a) for a in hp],
        out_specs=[rows(0), pl.BlockSpec((None, nh, RW_HEAD, RW_HEAD), lambda j: (j, 0, 0, 0))],
        out_shape=[jax.ShapeDtypeStruct((t, w), F32), jax.ShapeDtypeStruct((n, nh, RW_HEAD, RW_HEAD), F32)],
        scratch_shapes=[pltpu.VMEM((nh, RW_HEAD, RW_HEAD), F32)],
        compiler_params=_params(("arbitrary",)),
    )(sh, sh, sh, sh, sh, *hp)


def _rw_scan_bwd(name, sh, hp, states, dy, d, n_ctx):
    t = sh.shape[0]
    w = (sh.shape[1] - 4 * RW_LORA) // 3
    nh = w // RW_HEAD
    n = t // STEP
    rev = d == 1
    lo = 3 * w // LANE

    def body(r_ref, k_ref, v_ref, wl_ref, al_ref, w0_ref, w2_ref, a0_ref, a2_ref, kk_ref, ka_ref, st_ref, dy_ref,
             dm_ref, dl_ref, dw0_ref, dw2_ref, da0_ref, da2_ref, dkk_ref, dka_ref, ds_ref):
        step = pl.program_id(0)
        pouts = (dw0_ref, dw2_ref, da0_ref, da2_ref, dkk_ref, dka_ref)

        @pl.when(step == 0)
        def _():
            ds_ref[...] = jnp.zeros_like(ds_ref)
            for ref in pouts:
                ref[...] = jnp.zeros_like(ref)

        wl = wl_ref[...][:, d * RW_LORA:(d + 1) * RW_LORA]
        al = al_ref[...][:, d * RW_LORA:(d + 1) * RW_LORA]
        _, vjp = jax.vjp(functools.partial(_rw_step, rev=rev),
                         st_ref[...], _to_heads(r_ref[...], nh), _to_heads(k_ref[...], nh), _to_heads(v_ref[...], nh),
                         wl, al, w0_ref[...], w2_ref[...], a0_ref[...], a2_ref[...], kk_ref[...], ka_ref[...])
        g = vjp((_to_heads(dy_ref[...], nh), ds_ref[...]))
        ds_ref[...] = g[0]
        dm_ref[...] = jnp.concatenate([_from_heads(g[1]), _from_heads(g[2]), _from_heads(g[3])], axis=-1)
        zero = jnp.zeros_like(g[4])
        parts = [zero] * 4
        parts[d], parts[2 + d] = g[4], g[5]
        dl_ref[...] = jnp.concatenate(parts, axis=-1)
        for ref, val in zip(pouts, g[6:]):
            ref[...] += val

    def rows(cb, width=w):
        return pl.BlockSpec((STEP, width), lambda s: (_scan_order(n - 1 - s, n_ctx, n, rev), cb))

    def whole(a):
        nd = a.ndim
        return pl.BlockSpec(a.shape, lambda s: (0,) * nd)

    return pl.pallas_call(
        body, name=name, grid=(n,),
        in_specs=[rows(0), rows(1), rows(2), rows(lo, LANE), rows(lo + 1, LANE)] + [whole(a) for a in hp] +
                 [pl.BlockSpec((None, nh, RW_HEAD, RW_HEAD), lambda s: (n - 1 - s, 0, 0, 0)), rows(0)],
        out_specs=[rows(0, 3 * w), rows(0, 4 * RW_LORA)] + [whole(a) for a in hp],
        out_shape=[jax.ShapeDtypeStruct((t, 3 * w), F32), jax.ShapeDtypeStruct((t, 4 * RW_LORA), F32)] +
                  [jax.ShapeDtypeStruct(a.shape, F32) for a in hp],
        scratch_shapes=[pltpu.VMEM((nh, RW_HEAD, RW_HEAD), F32)],
        compiler_params=_params(("arbitrary",)),
    )(sh, sh, sh, sh, sh, *hp, states, dy)


def _shift_masks(t, n_ctx_rows):
    row = lax.broadcasted_iota(jnp.int32, (t, 1), 0)
    isx = row >= n_ctx_rows
    pos = jnp.where(isx, row - n_ctx_rows, row)
    col = jnp.where(isx, jnp.bitwise_and(pos, GRID_W - 1), pos)
    ncol = jnp.where(isx, GRID_W, n_ctx_rows)
    n_x = t - n_ctx_rows
    ml = col != 0
    mr = col != ncol - 1
    mu = isx & (pos >= GRID_W)
    md = isx & (pos < n_x - GRID_W)
    return ml, mr, mu, md, isx


def _shift_fwd(name, p, mu, n_ctx_rows):
    t, c = p.shape
    cw = LANE

    def body(p_ref, mu_ref, o_ref):
        x = p_ref[...]
        m = mu_ref[...]
        ml, mr, mup, mdn, isx = _shift_masks(t, n_ctx_rows)
        left = jnp.where(ml, pltpu.roll(x, 1, 0), 0.0)
        right = jnp.where(mr, pltpu.roll(x, t - 1, 0), 0.0)
        up = jnp.where(mup, pltpu.roll(x, GRID_W, 0), 0.0)
        down = jnp.where(mdn, pltpu.roll(x, t - GRID_W, 0), 0.0)
        out = x + m[0:1] * (left - x) + m[1:2] * (right - x)
        vert = m[2:3] * (up - x) + m[3:4] * (down - x)
        o_ref[...] = out + jnp.where(isx, vert, 0.0)

    return pl.pallas_call(
        body, name=name, grid=(c // cw,),
        in_specs=[pl.BlockSpec((t, cw), lambda j: (0, j)), pl.BlockSpec((4, cw), lambda j: (0, j))],
        out_specs=pl.BlockSpec((t, cw), lambda j: (0, j)),
        out_shape=jax.ShapeDtypeStruct((t, c), F32),
        compiler_params=_params(("parallel",)),
    )(p, mu)


def _shift_bwd(name, p, mu, dparts, n_ctx_rows):
    t, c = p.shape
    cw = LANE
    npart = len(dparts)

    def body(*refs):
        p_ref, mu_ref = refs[0], refs[1]
        dp_ref, dmu_ref = refs[2 + npart], refs[3 + npart]
        x = p_ref[...]
        m = mu_ref[...]
        g = refs[2][...]
        for r in refs[3:2 + npart]:
            g = g + r[...]
        ml, mr, mup, mdn, isx = _shift_masks(t, n_ctx_rows)
        left = jnp.where(ml, pltpu.roll(x, 1, 0), 0.0)
        right = jnp.where(mr, pltpu.roll(x, t - 1, 0), 0.0)
        up = jnp.where(mup, pltpu.roll(x, GRID_W, 0), 0.0)
        down = jnp.where(mdn, pltpu.roll(x, t - GRID_W, 0), 0.0)
        gx = jnp.where(isx, g, 0.0)
        dmu_ref[...] = jnp.concatenate([
            jnp.sum(g * (left - x), axis=0, keepdims=True), jnp.sum(g * (right - x), axis=0, keepdims=True),
            jnp.sum(gx * (up - x), axis=0, keepdims=True), jnp.sum(gx * (down - x), axis=0, keepdims=True)], axis=0)
        coef = 1.0 - m[0:1] - m[1:2] - jnp.where(isx, m[2:3] + m[3:4], 0.0)
        dp = coef * g
        dp = dp + m[0:1] * pltpu.roll(jnp.where(ml, g, 0.0), t - 1, 0)
        dp = dp + m[1:2] * pltpu.roll(jnp.where(mr, g, 0.0), 1, 0)
        dp = dp + m[2:3] * pltpu.roll(jnp.where(mup, g, 0.0), t - GRID_W, 0)
        dp = dp + m[3:4] * pltpu.roll(jnp.where(mdn, g, 0.0), GRID_W, 0)
        dp_ref[...] = dp

    col = pl.BlockSpec((t, cw), lambda j: (0, j))
    par = pl.BlockSpec((4, cw), lambda j: (0, j))
    return pl.pallas_call(
        body, name=name, grid=(c // cw,),
        in_specs=[col, par] + [col] * npart,
        out_specs=[col, par],
        out_shape=[jax.ShapeDtypeStruct((t, c), F32), jax.ShapeDtypeStruct((4, c), F32)],
        compiler_params=_params(("parallel",)),
    )(p, mu, *dparts)


def _local_step(x, c, ctx, c_ctx, ada_st, ada_b, norm_g, w_in_st, hg_lb, hg_norm_g, rw_mu, rw_w0, rw_w2, rw_a0, rw_a2,
                rw_kk, rw_ka, rw_rk, rw_gn_g, rw_gn_b, w_hg_st, w_rw_st, w_out, final_g, tgt):
    seq, dm = x.shape
    n_ctx_rows = ctx.shape[0]
    t = seq + n_ctx_rows
    hw = hg_norm_g.shape[-1]
    rw = rw_kk.shape[-1]
    nh_rw = rw // RW_HEAD
    n_ctx = n_ctx_rows // STEP
    tm = _tile(n_ctx_rows, (256, 128, 64))
    nt = t // tm
    nct = n_ctx_rows // tm
    n_sh_cols = 3 * rw + 4 * RW_LORA

    xs = jnp.concatenate([ctx, x], axis=0)
    cond = jnp.concatenate([c.reshape(1, dm), c_ctx.reshape(1, dm), jnp.zeros((6, dm), F32)], axis=0)
    final_g2 = final_g.reshape(1, dm)

    def unstack(a_st):
        return jnp.swapaxes(a_st, 0, 1).reshape(a_st.shape[1], -1)

    def restack(a, ns=N_SHARD):
        return jnp.swapaxes(a.reshape(a.shape[0], ns, -1), 0, 1)

    (sc,) = _row_call("cond_silu", lambda i, r, f: ([jax.nn.silu(r[0])], []), 1, 8, [(cond, 0, dm, 0)], [],
                      [(8, dm, F32, 0)], [])
    mod_st = _mm_n_st("mod_mm", sc, ada_st)
    mod = unstack(mod_st) + ada_b
    mod3 = mod.reshape(8, 3, dm)

    def pick(i, m3):
        r = jnp.where(i < nct, m3[1], m3[0])
        return r[0:1], r[1:2]

    def h_fn(i, r, f):
        shift, scale = pick(i, f[1])
        return [_fn_h(r[0], f[0], scale, shift)], []

    (h,) = _row_call("h_fwd", h_fn, nt, tm, [(xs, 0, dm, 0)], [norm_g, mod3], [(t, dm, BF16, 0)], [])
    proj = unstack(_mm_n_st("proj_mm", h, w_in_st))
    p_hg = proj[:, :5 * hw]
    p_rs = proj[:, 5 * hw:5 * hw + n_sh_cols]
    p_zr = proj[:, 5 * hw + n_sh_cols:5 * hw + n_sh_cols + rw]
    p_gt = proj[:, 5 * hw + n_sh_cols + rw:]

    o_hg, st_hg = [], []
    for d in range(2):
        o, st = _hg_scan_fwd(f"hg_scan_fwd{d}", p_hg, hg_lb[d], d, n_ctx)
        o_hg.append(o)
        st_hg.append(st)

    def hgpost_fn(i, r, f):
        return [_fn_hgpost(r[0], r[1], r[2], f[0])], []

    hg_in = [(o_hg[0], 0, hw, 0), (o_hg[1], 0, hw, 0), (p_hg, 4, hw, 0)]
    (y_hg,) = _row_call("hg_post", hgpost_fn, nt, tm, hg_in, [hg_norm_g], [(t, hw, BF16, 0)], [])

    sh = _shift_fwd("rw_shift", p_rs, rw_mu, n_ctx_rows)
    hps = []
    for d in range(2):
        hps.append([rw_w0[d].reshape(nh_rw, 1, RW_HEAD), jnp.swapaxes(rw_w2[d].reshape(RW_LORA, nh_rw, RW_HEAD), 0, 1),
                    rw_a0[d].reshape(nh_rw, 1, RW_HEAD), jnp.swapaxes(rw_a2[d].reshape(RW_LORA, nh_rw, RW_HEAD), 0, 1),
                    rw_kk.reshape(nh_rw, 1, RW_HEAD), rw_ka.reshape(nh_rw, 1, RW_HEAD)])
    y_rw_d, st_rw = [], []
    for d in range(2):
        y, st = _rw_scan_fwd(f"rw_scan_fwd{d}", sh, hps[d], d, n_ctx)
        y_rw_d.append(y)
        st_rw.append(st)

    rw_full = [rw_a0, rw_a2, rw_ka, rw_rk, rw_gn_g, rw_gn_b]
    lo = 3 * rw // LANE
    rw_in = [(y_rw_d[0], 0, rw, 0), (y_rw_d[1], 0, rw, 0), (sh, 0, rw, 0), (sh, 1, rw, 0), (sh, 2, rw, 0),
             (sh, lo + 1, LANE, 0), (p_zr, 0, rw, 0)]

    def rwpost_fn(i, r, f):
        return [_fn_rwpost(*r, *f)], []

    (y_rw,) = _row_call("rw_post", rwpost_fn, nt, tm, rw_in, rw_full, [(t, rw, BF16, 0)], [])

    a_hg = unstack(_mm_n_st("hg_out_mm", y_hg, w_hg_st))
    a_rw = unstack(_mm_n_st("rw_out_mm", y_rw, w_rw_st))
    mg_in = [(a_hg, 0, dm, 0), (a_rw, 0, dm, 0), (p_gt, 0, dm, 0), (p_gt, 1, dm, 0)]
    (merged,) = _row_call("merge", lambda i, r, f: ([_fn_merge(*r)], []), nt, tm, mg_in, [], [(t, dm, BF16, 0)], [])
    o_out = _mm_nn("out_mm", merged, w_out)

    def final_fn(i, r, f):
        gate = f[0][0][2:3]
        loss, vjp = jax.vjp(_fn_final, r[0], r[1], gate, f[1], r[2])
        dx, do, dgate, dfg, _ = vjp(jnp.ones((), F32))
        live = i >= nct
        zero = lambda a: jnp.where(live, a, 0.0)
        dmod = jnp.concatenate([jnp.concatenate([jnp.zeros((1, 2 * dm), F32), zero(dgate)], axis=1),
                                jnp.zeros((7, 3 * dm), F32)], axis=0)
        return [zero(dx), zero(do)], [jnp.broadcast_to(zero(loss), (8, LANE)), dmod, zero(dfg)]

    fin_in = [(xs, 0, dm, 0), (o_out, 0, dm, 0), (tgt, 0, dm, nct)]
    dx_res, d_o, loss_acc, dmod_gate, d_final_g = _row_call(
        "final", final_fn, nt, tm, fin_in, [mod3, final_g2], [(t, dm, F32, 0), (t, dm, BF16, 0)],
        [((8, LANE), F32), ((8, 3 * dm), F32), ((1, dm), F32)])

    g_w_out = _mm_tn("d_w_out", merged, d_o)
    d_merged = _mm_nt("d_merged", d_o, w_out)

    def merge_bwd(i, r, f):
        _, vjp = jax.vjp(_fn_merge, r[0], r[1], r[2], r[3])
        da, db, dgh, dgr = vjp(r[4])
        return [da, db, jnp.concatenate([dgh, dgr], axis=1)], []

    da_hg, da_rw, dp_gt = _row_call("merge_bwd", merge_bwd, nt, tm, mg_in + [(d_merged, 0, dm, 0)], [],
                                    [(t, dm, BF16, 0), (t, dm, BF16, 0), (t, 2 * dm, F32, 0)], [])
    g_w_hg_st = _mm_t_st("d_w_hg", y_hg, restack(da_hg))
    g_w_rw_st = _mm_t_st("d_w_rw", y_rw, restack(da_rw))
    dy_hg = _mm_st_t("d_y_hg", restack(da_hg), w_hg_st)
    dy_rw = _mm_st_t("d_y_rw", restack(da_rw), w_rw_st)

    def hgpost_bwd(i, r, f):
        _, vjp = jax.vjp(_fn_hgpost, r[0], r[1], r[2], f[0])
        dof, _, dz, dg = vjp(r[3])
        return [dof, dz], [dg]

    do_hg, dz_hg, g_hg_norm = _row_call("hg_post_bwd", hgpost_bwd, nt, tm, hg_in + [(dy_hg, 0, hw, 0)], [hg_norm_g],
                                        [(t, hw, F32, 0), (t, hw, F32, 0)], [((1, hw), F32)])
    dqs, dis, dfs, g_lb = [], [], [], []
    for d in range(2):
        dq, di, df, dlb = _hg_scan_bwd(f"hg_scan_bwd{d}", p_hg, hg_lb[d], st_hg[d], do_hg, d, n_ctx)
        dqs.append(dq)
        dis.append(di)
        dfs.append(df)
        g_lb.append(dlb)
    (dqi,) = _row_call("hg_dsum", lambda i, r, f: ([jnp.concatenate([r[0] + r[1], r[2] + r[3]], axis=1)], []), nt, tm,
                       [(dqs[0], 0, hw, 0), (dqs[1], 0, hw, 0), (dis[0], 0, hw, 0), (dis[1], 0, hw, 0)], [],
                       [(t, 2 * hw, F32, 0)], [])
    g_hg_lb = jnp.stack(g_lb, axis=0)

    def rwpost_bwd(i, r, f):
        _, vjp = jax.vjp(_fn_rwpost, *r[:7], *f)
        g = vjp(r[7])
        zl = jnp.zeros((g[5].shape[0], 2 * RW_LORA), F32)
        dmain = jnp.concatenate([g[2], g[3], g[4]], axis=1)
        return [g[0], dmain, jnp.concatenate([zl, g[5]], axis=1), g[6]], list(g[7:])

    dy_sum, dsh_p, dsl_p, dz_rw, g_a0_p, g_a2_p, g_ka_p, g_rk, g_gn_g, g_gn_b = _row_call(
        "rw_post_bwd", rwpost_bwd, nt, tm, rw_in + [(dy_rw, 0, rw, 0)], rw_full,
        [(t, rw, F32, 0), (t, 3 * rw, F32, 0), (t, 4 * RW_LORA, F32, 0), (t, rw, F32, 0)],
        [(a.shape, F32) for a in rw_full])
    dmains, dloras, hp_grads = [dsh_p], [dsl_p], []
    for d in range(2):
        res = _rw_scan_bwd(f"rw_scan_bwd{d}", sh, hps[d], st_rw[d], dy_sum, d, n_ctx)
        dmains.append(res[0])
        dloras.append(res[1])
        hp_grads.append(res[2:])
    dsh_parts = [jnp.concatenate([m, l], axis=1) for m, l in zip(dmains, dloras)]
    dp_rs, g_mu = _shift_bwd("rw_shift_bwd", p_rs, rw_mu, dsh_parts, n_ctx_rows)

    def flat(a):
        if a.shape[1] == 1:
            return a.reshape(rw)
        return jnp.swapaxes(a, 0, 1).reshape(RW_LORA, rw)

    g_w0 = jnp.stack([flat(hp_grads[d][0]) for d in range(2)], axis=0)
    g_w2 = jnp.stack([flat(hp_grads[d][1]) for d in range(2)], axis=0)
    g_a0 = jnp.stack([flat(hp_grads[d][2]) for d in range(2)], axis=0) + g_a0_p
    g_a2 = jnp.stack([flat(hp_grads[d][3]) for d in range(2)], axis=0) + g_a2_p
    g_kk = (flat(hp_grads[0][4]) + flat(hp_grads[1][4])).reshape(1, rw)
    g_ka = (flat(hp_grads[0][5]) + flat(hp_grads[1][5])).reshape(1, rw) + g_ka_p

    dproj = jnp.concatenate([dqi, dfs[0], dfs[1], dz_hg, dp_rs, dz_rw, dp_gt], axis=1).astype(BF16)
    dproj_st = restack(dproj)
    g_w_in_st = _mm_t_st("d_w_in", h, dproj_st)
    dh = _mm_st_t("d_h", dproj_st, w_in_st)

    def h_bwd(i, r, f):
        shift, scale = pick(i, f[1])
        _, vjp = jax.vjp(_fn_h, r[0], f[0], scale, shift)
        ds, dg, dscale, dshift = vjp(r[1])
        row = jnp.concatenate([dshift, dscale, jnp.zeros((1, dm), F32)], axis=1)
        z = jnp.zeros_like(row)
        is_ctx = i < nct
        dmod = jnp.concatenate([jnp.where(is_ctx, z, row), jnp.where(is_ctx, row, z), jnp.zeros((6, 3 * dm), F32)], axis=0)
        return [ds + r[2]], [dg, dmod]

    grad_x, g_norm_g, dmod_h = _row_call(
        "h_bwd", h_bwd, nt, tm, [(xs, 0, dm, 0), (dh, 0, dm, 0), (dx_res, 0, dm, 0)], [norm_g, mod3],
        [(seq, dm, F32, nct)], [((1, dm), F32), ((8, 3 * dm), F32)])
    dmod = dmod_h + dmod_gate
    g_ada_b = (dmod[0] + dmod[1]).reshape(1, 3 * dm)
    g_ada_st = _mm_t_st("d_ada_w", sc, restack(dmod))
    d_sc = _mm_st_t("d_cond", restack(dmod), ada_st)

    def cond_bwd(i, r, f):
        _, vjp = jax.vjp(jax.nn.silu, r[0])
        return [vjp(r[1])[0]], []

    (d_cond,) = _row_call("cond_bwd", cond_bwd, 1, 8, [(cond, 0, dm, 0), (d_sc, 0, dm, 0)], [], [(8, dm, F32, 0)], [])

    grads = dict(
        c_ctx=d_cond[1], ada_w=g_ada_st, ada_b=g_ada_b, norm_g=g_norm_g, w_in=g_w_in_st, hg_lb=g_hg_lb,
        hg_norm_g=g_hg_norm, rw_mu=g_mu, rw_w0=g_w0, rw_w2=g_w2, rw_a0=g_a0, rw_a2=g_a2, rw_kk=g_kk, rw_ka=g_ka,
        rw_rk=g_rk, rw_gn_g=g_gn_g, rw_gn_b=g_gn_b, w_hg_out=g_w_hg_st, w_rw_out=g_w_rw_st, w_out=g_w_out,
        final_g=d_final_g.reshape(dm))
    return loss_acc[0:1, 0:1], grad_x, grads


def _my_place():
    return lax.axis_index("x"), lax.axis_index("y"), lax.axis_index("c")


MIN_CHUNK_BYTES = 1 << 18
ROW_ALIGN = 16


def _n_chunks(rows, row_bytes):
    for n in (8, 4, 2):
        if rows % (n * ROW_ALIGN) == 0 and rows // n * row_bytes >= MIN_CHUNK_BYTES:
            return n
    return 1


def _row_bytes(a, lead=1):
    n = a.dtype.itemsize
    for d in a.shape[lead:]:
        n *= d
    return n


def _rows(ref, start, size):
    return ref.at[pl.ds(start, size)]


def _chunked(make, start, size, n):
    cs = size // n
    return [make(start + j * cs, cs) for j in range(n)]


_PEER_CHIPS = 3


def _weights_gather(name, big, small):
    nb, na = len(big), len(big) + len(small)
    arrays = list(big) + list(small)

    def body(*refs):
        outs = refs[na:2 * na]
        send_sems, recv_sems, fsend_sems, frecv_sems = refs[2 * na:]
        x, y, c = _my_place()
        me = 2 * x + y
        chips = [(1 - x, y), (x, 1 - y), (1 - x, 1 - y)]

        def over_ici(a, k, slot, r0, nr):
            px, py = chips[k]
            return pltpu.make_async_remote_copy(
                src_ref=_rows(outs[a].at[me], r0, nr), dst_ref=_rows(outs[a].at[slot], r0, nr), send_sem=send_sems.at[a, k],
                recv_sem=recv_sems.at[a, k], device_id=(px, py, c), device_id_type=pl.DeviceIdType.MESH)

        def to_sibling(a, k, r0, nr):
            px, py = chips[k]
            rows = _rows(outs[a].at[2 * px + py], r0, nr)
            return pltpu.make_async_remote_copy(
                src_ref=rows, dst_ref=rows, send_sem=fsend_sems.at[a, k], recv_sem=frecv_sems.at[a, k],
                device_id=(x, y, 1 - c), device_id_type=pl.DeviceIdType.MESH)

        span = []
        for a in range(na):
            rows = arrays[a].shape[1]
            if a < nb:
                half = rows // 2
                span.append((pl.multiple_of(c * half, ROW_ALIGN), pl.multiple_of((1 - c) * half, ROW_ALIGN), half,
                             _n_chunks(half, _row_bytes(arrays[a], 2))))
            else:
                span.append((0, 0, rows, 1))
        for a in range(na):
            mine, _, nr, n = span[a]
            for k in range(_PEER_CHIPS):
                for cp in _chunked(lambda r0, cs: over_ici(a, k, me, r0, cs), mine, nr, n):
                    cp.start()
        for k in range(_PEER_CHIPS):
            px, py = chips[k]
            for a in range(na):
                mine, _, nr, n = span[a]
                over_ici(a, k, 2 * px + py, mine, nr).wait_recv()
                if a < nb:
                    for cp in _chunked(lambda r0, cs: to_sibling(a, k, r0, cs), mine, nr, n):
                        cp.start()
        for k in range(_PEER_CHIPS):
            for a in range(nb):
                _, theirs, nr, _ = span[a]
                to_sibling(a, k, theirs, nr).wait_recv()
        for a in range(na):
            mine, _, nr, _ = span[a]
            for k in range(_PEER_CHIPS):
                over_ici(a, k, me, mine, nr).wait_send()
                if a < nb:
                    to_sibling(a, k, mine, nr).wait_send()

    hbm = pl.BlockSpec(memory_space=pl.ANY)
    sems = pltpu.SemaphoreType.DMA((na, _PEER_CHIPS))
    return pl.pallas_call(
        body, name=name, in_specs=[hbm] * na, out_specs=[hbm] * na,
        out_shape=[jax.ShapeDtypeStruct(a.shape, a.dtype) for a in arrays],
        input_output_aliases={a: a for a in range(na)}, scratch_shapes=[sems, sems, sems, sems],
    )(*arrays)


def _chip_scatter(name, arrays):
    na = len(arrays)

    def body(*refs):
        ins, outs = refs[:na], refs[na:2 * na]
        send_sems, recv_sems = refs[2 * na:]
        x, y, c = _my_place()
        me = 2 * x + y
        chips = [(1 - x, y), (x, 1 - y), (1 - x, 1 - y)]

        def remote(a, k, slot, r0, nr):
            px, py = chips[k]
            return pltpu.make_async_remote_copy(
                src_ref=_rows(ins[a].at[2 * px + py], r0, nr), dst_ref=_rows(outs[a].at[slot], r0, nr),
                send_sem=send_sems.at[a, k], recv_sem=recv_sems.at[a, k], device_id=(px, py, c),
                device_id_type=pl.DeviceIdType.MESH)

        for a in range(na):
            rows = arrays[a].shape[1]
            for k in range(_PEER_CHIPS):
                for cp in _chunked(lambda r0, cs: remote(a, k, me, r0, cs), 0, rows, _n_chunks(rows, _row_bytes(arrays[a], 2))):
                    cp.start()
        for k in range(_PEER_CHIPS):
            px, py = chips[k]
            for a in range(na):
                remote(a, k, 2 * px + py, 0, arrays[a].shape[1]).wait_recv()
        for a in range(na):
            for k in range(_PEER_CHIPS):
                remote(a, k, me, 0, arrays[a].shape[1]).wait_send()

    hbm = pl.BlockSpec(memory_space=pl.ANY)
    sems = pltpu.SemaphoreType.DMA((na, _PEER_CHIPS))
    return pl.pallas_call(
        body, name=name, in_specs=[hbm] * na, out_specs=[hbm] * na,
        out_shape=[jax.ShapeDtypeStruct(a.shape, a.dtype) for a in arrays], scratch_shapes=[sems, sems],
    )(*arrays)


PAIR_TILE_BYTES = 2 << 20


def _pair_exchange(name, a, place, reduce, out_dtype):
    rows, cols = a.shape[-2], a.shape[-1]
    half = rows // 2 if reduce else rows
    tr = _row_tile_for(half, cols, budget=PAIR_TILE_BYTES)
    nh = half // tr
    n_steps = (N_SHARD if reduce else 1) * nh

    def body(pc_ref, *refs):
        if reduce:
            keep_ref, send_ref, o_ref, land, send_sems, recv_sems, credit = refs
        else:
            send_ref, o_ref, land, send_sems, recv_sems, credit = refs
        x, y, c = _my_place()
        other = (x, y, 1 - c)
        t = pl.program_id(0) * nh + pl.program_id(1) if reduce else pl.program_id(0)
        slot = t % 2

        @pl.when(t >= 2)
        def _():
            pl.semaphore_wait(credit, 1)

        copy = pltpu.make_async_remote_copy(
            src_ref=send_ref, dst_ref=land.at[pl.ds(slot, 1)] if reduce else land.at[slot], send_sem=send_sems.at[slot],
            recv_sem=recv_sems.at[slot], device_id=other, device_id_type=pl.DeviceIdType.MESH)
        copy.start()
        copy.wait_recv()
        got = land[slot]
        o_ref[...] = ((keep_ref[...] + got) if reduce else got).astype(out_dtype)
        copy.wait_send()

        @pl.when(t < n_steps - 2)
        def _():
            pl.semaphore_signal(credit, inc=1, device_id=other, device_id_type=pl.DeviceIdType.MESH)

    if reduce:
        grid = (N_SHARD, nh)
        in_specs = [pl.BlockSpec((None, tr, cols), lambda j, i, pc: (j, pc[0] * nh + i, 0)),
                    pl.BlockSpec((1, tr, cols), lambda j, i, pc: (j, (1 - pc[0]) * nh + i, 0))]
        out_spec = pl.BlockSpec((None, tr, cols), lambda j, i, pc: (j, i, 0))
        out_shape = jax.ShapeDtypeStruct((N_SHARD, half, cols), out_dtype)
        operands = (a, a)
        sem = ("arbitrary", "arbitrary")
    else:
        grid = (nh,)
        in_specs = [pl.BlockSpec((tr, cols), lambda i, pc: (i, 0))]
        out_spec = pl.BlockSpec((tr, cols), lambda i, pc: (i, 0))
        out_shape = jax.ShapeDtypeStruct((half, cols), out_dtype)
        operands = (a,)
        sem = ("arbitrary",)
    return pl.pallas_call(
        body, name=name,
        grid_spec=pltpu.PrefetchScalarGridSpec(
            num_scalar_prefetch=1, grid=grid, in_specs=in_specs, out_specs=out_spec,
            scratch_shapes=[pltpu.VMEM((2, tr, cols), a.dtype), pltpu.SemaphoreType.DMA((2,)),
                            pltpu.SemaphoreType.DMA((2,)), pltpu.SemaphoreType.REGULAR]),
        out_shape=out_shape, compiler_params=_params(sem),
    )(place, *operands)


def _cast_into_slot(name, a, chip):
    rows, cols = a.shape
    tm = _row_tile_for(rows, cols)

    def body(pc_ref, a_ref, o_ref):
        o_ref[...] = a_ref[...].astype(BF16)

    return pl.pallas_call(
        body, name=name,
        grid_spec=pltpu.PrefetchScalarGridSpec(
            num_scalar_prefetch=1, grid=(rows // tm,), in_specs=[pl.BlockSpec((tm, cols), lambda i, pc: (i, 0))],
            out_specs=pl.BlockSpec((None, tm, cols), lambda i, pc: (pc[0], i, 0))),
        out_shape=jax.ShapeDtypeStruct((N_SHARD, rows, cols), BF16), compiler_params=_params(("parallel",)),
    )(chip, a)


def _sum_landed(name, landed, sent, chip):
    ns, rows, cols = landed.shape
    tm = _row_tile_for(rows, cols)

    def body(pc_ref, *refs):
        own_ref, o_ref = refs[ns], refs[ns + 1]
        me = pc_ref[0]
        terms = [jnp.where(me == j, own_ref[...], refs[j][...]).astype(F32) for j in range(ns)]
        o_ref[...] = _slot_sum(terms)

    def landed_spec(j):
        return pl.BlockSpec((None, tm, cols), lambda i, pc: (jnp.where(pc[0] == j, (j + 1) % ns, j), i, 0))

    return pl.pallas_call(
        body, name=name,
        grid_spec=pltpu.PrefetchScalarGridSpec(
            num_scalar_prefetch=1, grid=(rows // tm,),
            in_specs=[landed_spec(j) for j in range(ns)] + [pl.BlockSpec((None, tm, cols), lambda i, pc: (pc[0], i, 0))],
            out_specs=pl.BlockSpec((tm, cols), lambda i, pc: (i, 0))),
        out_shape=jax.ShapeDtypeStruct((rows, cols), F32), compiler_params=_params(("parallel",)),
    )(chip, *([landed] * ns), sent)


def _gather_all(name, a):
    def body(in_ref, out_ref, send_sems, recv_sems, local_sem):
        x, y, c = _my_place()
        me = 4 * x + 2 * y + c

        def peer(k):
            return (x ^ (k >> 2), y ^ ((k >> 1) & 1), c ^ (k & 1))

        def remote(k, land):
            return pltpu.make_async_remote_copy(
                src_ref=in_ref, dst_ref=out_ref.at[land], send_sem=send_sems.at[k - 1], recv_sem=recv_sems.at[k - 1],
                device_id=peer(k), device_id_type=pl.DeviceIdType.MESH)

        local = pltpu.make_async_copy(in_ref, out_ref.at[me], local_sem)
        local.start()
        for k in range(1, N_DEV):
            remote(k, me).start()
        for k in range(1, N_DEV):
            px, py, pc = peer(k)
            remote(k, 4 * px + 2 * py + pc).wait_recv()
        for k in range(1, N_DEV):
            remote(k, me).wait_send()
        local.wait()

    hbm = pl.BlockSpec(memory_space=pl.ANY)
    return pl.pallas_call(
        body, name=name, in_specs=[hbm], out_specs=hbm,
        out_shape=jax.ShapeDtypeStruct((N_DEV,) + a.shape, a.dtype),
        scratch_shapes=[pltpu.SemaphoreType.DMA((N_DEV - 1,)), pltpu.SemaphoreType.DMA((N_DEV - 1,)), pltpu.SemaphoreType.DMA],
    )(a)


def _row_tile_for(rows, cols, budget=1 << 20):
    for tm in (1024, 512, 256, 128, 64, 32, 16, 8):
        if rows % tm == 0 and tm * cols * 4 <= budget:
            return tm
    return rows


def _slot_sum(vals):
    g = vals[0]
    for v in vals[1:]:
        g = g + v
    return g


def _rowwise(name, fn, arrays, out_dtype):
    rows, cols = arrays[0].shape
    tm = _row_tile_for(rows, cols)

    def body(*refs):
        refs[-1][...] = fn(*[r[...] for r in refs[:-1]]).astype(out_dtype)

    blk = pl.BlockSpec((tm, cols), lambda i: (i, 0))
    return pl.pallas_call(
        body, name=name, grid=(rows // tm,), in_specs=[blk] * len(arrays), out_specs=blk,
        out_shape=jax.ShapeDtypeStruct((rows, cols), out_dtype), compiler_params=_params(("parallel",)),
    )(*arrays)


def _sum_slots(name, st):
    ns, rows, cols = st.shape
    tm = _row_tile_for(rows, cols)

    def body(s_ref, o_ref):
        o_ref[...] = _slot_sum([s_ref[j].astype(F32) for j in range(ns)])

    return pl.pallas_call(
        body, name=name, grid=(rows // tm,),
        in_specs=[pl.BlockSpec((ns, tm, cols), lambda i: (0, i, 0))],
        out_specs=pl.BlockSpec((tm, cols), lambda i: (i, 0)),
        out_shape=jax.ShapeDtypeStruct((rows, cols), F32),
        compiler_params=_params(("parallel",)),
    )(st)


ADAM_TILE_BYTES = 1 << 19


def _adam_update(g, p_ref, m_ref, v_ref, go_ref, d_ref, mo_ref, vo_ref):
    mn = ADAM_B1 * m_ref[...] + (1.0 - ADAM_B1) * g
    vn = ADAM_B2 * v_ref[...] + (1.0 - ADAM_B2) * jnp.square(g)
    m_hat = mn / (1.0 - ADAM_B1 ** ADAM_STEP)
    v_hat = vn / (1.0 - ADAM_B2 ** ADAM_STEP)
    go_ref[...] = g
    d_ref[...] = -ADAM_LR * (m_hat / (jnp.sqrt(v_hat) + ADAM_EPS) + ADAM_WD * p_ref[...])
    mo_ref[...] = mn
    vo_ref[...] = vn


def _adamw(name, p, m, v, gst):
    rows, cols = p.shape
    ns = gst.shape[0]
    tm = _row_tile_for(rows, cols, budget=ADAM_TILE_BYTES)

    def body(p_ref, m_ref, v_ref, g_ref, *outs):
        _adam_update(_slot_sum([g_ref[j] for j in range(ns)]), p_ref, m_ref, v_ref, *outs)

    blk = pl.BlockSpec((tm, cols), lambda i: (i, 0))
    return pl.pallas_call(
        body, name=name, grid=(rows // tm,),
        in_specs=[blk, blk, blk, pl.BlockSpec((ns, tm, cols), lambda i: (0, i, 0))],
        out_specs=[blk] * 4, out_shape=[jax.ShapeDtypeStruct((rows, cols), F32)] * 4,
        compiler_params=_params(("parallel",)),
    )(p, m, v, gst)


def _adamw_halves(name, p, m, v, mine, theirs, place):
    rows, cols = p.shape
    half = rows // 2
    tm = _row_tile_for(half, cols, budget=ADAM_TILE_BYTES)
    nh = half // tm

    def body(pc_ref, p_ref, m_ref, v_ref, mine_ref, theirs_ref, *outs):
        g = jnp.where(pl.program_id(0) == pc_ref[0], mine_ref[...], theirs_ref[...])
        _adam_update(g, p_ref, m_ref, v_ref, *outs)

    blk = pl.BlockSpec((tm, cols), lambda h, i, pc: (h * nh + i, 0))
    hblk = pl.BlockSpec((tm, cols), lambda h, i, pc: (i, 0))
    return pl.pallas_call(
        body, name=name,
        grid_spec=pltpu.PrefetchScalarGridSpec(
            num_scalar_prefetch=1, grid=(2, nh), in_specs=[blk, blk, blk, hblk, hblk], out_specs=[blk] * 4),
        out_shape=[jax.ShapeDtypeStruct((rows, cols), F32)] * 4, compiler_params=_params(("parallel", "parallel")),
    )(place, p, m, v, mine, theirs)


def _pack(parts, width=LANE, mult=8):
    flat = jnp.concatenate([a.reshape(-1) for a in parts])
    n = flat.shape[0]
    per = width * mult
    total = -(-n // per) * per
    return jnp.pad(flat, (0, total - n)).reshape(total // width, width)


def _unpack(packed, shapes):
    flat = packed.reshape(-1)
    out, off = [], 0
    for s in shapes:
        n = 1
        for d in s:
            n *= d
        out.append(flat[off:off + n].reshape(s))
        off += n
    return out


_SMALL_SHARDED = ("hg_lb", "rw_mu", "rw_w0", "rw_w2", "rw_a0", "rw_a2")
_REPLICATED = ("c_ctx", "ada_b", "norm_g", "hg_norm_g", "rw_kk", "rw_ka", "rw_rk", "rw_gn_g", "rw_gn_b", "final_g")
_BIG = ("ada_w", "w_in", "w_hg_out", "w_rw_out", "w_out")
_WEIGHTS = ("c_ctx", "ada_w", "ada_b", "norm_g", "w_in", "hg_lb", "hg_norm_g", "rw_mu", "rw_w0", "rw_w2", "rw_a0", "rw_a2",
            "rw_kk", "rw_ka", "rw_rk", "rw_gn_g", "rw_gn_b", "w_hg_out", "w_rw_out", "w_out", "final_g")


def _join_shards(st):
    a = jnp.moveaxis(st, 0, -2)
    return a.reshape(a.shape[:-2] + (a.shape[-2] * a.shape[-1],))


def _split_shards(a):
    s = a.reshape(a.shape[:-1] + (N_SHARD, a.shape[-1] // N_SHARD))
    return jnp.moveaxis(s, -2, 0)


def kernel(x, c, ctx, c_ctx, ada_w, ada_b, norm_g, w_in, hg_lb, hg_norm_g, rw_mu, rw_w0, rw_w2, rw_a0, rw_a2, rw_kk, rw_ka, rw_rk, rw_gn_g, rw_gn_b, w_hg_out, w_rw_out, w_out, final_g, loss_target, m_c_ctx, m_ada_w, m_ada_b, m_norm_g, m_w_in, m_hg_lb, m_hg_norm_g, m_rw_mu, m_rw_w0, m_rw_w2, m_rw_a0, m_rw_a2, m_rw_kk, m_rw_ka, m_rw_rk, m_rw_gn_g, m_rw_gn_b, m_w_hg_out, m_w_rw_out, m_w_out, m_final_g, v_c_ctx, v_ada_w, v_ada_b, v_norm_g, v_w_in, v_hg_lb, v_hg_norm_g, v_rw_mu, v_rw_w0, v_rw_w2, v_rw_a0, v_rw_a2, v_rw_kk, v_rw_ka, v_rw_rk, v_rw_gn_g, v_rw_gn_b, v_w_hg_out, v_w_rw_out, v_w_out, v_final_g):
    w = dict(c_ctx=c_ctx, ada_w=ada_w, ada_b=ada_b, norm_g=norm_g, w_in=w_in, hg_lb=hg_lb, hg_norm_g=hg_norm_g, rw_mu=rw_mu,
             rw_w0=rw_w0, rw_w2=rw_w2, rw_a0=rw_a0, rw_a2=rw_a2, rw_kk=rw_kk, rw_ka=rw_ka, rw_rk=rw_rk, rw_gn_g=rw_gn_g,
             rw_gn_b=rw_gn_b, w_hg_out=w_hg_out, w_rw_out=w_rw_out, w_out=w_out, final_g=final_g)
    m = dict(c_ctx=m_c_ctx, ada_w=m_ada_w, ada_b=m_ada_b, norm_g=m_norm_g, w_in=m_w_in, hg_lb=m_hg_lb, hg_norm_g=m_hg_norm_g,
             rw_mu=m_rw_mu, rw_w0=m_rw_w0, rw_w2=m_rw_w2, rw_a0=m_rw_a0, rw_a2=m_rw_a2, rw_kk=m_rw_kk, rw_ka=m_rw_ka,
             rw_rk=m_rw_rk, rw_gn_g=m_rw_gn_g, rw_gn_b=m_rw_gn_b, w_hg_out=m_w_hg_out, w_rw_out=m_w_rw_out, w_out=m_w_out,
             final_g=m_final_g)
    v = dict(c_ctx=v_c_ctx, ada_w=v_ada_w, ada_b=v_ada_b, norm_g=v_norm_g, w_in=v_w_in, hg_lb=v_hg_lb, hg_norm_g=v_hg_norm_g,
             rw_mu=v_rw_mu, rw_w0=v_rw_w0, rw_w2=v_rw_w2, rw_a0=v_rw_a0, rw_a2=v_rw_a2, rw_kk=v_rw_kk, rw_ka=v_rw_ka,
             rw_rk=v_rw_rk, rw_gn_g=v_rw_gn_g, rw_gn_b=v_rw_gn_b, w_hg_out=v_w_hg_out, w_rw_out=v_w_rw_out, w_out=v_w_out,
             final_g=v_final_g)

    def mat(a):
        return a.reshape(a.shape[-2], a.shape[-1])

    def pack_small(d):
        return _pack([d[n] for n in _SMALL_SHARDED], mult=2 * ROW_ALIGN)

    my_core = lax.axis_index("c").astype(jnp.int32).reshape(1)
    my_chip = (2 * lax.axis_index("x") + lax.axis_index("y")).astype(jnp.int32).reshape(1)

    small_shapes = [w[n].shape for n in _SMALL_SHARDED]
    big_bf = [_cast_into_slot(f"to_bf16_{n}", mat(w[n]), my_chip) for n in _BIG]
    small_mine = pack_small(w)
    small_slots = lax.dynamic_update_slice(jnp.zeros((N_SHARD,) + small_mine.shape, F32), small_mine[None], (my_chip[0], 0, 0))
    gathered = _weights_gather("weights_gather", big_bf, [small_slots])
    ada_st, w_in_st, w_hg_st, w_rw_st, w_out_st, small_st = gathered
    full_small = {}
    per_chip = [_unpack(small_st[j], small_shapes) for j in range(N_SHARD)]
    for i, n in enumerate(_SMALL_SHARDED):
        full_small[n] = _join_shards(jnp.stack([per_chip[j][i] for j in range(N_SHARD)], axis=0))
    dm = x.shape[-1]
    w_out_full = w_out_st.reshape(dm, dm)

    loss_b, grad_x, g = _local_step(
        x[0], c, ctx[0], c_ctx, ada_st, ada_b, norm_g, w_in_st, full_small["hg_lb"], hg_norm_g, full_small["rw_mu"][0],
        full_small["rw_w0"][0], full_small["rw_w2"][0], full_small["rw_a0"][0], full_small["rw_a2"][0], rw_kk, rw_ka, rw_rk,
        rw_gn_g, rw_gn_b, w_hg_st, w_rw_st, w_out_full, final_g, loss_target[0])
    loss = lax.psum(loss_b[0, 0], ("x", "y", "c"))

    g_small = {"hg_lb": g["hg_lb"], "rw_mu": g["rw_mu"][None], "rw_w0": g["rw_w0"][None], "rw_w2": g["rw_w2"][None],
               "rw_a0": g["rw_a0"][None], "rw_a2": g["rw_a2"][None]}
    split = {n: _split_shards(g_small[n]) for n in _SMALL_SHARDED}
    small_parts = jnp.stack([pack_small({n: split[n][j] for n in _SMALL_SHARDED}) for j in range(N_SHARD)], axis=0)
    partial = [g["ada_w"], g["w_in"], g["w_hg_out"], g["w_rw_out"], g["w_out"].reshape(N_SHARD, dm // N_SHARD, dm), small_parts]
    chip_sums = [_pair_exchange(f"grads_pair_sum{i}", a, my_core, True, BF16) for i, a in enumerate(partial)]
    landed = _chip_scatter("grads_scatter", chip_sums)
    mine = [_sum_landed(f"grads_sum{i}", a, s, my_chip) for i, (a, s) in enumerate(zip(landed, chip_sums))]
    theirs = [_pair_exchange(f"grads_pair_swap{i}", a, my_core, False, F32) for i, a in enumerate(mine)]
    rep_shapes = [w[n].shape for n in _REPLICATED]
    rep_all = _gather_all("grads_replicated", _pack([g[n].reshape(w[n].shape) for n in _REPLICATED]))

    res = {}
    for i, n in enumerate(_BIG):
        outs = _adamw_halves(f"adamw_{n}", mat(w[n]), mat(m[n]), mat(v[n]), mine[i], theirs[i], my_core)
        res[n] = [o.reshape(w[n].shape) for o in outs]
    outs = _adamw_halves("adamw_small", small_mine, pack_small(m), pack_small(v), mine[len(_BIG)], theirs[len(_BIG)], my_core)
    for i, vals in enumerate(zip(*[_unpack(o, small_shapes) for o in outs])):
        res[_SMALL_SHARDED[i]] = list(vals)
    outs = _adamw("adamw_replicated", _pack([w[n] for n in _REPLICATED]), _pack([m[n] for n in _REPLICATED]),
                  _pack([v[n] for n in _REPLICATED]), rep_all)
    for i, vals in enumerate(zip(*[_unpack(o, rep_shapes) for o in outs])):
        res[_REPLICATED[i]] = list(vals)

    return (loss, grad_x[None], *[res[n][0] for n in _WEIGHTS], *[res[n][1] for n in _WEIGHTS],
            *[res[n][2] for n in _WEIGHTS], *[res[n][3] for n in _WEIGHTS])
```

```python
import functools

import jax
import jax.numpy as jnp
from jax import lax
from jax.experimental import pallas as pl
from jax.experimental.pallas import tpu as pltpu

HI = lax.Precision.HIGHEST
F32 = jnp.float32
BF16 = jnp.bfloat16

NORM_EPS = 1e-6
HG_HEAD = 128
RW_HEAD = 64
RW_LORA = 64
RW_GN_EPS = 64e-5
GRID_W = 64
SUB = 16
STEP = 64
N_SHARD = 4
N_DEV = 8
LANE = 128

ADAM_LR = 0.001
ADAM_B1 = 0.9
ADAM_B2 = 0.999
ADAM_EPS = 1e-08
ADAM_WD = 0.01
ADAM_STEP = 10

VMEM_LIMIT = 56 * 1024 * 1024


def _params(sem=None):
    return pltpu.CompilerParams(dimension_semantics=sem, vmem_limit_bytes=VMEM_LIMIT)


def _tile(n, cands):
    for c in cands:
        if n % c == 0:
            return c
    return n


def _iota2(n, m, d):
    return lax.broadcasted_iota(jnp.int32, (n, m), d)


def _before(n, rev, strict):
    t, s = _iota2(n, n, 0), _iota2(n, n, 1)
    if rev:
        return (s > t) if strict else (s >= t)
    return (s < t) if strict else (s <= t)


def _bdot(a, b, spec):
    return jnp.einsum(spec, a, b, precision=HI, preferred_element_type=F32)


def _sdot(a, b, spec):
    return jnp.einsum(spec, a, b, precision=lax.Precision.DEFAULT, preferred_element_type=F32)


def _hg_step(s0, qraw, iin, fin, lb2, rev):
    c, w = qraw.shape
    h = w // HG_HEAD
    nsub = c // SUB
    lb = jax.nn.sigmoid(lb2[0:1] - lb2[1:2])
    q = jax.nn.silu(qraw)
    fg = lb + (1.0 - lb) * jax.nn.sigmoid(fin)
    kk = 1.0 - fg
    g = jnp.log(fg)
    bcum = jnp.dot(_before(c, rev, False).astype(F32), g, precision=HI, preferred_element_type=F32)
    rows = lax.broadcasted_iota(jnp.int32, (SUB, 1, 1), 0)

    def heads(a):
        return jnp.swapaxes(a.reshape(a.shape[0], h, HG_HEAD), 0, 1)

    def unheads(a):
        return jnp.swapaxes(a, 0, 1).reshape(a.shape[1], w)

    blocks = [slice(j * SUB, (j + 1) * SUB) for j in range(nsub)]
    outs = []
    for sl in blocks:
        qs, ks, vs, bc = [a[sl].reshape(SUB, h, HG_HEAD) for a in (q, kk, iin, bcum)]
        o = jnp.zeros((SUB, h, HG_HEAD), F32)
        for si in range(SUB):
            dec = jnp.exp(jnp.minimum(bc - bc[si:si + 1], 0.0))
            a = jnp.sum(qs * ks[si:si + 1] * dec, axis=-1, keepdims=True)
            valid = (rows <= si) if rev else (rows >= si)
            o = o + jnp.where(valid, a, 0.0) * vs[si:si + 1]
        outs.append(o.reshape(SUB, w))
    order = list(range(nsub - 1, -1, -1)) if rev else list(range(nsub))
    for pos in range(1, nsub):
        j, before = order[pos], order[:pos]
        first = (j + 1) * SUB - 1 if rev else j * SUB
        bstart = bcum[first:first + 1] - g[first:first + 1]
        qp = heads(q[blocks[j]] * jnp.exp(bcum[blocks[j]] - bstart))
        kp = heads(jnp.concatenate([kk[blocks[p]] * jnp.exp(bstart - bcum[blocks[p]]) for p in before], axis=0))
        vp = heads(jnp.concatenate([iin[blocks[p]] for p in before], axis=0))
        outs[j] = outs[j] + unheads(_sdot(_sdot(qp, kp, 'htk,hsk->hts'), vp, 'hts,hsv->htv'))
    o_state = unheads(_sdot(heads(q * jnp.exp(bcum)), s0, 'htk,hvk->htv'))
    last = 0 if rev else c - 1
    blast = bcum[last:last + 1]
    s_new = heads(jnp.exp(blast)) * s0 + _sdot(heads(iin), heads(kk * jnp.exp(blast - bcum)), 'hsv,hsk->hvk')
    return jnp.concatenate(outs, axis=0) + o_state, s_new


def _tri_solve(lmat, rhs, rev):
    hh, c, _ = lmat.shape
    nb = c // SUB
    diag = jnp.concatenate([lmat[:, i * SUB:(i + 1) * SUB, i * SUB:(i + 1) * SUB] for i in range(nb)], axis=0)
    dt = jnp.transpose(diag, (1, 2, 0))
    col = lax.broadcasted_iota(jnp.int32, (SUB, 1), 0)
    inv_rows = [None] * SUB
    order = list(range(SUB - 1, -1, -1)) if rev else list(range(SUB))
    for pos, t in enumerate(order):
        row = jnp.broadcast_to((col == t).astype(F32), (SUB, dt.shape[2]))
        for s in order[:pos]:
            row = row - dt[t, s:s + 1, :] * inv_rows[s]
        inv_rows[t] = row
    tinv = jnp.transpose(jnp.concatenate([r[None] for r in inv_rows], axis=0), (2, 0, 1))
    p = [None] * nb
    done = []
    for i in (range(nb - 1, -1, -1) if rev else range(nb)):
        r = rhs[:, i * SUB:(i + 1) * SUB]
        if done:
            lrow = jnp.concatenate([lmat[:, i * SUB:(i + 1) * SUB, m * SUB:(m + 1) * SUB] for m in done], axis=2)
            r = r - _sdot(lrow, jnp.concatenate([p[m] for m in done], axis=1), 'hts,hsv->htv')
        p[i] = _sdot(tinv[i * hh:(i + 1) * hh], r, 'hts,hsv->htv')
        done.append(i)
    return jnp.concatenate(p, axis=1)


def _rw_step(s0, r, k, v, wlo, alo, w0h, w2h, a0h, a2h, kkh, kah, rev):
    hh, c, _ = r.shape
    tl = jnp.broadcast_to(jnp.tanh(wlo)[None], (hh, c, wlo.shape[1]))
    al = jnp.broadcast_to(alo[None], (hh, c, alo.shape[1]))
    wlog = -jax.nn.softplus(-(w0h + _sdot(tl, w2h, 'hcl,hlj->hcj'))) - 0.5
    lw = -jnp.exp(wlog)
    a = jax.nn.sigmoid(a0h + _sdot(al, a2h, 'hcl,hlj->hcj'))
    kk = k * kkh
    kk = kk * lax.rsqrt(jnp.sum(kk * kk, axis=-1, keepdims=True) + 1e-12)
    kd = k * (1.0 + (a - 1.0) * kah)
    b = kk * a
    incl = jnp.broadcast_to(_before(c, rev, False).astype(F32)[None], (hh, c, c))
    cum = _bdot(incl, lw, 'hts,hsk->htk')
    ecum, encum = jnp.exp(cum), jnp.exp(-cum)
    alpha = jnp.exp(cum - lw) * kk
    beta = b * encum
    kappa = kd * encum
    rho = r * ecum
    m_lt = _before(c, rev, True)[None]
    m_le = _before(c, rev, False)[None]
    ar = jnp.concatenate([alpha, rho], axis=1)
    kb = jnp.concatenate([kappa, beta], axis=1)
    gram = _sdot(ar, kb, 'htk,hsk->hts')
    a_kap = jnp.where(m_lt, gram[:, :c, :c], 0.0)
    a_bet = jnp.where(m_lt, gram[:, :c, c:], 0.0)
    b_kap = jnp.where(m_le, gram[:, c:, :c], 0.0)
    b_bet = jnp.where(m_le, gram[:, c:, c:], 0.0)
    from_state = _sdot(ar, s0, 'htk,hvk->htv')
    p = _tri_solve(a_bet, from_state[:, :c] + _sdot(a_kap, v, 'hts,hsv->htv'), rev)
    vp = jnp.concatenate([v, -p], axis=1)
    y = from_state[:, c:] + _sdot(jnp.concatenate([b_kap, b_bet], axis=2), vp, 'hts,hsv->htv')
    stil = s0 + _sdot(vp, kb, 'hsv,hsk->hvk')
    last = 0 if rev else c - 1
    return y, stil * ecum[:, last:last + 1, :]


def _fn_h(s, norm_g, scale, shift):
    return s * lax.rsqrt(jnp.mean(s * s, axis=-1, keepdims=True) + NORM_EPS) * norm_g * (1.0 + scale) + shift


def _fn_hgpost(of, ob, z, g):
    tm, w = of.shape
    o = (of + ob).reshape(tm, w // HG_HEAD, HG_HEAD)
    o = o * lax.rsqrt(jnp.mean(o * o, axis=-1, keepdims=True) + NORM_EPS)
    return o.reshape(tm, w) * g * jax.nn.silu(z)


def _fn_rwpost(y0, y1, r, k, v, alo, z, a0, a2, k_a, r_k, gn_g, gn_b):
    tm, w = r.shape
    nh = w // RW_HEAD
    asum = 0.0
    for d in range(2):
        asum = asum + jax.nn.sigmoid(a0[d:d + 1] + jnp.dot(alo[:, d * RW_LORA:(d + 1) * RW_LORA], a2[d],
                                                           precision=HI, preferred_element_type=F32))
    k_sum = k * (2.0 + (asum - 2.0) * k_a)
    ys = (y0 + y1).reshape(tm, nh, RW_HEAD)
    mean = jnp.mean(ys, axis=-1, keepdims=True)
    var = jnp.mean(jnp.square(ys - mean), axis=-1, keepdims=True)
    y = ((ys - mean) * lax.rsqrt(var + RW_GN_EPS)).reshape(tm, w) * gn_g + gn_b
    bonus = jnp.sum((r * k_sum * r_k).reshape(tm, nh, RW_HEAD), axis=-1, keepdims=True) * v.reshape(tm, nh, RW_HEAD)
    return (y + bonus.reshape(tm, w)) * jax.nn.silu(z)


def _fn_merge(a, b, ghg, grw):
    return jax.nn.sigmoid(ghg) * a + jax.nn.sigmoid(grw) * b


def _fn_final(xs, o, gate, final_g, tgt):
    x2 = xs + gate * o
    y = x2 * lax.rsqrt(jnp.mean(x2 * x2, axis=-1, keepdims=True) + NORM_EPS) * final_g
    return 0.5 * jnp.sum(jnp.mean(jnp.square(y - tgt), axis=-1))


def _row_call(name, fn, n_tiles, tm, row_ins, full_ins, row_outs, acc_outs):
    n_ri, n_fi, n_ro = len(row_ins), len(full_ins), len(row_outs)

    def body(*refs):
        i = pl.program_id(0)
        rvals = [r[...] for r in refs[:n_ri]]
        fvals = [r[...] for r in refs[n_ri:n_ri + n_fi]]
        outs = refs[n_ri + n_fi:]
        ro, ao = fn(i, rvals, fvals)
        for ref, val in zip(outs[:n_ro], ro):
            ref[...] = val.astype(ref.dtype)
        for ref, val in zip(outs[n_ro:], ao):
            @pl.when(i == 0)
            def _(ref=ref):
                ref[...] = jnp.zeros_like(ref)
            ref[...] += val.astype(ref.dtype)

    def rspec(width, cb, off):
        return pl.BlockSpec((tm, width), lambda i: (jnp.maximum(i - off, 0), cb))

    def fspec(shape):
        nd = len(shape)
        return pl.BlockSpec(shape, lambda i: (0,) * nd)

    in_specs = [rspec(w, cb, off) for (_, cb, w, off) in row_ins] + [fspec(a.shape) for a in full_ins]
    out_specs = [rspec(w, 0, off) for (_, w, _, off) in row_outs] + [fspec(s) for (s, _) in acc_outs]
    out_shape = [jax.ShapeDtypeStruct((rows, w), dt) for (rows, w, dt, _) in row_outs] + \
                [jax.ShapeDtypeStruct(s, dt) for (s, dt) in acc_outs]
    res = pl.pallas_call(
        body, name=name, grid=(n_tiles,), in_specs=in_specs, out_specs=out_specs, out_shape=out_shape,
        compiler_params=_params(("arbitrary",)),
    )(*[a for (a, _, _, _) in row_ins], *full_ins)
    return list(res)


def _mm(name, a, b, m, n, k_steps, tm, tn, a_block, a_map, b_block, b_map, o_shape, o_block, o_map,
        contract, out_dtype=F32):
    def body(a_ref, b_ref, o_ref, acc_ref):
        kk = pl.program_id(2)

        @pl.when(kk == 0)
        def _():
            acc_ref[...] = jnp.zeros_like(acc_ref)

        acc_ref[...] += lax.dot_general(a_ref[...].astype(BF16), b_ref[...].astype(BF16),
                                        (contract, ((), ())), preferred_element_type=F32)

        @pl.when(kk == k_steps - 1)
        def _():
            o_ref[...] = acc_ref[...].astype(o_ref.dtype)

    return pl.pallas_call(
        body, name=name, grid=(m // tm, n // tn, k_steps),
        in_specs=[pl.BlockSpec(a_block, a_map), pl.BlockSpec(b_block, b_map)],
        out_specs=pl.BlockSpec(o_block, o_map),
        out_shape=jax.ShapeDtypeStruct(o_shape, out_dtype),
        scratch_shapes=[pltpu.VMEM((tm, tn), F32)],
        compiler_params=_params(("parallel", "parallel", "arbitrary")),
    )(a, b)


_TM = (768, 512, 256, 128, 64, 32, 16, 8)
_TN = (512, 256, 128)
_TK = (1024, 768, 512, 256, 128)
_TK_WIDE = (768, 512, 256, 128)
WIDE_OUT_BYTES = 32 << 20


def _tm_wide(m, ns):
    for tm in _TM:
        if m % tm == 0 and 3 * 4 * tm * ns <= WIDE_OUT_BYTES:
            return tm
    return m


def _mm_nn(name, a, b, out_dtype=F32):
    m, k = a.shape
    n = b.shape[1]
    tm, tn, tk = _tile(m, _TM), _tile(n, _TN), _tile(k, _TK)
    return _mm(name, a, b, m, n, k // tk, tm, tn, (tm, tk), lambda i, j, s: (i, s), (tk, tn), lambda i, j, s: (s, j),
               (m, n), (tm, tn), lambda i, j, s: (i, j), ((1,), (0,)), out_dtype)


def _mm_nt(name, a, b, out_dtype=F32):
    m, k = a.shape
    n = b.shape[0]
    tm, tn, tk = _tile(m, _TM), _tile(n, _TN), _tile(k, _TK)
    return _mm(name, a, b, m, n, k // tk, tm, tn, (tm, tk), lambda i, j, s: (i, s), (tn, tk), lambda i, j, s: (j, s),
               (m, n), (tm, tn), lambda i, j, s: (i, j), ((1,), (1,)), out_dtype)


def _mm_tn(name, a, b, out_dtype=F32):
    k, m = a.shape
    n = b.shape[1]
    tm, tn, tk = _tile(m, _TM), _tile(n, _TN), _tile(k, _TK)
    return _mm(name, a, b, m, n, k // tk, tm, tn, (tk, tm), lambda i, j, s: (s, i), (tk, tn), lambda i, j, s: (s, j),
               (m, n), (tm, tn), lambda i, j, s: (i, j), ((0,), (0,)), out_dtype)


def _mm_n_st(name, a, bst, out_dtype=F32):
    m, k = a.shape
    ns_, _, ns = bst.shape
    tm, tk = _tm_wide(m, ns), _tile(k, (512, 256, 128))
    return _mm(name, a, bst, m, ns_ * ns, k // tk, tm, ns,
               (tm, tk), lambda i, j, s: (i, s), (None, tk, ns), lambda i, j, s: (j, s, 0),
               (ns_, m, ns), (None, tm, ns), lambda i, j, s: (j, i, 0), ((1,), (0,)), out_dtype)


def _mm_st_t(name, ast, bst, out_dtype=F32):
    ns_, m, ns = ast.shape
    n = bst.shape[1]
    tm, tn = _tile(m, _TM), _tile(n, _TN)
    return _mm(name, ast, bst, m, n, ns_, tm, tn,
               (None, tm, ns), lambda i, j, s: (s, i, 0), (None, tn, ns), lambda i, j, s: (s, j, 0),
               (m, n), (tm, tn), lambda i, j, s: (i, j), ((1,), (1,)), out_dtype)


def _mm_t_st(name, a, bst, out_dtype=F32):
    k, m = a.shape
    ns_, _, ns = bst.shape
    tm, tk = _tile(m, _TN), _tile(k, _TK_WIDE)
    return _mm(name, a, bst, m, ns_ * ns, k // tk, tm, ns,
               (tk, tm), lambda i, j, s: (s, i), (None, tk, ns), lambda i, j, s: (j, s, 0),
               (ns_, m, ns), (None, tm, ns), lambda i, j, s: (j, i, 0), ((0,), (0,)), out_dtype)


def _scan_order(j, n_ctx, n_all, rev):
    if not rev:
        return j
    return jnp.where(j < n_ctx, n_ctx - 1 - j, n_all - 1 - (j - n_ctx))


def _hg_scan_fwd(name, p_hg, lb2, d, n_ctx):
    t, w5 = p_hg.shape
    w = w5 // 5
    h = w // HG_HEAD
    n = t // STEP
    rev = d == 1

    def body(q_ref, i_ref, f_ref, lb_ref, o_ref, st_ref, s_ref):
        j = pl.program_id(0)

        @pl.when(j == 0)
        def _():
            s_ref[...] = jnp.zeros_like(s_ref)

        s0 = s_ref[...]
        st_ref[...] = s0
        o, s1 = _hg_step(s0, q_ref[...], i_ref[...], f_ref[...], lb_ref[...], rev)
        o_ref[...] = o
        s_ref[...] = s1

    def rows(cb):
        return pl.BlockSpec((STEP, w), lambda j: (_scan_order(j, n_ctx, n, rev), cb))

    return pl.pallas_call(
        body, name=name, grid=(n,),
        in_specs=[rows(0), rows(1), rows(2 + d), pl.BlockSpec((2, w), lambda j: (0, 0))],
        out_specs=[rows(0), pl.BlockSpec((None, h, HG_HEAD, HG_HEAD), lambda j: (j, 0, 0, 0))],
        out_shape=[jax.ShapeDtypeStruct((t, w), F32), jax.ShapeDtypeStruct((n, h, HG_HEAD, HG_HEAD), F32)],
        scratch_shapes=[pltpu.VMEM((h, HG_HEAD, HG_HEAD), F32)],
        compiler_params=_params(("arbitrary",)),
    )(p_hg, p_hg, p_hg, lb2)


def _hg_scan_bwd(name, p_hg, lb2, states, do, d, n_ctx):
    t, w5 = p_hg.shape
    w = w5 // 5
    h = w // HG_HEAD
    n = t // STEP
    rev = d == 1

    def body(q_ref, i_ref, f_ref, lb_ref, st_ref, do_ref, dq_ref, di_ref, df_ref, dlb_ref, ds_ref):
        step = pl.program_id(0)

        @pl.when(step == 0)
        def _():
            ds_ref[...] = jnp.zeros_like(ds_ref)
            dlb_ref[...] = jnp.zeros_like(dlb_ref)

        _, vjp = jax.vjp(lambda s0, q, i, f, lb: _hg_step(s0, q, i, f, lb, rev),
                         st_ref[...], q_ref[...], i_ref[...], f_ref[...], lb_ref[...])
        ds0, dq, di, df, dlb = vjp((do_ref[...], ds_ref[...]))
        dq_ref[...] = dq
        di_ref[...] = di
        df_ref[...] = df
        dlb_ref[...] += dlb
        ds_ref[...] = ds0

    def rows(cb):
        return pl.BlockSpec((STEP, w), lambda s: (_scan_order(n - 1 - s, n_ctx, n, rev), cb))

    return pl.pallas_call(
        body, name=name, grid=(n,),
        in_specs=[rows(0), rows(1), rows(2 + d), pl.BlockSpec((2, w), lambda s: (0, 0)),
                  pl.BlockSpec((None, h, HG_HEAD, HG_HEAD), lambda s: (n - 1 - s, 0, 0, 0)), rows(0)],
        out_specs=[rows(0), rows(0), rows(0), pl.BlockSpec((2, w), lambda s: (0, 0))],
        out_shape=[jax.ShapeDtypeStruct((t, w), F32)] * 3 + [jax.ShapeDtypeStruct((2, w), F32)],
        scratch_shapes=[pltpu.VMEM((h, HG_HEAD, HG_HEAD), F32)],
        compiler_params=_params(("arbitrary",)),
    )(p_hg, p_hg, p_hg, lb2, states, do)


def _to_heads(a, nh):
    return jnp.stack([a[:, i * RW_HEAD:(i + 1) * RW_HEAD] for i in range(nh)], axis=0)


def _from_heads(a):
    return jnp.concatenate([a[i] for i in range(a.shape[0])], axis=-1)


def _rw_scan_fwd(name, sh, hp, d, n_ctx):
    t = sh.shape[0]
    w = (sh.shape[1] - 4 * RW_LORA) // 3
    nh = w // RW_HEAD
    n = t // STEP
    rev = d == 1
    lo = 3 * w // LANE

    def body(r_ref, k_ref, v_ref, wl_ref, al_ref, w0_ref, w2_ref, a0_ref, a2_ref, kk_ref, ka_ref,
             y_ref, st_ref, s_ref):
        j = pl.program_id(0)

        @pl.when(j == 0)
        def _():
            s_ref[...] = jnp.zeros_like(s_ref)

        s0 = s_ref[...]
        st_ref[...] = s0
        wl = wl_ref[...][:, d * RW_LORA:(d + 1) * RW_LORA]
        al = al_ref[...][:, d * RW_LORA:(d + 1) * RW_LORA]
        y, s1 = _rw_step(s0, _to_heads(r_ref[...], nh), _to_heads(k_ref[...], nh), _to_heads(v_ref[...], nh), wl, al,
                         w0_ref[...], w2_ref[...], a0_ref[...], a2_ref[...], kk_ref[...], ka_ref[...], rev)
        y_ref[...] = _from_heads(y)
        s_ref[...] = s1

    def rows(cb, width=w):
        return pl.BlockSpec((STEP, width), lambda j: (_scan_order(j, n_ctx, n, rev), cb))

    def whole(a):
        nd = a.ndim
        return pl.BlockSpec(a.shape, lambda j: (0,) * nd)

    return pl.pallas_call(
        body, name=name, grid=(n,),
        in_specs=[rows(0), rows(1), rows(2), rows(lo, LANE), rows(lo + 1, LANE)] + [whole(a) for a in hp],
        out_specs=[rows(0), pl.BlockSpec((None, nh, RW_HEAD, RW_HEAD), lambda j: (j, 0, 0, 0))],
        out_shape=[jax.ShapeDtypeStruct((t, w), F32), jax.ShapeDtypeStruct((n, nh, RW_HEAD, RW_HEAD), F32)],
        scratch_shapes=[pltpu.VMEM((nh, RW_HEAD, RW_HEAD), F32)],
        compiler_params=_params(("arbitrary",)),
    )(sh, sh, sh, sh, sh, *hp)


def _rw_scan_bwd(name, sh, hp, states, dy, d, n_ctx):
    t = sh.shape[0]
    w = (sh.shape[1] - 4 * RW_LORA) // 3
    nh = w // RW_HEAD
    n = t // STEP
    rev = d == 1
    lo = 3 * w // LANE

    def body(r_ref, k_ref, v_ref, wl_ref, al_ref, w0_ref, w2_ref, a0_ref, a2_ref, kk_ref, ka_ref, st_ref, dy_ref,
             dsh_ref, dw0_ref, dw2_ref, da0_ref, da2_ref, dkk_ref, dka_ref, ds_ref):
        step = pl.program_id(0)
        pouts = (dw0_ref, dw2_ref, da0_ref, da2_ref, dkk_ref, dka_ref)

        @pl.when(step == 0)
        def _():
            ds_ref[...] = jnp.zeros_like(ds_ref)
            for ref in pouts:
                ref[...] = jnp.zeros_like(ref)

        wl = wl_ref[...][:, d * RW_LORA:(d + 1) * RW_LORA]
        al = al_ref[...][:, d * RW_LORA:(d + 1) * RW_LORA]
        _, vjp = jax.vjp(functools.partial(_rw_step, rev=rev),
                         st_ref[...], _to_heads(r_ref[...], nh), _to_heads(k_ref[...], nh), _to_heads(v_ref[...], nh),
                         wl, al, w0_ref[...], w2_ref[...], a0_ref[...], a2_ref[...], kk_ref[...], ka_ref[...])
        g = vjp((_to_heads(dy_ref[...], nh), ds_ref[...]))
        ds_ref[...] = g[0]
        zero = jnp.zeros_like(g[4])
        lora = [zero] * 4
        lora[d], lora[2 + d] = g[4], g[5]
        dsh_ref[...] = jnp.concatenate([_from_heads(g[1]), _from_heads(g[2]), _from_heads(g[3])] + lora, axis=-1)
        for ref, val in zip(pouts, g[6:]):
            ref[...] += val

    def rows(cb, width=w):
        return pl.BlockSpec((STEP, width), lambda s: (_scan_order(n - 1 - s, n_ctx, n, rev), cb))

    def whole(a):
        nd = a.ndim
        return pl.BlockSpec(a.shape, lambda s: (0,) * nd)

    return pl.pallas_call(
        body, name=name, grid=(n,),
        in_specs=[rows(0), rows(1), rows(2), rows(lo, LANE), rows(lo + 1, LANE)] + [whole(a) for a in hp] +
                 [pl.BlockSpec((None, nh, RW_HEAD, RW_HEAD), lambda s: (n - 1 - s, 0, 0, 0)), rows(0)],
        out_specs=[rows(0, sh.shape[1])] + [whole(a) for a in hp],
        out_shape=[jax.ShapeDtypeStruct(sh.shape, F32)] + [jax.ShapeDtypeStruct(a.shape, F32) for a in hp],
        scratch_shapes=[pltpu.VMEM((nh, RW_HEAD, RW_HEAD), F32)],
        compiler_params=_params(("arbitrary",)),
    )(sh, sh, sh, sh, sh, *hp, states, dy)


def _rw_scan_bwd_both(name, sh, hps, states, dy, n_ctx):
    t = sh.shape[0]
    w = (sh.shape[1] - 4 * RW_LORA) // 3
    nh = w // RW_HEAD
    n = t // STEP
    lo = 3 * w // LANE
    n_in, n_p = 13, 6

    def body(*refs):
        step = pl.program_id(0)
        ins = [refs[d * n_in:(d + 1) * n_in] for d in range(2)]
        outs = [refs[2 * n_in + d * (1 + n_p):2 * n_in + (d + 1) * (1 + n_p)] for d in range(2)]
        ds_refs = refs[2 * n_in + 2 * (1 + n_p):]

        @pl.when(step == 0)
        def _():
            for d in range(2):
                ds_refs[d][...] = jnp.zeros_like(ds_refs[d])
                for ref in outs[d][1:]:
                    ref[...] = jnp.zeros_like(ref)

        for d in range(2):
            r_ref, k_ref, v_ref, wl_ref, al_ref = ins[d][:5]
            hp_refs, st_ref, dy_ref = ins[d][5:11], ins[d][11], ins[d][12]
            wl = wl_ref[...][:, d * RW_LORA:(d + 1) * RW_LORA]
            al = al_ref[...][:, d * RW_LORA:(d + 1) * RW_LORA]
            _, vjp = jax.vjp(functools.partial(_rw_step, rev=d == 1),
                             st_ref[...], _to_heads(r_ref[...], nh), _to_heads(k_ref[...], nh), _to_heads(v_ref[...], nh),
                             wl, al, *[p[...] for p in hp_refs])
            g = vjp((_to_heads(dy_ref[...], nh), ds_refs[d][...]))
            ds_refs[d][...] = g[0]
            zero = jnp.zeros_like(g[4])
            lora = [zero] * 4
            lora[d], lora[2 + d] = g[4], g[5]
            outs[d][0][...] = jnp.concatenate([_from_heads(g[1]), _from_heads(g[2]), _from_heads(g[3])] + lora, axis=-1)
            for ref, val in zip(outs[d][1:], g[6:]):
                ref[...] += val

    def rows(d, cb, width=w):
        return pl.BlockSpec((STEP, width), lambda s: (_scan_order(n - 1 - s, n_ctx, n, d == 1), cb))

    def whole(a):
        nd = a.ndim
        return pl.BlockSpec(a.shape, lambda s: (0,) * nd)

    in_specs, operands, out_specs, out_shape = [], [], [], []
    for d in range(2):
        in_specs += [rows(d, 0), rows(d, 1), rows(d, 2), rows(d, lo, LANE), rows(d, lo + 1, LANE)]
        in_specs += [whole(a) for a in hps[d]]
        in_specs += [pl.BlockSpec((None, nh, RW_HEAD, RW_HEAD), lambda s: (n - 1 - s, 0, 0, 0)), rows(d, 0)]
        operands += [sh] * 5 + list(hps[d]) + [states[d], dy]
        out_specs += [rows(d, 0, sh.shape[1])] + [whole(a) for a in hps[d]]
        out_shape += [jax.ShapeDtypeStruct(sh.shape, F32)] + [jax.ShapeDtypeStruct(a.shape, F32) for a in hps[d]]
    res = pl.pallas_call(
        body, name=name, grid=(n,), in_specs=in_specs, out_specs=out_specs, out_shape=out_shape,
        scratch_shapes=[pltpu.VMEM((nh, RW_HEAD, RW_HEAD), F32)] * 2, compiler_params=_params(("arbitrary",)),
    )(*operands)
    return [res[0], res[1 + n_p]], [res[1:1 + n_p], res[2 + n_p:]]


def _shift_masks(t, n_ctx_rows):
    row = lax.broadcasted_iota(jnp.int32, (t, 1), 0)
    isx = row >= n_ctx_rows
    pos = jnp.where(isx, row - n_ctx_rows, row)
    col = jnp.where(isx, jnp.bitwise_and(pos, GRID_W - 1), pos)
    ncol = jnp.where(isx, GRID_W, n_ctx_rows)
    n_x = t - n_ctx_rows
    ml = col != 0
    mr = col != ncol - 1
    mu = isx & (pos >= GRID_W)
    md = isx & (pos < n_x - GRID_W)
    return ml, mr, mu, md, isx


def _shift_fwd(name, p, mu, n_ctx_rows):
    t, c = p.shape
    cw = LANE

    def body(p_ref, mu_ref, o_ref):
        x = p_ref[...]
        m = mu_ref[...]
        ml, mr, mup, mdn, isx = _shift_masks(t, n_ctx_rows)
        left = jnp.where(ml, pltpu.roll(x, 1, 0), 0.0)
        right = jnp.where(mr, pltpu.roll(x, t - 1, 0), 0.0)
        up = jnp.where(mup, pltpu.roll(x, GRID_W, 0), 0.0)
        down = jnp.where(mdn, pltpu.roll(x, t - GRID_W, 0), 0.0)
        out = x + m[0:1] * (left - x) + m[1:2] * (right - x)
        vert = m[2:3] * (up - x) + m[3:4] * (down - x)
        o_ref[...] = out + jnp.where(isx, vert, 0.0)

    return pl.pallas_call(
        body, name=name, grid=(c // cw,),
        in_specs=[pl.BlockSpec((t, cw), lambda j: (0, j)), pl.BlockSpec((4, cw), lambda j: (0, j))],
        out_specs=pl.BlockSpec((t, cw), lambda j: (0, j)),
        out_shape=jax.ShapeDtypeStruct((t, c), F32),
        compiler_params=_params(("parallel",)),
    )(p, mu)


def _shift_bwd(name, p, mu, dparts, n_ctx_rows):
    t, c = p.shape
    cw = LANE
    npart = len(dparts)

    def body(*refs):
        p_ref, mu_ref = refs[0], refs[1]
        dp_ref, dmu_ref = refs[2 + npart], refs[3 + npart]
        x = p_ref[...]
        m = mu_ref[...]
        g = refs[2][...]
        for r in refs[3:2 + npart]:
            g = g + r[...]
        ml, mr, mup, mdn, isx = _shift_masks(t, n_ctx_rows)
        left = jnp.where(ml, pltpu.roll(x, 1, 0), 0.0)
        right = jnp.where(mr, pltpu.roll(x, t - 1, 0), 0.0)
        up = jnp.where(mup, pltpu.roll(x, GRID_W, 0), 0.0)
        down = jnp.where(mdn, pltpu.roll(x, t - GRID_W, 0), 0.0)
        gx = jnp.where(isx, g, 0.0)
        dmu_ref[...] = jnp.concatenate([
            jnp.sum(g * (left - x), axis=0, keepdims=True), jnp.sum(g * (right - x), axis=0, keepdims=True),
            jnp.sum(gx * (up - x), axis=0, keepdims=True), jnp.sum(gx * (down - x), axis=0, keepdims=True)], axis=0)
        coef = 1.0 - m[0:1] - m[1:2] - jnp.where(isx, m[2:3] + m[3:4], 0.0)
        dp = coef * g
        dp = dp + m[0:1] * pltpu.roll(jnp.where(ml, g, 0.0), t - 1, 0)
        dp = dp + m[1:2] * pltpu.roll(jnp.where(mr, g, 0.0), 1, 0)
        dp = dp + m[2:3] * pltpu.roll(jnp.where(mup, g, 0.0), t - GRID_W, 0)
        dp = dp + m[3:4] * pltpu.roll(jnp.where(mdn, g, 0.0), GRID_W, 0)
        dp_ref[...] = dp

    col = pl.BlockSpec((t, cw), lambda j: (0, j))
    par = pl.BlockSpec((4, cw), lambda j: (0, j))
    return pl.pallas_call(
        body, name=name, grid=(c // cw,),
        in_specs=[col, par] + [col] * npart,
        out_specs=[col, par],
        out_shape=[jax.ShapeDtypeStruct((t, c), F32), jax.ShapeDtypeStruct((4, c), F32)],
        compiler_params=_params(("parallel",)),
    )(p, mu, *dparts)


def _local_step(x, c, ctx, c_ctx, ada_st, ada_b, norm_g, w_in_st, hg_lb, hg_norm_g, rw_mu, rw_w0, rw_w2, rw_a0, rw_a2,
                rw_kk, rw_ka, rw_rk, rw_gn_g, rw_gn_b, w_hg_st, w_rw_st, w_out, final_g, tgt):
    seq, dm = x.shape
    n_ctx_rows = ctx.shape[0]
    t = seq + n_ctx_rows
    hw = hg_norm_g.shape[-1]
    rw = rw_kk.shape[-1]
    nh_rw = rw // RW_HEAD
    n_ctx = n_ctx_rows // STEP
    tm = _tile(n_ctx_rows, (256, 128, 64))
    nt = t // tm
    nct = n_ctx_rows // tm
    n_sh_cols = 3 * rw + 4 * RW_LORA

    xs = jnp.concatenate([ctx, x], axis=0)
    cond = jnp.concatenate([c.reshape(1, dm), c_ctx.reshape(1, dm), jnp.zeros((6, dm), F32)], axis=0)
    final_g2 = final_g.reshape(1, dm)

    def unstack(a_st):
        return jnp.swapaxes(a_st, 0, 1).reshape(a_st.shape[1], -1)

    def restack(a, ns=N_SHARD):
        return jnp.swapaxes(a.reshape(a.shape[0], ns, -1), 0, 1)

    (sc,) = _row_call("cond_silu", lambda i, r, f: ([jax.nn.silu(r[0])], []), 1, 8, [(cond, 0, dm, 0)], [],
                      [(8, dm, F32, 0)], [])
    mod_st = _mm_n_st("mod_mm", sc, ada_st)
    def add(name, terms, shape):
        flat2 = [a.reshape(-1, a.shape[-1]) for a in terms]
        return _rowwise(name, lambda *v: _slot_sum(list(v)), flat2, F32).reshape(shape)

    mod = add("mod_bias", [unstack(mod_st), jnp.broadcast_to(ada_b, (8, 3 * dm))], (8, 3 * dm))
    mod3 = mod.reshape(8, 3, dm)

    def pick(i, m3):
        r = jnp.where(i < nct, m3[1], m3[0])
        return r[0:1], r[1:2]

    def h_fn(i, r, f):
        shift, scale = pick(i, f[1])
        return [_fn_h(r[0], f[0], scale, shift)], []

    (h,) = _row_call("h_fwd", h_fn, nt, tm, [(xs, 0, dm, 0)], [norm_g, mod3], [(t, dm, BF16, 0)], [])
    proj = unstack(_mm_n_st("proj_mm", h, w_in_st))
    p_hg = proj[:, :5 * hw]
    p_rs = proj[:, 5 * hw:5 * hw + n_sh_cols]
    p_zr = proj[:, 5 * hw + n_sh_cols:5 * hw + n_sh_cols + rw]
    p_gt = proj[:, 5 * hw + n_sh_cols + rw:]

    o_hg, st_hg = [], []
    for d in range(2):
        o, st = _hg_scan_fwd(f"hg_scan_fwd{d}", p_hg, hg_lb[d], d, n_ctx)
        o_hg.append(o)
        st_hg.append(st)

    def hgpost_fn(i, r, f):
        return [_fn_hgpost(r[0], r[1], r[2], f[0])], []

    hg_in = [(o_hg[0], 0, hw, 0), (o_hg[1], 0, hw, 0), (p_hg, 4, hw, 0)]
    (y_hg,) = _row_call("hg_post", hgpost_fn, nt, tm, hg_in, [hg_norm_g], [(t, hw, BF16, 0)], [])

    sh = _shift_fwd("rw_shift", p_rs, rw_mu, n_ctx_rows)
    hps = []
    for d in range(2):
        hps.append([rw_w0[d].reshape(nh_rw, 1, RW_HEAD), jnp.swapaxes(rw_w2[d].reshape(RW_LORA, nh_rw, RW_HEAD), 0, 1),
                    rw_a0[d].reshape(nh_rw, 1, RW_HEAD), jnp.swapaxes(rw_a2[d].reshape(RW_LORA, nh_rw, RW_HEAD), 0, 1),
                    rw_kk.reshape(nh_rw, 1, RW_HEAD), rw_ka.reshape(nh_rw, 1, RW_HEAD)])
    y_rw_d, st_rw = [], []
    for d in range(2):
        y, st = _rw_scan_fwd(f"rw_scan_fwd{d}", sh, hps[d], d, n_ctx)
        y_rw_d.append(y)
        st_rw.append(st)

    rw_full = [rw_a0, rw_a2, rw_ka, rw_rk, rw_gn_g, rw_gn_b]
    lo = 3 * rw // LANE
    rw_in = [(y_rw_d[0], 0, rw, 0), (y_rw_d[1], 0, rw, 0), (sh, 0, rw, 0), (sh, 1, rw, 0), (sh, 2, rw, 0),
             (sh, lo + 1, LANE, 0), (p_zr, 0, rw, 0)]

    def rwpost_fn(i, r, f):
        return [_fn_rwpost(*r, *f)], []

    (y_rw,) = _row_call("rw_post", rwpost_fn, nt, tm, rw_in, rw_full, [(t, rw, BF16, 0)], [])

    a_hg = unstack(_mm_n_st("hg_out_mm", y_hg, w_hg_st))
    a_rw = unstack(_mm_n_st("rw_out_mm", y_rw, w_rw_st))
    mg_in = [(a_hg, 0, dm, 0), (a_rw, 0, dm, 0), (p_gt, 0, dm, 0), (p_gt, 1, dm, 0)]
    (merged,) = _row_call("merge", lambda i, r, f: ([_fn_merge(*r)], []), nt, tm, mg_in, [], [(t, dm, BF16, 0)], [])
    o_out = _mm_nn("out_mm", merged, w_out)

    def final_fn(i, r, f):
        gate = f[0][0][2:3]
        loss, vjp = jax.vjp(_fn_final, r[0], r[1], gate, f[1], r[2])
        dx, do, dgate, dfg, _ = vjp(jnp.ones((), F32))
        live = i >= nct
        zero = lambda a: jnp.where(live, a, 0.0)
        dmod = jnp.concatenate([jnp.concatenate([jnp.zeros((1, 2 * dm), F32), zero(dgate)], axis=1),
                                jnp.zeros((7, 3 * dm), F32)], axis=0)
        return [zero(dx), zero(do)], [jnp.broadcast_to(zero(loss), (8, LANE)), dmod, zero(dfg)]

    fin_in = [(xs, 0, dm, 0), (o_out, 0, dm, 0), (tgt, 0, dm, nct)]
    dx_res, d_o, loss_acc, dmod_gate, d_final_g = _row_call(
        "final", final_fn, nt, tm, fin_in, [mod3, final_g2], [(t, dm, F32, 0), (t, dm, BF16, 0)],
        [((8, LANE), F32), ((8, 3 * dm), F32), ((1, dm), F32)])

    g_w_out = _mm_tn("d_w_out", merged, d_o)
    d_merged = _mm_nt("d_merged", d_o, w_out)

    def merge_bwd(i, r, f):
        _, vjp = jax.vjp(_fn_merge, r[0], r[1], r[2], r[3])
        da, db, dgh, dgr = vjp(r[4])
        return [da, db, jnp.concatenate([dgh, dgr], axis=1)], []

    da_hg, da_rw, dp_gt = _row_call("merge_bwd", merge_bwd, nt, tm, mg_in + [(d_merged, 0, dm, 0)], [],
                                    [(t, dm, BF16, 0), (t, dm, BF16, 0), (t, 2 * dm, F32, 0)], [])
    g_w_hg_st = _mm_t_st("d_w_hg", y_hg, restack(da_hg))
    g_w_rw_st = _mm_t_st("d_w_rw", y_rw, restack(da_rw))
    dy_hg = _mm_st_t("d_y_hg", restack(da_hg), w_hg_st)
    dy_rw = _mm_st_t("d_y_rw", restack(da_rw), w_rw_st)

    def hgpost_bwd(i, r, f):
        _, vjp = jax.vjp(_fn_hgpost, r[0], r[1], r[2], f[0])
        dof, _, dz, dg = vjp(r[3])
        return [dof, dz], [dg]

    do_hg, dz_hg, g_hg_norm = _row_call("hg_post_bwd", hgpost_bwd, nt, tm, hg_in + [(dy_hg, 0, hw, 0)], [hg_norm_g],
                                        [(t, hw, F32, 0), (t, hw, F32, 0)], [((1, hw), F32)])
    dqs, dis, dfs, g_lb = [], [], [], []
    for d in range(2):
        dq, di, df, dlb = _hg_scan_bwd(f"hg_scan_bwd{d}", p_hg, hg_lb[d], st_hg[d], do_hg, d, n_ctx)
        dqs.append(dq)
        dis.append(di)
        dfs.append(df)
        g_lb.append(dlb)
    (dqi,) = _row_call("hg_dsum", lambda i, r, f: ([jnp.concatenate([r[0] + r[1], r[2] + r[3]], axis=1)], []), nt, tm,
                       [(dqs[0], 0, hw, 0), (dqs[1], 0, hw, 0), (dis[0], 0, hw, 0), (dis[1], 0, hw, 0)], [],
                       [(t, 2 * hw, F32, 0)], [])
    g_hg_lb = jnp.stack(g_lb, axis=0)

    def rwpost_bwd(i, r, f):
        _, vjp = jax.vjp(_fn_rwpost, *r[:7], *f)
        g = vjp(r[7])
        zl = jnp.zeros((g[5].shape[0], 2 * RW_LORA), F32)
        return [g[0], jnp.concatenate([g[2], g[3], g[4], zl, g[5]], axis=1), g[6]], list(g[7:])

    dy_sum, dsh_p, dz_rw, g_a0_p, g_a2_p, g_ka_p, g_rk, g_gn_g, g_gn_b = _row_call(
        "rw_post_bwd", rwpost_bwd, nt, tm, rw_in + [(dy_rw, 0, rw, 0)], rw_full,
        [(t, rw, F32, 0), (t, n_sh_cols, F32, 0), (t, rw, F32, 0)], [(a.shape, F32) for a in rw_full])
    dsh_dirs, hp_grads = _rw_scan_bwd_both("rw_scan_bwd", sh, hps, st_rw, dy_sum, n_ctx)
    dp_rs, g_mu = _shift_bwd("rw_shift_bwd", p_rs, rw_mu, [dsh_p] + dsh_dirs, n_ctx_rows)

    def flat(a):
        if a.shape[1] == 1:
            return a.reshape(rw)
        return jnp.swapaxes(a, 0, 1).reshape(RW_LORA, rw)

    g_w0 = jnp.stack([flat(hp_grads[d][0]) for d in range(2)], axis=0)
    g_w2 = jnp.stack([flat(hp_grads[d][1]) for d in range(2)], axis=0)
    g_a0 = add("g_a0", [jnp.stack([flat(hp_grads[d][2]) for d in range(2)], axis=0), g_a0_p], (2, rw))
    g_a2 = add("g_a2", [jnp.stack([flat(hp_grads[d][3]) for d in range(2)], axis=0), g_a2_p], (2, RW_LORA, rw))
    g_kk = add("g_kk", [flat(hp_grads[0][4]).reshape(1, rw), flat(hp_grads[1][4]).reshape(1, rw)], (1, rw))
    g_ka = add("g_ka", [flat(hp_grads[0][5]).reshape(1, rw), flat(hp_grads[1][5]).reshape(1, rw), g_ka_p], (1, rw))

    dproj = jnp.concatenate([dqi, dfs[0], dfs[1], dz_hg, dp_rs, dz_rw, dp_gt], axis=1).astype(BF16)
    dproj_st = restack(dproj)
    g_w_in_st = _mm_t_st("d_w_in", h, dproj_st)
    dh = _mm_st_t("d_h", dproj_st, w_in_st)

    def h_bwd(i, r, f):
        shift, scale = pick(i, f[1])
        _, vjp = jax.vjp(_fn_h, r[0], f[0], scale, shift)
        ds, dg, dscale, dshift = vjp(r[1])
        row = jnp.concatenate([dshift, dscale, jnp.zeros((1, dm), F32)], axis=1)
        z = jnp.zeros_like(row)
        is_ctx = i < nct
        dmod = jnp.concatenate([jnp.where(is_ctx, z, row), jnp.where(is_ctx, row, z), jnp.zeros((6, 3 * dm), F32)], axis=0)
        return [ds + r[2]], [dg, dmod]

    grad_x, g_norm_g, dmod_h = _row_call(
        "h_bwd", h_bwd, nt, tm, [(xs, 0, dm, 0), (dh, 0, dm, 0), (dx_res, 0, dm, 0)], [norm_g, mod3],
        [(seq, dm, F32, nct)], [((1, dm), F32), ((8, 3 * dm), F32)])
    dmod = add("d_mod", [dmod_h, dmod_gate], (8, 3 * dm))
    g_ada_b = add("g_ada_b", [dmod[0:1], dmod[1:2]], (1, 3 * dm))
    g_ada_st = _mm_t_st("d_ada_w", sc, restack(dmod))
    d_sc = _mm_st_t("d_cond", restack(dmod), ada_st)

    def cond_bwd(i, r, f):
        _, vjp = jax.vjp(jax.nn.silu, r[0])
        return [vjp(r[1])[0]], []

    (d_cond,) = _row_call("cond_bwd", cond_bwd, 1, 8, [(cond, 0, dm, 0), (d_sc, 0, dm, 0)], [], [(8, dm, F32, 0)], [])

    grads = dict(
        c_ctx=d_cond[1], ada_w=g_ada_st, ada_b=g_ada_b, norm_g=g_norm_g, w_in=g_w_in_st, hg_lb=g_hg_lb,
        hg_norm_g=g_hg_norm, rw_mu=g_mu, rw_w0=g_w0, rw_w2=g_w2, rw_a0=g_a0, rw_a2=g_a2, rw_kk=g_kk, rw_ka=g_ka,
        rw_rk=g_rk, rw_gn_g=g_gn_g, rw_gn_b=g_gn_b, w_hg_out=g_w_hg_st, w_rw_out=g_w_rw_st, w_out=g_w_out,
        final_g=d_final_g.reshape(dm))
    return loss_acc[0:1, 0:1], grad_x, grads


def _my_place():
    return lax.axis_index("x"), lax.axis_index("y"), lax.axis_index("c")


MIN_CHUNK_BYTES = 1 << 18
ROW_ALIGN = 16


def _n_chunks(rows, row_bytes):
    for n in (8, 4, 2):
        if rows % (n * ROW_ALIGN) == 0 and rows // n * row_bytes >= MIN_CHUNK_BYTES:
            return n
    return 1


def _row_bytes(a, lead=1):
    n = a.dtype.itemsize
    for d in a.shape[lead:]:
        n *= d
    return n


def _rows(ref, start, size):
    return ref.at[pl.ds(start, size)]


def _chunked(make, start, size, n):
    cs = size // n
    return [make(start + j * cs, cs) for j in range(n)]


_PEER_CHIPS = 3


def _weights_gather(name, big, small):
    nb, na = len(big), len(big) + len(small)
    arrays = list(big) + list(small)
    n_ici = 6

    def body(*refs):
        outs = refs[na:2 * na]
        send_sems, recv_sems, fsend_sems, frecv_sems = refs[2 * na:]
        x, y, c = _my_place()
        me, sx, sy, sd = 2 * x + y, 2 * (1 - x) + y, 2 * x + (1 - y), 2 * (1 - x) + (1 - y)
        kx, ky, kd = (1 - x, y, c), (x, 1 - y, c), (1 - x, 1 - y, c)

        def ici(a, j, src_slot, dst_slot, to, r0, nr):
            return pltpu.make_async_remote_copy(
                src_ref=_rows(outs[a].at[src_slot], r0, nr), dst_ref=_rows(outs[a].at[dst_slot], r0, nr),
                send_sem=send_sems.at[a, j], recv_sem=recv_sems.at[a, j], device_id=to,
                device_id_type=pl.DeviceIdType.MESH)

        def to_sibling(a, k, slot, r0, nr):
            rows = _rows(outs[a].at[slot], r0, nr)
            return pltpu.make_async_remote_copy(
                src_ref=rows, dst_ref=rows, send_sem=fsend_sems.at[a, k], recv_sem=frecv_sems.at[a, k],
                device_id=(x, y, 1 - c), device_id_type=pl.DeviceIdType.MESH)

        def start(copies):
            for cp in copies:
                cp.start()

        geo = []
        for a in range(nb):
            half = arrays[a].shape[1] // 2
            geo.append((pl.multiple_of(c * half, ROW_ALIGN), pl.multiple_of((1 - c) * half, ROW_ALIGN), half // 2,
                        _n_chunks(half // 2, _row_bytes(arrays[a], 2))))
        plan = [(me, sx, kx, 0), (me, sx, kx, 1), (me, sy, ky, 0), (me, sy, ky, 1), (sx, sd, ky, 0), (sy, sd, kx, 1)]

        def piece(a, j):
            return geo[a][0] + plan[j][3] * geo[a][2]

        for a in range(nb):
            for j in range(4):
                start(_chunked(lambda r0, cs: ici(a, j, me, me, plan[j][2], r0, cs), piece(a, j), geo[a][2], geo[a][3]))
        for a in range(nb, na):
            rows = arrays[a].shape[1]
            for j, to in ((0, kx), (2, ky), (1, kd)):
                ici(a, j, me, me, to, 0, rows).start()
        for a in range(nb):
            for j, first in ((4, 0), (5, 3)):
                src_slot, _, to, _ = plan[j]
                ici(a, first, me, plan[first][1], plan[first][2], piece(a, first), geo[a][2]).wait_recv()
                start(_chunked(lambda r0, cs: ici(a, j, src_slot, src_slot, to, r0, cs), piece(a, j), geo[a][2], geo[a][3]))
        for a in range(nb):
            for j in (1, 2):
                ici(a, j, me, plan[j][1], plan[j][2], piece(a, j), geo[a][2]).wait_recv()
            for k, slot in ((0, sx), (1, sy)):
                start(_chunked(lambda r0, cs: to_sibling(a, k, slot, r0, cs), geo[a][0], 2 * geo[a][2], geo[a][3]))
        for a in range(nb):
            for j in (4, 5):
                ici(a, j, me, sd, plan[j][2], piece(a, j), geo[a][2]).wait_recv()
            start(_chunked(lambda r0, cs: to_sibling(a, 2, sd, r0, cs), geo[a][0], 2 * geo[a][2], geo[a][3]))
        for a in range(nb, na):
            rows = arrays[a].shape[1]
            for j, slot, to in ((0, sx, kx), (2, sy, ky), (1, sd, kd)):
                ici(a, j, me, slot, to, 0, rows).wait_recv()
        for a in range(nb):
            for k, slot in ((0, sx), (1, sy), (2, sd)):
                to_sibling(a, k, slot, geo[a][1], 2 * geo[a][2]).wait_recv()
        for a in range(nb):
            for j in range(n_ici):
                ici(a, j, me, me, plan[j][2], piece(a, j), geo[a][2]).wait_send()
            for k, slot in ((0, sx), (1, sy), (2, sd)):
                to_sibling(a, k, slot, geo[a][0], 2 * geo[a][2]).wait_send()
        for a in range(nb, na):
            rows = arrays[a].shape[1]
            for j, to in ((0, kx), (2, ky), (1, kd)):
                ici(a, j, me, me, to, 0, rows).wait_send()

    hbm = pl.BlockSpec(memory_space=pl.ANY)
    ici_sems = pltpu.SemaphoreType.DMA((na, n_ici))
    pair_sems = pltpu.SemaphoreType.DMA((na, _PEER_CHIPS))
    return pl.pallas_call(
        body, name=name, in_specs=[hbm] * na, out_specs=[hbm] * na,
        out_shape=[jax.ShapeDtypeStruct(a.shape, a.dtype) for a in arrays],
        input_output_aliases={a: a for a in range(na)}, scratch_shapes=[ici_sems, ici_sems, pair_sems, pair_sems],
    )(*arrays)


def _chip_scatter(name, arrays):
    na = len(arrays)

    def body(*refs):
        ins, outs = refs[:na], refs[na:2 * na]
        send_sems, recv_sems = refs[2 * na:]
        x, y, c = _my_place()
        me = 2 * x + y
        chips = [(1 - x, y), (x, 1 - y), (1 - x, 1 - y)]

        def remote(a, k, slot, r0, nr):
            px, py = chips[k]
            return pltpu.make_async_remote_copy(
                src_ref=_rows(ins[a].at[2 * px + py], r0, nr), dst_ref=_rows(outs[a].at[slot], r0, nr),
                send_sem=send_sems.at[a, k], recv_sem=recv_sems.at[a, k], device_id=(px, py, c),
                device_id_type=pl.DeviceIdType.MESH)

        for a in range(na):
            rows = arrays[a].shape[1]
            for k in range(_PEER_CHIPS):
                for cp in _chunked(lambda r0, cs: remote(a, k, me, r0, cs), 0, rows, _n_chunks(rows, _row_bytes(arrays[a], 2))):
                    cp.start()
        for k in range(_PEER_CHIPS):
            px, py = chips[k]
            for a in range(na):
                remote(a, k, 2 * px + py, 0, arrays[a].shape[1]).wait_recv()
        for a in range(na):
            for k in range(_PEER_CHIPS):
                remote(a, k, me, 0, arrays[a].shape[1]).wait_send()

    hbm = pl.BlockSpec(memory_space=pl.ANY)
    sems = pltpu.SemaphoreType.DMA((na, _PEER_CHIPS))
    return pl.pallas_call(
        body, name=name, in_specs=[hbm] * na, out_specs=[hbm] * na,
        out_shape=[jax.ShapeDtypeStruct(a.shape, a.dtype) for a in arrays], scratch_shapes=[sems, sems],
    )(*arrays)


PAIR_TILE_BYTES = 2 << 20


def _pair_exchange(name, a, place, reduce, out_dtype):
    rows, cols = a.shape[-2], a.shape[-1]
    half = rows // 2 if reduce else rows
    tr = _row_tile_for(half, cols, budget=PAIR_TILE_BYTES)
    nh = half // tr
    n_steps = (N_SHARD if reduce else 1) * nh

    def body(pc_ref, *refs):
        if reduce:
            keep_ref, send_ref, o_ref, land, send_sems, recv_sems, credit = refs
        else:
            send_ref, o_ref, land, send_sems, recv_sems, credit = refs
        x, y, c = _my_place()
        other = (x, y, 1 - c)
        t = pl.program_id(0) * nh + pl.program_id(1) if reduce else pl.program_id(0)
        slot = t % 2

        @pl.when(t >= 2)
        def _():
            pl.semaphore_wait(credit, 1)

        copy = pltpu.make_async_remote_copy(
            src_ref=send_ref, dst_ref=land.at[pl.ds(slot, 1)] if reduce else land.at[slot], send_sem=send_sems.at[slot],
            recv_sem=recv_sems.at[slot], device_id=other, device_id_type=pl.DeviceIdType.MESH)
        copy.start()
        copy.wait_recv()
        got = land[slot]
        o_ref[...] = ((keep_ref[...] + got) if reduce else got).astype(out_dtype)
        copy.wait_send()

        @pl.when(t < n_steps - 2)
        def _():
            pl.semaphore_signal(credit, inc=1, device_id=other, device_id_type=pl.DeviceIdType.MESH)

    if reduce:
        grid = (N_SHARD, nh)
        in_specs = [pl.BlockSpec((None, tr, cols), lambda j, i, pc: (j, pc[0] * nh + i, 0)),
                    pl.BlockSpec((1, tr, cols), lambda j, i, pc: (j, (1 - pc[0]) * nh + i, 0))]
        out_spec = pl.BlockSpec((None, tr, cols), lambda j, i, pc: (j, i, 0))
        out_shape = jax.ShapeDtypeStruct((N_SHARD, half, cols), out_dtype)
        operands = (a, a)
        sem = ("arbitrary", "arbitrary")
    else:
        grid = (nh,)
        in_specs = [pl.BlockSpec((tr, cols), lambda i, pc: (i, 0))]
        out_spec = pl.BlockSpec((tr, cols), lambda i, pc: (i, 0))
        out_shape = jax.ShapeDtypeStruct((half, cols), out_dtype)
        operands = (a,)
        sem = ("arbitrary",)
    return pl.pallas_call(
        body, name=name,
        grid_spec=pltpu.PrefetchScalarGridSpec(
            num_scalar_prefetch=1, grid=grid, in_specs=in_specs, out_specs=out_spec,
            scratch_shapes=[pltpu.VMEM((2, tr, cols), a.dtype), pltpu.SemaphoreType.DMA((2,)),
                            pltpu.SemaphoreType.DMA((2,)), pltpu.SemaphoreType.REGULAR]),
        out_shape=out_shape, compiler_params=_params(sem),
    )(place, *operands)


def _cast_into_slot(name, a, chip):
    rows, cols = a.shape
    tm = _row_tile_for(rows, cols)

    def body(pc_ref, a_ref, o_ref):
        o_ref[...] = a_ref[...].astype(BF16)

    return pl.pallas_call(
        body, name=name,
        grid_spec=pltpu.PrefetchScalarGridSpec(
            num_scalar_prefetch=1, grid=(rows // tm,), in_specs=[pl.BlockSpec((tm, cols), lambda i, pc: (i, 0))],
            out_specs=pl.BlockSpec((None, tm, cols), lambda i, pc: (pc[0], i, 0))),
        out_shape=jax.ShapeDtypeStruct((N_SHARD, rows, cols), BF16), compiler_params=_params(("parallel",)),
    )(chip, a)


def _sum_landed(name, landed, sent, chip):
    ns, rows, cols = landed.shape
    tm = _row_tile_for(rows, cols)

    def body(pc_ref, *refs):
        own_ref, o_ref = refs[ns], refs[ns + 1]
        me = pc_ref[0]
        terms = [jnp.where(me == j, own_ref[...], refs[j][...]).astype(F32) for j in range(ns)]
        o_ref[...] = _slot_sum(terms)

    def landed_spec(j):
        return pl.BlockSpec((None, tm, cols), lambda i, pc: (jnp.where(pc[0] == j, (j + 1) % ns, j), i, 0))

    return pl.pallas_call(
        body, name=name,
        grid_spec=pltpu.PrefetchScalarGridSpec(
            num_scalar_prefetch=1, grid=(rows // tm,),
            in_specs=[landed_spec(j) for j in range(ns)] + [pl.BlockSpec((None, tm, cols), lambda i, pc: (pc[0], i, 0))],
            out_specs=pl.BlockSpec((tm, cols), lambda i, pc: (i, 0))),
        out_shape=jax.ShapeDtypeStruct((rows, cols), F32), compiler_params=_params(("parallel",)),
    )(chip, *([landed] * ns), sent)


def _gather_all(name, a):
    def body(in_ref, out_ref, send_sems, recv_sems, local_sem):
        x, y, c = _my_place()
        me = 4 * x + 2 * y + c

        def peer(k):
            return (x ^ (k >> 2), y ^ ((k >> 1) & 1), c ^ (k & 1))

        def remote(k, land):
            return pltpu.make_async_remote_copy(
                src_ref=in_ref, dst_ref=out_ref.at[land], send_sem=send_sems.at[k - 1], recv_sem=recv_sems.at[k - 1],
                device_id=peer(k), device_id_type=pl.DeviceIdType.MESH)

        local = pltpu.make_async_copy(in_ref, out_ref.at[me], local_sem)
        local.start()
        for k in range(1, N_DEV):
            remote(k, me).start()
        for k in range(1, N_DEV):
            px, py, pc = peer(k)
            remote(k, 4 * px + 2 * py + pc).wait_recv()
        for k in range(1, N_DEV):
            remote(k, me).wait_send()
        local.wait()

    hbm = pl.BlockSpec(memory_space=pl.ANY)
    return pl.pallas_call(
        body, name=name, in_specs=[hbm], out_specs=hbm,
        out_shape=jax.ShapeDtypeStruct((N_DEV,) + a.shape, a.dtype),
        scratch_shapes=[pltpu.SemaphoreType.DMA((N_DEV - 1,)), pltpu.SemaphoreType.DMA((N_DEV - 1,)), pltpu.SemaphoreType.DMA],
    )(a)


def _row_tile_for(rows, cols, budget=1 << 20):
    for tm in (1024, 512, 256, 128, 64, 32, 16, 8):
        if rows % tm == 0 and tm * cols * 4 <= budget:
            return tm
    return rows


def _slot_sum(vals):
    g = vals[0]
    for v in vals[1:]:
        g = g + v
    return g


def _rowwise(name, fn, arrays, out_dtype):
    rows, cols = arrays[0].shape
    tm = _row_tile_for(rows, cols)

    def body(*refs):
        refs[-1][...] = fn(*[r[...] for r in refs[:-1]]).astype(out_dtype)

    blk = pl.BlockSpec((tm, cols), lambda i: (i, 0))
    return pl.pallas_call(
        body, name=name, grid=(rows // tm,), in_specs=[blk] * len(arrays), out_specs=blk,
        out_shape=jax.ShapeDtypeStruct((rows, cols), out_dtype), compiler_params=_params(("parallel",)),
    )(*arrays)


def _sum_slots(name, st):
    ns, rows, cols = st.shape
    tm = _row_tile_for(rows, cols)

    def body(s_ref, o_ref):
        o_ref[...] = _slot_sum([s_ref[j].astype(F32) for j in range(ns)])

    return pl.pallas_call(
        body, name=name, grid=(rows // tm,),
        in_specs=[pl.BlockSpec((ns, tm, cols), lambda i: (0, i, 0))],
        out_specs=pl.BlockSpec((tm, cols), lambda i: (i, 0)),
        out_shape=jax.ShapeDtypeStruct((rows, cols), F32),
        compiler_params=_params(("parallel",)),
    )(st)


ADAM_TILE_BYTES = 1 << 19


def _adam_update(g, p_ref, m_ref, v_ref, go_ref, d_ref, mo_ref, vo_ref):
    mn = ADAM_B1 * m_ref[...] + (1.0 - ADAM_B1) * g
    vn = ADAM_B2 * v_ref[...] + (1.0 - ADAM_B2) * jnp.square(g)
    m_hat = mn / (1.0 - ADAM_B1 ** ADAM_STEP)
    v_hat = vn / (1.0 - ADAM_B2 ** ADAM_STEP)
    go_ref[...] = g
    d_ref[...] = -ADAM_LR * (m_hat / (jnp.sqrt(v_hat) + ADAM_EPS) + ADAM_WD * p_ref[...])
    mo_ref[...] = mn
    vo_ref[...] = vn


def _adamw(name, p, m, v, gst):
    rows, cols = p.shape
    ns = gst.shape[0]
    tm = _row_tile_for(rows, cols, budget=ADAM_TILE_BYTES)

    def body(p_ref, m_ref, v_ref, g_ref, *outs):
        _adam_update(_slot_sum([g_ref[j] for j in range(ns)]), p_ref, m_ref, v_ref, *outs)

    blk = pl.BlockSpec((tm, cols), lambda i: (i, 0))
    return pl.pallas_call(
        body, name=name, grid=(rows // tm,),
        in_specs=[blk, blk, blk, pl.BlockSpec((ns, tm, cols), lambda i: (0, i, 0))],
        out_specs=[blk] * 4, out_shape=[jax.ShapeDtypeStruct((rows, cols), F32)] * 4,
        compiler_params=_params(("parallel",)),
    )(p, m, v, gst)


def _adamw_halves(name, p, m, v, mine, theirs, place):
    rows, cols = p.shape
    half = rows // 2
    tm = _row_tile_for(half, cols, budget=ADAM_TILE_BYTES)
    nh = half // tm

    def body(pc_ref, p_ref, m_ref, v_ref, mine_ref, theirs_ref, *outs):
        g = jnp.where(pl.program_id(0) == pc_ref[0], mine_ref[...], theirs_ref[...])
        _adam_update(g, p_ref, m_ref, v_ref, *outs)

    blk = pl.BlockSpec((tm, cols), lambda h, i, pc: (h * nh + i, 0))
    hblk = pl.BlockSpec((tm, cols), lambda h, i, pc: (i, 0))
    return pl.pallas_call(
        body, name=name,
        grid_spec=pltpu.PrefetchScalarGridSpec(
            num_scalar_prefetch=1, grid=(2, nh), in_specs=[blk, blk, blk, hblk, hblk], out_specs=[blk] * 4),
        out_shape=[jax.ShapeDtypeStruct((rows, cols), F32)] * 4, compiler_params=_params(("parallel", "parallel")),
    )(place, p, m, v, mine, theirs)


def _pack(parts, width=LANE, mult=8):
    flat = jnp.concatenate([a.reshape(-1) for a in parts])
    n = flat.shape[0]
    per = width * mult
    total = -(-n // per) * per
    return jnp.pad(flat, (0, total - n)).reshape(total // width, width)


def _unpack(packed, shapes):
    flat = packed.reshape(-1)
    out, off = [], 0
    for s in shapes:
        n = 1
        for d in s:
            n *= d
        out.append(flat[off:off + n].reshape(s))
        off += n
    return out


_SMALL_SHARDED = ("hg_lb", "rw_mu", "rw_w0", "rw_w2", "rw_a0", "rw_a2")
_REPLICATED = ("c_ctx", "ada_b", "norm_g", "hg_norm_g", "rw_kk", "rw_ka", "rw_rk", "rw_gn_g", "rw_gn_b", "final_g")
_BIG = ("ada_w", "w_in", "w_hg_out", "w_rw_out", "w_out")
_WEIGHTS = ("c_ctx", "ada_w", "ada_b", "norm_g", "w_in", "hg_lb", "hg_norm_g", "rw_mu", "rw_w0", "rw_w2", "rw_a0", "rw_a2",
            "rw_kk", "rw_ka", "rw_rk", "rw_gn_g", "rw_gn_b", "w_hg_out", "w_rw_out", "w_out", "final_g")


def _join_shards(st):
    a = jnp.moveaxis(st, 0, -2)
    return a.reshape(a.shape[:-2] + (a.shape[-2] * a.shape[-1],))


def _split_shards(a):
    s = a.reshape(a.shape[:-1] + (N_SHARD, a.shape[-1] // N_SHARD))
    return jnp.moveaxis(s, -2, 0)


def kernel(x, c, ctx, c_ctx, ada_w, ada_b, norm_g, w_in, hg_lb, hg_norm_g, rw_mu, rw_w0, rw_w2, rw_a0, rw_a2, rw_kk, rw_ka, rw_rk, rw_gn_g, rw_gn_b, w_hg_out, w_rw_out, w_out, final_g, loss_target, m_c_ctx, m_ada_w, m_ada_b, m_norm_g, m_w_in, m_hg_lb, m_hg_norm_g, m_rw_mu, m_rw_w0, m_rw_w2, m_rw_a0, m_rw_a2, m_rw_kk, m_rw_ka, m_rw_rk, m_rw_gn_g, m_rw_gn_b, m_w_hg_out, m_w_rw_out, m_w_out, m_final_g, v_c_ctx, v_ada_w, v_ada_b, v_norm_g, v_w_in, v_hg_lb, v_hg_norm_g, v_rw_mu, v_rw_w0, v_rw_w2, v_rw_a0, v_rw_a2, v_rw_kk, v_rw_ka, v_rw_rk, v_rw_gn_g, v_rw_gn_b, v_w_hg_out, v_w_rw_out, v_w_out, v_final_g):
    w = dict(c_ctx=c_ctx, ada_w=ada_w, ada_b=ada_b, norm_g=norm_g, w_in=w_in, hg_lb=hg_lb, hg_norm_g=hg_norm_g, rw_mu=rw_mu,
             rw_w0=rw_w0, rw_w2=rw_w2, rw_a0=rw_a0, rw_a2=rw_a2, rw_kk=rw_kk, rw_ka=rw_ka, rw_rk=rw_rk, rw_gn_g=rw_gn_g,
             rw_gn_b=rw_gn_b, w_hg_out=w_hg_out, w_rw_out=w_rw_out, w_out=w_out, final_g=final_g)
    m = dict(c_ctx=m_c_ctx, ada_w=m_ada_w, ada_b=m_ada_b, norm_g=m_norm_g, w_in=m_w_in, hg_lb=m_hg_lb, hg_norm_g=m_hg_norm_g,
             rw_mu=m_rw_mu, rw_w0=m_rw_w0, rw_w2=m_rw_w2, rw_a0=m_rw_a0, rw_a2=m_rw_a2, rw_kk=m_rw_kk, rw_ka=m_rw_ka,
             rw_rk=m_rw_rk, rw_gn_g=m_rw_gn_g, rw_gn_b=m_rw_gn_b, w_hg_out=m_w_hg_out, w_rw_out=m_w_rw_out, w_out=m_w_out,
             final_g=m_final_g)
    v = dict(c_ctx=v_c_ctx, ada_w=v_ada_w, ada_b=v_ada_b, norm_g=v_norm_g, w_in=v_w_in, hg_lb=v_hg_lb, hg_norm_g=v_hg_norm_g,
             rw_mu=v_rw_mu, rw_w0=v_rw_w0, rw_w2=v_rw_w2, rw_a0=v_rw_a0, rw_a2=v_rw_a2, rw_kk=v_rw_kk, rw_ka=v_rw_ka,
             rw_rk=v_rw_rk, rw_gn_g=v_rw_gn_g, rw_gn_b=v_rw_gn_b, w_hg_out=v_w_hg_out, w_rw_out=v_w_rw_out, w_out=v_w_out,
             final_g=v_final_g)

    def mat(a):
        return a.reshape(a.shape[-2], a.shape[-1])

    def pack_small(d):
        return _pack([d[n] for n in _SMALL_SHARDED], mult=2 * ROW_ALIGN)

    my_core = lax.axis_index("c").astype(jnp.int32).reshape(1)
    my_chip = (2 * lax.axis_index("x") + lax.axis_index("y")).astype(jnp.int32).reshape(1)

    small_shapes = [w[n].shape for n in _SMALL_SHARDED]
    big_bf = [_cast_into_slot(f"to_bf16_{n}", mat(w[n]), my_chip) for n in _BIG]
    small_mine = pack_small(w)
    small_slots = lax.dynamic_update_slice(jnp.zeros((N_SHARD,) + small_mine.shape, F32), small_mine[None], (my_chip[0], 0, 0))
    gathered = _weights_gather("weights_gather", big_bf, [small_slots])
    ada_st, w_in_st, w_hg_st, w_rw_st, w_out_st, small_st = gathered
    full_small = {}
    per_chip = [_unpack(small_st[j], small_shapes) for j in range(N_SHARD)]
    for i, n in enumerate(_SMALL_SHARDED):
        full_small[n] = _join_shards(jnp.stack([per_chip[j][i] for j in range(N_SHARD)], axis=0))
    dm = x.shape[-1]
    w_out_full = w_out_st.reshape(dm, dm)

    loss_b, grad_x, g = _local_step(
        x[0], c, ctx[0], c_ctx, ada_st, ada_b, norm_g, w_in_st, full_small["hg_lb"], hg_norm_g, full_small["rw_mu"][0],
        full_small["rw_w0"][0], full_small["rw_w2"][0], full_small["rw_a0"][0], full_small["rw_a2"][0], rw_kk, rw_ka, rw_rk,
        rw_gn_g, rw_gn_b, w_hg_st, w_rw_st, w_out_full, final_g, loss_target[0])
    loss = lax.psum(loss_b[0, 0], ("x", "y", "c"))

    g_small = {"hg_lb": g["hg_lb"], "rw_mu": g["rw_mu"][None], "rw_w0": g["rw_w0"][None], "rw_w2": g["rw_w2"][None],
               "rw_a0": g["rw_a0"][None], "rw_a2": g["rw_a2"][None]}
    split = {n: _split_shards(g_small[n]) for n in _SMALL_SHARDED}
    small_parts = jnp.stack([pack_small({n: split[n][j] for n in _SMALL_SHARDED}) for j in range(N_SHARD)], axis=0)
    partial = [g["ada_w"], g["w_in"], g["w_hg_out"], g["w_rw_out"], g["w_out"].reshape(N_SHARD, dm // N_SHARD, dm), small_parts]
    chip_sums = [_pair_exchange(f"grads_pair_sum{i}", a, my_core, True, BF16) for i, a in enumerate(partial)]
    landed = _chip_scatter("grads_scatter", chip_sums)
    mine = [_sum_landed(f"grads_sum{i}", a, s, my_chip) for i, (a, s) in enumerate(zip(landed, chip_sums))]
    theirs = [_pair_exchange(f"grads_pair_swap{i}", a, my_core, False, F32) for i, a in enumerate(mine)]
    rep_shapes = [w[n].shape for n in _REPLICATED]
    rep_all = _gather_all("grads_replicated", _pack([g[n].reshape(w[n].shape) for n in _REPLICATED]))

    res = {}
    for i, n in enumerate(_BIG):
        outs = _adamw_halves(f"adamw_{n}", mat(w[n]), mat(m[n]), mat(v[n]), mine[i], theirs[i], my_core)
        res[n] = [o.reshape(w[n].shape) for o in outs]
    outs = _adamw_halves("adamw_small", small_mine, pack_small(m), pack_small(v), mine[len(_BIG)], theirs[len(_BIG)], my_core)
    for i, vals in enumerate(zip(*[_unpack(o, small_shapes) for o in outs])):
        res[_SMALL_SHARDED[i]] = list(vals)
    outs = _adamw("adamw_replicated", _pack([w[n] for n in _REPLICATED]), _pack([m[n] for n in _REPLICATED]),
                  _pack([v[n] for n in _REPLICATED]), rep_all)
    for i, vals in enumerate(zip(*[_unpack(o, rep_shapes) for o in outs])):
        res[_REPLICATED[i]] = list(vals)

    return (loss, grad_x[None], *[res[n][0] for n in _WEIGHTS], *[res[n][1] for n in _WEIGHTS],
            *[res[n][2] for n in _WEIGHTS], *[res[n][3] for n in _WEIGHTS])
```

```python
import functools

import jax
import jax.numpy as jnp
from jax import lax
from jax.experimental import pallas as pl
from jax.experimental.pallas import tpu as pltpu

HI = lax.Precision.HIGHEST
F32 = jnp.float32
BF16 = jnp.bfloat16

NORM_EPS = 1e-6
HG_HEAD = 128
RW_HEAD = 64
RW_LORA = 64
RW_GN_EPS = 64e-5
GRID_W = 64
SUB = 16
STEP = 64
N_SHARD = 4
N_DEV = 8
LANE = 128

ADAM_LR = 0.001
ADAM_B1 = 0.9
ADAM_B2 = 0.999
ADAM_EPS = 1e-08
ADAM_WD = 0.01
ADAM_STEP = 10

VMEM_LIMIT = 56 * 1024 * 1024


def _params(sem=None):
    return pltpu.CompilerParams(dimension_semantics=sem, vmem_limit_bytes=VMEM_LIMIT)


def _tile(n, cands):
    for c in cands:
        if n % c == 0:
            return c
    return n


def _iota2(n, m, d):
    return lax.broadcasted_iota(jnp.int32, (n, m), d)


def _before(n, rev, strict):
    t, s = _iota2(n, n, 0), _iota2(n, n, 1)
    if rev:
        return (s > t) if strict else (s >= t)
    return (s < t) if strict else (s <= t)


def _bdot(a, b, spec):
    return jnp.einsum(spec, a, b, precision=HI, preferred_element_type=F32)


def _sdot(a, b, spec):
    return jnp.einsum(spec, a, b, precision=lax.Precision.DEFAULT, preferred_element_type=F32)


def _hg_step(s0, qraw, iin, fin, lb2, rev):
    c, w = qraw.shape
    h = w // HG_HEAD
    nsub = c // SUB
    lb = jax.nn.sigmoid(lb2[0:1] - lb2[1:2])
    q = jax.nn.silu(qraw)
    fg = lb + (1.0 - lb) * jax.nn.sigmoid(fin)
    kk = 1.0 - fg
    g = jnp.log(fg)
    bcum = jnp.dot(_before(c, rev, False).astype(F32), g, precision=HI, preferred_element_type=F32)
    rows = lax.broadcasted_iota(jnp.int32, (SUB, 1, 1), 0)

    def heads(a):
        return jnp.swapaxes(a.reshape(a.shape[0], h, HG_HEAD), 0, 1)

    def unheads(a):
        return jnp.swapaxes(a, 0, 1).reshape(a.shape[1], w)

    blocks = [slice(j * SUB, (j + 1) * SUB) for j in range(nsub)]
    outs = []
    for sl in blocks:
        qs, ks, vs, bc = [a[sl].reshape(SUB, h, HG_HEAD) for a in (q, kk, iin, bcum)]
        o = jnp.zeros((SUB, h, HG_HEAD), F32)
        for si in range(SUB):
            dec = jnp.exp(jnp.minimum(bc - bc[si:si + 1], 0.0))
            a = jnp.sum(qs * ks[si:si + 1] * dec, axis=-1, keepdims=True)
            valid = (rows <= si) if rev else (rows >= si)
            o = o + jnp.where(valid, a, 0.0) * vs[si:si + 1]
        outs.append(o.reshape(SUB, w))
    order = list(range(nsub - 1, -1, -1)) if rev else list(range(nsub))
    for pos in range(1, nsub):
        j, before = order[pos], order[:pos]
        first = (j + 1) * SUB - 1 if rev else j * SUB
        bstart = bcum[first:first + 1] - g[first:first + 1]
        qp = heads(q[blocks[j]] * jnp.exp(bcum[blocks[j]] - bstart))
        kp = heads(jnp.concatenate([kk[blocks[p]] * jnp.exp(bstart - bcum[blocks[p]]) for p in before], axis=0))
        vp = heads(jnp.concatenate([iin[blocks[p]] for p in before], axis=0))
        outs[j] = outs[j] + unheads(_sdot(_sdot(qp, kp, 'htk,hsk->hts'), vp, 'hts,hsv->htv'))
    o_state = unheads(_sdot(heads(q * jnp.exp(bcum)), s0, 'htk,hvk->htv'))
    last = 0 if rev else c - 1
    blast = bcum[last:last + 1]
    s_new = heads(jnp.exp(blast)) * s0 + _sdot(heads(iin), heads(kk * jnp.exp(blast - bcum)), 'hsv,hsk->hvk')
    return jnp.concatenate(outs, axis=0) + o_state, s_new


def _tri_solve(lmat, rhs, rev):
    hh, c, _ = lmat.shape
    nb = c // SUB
    diag = jnp.concatenate([lmat[:, i * SUB:(i + 1) * SUB, i * SUB:(i + 1) * SUB] for i in range(nb)], axis=0)
    dt = jnp.transpose(diag, (1, 2, 0))
    col = lax.broadcasted_iota(jnp.int32, (SUB, 1), 0)
    inv_rows = [None] * SUB
    order = list(range(SUB - 1, -1, -1)) if rev else list(range(SUB))
    for pos, t in enumerate(order):
        row = jnp.broadcast_to((col == t).astype(F32), (SUB, dt.shape[2]))
        for s in order[:pos]:
            row = row - dt[t, s:s + 1, :] * inv_rows[s]
        inv_rows[t] = row
    tinv = jnp.transpose(jnp.concatenate([r[None] for r in inv_rows], axis=0), (2, 0, 1))
    p = [None] * nb
    done = []
    for i in (range(nb - 1, -1, -1) if rev else range(nb)):
        r = rhs[:, i * SUB:(i + 1) * SUB]
        if done:
            lrow = jnp.concatenate([lmat[:, i * SUB:(i + 1) * SUB, m * SUB:(m + 1) * SUB] for m in done], axis=2)
            r = r - _sdot(lrow, jnp.concatenate([p[m] for m in done], axis=1), 'hts,hsv->htv')
        p[i] = _sdot(tinv[i * hh:(i + 1) * hh], r, 'hts,hsv->htv')
        done.append(i)
    return jnp.concatenate(p, axis=1)


def _rw_step(s0, r, k, v, wlo, alo, w0h, w2h, a0h, a2h, kkh, kah, rev):
    hh, c, _ = r.shape
    tl = jnp.broadcast_to(jnp.tanh(wlo)[None], (hh, c, wlo.shape[1]))
    al = jnp.broadcast_to(alo[None], (hh, c, alo.shape[1]))
    wlog = -jax.nn.softplus(-(w0h + _sdot(tl, w2h, 'hcl,hlj->hcj'))) - 0.5
    lw = -jnp.exp(wlog)
    a = jax.nn.sigmoid(a0h + _sdot(al, a2h, 'hcl,hlj->hcj'))
    kk = k * kkh
    kk = kk * lax.rsqrt(jnp.sum(kk * kk, axis=-1, keepdims=True) + 1e-12)
    kd = k * (1.0 + (a - 1.0) * kah)
    b = kk * a
    incl = jnp.broadcast_to(_before(c, rev, False).astype(F32)[None], (hh, c, c))
    cum = _bdot(incl, lw, 'hts,hsk->htk')
    ecum, encum = jnp.exp(cum), jnp.exp(-cum)
    alpha = jnp.exp(cum - lw) * kk
    beta = b * encum
    kappa = kd * encum
    rho = r * ecum
    m_lt = _before(c, rev, True)[None]
    m_le = _before(c, rev, False)[None]
    ar = jnp.concatenate([alpha, rho], axis=1)
    kb = jnp.concatenate([kappa, beta], axis=1)
    gram = _sdot(ar, kb, 'htk,hsk->hts')
    a_kap = jnp.where(m_lt, gram[:, :c, :c], 0.0)
    a_bet = jnp.where(m_lt, gram[:, :c, c:], 0.0)
    b_kap = jnp.where(m_le, gram[:, c:, :c], 0.0)
    b_bet = jnp.where(m_le, gram[:, c:, c:], 0.0)
    from_state = _sdot(ar, s0, 'htk,hvk->htv')
    p = _tri_solve(a_bet, from_state[:, :c] + _sdot(a_kap, v, 'hts,hsv->htv'), rev)
    vp = jnp.concatenate([v, -p], axis=1)
    y = from_state[:, c:] + _sdot(jnp.concatenate([b_kap, b_bet], axis=2), vp, 'hts,hsv->htv')
    stil = s0 + _sdot(vp, kb, 'hsv,hsk->hvk')
    last = 0 if rev else c - 1
    return y, stil * ecum[:, last:last + 1, :]


def _fn_h(s, norm_g, scale, shift):
    return s * lax.rsqrt(jnp.mean(s * s, axis=-1, keepdims=True) + NORM_EPS) * norm_g * (1.0 + scale) + shift


def _fn_hgpost(of, ob, z, g):
    tm, w = of.shape
    o = (of + ob).reshape(tm, w // HG_HEAD, HG_HEAD)
    o = o * lax.rsqrt(jnp.mean(o * o, axis=-1, keepdims=True) + NORM_EPS)
    return o.reshape(tm, w) * g * jax.nn.silu(z)


def _fn_rwpost(y0, y1, r, k, v, alo, z, a0, a2, k_a, r_k, gn_g, gn_b):
    tm, w = r.shape
    nh = w // RW_HEAD
    asum = 0.0
    for d in range(2):
        asum = asum + jax.nn.sigmoid(a0[d:d + 1] + jnp.dot(alo[:, d * RW_LORA:(d + 1) * RW_LORA], a2[d],
                                                           precision=HI, preferred_element_type=F32))
    k_sum = k * (2.0 + (asum - 2.0) * k_a)
    ys = (y0 + y1).reshape(tm, nh, RW_HEAD)
    mean = jnp.mean(ys, axis=-1, keepdims=True)
    var = jnp.mean(jnp.square(ys - mean), axis=-1, keepdims=True)
    y = ((ys - mean) * lax.rsqrt(var + RW_GN_EPS)).reshape(tm, w) * gn_g + gn_b
    bonus = jnp.sum((r * k_sum * r_k).reshape(tm, nh, RW_HEAD), axis=-1, keepdims=True) * v.reshape(tm, nh, RW_HEAD)
    return (y + bonus.reshape(tm, w)) * jax.nn.silu(z)


def _fn_merge(a, b, ghg, grw):
    return jax.nn.sigmoid(ghg) * a + jax.nn.sigmoid(grw) * b


def _fn_final(xs, o, gate, final_g, tgt):
    x2 = xs + gate * o
    y = x2 * lax.rsqrt(jnp.mean(x2 * x2, axis=-1, keepdims=True) + NORM_EPS) * final_g
    return 0.5 * jnp.sum(jnp.mean(jnp.square(y - tgt), axis=-1))


def _row_call(name, fn, n_tiles, tm, row_ins, full_ins, row_outs, acc_outs):
    n_ri, n_fi, n_ro = len(row_ins), len(full_ins), len(row_outs)

    def body(*refs):
        i = pl.program_id(0)
        rvals = [r[...] for r in refs[:n_ri]]
        fvals = [r[...] for r in refs[n_ri:n_ri + n_fi]]
        outs = refs[n_ri + n_fi:]
        ro, ao = fn(i, rvals, fvals)
        for ref, val in zip(outs[:n_ro], ro):
            ref[...] = val.astype(ref.dtype)
        for ref, val in zip(outs[n_ro:], ao):
            @pl.when(i == 0)
            def _(ref=ref):
                ref[...] = jnp.zeros_like(ref)
            ref[...] += val.astype(ref.dtype)

    def rspec(width, cb, off):
        return pl.BlockSpec((tm, width), lambda i: (jnp.maximum(i - off, 0), cb))

    def fspec(shape):
        nd = len(shape)
        return pl.BlockSpec(shape, lambda i: (0,) * nd)

    in_specs = [rspec(w, cb, off) for (_, cb, w, off) in row_ins] + [fspec(a.shape) for a in full_ins]
    out_specs = [rspec(w, 0, off) for (_, w, _, off) in row_outs] + [fspec(s) for (s, _) in acc_outs]
    out_shape = [jax.ShapeDtypeStruct((rows, w), dt) for (rows, w, dt, _) in row_outs] + \
                [jax.ShapeDtypeStruct(s, dt) for (s, dt) in acc_outs]
    res = pl.pallas_call(
        body, name=name, grid=(n_tiles,), in_specs=in_specs, out_specs=out_specs, out_shape=out_shape,
        compiler_params=_params(("arbitrary",)),
    )(*[a for (a, _, _, _) in row_ins], *full_ins)
    return list(res)


def _mm(name, a, b, m, n, k_steps, tm, tn, a_block, a_map, b_block, b_map, o_shape, o_block, o_map,
        contract, out_dtype=F32, scatter=()):
    ns = len(scatter)
    grid = (m // tm, n // tn, k_steps)

    def body(*refs):
        a_ref, b_ref, o_ref, acc_ref = refs[0], refs[1], refs[2 + ns], refs[3 + 2 * ns]
        kk = pl.program_id(2)
        if ns:
            sc_refs = (refs[2:2 + ns], refs[3 + ns:3 + 2 * ns]) + tuple(refs[4 + 2 * ns:])
            at = (pl.program_id(0) * grid[1] + pl.program_id(1)) * grid[2] + kk
            pl.when(at == 0)(lambda: _scatter_start(scatter, *sc_refs))

        @pl.when(kk == 0)
        def _():
            acc_ref[...] = jnp.zeros_like(acc_ref)

        acc_ref[...] += lax.dot_general(a_ref[...].astype(BF16), b_ref[...].astype(BF16),
                                        (contract, ((), ())), preferred_element_type=F32)

        @pl.when(kk == k_steps - 1)
        def _():
            o_ref[...] = acc_ref[...].astype(o_ref.dtype)

        if ns:
            pl.when(at == grid[0] * grid[1] * grid[2] - 1)(lambda: _scatter_wait(scatter, *sc_refs))

    hbm = pl.BlockSpec(memory_space=pl.ANY)
    sems = [pltpu.SemaphoreType.DMA((ns, _PEER_CHIPS))] * 2 if ns else []
    res = pl.pallas_call(
        body, name=name, grid=grid,
        in_specs=[pl.BlockSpec(a_block, a_map), pl.BlockSpec(b_block, b_map)] + [hbm] * ns,
        out_specs=[pl.BlockSpec(o_block, o_map)] + [hbm] * ns,
        out_shape=[jax.ShapeDtypeStruct(o_shape, out_dtype)] + [jax.ShapeDtypeStruct(s.shape, s.dtype) for s in scatter],
        scratch_shapes=[pltpu.VMEM((tm, tn), F32)] + sems,
        compiler_params=_params(("arbitrary",) * 3 if ns else ("parallel", "parallel", "arbitrary")),
    )(a, b, *scatter)
    return (res[0], list(res[1:])) if ns else res[0]


_TM = (768, 512, 256, 128, 64, 32, 16, 8)
_TN = (512, 256, 128)
_TK = (1024, 768, 512, 256, 128)
_TK_WIDE = (768, 512, 256, 128)
WIDE_OUT_BYTES = 32 << 20


def _tm_wide(m, ns):
    for tm in _TM:
        if m % tm == 0 and 3 * 4 * tm * ns <= WIDE_OUT_BYTES:
            return tm
    return m


def _mm_nn(name, a, b, out_dtype=F32):
    m, k = a.shape
    n = b.shape[1]
    tm, tn, tk = _tile(m, _TM), _tile(n, _TN), _tile(k, _TK)
    return _mm(name, a, b, m, n, k // tk, tm, tn, (tm, tk), lambda i, j, s: (i, s), (tk, tn), lambda i, j, s: (s, j),
               (m, n), (tm, tn), lambda i, j, s: (i, j), ((1,), (0,)), out_dtype)


def _mm_nt(name, a, b, out_dtype=F32):
    m, k = a.shape
    n = b.shape[0]
    tm, tn, tk = _tile(m, _TM), _tile(n, _TN), _tile(k, _TK)
    return _mm(name, a, b, m, n, k // tk, tm, tn, (tm, tk), lambda i, j, s: (i, s), (tn, tk), lambda i, j, s: (j, s),
               (m, n), (tm, tn), lambda i, j, s: (i, j), ((1,), (1,)), out_dtype)


def _mm_tn(name, a, b, out_dtype=F32):
    k, m = a.shape
    n = b.shape[1]
    tm, tn, tk = _tile(m, _TM), _tile(n, _TN), _tile(k, _TK)
    return _mm(name, a, b, m, n, k // tk, tm, tn, (tk, tm), lambda i, j, s: (s, i), (tk, tn), lambda i, j, s: (s, j),
               (m, n), (tm, tn), lambda i, j, s: (i, j), ((0,), (0,)), out_dtype)


def _mm_n_st(name, a, bst, out_dtype=F32):
    m, k = a.shape
    ns_, _, ns = bst.shape
    tm, tk = _tm_wide(m, ns), _tile(k, (512, 256, 128))
    return _mm(name, a, bst, m, ns_ * ns, k // tk, tm, ns,
               (tm, tk), lambda i, j, s: (i, s), (None, tk, ns), lambda i, j, s: (j, s, 0),
               (ns_, m, ns), (None, tm, ns), lambda i, j, s: (j, i, 0), ((1,), (0,)), out_dtype)


def _mm_st_t(name, ast, bst, out_dtype=F32, scatter=()):
    ns_, m, ns = ast.shape
    n = bst.shape[1]
    tm, tn = _tile(m, _TM), _tile(n, _TN)
    return _mm(name, ast, bst, m, n, ns_, tm, tn,
               (None, tm, ns), lambda i, j, s: (s, i, 0), (None, tn, ns), lambda i, j, s: (s, j, 0),
               (m, n), (tm, tn), lambda i, j, s: (i, j), ((1,), (1,)), out_dtype, scatter)


def _mm_t_st(name, a, bst, out_dtype=F32):
    k, m = a.shape
    ns_, _, ns = bst.shape
    tm, tk = _tile(m, _TN), _tile(k, _TK_WIDE)
    return _mm(name, a, bst, m, ns_ * ns, k // tk, tm, ns,
               (tk, tm), lambda i, j, s: (s, i), (None, tk, ns), lambda i, j, s: (j, s, 0),
               (ns_, m, ns), (None, tm, ns), lambda i, j, s: (j, i, 0), ((0,), (0,)), out_dtype)


def _scan_order(j, n_ctx, n_all, rev):
    if not rev:
        return j
    return jnp.where(j < n_ctx, n_ctx - 1 - j, n_all - 1 - (j - n_ctx))


def _hg_scan_fwd(name, p_hg, lb2, d, n_ctx):
    t, w5 = p_hg.shape
    w = w5 // 5
    h = w // HG_HEAD
    n = t // STEP
    rev = d == 1

    def body(q_ref, i_ref, f_ref, lb_ref, o_ref, st_ref, s_ref):
        j = pl.program_id(0)

        @pl.when(j == 0)
        def _():
            s_ref[...] = jnp.zeros_like(s_ref)

        s0 = s_ref[...]
        st_ref[...] = s0
        o, s1 = _hg_step(s0, q_ref[...], i_ref[...], f_ref[...], lb_ref[...], rev)
        o_ref[...] = o
        s_ref[...] = s1

    def rows(cb):
        return pl.BlockSpec((STEP, w), lambda j: (_scan_order(j, n_ctx, n, rev), cb))

    return pl.pallas_call(
        body, name=name, grid=(n,),
        in_specs=[rows(0), rows(1), rows(2 + d), pl.BlockSpec((2, w), lambda j: (0, 0))],
        out_specs=[rows(0), pl.BlockSpec((None, h, HG_HEAD, HG_HEAD), lambda j: (j, 0, 0, 0))],
        out_shape=[jax.ShapeDtypeStruct((t, w), F32), jax.ShapeDtypeStruct((n, h, HG_HEAD, HG_HEAD), F32)],
        scratch_shapes=[pltpu.VMEM((h, HG_HEAD, HG_HEAD), F32)],
        compiler_params=_params(("arbitrary",)),
    )(p_hg, p_hg, p_hg, lb2)


def _hg_scan_bwd(name, p_hg, lb2, states, do, d, n_ctx):
    t, w5 = p_hg.shape
    w = w5 // 5
    h = w // HG_HEAD
    n = t // STEP
    rev = d == 1

    def body(q_ref, i_ref, f_ref, lb_ref, st_ref, do_ref, dq_ref, di_ref, df_ref, dlb_ref, ds_ref):
        step = pl.program_id(0)

        @pl.when(step == 0)
        def _():
            ds_ref[...] = jnp.zeros_like(ds_ref)
            dlb_ref[...] = jnp.zeros_like(dlb_ref)

        _, vjp = jax.vjp(lambda s0, q, i, f, lb: _hg_step(s0, q, i, f, lb, rev),
                         st_ref[...], q_ref[...], i_ref[...], f_ref[...], lb_ref[...])
        ds0, dq, di, df, dlb = vjp((do_ref[...], ds_ref[...]))
        dq_ref[...] = dq
        di_ref[...] = di
        df_ref[...] = df
        dlb_ref[...] += dlb
        ds_ref[...] = ds0

    def rows(cb):
        return pl.BlockSpec((STEP, w), lambda s: (_scan_order(n - 1 - s, n_ctx, n, rev), cb))

    return pl.pallas_call(
        body, name=name, grid=(n,),
        in_specs=[rows(0), rows(1), rows(2 + d), pl.BlockSpec((2, w), lambda s: (0, 0)),
                  pl.BlockSpec((None, h, HG_HEAD, HG_HEAD), lambda s: (n - 1 - s, 0, 0, 0)), rows(0)],
        out_specs=[rows(0), rows(0), rows(0), pl.BlockSpec((2, w), lambda s: (0, 0))],
        out_shape=[jax.ShapeDtypeStruct((t, w), F32)] * 3 + [jax.ShapeDtypeStruct((2, w), F32)],
        scratch_shapes=[pltpu.VMEM((h, HG_HEAD, HG_HEAD), F32)],
        compiler_params=_params(("arbitrary",)),
    )(p_hg, p_hg, p_hg, lb2, states, do)


def _to_heads(a, nh):
    return jnp.stack([a[:, i * RW_HEAD:(i + 1) * RW_HEAD] for i in range(nh)], axis=0)


def _from_heads(a):
    return jnp.concatenate([a[i] for i in range(a.shape[0])], axis=-1)


def _rw_scan_fwd(name, sh, hp, d, n_ctx):
    t = sh.shape[0]
    w = (sh.shape[1] - 4 * RW_LORA) // 3
    nh = w // RW_HEAD
    n = t // STEP
    rev = d == 1
    lo = 3 * w // LANE

    def body(r_ref, k_ref, v_ref, wl_ref, al_ref, w0_ref, w2_ref, a0_ref, a2_ref, kk_ref, ka_ref,
             y_ref, st_ref, s_ref):
        j = pl.program_id(0)

        @pl.when(j == 0)
        def _():
            s_ref[...] = jnp.zeros_like(s_ref)

        s0 = s_ref[...]
        st_ref[...] = s0
        wl = wl_ref[...][:, d * RW_LORA:(d + 1) * RW_LORA]
        al = al_ref[...][:, d * RW_LORA:(d + 1) * RW_LORA]
        y, s1 = _rw_step(s0, _to_heads(r_ref[...], nh), _to_heads(k_ref[...], nh), _to_heads(v_ref[...], nh), wl, al,
                         w0_ref[...], w2_ref[...], a0_ref[...], a2_ref[...], kk_ref[...], ka_ref[...], rev)
        y_ref[...] = _from_heads(y)
        s_ref[...] = s1

    def rows(cb, width=w):
        return pl.BlockSpec((STEP, width), lambda j: (_scan_order(j, n_ctx, n, rev), cb))

    def whole(a):
        nd = a.ndim
        return pl.BlockSpec(a.shape, lambda j: (0,) * nd)

    return pl.pallas_call(
        body, name=name, grid=(n,),
        in_specs=[rows(0), rows(1), rows(2), rows(lo, LANE), rows(lo + 1, LANE)] + [whole(a) for a in hp],
        out_specs=[rows(0), pl.BlockSpec((None, nh, RW_HEAD, RW_HEAD), lambda j: (j, 0, 0, 0))],
        out_shape=[jax.ShapeDtypeStruct((t, w), F32), jax.ShapeDtypeStruct((n, nh, RW_HEAD, RW_HEAD), F32)],
        scratch_shapes=[pltpu.VMEM((nh, RW_HEAD, RW_HEAD), F32)],
        compiler_params=_params(("arbitrary",)),
    )(sh, sh, sh, sh, sh, *hp)


def _rw_scan_bwd(name, sh, hp, states, dy, d, n_ctx):
    t = sh.shape[0]
    w = (sh.shape[1] - 4 * RW_LORA) // 3
    nh = w // RW_HEAD
    n = t // STEP
    rev = d == 1
    lo = 3 * w // LANE

    def body(r_ref, k_ref, v_ref, wl_ref, al_ref, w0_ref, w2_ref, a0_ref, a2_ref, kk_ref, ka_ref, st_ref, dy_ref,
             dsh_ref, dw0_ref, dw2_ref, da0_ref, da2_ref, dkk_ref, dka_ref, ds_ref):
        step = pl.program_id(0)
        pouts = (dw0_ref, dw2_ref, da0_ref, da2_ref, dkk_ref, dka_ref)

        @pl.when(step == 0)
        def _():
            ds_ref[...] = jnp.zeros_like(ds_ref)
            for ref in pouts:
                ref[...] = jnp.zeros_like(ref)

        wl = wl_ref[...][:, d * RW_LORA:(d + 1) * RW_LORA]
        al = al_ref[...][:, d * RW_LORA:(d + 1) * RW_LORA]
        _, vjp = jax.vjp(functools.partial(_rw_step, rev=rev),
                         st_ref[...], _to_heads(r_ref[...], nh), _to_heads(k_ref[...], nh), _to_heads(v_ref[...], nh),
                         wl, al, w0_ref[...], w2_ref[...], a0_ref[...], a2_ref[...], kk_ref[...], ka_ref[...])
        g = vjp((_to_heads(dy_ref[...], nh), ds_ref[...]))
        ds_ref[...] = g[0]
        zero = jnp.zeros_like(g[4])
        lora = [zero] * 4
        lora[d], lora[2 + d] = g[4], g[5]
        dsh_ref[...] = jnp.concatenate([_from_heads(g[1]), _from_heads(g[2]), _from_heads(g[3])] + lora, axis=-1)
        for ref, val in zip(pouts, g[6:]):
            ref[...] += val

    def rows(cb, width=w):
        return pl.BlockSpec((STEP, width), lambda s: (_scan_order(n - 1 - s, n_ctx, n, rev), cb))

    def whole(a):
        nd = a.ndim
        return pl.BlockSpec(a.shape, lambda s: (0,) * nd)

    return pl.pallas_call(
        body, name=name, grid=(n,),
        in_specs=[rows(0), rows(1), rows(2), rows(lo, LANE), rows(lo + 1, LANE)] + [whole(a) for a in hp] +
                 [pl.BlockSpec((None, nh, RW_HEAD, RW_HEAD), lambda s: (n - 1 - s, 0, 0, 0)), rows(0)],
        out_specs=[rows(0, sh.shape[1])] + [whole(a) for a in hp],
        out_shape=[jax.ShapeDtypeStruct(sh.shape, F32)] + [jax.ShapeDtypeStruct(a.shape, F32) for a in hp],
        scratch_shapes=[pltpu.VMEM((nh, RW_HEAD, RW_HEAD), F32)],
        compiler_params=_params(("arbitrary",)),
    )(sh, sh, sh, sh, sh, *hp, states, dy)


def _rw_scan_bwd_both(name, sh, hps, states, dy, n_ctx):
    t = sh.shape[0]
    w = (sh.shape[1] - 4 * RW_LORA) // 3
    nh = w // RW_HEAD
    n = t // STEP
    lo = 3 * w // LANE
    n_in, n_p = 13, 6

    def body(*refs):
        step = pl.program_id(0)
        ins = [refs[d * n_in:(d + 1) * n_in] for d in range(2)]
        outs = [refs[2 * n_in + d * (1 + n_p):2 * n_in + (d + 1) * (1 + n_p)] for d in range(2)]
        ds_refs = refs[2 * n_in + 2 * (1 + n_p):]

        @pl.when(step == 0)
        def _():
            for d in range(2):
                ds_refs[d][...] = jnp.zeros_like(ds_refs[d])
                for ref in outs[d][1:]:
                    ref[...] = jnp.zeros_like(ref)

        for d in range(2):
            r_ref, k_ref, v_ref, wl_ref, al_ref = ins[d][:5]
            hp_refs, st_ref, dy_ref = ins[d][5:11], ins[d][11], ins[d][12]
            wl = wl_ref[...][:, d * RW_LORA:(d + 1) * RW_LORA]
            al = al_ref[...][:, d * RW_LORA:(d + 1) * RW_LORA]
            _, vjp = jax.vjp(functools.partial(_rw_step, rev=d == 1),
                             st_ref[...], _to_heads(r_ref[...], nh), _to_heads(k_ref[...], nh), _to_heads(v_ref[...], nh),
                             wl, al, *[p[...] for p in hp_refs])
            g = vjp((_to_heads(dy_ref[...], nh), ds_refs[d][...]))
            ds_refs[d][...] = g[0]
            zero = jnp.zeros_like(g[4])
            lora = [zero] * 4
            lora[d], lora[2 + d] = g[4], g[5]
            outs[d][0][...] = jnp.concatenate([_from_heads(g[1]), _from_heads(g[2]), _from_heads(g[3])] + lora, axis=-1)
            for ref, val in zip(outs[d][1:], g[6:]):
                ref[...] += val

    def rows(d, cb, width=w):
        return pl.BlockSpec((STEP, width), lambda s: (_scan_order(n - 1 - s, n_ctx, n, d == 1), cb))

    def whole(a):
        nd = a.ndim
        return pl.BlockSpec(a.shape, lambda s: (0,) * nd)

    in_specs, operands, out_specs, out_shape = [], [], [], []
    for d in range(2):
        in_specs += [rows(d, 0), rows(d, 1), rows(d, 2), rows(d, lo, LANE), rows(d, lo + 1, LANE)]
        in_specs += [whole(a) for a in hps[d]]
        in_specs += [pl.BlockSpec((None, nh, RW_HEAD, RW_HEAD), lambda s: (n - 1 - s, 0, 0, 0)), rows(d, 0)]
        operands += [sh] * 5 + list(hps[d]) + [states[d], dy]
        out_specs += [rows(d, 0, sh.shape[1])] + [whole(a) for a in hps[d]]
        out_shape += [jax.ShapeDtypeStruct(sh.shape, F32)] + [jax.ShapeDtypeStruct(a.shape, F32) for a in hps[d]]
    res = pl.pallas_call(
        body, name=name, grid=(n,), in_specs=in_specs, out_specs=out_specs, out_shape=out_shape,
        scratch_shapes=[pltpu.VMEM((nh, RW_HEAD, RW_HEAD), F32)] * 2, compiler_params=_params(("arbitrary",)),
    )(*operands)
    return [res[0], res[1 + n_p]], [res[1:1 + n_p], res[2 + n_p:]]


def _shift_masks(t, n_ctx_rows):
    row = lax.broadcasted_iota(jnp.int32, (t, 1), 0)
    isx = row >= n_ctx_rows
    pos = jnp.where(isx, row - n_ctx_rows, row)
    col = jnp.where(isx, jnp.bitwise_and(pos, GRID_W - 1), pos)
    ncol = jnp.where(isx, GRID_W, n_ctx_rows)
    n_x = t - n_ctx_rows
    ml = col != 0
    mr = col != ncol - 1
    mu = isx & (pos >= GRID_W)
    md = isx & (pos < n_x - GRID_W)
    return ml, mr, mu, md, isx


def _shift_fwd(name, p, mu, n_ctx_rows):
    t, c = p.shape
    cw = LANE

    def body(p_ref, mu_ref, o_ref):
        x = p_ref[...]
        m = mu_ref[...]
        ml, mr, mup, mdn, isx = _shift_masks(t, n_ctx_rows)
        left = jnp.where(ml, pltpu.roll(x, 1, 0), 0.0)
        right = jnp.where(mr, pltpu.roll(x, t - 1, 0), 0.0)
        up = jnp.where(mup, pltpu.roll(x, GRID_W, 0), 0.0)
        down = jnp.where(mdn, pltpu.roll(x, t - GRID_W, 0), 0.0)
        out = x + m[0:1] * (left - x) + m[1:2] * (right - x)
        vert = m[2:3] * (up - x) + m[3:4] * (down - x)
        o_ref[...] = out + jnp.where(isx, vert, 0.0)

    return pl.pallas_call(
        body, name=name, grid=(c // cw,),
        in_specs=[pl.BlockSpec((t, cw), lambda j: (0, j)), pl.BlockSpec((4, cw), lambda j: (0, j))],
        out_specs=pl.BlockSpec((t, cw), lambda j: (0, j)),
        out_shape=jax.ShapeDtypeStruct((t, c), F32),
        compiler_params=_params(("parallel",)),
    )(p, mu)


def _shift_bwd(name, p, mu, dparts, n_ctx_rows):
    t, c = p.shape
    cw = LANE
    npart = len(dparts)

    def body(*refs):
        p_ref, mu_ref = refs[0], refs[1]
        dp_ref, dmu_ref = refs[2 + npart], refs[3 + npart]
        x = p_ref[...]
        m = mu_ref[...]
        g = refs[2][...]
        for r in refs[3:2 + npart]:
            g = g + r[...]
        ml, mr, mup, mdn, isx = _shift_masks(t, n_ctx_rows)
        left = jnp.where(ml, pltpu.roll(x, 1, 0), 0.0)
        right = jnp.where(mr, pltpu.roll(x, t - 1, 0), 0.0)
        up = jnp.where(mup, pltpu.roll(x, GRID_W, 0), 0.0)
        down = jnp.where(mdn, pltpu.roll(x, t - GRID_W, 0), 0.0)
        gx = jnp.where(isx, g, 0.0)
        dmu_ref[...] = jnp.concatenate([
            jnp.sum(g * (left - x), axis=0, keepdims=True), jnp.sum(g * (right - x), axis=0, keepdims=True),
            jnp.sum(gx * (up - x), axis=0, keepdims=True), jnp.sum(gx * (down - x), axis=0, keepdims=True)], axis=0)
        coef = 1.0 - m[0:1] - m[1:2] - jnp.where(isx, m[2:3] + m[3:4], 0.0)
        dp = coef * g
        dp = dp + m[0:1] * pltpu.roll(jnp.where(ml, g, 0.0), t - 1, 0)
        dp = dp + m[1:2] * pltpu.roll(jnp.where(mr, g, 0.0), 1, 0)
        dp = dp + m[2:3] * pltpu.roll(jnp.where(mup, g, 0.0), t - GRID_W, 0)
        dp = dp + m[3:4] * pltpu.roll(jnp.where(mdn, g, 0.0), GRID_W, 0)
        dp_ref[...] = dp

    col = pl.BlockSpec((t, cw), lambda j: (0, j))
    par = pl.BlockSpec((4, cw), lambda j: (0, j))
    return pl.pallas_call(
        body, name=name, grid=(c // cw,),
        in_specs=[col, par] + [col] * npart,
        out_specs=[col, par],
        out_shape=[jax.ShapeDtypeStruct((t, c), F32), jax.ShapeDtypeStruct((4, c), F32)],
        compiler_params=_params(("parallel",)),
    )(p, mu, *dparts)


def _local_step(x, c, ctx, c_ctx, ada_st, ada_b, norm_g, w_in_st, hg_lb, hg_norm_g, rw_mu, rw_w0, rw_w2, rw_a0, rw_a2,
                rw_kk, rw_ka, rw_rk, rw_gn_g, rw_gn_b, w_hg_st, w_rw_st, w_out, final_g, tgt, my_core):
    seq, dm = x.shape
    n_ctx_rows = ctx.shape[0]
    t = seq + n_ctx_rows
    hw = hg_norm_g.shape[-1]
    rw = rw_kk.shape[-1]
    nh_rw = rw // RW_HEAD
    n_ctx = n_ctx_rows // STEP
    tm = _tile(n_ctx_rows, (256, 128, 64))
    nt = t // tm
    nct = n_ctx_rows // tm
    n_sh_cols = 3 * rw + 4 * RW_LORA

    xs = jnp.concatenate([ctx, x], axis=0)
    cond = jnp.concatenate([c.reshape(1, dm), c_ctx.reshape(1, dm), jnp.zeros((6, dm), F32)], axis=0)
    final_g2 = final_g.reshape(1, dm)

    def unstack(a_st):
        return jnp.swapaxes(a_st, 0, 1).reshape(a_st.shape[1], -1)

    def restack(a, ns=N_SHARD):
        return jnp.swapaxes(a.reshape(a.shape[0], ns, -1), 0, 1)

    (sc,) = _row_call("cond_silu", lambda i, r, f: ([jax.nn.silu(r[0])], []), 1, 8, [(cond, 0, dm, 0)], [],
                      [(8, dm, F32, 0)], [])
    mod_st = _mm_n_st("mod_mm", sc, ada_st)
    def add(name, terms, shape):
        flat2 = [a.reshape(-1, a.shape[-1]) for a in terms]
        return _rowwise(name, lambda *v: _slot_sum(list(v)), flat2, F32).reshape(shape)

    mod = add("mod_bias", [unstack(mod_st), jnp.broadcast_to(ada_b, (8, 3 * dm))], (8, 3 * dm))
    mod3 = mod.reshape(8, 3, dm)

    def pick(i, m3):
        r = jnp.where(i < nct, m3[1], m3[0])
        return r[0:1], r[1:2]

    def h_fn(i, r, f):
        shift, scale = pick(i, f[1])
        return [_fn_h(r[0], f[0], scale, shift)], []

    (h,) = _row_call("h_fwd", h_fn, nt, tm, [(xs, 0, dm, 0)], [norm_g, mod3], [(t, dm, BF16, 0)], [])
    proj = unstack(_mm_n_st("proj_mm", h, w_in_st))
    p_hg = proj[:, :5 * hw]
    p_rs = proj[:, 5 * hw:5 * hw + n_sh_cols]
    p_zr = proj[:, 5 * hw + n_sh_cols:5 * hw + n_sh_cols + rw]
    p_gt = proj[:, 5 * hw + n_sh_cols + rw:]

    o_hg, st_hg = [], []
    for d in range(2):
        o, st = _hg_scan_fwd(f"hg_scan_fwd{d}", p_hg, hg_lb[d], d, n_ctx)
        o_hg.append(o)
        st_hg.append(st)

    def hgpost_fn(i, r, f):
        return [_fn_hgpost(r[0], r[1], r[2], f[0])], []

    hg_in = [(o_hg[0], 0, hw, 0), (o_hg[1], 0, hw, 0), (p_hg, 4, hw, 0)]
    (y_hg,) = _row_call("hg_post", hgpost_fn, nt, tm, hg_in, [hg_norm_g], [(t, hw, BF16, 0)], [])

    sh = _shift_fwd("rw_shift", p_rs, rw_mu, n_ctx_rows)
    hps = []
    for d in range(2):
        hps.append([rw_w0[d].reshape(nh_rw, 1, RW_HEAD), jnp.swapaxes(rw_w2[d].reshape(RW_LORA, nh_rw, RW_HEAD), 0, 1),
                    rw_a0[d].reshape(nh_rw, 1, RW_HEAD), jnp.swapaxes(rw_a2[d].reshape(RW_LORA, nh_rw, RW_HEAD), 0, 1),
                    rw_kk.reshape(nh_rw, 1, RW_HEAD), rw_ka.reshape(nh_rw, 1, RW_HEAD)])
    y_rw_d, st_rw = [], []
    for d in range(2):
        y, st = _rw_scan_fwd(f"rw_scan_fwd{d}", sh, hps[d], d, n_ctx)
        y_rw_d.append(y)
        st_rw.append(st)

    rw_full = [rw_a0, rw_a2, rw_ka, rw_rk, rw_gn_g, rw_gn_b]
    lo = 3 * rw // LANE
    rw_in = [(y_rw_d[0], 0, rw, 0), (y_rw_d[1], 0, rw, 0), (sh, 0, rw, 0), (sh, 1, rw, 0), (sh, 2, rw, 0),
             (sh, lo + 1, LANE, 0), (p_zr, 0, rw, 0)]

    def rwpost_fn(i, r, f):
        return [_fn_rwpost(*r, *f)], []

    (y_rw,) = _row_call("rw_post", rwpost_fn, nt, tm, rw_in, rw_full, [(t, rw, BF16, 0)], [])

    a_hg = unstack(_mm_n_st("hg_out_mm", y_hg, w_hg_st))
    a_rw = unstack(_mm_n_st("rw_out_mm", y_rw, w_rw_st))
    mg_in = [(a_hg, 0, dm, 0), (a_rw, 0, dm, 0), (p_gt, 0, dm, 0), (p_gt, 1, dm, 0)]
    (merged,) = _row_call("merge", lambda i, r, f: ([_fn_merge(*r)], []), nt, tm, mg_in, [], [(t, dm, BF16, 0)], [])
    o_out = _mm_nn("out_mm", merged, w_out)

    def final_fn(i, r, f):
        gate = f[0][0][2:3]
        loss, vjp = jax.vjp(_fn_final, r[0], r[1], gate, f[1], r[2])
        dx, do, dgate, dfg, _ = vjp(jnp.ones((), F32))
        live = i >= nct
        zero = lambda a: jnp.where(live, a, 0.0)
        dmod = jnp.concatenate([jnp.concatenate([jnp.zeros((1, 2 * dm), F32), zero(dgate)], axis=1),
                                jnp.zeros((7, 3 * dm), F32)], axis=0)
        return [zero(dx), zero(do)], [jnp.broadcast_to(zero(loss), (8, LANE)), dmod, zero(dfg)]

    fin_in = [(xs, 0, dm, 0), (o_out, 0, dm, 0), (tgt, 0, dm, nct)]
    dx_res, d_o, loss_acc, dmod_gate, d_final_g = _row_call(
        "final", final_fn, nt, tm, fin_in, [mod3, final_g2], [(t, dm, F32, 0), (t, dm, BF16, 0)],
        [((8, LANE), F32), ((8, 3 * dm), F32), ((1, dm), F32)])

    g_w_out = _mm_tn("d_w_out", merged, d_o)
    d_merged = _mm_nt("d_merged", d_o, w_out)

    def merge_bwd(i, r, f):
        _, vjp = jax.vjp(_fn_merge, r[0], r[1], r[2], r[3])
        da, db, dgh, dgr = vjp(r[4])
        return [da, db, jnp.concatenate([dgh, dgr], axis=1)], []

    da_hg, da_rw, dp_gt = _row_call("merge_bwd", merge_bwd, nt, tm, mg_in + [(d_merged, 0, dm, 0)], [],
                                    [(t, dm, BF16, 0), (t, dm, BF16, 0), (t, 2 * dm, F32, 0)], [])
    g_w_hg_st = _mm_t_st("d_w_hg", y_hg, restack(da_hg))
    g_w_rw_st = _mm_t_st("d_w_rw", y_rw, restack(da_rw))
    dy_hg = _mm_st_t("d_y_hg", restack(da_hg), w_hg_st)
    dy_rw = _mm_st_t("d_y_rw", restack(da_rw), w_rw_st)

    def hgpost_bwd(i, r, f):
        _, vjp = jax.vjp(_fn_hgpost, r[0], r[1], r[2], f[0])
        dof, _, dz, dg = vjp(r[3])
        return [dof, dz], [dg]

    do_hg, dz_hg, g_hg_norm = _row_call("hg_post_bwd", hgpost_bwd, nt, tm, hg_in + [(dy_hg, 0, hw, 0)], [hg_norm_g],
                                        [(t, hw, F32, 0), (t, hw, F32, 0)], [((1, hw), F32)])
    dqs, dis, dfs, g_lb = [], [], [], []
    for d in range(2):
        dq, di, df, dlb = _hg_scan_bwd(f"hg_scan_bwd{d}", p_hg, hg_lb[d], st_hg[d], do_hg, d, n_ctx)
        dqs.append(dq)
        dis.append(di)
        dfs.append(df)
        g_lb.append(dlb)
    (dqi,) = _row_call("hg_dsum", lambda i, r, f: ([jnp.concatenate([r[0] + r[1], r[2] + r[3]], axis=1)], []), nt, tm,
                       [(dqs[0], 0, hw, 0), (dqs[1], 0, hw, 0), (dis[0], 0, hw, 0), (dis[1], 0, hw, 0)], [],
                       [(t, 2 * hw, F32, 0)], [])
    g_hg_lb = jnp.stack(g_lb, axis=0)

    def rwpost_bwd(i, r, f):
        _, vjp = jax.vjp(_fn_rwpost, *r[:7], *f)
        g = vjp(r[7])
        zl = jnp.zeros((g[5].shape[0], 2 * RW_LORA), F32)
        return [g[0], jnp.concatenate([g[2], g[3], g[4], zl, g[5]], axis=1), g[6]], list(g[7:])

    dy_sum, dsh_p, dz_rw, g_a0_p, g_a2_p, g_ka_p, g_rk, g_gn_g, g_gn_b = _row_call(
        "rw_post_bwd", rwpost_bwd, nt, tm, rw_in + [(dy_rw, 0, rw, 0)], rw_full,
        [(t, rw, F32, 0), (t, n_sh_cols, F32, 0), (t, rw, F32, 0)], [(a.shape, F32) for a in rw_full])
    dsh_dirs, hp_grads = _rw_scan_bwd_both("rw_scan_bwd", sh, hps, st_rw, dy_sum, n_ctx)
    dp_rs, g_mu = _shift_bwd("rw_shift_bwd", p_rs, rw_mu, [dsh_p] + dsh_dirs, n_ctx_rows)

    def flat(a):
        if a.shape[1] == 1:
            return a.reshape(rw)
        return jnp.swapaxes(a, 0, 1).reshape(RW_LORA, rw)

    g_w0 = jnp.stack([flat(hp_grads[d][0]) for d in range(2)], axis=0)
    g_w2 = jnp.stack([flat(hp_grads[d][1]) for d in range(2)], axis=0)
    g_a0 = add("g_a0", [jnp.stack([flat(hp_grads[d][2]) for d in range(2)], axis=0), g_a0_p], (2, rw))
    g_a2 = add("g_a2", [jnp.stack([flat(hp_grads[d][3]) for d in range(2)], axis=0), g_a2_p], (2, RW_LORA, rw))
    g_kk = add("g_kk", [flat(hp_grads[0][4]).reshape(1, rw), flat(hp_grads[1][4]).reshape(1, rw)], (1, rw))
    g_ka = add("g_ka", [flat(hp_grads[0][5]).reshape(1, rw), flat(hp_grads[1][5]).reshape(1, rw), g_ka_p], (1, rw))

    dproj = jnp.concatenate([dqi, dfs[0], dfs[1], dz_hg, dp_rs, dz_rw, dp_gt], axis=1).astype(BF16)
    dproj_st = restack(dproj)
    g_w_in_st = _mm_t_st("d_w_in", h, dproj_st)
    w_in_chip = _pair_exchange("grads_pair_sum_w_in", g_w_in_st, my_core, True, BF16)
    dh, (w_in_landed,) = _mm_st_t("d_h", dproj_st, w_in_st, scatter=(w_in_chip,))

    def h_bwd(i, r, f):
        shift, scale = pick(i, f[1])
        _, vjp = jax.vjp(_fn_h, r[0], f[0], scale, shift)
        ds, dg, dscale, dshift = vjp(r[1])
        row = jnp.concatenate([dshift, dscale, jnp.zeros((1, dm), F32)], axis=1)
        z = jnp.zeros_like(row)
        is_ctx = i < nct
        dmod = jnp.concatenate([jnp.where(is_ctx, z, row), jnp.where(is_ctx, row, z), jnp.zeros((6, 3 * dm), F32)], axis=0)
        return [ds + r[2]], [dg, dmod]

    grad_x, g_norm_g, dmod_h = _row_call(
        "h_bwd", h_bwd, nt, tm, [(xs, 0, dm, 0), (dh, 0, dm, 0), (dx_res, 0, dm, 0)], [norm_g, mod3],
        [(seq, dm, F32, nct)], [((1, dm), F32), ((8, 3 * dm), F32)])
    dmod = add("d_mod", [dmod_h, dmod_gate], (8, 3 * dm))
    g_ada_b = add("g_ada_b", [dmod[0:1], dmod[1:2]], (1, 3 * dm))
    g_ada_st = _mm_t_st("d_ada_w", sc, restack(dmod))
    d_sc = _mm_st_t("d_cond", restack(dmod), ada_st)

    def cond_bwd(i, r, f):
        _, vjp = jax.vjp(jax.nn.silu, r[0])
        return [vjp(r[1])[0]], []

    (d_cond,) = _row_call("cond_bwd", cond_bwd, 1, 8, [(cond, 0, dm, 0), (d_sc, 0, dm, 0)], [], [(8, dm, F32, 0)], [])

    grads = dict(
        c_ctx=d_cond[1], ada_w=g_ada_st, ada_b=g_ada_b, norm_g=g_norm_g, w_in=(w_in_chip, w_in_landed), hg_lb=g_hg_lb,
        hg_norm_g=g_hg_norm, rw_mu=g_mu, rw_w0=g_w0, rw_w2=g_w2, rw_a0=g_a0, rw_a2=g_a2, rw_kk=g_kk, rw_ka=g_ka,
        rw_rk=g_rk, rw_gn_g=g_gn_g, rw_gn_b=g_gn_b, w_hg_out=g_w_hg_st, w_rw_out=g_w_rw_st, w_out=g_w_out,
        final_g=d_final_g.reshape(dm))
    return loss_acc[0:1, 0:1], grad_x, grads


def _my_place():
    return lax.axis_index("x"), lax.axis_index("y"), lax.axis_index("c")


MIN_CHUNK_BYTES = 1 << 18
ROW_ALIGN = 16


def _n_chunks(rows, row_bytes):
    for n in (8, 4, 2):
        if rows % (n * ROW_ALIGN) == 0 and rows // n * row_bytes >= MIN_CHUNK_BYTES:
            return n
    return 1


def _row_bytes(a, lead=1):
    n = a.dtype.itemsize
    for d in a.shape[lead:]:
        n *= d
    return n


def _rows(ref, start, size):
    return ref.at[pl.ds(start, size)]


def _chunked(make, start, size, n):
    cs = size // n
    return [make(start + j * cs, cs) for j in range(n)]


_PEER_CHIPS = 3


def _weights_gather(name, big, small):
    nb, na = len(big), len(big) + len(small)
    arrays = list(big) + list(small)
    n_ici = 6

    def body(*refs):
        outs = refs[na:2 * na]
        send_sems, recv_sems, fsend_sems, frecv_sems = refs[2 * na:]
        x, y, c = _my_place()
        me, sx, sy, sd = 2 * x + y, 2 * (1 - x) + y, 2 * x + (1 - y), 2 * (1 - x) + (1 - y)
        kx, ky, kd = (1 - x, y, c), (x, 1 - y, c), (1 - x, 1 - y, c)

        def ici(a, j, src_slot, dst_slot, to, r0, nr):
            return pltpu.make_async_remote_copy(
                src_ref=_rows(outs[a].at[src_slot], r0, nr), dst_ref=_rows(outs[a].at[dst_slot], r0, nr),
                send_sem=send_sems.at[a, j], recv_sem=recv_sems.at[a, j], device_id=to,
                device_id_type=pl.DeviceIdType.MESH)

        def to_sibling(a, k, slot, r0, nr):
            rows = _rows(outs[a].at[slot], r0, nr)
            return pltpu.make_async_remote_copy(
                src_ref=rows, dst_ref=rows, send_sem=fsend_sems.at[a, k], recv_sem=frecv_sems.at[a, k],
                device_id=(x, y, 1 - c), device_id_type=pl.DeviceIdType.MESH)

        def start(copies):
            for cp in copies:
                cp.start()

        geo = []
        for a in range(nb):
            half = arrays[a].shape[1] // 2
            geo.append((pl.multiple_of(c * half, ROW_ALIGN), pl.multiple_of((1 - c) * half, ROW_ALIGN), half // 2,
                        _n_chunks(half // 2, _row_bytes(arrays[a], 2))))
        plan = [(me, sx, kx, 0), (me, sx, kx, 1), (me, sy, ky, 0), (me, sy, ky, 1), (sx, sd, ky, 0), (sy, sd, kx, 1)]

        def piece(a, j):
            return geo[a][0] + plan[j][3] * geo[a][2]

        for a in range(nb):
            for j in range(4):
                start(_chunked(lambda r0, cs: ici(a, j, me, me, plan[j][2], r0, cs), piece(a, j), geo[a][2], geo[a][3]))
        for a in range(nb, na):
            rows = arrays[a].shape[1]
            for j, to in ((0, kx), (2, ky), (1, kd)):
                ici(a, j, me, me, to, 0, rows).start()
        for a in range(nb):
            for j, first in ((4, 0), (5, 3)):
                src_slot, _, to, _ = plan[j]
                ici(a, first, me, plan[first][1], plan[first][2], piece(a, first), geo[a][2]).wait_recv()
                start(_chunked(lambda r0, cs: ici(a, j, src_slot, src_slot, to, r0, cs), piece(a, j), geo[a][2], geo[a][3]))
        for a in range(nb):
            for j in (1, 2):
                ici(a, j, me, plan[j][1], plan[j][2], piece(a, j), geo[a][2]).wait_recv()
            for k, slot in ((0, sx), (1, sy)):
                start(_chunked(lambda r0, cs: to_sibling(a, k, slot, r0, cs), geo[a][0], 2 * geo[a][2], geo[a][3]))
        for a in range(nb):
            for j in (4, 5):
                ici(a, j, me, sd, plan[j][2], piece(a, j), geo[a][2]).wait_recv()
            start(_chunked(lambda r0, cs: to_sibling(a, 2, sd, r0, cs), geo[a][0], 2 * geo[a][2], geo[a][3]))
        for a in range(nb, na):
            rows = arrays[a].shape[1]
            for j, slot, to in ((0, sx, kx), (2, sy, ky), (1, sd, kd)):
                ici(a, j, me, slot, to, 0, rows).wait_recv()
        for a in range(nb):
            for k, slot in ((0, sx), (1, sy), (2, sd)):
                to_sibling(a, k, slot, geo[a][1], 2 * geo[a][2]).wait_recv()
        for a in range(nb):
            for j in range(n_ici):
                ici(a, j, me, me, plan[j][2], piece(a, j), geo[a][2]).wait_send()
            for k, slot in ((0, sx), (1, sy), (2, sd)):
                to_sibling(a, k, slot, geo[a][0], 2 * geo[a][2]).wait_send()
        for a in range(nb, na):
            rows = arrays[a].shape[1]
            for j, to in ((0, kx), (2, ky), (1, kd)):
                ici(a, j, me, me, to, 0, rows).wait_send()

    hbm = pl.BlockSpec(memory_space=pl.ANY)
    ici_sems = pltpu.SemaphoreType.DMA((na, n_ici))
    pair_sems = pltpu.SemaphoreType.DMA((na, _PEER_CHIPS))
    return pl.pallas_call(
        body, name=name, in_specs=[hbm] * na, out_specs=[hbm] * na,
        out_shape=[jax.ShapeDtypeStruct(a.shape, a.dtype) for a in arrays],
        input_output_aliases={a: a for a in range(na)}, scratch_shapes=[ici_sems, ici_sems, pair_sems, pair_sems],
    )(*arrays)


def _scatter_copy(arrays, ins, outs, send_sems, recv_sems, a, k, slot, r0, nr):
    x, y, c = _my_place()
    px, py = [(1 - x, y), (x, 1 - y), (1 - x, 1 - y)][k]
    return pltpu.make_async_remote_copy(
        src_ref=_rows(ins[a].at[2 * px + py], r0, nr), dst_ref=_rows(outs[a].at[slot], r0, nr),
        send_sem=send_sems.at[a, k], recv_sem=recv_sems.at[a, k], device_id=(px, py, c),
        device_id_type=pl.DeviceIdType.MESH)


def _scatter_start(arrays, ins, outs, send_sems, recv_sems):
    x, y, _ = _my_place()
    for a in range(len(arrays)):
        rows = arrays[a].shape[1]
        for k in range(_PEER_CHIPS):
            for cp in _chunked(lambda r0, cs: _scatter_copy(arrays, ins, outs, send_sems, recv_sems, a, k, 2 * x + y, r0, cs),
                               0, rows, _n_chunks(rows, _row_bytes(arrays[a], 2))):
                cp.start()


def _scatter_wait(arrays, ins, outs, send_sems, recv_sems):
    x, y, _ = _my_place()
    peer_slot = [2 * (1 - x) + y, 2 * x + (1 - y), 2 * (1 - x) + (1 - y)]
    for k in range(_PEER_CHIPS):
        for a in range(len(arrays)):
            _scatter_copy(arrays, ins, outs, send_sems, recv_sems, a, k, peer_slot[k], 0, arrays[a].shape[1]).wait_recv()
    for a in range(len(arrays)):
        for k in range(_PEER_CHIPS):
            _scatter_copy(arrays, ins, outs, send_sems, recv_sems, a, k, 2 * x + y, 0, arrays[a].shape[1]).wait_send()


def _chip_scatter(name, arrays):
    na = len(arrays)

    def body(*refs):
        sc_refs = (refs[:na], refs[na:2 * na]) + tuple(refs[2 * na:])
        _scatter_start(arrays, *sc_refs)
        _scatter_wait(arrays, *sc_refs)

    hbm = pl.BlockSpec(memory_space=pl.ANY)
    sems = pltpu.SemaphoreType.DMA((na, _PEER_CHIPS))
    return pl.pallas_call(
        body, name=name, in_specs=[hbm] * na, out_specs=[hbm] * na,
        out_shape=[jax.ShapeDtypeStruct(a.shape, a.dtype) for a in arrays], scratch_shapes=[sems, sems],
    )(*arrays)


PAIR_TILE_BYTES = 2 << 20


def _pair_exchange(name, a, place, reduce, out_dtype):
    rows, cols = a.shape[-2], a.shape[-1]
    half = rows // 2 if reduce else rows
    tr = _row_tile_for(half, cols, budget=PAIR_TILE_BYTES)
    nh = half // tr
    n_steps = (N_SHARD if reduce else 1) * nh

    def body(pc_ref, *refs):
        if reduce:
            keep_ref, send_ref, o_ref, land, send_sems, recv_sems, credit, wire = refs
            wire[...] = send_ref[...].astype(BF16)
            src = wire
        else:
            send_ref, o_ref, land, send_sems, recv_sems, credit = refs
            src = send_ref
        x, y, c = _my_place()
        other = (x, y, 1 - c)
        t = pl.program_id(0) * nh + pl.program_id(1) if reduce else pl.program_id(0)
        slot = t % 2

        @pl.when(t >= 2)
        def _():
            pl.semaphore_wait(credit, 1)

        copy = pltpu.make_async_remote_copy(
            src_ref=src, dst_ref=land.at[slot], send_sem=send_sems.at[slot], recv_sem=recv_sems.at[slot],
            device_id=other, device_id_type=pl.DeviceIdType.MESH)
        copy.start()
        copy.wait_recv()
        got = land[slot]
        o_ref[...] = ((keep_ref[...] + got.astype(F32)) if reduce else got).astype(out_dtype)
        copy.wait_send()

        @pl.when(t < n_steps - 2)
        def _():
            pl.semaphore_signal(credit, inc=1, device_id=other, device_id_type=pl.DeviceIdType.MESH)

    if reduce:
        grid = (N_SHARD, nh)
        in_specs = [pl.BlockSpec((None, tr, cols), lambda j, i, pc: (j, pc[0] * nh + i, 0)),
                    pl.BlockSpec((None, tr, cols), lambda j, i, pc: (j, (1 - pc[0]) * nh + i, 0))]
        out_spec = pl.BlockSpec((None, tr, cols), lambda j, i, pc: (j, i, 0))
        out_shape = jax.ShapeDtypeStruct((N_SHARD, half, cols), out_dtype)
        operands = (a, a)
        sem = ("arbitrary", "arbitrary")
    else:
        grid = (nh,)
        in_specs = [pl.BlockSpec((tr, cols), lambda i, pc: (i, 0))]
        out_spec = pl.BlockSpec((tr, cols), lambda i, pc: (i, 0))
        out_shape = jax.ShapeDtypeStruct((half, cols), out_dtype)
        operands = (a,)
        sem = ("arbitrary",)
    return pl.pallas_call(
        body, name=name,
        grid_spec=pltpu.PrefetchScalarGridSpec(
            num_scalar_prefetch=1, grid=grid, in_specs=in_specs, out_specs=out_spec,
            scratch_shapes=[pltpu.VMEM((2, tr, cols), BF16 if reduce else a.dtype), pltpu.SemaphoreType.DMA((2,)),
                            pltpu.SemaphoreType.DMA((2,)), pltpu.SemaphoreType.REGULAR] +
                           ([pltpu.VMEM((tr, cols), BF16)] if reduce else [])),
        out_shape=out_shape, compiler_params=_params(sem),
    )(place, *operands)


def _cast_into_slot(name, a, chip):
    rows, cols = a.shape
    tm = _row_tile_for(rows, cols)

    def body(pc_ref, a_ref, o_ref):
        o_ref[...] = a_ref[...].astype(BF16)

    return pl.pallas_call(
        body, name=name,
        grid_spec=pltpu.PrefetchScalarGridSpec(
            num_scalar_prefetch=1, grid=(rows // tm,), in_specs=[pl.BlockSpec((tm, cols), lambda i, pc: (i, 0))],
            out_specs=pl.BlockSpec((None, tm, cols), lambda i, pc: (pc[0], i, 0))),
        out_shape=jax.ShapeDtypeStruct((N_SHARD, rows, cols), BF16), compiler_params=_params(("parallel",)),
    )(chip, a)


def _sum_landed(name, landed, sent, chip):
    ns, rows, cols = landed.shape
    tm = _row_tile_for(rows, cols)

    def body(pc_ref, *refs):
        own_ref, o_ref = refs[ns], refs[ns + 1]
        me = pc_ref[0]
        terms = [jnp.where(me == j, own_ref[...], refs[j][...]).astype(F32) for j in range(ns)]
        o_ref[...] = _slot_sum(terms)

    def landed_spec(j):
        return pl.BlockSpec((None, tm, cols), lambda i, pc: (jnp.where(pc[0] == j, (j + 1) % ns, j), i, 0))

    return pl.pallas_call(
        body, name=name,
        grid_spec=pltpu.PrefetchScalarGridSpec(
            num_scalar_prefetch=1, grid=(rows // tm,),
            in_specs=[landed_spec(j) for j in range(ns)] + [pl.BlockSpec((None, tm, cols), lambda i, pc: (pc[0], i, 0))],
            out_specs=pl.BlockSpec((tm, cols), lambda i, pc: (i, 0))),
        out_shape=jax.ShapeDtypeStruct((rows, cols), F32), compiler_params=_params(("parallel",)),
    )(chip, *([landed] * ns), sent)


def _gather_all(name, a):
    def body(in_ref, out_ref, send_sems, recv_sems, local_sem):
        x, y, c = _my_place()
        me = 4 * x + 2 * y + c

        def peer(k):
            return (x ^ (k >> 2), y ^ ((k >> 1) & 1), c ^ (k & 1))

        def remote(k, land):
            return pltpu.make_async_remote_copy(
                src_ref=in_ref, dst_ref=out_ref.at[land], send_sem=send_sems.at[k - 1], recv_sem=recv_sems.at[k - 1],
                device_id=peer(k), device_id_type=pl.DeviceIdType.MESH)

        local = pltpu.make_async_copy(in_ref, out_ref.at[me], local_sem)
        local.start()
        for k in range(1, N_DEV):
            remote(k, me).start()
        for k in range(1, N_DEV):
            px, py, pc = peer(k)
            remote(k, 4 * px + 2 * py + pc).wait_recv()
        for k in range(1, N_DEV):
            remote(k, me).wait_send()
        local.wait()

    hbm = pl.BlockSpec(memory_space=pl.ANY)
    return pl.pallas_call(
        body, name=name, in_specs=[hbm], out_specs=hbm,
        out_shape=jax.ShapeDtypeStruct((N_DEV,) + a.shape, a.dtype),
        scratch_shapes=[pltpu.SemaphoreType.DMA((N_DEV - 1,)), pltpu.SemaphoreType.DMA((N_DEV - 1,)), pltpu.SemaphoreType.DMA],
    )(a)


def _row_tile_for(rows, cols, budget=1 << 20):
    for tm in (1024, 512, 256, 128, 64, 32, 16, 8):
        if rows % tm == 0 and tm * cols * 4 <= budget:
            return tm
    return rows


def _slot_sum(vals):
    g = vals[0]
    for v in vals[1:]:
        g = g + v
    return g


def _rowwise(name, fn, arrays, out_dtype):
    rows, cols = arrays[0].shape
    tm = _row_tile_for(rows, cols)

    def body(*refs):
        refs[-1][...] = fn(*[r[...] for r in refs[:-1]]).astype(out_dtype)

    blk = pl.BlockSpec((tm, cols), lambda i: (i, 0))
    return pl.pallas_call(
        body, name=name, grid=(rows // tm,), in_specs=[blk] * len(arrays), out_specs=blk,
        out_shape=jax.ShapeDtypeStruct((rows, cols), out_dtype), compiler_params=_params(("parallel",)),
    )(*arrays)


def _sum_slots(name, st):
    ns, rows, cols = st.shape
    tm = _row_tile_for(rows, cols)

    def body(s_ref, o_ref):
        o_ref[...] = _slot_sum([s_ref[j].astype(F32) for j in range(ns)])

    return pl.pallas_call(
        body, name=name, grid=(rows // tm,),
        in_specs=[pl.BlockSpec((ns, tm, cols), lambda i: (0, i, 0))],
        out_specs=pl.BlockSpec((tm, cols), lambda i: (i, 0)),
        out_shape=jax.ShapeDtypeStruct((rows, cols), F32),
        compiler_params=_params(("parallel",)),
    )(st)


ADAM_TILE_BYTES = 1 << 19


def _adam_update(g, p_ref, m_ref, v_ref, go_ref, d_ref, mo_ref, vo_ref):
    mn = ADAM_B1 * m_ref[...] + (1.0 - ADAM_B1) * g
    vn = ADAM_B2 * v_ref[...] + (1.0 - ADAM_B2) * jnp.square(g)
    m_hat = mn / (1.0 - ADAM_B1 ** ADAM_STEP)
    v_hat = vn / (1.0 - ADAM_B2 ** ADAM_STEP)
    go_ref[...] = g
    d_ref[...] = -ADAM_LR * (m_hat / (jnp.sqrt(v_hat) + ADAM_EPS) + ADAM_WD * p_ref[...])
    mo_ref[...] = mn
    vo_ref[...] = vn


def _adamw(name, p, m, v, gst):
    rows, cols = p.shape
    ns = gst.shape[0]
    tm = _row_tile_for(rows, cols, budget=ADAM_TILE_BYTES)

    def body(p_ref, m_ref, v_ref, g_ref, *outs):
        _adam_update(_slot_sum([g_ref[j] for j in range(ns)]), p_ref, m_ref, v_ref, *outs)

    blk = pl.BlockSpec((tm, cols), lambda i: (i, 0))
    return pl.pallas_call(
        body, name=name, grid=(rows // tm,),
        in_specs=[blk, blk, blk, pl.BlockSpec((ns, tm, cols), lambda i: (0, i, 0))],
        out_specs=[blk] * 4, out_shape=[jax.ShapeDtypeStruct((rows, cols), F32)] * 4,
        compiler_params=_params(("parallel",)),
    )(p, m, v, gst)


def _adamw_halves(name, p, m, v, mine, theirs, place):
    rows, cols = p.shape
    half = rows // 2
    tm = _row_tile_for(half, cols, budget=ADAM_TILE_BYTES)
    nh = half // tm

    def body(pc_ref, p_ref, m_ref, v_ref, mine_ref, theirs_ref, *outs):
        g = jnp.where(pl.program_id(0) == pc_ref[0], mine_ref[...], theirs_ref[...])
        _adam_update(g, p_ref, m_ref, v_ref, *outs)

    blk = pl.BlockSpec((tm, cols), lambda h, i, pc: (h * nh + i, 0))
    hblk = pl.BlockSpec((tm, cols), lambda h, i, pc: (i, 0))
    return pl.pallas_call(
        body, name=name,
        grid_spec=pltpu.PrefetchScalarGridSpec(
            num_scalar_prefetch=1, grid=(2, nh), in_specs=[blk, blk, blk, hblk, hblk], out_specs=[blk] * 4),
        out_shape=[jax.ShapeDtypeStruct((rows, cols), F32)] * 4, compiler_params=_params(("parallel", "parallel")),
    )(place, p, m, v, mine, theirs)


def _pack(parts, width=LANE, mult=8):
    flat = jnp.concatenate([a.reshape(-1) for a in parts])
    n = flat.shape[0]
    per = width * mult
    total = -(-n // per) * per
    return jnp.pad(flat, (0, total - n)).reshape(total // width, width)


def _unpack(packed, shapes):
    flat = packed.reshape(-1)
    out, off = [], 0
    for s in shapes:
        n = 1
        for d in s:
            n *= d
        out.append(flat[off:off + n].reshape(s))
        off += n
    return out


_SMALL_SHARDED = ("hg_lb", "rw_mu", "rw_w0", "rw_w2", "rw_a0", "rw_a2")
_REPLICATED = ("c_ctx", "ada_b", "norm_g", "hg_norm_g", "rw_kk", "rw_ka", "rw_rk", "rw_gn_g", "rw_gn_b", "final_g")
_BIG = ("ada_w", "w_in", "w_hg_out", "w_rw_out", "w_out")
_WEIGHTS = ("c_ctx", "ada_w", "ada_b", "norm_g", "w_in", "hg_lb", "hg_norm_g", "rw_mu", "rw_w0", "rw_w2", "rw_a0", "rw_a2",
            "rw_kk", "rw_ka", "rw_rk", "rw_gn_g", "rw_gn_b", "w_hg_out", "w_rw_out", "w_out", "final_g")


def _join_shards(st):
    a = jnp.moveaxis(st, 0, -2)
    return a.reshape(a.shape[:-2] + (a.shape[-2] * a.shape[-1],))


def _split_shards(a):
    s = a.reshape(a.shape[:-1] + (N_SHARD, a.shape[-1] // N_SHARD))
    return jnp.moveaxis(s, -2, 0)


def kernel(x, c, ctx, c_ctx, ada_w, ada_b, norm_g, w_in, hg_lb, hg_norm_g, rw_mu, rw_w0, rw_w2, rw_a0, rw_a2, rw_kk, rw_ka, rw_rk, rw_gn_g, rw_gn_b, w_hg_out, w_rw_out, w_out, final_g, loss_target, m_c_ctx, m_ada_w, m_ada_b, m_norm_g, m_w_in, m_hg_lb, m_hg_norm_g, m_rw_mu, m_rw_w0, m_rw_w2, m_rw_a0, m_rw_a2, m_rw_kk, m_rw_ka, m_rw_rk, m_rw_gn_g, m_rw_gn_b, m_w_hg_out, m_w_rw_out, m_w_out, m_final_g, v_c_ctx, v_ada_w, v_ada_b, v_norm_g, v_w_in, v_hg_lb, v_hg_norm_g, v_rw_mu, v_rw_w0, v_rw_w2, v_rw_a0, v_rw_a2, v_rw_kk, v_rw_ka, v_rw_rk, v_rw_gn_g, v_rw_gn_b, v_w_hg_out, v_w_rw_out, v_w_out, v_final_g):
    w = dict(c_ctx=c_ctx, ada_w=ada_w, ada_b=ada_b, norm_g=norm_g, w_in=w_in, hg_lb=hg_lb, hg_norm_g=hg_norm_g, rw_mu=rw_mu,
             rw_w0=rw_w0, rw_w2=rw_w2, rw_a0=rw_a0, rw_a2=rw_a2, rw_kk=rw_kk, rw_ka=rw_ka, rw_rk=rw_rk, rw_gn_g=rw_gn_g,
             rw_gn_b=rw_gn_b, w_hg_out=w_hg_out, w_rw_out=w_rw_out, w_out=w_out, final_g=final_g)
    m = dict(c_ctx=m_c_ctx, ada_w=m_ada_w, ada_b=m_ada_b, norm_g=m_norm_g, w_in=m_w_in, hg_lb=m_hg_lb, hg_norm_g=m_hg_norm_g,
             rw_mu=m_rw_mu, rw_w0=m_rw_w0, rw_w2=m_rw_w2, rw_a0=m_rw_a0, rw_a2=m_rw_a2, rw_kk=m_rw_kk, rw_ka=m_rw_ka,
             rw_rk=m_rw_rk, rw_gn_g=m_rw_gn_g, rw_gn_b=m_rw_gn_b, w_hg_out=m_w_hg_out, w_rw_out=m_w_rw_out, w_out=m_w_out,
             final_g=m_final_g)
    v = dict(c_ctx=v_c_ctx, ada_w=v_ada_w, ada_b=v_ada_b, norm_g=v_norm_g, w_in=v_w_in, hg_lb=v_hg_lb, hg_norm_g=v_hg_norm_g,
             rw_mu=v_rw_mu, rw_w0=v_rw_w0, rw_w2=v_rw_w2, rw_a0=v_rw_a0, rw_a2=v_rw_a2, rw_kk=v_rw_kk, rw_ka=v_rw_ka,
             rw_rk=v_rw_rk, rw_gn_g=v_rw_gn_g, rw_gn_b=v_rw_gn_b, w_hg_out=v_w_hg_out, w_rw_out=v_w_rw_out, w_out=v_w_out,
             final_g=v_final_g)

    def mat(a):
        return a.reshape(a.shape[-2], a.shape[-1])

    def pack_small(d):
        return _pack([d[n] for n in _SMALL_SHARDED], mult=2 * ROW_ALIGN)

    my_core = lax.axis_index("c").astype(jnp.int32).reshape(1)
    my_chip = (2 * lax.axis_index("x") + lax.axis_index("y")).astype(jnp.int32).reshape(1)

    small_shapes = [w[n].shape for n in _SMALL_SHARDED]
    big_bf = [_cast_into_slot(f"to_bf16_{n}", mat(w[n]), my_chip) for n in _BIG]
    small_mine = pack_small(w)
    small_slots = lax.dynamic_update_slice(jnp.zeros((N_SHARD,) + small_mine.shape, F32), small_mine[None], (my_chip[0], 0, 0))
    gathered = _weights_gather("weights_gather", big_bf, [small_slots])
    ada_st, w_in_st, w_hg_st, w_rw_st, w_out_st, small_st = gathered
    full_small = {}
    per_chip = [_unpack(small_st[j], small_shapes) for j in range(N_SHARD)]
    for i, n in enumerate(_SMALL_SHARDED):
        full_small[n] = _join_shards(jnp.stack([per_chip[j][i] for j in range(N_SHARD)], axis=0))
    dm = x.shape[-1]
    w_out_full = w_out_st.reshape(dm, dm)

    loss_b, grad_x, g = _local_step(
        x[0], c, ctx[0], c_ctx, ada_st, ada_b, norm_g, w_in_st, full_small["hg_lb"], hg_norm_g, full_small["rw_mu"][0],
        full_small["rw_w0"][0], full_small["rw_w2"][0], full_small["rw_a0"][0], full_small["rw_a2"][0], rw_kk, rw_ka, rw_rk,
        rw_gn_g, rw_gn_b, w_hg_st, w_rw_st, w_out_full, final_g, loss_target[0], my_core)
    loss = lax.psum(loss_b[0, 0], ("x", "y", "c"))

    g_small = {"hg_lb": g["hg_lb"], "rw_mu": g["rw_mu"][None], "rw_w0": g["rw_w0"][None], "rw_w2": g["rw_w2"][None],
               "rw_a0": g["rw_a0"][None], "rw_a2": g["rw_a2"][None]}
    split = {n: _split_shards(g_small[n]) for n in _SMALL_SHARDED}
    small_parts = jnp.stack([pack_small({n: split[n][j] for n in _SMALL_SHARDED}) for j in range(N_SHARD)], axis=0)
    later = {"ada_w": g["ada_w"], "w_hg_out": g["w_hg_out"], "w_rw_out": g["w_rw_out"],
             "w_out": g["w_out"].reshape(N_SHARD, dm // N_SHARD, dm), "small": small_parts}
    later_sums = [_pair_exchange(f"grads_pair_sum_{n}", a, my_core, True, BF16) for n, a in later.items()]
    later_landed = dict(zip(later, zip(later_sums, _chip_scatter("grads_scatter", later_sums))))
    later_landed["w_in"] = g["w_in"]
    order = list(_BIG) + ["small"]
    mine = [_sum_landed(f"grads_sum_{n}", later_landed[n][1], later_landed[n][0], my_chip) for n in order]
    theirs = [_pair_exchange(f"grads_pair_swap{i}", a, my_core, False, F32) for i, a in enumerate(mine)]
    rep_shapes = [w[n].shape for n in _REPLICATED]
    rep_all = _gather_all("grads_replicated", _pack([g[n].reshape(w[n].shape) for n in _REPLICATED]))

    res = {}
    for i, n in enumerate(_BIG):
        outs = _adamw_halves(f"adamw_{n}", mat(w[n]), mat(m[n]), mat(v[n]), mine[i], theirs[i], my_core)
        res[n] = [o.reshape(w[n].shape) for o in outs]
    outs = _adamw_halves("adamw_small", small_mine, pack_small(m), pack_small(v), mine[len(_BIG)], theirs[len(_BIG)], my_core)
    for i, vals in enumerate(zip(*[_unpack(o, small_shapes) for o in outs])):
        res[_SMALL_SHARDED[i]] = list(vals)
    outs = _adamw("adamw_replicated", _pack([w[n] for n in _REPLICATED]), _pack([m[n] for n in _REPLICATED]),
                  _pack([v[n] for n in _REPLICATED]), rep_all)
    for i, vals in enumerate(zip(*[_unpack(o, rep_shapes) for o in outs])):
        res[_REPLICATED[i]] = list(vals)

    return (loss, grad_x[None], *[res[n][0] for n in _WEIGHTS], *[res[n][1] for n in _WEIGHTS],
            *[res[n][2] for n in _WEIGHTS], *[res[n][3] for n in _WEIGHTS])
```

```python
import functools

import jax
import jax.numpy as jnp
from jax import lax
from jax.experimental import pallas as pl
from jax.experimental.pallas import tpu as pltpu

HI = lax.Precision.HIGHEST
F32 = jnp.float32
BF16 = jnp.bfloat16

NORM_EPS = 1e-6
HG_HEAD = 128
RW_HEAD = 64
RW_LORA = 64
RW_GN_EPS = 64e-5
GRID_W = 64
SUB = 16
STEP = 64
N_SHARD = 4
N_DEV = 8
LANE = 128

ADAM_LR = 0.001
ADAM_B1 = 0.9
ADAM_B2 = 0.999
ADAM_EPS = 1e-08
ADAM_WD = 0.01
ADAM_STEP = 10

VMEM_LIMIT = 56 * 1024 * 1024


def _params(sem=None):
    return pltpu.CompilerParams(dimension_semantics=sem, vmem_limit_bytes=VMEM_LIMIT)


def _tile(n, cands):
    for c in cands:
        if n % c == 0:
            return c
    return n


def _iota2(n, m, d):
    return lax.broadcasted_iota(jnp.int32, (n, m), d)


def _before(n, rev, strict):
    t, s = _iota2(n, n, 0), _iota2(n, n, 1)
    if rev:
        return (s > t) if strict else (s >= t)
    return (s < t) if strict else (s <= t)


def _bdot(a, b, spec):
    return jnp.einsum(spec, a, b, precision=HI, preferred_element_type=F32)


def _sdot(a, b, spec):
    return jnp.einsum(spec, a, b, precision=lax.Precision.DEFAULT, preferred_element_type=F32)


def _hg_step(s0, qraw, iin, fin, lb2, rev):
    c, w = qraw.shape
    h = w // HG_HEAD
    nsub = c // SUB
    lb = jax.nn.sigmoid(lb2[0:1] - lb2[1:2])
    q = jax.nn.silu(qraw)
    fg = lb + (1.0 - lb) * jax.nn.sigmoid(fin)
    kk = 1.0 - fg
    g = jnp.log(fg)
    bcum = jnp.dot(_before(c, rev, False).astype(F32), g, precision=HI, preferred_element_type=F32)
    def heads(a):
        return jnp.swapaxes(a.reshape(a.shape[0], h, HG_HEAD), 0, 1)

    def unheads(a):
        return jnp.swapaxes(a, 0, 1).reshape(a.shape[1], w)

    blocks = [slice(j * SUB, (j + 1) * SUB) for j in range(nsub)]
    outs = []
    for sl in blocks:
        qs, ks, vs, bc = [a[sl].reshape(SUB, h, HG_HEAD) for a in (q, kk, iin, bcum)]
        o = jnp.zeros((SUB, h, HG_HEAD), F32)
        for si in range(SUB):
            after = slice(0, si + 1) if rev else slice(si, SUB)
            dec = jnp.exp(jnp.minimum(bc[after] - bc[si:si + 1], 0.0))
            a = jnp.sum(qs[after] * ks[si:si + 1] * dec, axis=-1, keepdims=True)
            term = a * vs[si:si + 1]
            n_rest = SUB - 1 - si if rev else si
            if n_rest:
                rest = jnp.zeros((n_rest, h, HG_HEAD), F32)
                term = jnp.concatenate([term, rest] if rev else [rest, term], axis=0)
            o = o + term
        outs.append(o.reshape(SUB, w))
    order = list(range(nsub - 1, -1, -1)) if rev else list(range(nsub))
    for pos in range(1, nsub):
        j, before = order[pos], order[:pos]
        first = (j + 1) * SUB - 1 if rev else j * SUB
        bstart = bcum[first:first + 1] - g[first:first + 1]
        qp = heads(q[blocks[j]] * jnp.exp(bcum[blocks[j]] - bstart))
        kp = heads(jnp.concatenate([kk[blocks[p]] * jnp.exp(bstart - bcum[blocks[p]]) for p in before], axis=0))
        vp = heads(jnp.concatenate([iin[blocks[p]] for p in before], axis=0))
        outs[j] = outs[j] + unheads(_sdot(_sdot(qp, kp, 'htk,hsk->hts'), vp, 'hts,hsv->htv'))
    o_state = unheads(_sdot(heads(q * jnp.exp(bcum)), s0, 'htk,hvk->htv'))
    last = 0 if rev else c - 1
    blast = bcum[last:last + 1]
    s_new = heads(jnp.exp(blast)) * s0 + _sdot(heads(iin), heads(kk * jnp.exp(blast - bcum)), 'hsv,hsk->hvk')
    return jnp.concatenate(outs, axis=0) + o_state, s_new


def _tri_solve(lmat, rhs, rev):
    hh, c, _ = lmat.shape
    nb = c // SUB
    diag = jnp.concatenate([lmat[:, i * SUB:(i + 1) * SUB, i * SUB:(i + 1) * SUB] for i in range(nb)], axis=0)
    dt = jnp.transpose(diag, (1, 2, 0))
    col = lax.broadcasted_iota(jnp.int32, (SUB, 1), 0)
    inv_rows = [None] * SUB
    order = list(range(SUB - 1, -1, -1)) if rev else list(range(SUB))
    for pos, t in enumerate(order):
        row = jnp.broadcast_to((col == t).astype(F32), (SUB, dt.shape[2]))
        for s in order[:pos]:
            row = row - dt[t, s:s + 1, :] * inv_rows[s]
        inv_rows[t] = row
    tinv = jnp.transpose(jnp.concatenate([r[None] for r in inv_rows], axis=0), (2, 0, 1))
    p = [None] * nb
    done = []
    for i in (range(nb - 1, -1, -1) if rev else range(nb)):
        r = rhs[:, i * SUB:(i + 1) * SUB]
        if done:
            lrow = jnp.concatenate([lmat[:, i * SUB:(i + 1) * SUB, m * SUB:(m + 1) * SUB] for m in done], axis=2)
            r = r - _sdot(lrow, jnp.concatenate([p[m] for m in done], axis=1), 'hts,hsv->htv')
        p[i] = _sdot(tinv[i * hh:(i + 1) * hh], r, 'hts,hsv->htv')
        done.append(i)
    return jnp.concatenate(p, axis=1)


def _rw_step(s0, r, k, v, wlo, alo, w0h, w2h, a0h, a2h, kkh, kah, rev):
    hh, c, _ = r.shape
    tl = jnp.broadcast_to(jnp.tanh(wlo)[None], (hh, c, wlo.shape[1]))
    al = jnp.broadcast_to(alo[None], (hh, c, alo.shape[1]))
    wlog = -jax.nn.softplus(-(w0h + _sdot(tl, w2h, 'hcl,hlj->hcj'))) - 0.5
    lw = -jnp.exp(wlog)
    a = jax.nn.sigmoid(a0h + _sdot(al, a2h, 'hcl,hlj->hcj'))
    kk = k * kkh
    kk = kk * lax.rsqrt(jnp.sum(kk * kk, axis=-1, keepdims=True) + 1e-12)
    kd = k * (1.0 + (a - 1.0) * kah)
    b = kk * a
    incl = jnp.broadcast_to(_before(c, rev, False).astype(F32)[None], (hh, c, c))
    cum = _bdot(incl, lw, 'hts,hsk->htk')
    ecum, encum = jnp.exp(cum), jnp.exp(-cum)
    alpha = jnp.exp(cum - lw) * kk
    beta = b * encum
    kappa = kd * encum
    rho = r * ecum
    m_lt = _before(c, rev, True)[None]
    m_le = _before(c, rev, False)[None]
    ar = jnp.concatenate([alpha, rho], axis=1)
    kb = jnp.concatenate([kappa, beta], axis=1)
    gram = _sdot(ar, kb, 'htk,hsk->hts')
    a_kap = jnp.where(m_lt, gram[:, :c, :c], 0.0)
    a_bet = jnp.where(m_lt, gram[:, :c, c:], 0.0)
    b_kap = jnp.where(m_le, gram[:, c:, :c], 0.0)
    b_bet = jnp.where(m_le, gram[:, c:, c:], 0.0)
    from_state = _sdot(ar, s0, 'htk,hvk->htv')
    p = _tri_solve(a_bet, from_state[:, :c] + _sdot(a_kap, v, 'hts,hsv->htv'), rev)
    vp = jnp.concatenate([v, -p], axis=1)
    y = from_state[:, c:] + _sdot(jnp.concatenate([b_kap, b_bet], axis=2), vp, 'hts,hsv->htv')
    stil = s0 + _sdot(vp, kb, 'hsv,hsk->hvk')
    last = 0 if rev else c - 1
    return y, stil * ecum[:, last:last + 1, :]


def _fn_h(s, norm_g, scale, shift):
    return s * lax.rsqrt(jnp.mean(s * s, axis=-1, keepdims=True) + NORM_EPS) * norm_g * (1.0 + scale) + shift


def _fn_hgpost(of, ob, z, g):
    tm, w = of.shape
    o = (of + ob).reshape(tm, w // HG_HEAD, HG_HEAD)
    o = o * lax.rsqrt(jnp.mean(o * o, axis=-1, keepdims=True) + NORM_EPS)
    return o.reshape(tm, w) * g * jax.nn.silu(z)


def _fn_rwpost(y0, y1, r, k, v, alo, z, a0, a2, k_a, r_k, gn_g, gn_b):
    tm, w = r.shape
    nh = w // RW_HEAD
    asum = 0.0
    for d in range(2):
        asum = asum + jax.nn.sigmoid(a0[d:d + 1] + jnp.dot(alo[:, d * RW_LORA:(d + 1) * RW_LORA], a2[d],
                                                           precision=HI, preferred_element_type=F32))
    k_sum = k * (2.0 + (asum - 2.0) * k_a)
    ys = (y0 + y1).reshape(tm, nh, RW_HEAD)
    mean = jnp.mean(ys, axis=-1, keepdims=True)
    var = jnp.mean(jnp.square(ys - mean), axis=-1, keepdims=True)
    y = ((ys - mean) * lax.rsqrt(var + RW_GN_EPS)).reshape(tm, w) * gn_g + gn_b
    bonus = jnp.sum((r * k_sum * r_k).reshape(tm, nh, RW_HEAD), axis=-1, keepdims=True) * v.reshape(tm, nh, RW_HEAD)
    return (y + bonus.reshape(tm, w)) * jax.nn.silu(z)


def _fn_merge(a, b, ghg, grw):
    return jax.nn.sigmoid(ghg) * a + jax.nn.sigmoid(grw) * b


def _fn_final(xs, o, gate, final_g, tgt):
    x2 = xs + gate * o
    y = x2 * lax.rsqrt(jnp.mean(x2 * x2, axis=-1, keepdims=True) + NORM_EPS) * final_g
    return 0.5 * jnp.sum(jnp.mean(jnp.square(y - tgt), axis=-1))


def _row_call(name, fn, n_tiles, tm, row_ins, full_ins, row_outs, acc_outs):
    n_ri, n_fi, n_ro = len(row_ins), len(full_ins), len(row_outs)

    def body(*refs):
        i = pl.program_id(0)
        rvals = [r[...] for r in refs[:n_ri]]
        fvals = [r[...] for r in refs[n_ri:n_ri + n_fi]]
        outs = refs[n_ri + n_fi:]
        ro, ao = fn(i, rvals, fvals)
        for ref, val in zip(outs[:n_ro], ro):
            ref[...] = val.astype(ref.dtype)
        for ref, val in zip(outs[n_ro:], ao):
            @pl.when(i == 0)
            def _(ref=ref):
                ref[...] = jnp.zeros_like(ref)
            ref[...] += val.astype(ref.dtype)

    def rspec(width, cb, off):
        return pl.BlockSpec((tm, width), lambda i: (jnp.maximum(i - off, 0), cb))

    def fspec(shape):
        nd = len(shape)
        return pl.BlockSpec(shape, lambda i: (0,) * nd)

    in_specs = [rspec(w, cb, off) for (_, cb, w, off) in row_ins] + [fspec(a.shape) for a in full_ins]
    out_specs = [rspec(w, 0, off) for (_, w, _, off) in row_outs] + [fspec(s) for (s, _) in acc_outs]
    out_shape = [jax.ShapeDtypeStruct((rows, w), dt) for (rows, w, dt, _) in row_outs] + \
                [jax.ShapeDtypeStruct(s, dt) for (s, dt) in acc_outs]
    res = pl.pallas_call(
        body, name=name, grid=(n_tiles,), in_specs=in_specs, out_specs=out_specs, out_shape=out_shape,
        compiler_params=_params(("arbitrary",)),
    )(*[a for (a, _, _, _) in row_ins], *full_ins)
    return list(res)


def _mm(name, a, b, m, n, k_steps, tm, tn, a_block, a_map, b_block, b_map, o_shape, o_block, o_map,
        contract, out_dtype=F32, scatter=()):
    ns = len(scatter)
    grid = (m // tm, n // tn, k_steps)

    def body(*refs):
        a_ref, b_ref, o_ref, acc_ref = refs[0], refs[1], refs[2 + ns], refs[3 + 2 * ns]
        kk = pl.program_id(2)
        if ns:
            sc_refs = (refs[2:2 + ns], refs[3 + ns:3 + 2 * ns]) + tuple(refs[4 + 2 * ns:])
            at = (pl.program_id(0) * grid[1] + pl.program_id(1)) * grid[2] + kk
            pl.when(at == 0)(lambda: _scatter_start(scatter, *sc_refs))

        @pl.when(kk == 0)
        def _():
            acc_ref[...] = jnp.zeros_like(acc_ref)

        acc_ref[...] += lax.dot_general(a_ref[...].astype(BF16), b_ref[...].astype(BF16),
                                        (contract, ((), ())), preferred_element_type=F32)

        @pl.when(kk == k_steps - 1)
        def _():
            o_ref[...] = acc_ref[...].astype(o_ref.dtype)

        if ns:
            pl.when(at == grid[0] * grid[1] * grid[2] - 1)(lambda: _scatter_wait(scatter, *sc_refs))

    hbm = pl.BlockSpec(memory_space=pl.ANY)
    sems = [pltpu.SemaphoreType.DMA((ns, _PEER_CHIPS))] * 2 if ns else []
    res = pl.pallas_call(
        body, name=name, grid=grid,
        in_specs=[pl.BlockSpec(a_block, a_map), pl.BlockSpec(b_block, b_map)] + [hbm] * ns,
        out_specs=[pl.BlockSpec(o_block, o_map)] + [hbm] * ns,
        out_shape=[jax.ShapeDtypeStruct(o_shape, out_dtype)] + [jax.ShapeDtypeStruct(s.shape, s.dtype) for s in scatter],
        scratch_shapes=[pltpu.VMEM((tm, tn), F32)] + sems,
        compiler_params=_params(("arbitrary",) * 3 if ns else ("parallel", "parallel", "arbitrary")),
    )(a, b, *scatter)
    return (res[0], list(res[1:])) if ns else res[0]


_TM = (768, 512, 256, 128, 64, 32, 16, 8)
_TN = (512, 256, 128)
_TK = (1024, 768, 512, 256, 128)
_TK_WIDE = (768, 512, 256, 128)
WIDE_OUT_BYTES = 32 << 20


def _tm_wide(m, ns):
    for tm in _TM:
        if m % tm == 0 and 3 * 4 * tm * ns <= WIDE_OUT_BYTES:
            return tm
    return m


def _mm_nn(name, a, b, out_dtype=F32):
    m, k = a.shape
    n = b.shape[1]
    tm, tn, tk = _tile(m, _TM), _tile(n, _TN), _tile(k, _TK)
    return _mm(name, a, b, m, n, k // tk, tm, tn, (tm, tk), lambda i, j, s: (i, s), (tk, tn), lambda i, j, s: (s, j),
               (m, n), (tm, tn), lambda i, j, s: (i, j), ((1,), (0,)), out_dtype)


def _mm_nt(name, a, b, out_dtype=F32):
    m, k = a.shape
    n = b.shape[0]
    tm, tn, tk = _tile(m, _TM), _tile(n, _TN), _tile(k, _TK)
    return _mm(name, a, b, m, n, k // tk, tm, tn, (tm, tk), lambda i, j, s: (i, s), (tn, tk), lambda i, j, s: (j, s),
               (m, n), (tm, tn), lambda i, j, s: (i, j), ((1,), (1,)), out_dtype)


def _mm_tn(name, a, b, out_dtype=F32):
    k, m = a.shape
    n = b.shape[1]
    tm, tn, tk = _tile(m, _TM), _tile(n, _TN), _tile(k, _TK)
    return _mm(name, a, b, m, n, k // tk, tm, tn, (tk, tm), lambda i, j, s: (s, i), (tk, tn), lambda i, j, s: (s, j),
               (m, n), (tm, tn), lambda i, j, s: (i, j), ((0,), (0,)), out_dtype)


def _mm_n_st(name, a, bst, out_dtype=F32):
    m, k = a.shape
    ns_, _, ns = bst.shape
    tm, tk = _tm_wide(m, ns), _tile(k, (512, 256, 128))
    return _mm(name, a, bst, m, ns_ * ns, k // tk, tm, ns,
               (tm, tk), lambda i, j, s: (i, s), (None, tk, ns), lambda i, j, s: (j, s, 0),
               (ns_, m, ns), (None, tm, ns), lambda i, j, s: (j, i, 0), ((1,), (0,)), out_dtype)


def _mm_st_t(name, ast, bst, out_dtype=F32, scatter=()):
    ns_, m, ns = ast.shape
    n = bst.shape[1]
    tm, tn = _tile(m, _TM), _tile(n, _TN)
    return _mm(name, ast, bst, m, n, ns_, tm, tn,
               (None, tm, ns), lambda i, j, s: (s, i, 0), (None, tn, ns), lambda i, j, s: (s, j, 0),
               (m, n), (tm, tn), lambda i, j, s: (i, j), ((1,), (1,)), out_dtype, scatter)


def _mm_t_st(name, a, bst, out_dtype=F32, scatter=()):
    k, m = a.shape
    ns_, _, ns = bst.shape
    tm, tk = _tile(m, _TN), _tile(k, _TK_WIDE)
    return _mm(name, a, bst, m, ns_ * ns, k // tk, tm, ns,
               (tk, tm), lambda i, j, s: (s, i), (None, tk, ns), lambda i, j, s: (j, s, 0),
               (ns_, m, ns), (None, tm, ns), lambda i, j, s: (j, i, 0), ((0,), (0,)), out_dtype, scatter)


def _scan_order(j, n_ctx, n_all, rev):
    if not rev:
        return j
    return jnp.where(j < n_ctx, n_ctx - 1 - j, n_all - 1 - (j - n_ctx))


def _hg_scan_fwd(name, p_hg, lb2, d, n_ctx):
    t, w5 = p_hg.shape
    w = w5 // 5
    h = w // HG_HEAD
    n = t // STEP
    rev = d == 1

    def body(q_ref, i_ref, f_ref, lb_ref, o_ref, st_ref, s_ref):
        j = pl.program_id(0)

        @pl.when(j == 0)
        def _():
            s_ref[...] = jnp.zeros_like(s_ref)

        s0 = s_ref[...]
        st_ref[...] = s0
        o, s1 = _hg_step(s0, q_ref[...], i_ref[...], f_ref[...], lb_ref[...], rev)
        o_ref[...] = o
        s_ref[...] = s1

    def rows(cb):
        return pl.BlockSpec((STEP, w), lambda j: (_scan_order(j, n_ctx, n, rev), cb))

    return pl.pallas_call(
        body, name=name, grid=(n,),
        in_specs=[rows(0), rows(1), rows(2 + d), pl.BlockSpec((2, w), lambda j: (0, 0))],
        out_specs=[rows(0), pl.BlockSpec((None, h, HG_HEAD, HG_HEAD), lambda j: (j, 0, 0, 0))],
        out_shape=[jax.ShapeDtypeStruct((t, w), F32), jax.ShapeDtypeStruct((n, h, HG_HEAD, HG_HEAD), F32)],
        scratch_shapes=[pltpu.VMEM((h, HG_HEAD, HG_HEAD), F32)],
        compiler_params=_params(("arbitrary",)),
    )(p_hg, p_hg, p_hg, lb2)


def _hg_scan_bwd(name, p_hg, lb2, states, do, d, n_ctx):
    t, w5 = p_hg.shape
    w = w5 // 5
    h = w // HG_HEAD
    n = t // STEP
    rev = d == 1

    def body(q_ref, i_ref, f_ref, lb_ref, st_ref, do_ref, dq_ref, di_ref, df_ref, dlb_ref, ds_ref):
        step = pl.program_id(0)

        @pl.when(step == 0)
        def _():
            ds_ref[...] = jnp.zeros_like(ds_ref)
            dlb_ref[...] = jnp.zeros_like(dlb_ref)

        _, vjp = jax.vjp(lambda s0, q, i, f, lb: _hg_step(s0, q, i, f, lb, rev),
                         st_ref[...], q_ref[...], i_ref[...], f_ref[...], lb_ref[...])
        ds0, dq, di, df, dlb = vjp((do_ref[...], ds_ref[...]))
        dq_ref[...] = dq
        di_ref[...] = di
        df_ref[...] = df
        dlb_ref[...] += dlb
        ds_ref[...] = ds0

    def rows(cb):
        return pl.BlockSpec((STEP, w), lambda s: (_scan_order(n - 1 - s, n_ctx, n, rev), cb))

    return pl.pallas_call(
        body, name=name, grid=(n,),
        in_specs=[rows(0), rows(1), rows(2 + d), pl.BlockSpec((2, w), lambda s: (0, 0)),
                  pl.BlockSpec((None, h, HG_HEAD, HG_HEAD), lambda s: (n - 1 - s, 0, 0, 0)), rows(0)],
        out_specs=[rows(0), rows(0), rows(0), pl.BlockSpec((2, w), lambda s: (0, 0))],
        out_shape=[jax.ShapeDtypeStruct((t, w), F32)] * 3 + [jax.ShapeDtypeStruct((2, w), F32)],
        scratch_shapes=[pltpu.VMEM((h, HG_HEAD, HG_HEAD), F32)],
        compiler_params=_params(("arbitrary",)),
    )(p_hg, p_hg, p_hg, lb2, states, do)


def _to_heads(a, nh):
    return jnp.stack([a[:, i * RW_HEAD:(i + 1) * RW_HEAD] for i in range(nh)], axis=0)


def _from_heads(a):
    return jnp.concatenate([a[i] for i in range(a.shape[0])], axis=-1)


def _rw_scan_fwd(name, sh, hp, d, n_ctx):
    t = sh.shape[0]
    w = (sh.shape[1] - 4 * RW_LORA) // 3
    nh = w // RW_HEAD
    n = t // STEP
    rev = d == 1
    lo = 3 * w // LANE

    def body(r_ref, k_ref, v_ref, wl_ref, al_ref, w0_ref, w2_ref, a0_ref, a2_ref, kk_ref, ka_ref,
             y_ref, st_ref, s_ref):
        j = pl.program_id(0)

        @pl.when(j == 0)
        def _():
            s_ref[...] = jnp.zeros_like(s_ref)

        s0 = s_ref[...]
        st_ref[...] = s0
        wl = wl_ref[...][:, d * RW_LORA:(d + 1) * RW_LORA]
        al = al_ref[...][:, d * RW_LORA:(d + 1) * RW_LORA]
        y, s1 = _rw_step(s0, _to_heads(r_ref[...], nh), _to_heads(k_ref[...], nh), _to_heads(v_ref[...], nh), wl, al,
                         w0_ref[...], w2_ref[...], a0_ref[...], a2_ref[...], kk_ref[...], ka_ref[...], rev)
        y_ref[...] = _from_heads(y)
        s_ref[...] = s1

    def rows(cb, width=w):
        return pl.BlockSpec((STEP, width), lambda j: (_scan_order(j, n_ctx, n, rev), cb))

    def whole(a):
        nd = a.ndim
        return pl.BlockSpec(a.shape, lambda j: (0,) * nd)

    return pl.pallas_call(
        body, name=name, grid=(n,),
        in_specs=[rows(0), rows(1), rows(2), rows(lo, LANE), rows(lo + 1, LANE)] + [whole(a) for a in hp],
        out_specs=[rows(0), pl.BlockSpec((None, nh, RW_HEAD, RW_HEAD), lambda j: (j, 0, 0, 0))],
        out_shape=[jax.ShapeDtypeStruct((t, w), F32), jax.ShapeDtypeStruct((n, nh, RW_HEAD, RW_HEAD), F32)],
        scratch_shapes=[pltpu.VMEM((nh, RW_HEAD, RW_HEAD), F32)],
        compiler_params=_params(("arbitrary",)),
    )(sh, sh, sh, sh, sh, *hp)


def _rw_scan_bwd(name, sh, hp, states, dy, d, n_ctx):
    t = sh.shape[0]
    w = (sh.shape[1] - 4 * RW_LORA) // 3
    nh = w // RW_HEAD
    n = t // STEP
    rev = d == 1
    lo = 3 * w // LANE

    def body(r_ref, k_ref, v_ref, wl_ref, al_ref, w0_ref, w2_ref, a0_ref, a2_ref, kk_ref, ka_ref, st_ref, dy_ref,
             dsh_ref, dw0_ref, dw2_ref, da0_ref, da2_ref, dkk_ref, dka_ref, ds_ref):
        step = pl.program_id(0)
        pouts = (dw0_ref, dw2_ref, da0_ref, da2_ref, dkk_ref, dka_ref)

        @pl.when(step == 0)
        def _():
            ds_ref[...] = jnp.zeros_like(ds_ref)
            for ref in pouts:
                ref[...] = jnp.zeros_like(ref)

        wl = wl_ref[...][:, d * RW_LORA:(d + 1) * RW_LORA]
        al = al_ref[...][:, d * RW_LORA:(d + 1) * RW_LORA]
        _, vjp = jax.vjp(functools.partial(_rw_step, rev=rev),
                         st_ref[...], _to_heads(r_ref[...], nh), _to_heads(k_ref[...], nh), _to_heads(v_ref[...], nh),
                         wl, al, w0_ref[...], w2_ref[...], a0_ref[...], a2_ref[...], kk_ref[...], ka_ref[...])
        g = vjp((_to_heads(dy_ref[...], nh), ds_ref[...]))
        ds_ref[...] = g[0]
        zero = jnp.zeros_like(g[4])
        lora = [zero] * 4
        lora[d], lora[2 + d] = g[4], g[5]
        dsh_ref[...] = jnp.concatenate([_from_heads(g[1]), _from_heads(g[2]), _from_heads(g[3])] + lora, axis=-1)
        for ref, val in zip(pouts, g[6:]):
            ref[...] += val

    def rows(cb, width=w):
        return pl.BlockSpec((STEP, width), lambda s: (_scan_order(n - 1 - s, n_ctx, n, rev), cb))

    def whole(a):
        nd = a.ndim
        return pl.BlockSpec(a.shape, lambda s: (0,) * nd)

    return pl.pallas_call(
        body, name=name, grid=(n,),
        in_specs=[rows(0), rows(1), rows(2), rows(lo, LANE), rows(lo + 1, LANE)] + [whole(a) for a in hp] +
                 [pl.BlockSpec((None, nh, RW_HEAD, RW_HEAD), lambda s: (n - 1 - s, 0, 0, 0)), rows(0)],
        out_specs=[rows(0, sh.shape[1])] + [whole(a) for a in hp],
        out_shape=[jax.ShapeDtypeStruct(sh.shape, F32)] + [jax.ShapeDtypeStruct(a.shape, F32) for a in hp],
        scratch_shapes=[pltpu.VMEM((nh, RW_HEAD, RW_HEAD), F32)],
        compiler_params=_params(("arbitrary",)),
    )(sh, sh, sh, sh, sh, *hp, states, dy)


def _rw_scan_bwd_both(name, sh, hps, states, dy, n_ctx):
    t = sh.shape[0]
    w = (sh.shape[1] - 4 * RW_LORA) // 3
    nh = w // RW_HEAD
    n = t // STEP
    lo = 3 * w // LANE
    n_in, n_p = 13, 6

    def body(*refs):
        step = pl.program_id(0)
        ins = [refs[d * n_in:(d + 1) * n_in] for d in range(2)]
        outs = [refs[2 * n_in + d * (1 + n_p):2 * n_in + (d + 1) * (1 + n_p)] for d in range(2)]
        ds_refs = refs[2 * n_in + 2 * (1 + n_p):]

        @pl.when(step == 0)
        def _():
            for d in range(2):
                ds_refs[d][...] = jnp.zeros_like(ds_refs[d])
                for ref in outs[d][1:]:
                    ref[...] = jnp.zeros_like(ref)

        for d in range(2):
            r_ref, k_ref, v_ref, wl_ref, al_ref = ins[d][:5]
            hp_refs, st_ref, dy_ref = ins[d][5:11], ins[d][11], ins[d][12]
            wl = wl_ref[...][:, d * RW_LORA:(d + 1) * RW_LORA]
            al = al_ref[...][:, d * RW_LORA:(d + 1) * RW_LORA]
            _, vjp = jax.vjp(functools.partial(_rw_step, rev=d == 1),
                             st_ref[...], _to_heads(r_ref[...], nh), _to_heads(k_ref[...], nh), _to_heads(v_ref[...], nh),
                             wl, al, *[p[...] for p in hp_refs])
            g = vjp((_to_heads(dy_ref[...], nh), ds_refs[d][...]))
            ds_refs[d][...] = g[0]
            zero = jnp.zeros_like(g[4])
            lora = [zero] * 4
            lora[d], lora[2 + d] = g[4], g[5]
            outs[d][0][...] = jnp.concatenate([_from_heads(g[1]), _from_heads(g[2]), _from_heads(g[3])] + lora, axis=-1)
            for ref, val in zip(outs[d][1:], g[6:]):
                ref[...] += val

    def rows(d, cb, width=w):
        return pl.BlockSpec((STEP, width), lambda s: (_scan_order(n - 1 - s, n_ctx, n, d == 1), cb))

    def whole(a):
        nd = a.ndim
        return pl.BlockSpec(a.shape, lambda s: (0,) * nd)

    in_specs, operands, out_specs, out_shape = [], [], [], []
    for d in range(2):
        in_specs += [rows(d, 0), rows(d, 1), rows(d, 2), rows(d, lo, LANE), rows(d, lo + 1, LANE)]
        in_specs += [whole(a) for a in hps[d]]
        in_specs += [pl.BlockSpec((None, nh, RW_HEAD, RW_HEAD), lambda s: (n - 1 - s, 0, 0, 0)), rows(d, 0)]
        operands += [sh] * 5 + list(hps[d]) + [states[d], dy]
        out_specs += [rows(d, 0, sh.shape[1])] + [whole(a) for a in hps[d]]
        out_shape += [jax.ShapeDtypeStruct(sh.shape, F32)] + [jax.ShapeDtypeStruct(a.shape, F32) for a in hps[d]]
    res = pl.pallas_call(
        body, name=name, grid=(n,), in_specs=in_specs, out_specs=out_specs, out_shape=out_shape,
        scratch_shapes=[pltpu.VMEM((nh, RW_HEAD, RW_HEAD), F32)] * 2, compiler_params=_params(("arbitrary",)),
    )(*operands)
    return [res[0], res[1 + n_p]], [res[1:1 + n_p], res[2 + n_p:]]


def _shift_masks(t, n_ctx_rows):
    row = lax.broadcasted_iota(jnp.int32, (t, 1), 0)
    isx = row >= n_ctx_rows
    pos = jnp.where(isx, row - n_ctx_rows, row)
    col = jnp.where(isx, jnp.bitwise_and(pos, GRID_W - 1), pos)
    ncol = jnp.where(isx, GRID_W, n_ctx_rows)
    n_x = t - n_ctx_rows
    ml = col != 0
    mr = col != ncol - 1
    mu = isx & (pos >= GRID_W)
    md = isx & (pos < n_x - GRID_W)
    return ml, mr, mu, md, isx


def _shift_fwd(name, p, mu, n_ctx_rows):
    t, c = p.shape
    cw = LANE

    def body(p_ref, mu_ref, o_ref):
        x = p_ref[...]
        m = mu_ref[...]
        ml, mr, mup, mdn, isx = _shift_masks(t, n_ctx_rows)
        left = jnp.where(ml, pltpu.roll(x, 1, 0), 0.0)
        right = jnp.where(mr, pltpu.roll(x, t - 1, 0), 0.0)
        up = jnp.where(mup, pltpu.roll(x, GRID_W, 0), 0.0)
        down = jnp.where(mdn, pltpu.roll(x, t - GRID_W, 0), 0.0)
        out = x + m[0:1] * (left - x) + m[1:2] * (right - x)
        vert = m[2:3] * (up - x) + m[3:4] * (down - x)
        o_ref[...] = out + jnp.where(isx, vert, 0.0)

    return pl.pallas_call(
        body, name=name, grid=(c // cw,),
        in_specs=[pl.BlockSpec((t, cw), lambda j: (0, j)), pl.BlockSpec((4, cw), lambda j: (0, j))],
        out_specs=pl.BlockSpec((t, cw), lambda j: (0, j)),
        out_shape=jax.ShapeDtypeStruct((t, c), F32),
        compiler_params=_params(("parallel",)),
    )(p, mu)


def _shift_bwd(name, p, mu, dparts, n_ctx_rows):
    t, c = p.shape
    cw = LANE
    npart = len(dparts)

    def body(*refs):
        p_ref, mu_ref = refs[0], refs[1]
        dp_ref, dmu_ref = refs[2 + npart], refs[3 + npart]
        x = p_ref[...]
        m = mu_ref[...]
        g = refs[2][...]
        for r in refs[3:2 + npart]:
            g = g + r[...]
        ml, mr, mup, mdn, isx = _shift_masks(t, n_ctx_rows)
        left = jnp.where(ml, pltpu.roll(x, 1, 0), 0.0)
        right = jnp.where(mr, pltpu.roll(x, t - 1, 0), 0.0)
        up = jnp.where(mup, pltpu.roll(x, GRID_W, 0), 0.0)
        down = jnp.where(mdn, pltpu.roll(x, t - GRID_W, 0), 0.0)
        gx = jnp.where(isx, g, 0.0)
        dmu_ref[...] = jnp.concatenate([
            jnp.sum(g * (left - x), axis=0, keepdims=True), jnp.sum(g * (right - x), axis=0, keepdims=True),
            jnp.sum(gx * (up - x), axis=0, keepdims=True), jnp.sum(gx * (down - x), axis=0, keepdims=True)], axis=0)
        coef = 1.0 - m[0:1] - m[1:2] - jnp.where(isx, m[2:3] + m[3:4], 0.0)
        dp = coef * g
        dp = dp + m[0:1] * pltpu.roll(jnp.where(ml, g, 0.0), t - 1, 0)
        dp = dp + m[1:2] * pltpu.roll(jnp.where(mr, g, 0.0), 1, 0)
        dp = dp + m[2:3] * pltpu.roll(jnp.where(mup, g, 0.0), t - GRID_W, 0)
        dp = dp + m[3:4] * pltpu.roll(jnp.where(mdn, g, 0.0), GRID_W, 0)
        dp_ref[...] = dp

    col = pl.BlockSpec((t, cw), lambda j: (0, j))
    par = pl.BlockSpec((4, cw), lambda j: (0, j))
    return pl.pallas_call(
        body, name=name, grid=(c // cw,),
        in_specs=[col, par] + [col] * npart,
        out_specs=[col, par],
        out_shape=[jax.ShapeDtypeStruct((t, c), F32), jax.ShapeDtypeStruct((4, c), F32)],
        compiler_params=_params(("parallel",)),
    )(p, mu, *dparts)


def _local_step(x, c, ctx, c_ctx, ada_st, ada_b, norm_g, w_in_st, hg_lb, hg_norm_g, rw_mu, rw_w0, rw_w2, rw_a0, rw_a2,
                rw_kk, rw_ka, rw_rk, rw_gn_g, rw_gn_b, w_hg_st, w_rw_st, w_out, final_g, tgt, my_core):
    seq, dm = x.shape
    n_ctx_rows = ctx.shape[0]
    t = seq + n_ctx_rows
    hw = hg_norm_g.shape[-1]
    rw = rw_kk.shape[-1]
    nh_rw = rw // RW_HEAD
    n_ctx = n_ctx_rows // STEP
    tm = _tile(n_ctx_rows, (256, 128, 64))
    nt = t // tm
    nct = n_ctx_rows // tm
    n_sh_cols = 3 * rw + 4 * RW_LORA

    xs = jnp.concatenate([ctx, x], axis=0)
    cond = jnp.concatenate([c.reshape(1, dm), c_ctx.reshape(1, dm), jnp.zeros((6, dm), F32)], axis=0)
    final_g2 = final_g.reshape(1, dm)

    def unstack(a_st):
        return jnp.swapaxes(a_st, 0, 1).reshape(a_st.shape[1], -1)

    def restack(a, ns=N_SHARD):
        return jnp.swapaxes(a.reshape(a.shape[0], ns, -1), 0, 1)

    (sc,) = _row_call("cond_silu", lambda i, r, f: ([jax.nn.silu(r[0])], []), 1, 8, [(cond, 0, dm, 0)], [],
                      [(8, dm, F32, 0)], [])
    mod_st = _mm_n_st("mod_mm", sc, ada_st)
    def add(name, terms, shape):
        flat2 = [a.reshape(-1, a.shape[-1]) for a in terms]
        return _rowwise(name, lambda *v: _slot_sum(list(v)), flat2, F32).reshape(shape)

    mod = add("mod_bias", [unstack(mod_st), jnp.broadcast_to(ada_b, (8, 3 * dm))], (8, 3 * dm))
    mod3 = mod.reshape(8, 3, dm)

    def pick(i, m3):
        r = jnp.where(i < nct, m3[1], m3[0])
        return r[0:1], r[1:2]

    def h_fn(i, r, f):
        shift, scale = pick(i, f[1])
        return [_fn_h(r[0], f[0], scale, shift)], []

    (h,) = _row_call("h_fwd", h_fn, nt, tm, [(xs, 0, dm, 0)], [norm_g, mod3], [(t, dm, BF16, 0)], [])
    proj = unstack(_mm_n_st("proj_mm", h, w_in_st))
    p_hg = proj[:, :5 * hw]
    p_rs = proj[:, 5 * hw:5 * hw + n_sh_cols]
    p_zr = proj[:, 5 * hw + n_sh_cols:5 * hw + n_sh_cols + rw]
    p_gt = proj[:, 5 * hw + n_sh_cols + rw:]

    o_hg, st_hg = [], []
    for d in range(2):
        o, st = _hg_scan_fwd(f"hg_scan_fwd{d}", p_hg, hg_lb[d], d, n_ctx)
        o_hg.append(o)
        st_hg.append(st)

    def hgpost_fn(i, r, f):
        return [_fn_hgpost(r[0], r[1], r[2], f[0])], []

    hg_in = [(o_hg[0], 0, hw, 0), (o_hg[1], 0, hw, 0), (p_hg, 4, hw, 0)]
    (y_hg,) = _row_call("hg_post", hgpost_fn, nt, tm, hg_in, [hg_norm_g], [(t, hw, BF16, 0)], [])

    sh = _shift_fwd("rw_shift", p_rs, rw_mu, n_ctx_rows)
    hps = []
    for d in range(2):
        hps.append([rw_w0[d].reshape(nh_rw, 1, RW_HEAD), jnp.swapaxes(rw_w2[d].reshape(RW_LORA, nh_rw, RW_HEAD), 0, 1),
                    rw_a0[d].reshape(nh_rw, 1, RW_HEAD), jnp.swapaxes(rw_a2[d].reshape(RW_LORA, nh_rw, RW_HEAD), 0, 1),
                    rw_kk.reshape(nh_rw, 1, RW_HEAD), rw_ka.reshape(nh_rw, 1, RW_HEAD)])
    y_rw_d, st_rw = [], []
    for d in range(2):
        y, st = _rw_scan_fwd(f"rw_scan_fwd{d}", sh, hps[d], d, n_ctx)
        y_rw_d.append(y)
        st_rw.append(st)

    rw_full = [rw_a0, rw_a2, rw_ka, rw_rk, rw_gn_g, rw_gn_b]
    lo = 3 * rw // LANE
    rw_in = [(y_rw_d[0], 0, rw, 0), (y_rw_d[1], 0, rw, 0), (sh, 0, rw, 0), (sh, 1, rw, 0), (sh, 2, rw, 0),
             (sh, lo + 1, LANE, 0), (p_zr, 0, rw, 0)]

    def rwpost_fn(i, r, f):
        return [_fn_rwpost(*r, *f)], []

    (y_rw,) = _row_call("rw_post", rwpost_fn, nt, tm, rw_in, rw_full, [(t, rw, BF16, 0)], [])

    a_hg = unstack(_mm_n_st("hg_out_mm", y_hg, w_hg_st))
    a_rw = unstack(_mm_n_st("rw_out_mm", y_rw, w_rw_st))
    mg_in = [(a_hg, 0, dm, 0), (a_rw, 0, dm, 0), (p_gt, 0, dm, 0), (p_gt, 1, dm, 0)]
    (merged,) = _row_call("merge", lambda i, r, f: ([_fn_merge(*r)], []), nt, tm, mg_in, [], [(t, dm, BF16, 0)], [])
    o_out = _mm_nn("out_mm", merged, w_out)

    def final_fn(i, r, f):
        gate = f[0][0][2:3]
        loss, vjp = jax.vjp(_fn_final, r[0], r[1], gate, f[1], r[2])
        dx, do, dgate, dfg, _ = vjp(jnp.ones((), F32))
        live = i >= nct
        zero = lambda a: jnp.where(live, a, 0.0)
        dmod = jnp.concatenate([jnp.concatenate([jnp.zeros((1, 2 * dm), F32), zero(dgate)], axis=1),
                                jnp.zeros((7, 3 * dm), F32)], axis=0)
        return [zero(dx), zero(do)], [jnp.broadcast_to(zero(loss), (8, LANE)), dmod, zero(dfg)]

    fin_in = [(xs, 0, dm, 0), (o_out, 0, dm, 0), (tgt, 0, dm, nct)]
    dx_res, d_o, loss_acc, dmod_gate, d_final_g = _row_call(
        "final", final_fn, nt, tm, fin_in, [mod3, final_g2], [(t, dm, F32, 0), (t, dm, BF16, 0)],
        [((8, LANE), F32), ((8, 3 * dm), F32), ((1, dm), F32)])

    g_w_out = _mm_tn("d_w_out", merged, d_o)
    d_merged = _mm_nt("d_merged", d_o, w_out)

    def merge_bwd(i, r, f):
        _, vjp = jax.vjp(_fn_merge, r[0], r[1], r[2], r[3])
        da, db, dgh, dgr = vjp(r[4])
        return [da, db, jnp.concatenate([dgh, dgr], axis=1)], []

    da_hg, da_rw, dp_gt = _row_call("merge_bwd", merge_bwd, nt, tm, mg_in + [(d_merged, 0, dm, 0)], [],
                                    [(t, dm, BF16, 0), (t, dm, BF16, 0), (t, 2 * dm, F32, 0)], [])
    g_w_hg_st = _mm_t_st("d_w_hg", y_hg, restack(da_hg))
    g_w_rw_st = _mm_t_st("d_w_rw", y_rw, restack(da_rw))
    dy_hg = _mm_st_t("d_y_hg", restack(da_hg), w_hg_st)
    dy_rw = _mm_st_t("d_y_rw", restack(da_rw), w_rw_st)

    def hgpost_bwd(i, r, f):
        _, vjp = jax.vjp(_fn_hgpost, r[0], r[1], r[2], f[0])
        dof, _, dz, dg = vjp(r[3])
        return [dof, dz], [dg]

    do_hg, dz_hg, g_hg_norm = _row_call("hg_post_bwd", hgpost_bwd, nt, tm, hg_in + [(dy_hg, 0, hw, 0)], [hg_norm_g],
                                        [(t, hw, F32, 0), (t, hw, F32, 0)], [((1, hw), F32)])
    dqs, dis, dfs, g_lb = [], [], [], []
    for d in range(2):
        dq, di, df, dlb = _hg_scan_bwd(f"hg_scan_bwd{d}", p_hg, hg_lb[d], st_hg[d], do_hg, d, n_ctx)
        dqs.append(dq)
        dis.append(di)
        dfs.append(df)
        g_lb.append(dlb)
    (dqi,) = _row_call("hg_dsum", lambda i, r, f: ([jnp.concatenate([r[0] + r[1], r[2] + r[3]], axis=1)], []), nt, tm,
                       [(dqs[0], 0, hw, 0), (dqs[1], 0, hw, 0), (dis[0], 0, hw, 0), (dis[1], 0, hw, 0)], [],
                       [(t, 2 * hw, F32, 0)], [])
    g_hg_lb = jnp.stack(g_lb, axis=0)

    def rwpost_bwd(i, r, f):
        _, vjp = jax.vjp(_fn_rwpost, *r[:7], *f)
        g = vjp(r[7])
        zl = jnp.zeros((g[5].shape[0], 2 * RW_LORA), F32)
        return [g[0], jnp.concatenate([g[2], g[3], g[4], zl, g[5]], axis=1), g[6]], list(g[7:])

    dy_sum, dsh_p, dz_rw, g_a0_p, g_a2_p, g_ka_p, g_rk, g_gn_g, g_gn_b = _row_call(
        "rw_post_bwd", rwpost_bwd, nt, tm, rw_in + [(dy_rw, 0, rw, 0)], rw_full,
        [(t, rw, F32, 0), (t, n_sh_cols, F32, 0), (t, rw, F32, 0)], [(a.shape, F32) for a in rw_full])
    dsh_dirs, hp_grads = _rw_scan_bwd_both("rw_scan_bwd", sh, hps, st_rw, dy_sum, n_ctx)
    dp_rs, g_mu = _shift_bwd("rw_shift_bwd", p_rs, rw_mu, [dsh_p] + dsh_dirs, n_ctx_rows)

    def flat(a):
        if a.shape[1] == 1:
            return a.reshape(rw)
        return jnp.swapaxes(a, 0, 1).reshape(RW_LORA, rw)

    g_w0 = jnp.stack([flat(hp_grads[d][0]) for d in range(2)], axis=0)
    g_w2 = jnp.stack([flat(hp_grads[d][1]) for d in range(2)], axis=0)
    g_a0 = add("g_a0", [jnp.stack([flat(hp_grads[d][2]) for d in range(2)], axis=0), g_a0_p], (2, rw))
    g_a2 = add("g_a2", [jnp.stack([flat(hp_grads[d][3]) for d in range(2)], axis=0), g_a2_p], (2, RW_LORA, rw))
    g_kk = add("g_kk", [flat(hp_grads[0][4]).reshape(1, rw), flat(hp_grads[1][4]).reshape(1, rw)], (1, rw))
    g_ka = add("g_ka", [flat(hp_grads[0][5]).reshape(1, rw), flat(hp_grads[1][5]).reshape(1, rw), g_ka_p], (1, rw))

    dproj = jnp.concatenate([dqi, dfs[0], dfs[1], dz_hg, dp_rs, dz_rw, dp_gt], axis=1).astype(BF16)
    dproj_st = restack(dproj)
    early = {"w_hg_out": g_w_hg_st, "w_rw_out": g_w_rw_st, "w_out": g_w_out.reshape(N_SHARD, dm // N_SHARD, dm)}
    early_chip = [_pair_exchange(f"grads_pair_sum_{n}", a, my_core, True, BF16) for n, a in early.items()]
    g_w_in_st, early_landed = _mm_t_st("d_w_in", h, dproj_st, scatter=tuple(early_chip))
    w_in_chip = _pair_exchange("grads_pair_sum_w_in", g_w_in_st, my_core, True, BF16)
    dh, (w_in_landed,) = _mm_st_t("d_h", dproj_st, w_in_st, scatter=(w_in_chip,))

    def h_bwd(i, r, f):
        shift, scale = pick(i, f[1])
        _, vjp = jax.vjp(_fn_h, r[0], f[0], scale, shift)
        ds, dg, dscale, dshift = vjp(r[1])
        row = jnp.concatenate([dshift, dscale, jnp.zeros((1, dm), F32)], axis=1)
        z = jnp.zeros_like(row)
        is_ctx = i < nct
        dmod = jnp.concatenate([jnp.where(is_ctx, z, row), jnp.where(is_ctx, row, z), jnp.zeros((6, 3 * dm), F32)], axis=0)
        return [ds + r[2]], [dg, dmod]

    grad_x, g_norm_g, dmod_h = _row_call(
        "h_bwd", h_bwd, nt, tm, [(xs, 0, dm, 0), (dh, 0, dm, 0), (dx_res, 0, dm, 0)], [norm_g, mod3],
        [(seq, dm, F32, nct)], [((1, dm), F32), ((8, 3 * dm), F32)])
    dmod = add("d_mod", [dmod_h, dmod_gate], (8, 3 * dm))
    g_ada_b = add("g_ada_b", [dmod[0:1], dmod[1:2]], (1, 3 * dm))
    g_ada_st = _mm_t_st("d_ada_w", sc, restack(dmod))
    d_sc = _mm_st_t("d_cond", restack(dmod), ada_st)

    def cond_bwd(i, r, f):
        _, vjp = jax.vjp(jax.nn.silu, r[0])
        return [vjp(r[1])[0]], []

    (d_cond,) = _row_call("cond_bwd", cond_bwd, 1, 8, [(cond, 0, dm, 0), (d_sc, 0, dm, 0)], [], [(8, dm, F32, 0)], [])

    grads = dict(
        c_ctx=d_cond[1], ada_w=g_ada_st, ada_b=g_ada_b, norm_g=g_norm_g, w_in=(w_in_chip, w_in_landed), hg_lb=g_hg_lb,
        hg_norm_g=g_hg_norm, rw_mu=g_mu, rw_w0=g_w0, rw_w2=g_w2, rw_a0=g_a0, rw_a2=g_a2, rw_kk=g_kk, rw_ka=g_ka,
        rw_rk=g_rk, rw_gn_g=g_gn_g, rw_gn_b=g_gn_b, final_g=d_final_g.reshape(dm))
    grads.update(zip(early, zip(early_chip, early_landed)))
    return loss_acc[0:1, 0:1], grad_x, grads


def _my_place():
    return lax.axis_index("x"), lax.axis_index("y"), lax.axis_index("c")


MIN_CHUNK_BYTES = 1 << 18
ROW_ALIGN = 16


def _n_chunks(rows, row_bytes):
    for n in (8, 4, 2):
        if rows % (n * ROW_ALIGN) == 0 and rows // n * row_bytes >= MIN_CHUNK_BYTES:
            return n
    return 1


def _row_bytes(a, lead=1):
    n = a.dtype.itemsize
    for d in a.shape[lead:]:
        n *= d
    return n


def _rows(ref, start, size):
    return ref.at[pl.ds(start, size)]


def _chunked(make, start, size, n):
    cs = size // n
    return [make(start + j * cs, cs) for j in range(n)]


_PEER_CHIPS = 3


def _weights_gather(name, big, small):
    nb, na = len(big), len(big) + len(small)
    arrays = list(big) + list(small)
    n_ici = 6

    def body(*refs):
        outs = refs[na:2 * na]
        send_sems, recv_sems, fsend_sems, frecv_sems = refs[2 * na:]
        x, y, c = _my_place()
        me, sx, sy, sd = 2 * x + y, 2 * (1 - x) + y, 2 * x + (1 - y), 2 * (1 - x) + (1 - y)
        kx, ky, kd = (1 - x, y, c), (x, 1 - y, c), (1 - x, 1 - y, c)

        def ici(a, j, src_slot, dst_slot, to, r0, nr):
            return pltpu.make_async_remote_copy(
                src_ref=_rows(outs[a].at[src_slot], r0, nr), dst_ref=_rows(outs[a].at[dst_slot], r0, nr),
                send_sem=send_sems.at[a, j], recv_sem=recv_sems.at[a, j], device_id=to,
                device_id_type=pl.DeviceIdType.MESH)

        def to_sibling(a, k, slot, r0, nr):
            rows = _rows(outs[a].at[slot], r0, nr)
            return pltpu.make_async_remote_copy(
                src_ref=rows, dst_ref=rows, send_sem=fsend_sems.at[a, k], recv_sem=frecv_sems.at[a, k],
                device_id=(x, y, 1 - c), device_id_type=pl.DeviceIdType.MESH)

        def start(copies):
            for cp in copies:
                cp.start()

        geo = []
        for a in range(nb):
            half = arrays[a].shape[1] // 2
            geo.append((pl.multiple_of(c * half, ROW_ALIGN), pl.multiple_of((1 - c) * half, ROW_ALIGN), half // 2,
                        _n_chunks(half // 2, _row_bytes(arrays[a], 2))))
        plan = [(me, sx, kx, 0), (me, sx, kx, 1), (me, sy, ky, 0), (me, sy, ky, 1), (sx, sd, ky, 0), (sy, sd, kx, 1)]

        def piece(a, j):
            return geo[a][0] + plan[j][3] * geo[a][2]

        for a in range(nb):
            for j in range(4):
                start(_chunked(lambda r0, cs: ici(a, j, me, me, plan[j][2], r0, cs), piece(a, j), geo[a][2], geo[a][3]))
        for a in range(nb, na):
            rows = arrays[a].shape[1]
            for j, to in ((0, kx), (2, ky), (1, kd)):
                ici(a, j, me, me, to, 0, rows).start()
        for a in range(nb):
            for j, first in ((4, 0), (5, 3)):
                src_slot, _, to, _ = plan[j]
                ici(a, first, me, plan[first][1], plan[first][2], piece(a, first), geo[a][2]).wait_recv()
                start(_chunked(lambda r0, cs: ici(a, j, src_slot, src_slot, to, r0, cs), piece(a, j), geo[a][2], geo[a][3]))
        for a in range(nb):
            for j in (1, 2):
                ici(a, j, me, plan[j][1], plan[j][2], piece(a, j), geo[a][2]).wait_recv()
            for k, slot in ((0, sx), (1, sy)):
                start(_chunked(lambda r0, cs: to_sibling(a, k, slot, r0, cs), geo[a][0], 2 * geo[a][2], geo[a][3]))
        for a in range(nb):
            for j in (4, 5):
                ici(a, j, me, sd, plan[j][2], piece(a, j), geo[a][2]).wait_recv()
            start(_chunked(lambda r0, cs: to_sibling(a, 2, sd, r0, cs), geo[a][0], 2 * geo[a][2], geo[a][3]))
        for a in range(nb, na):
            rows = arrays[a].shape[1]
            for j, slot, to in ((0, sx, kx), (2, sy, ky), (1, sd, kd)):
                ici(a, j, me, slot, to, 0, rows).wait_recv()
        for a in range(nb):
            for k, slot in ((0, sx), (1, sy), (2, sd)):
                to_sibling(a, k, slot, geo[a][1], 2 * geo[a][2]).wait_recv()
        for a in range(nb):
            for j in range(n_ici):
                ici(a, j, me, me, plan[j][2], piece(a, j), geo[a][2]).wait_send()
            for k, slot in ((0, sx), (1, sy), (2, sd)):
                to_sibling(a, k, slot, geo[a][0], 2 * geo[a][2]).wait_send()
        for a in range(nb, na):
            rows = arrays[a].shape[1]
            for j, to in ((0, kx), (2, ky), (1, kd)):
                ici(a, j, me, me, to, 0, rows).wait_send()

    hbm = pl.BlockSpec(memory_space=pl.ANY)
    ici_sems = pltpu.SemaphoreType.DMA((na, n_ici))
    pair_sems = pltpu.SemaphoreType.DMA((na, _PEER_CHIPS))
    return pl.pallas_call(
        body, name=name, in_specs=[hbm] * na, out_specs=[hbm] * na,
        out_shape=[jax.ShapeDtypeStruct(a.shape, a.dtype) for a in arrays],
        input_output_aliases={a: a for a in range(na)}, scratch_shapes=[ici_sems, ici_sems, pair_sems, pair_sems],
    )(*arrays)


def _scatter_copy(arrays, ins, outs, send_sems, recv_sems, a, k, slot, r0, nr):
    x, y, c = _my_place()
    px, py = [(1 - x, y), (x, 1 - y), (1 - x, 1 - y)][k]
    return pltpu.make_async_remote_copy(
        src_ref=_rows(ins[a].at[2 * px + py], r0, nr), dst_ref=_rows(outs[a].at[slot], r0, nr),
        send_sem=send_sems.at[a, k], recv_sem=recv_sems.at[a, k], device_id=(px, py, c),
        device_id_type=pl.DeviceIdType.MESH)


def _scatter_start(arrays, ins, outs, send_sems, recv_sems):
    x, y, _ = _my_place()
    for a in range(len(arrays)):
        rows = arrays[a].shape[1]
        for k in range(_PEER_CHIPS):
            for cp in _chunked(lambda r0, cs: _scatter_copy(arrays, ins, outs, send_sems, recv_sems, a, k, 2 * x + y, r0, cs),
                               0, rows, _n_chunks(rows, _row_bytes(arrays[a], 2))):
                cp.start()


def _scatter_wait(arrays, ins, outs, send_sems, recv_sems):
    x, y, _ = _my_place()
    peer_slot = [2 * (1 - x) + y, 2 * x + (1 - y), 2 * (1 - x) + (1 - y)]
    for k in range(_PEER_CHIPS):
        for a in range(len(arrays)):
            _scatter_copy(arrays, ins, outs, send_sems, recv_sems, a, k, peer_slot[k], 0, arrays[a].shape[1]).wait_recv()
    for a in range(len(arrays)):
        for k in range(_PEER_CHIPS):
            _scatter_copy(arrays, ins, outs, send_sems, recv_sems, a, k, 2 * x + y, 0, arrays[a].shape[1]).wait_send()


PAIR_TILE_BYTES = 2 << 20


def _pair_exchange(name, a, place, reduce, out_dtype):
    rows, cols = a.shape[-2], a.shape[-1]
    half = rows // 2 if reduce else rows
    tr = _row_tile_for(half, cols, budget=PAIR_TILE_BYTES)
    nh = half // tr
    n_steps = (N_SHARD if reduce else 1) * nh

    def body(pc_ref, *refs):
        if reduce:
            keep_ref, send_ref, o_ref, land, send_sems, recv_sems, credit, wire = refs
            wire[...] = send_ref[...].astype(BF16)
            src = wire
        else:
            send_ref, o_ref, land, send_sems, recv_sems, credit = refs
            src = send_ref
        x, y, c = _my_place()
        other = (x, y, 1 - c)
        t = pl.program_id(0) * nh + pl.program_id(1) if reduce else pl.program_id(0)
        slot = t % 2

        @pl.when(t >= 2)
        def _():
            pl.semaphore_wait(credit, 1)

        copy = pltpu.make_async_remote_copy(
            src_ref=src, dst_ref=land.at[slot], send_sem=send_sems.at[slot], recv_sem=recv_sems.at[slot],
            device_id=other, device_id_type=pl.DeviceIdType.MESH)
        copy.start()
        copy.wait_recv()
        got = land[slot]
        o_ref[...] = ((keep_ref[...] + got.astype(F32)) if reduce else got).astype(out_dtype)
        copy.wait_send()

        @pl.when(t < n_steps - 2)
        def _():
            pl.semaphore_signal(credit, inc=1, device_id=other, device_id_type=pl.DeviceIdType.MESH)

    if reduce:
        grid = (N_SHARD, nh)
        in_specs = [pl.BlockSpec((None, tr, cols), lambda j, i, pc: (j, pc[0] * nh + i, 0)),
                    pl.BlockSpec((None, tr, cols), lambda j, i, pc: (j, (1 - pc[0]) * nh + i, 0))]
        out_spec = pl.BlockSpec((None, tr, cols), lambda j, i, pc: (j, i, 0))
        out_shape = jax.ShapeDtypeStruct((N_SHARD, half, cols), out_dtype)
        operands = (a, a)
        sem = ("arbitrary", "arbitrary")
    else:
        grid = (nh,)
        in_specs = [pl.BlockSpec((tr, cols), lambda i, pc: (i, 0))]
        out_spec = pl.BlockSpec((tr, cols), lambda i, pc: (i, 0))
        out_shape = jax.ShapeDtypeStruct((half, cols), out_dtype)
        operands = (a,)
        sem = ("arbitrary",)
    return pl.pallas_call(
        body, name=name,
        grid_spec=pltpu.PrefetchScalarGridSpec(
            num_scalar_prefetch=1, grid=grid, in_specs=in_specs, out_specs=out_spec,
            scratch_shapes=[pltpu.VMEM((2, tr, cols), BF16 if reduce else a.dtype), pltpu.SemaphoreType.DMA((2,)),
                            pltpu.SemaphoreType.DMA((2,)), pltpu.SemaphoreType.REGULAR] +
                           ([pltpu.VMEM((tr, cols), BF16)] if reduce else [])),
        out_shape=out_shape, compiler_params=_params(sem),
    )(place, *operands)


def _cast_into_slot(name, a, chip):
    rows, cols = a.shape
    tm = _row_tile_for(rows, cols)

    def body(pc_ref, a_ref, o_ref):
        o_ref[...] = a_ref[...].astype(BF16)

    return pl.pallas_call(
        body, name=name,
        grid_spec=pltpu.PrefetchScalarGridSpec(
            num_scalar_prefetch=1, grid=(rows // tm,), in_specs=[pl.BlockSpec((tm, cols), lambda i, pc: (i, 0))],
            out_specs=pl.BlockSpec((None, tm, cols), lambda i, pc: (pc[0], i, 0))),
        out_shape=jax.ShapeDtypeStruct((N_SHARD, rows, cols), BF16), compiler_params=_params(("parallel",)),
    )(chip, a)


def _sum_landed(name, landed, sent, chip):
    ns, rows, cols = landed.shape
    tm = _row_tile_for(rows, cols)

    def body(pc_ref, *refs):
        own_ref, o_ref = refs[ns], refs[ns + 1]
        me = pc_ref[0]
        terms = [jnp.where(me == j, own_ref[...], refs[j][...]).astype(F32) for j in range(ns)]
        o_ref[...] = _slot_sum(terms)

    def landed_spec(j):
        return pl.BlockSpec((None, tm, cols), lambda i, pc: (jnp.where(pc[0] == j, (j + 1) % ns, j), i, 0))

    return pl.pallas_call(
        body, name=name,
        grid_spec=pltpu.PrefetchScalarGridSpec(
            num_scalar_prefetch=1, grid=(rows // tm,),
            in_specs=[landed_spec(j) for j in range(ns)] + [pl.BlockSpec((None, tm, cols), lambda i, pc: (pc[0], i, 0))],
            out_specs=pl.BlockSpec((tm, cols), lambda i, pc: (i, 0))),
        out_shape=jax.ShapeDtypeStruct((rows, cols), F32), compiler_params=_params(("parallel",)),
    )(chip, *([landed] * ns), sent)


def _gather_all(name, a):
    def body(in_ref, out_ref, send_sems, recv_sems, local_sem):
        x, y, c = _my_place()
        me = 4 * x + 2 * y + c

        def peer(k):
            return (x ^ (k >> 2), y ^ ((k >> 1) & 1), c ^ (k & 1))

        def remote(k, land):
            return pltpu.make_async_remote_copy(
                src_ref=in_ref, dst_ref=out_ref.at[land], send_sem=send_sems.at[k - 1], recv_sem=recv_sems.at[k - 1],
                device_id=peer(k), device_id_type=pl.DeviceIdType.MESH)

        local = pltpu.make_async_copy(in_ref, out_ref.at[me], local_sem)
        local.start()
        for k in range(1, N_DEV):
            remote(k, me).start()
        for k in range(1, N_DEV):
            px, py, pc = peer(k)
            remote(k, 4 * px + 2 * py + pc).wait_recv()
        for k in range(1, N_DEV):
            remote(k, me).wait_send()
        local.wait()

    hbm = pl.BlockSpec(memory_space=pl.ANY)
    return pl.pallas_call(
        body, name=name, in_specs=[hbm], out_specs=hbm,
        out_shape=jax.ShapeDtypeStruct((N_DEV,) + a.shape, a.dtype),
        scratch_shapes=[pltpu.SemaphoreType.DMA((N_DEV - 1,)), pltpu.SemaphoreType.DMA((N_DEV - 1,)), pltpu.SemaphoreType.DMA],
    )(a)


def _row_tile_for(rows, cols, budget=1 << 20):
    if rows * cols * 4 <= budget:
        return rows
    for tm in (1024, 512, 256, 128, 64, 32, 16, 8):
        if rows % tm == 0 and tm * cols * 4 <= budget:
            return tm
    return rows


def _slot_sum(vals):
    g = vals[0]
    for v in vals[1:]:
        g = g + v
    return g


def _rowwise(name, fn, arrays, out_dtype):
    rows, cols = arrays[0].shape
    tm = _row_tile_for(rows, cols)

    def body(*refs):
        refs[-1][...] = fn(*[r[...] for r in refs[:-1]]).astype(out_dtype)

    blk = pl.BlockSpec((tm, cols), lambda i: (i, 0))
    return pl.pallas_call(
        body, name=name, grid=(rows // tm,), in_specs=[blk] * len(arrays), out_specs=blk,
        out_shape=jax.ShapeDtypeStruct((rows, cols), out_dtype), compiler_params=_params(("parallel",)),
    )(*arrays)


def _sum_slots(name, st):
    ns, rows, cols = st.shape
    tm = _row_tile_for(rows, cols)

    def body(s_ref, o_ref):
        o_ref[...] = _slot_sum([s_ref[j].astype(F32) for j in range(ns)])

    return pl.pallas_call(
        body, name=name, grid=(rows // tm,),
        in_specs=[pl.BlockSpec((ns, tm, cols), lambda i: (0, i, 0))],
        out_specs=pl.BlockSpec((tm, cols), lambda i: (i, 0)),
        out_shape=jax.ShapeDtypeStruct((rows, cols), F32),
        compiler_params=_params(("parallel",)),
    )(st)


ADAM_TILE_BYTES = 1 << 19


def _adam_update(g, p_ref, m_ref, v_ref, go_ref, d_ref, mo_ref, vo_ref):
    mn = ADAM_B1 * m_ref[...] + (1.0 - ADAM_B1) * g
    vn = ADAM_B2 * v_ref[...] + (1.0 - ADAM_B2) * jnp.square(g)
    m_hat = mn / (1.0 - ADAM_B1 ** ADAM_STEP)
    v_hat = vn / (1.0 - ADAM_B2 ** ADAM_STEP)
    go_ref[...] = g
    d_ref[...] = -ADAM_LR * (m_hat / (jnp.sqrt(v_hat) + ADAM_EPS) + ADAM_WD * p_ref[...])
    mo_ref[...] = mn
    vo_ref[...] = vn


def _adamw(name, p, m, v, gst):
    rows, cols = p.shape
    ns = gst.shape[0]
    tm = _row_tile_for(rows, cols, budget=ADAM_TILE_BYTES)

    def body(p_ref, m_ref, v_ref, g_ref, *outs):
        _adam_update(_slot_sum([g_ref[j] for j in range(ns)]), p_ref, m_ref, v_ref, *outs)

    blk = pl.BlockSpec((tm, cols), lambda i: (i, 0))
    return pl.pallas_call(
        body, name=name, grid=(rows // tm,),
        in_specs=[blk, blk, blk, pl.BlockSpec((ns, tm, cols), lambda i: (0, i, 0))],
        out_specs=[blk] * 4, out_shape=[jax.ShapeDtypeStruct((rows, cols), F32)] * 4,
        compiler_params=_params(("parallel",)),
    )(p, m, v, gst)


def _adamw_halves(name, p, m, v, mine, theirs, place, scatter=()):
    rows, cols = p.shape
    half = rows // 2
    tm = _row_tile_for(half, cols, budget=ADAM_TILE_BYTES)
    nh = half // tm
    ns = len(scatter)

    def body(pc_ref, p_ref, m_ref, v_ref, mine_ref, theirs_ref, *refs):
        if ns:
            sc_refs = (refs[:ns], refs[ns + 4:2 * ns + 4]) + tuple(refs[2 * ns + 4:])
            at = pl.program_id(0) * nh + pl.program_id(1)
            pl.when(at == 0)(lambda: _scatter_start(scatter, *sc_refs))
        g = jnp.where(pl.program_id(0) == pc_ref[0], mine_ref[...], theirs_ref[...])
        _adam_update(g, p_ref, m_ref, v_ref, *refs[ns:ns + 4])
        if ns:
            pl.when(at == 2 * nh - 1)(lambda: _scatter_wait(scatter, *sc_refs))

    blk = pl.BlockSpec((tm, cols), lambda h, i, pc: (h * nh + i, 0))
    hblk = pl.BlockSpec((tm, cols), lambda h, i, pc: (i, 0))
    hbm = pl.BlockSpec(memory_space=pl.ANY)
    res = pl.pallas_call(
        body, name=name,
        grid_spec=pltpu.PrefetchScalarGridSpec(
            num_scalar_prefetch=1, grid=(2, nh), in_specs=[blk, blk, blk, hblk, hblk] + [hbm] * ns,
            out_specs=[blk] * 4 + [hbm] * ns,
            scratch_shapes=[pltpu.SemaphoreType.DMA((ns, _PEER_CHIPS))] * 2 if ns else []),
        out_shape=[jax.ShapeDtypeStruct((rows, cols), F32)] * 4 + [jax.ShapeDtypeStruct(s.shape, s.dtype) for s in scatter],
        compiler_params=_params(("arbitrary", "arbitrary") if ns else ("parallel", "parallel")),
    )(place, p, m, v, mine, theirs, *scatter)
    return (list(res[:4]), list(res[4:])) if ns else res


def _pack(parts, width=LANE, mult=8):
    flat = jnp.concatenate([a.reshape(-1) for a in parts])
    n = flat.shape[0]
    per = width * mult
    total = -(-n // per) * per
    return jnp.pad(flat, (0, total - n)).reshape(total // width, width)


def _unpack(packed, shapes):
    flat = packed.reshape(-1)
    out, off = [], 0
    for s in shapes:
        n = 1
        for d in s:
            n *= d
        out.append(flat[off:off + n].reshape(s))
        off += n
    return out


_SMALL_SHARDED = ("hg_lb", "rw_mu", "rw_w0", "rw_w2", "rw_a0", "rw_a2")
_REPLICATED = ("c_ctx", "ada_b", "norm_g", "hg_norm_g", "rw_kk", "rw_ka", "rw_rk", "rw_gn_g", "rw_gn_b", "final_g")
_BIG = ("ada_w", "w_in", "w_hg_out", "w_rw_out", "w_out")
_WEIGHTS = ("c_ctx", "ada_w", "ada_b", "norm_g", "w_in", "hg_lb", "hg_norm_g", "rw_mu", "rw_w0", "rw_w2", "rw_a0", "rw_a2",
            "rw_kk", "rw_ka", "rw_rk", "rw_gn_g", "rw_gn_b", "w_hg_out", "w_rw_out", "w_out", "final_g")


def _join_shards(st):
    a = jnp.moveaxis(st, 0, -2)
    return a.reshape(a.shape[:-2] + (a.shape[-2] * a.shape[-1],))


def _split_shards(a):
    s = a.reshape(a.shape[:-1] + (N_SHARD, a.shape[-1] // N_SHARD))
    return jnp.moveaxis(s, -2, 0)


def kernel(x, c, ctx, c_ctx, ada_w, ada_b, norm_g, w_in, hg_lb, hg_norm_g, rw_mu, rw_w0, rw_w2, rw_a0, rw_a2, rw_kk, rw_ka, rw_rk, rw_gn_g, rw_gn_b, w_hg_out, w_rw_out, w_out, final_g, loss_target, m_c_ctx, m_ada_w, m_ada_b, m_norm_g, m_w_in, m_hg_lb, m_hg_norm_g, m_rw_mu, m_rw_w0, m_rw_w2, m_rw_a0, m_rw_a2, m_rw_kk, m_rw_ka, m_rw_rk, m_rw_gn_g, m_rw_gn_b, m_w_hg_out, m_w_rw_out, m_w_out, m_final_g, v_c_ctx, v_ada_w, v_ada_b, v_norm_g, v_w_in, v_hg_lb, v_hg_norm_g, v_rw_mu, v_rw_w0, v_rw_w2, v_rw_a0, v_rw_a2, v_rw_kk, v_rw_ka, v_rw_rk, v_rw_gn_g, v_rw_gn_b, v_w_hg_out, v_w_rw_out, v_w_out, v_final_g):
    w = dict(c_ctx=c_ctx, ada_w=ada_w, ada_b=ada_b, norm_g=norm_g, w_in=w_in, hg_lb=hg_lb, hg_norm_g=hg_norm_g, rw_mu=rw_mu,
             rw_w0=rw_w0, rw_w2=rw_w2, rw_a0=rw_a0, rw_a2=rw_a2, rw_kk=rw_kk, rw_ka=rw_ka, rw_rk=rw_rk, rw_gn_g=rw_gn_g,
             rw_gn_b=rw_gn_b, w_hg_out=w_hg_out, w_rw_out=w_rw_out, w_out=w_out, final_g=final_g)
    m = dict(c_ctx=m_c_ctx, ada_w=m_ada_w, ada_b=m_ada_b, norm_g=m_norm_g, w_in=m_w_in, hg_lb=m_hg_lb, hg_norm_g=m_hg_norm_g,
             rw_mu=m_rw_mu, rw_w0=m_rw_w0, rw_w2=m_rw_w2, rw_a0=m_rw_a0, rw_a2=m_rw_a2, rw_kk=m_rw_kk, rw_ka=m_rw_ka,
             rw_rk=m_rw_rk, rw_gn_g=m_rw_gn_g, rw_gn_b=m_rw_gn_b, w_hg_out=m_w_hg_out, w_rw_out=m_w_rw_out, w_out=m_w_out,
             final_g=m_final_g)
    v = dict(c_ctx=v_c_ctx, ada_w=v_ada_w, ada_b=v_ada_b, norm_g=v_norm_g, w_in=v_w_in, hg_lb=v_hg_lb, hg_norm_g=v_hg_norm_g,
             rw_mu=v_rw_mu, rw_w0=v_rw_w0, rw_w2=v_rw_w2, rw_a0=v_rw_a0, rw_a2=v_rw_a2, rw_kk=v_rw_kk, rw_ka=v_rw_ka,
             rw_rk=v_rw_rk, rw_gn_g=v_rw_gn_g, rw_gn_b=v_rw_gn_b, w_hg_out=v_w_hg_out, w_rw_out=v_w_rw_out, w_out=v_w_out,
             final_g=v_final_g)

    def mat(a):
        return a.reshape(a.shape[-2], a.shape[-1])

    def pack_small(d):
        return _pack([d[n] for n in _SMALL_SHARDED], mult=2 * ROW_ALIGN)

    my_core = lax.axis_index("c").astype(jnp.int32).reshape(1)
    my_chip = (2 * lax.axis_index("x") + lax.axis_index("y")).astype(jnp.int32).reshape(1)

    small_shapes = [w[n].shape for n in _SMALL_SHARDED]
    big_bf = [_cast_into_slot(f"to_bf16_{n}", mat(w[n]), my_chip) for n in _BIG]
    small_mine = pack_small(w)
    small_slots = lax.dynamic_update_slice(jnp.zeros((N_SHARD,) + small_mine.shape, F32), small_mine[None], (my_chip[0], 0, 0))
    gathered = _weights_gather("weights_gather", big_bf, [small_slots])
    ada_st, w_in_st, w_hg_st, w_rw_st, w_out_st, small_st = gathered
    full_small = {}
    per_chip = [_unpack(small_st[j], small_shapes) for j in range(N_SHARD)]
    for i, n in enumerate(_SMALL_SHARDED):
        full_small[n] = _join_shards(jnp.stack([per_chip[j][i] for j in range(N_SHARD)], axis=0))
    dm = x.shape[-1]
    w_out_full = w_out_st.reshape(dm, dm)

    loss_b, grad_x, g = _local_step(
        x[0], c, ctx[0], c_ctx, ada_st, ada_b, norm_g, w_in_st, full_small["hg_lb"], hg_norm_g, full_small["rw_mu"][0],
        full_small["rw_w0"][0], full_small["rw_w2"][0], full_small["rw_a0"][0], full_small["rw_a2"][0], rw_kk, rw_ka, rw_rk,
        rw_gn_g, rw_gn_b, w_hg_st, w_rw_st, w_out_full, final_g, loss_target[0], my_core)
    loss = lax.psum(loss_b[0, 0], ("x", "y", "c"))

    g_small = {"hg_lb": g["hg_lb"], "rw_mu": g["rw_mu"][None], "rw_w0": g["rw_w0"][None], "rw_w2": g["rw_w2"][None],
               "rw_a0": g["rw_a0"][None], "rw_a2": g["rw_a2"][None]}
    split = {n: _split_shards(g_small[n]) for n in _SMALL_SHARDED}
    small_parts = jnp.stack([pack_small({n: split[n][j] for n in _SMALL_SHARDED}) for j in range(N_SHARD)], axis=0)
    def finish(name, chip_sum, landed):
        half = _sum_landed(f"grads_sum_{name}", landed, chip_sum, my_chip)
        return half, _pair_exchange(f"grads_pair_swap_{name}", half, my_core, False, F32)

    res = {}
    later = {"ada_w": g["ada_w"], "small": small_parts}
    later_chip = [_pair_exchange(f"grads_pair_sum_{n}", a, my_core, True, BF16) for n, a in later.items()]
    outs, later_landed = _adamw_halves("adamw_w_in", mat(w["w_in"]), mat(m["w_in"]), mat(v["w_in"]),
                                       *finish("w_in", *g["w_in"]), my_core, scatter=tuple(later_chip))
    res["w_in"] = [o.reshape(w["w_in"].shape) for o in outs]
    pending = {n: g[n] for n in ("w_hg_out", "w_rw_out", "w_out")}
    pending.update(zip(later, zip(later_chip, later_landed)))
    rep_shapes = [w[n].shape for n in _REPLICATED]
    rep_all = _gather_all("grads_replicated", _pack([g[n].reshape(w[n].shape) for n in _REPLICATED]))

    for n in ("ada_w", "w_hg_out", "w_rw_out", "w_out"):
        outs = _adamw_halves(f"adamw_{n}", mat(w[n]), mat(m[n]), mat(v[n]), *finish(n, *pending[n]), my_core)
        res[n] = [o.reshape(w[n].shape) for o in outs]
    outs = _adamw_halves("adamw_small", small_mine, pack_small(m), pack_small(v), *finish("small", *pending["small"]), my_core)
    for i, vals in enumerate(zip(*[_unpack(o, small_shapes) for o in outs])):
        res[_SMALL_SHARDED[i]] = list(vals)
    outs = _adamw("adamw_replicated", _pack([w[n] for n in _REPLICATED]), _pack([m[n] for n in _REPLICATED]),
                  _pack([v[n] for n in _REPLICATED]), rep_all)
    for i, vals in enumerate(zip(*[_unpack(o, rep_shapes) for o in outs])):
        res[_REPLICATED[i]] = list(vals)

    return (loss, grad_x[None], *[res[n][0] for n in _WEIGHTS], *[res[n][1] for n in _WEIGHTS],
            *[res[n][2] for n in _WEIGHTS], *[res[n][3] for n in _WEIGHTS])
```

```python
import functools

import jax
import jax.numpy as jnp
from jax import lax
from jax.experimental import pallas as pl
from jax.experimental.pallas import tpu as pltpu

HI = lax.Precision.HIGHEST
F32 = jnp.float32
BF16 = jnp.bfloat16

NORM_EPS = 1e-6
HG_HEAD = 128
RW_HEAD = 64
RW_LORA = 64
RW_GN_EPS = 64e-5
GRID_W = 64
SUB = 16
STEP = 64
RW_STEP = 64
N_SHARD = 4
N_DEV = 8
LANE = 128

ADAM_LR = 0.001
ADAM_B1 = 0.9
ADAM_B2 = 0.999
ADAM_EPS = 1e-08
ADAM_WD = 0.01
ADAM_STEP = 10

VMEM_LIMIT = 56 * 1024 * 1024


def _params(sem=None):
    return pltpu.CompilerParams(dimension_semantics=sem, vmem_limit_bytes=VMEM_LIMIT)


def _tile(n, cands):
    for c in cands:
        if n % c == 0:
            return c
    return n


def _iota2(n, m, d):
    return lax.broadcasted_iota(jnp.int32, (n, m), d)


def _before(n, rev, strict):
    t, s = _iota2(n, n, 0), _iota2(n, n, 1)
    if rev:
        return (s > t) if strict else (s >= t)
    return (s < t) if strict else (s <= t)


def _bdot(a, b, spec):
    return jnp.einsum(spec, a, b, precision=HI, preferred_element_type=F32)


def _sdot(a, b, spec):
    return jnp.einsum(spec, a, b, precision=lax.Precision.DEFAULT, preferred_element_type=F32)


def _hg_step(s0, qraw, iin, fin, lb2, rev):
    c, w = qraw.shape
    h = w // HG_HEAD
    nsub = c // SUB
    lb = jax.nn.sigmoid(lb2[0:1] - lb2[1:2])
    q = jax.nn.silu(qraw)
    fg = lb + (1.0 - lb) * jax.nn.sigmoid(fin)
    kk = 1.0 - fg
    g = jnp.log(fg)
    bcum = jnp.dot(_before(c, rev, False).astype(F32), g, precision=HI, preferred_element_type=F32)
    def heads(a):
        return jnp.swapaxes(a.reshape(a.shape[0], h, HG_HEAD), 0, 1)

    def unheads(a):
        return jnp.swapaxes(a, 0, 1).reshape(a.shape[1], w)

    blocks = [slice(j * SUB, (j + 1) * SUB) for j in range(nsub)]
    outs = []
    for sl in blocks:
        qs, ks, vs, bc = [a[sl].reshape(SUB, h, HG_HEAD) for a in (q, kk, iin, bcum)]
        o = jnp.zeros((SUB, h, HG_HEAD), F32)
        for si in range(SUB):
            after = slice(0, si + 1) if rev else slice(si, SUB)
            dec = jnp.exp(jnp.minimum(bc[after] - bc[si:si + 1], 0.0))
            a = jnp.sum(qs[after] * ks[si:si + 1] * dec, axis=-1, keepdims=True)
            term = a * vs[si:si + 1]
            n_rest = SUB - 1 - si if rev else si
            if n_rest:
                rest = jnp.zeros((n_rest, h, HG_HEAD), F32)
                term = jnp.concatenate([term, rest] if rev else [rest, term], axis=0)
            o = o + term
        outs.append(o.reshape(SUB, w))
    order = list(range(nsub - 1, -1, -1)) if rev else list(range(nsub))
    for pos in range(1, nsub):
        j, before = order[pos], order[:pos]
        first = (j + 1) * SUB - 1 if rev else j * SUB
        bstart = bcum[first:first + 1] - g[first:first + 1]
        qp = heads(q[blocks[j]] * jnp.exp(bcum[blocks[j]] - bstart))
        kp = heads(jnp.concatenate([kk[blocks[p]] * jnp.exp(bstart - bcum[blocks[p]]) for p in before], axis=0))
        vp = heads(jnp.concatenate([iin[blocks[p]] for p in before], axis=0))
        outs[j] = outs[j] + unheads(_sdot(_sdot(qp, kp, 'htk,hsk->hts'), vp, 'hts,hsv->htv'))
    o_state = unheads(_sdot(heads(q * jnp.exp(bcum)), s0, 'htk,hvk->htv'))
    last = 0 if rev else c - 1
    blast = bcum[last:last + 1]
    s_new = heads(jnp.exp(blast)) * s0 + _sdot(heads(iin), heads(kk * jnp.exp(blast - bcum)), 'hsv,hsk->hvk')
    return jnp.concatenate(outs, axis=0) + o_state, s_new


def _tri_solve(lmat, rhs, rev):
    hh, c, _ = lmat.shape
    nb = c // SUB
    diag = jnp.concatenate([lmat[:, i * SUB:(i + 1) * SUB, i * SUB:(i + 1) * SUB] for i in range(nb)], axis=0)
    dt = jnp.transpose(diag, (1, 2, 0))
    col = lax.broadcasted_iota(jnp.int32, (SUB, 1), 0)
    inv_rows = [None] * SUB
    order = list(range(SUB - 1, -1, -1)) if rev else list(range(SUB))
    for pos, t in enumerate(order):
        row = jnp.broadcast_to((col == t).astype(F32), (SUB, dt.shape[2]))
        for s in order[:pos]:
            row = row - dt[t, s:s + 1, :] * inv_rows[s]
        inv_rows[t] = row
    tinv = jnp.transpose(jnp.concatenate([r[None] for r in inv_rows], axis=0), (2, 0, 1))
    p = [None] * nb
    done = []
    for i in (range(nb - 1, -1, -1) if rev else range(nb)):
        r = rhs[:, i * SUB:(i + 1) * SUB]
        if done:
            lrow = jnp.concatenate([lmat[:, i * SUB:(i + 1) * SUB, m * SUB:(m + 1) * SUB] for m in done], axis=2)
            r = r - _sdot(lrow, jnp.concatenate([p[m] for m in done], axis=1), 'hts,hsv->htv')
        p[i] = _sdot(tinv[i * hh:(i + 1) * hh], r, 'hts,hsv->htv')
        done.append(i)
    return jnp.concatenate(p, axis=1)


def _rw_step(s0, r, k, v, wlo, alo, w0h, w2h, a0h, a2h, kkh, kah, rev):
    hh, c, _ = r.shape
    tl = jnp.broadcast_to(jnp.tanh(wlo)[None], (hh, c, wlo.shape[1]))
    al = jnp.broadcast_to(alo[None], (hh, c, alo.shape[1]))
    wlog = -jax.nn.softplus(-(w0h + _sdot(tl, w2h, 'hcl,hlj->hcj'))) - 0.5
    lw = -jnp.exp(wlog)
    a = jax.nn.sigmoid(a0h + _sdot(al, a2h, 'hcl,hlj->hcj'))
    kk = k * kkh
    kk = kk * lax.rsqrt(jnp.sum(kk * kk, axis=-1, keepdims=True) + 1e-12)
    kd = k * (1.0 + (a - 1.0) * kah)
    b = kk * a
    incl = jnp.broadcast_to(_before(c, rev, False).astype(F32)[None], (hh, c, c))
    cum = _bdot(incl, lw, 'hts,hsk->htk')
    ecum, encum = jnp.exp(cum), jnp.exp(-cum)
    alpha = jnp.exp(cum - lw) * kk
    beta = b * encum
    kappa = kd * encum
    rho = r * ecum
    m_lt = _before(c, rev, True)[None]
    m_le = _before(c, rev, False)[None]
    ar = jnp.concatenate([alpha, rho], axis=1)
    kb = jnp.concatenate([kappa, beta], axis=1)
    gram = _sdot(ar, kb, 'htk,hsk->hts')
    a_kap = jnp.where(m_lt, gram[:, :c, :c], 0.0)
    a_bet = jnp.where(m_lt, gram[:, :c, c:], 0.0)
    b_kap = jnp.where(m_le, gram[:, c:, :c], 0.0)
    b_bet = jnp.where(m_le, gram[:, c:, c:], 0.0)
    from_state = _sdot(ar, s0, 'htk,hvk->htv')
    p = _tri_solve(a_bet, from_state[:, :c] + _sdot(a_kap, v, 'hts,hsv->htv'), rev)
    vp = jnp.concatenate([v, -p], axis=1)
    y = from_state[:, c:] + _sdot(jnp.concatenate([b_kap, b_bet], axis=2), vp, 'hts,hsv->htv')
    stil = s0 + _sdot(vp, kb, 'hsv,hsk->hvk')
    last = 0 if rev else c - 1
    return y, stil * ecum[:, last:last + 1, :]


def _fn_h(s, norm_g, scale, shift):
    return s * lax.rsqrt(jnp.mean(s * s, axis=-1, keepdims=True) + NORM_EPS) * norm_g * (1.0 + scale) + shift


def _fn_hgpost(of, ob, z, g):
    tm, w = of.shape
    o = (of + ob).reshape(tm, w // HG_HEAD, HG_HEAD)
    o = o * lax.rsqrt(jnp.mean(o * o, axis=-1, keepdims=True) + NORM_EPS)
    return o.reshape(tm, w) * g * jax.nn.silu(z)


def _fn_rwpost(y0, y1, r, k, v, alo, z, a0, a2, k_a, r_k, gn_g, gn_b):
    tm, w = r.shape
    nh = w // RW_HEAD
    asum = 0.0
    for d in range(2):
        asum = asum + jax.nn.sigmoid(a0[d:d + 1] + jnp.dot(alo[:, d * RW_LORA:(d + 1) * RW_LORA], a2[d],
                                                           precision=HI, preferred_element_type=F32))
    k_sum = k * (2.0 + (asum - 2.0) * k_a)
    ys = (y0 + y1).reshape(tm, nh, RW_HEAD)
    mean = jnp.mean(ys, axis=-1, keepdims=True)
    var = jnp.mean(jnp.square(ys - mean), axis=-1, keepdims=True)
    y = ((ys - mean) * lax.rsqrt(var + RW_GN_EPS)).reshape(tm, w) * gn_g + gn_b
    bonus = jnp.sum((r * k_sum * r_k).reshape(tm, nh, RW_HEAD), axis=-1, keepdims=True) * v.reshape(tm, nh, RW_HEAD)
    return (y + bonus.reshape(tm, w)) * jax.nn.silu(z)


def _fn_merge(a, b, ghg, grw):
    return jax.nn.sigmoid(ghg) * a + jax.nn.sigmoid(grw) * b


def _fn_final(xs, o, gate, final_g, tgt):
    x2 = xs + gate * o
    y = x2 * lax.rsqrt(jnp.mean(x2 * x2, axis=-1, keepdims=True) + NORM_EPS) * final_g
    return 0.5 * jnp.sum(jnp.mean(jnp.square(y - tgt), axis=-1))


def _row_call(name, fn, n_tiles, tm, row_ins, full_ins, row_outs, acc_outs):
    n_ri, n_fi, n_ro = len(row_ins), len(full_ins), len(row_outs)

    def body(*refs):
        i = pl.program_id(0)
        rvals = [r[...] for r in refs[:n_ri]]
        fvals = [r[...] for r in refs[n_ri:n_ri + n_fi]]
        outs = refs[n_ri + n_fi:]
        ro, ao = fn(i, rvals, fvals)
        for ref, val in zip(outs[:n_ro], ro):
            ref[...] = val.astype(ref.dtype)
        for ref, val in zip(outs[n_ro:], ao):
            @pl.when(i == 0)
            def _(ref=ref):
                ref[...] = jnp.zeros_like(ref)
            ref[...] += val.astype(ref.dtype)

    def rspec(width, cb, off, rows):
        return pl.BlockSpec((tm, width), lambda i: (jnp.clip(i - off, 0, rows // tm - 1), cb))

    def fspec(shape):
        nd = len(shape)
        return pl.BlockSpec(shape, lambda i: (0,) * nd)

    in_specs = [rspec(w, cb, off, a.shape[0]) for (a, cb, w, off) in row_ins] + [fspec(a.shape) for a in full_ins]
    out_specs = [rspec(w, 0, off, rows) for (rows, w, _, off) in row_outs] + [fspec(s) for (s, _) in acc_outs]
    out_shape = [jax.ShapeDtypeStruct((rows, w), dt) for (rows, w, dt, _) in row_outs] + \
                [jax.ShapeDtypeStruct(s, dt) for (s, dt) in acc_outs]
    res = pl.pallas_call(
        body, name=name, grid=(n_tiles,), in_specs=in_specs, out_specs=out_specs, out_shape=out_shape,
        compiler_params=_params(("arbitrary",)),
    )(*[a for (a, _, _, _) in row_ins], *full_ins)
    return list(res)


def _mm(name, a, b, m, n, k_steps, tm, tn, a_block, a_map, b_block, b_map, o_shape, o_block, o_map,
        contract, out_dtype=F32, scatter=()):
    ns = len(scatter)
    grid = (m // tm, n // tn, k_steps)

    def body(*refs):
        a_ref, b_ref, o_ref, acc_ref = refs[0], refs[1], refs[2 + ns], refs[3 + 2 * ns]
        kk = pl.program_id(2)
        if ns:
            sc_refs = (refs[2:2 + ns], refs[3 + ns:3 + 2 * ns]) + tuple(refs[4 + 2 * ns:])
            at = (pl.program_id(0) * grid[1] + pl.program_id(1)) * grid[2] + kk
            pl.when(at == 0)(lambda: _scatter_start(scatter, *sc_refs))

        @pl.when(kk == 0)
        def _():
            acc_ref[...] = jnp.zeros_like(acc_ref)

        acc_ref[...] += lax.dot_general(a_ref[...].astype(BF16), b_ref[...].astype(BF16),
                                        (contract, ((), ())), preferred_element_type=F32)

        @pl.when(kk == k_steps - 1)
        def _():
            o_ref[...] = acc_ref[...].astype(o_ref.dtype)

        if ns:
            pl.when(at == grid[0] * grid[1] * grid[2] - 1)(lambda: _scatter_wait(scatter, *sc_refs))

    hbm = pl.BlockSpec(memory_space=pl.ANY)
    sems = [pltpu.SemaphoreType.DMA((ns, _PEER_CHIPS))] * 2 if ns else []
    res = pl.pallas_call(
        body, name=name, grid=grid,
        in_specs=[pl.BlockSpec(a_block, a_map), pl.BlockSpec(b_block, b_map)] + [hbm] * ns,
        out_specs=[pl.BlockSpec(o_block, o_map)] + [hbm] * ns,
        out_shape=[jax.ShapeDtypeStruct(o_shape, out_dtype)] + [jax.ShapeDtypeStruct(s.shape, s.dtype) for s in scatter],
        scratch_shapes=[pltpu.VMEM((tm, tn), F32)] + sems,
        compiler_params=_params(("arbitrary",) * 3 if ns else ("parallel", "parallel", "arbitrary")),
    )(a, b, *scatter)
    return (res[0], list(res[1:])) if ns else res[0]


_TM = (768, 512, 256, 128, 64, 32, 16, 8)
_TN = (512, 256, 128)
_TK = (1024, 768, 512, 256, 128)
_TK_WIDE = (768, 512, 256, 128)
WIDE_OUT_BYTES = 32 << 20


def _tm_wide(m, ns):
    for tm in _TM:
        if m % tm == 0 and 3 * 4 * tm * ns <= WIDE_OUT_BYTES:
            return tm
    return m


def _mm_nn(name, a, b, out_dtype=F32):
    m, k = a.shape
    n = b.shape[1]
    tm, tn, tk = _tile(m, _TM), _tile(n, _TN), _tile(k, _TK)
    return _mm(name, a, b, m, n, k // tk, tm, tn, (tm, tk), lambda i, j, s: (i, s), (tk, tn), lambda i, j, s: (s, j),
               (m, n), (tm, tn), lambda i, j, s: (i, j), ((1,), (0,)), out_dtype)


def _mm_nt(name, a, b, out_dtype=F32):
    m, k = a.shape
    n = b.shape[0]
    tm, tn, tk = _tile(m, _TM), _tile(n, _TN), _tile(k, _TK)
    return _mm(name, a, b, m, n, k // tk, tm, tn, (tm, tk), lambda i, j, s: (i, s), (tn, tk), lambda i, j, s: (j, s),
               (m, n), (tm, tn), lambda i, j, s: (i, j), ((1,), (1,)), out_dtype)


def _mm_tn(name, a, b, out_dtype=F32):
    k, m = a.shape
    n = b.shape[1]
    tm, tn, tk = _tile(m, _TM), _tile(n, _TN), _tile(k, _TK)
    return _mm(name, a, b, m, n, k // tk, tm, tn, (tk, tm), lambda i, j, s: (s, i), (tk, tn), lambda i, j, s: (s, j),
               (m, n), (tm, tn), lambda i, j, s: (i, j), ((0,), (0,)), out_dtype)


def _mm_n_st(name, a, bst, out_dtype=F32, joined=False):
    m, k = a.shape
    ns_, _, ns = bst.shape
    tm, tk = _tm_wide(m, ns), _tile(k, (512, 256, 128))
    out = ((m, ns_ * ns), (tm, ns), lambda i, j, s: (i, j)) if joined else \
          ((ns_, m, ns), (None, tm, ns), lambda i, j, s: (j, i, 0))
    return _mm(name, a, bst, m, ns_ * ns, k // tk, tm, ns,
               (tm, tk), lambda i, j, s: (i, s), (None, tk, ns), lambda i, j, s: (j, s, 0), *out, ((1,), (0,)), out_dtype)


def _mm_st_t(name, ast, bst, out_dtype=F32, scatter=()):
    ns_, n, ns = bst.shape
    m = ast.shape[-2]
    tm, tn = _tile(m, _TM), _tile(n, _TN)
    a_side = ((None, tm, ns), lambda i, j, s: (s, i, 0)) if ast.ndim == 3 else ((tm, ns), lambda i, j, s: (i, s))
    return _mm(name, ast, bst, m, n, ns_, tm, tn, *a_side, (None, tn, ns), lambda i, j, s: (s, j, 0),
               (m, n), (tm, tn), lambda i, j, s: (i, j), ((1,), (1,)), out_dtype, scatter)


def _mm_t_st(name, a, bst, out_dtype=F32, scatter=(), n_shard=N_SHARD):
    k, m = a.shape
    ns = bst.shape[-1] if bst.ndim == 3 else bst.shape[-1] // n_shard
    tm, tk = _tile(m, _TN), _tile(k, _TK_WIDE)
    b_side = ((None, tk, ns), lambda i, j, s: (j, s, 0)) if bst.ndim == 3 else ((tk, ns), lambda i, j, s: (s, j))
    return _mm(name, a, bst, m, n_shard * ns, k // tk, tm, ns, (tk, tm), lambda i, j, s: (s, i), *b_side,
               (n_shard, m, ns), (None, tm, ns), lambda i, j, s: (j, i, 0), ((0,), (0,)), out_dtype, scatter)


def _scan_order(j, n_ctx, n_all, rev):
    if not rev:
        return j
    return jnp.where(j < n_ctx, n_ctx - 1 - j, n_all - 1 - (j - n_ctx))


def _hg_scan_fwd(name, p_hg, lb2, d, n_ctx):
    t, w5 = p_hg.shape
    w = w5 // 5
    h = w // HG_HEAD
    n = t // STEP
    rev = d == 1

    def body(q_ref, i_ref, f_ref, lb_ref, o_ref, st_ref, s_ref):
        j = pl.program_id(0)

        @pl.when(j == 0)
        def _():
            s_ref[...] = jnp.zeros_like(s_ref)

        s0 = s_ref[...]
        st_ref[...] = s0
        o, s1 = _hg_step(s0, q_ref[...], i_ref[...], f_ref[...], lb_ref[...], rev)
        o_ref[...] = o
        s_ref[...] = s1

    def rows(cb):
        return pl.BlockSpec((STEP, w), lambda j: (_scan_order(j, n_ctx, n, rev), cb))

    return pl.pallas_call(
        body, name=name, grid=(n,),
        in_specs=[rows(0), rows(1), rows(2 + d), pl.BlockSpec((2, w), lambda j: (0, 0))],
        out_specs=[rows(0), pl.BlockSpec((None, h, HG_HEAD, HG_HEAD), lambda j: (j, 0, 0, 0))],
        out_shape=[jax.ShapeDtypeStruct((t, w), F32), jax.ShapeDtypeStruct((n, h, HG_HEAD, HG_HEAD), F32)],
        scratch_shapes=[pltpu.VMEM((h, HG_HEAD, HG_HEAD), F32)],
        compiler_params=_params(("arbitrary",)),
    )(p_hg, p_hg, p_hg, lb2)


def _hg_scan_bwd(name, p_hg, lb2, states, do, d, n_ctx):
    t, w5 = p_hg.shape
    w = w5 // 5
    h = w // HG_HEAD
    n = t // STEP
    rev = d == 1

    def body(q_ref, i_ref, f_ref, lb_ref, st_ref, do_ref, dq_ref, di_ref, df_ref, dlb_ref, ds_ref):
        step = pl.program_id(0)

        @pl.when(step == 0)
        def _():
            ds_ref[...] = jnp.zeros_like(ds_ref)
            dlb_ref[...] = jnp.zeros_like(dlb_ref)

        _, vjp = jax.vjp(lambda s0, q, i, f, lb: _hg_step(s0, q, i, f, lb, rev),
                         st_ref[...], q_ref[...], i_ref[...], f_ref[...], lb_ref[...])
        ds0, dq, di, df, dlb = vjp((do_ref[...], ds_ref[...]))
        dq_ref[...] = dq
        di_ref[...] = di
        df_ref[...] = df
        dlb_ref[...] += dlb
        ds_ref[...] = ds0

    def rows(cb):
        return pl.BlockSpec((STEP, w), lambda s: (_scan_order(n - 1 - s, n_ctx, n, rev), cb))

    return pl.pallas_call(
        body, name=name, grid=(n,),
        in_specs=[rows(0), rows(1), rows(2 + d), pl.BlockSpec((2, w), lambda s: (0, 0)),
                  pl.BlockSpec((None, h, HG_HEAD, HG_HEAD), lambda s: (n - 1 - s, 0, 0, 0)), rows(0)],
        out_specs=[rows(0), rows(0), rows(0), pl.BlockSpec((2, w), lambda s: (0, 0))],
        out_shape=[jax.ShapeDtypeStruct((t, w), F32)] * 3 + [jax.ShapeDtypeStruct((2, w), F32)],
        scratch_shapes=[pltpu.VMEM((h, HG_HEAD, HG_HEAD), F32)],
        compiler_params=_params(("arbitrary",)),
    )(p_hg, p_hg, p_hg, lb2, states, do)


def _to_heads(a, nh):
    return jnp.stack([a[:, i * RW_HEAD:(i + 1) * RW_HEAD] for i in range(nh)], axis=0)


def _from_heads(a):
    return jnp.concatenate([a[i] for i in range(a.shape[0])], axis=-1)


def _rw_scan_fwd(name, sh, hp, d, n_ctx):
    t = sh.shape[0]
    w = (sh.shape[1] - 4 * RW_LORA) // 3
    nh = w // RW_HEAD
    n = t // RW_STEP
    rev = d == 1
    lo = 3 * w // LANE

    def body(r_ref, k_ref, v_ref, wl_ref, al_ref, w0_ref, w2_ref, a0_ref, a2_ref, kk_ref, ka_ref,
             y_ref, st_ref, s_ref):
        j = pl.program_id(0)

        @pl.when(j == 0)
        def _():
            s_ref[...] = jnp.zeros_like(s_ref)

        s0 = s_ref[...]
        st_ref[...] = s0
        wl = wl_ref[...][:, d * RW_LORA:(d + 1) * RW_LORA]
        al = al_ref[...][:, d * RW_LORA:(d + 1) * RW_LORA]
        y, s1 = _rw_step(s0, _to_heads(r_ref[...], nh), _to_heads(k_ref[...], nh), _to_heads(v_ref[...], nh), wl, al,
                         w0_ref[...], w2_ref[...], a0_ref[...], a2_ref[...], kk_ref[...], ka_ref[...], rev)
        y_ref[...] = _from_heads(y)
        s_ref[...] = s1

    def rows(cb, width=w):
        return pl.BlockSpec((RW_STEP, width), lambda j: (_scan_order(j, n_ctx, n, rev), cb))

    def whole(a):
        nd = a.ndim
        return pl.BlockSpec(a.shape, lambda j: (0,) * nd)

    return pl.pallas_call(
        body, name=name, grid=(n,),
        in_specs=[rows(0), rows(1), rows(2), rows(lo, LANE), rows(lo + 1, LANE)] + [whole(a) for a in hp],
        out_specs=[rows(0), pl.BlockSpec((None, nh, RW_HEAD, RW_HEAD), lambda j: (j, 0, 0, 0))],
        out_shape=[jax.ShapeDtypeStruct((t, w), F32), jax.ShapeDtypeStruct((n, nh, RW_HEAD, RW_HEAD), F32)],
        scratch_shapes=[pltpu.VMEM((nh, RW_HEAD, RW_HEAD), F32)],
        compiler_params=_params(("arbitrary",)),
    )(sh, sh, sh, sh, sh, *hp)


def _rw_scan_bwd_both(name, sh, hps, states, dy, n_ctx):
    t = sh.shape[0]
    w = (sh.shape[1] - 4 * RW_LORA) // 3
    nh = w // RW_HEAD
    n = t // RW_STEP
    lo = 3 * w // LANE
    n_in, n_p = 13, 6

    def body(*refs):
        step = pl.program_id(0)
        ins = [refs[d * n_in:(d + 1) * n_in] for d in range(2)]
        outs = [refs[2 * n_in + d * (1 + n_p):2 * n_in + (d + 1) * (1 + n_p)] for d in range(2)]
        ds_refs = refs[2 * n_in + 2 * (1 + n_p):]

        @pl.when(step == 0)
        def _():
            for d in range(2):
                ds_refs[d][...] = jnp.zeros_like(ds_refs[d])
                for ref in outs[d][1:]:
                    ref[...] = jnp.zeros_like(ref)

        for d in range(2):
            r_ref, k_ref, v_ref, wl_ref, al_ref = ins[d][:5]
            hp_refs, st_ref, dy_ref = ins[d][5:11], ins[d][11], ins[d][12]
            wl = wl_ref[...][:, d * RW_LORA:(d + 1) * RW_LORA]
            al = al_ref[...][:, d * RW_LORA:(d + 1) * RW_LORA]
            _, vjp = jax.vjp(functools.partial(_rw_step, rev=d == 1),
                             st_ref[...], _to_heads(r_ref[...], nh), _to_heads(k_ref[...], nh), _to_heads(v_ref[...], nh),
                             wl, al, *[p[...] for p in hp_refs])
            g = vjp((_to_heads(dy_ref[...], nh), ds_refs[d][...]))
            ds_refs[d][...] = g[0]
            zero = jnp.zeros_like(g[4])
            lora = [zero] * 4
            lora[d], lora[2 + d] = g[4], g[5]
            outs[d][0][...] = jnp.concatenate([_from_heads(g[1]), _from_heads(g[2]), _from_heads(g[3])] + lora, axis=-1)
            for ref, val in zip(outs[d][1:], g[6:]):
                ref[...] += val

    def rows(d, cb, width=w):
        return pl.BlockSpec((RW_STEP, width), lambda s: (_scan_order(n - 1 - s, n_ctx, n, d == 1), cb))

    def whole(a):
        nd = a.ndim
        return pl.BlockSpec(a.shape, lambda s: (0,) * nd)

    in_specs, operands, out_specs, out_shape = [], [], [], []
    for d in range(2):
        in_specs += [rows(d, 0), rows(d, 1), rows(d, 2), rows(d, lo, LANE), rows(d, lo + 1, LANE)]
        in_specs += [whole(a) for a in hps[d]]
        in_specs += [pl.BlockSpec((None, nh, RW_HEAD, RW_HEAD), lambda s: (n - 1 - s, 0, 0, 0)), rows(d, 0)]
        operands += [sh] * 5 + list(hps[d]) + [states[d], dy]
        out_specs += [rows(d, 0, sh.shape[1])] + [whole(a) for a in hps[d]]
        out_shape += [jax.ShapeDtypeStruct(sh.shape, F32)] + [jax.ShapeDtypeStruct(a.shape, F32) for a in hps[d]]
    res = pl.pallas_call(
        body, name=name, grid=(n,), in_specs=in_specs, out_specs=out_specs, out_shape=out_shape,
        scratch_shapes=[pltpu.VMEM((nh, RW_HEAD, RW_HEAD), F32)] * 2, compiler_params=_params(("arbitrary",)),
    )(*operands)
    return [res[0], res[1 + n_p]], [res[1:1 + n_p], res[2 + n_p:]]


def _shift_masks(t, n_ctx_rows):
    row = lax.broadcasted_iota(jnp.int32, (t, 1), 0)
    isx = row >= n_ctx_rows
    pos = jnp.where(isx, row - n_ctx_rows, row)
    col = jnp.where(isx, jnp.bitwise_and(pos, GRID_W - 1), pos)
    ncol = jnp.where(isx, GRID_W, n_ctx_rows)
    n_x = t - n_ctx_rows
    ml = col != 0
    mr = col != ncol - 1
    mu = isx & (pos >= GRID_W)
    md = isx & (pos < n_x - GRID_W)
    return ml, mr, mu, md, isx


def _shift_fwd(name, p, mu, n_ctx_rows):
    t, c = p.shape
    cw = LANE

    def body(p_ref, mu_ref, o_ref):
        x = p_ref[...]
        m = mu_ref[...]
        ml, mr, mup, mdn, isx = _shift_masks(t, n_ctx_rows)
        left = jnp.where(ml, pltpu.roll(x, 1, 0), 0.0)
        right = jnp.where(mr, pltpu.roll(x, t - 1, 0), 0.0)
        up = jnp.where(mup, pltpu.roll(x, GRID_W, 0), 0.0)
        down = jnp.where(mdn, pltpu.roll(x, t - GRID_W, 0), 0.0)
        out = x + m[0:1] * (left - x) + m[1:2] * (right - x)
        vert = m[2:3] * (up - x) + m[3:4] * (down - x)
        o_ref[...] = out + jnp.where(isx, vert, 0.0)

    return pl.pallas_call(
        body, name=name, grid=(c // cw,),
        in_specs=[pl.BlockSpec((t, cw), lambda j: (0, j)), pl.BlockSpec((4, cw), lambda j: (0, j))],
        out_specs=pl.BlockSpec((t, cw), lambda j: (0, j)),
        out_shape=jax.ShapeDtypeStruct((t, c), F32),
        compiler_params=_params(("parallel",)),
    )(p, mu)


def _shift_bwd(name, p, mu, dparts, n_ctx_rows):
    t, c = p.shape
    cw = LANE
    npart = len(dparts)

    def body(*refs):
        p_ref, mu_ref = refs[0], refs[1]
        dp_ref, dmu_ref = refs[2 + npart], refs[3 + npart]
        x = p_ref[...]
        m = mu_ref[...]
        g = refs[2][...]
        for r in refs[3:2 + npart]:
            g = g + r[...]
        ml, mr, mup, mdn, isx = _shift_masks(t, n_ctx_rows)
        left = jnp.where(ml, pltpu.roll(x, 1, 0), 0.0)
        right = jnp.where(mr, pltpu.roll(x, t - 1, 0), 0.0)
        up = jnp.where(mup, pltpu.roll(x, GRID_W, 0), 0.0)
        down = jnp.where(mdn, pltpu.roll(x, t - GRID_W, 0), 0.0)
        gx = jnp.where(isx, g, 0.0)
        dmu_ref[...] = jnp.concatenate([
            jnp.sum(g * (left - x), axis=0, keepdims=True), jnp.sum(g * (right - x), axis=0, keepdims=True),
            jnp.sum(gx * (up - x), axis=0, keepdims=True), jnp.sum(gx * (down - x), axis=0, keepdims=True)], axis=0)
        coef = 1.0 - m[0:1] - m[1:2] - jnp.where(isx, m[2:3] + m[3:4], 0.0)
        dp = coef * g
        dp = dp + m[0:1] * pltpu.roll(jnp.where(ml, g, 0.0), t - 1, 0)
        dp = dp + m[1:2] * pltpu.roll(jnp.where(mr, g, 0.0), 1, 0)
        dp = dp + m[2:3] * pltpu.roll(jnp.where(mup, g, 0.0), t - GRID_W, 0)
        dp = dp + m[3:4] * pltpu.roll(jnp.where(mdn, g, 0.0), GRID_W, 0)
        dp_ref[...] = dp

    col = pl.BlockSpec((t, cw), lambda j: (0, j))
    par = pl.BlockSpec((4, cw), lambda j: (0, j))
    return pl.pallas_call(
        body, name=name, grid=(c // cw,),
        in_specs=[col, par] + [col] * npart,
        out_specs=[col, par],
        out_shape=[jax.ShapeDtypeStruct((t, c), F32), jax.ShapeDtypeStruct((4, c), F32)],
        compiler_params=_params(("parallel",)),
    )(p, mu, *dparts)


def _local_step(x, c, ctx, c_ctx, ada_st, ada_b, norm_g, w_in_st, hg_lb, hg_norm_g, rw_mu, rw_w0, rw_w2, rw_a0, rw_a2,
                rw_kk, rw_ka, rw_rk, rw_gn_g, rw_gn_b, w_hg_st, w_rw_st, w_out, final_g, tgt, my_core):
    seq, dm = x.shape
    n_ctx_rows = ctx.shape[0]
    t = seq + n_ctx_rows
    hw = hg_norm_g.shape[-1]
    rw = rw_kk.shape[-1]
    nh_rw = rw // RW_HEAD
    n_ctx = n_ctx_rows // STEP
    tm = _tile(n_ctx_rows, (256, 128, 64))
    nt = t // tm
    nct = n_ctx_rows // tm
    n_sh_cols = 3 * rw + 4 * RW_LORA

    cond = jnp.concatenate([c.reshape(1, dm), c_ctx.reshape(1, dm), jnp.zeros((6, dm), F32)], axis=0)
    final_g2 = final_g.reshape(1, dm)

    def unstack(a_st):
        return jnp.swapaxes(a_st, 0, 1).reshape(a_st.shape[1], -1)

    def restack(a, ns=N_SHARD):
        return jnp.swapaxes(a.reshape(a.shape[0], ns, -1), 0, 1)

    (sc,) = _row_call("cond_silu", lambda i, r, f: ([jax.nn.silu(r[0])], []), 1, 8, [(cond, 0, dm, 0)], [],
                      [(8, dm, F32, 0)], [])
    mod_mm = _mm_n_st("mod_mm", sc, ada_st, joined=True)

    def add(name, terms, shape):
        flat2 = [a.reshape(-1, a.shape[-1]) for a in terms]
        return _rowwise(name, lambda *v: _slot_sum(list(v)), flat2, F32).reshape(shape)

    mod = add("mod_bias", [mod_mm, jnp.broadcast_to(ada_b, (8, 3 * dm))], (8, 3 * dm))
    mod3 = mod.reshape(8, 3, dm)

    def pick(i, m3):
        r = jnp.where(i < nct, m3[1], m3[0])
        return r[0:1], r[1:2]

    tokens = [(ctx, 0, dm, 0), (x, 0, dm, nct)]

    def h_fn(i, r, f):
        shift, scale = pick(i, f[1])
        return [_fn_h(jnp.where(i < nct, r[0], r[1]), f[0], scale, shift)], []

    (h,) = _row_call("h_fwd", h_fn, nt, tm, tokens, [norm_g, mod3], [(t, dm, BF16, 0)], [])
    proj = unstack(_mm_n_st("proj_mm", h, w_in_st))
    p_hg = proj[:, :5 * hw]
    p_rs = proj[:, 5 * hw:5 * hw + n_sh_cols]
    p_zr = proj[:, 5 * hw + n_sh_cols:5 * hw + n_sh_cols + rw]
    p_gt = proj[:, 5 * hw + n_sh_cols + rw:]

    o_hg, st_hg = [], []
    for d in range(2):
        o, st = _hg_scan_fwd(f"hg_scan_fwd{d}", p_hg, hg_lb[d], d, n_ctx)
        o_hg.append(o)
        st_hg.append(st)

    def hgpost_fn(i, r, f):
        return [_fn_hgpost(r[0], r[1], r[2], f[0])], []

    hg_in = [(o_hg[0], 0, hw, 0), (o_hg[1], 0, hw, 0), (p_hg, 4, hw, 0)]
    (y_hg,) = _row_call("hg_post", hgpost_fn, nt, tm, hg_in, [hg_norm_g], [(t, hw, BF16, 0)], [])

    sh = _shift_fwd("rw_shift", p_rs, rw_mu, n_ctx_rows)
    hps = []
    for d in range(2):
        hps.append([rw_w0[d].reshape(nh_rw, 1, RW_HEAD), jnp.swapaxes(rw_w2[d].reshape(RW_LORA, nh_rw, RW_HEAD), 0, 1),
                    rw_a0[d].reshape(nh_rw, 1, RW_HEAD), jnp.swapaxes(rw_a2[d].reshape(RW_LORA, nh_rw, RW_HEAD), 0, 1),
                    rw_kk.reshape(nh_rw, 1, RW_HEAD), rw_ka.reshape(nh_rw, 1, RW_HEAD)])
    y_rw_d, st_rw = [], []
    for d in range(2):
        y, st = _rw_scan_fwd(f"rw_scan_fwd{d}", sh, hps[d], d, n_ctx_rows // RW_STEP)
        y_rw_d.append(y)
        st_rw.append(st)

    rw_full = [rw_a0, rw_a2, rw_ka, rw_rk, rw_gn_g, rw_gn_b]
    lo = 3 * rw // LANE
    rw_in = [(y_rw_d[0], 0, rw, 0), (y_rw_d[1], 0, rw, 0), (sh, 0, rw, 0), (sh, 1, rw, 0), (sh, 2, rw, 0),
             (sh, lo + 1, LANE, 0), (p_zr, 0, rw, 0)]

    def rwpost_fn(i, r, f):
        return [_fn_rwpost(*r, *f)], []

    (y_rw,) = _row_call("rw_post", rwpost_fn, nt, tm, rw_in, rw_full, [(t, rw, BF16, 0)], [])

    a_hg = _mm_n_st("hg_out_mm", y_hg, w_hg_st, joined=True)
    a_rw = _mm_n_st("rw_out_mm", y_rw, w_rw_st, joined=True)
    mg_in = [(a_hg, 0, dm, 0), (a_rw, 0, dm, 0), (p_gt, 0, dm, 0), (p_gt, 1, dm, 0)]
    (merged,) = _row_call("merge", lambda i, r, f: ([_fn_merge(*r)], []), nt, tm, mg_in, [], [(t, dm, BF16, 0)], [])
    o_out = _mm_nn("out_mm", merged, w_out)

    def final_fn(i, r, f):
        gate = f[0][0][2:3]
        loss, vjp = jax.vjp(_fn_final, r[0], r[1], gate, f[1], r[2])
        dx, do, dgate, dfg, _ = vjp(jnp.ones((), F32))
        live = i >= nct
        zero = lambda a: jnp.where(live, a, 0.0)
        dmod = jnp.concatenate([jnp.concatenate([jnp.zeros((1, 2 * dm), F32), zero(dgate)], axis=1),
                                jnp.zeros((7, 3 * dm), F32)], axis=0)
        return [zero(dx), zero(do)], [jnp.broadcast_to(zero(loss), (8, LANE)), dmod, zero(dfg)]

    fin_in = [(x, 0, dm, nct), (o_out, 0, dm, 0), (tgt, 0, dm, nct)]
    dx_res, d_o, loss_acc, dmod_gate, d_final_g = _row_call(
        "final", final_fn, nt, tm, fin_in, [mod3, final_g2], [(t, dm, F32, 0), (t, dm, BF16, 0)],
        [((8, LANE), F32), ((8, 3 * dm), F32), ((1, dm), F32)])

    g_w_out = _mm_tn("d_w_out", merged, d_o)
    d_merged = _mm_nt("d_merged", d_o, w_out)

    def merge_bwd(i, r, f):
        _, vjp = jax.vjp(_fn_merge, r[0], r[1], r[2], r[3])
        da, db, dgh, dgr = vjp(r[4])
        return [da, db, jnp.concatenate([dgh, dgr], axis=1)], []

    da_hg, da_rw, dp_gt = _row_call("merge_bwd", merge_bwd, nt, tm, mg_in + [(d_merged, 0, dm, 0)], [],
                                    [(t, dm, BF16, 0), (t, dm, BF16, 0), (t, 2 * dm, F32, 0)], [])
    g_w_hg_st = _mm_t_st("d_w_hg", y_hg, da_hg)
    g_w_rw_st = _mm_t_st("d_w_rw", y_rw, da_rw)
    dy_hg = _mm_st_t("d_y_hg", da_hg, w_hg_st)
    dy_rw = _mm_st_t("d_y_rw", da_rw, w_rw_st)

    def hgpost_bwd(i, r, f):
        _, vjp = jax.vjp(_fn_hgpost, r[0], r[1], r[2], f[0])
        dof, _, dz, dg = vjp(r[3])
        return [dof, dz], [dg]

    do_hg, dz_hg, g_hg_norm = _row_call("hg_post_bwd", hgpost_bwd, nt, tm, hg_in + [(dy_hg, 0, hw, 0)], [hg_norm_g],
                                        [(t, hw, F32, 0), (t, hw, F32, 0)], [((1, hw), F32)])
    dqs, dis, dfs, g_lb = [], [], [], []
    for d in range(2):
        dq, di, df, dlb = _hg_scan_bwd(f"hg_scan_bwd{d}", p_hg, hg_lb[d], st_hg[d], do_hg, d, n_ctx)
        dqs.append(dq)
        dis.append(di)
        dfs.append(df)
        g_lb.append(dlb)
    (dqi,) = _row_call("hg_dsum", lambda i, r, f: ([jnp.concatenate([r[0] + r[1], r[2] + r[3]], axis=1)], []), nt, tm,
                       [(dqs[0], 0, hw, 0), (dqs[1], 0, hw, 0), (dis[0], 0, hw, 0), (dis[1], 0, hw, 0)], [],
                       [(t, 2 * hw, F32, 0)], [])
    g_hg_lb = jnp.stack(g_lb, axis=0)

    def rwpost_bwd(i, r, f):
        _, vjp = jax.vjp(_fn_rwpost, *r[:7], *f)
        g = vjp(r[7])
        zl = jnp.zeros((g[5].shape[0], 2 * RW_LORA), F32)
        return [g[0], jnp.concatenate([g[2], g[3], g[4], zl, g[5]], axis=1), g[6]], list(g[7:])

    dy_sum, dsh_p, dz_rw, g_a0_p, g_a2_p, g_ka_p, g_rk, g_gn_g, g_gn_b = _row_call(
        "rw_post_bwd", rwpost_bwd, nt, tm, rw_in + [(dy_rw, 0, rw, 0)], rw_full,
        [(t, rw, F32, 0), (t, n_sh_cols, F32, 0), (t, rw, F32, 0)], [(a.shape, F32) for a in rw_full])
    dsh_dirs, hp_grads = _rw_scan_bwd_both("rw_scan_bwd", sh, hps, st_rw, dy_sum, n_ctx_rows // RW_STEP)
    dp_rs, g_mu = _shift_bwd("rw_shift_bwd", p_rs, rw_mu, [dsh_p] + dsh_dirs, n_ctx_rows)

    def flat(a):
        if a.shape[1] == 1:
            return a.reshape(rw)
        return jnp.swapaxes(a, 0, 1).reshape(RW_LORA, rw)

    g_w0 = jnp.stack([flat(hp_grads[d][0]) for d in range(2)], axis=0)
    g_w2 = jnp.stack([flat(hp_grads[d][1]) for d in range(2)], axis=0)
    g_a0 = add("g_a0", [jnp.stack([flat(hp_grads[d][2]) for d in range(2)], axis=0), g_a0_p], (2, rw))
    g_a2 = add("g_a2", [jnp.stack([flat(hp_grads[d][3]) for d in range(2)], axis=0), g_a2_p], (2, RW_LORA, rw))
    g_kk = add("g_kk", [flat(hp_grads[0][4]).reshape(1, rw), flat(hp_grads[1][4]).reshape(1, rw)], (1, rw))
    g_ka = add("g_ka", [flat(hp_grads[0][5]).reshape(1, rw), flat(hp_grads[1][5]).reshape(1, rw), g_ka_p], (1, rw))

    dproj = jnp.concatenate([dqi, dfs[0], dfs[1], dz_hg, dp_rs, dz_rw, dp_gt], axis=1).astype(BF16)
    dproj_st = restack(dproj)
    early = {"w_hg_out": g_w_hg_st, "w_rw_out": g_w_rw_st, "w_out": g_w_out.reshape(N_SHARD, dm // N_SHARD, dm)}
    early_chip = [_pair_exchange(f"grads_pair_sum_{n}", a, my_core, True, BF16) for n, a in early.items()]
    g_w_in_st, early_landed = _mm_t_st("d_w_in", h, dproj_st, scatter=tuple(early_chip))
    w_in_chip = _pair_exchange("grads_pair_sum_w_in", g_w_in_st, my_core, True, BF16)
    dh, (w_in_landed,) = _mm_st_t("d_h", dproj_st, w_in_st, scatter=(w_in_chip,))

    def h_bwd(i, r, f):
        shift, scale = pick(i, f[1])
        is_ctx = i < nct
        _, vjp = jax.vjp(_fn_h, jnp.where(is_ctx, r[0], r[1]), f[0], scale, shift)
        ds, dg, dscale, dshift = vjp(r[2])
        row = jnp.concatenate([dshift, dscale, jnp.zeros((1, dm), F32)], axis=1)
        z = jnp.zeros_like(row)
        dmod = jnp.concatenate([jnp.where(is_ctx, z, row), jnp.where(is_ctx, row, z), jnp.zeros((6, 3 * dm), F32)], axis=0)
        return [ds + r[3]], [dg, dmod]

    grad_x, g_norm_g, dmod_h = _row_call(
        "h_bwd", h_bwd, nt, tm, tokens + [(dh, 0, dm, 0), (dx_res, 0, dm, 0)], [norm_g, mod3],
        [(seq, dm, F32, nct)], [((1, dm), F32), ((8, 3 * dm), F32)])
    dmod = add("d_mod", [dmod_h, dmod_gate], (8, 3 * dm))
    g_ada_b = add("g_ada_b", [dmod[0:1], dmod[1:2]], (1, 3 * dm))
    g_ada_st = _mm_t_st("d_ada_w", sc, dmod)
    d_sc = _mm_st_t("d_cond", dmod, ada_st)

    def cond_bwd(i, r, f):
        _, vjp = jax.vjp(jax.nn.silu, r[0])
        return [vjp(r[1])[0]], []

    (d_cond,) = _row_call("cond_bwd", cond_bwd, 1, 8, [(cond, 0, dm, 0), (d_sc, 0, dm, 0)], [], [(8, dm, F32, 0)], [])

    grads = dict(
        c_ctx=d_cond[1], ada_w=g_ada_st, ada_b=g_ada_b, norm_g=g_norm_g, w_in=(w_in_chip, w_in_landed), hg_lb=g_hg_lb,
        hg_norm_g=g_hg_norm, rw_mu=g_mu, rw_w0=g_w0, rw_w2=g_w2, rw_a0=g_a0, rw_a2=g_a2, rw_kk=g_kk, rw_ka=g_ka,
        rw_rk=g_rk, rw_gn_g=g_gn_g, rw_gn_b=g_gn_b, final_g=d_final_g.reshape(dm))
    grads.update(zip(early, zip(early_chip, early_landed)))
    return loss_acc[0:1, 0:1], grad_x, grads


def _my_place():
    return lax.axis_index("x"), lax.axis_index("y"), lax.axis_index("c")


MIN_CHUNK_BYTES = 1 << 18
ROW_ALIGN = 16


def _n_chunks(rows, row_bytes):
    for n in (8, 4, 2):
        if rows % (n * ROW_ALIGN) == 0 and rows // n * row_bytes >= MIN_CHUNK_BYTES:
            return n
    return 1


def _row_bytes(a, lead=1):
    n = a.dtype.itemsize
    for d in a.shape[lead:]:
        n *= d
    return n


def _rows(ref, start, size):
    return ref.at[pl.ds(start, size)]


def _chunked(make, start, size, n):
    cs = size // n
    return [make(start + j * cs, cs) for j in range(n)]


_PEER_CHIPS = 3


def _weights_gather(name, big, small):
    nb, na = len(big), len(big) + len(small)
    arrays = list(big) + list(small)
    n_ici = 6

    def body(*refs):
        outs = refs[na:2 * na]
        send_sems, recv_sems, fsend_sems, frecv_sems = refs[2 * na:]
        x, y, c = _my_place()
        me, sx, sy, sd = 2 * x + y, 2 * (1 - x) + y, 2 * x + (1 - y), 2 * (1 - x) + (1 - y)
        kx, ky, kd = (1 - x, y, c), (x, 1 - y, c), (1 - x, 1 - y, c)

        def ici(a, j, src_slot, dst_slot, to, r0, nr):
            return pltpu.make_async_remote_copy(
                src_ref=_rows(outs[a].at[src_slot], r0, nr), dst_ref=_rows(outs[a].at[dst_slot], r0, nr),
                send_sem=send_sems.at[a, j], recv_sem=recv_sems.at[a, j], device_id=to,
                device_id_type=pl.DeviceIdType.MESH)

        def to_sibling(a, k, slot, r0, nr):
            rows = _rows(outs[a].at[slot], r0, nr)
            return pltpu.make_async_remote_copy(
                src_ref=rows, dst_ref=rows, send_sem=fsend_sems.at[a, k], recv_sem=frecv_sems.at[a, k],
                device_id=(x, y, 1 - c), device_id_type=pl.DeviceIdType.MESH)

        def start(copies):
            for cp in copies:
                cp.start()

        geo = []
        for a in range(nb):
            half = arrays[a].shape[1] // 2
            geo.append((pl.multiple_of(c * half, ROW_ALIGN), pl.multiple_of((1 - c) * half, ROW_ALIGN), half // 2,
                        _n_chunks(half // 2, _row_bytes(arrays[a], 2))))
        plan = [(me, sx, kx, 0), (me, sx, kx, 1), (me, sy, ky, 0), (me, sy, ky, 1), (sx, sd, ky, 0), (sy, sd, kx, 1)]

        def piece(a, j):
            return geo[a][0] + plan[j][3] * geo[a][2]

        for a in range(nb):
            for j in range(4):
                start(_chunked(lambda r0, cs: ici(a, j, me, me, plan[j][2], r0, cs), piece(a, j), geo[a][2], geo[a][3]))
        for a in range(nb, na):
            rows = arrays[a].shape[1]
            for j, to in ((0, kx), (2, ky), (1, kd)):
                ici(a, j, me, me, to, 0, rows).start()
        for a in range(nb):
            for j, first in ((4, 0), (5, 3)):
                src_slot, _, to, _ = plan[j]
                ici(a, first, me, plan[first][1], plan[first][2], piece(a, first), geo[a][2]).wait_recv()
                start(_chunked(lambda r0, cs: ici(a, j, src_slot, src_slot, to, r0, cs), piece(a, j), geo[a][2], geo[a][3]))
        for a in range(nb):
            for j in (1, 2):
                ici(a, j, me, plan[j][1], plan[j][2], piece(a, j), geo[a][2]).wait_recv()
            for k, slot in ((0, sx), (1, sy)):
                start(_chunked(lambda r0, cs: to_sibling(a, k, slot, r0, cs), geo[a][0], 2 * geo[a][2], geo[a][3]))
        for a in range(nb):
            for j in (4, 5):
                ici(a, j, me, sd, plan[j][2], piece(a, j), geo[a][2]).wait_recv()
            start(_chunked(lambda r0, cs: to_sibling(a, 2, sd, r0, cs), geo[a][0], 2 * geo[a][2], geo[a][3]))
        for a in range(nb, na):
            rows = arrays[a].shape[1]
            for j, slot, to in ((0, sx, kx), (2, sy, ky), (1, sd, kd)):
                ici(a, j, me, slot, to, 0, rows).wait_recv()
        for a in range(nb):
            for k, slot in ((0, sx), (1, sy), (2, sd)):
                to_sibling(a, k, slot, geo[a][1], 2 * geo[a][2]).wait_recv()
        for a in range(nb):
            for j in range(n_ici):
                ici(a, j, me, me, plan[j][2], piece(a, j), geo[a][2]).wait_send()
            for k, slot in ((0, sx), (1, sy), (2, sd)):
                to_sibling(a, k, slot, geo[a][0], 2 * geo[a][2]).wait_send()
        for a in range(nb, na):
            rows = arrays[a].shape[1]
            for j, to in ((0, kx), (2, ky), (1, kd)):
                ici(a, j, me, me, to, 0, rows).wait_send()

    hbm = pl.BlockSpec(memory_space=pl.ANY)
    ici_sems = pltpu.SemaphoreType.DMA((na, n_ici))
    pair_sems = pltpu.SemaphoreType.DMA((na, _PEER_CHIPS))
    return pl.pallas_call(
        body, name=name, in_specs=[hbm] * na, out_specs=[hbm] * na,
        out_shape=[jax.ShapeDtypeStruct(a.shape, a.dtype) for a in arrays],
        input_output_aliases={a: a for a in range(na)}, scratch_shapes=[ici_sems, ici_sems, pair_sems, pair_sems],
    )(*arrays)


def _scatter_copy(arrays, ins, outs, send_sems, recv_sems, a, k, slot, r0, nr):
    x, y, c = _my_place()
    px, py = [(1 - x, y), (x, 1 - y), (1 - x, 1 - y)][k]
    return pltpu.make_async_remote_copy(
        src_ref=_rows(ins[a].at[2 * px + py], r0, nr), dst_ref=_rows(outs[a].at[slot], r0, nr),
        send_sem=send_sems.at[a, k], recv_sem=recv_sems.at[a, k], device_id=(px, py, c),
        device_id_type=pl.DeviceIdType.MESH)


def _scatter_start(arrays, ins, outs, send_sems, recv_sems):
    x, y, _ = _my_place()
    for a in range(len(arrays)):
        rows = arrays[a].shape[1]
        for k in range(_PEER_CHIPS):
            for cp in _chunked(lambda r0, cs: _scatter_copy(arrays, ins, outs, send_sems, recv_sems, a, k, 2 * x + y, r0, cs),
                               0, rows, _n_chunks(rows, _row_bytes(arrays[a], 2))):
                cp.start()


def _scatter_wait(arrays, ins, outs, send_sems, recv_sems):
    x, y, _ = _my_place()
    peer_slot = [2 * (1 - x) + y, 2 * x + (1 - y), 2 * (1 - x) + (1 - y)]
    for k in range(_PEER_CHIPS):
        for a in range(len(arrays)):
            _scatter_copy(arrays, ins, outs, send_sems, recv_sems, a, k, peer_slot[k], 0, arrays[a].shape[1]).wait_recv()
    for a in range(len(arrays)):
        for k in range(_PEER_CHIPS):
            _scatter_copy(arrays, ins, outs, send_sems, recv_sems, a, k, 2 * x + y, 0, arrays[a].shape[1]).wait_send()


PAIR_TILE_BYTES = 2 << 20


def _pair_exchange(name, a, place, reduce, out_dtype):
    rows, cols = a.shape[-2], a.shape[-1]
    half = rows // 2 if reduce else rows
    tr = _row_tile_for(half, cols, budget=PAIR_TILE_BYTES)
    nh = half // tr
    n_steps = (N_SHARD if reduce else 1) * nh

    def body(pc_ref, *refs):
        if reduce:
            keep_ref, send_ref, o_ref, land, send_sems, recv_sems, credit, wire = refs
            wire[...] = send_ref[...].astype(BF16)
            src = wire
        else:
            send_ref, o_ref, land, send_sems, recv_sems, credit = refs
            src = send_ref
        x, y, c = _my_place()
        other = (x, y, 1 - c)
        t = pl.program_id(0) * nh + pl.program_id(1) if reduce else pl.program_id(0)
        slot = t % 2

        @pl.when(t >= 2)
        def _():
            pl.semaphore_wait(credit, 1)

        copy = pltpu.make_async_remote_copy(
            src_ref=src, dst_ref=land.at[slot], send_sem=send_sems.at[slot], recv_sem=recv_sems.at[slot],
            device_id=other, device_id_type=pl.DeviceIdType.MESH)
        copy.start()
        copy.wait_recv()
        got = land[slot]
        o_ref[...] = ((keep_ref[...] + got.astype(F32)) if reduce else got).astype(out_dtype)
        copy.wait_send()

        @pl.when(t < n_steps - 2)
        def _():
            pl.semaphore_signal(credit, inc=1, device_id=other, device_id_type=pl.DeviceIdType.MESH)

    if reduce:
        grid = (N_SHARD, nh)
        in_specs = [pl.BlockSpec((None, tr, cols), lambda j, i, pc: (j, pc[0] * nh + i, 0)),
                    pl.BlockSpec((None, tr, cols), lambda j, i, pc: (j, (1 - pc[0]) * nh + i, 0))]
        out_spec = pl.BlockSpec((None, tr, cols), lambda j, i, pc: (j, i, 0))
        out_shape = jax.ShapeDtypeStruct((N_SHARD, half, cols), out_dtype)
        operands = (a, a)
        sem = ("arbitrary", "arbitrary")
    else:
        grid = (nh,)
        in_specs = [pl.BlockSpec((tr, cols), lambda i, pc: (i, 0))]
        out_spec = pl.BlockSpec((tr, cols), lambda i, pc: (i, 0))
        out_shape = jax.ShapeDtypeStruct((half, cols), out_dtype)
        operands = (a,)
        sem = ("arbitrary",)
    return pl.pallas_call(
        body, name=name,
        grid_spec=pltpu.PrefetchScalarGridSpec(
            num_scalar_prefetch=1, grid=grid, in_specs=in_specs, out_specs=out_spec,
            scratch_shapes=[pltpu.VMEM((2, tr, cols), BF16 if reduce else a.dtype), pltpu.SemaphoreType.DMA((2,)),
                            pltpu.SemaphoreType.DMA((2,)), pltpu.SemaphoreType.REGULAR] +
                           ([pltpu.VMEM((tr, cols), BF16)] if reduce else [])),
        out_shape=out_shape, compiler_params=_params(sem),
    )(place, *operands)


def _cast_into_slot(name, a, chip):
    rows, cols = a.shape
    tm = _row_tile_for(rows, cols)

    def body(pc_ref, a_ref, o_ref):
        o_ref[...] = a_ref[...].astype(BF16)

    return pl.pallas_call(
        body, name=name,
        grid_spec=pltpu.PrefetchScalarGridSpec(
            num_scalar_prefetch=1, grid=(rows // tm,), in_specs=[pl.BlockSpec((tm, cols), lambda i, pc: (i, 0))],
            out_specs=pl.BlockSpec((None, tm, cols), lambda i, pc: (pc[0], i, 0))),
        out_shape=jax.ShapeDtypeStruct((N_SHARD, rows, cols), BF16), compiler_params=_params(("parallel",)),
    )(chip, a)


def _sum_landed(name, landed, sent, chip):
    ns, rows, cols = landed.shape
    tm = _row_tile_for(rows, cols)

    def body(pc_ref, *refs):
        own_ref, o_ref = refs[ns], refs[ns + 1]
        me = pc_ref[0]
        terms = [jnp.where(me == j, own_ref[...], refs[j][...]).astype(F32) for j in range(ns)]
        o_ref[...] = _slot_sum(terms)

    def landed_spec(j):
        return pl.BlockSpec((None, tm, cols), lambda i, pc: (jnp.where(pc[0] == j, (j + 1) % ns, j), i, 0))

    return pl.pallas_call(
        body, name=name,
        grid_spec=pltpu.PrefetchScalarGridSpec(
            num_scalar_prefetch=1, grid=(rows // tm,),
            in_specs=[landed_spec(j) for j in range(ns)] + [pl.BlockSpec((None, tm, cols), lambda i, pc: (pc[0], i, 0))],
            out_specs=pl.BlockSpec((tm, cols), lambda i, pc: (i, 0))),
        out_shape=jax.ShapeDtypeStruct((rows, cols), F32), compiler_params=_params(("parallel",)),
    )(chip, *([landed] * ns), sent)


def _gather_all(name, a):
    def body(in_ref, out_ref, send_sems, recv_sems, local_sem):
        x, y, c = _my_place()
        me = 4 * x + 2 * y + c

        def peer(k):
            return (x ^ (k >> 2), y ^ ((k >> 1) & 1), c ^ (k & 1))

        def remote(k, land):
            return pltpu.make_async_remote_copy(
                src_ref=in_ref, dst_ref=out_ref.at[land], send_sem=send_sems.at[k - 1], recv_sem=recv_sems.at[k - 1],
                device_id=peer(k), device_id_type=pl.DeviceIdType.MESH)

        local = pltpu.make_async_copy(in_ref, out_ref.at[me], local_sem)
        local.start()
        for k in range(1, N_DEV):
            remote(k, me).start()
        for k in range(1, N_DEV):
            px, py, pc = peer(k)
            remote(k, 4 * px + 2 * py + pc).wait_recv()
        for k in range(1, N_DEV):
            remote(k, me).wait_send()
        local.wait()

    hbm = pl.BlockSpec(memory_space=pl.ANY)
    return pl.pallas_call(
        body, name=name, in_specs=[hbm], out_specs=hbm,
        out_shape=jax.ShapeDtypeStruct((N_DEV,) + a.shape, a.dtype),
        scratch_shapes=[pltpu.SemaphoreType.DMA((N_DEV - 1,)), pltpu.SemaphoreType.DMA((N_DEV - 1,)), pltpu.SemaphoreType.DMA],
    )(a)


def _row_tile_for(rows, cols, budget=1 << 20):
    if rows * cols * 4 <= budget:
        return rows
    for tm in (1024, 512, 256, 128, 64, 32, 16, 8):
        if rows % tm == 0 and tm * cols * 4 <= budget:
            return tm
    return rows


def _slot_sum(vals):
    g = vals[0]
    for v in vals[1:]:
        g = g + v
    return g


def _rowwise(name, fn, arrays, out_dtype):
    rows, cols = arrays[0].shape
    tm = _row_tile_for(rows, cols)

    def body(*refs):
        refs[-1][...] = fn(*[r[...] for r in refs[:-1]]).astype(out_dtype)

    blk = pl.BlockSpec((tm, cols), lambda i: (i, 0))
    return pl.pallas_call(
        body, name=name, grid=(rows // tm,), in_specs=[blk] * len(arrays), out_specs=blk,
        out_shape=jax.ShapeDtypeStruct((rows, cols), out_dtype), compiler_params=_params(("parallel",)),
    )(*arrays)


def _sum_slots(name, st):
    ns, rows, cols = st.shape
    tm = _row_tile_for(rows, cols)

    def body(s_ref, o_ref):
        o_ref[...] = _slot_sum([s_ref[j].astype(F32) for j in range(ns)])

    return pl.pallas_call(
        body, name=name, grid=(rows // tm,),
        in_specs=[pl.BlockSpec((ns, tm, cols), lambda i: (0, i, 0))],
        out_specs=pl.BlockSpec((tm, cols), lambda i: (i, 0)),
        out_shape=jax.ShapeDtypeStruct((rows, cols), F32),
        compiler_params=_params(("parallel",)),
    )(st)


ADAM_TILE_BYTES = 1 << 19


def _adam_update(g, p_ref, m_ref, v_ref, go_ref, d_ref, mo_ref, vo_ref):
    mn = ADAM_B1 * m_ref[...] + (1.0 - ADAM_B1) * g
    vn = ADAM_B2 * v_ref[...] + (1.0 - ADAM_B2) * jnp.square(g)
    m_hat = mn / (1.0 - ADAM_B1 ** ADAM_STEP)
    v_hat = vn / (1.0 - ADAM_B2 ** ADAM_STEP)
    go_ref[...] = g
    d_ref[...] = -ADAM_LR * (m_hat / (jnp.sqrt(v_hat) + ADAM_EPS) + ADAM_WD * p_ref[...])
    mo_ref[...] = mn
    vo_ref[...] = vn


def _adamw(name, p, m, v, gst):
    rows, cols = p.shape
    ns = gst.shape[0]
    tm = _row_tile_for(rows, cols, budget=ADAM_TILE_BYTES)

    def body(p_ref, m_ref, v_ref, g_ref, *outs):
        _adam_update(_slot_sum([g_ref[j] for j in range(ns)]), p_ref, m_ref, v_ref, *outs)

    blk = pl.BlockSpec((tm, cols), lambda i: (i, 0))
    return pl.pallas_call(
        body, name=name, grid=(rows // tm,),
        in_specs=[blk, blk, blk, pl.BlockSpec((ns, tm, cols), lambda i: (0, i, 0))],
        out_specs=[blk] * 4, out_shape=[jax.ShapeDtypeStruct((rows, cols), F32)] * 4,
        compiler_params=_params(("parallel",)),
    )(p, m, v, gst)


def _adamw_halves(name, p, m, v, mine, theirs, place, scatter=()):
    rows, cols = p.shape
    half = rows // 2
    tm = _row_tile_for(half, cols, budget=ADAM_TILE_BYTES)
    nh = half // tm
    ns = len(scatter)

    def body(pc_ref, p_ref, m_ref, v_ref, mine_ref, theirs_ref, *refs):
        if ns:
            sc_refs = (refs[:ns], refs[ns + 4:2 * ns + 4]) + tuple(refs[2 * ns + 4:])
            at = pl.program_id(0) * nh + pl.program_id(1)
            pl.when(at == 0)(lambda: _scatter_start(scatter, *sc_refs))
        g = jnp.where(pl.program_id(0) == pc_ref[0], mine_ref[...], theirs_ref[...])
        _adam_update(g, p_ref, m_ref, v_ref, *refs[ns:ns + 4])
        if ns:
            pl.when(at == 2 * nh - 1)(lambda: _scatter_wait(scatter, *sc_refs))

    blk = pl.BlockSpec((tm, cols), lambda h, i, pc: (h * nh + i, 0))
    hblk = pl.BlockSpec((tm, cols), lambda h, i, pc: (i, 0))
    hbm = pl.BlockSpec(memory_space=pl.ANY)
    res = pl.pallas_call(
        body, name=name,
        grid_spec=pltpu.PrefetchScalarGridSpec(
            num_scalar_prefetch=1, grid=(2, nh), in_specs=[blk, blk, blk, hblk, hblk] + [hbm] * ns,
            out_specs=[blk] * 4 + [hbm] * ns,
            scratch_shapes=[pltpu.SemaphoreType.DMA((ns, _PEER_CHIPS))] * 2 if ns else []),
        out_shape=[jax.ShapeDtypeStruct((rows, cols), F32)] * 4 + [jax.ShapeDtypeStruct(s.shape, s.dtype) for s in scatter],
        compiler_params=_params(("arbitrary", "arbitrary") if ns else ("parallel", "parallel")),
    )(place, p, m, v, mine, theirs, *scatter)
    return (list(res[:4]), list(res[4:])) if ns else res


def _pack(parts, width=LANE, mult=8):
    flat = jnp.concatenate([a.reshape(-1) for a in parts])
    n = flat.shape[0]
    per = width * mult
    total = -(-n // per) * per
    return jnp.pad(flat, (0, total - n)).reshape(total // width, width)


def _unpack(packed, shapes):
    flat = packed.reshape(-1)
    out, off = [], 0
    for s in shapes:
        n = 1
        for d in s:
            n *= d
        out.append(flat[off:off + n].reshape(s))
        off += n
    return out


_SMALL_SHARDED = ("hg_lb", "rw_mu", "rw_w0", "rw_w2", "rw_a0", "rw_a2")
_REPLICATED = ("c_ctx", "ada_b", "norm_g", "hg_norm_g", "rw_kk", "rw_ka", "rw_rk", "rw_gn_g", "rw_gn_b", "final_g")
_BIG = ("ada_w", "w_in", "w_hg_out", "w_rw_out", "w_out")
_WEIGHTS = ("c_ctx", "ada_w", "ada_b", "norm_g", "w_in", "hg_lb", "hg_norm_g", "rw_mu", "rw_w0", "rw_w2", "rw_a0", "rw_a2",
            "rw_kk", "rw_ka", "rw_rk", "rw_gn_g", "rw_gn_b", "w_hg_out", "w_rw_out", "w_out", "final_g")


def _join_shards(st):
    a = jnp.moveaxis(st, 0, -2)
    return a.reshape(a.shape[:-2] + (a.shape[-2] * a.shape[-1],))


def _split_shards(a):
    s = a.reshape(a.shape[:-1] + (N_SHARD, a.shape[-1] // N_SHARD))
    return jnp.moveaxis(s, -2, 0)


def kernel(x, c, ctx, c_ctx, ada_w, ada_b, norm_g, w_in, hg_lb, hg_norm_g, rw_mu, rw_w0, rw_w2, rw_a0, rw_a2, rw_kk, rw_ka, rw_rk, rw_gn_g, rw_gn_b, w_hg_out, w_rw_out, w_out, final_g, loss_target, m_c_ctx, m_ada_w, m_ada_b, m_norm_g, m_w_in, m_hg_lb, m_hg_norm_g, m_rw_mu, m_rw_w0, m_rw_w2, m_rw_a0, m_rw_a2, m_rw_kk, m_rw_ka, m_rw_rk, m_rw_gn_g, m_rw_gn_b, m_w_hg_out, m_w_rw_out, m_w_out, m_final_g, v_c_ctx, v_ada_w, v_ada_b, v_norm_g, v_w_in, v_hg_lb, v_hg_norm_g, v_rw_mu, v_rw_w0, v_rw_w2, v_rw_a0, v_rw_a2, v_rw_kk, v_rw_ka, v_rw_rk, v_rw_gn_g, v_rw_gn_b, v_w_hg_out, v_w_rw_out, v_w_out, v_final_g):
    w = dict(c_ctx=c_ctx, ada_w=ada_w, ada_b=ada_b, norm_g=norm_g, w_in=w_in, hg_lb=hg_lb, hg_norm_g=hg_norm_g, rw_mu=rw_mu,
             rw_w0=rw_w0, rw_w2=rw_w2, rw_a0=rw_a0, rw_a2=rw_a2, rw_kk=rw_kk, rw_ka=rw_ka, rw_rk=rw_rk, rw_gn_g=rw_gn_g,
             rw_gn_b=rw_gn_b, w_hg_out=w_hg_out, w_rw_out=w_rw_out, w_out=w_out, final_g=final_g)
    m = dict(c_ctx=m_c_ctx, ada_w=m_ada_w, ada_b=m_ada_b, norm_g=m_norm_g, w_in=m_w_in, hg_lb=m_hg_lb, hg_norm_g=m_hg_norm_g,
             rw_mu=m_rw_mu, rw_w0=m_rw_w0, rw_w2=m_rw_w2, rw_a0=m_rw_a0, rw_a2=m_rw_a2, rw_kk=m_rw_kk, rw_ka=m_rw_ka,
             rw_rk=m_rw_rk, rw_gn_g=m_rw_gn_g, rw_gn_b=m_rw_gn_b, w_hg_out=m_w_hg_out, w_rw_out=m_w_rw_out, w_out=m_w_out,
             final_g=m_final_g)
    v = dict(c_ctx=v_c_ctx, ada_w=v_ada_w, ada_b=v_ada_b, norm_g=v_norm_g, w_in=v_w_in, hg_lb=v_hg_lb, hg_norm_g=v_hg_norm_g,
             rw_mu=v_rw_mu, rw_w0=v_rw_w0, rw_w2=v_rw_w2, rw_a0=v_rw_a0, rw_a2=v_rw_a2, rw_kk=v_rw_kk, rw_ka=v_rw_ka,
             rw_rk=v_rw_rk, rw_gn_g=v_rw_gn_g, rw_gn_b=v_rw_gn_b, w_hg_out=v_w_hg_out, w_rw_out=v_w_rw_out, w_out=v_w_out,
             final_g=v_final_g)

    def mat(a):
        return a.reshape(a.shape[-2], a.shape[-1])

    def pack_small(d):
        return _pack([d[n] for n in _SMALL_SHARDED], mult=2 * ROW_ALIGN)

    my_core = lax.axis_index("c").astype(jnp.int32).reshape(1)
    my_chip = (2 * lax.axis_index("x") + lax.axis_index("y")).astype(jnp.int32).reshape(1)

    small_shapes = [w[n].shape for n in _SMALL_SHARDED]
    big_bf = [_cast_into_slot(f"to_bf16_{n}", mat(w[n]), my_chip) for n in _BIG]
    small_mine = pack_small(w)
    small_slots = lax.dynamic_update_slice(jnp.zeros((N_SHARD,) + small_mine.shape, F32), small_mine[None], (my_chip[0], 0, 0))
    gathered = _weights_gather("weights_gather", big_bf, [small_slots])
    ada_st, w_in_st, w_hg_st, w_rw_st, w_out_st, small_st = gathered
    full_small = {}
    per_chip = [_unpack(small_st[j], small_shapes) for j in range(N_SHARD)]
    for i, n in enumerate(_SMALL_SHARDED):
        full_small[n] = _join_shards(jnp.stack([per_chip[j][i] for j in range(N_SHARD)], axis=0))
    dm = x.shape[-1]
    w_out_full = w_out_st.reshape(dm, dm)

    loss_b, grad_x, g = _local_step(
        x[0], c, ctx[0], c_ctx, ada_st, ada_b, norm_g, w_in_st, full_small["hg_lb"], hg_norm_g, full_small["rw_mu"][0],
        full_small["rw_w0"][0], full_small["rw_w2"][0], full_small["rw_a0"][0], full_small["rw_a2"][0], rw_kk, rw_ka, rw_rk,
        rw_gn_g, rw_gn_b, w_hg_st, w_rw_st, w_out_full, final_g, loss_target[0], my_core)
    loss = lax.psum(loss_b[0, 0], ("x", "y", "c"))

    g_small = {"hg_lb": g["hg_lb"], "rw_mu": g["rw_mu"][None], "rw_w0": g["rw_w0"][None], "rw_w2": g["rw_w2"][None],
               "rw_a0": g["rw_a0"][None], "rw_a2": g["rw_a2"][None]}
    split = {n: _split_shards(g_small[n]) for n in _SMALL_SHARDED}
    small_parts = jnp.stack([pack_small({n: split[n][j] for n in _SMALL_SHARDED}) for j in range(N_SHARD)], axis=0)
    def finish(name, chip_sum, landed):
        half = _sum_landed(f"grads_sum_{name}", landed, chip_sum, my_chip)
        return half, _pair_exchange(f"grads_pair_swap_{name}", half, my_core, False, F32)

    res = {}
    later = {"ada_w": g["ada_w"], "small": small_parts}
    later_chip = [_pair_exchange(f"grads_pair_sum_{n}", a, my_core, True, BF16) for n, a in later.items()]
    outs, later_landed = _adamw_halves("adamw_w_in", mat(w["w_in"]), mat(m["w_in"]), mat(v["w_in"]),
                                       *finish("w_in", *g["w_in"]), my_core, scatter=tuple(later_chip))
    res["w_in"] = [o.reshape(w["w_in"].shape) for o in outs]
    pending = {n: g[n] for n in ("w_hg_out", "w_rw_out", "w_out")}
    pending.update(zip(later, zip(later_chip, later_landed)))
    rep_shapes = [w[n].shape for n in _REPLICATED]
    rep_all = _gather_all("grads_replicated", _pack([g[n].reshape(w[n].shape) for n in _REPLICATED]))

    for n in ("ada_w", "w_hg_out", "w_rw_out", "w_out"):
        outs = _adamw_halves(f"adamw_{n}", mat(w[n]), mat(m[n]), mat(v[n]), *finish(n, *pending[n]), my_core)
        res[n] = [o.reshape(w[n].shape) for o in outs]
    outs = _adamw_halves("adamw_small", small_mine, pack_small(m), pack_small(v), *finish("small", *pending["small"]), my_core)
    for i, vals in enumerate(zip(*[_unpack(o, small_shapes) for o in outs])):
        res[_SMALL_SHARDED[i]] = list(vals)
    outs = _adamw("adamw_replicated", _pack([w[n] for n in _REPLICATED]), _pack([m[n] for n in _REPLICATED]),
                  _pack([v[n] for n in _REPLICATED]), rep_all)
    for i, vals in enumerate(zip(*[_unpack(o, rep_shapes) for o in outs])):
        res[_REPLICATED[i]] = list(vals)

    return (loss, grad_x[None], *[res[n][0] for n in _WEIGHTS], *[res[n][1] for n in _WEIGHTS],
            *[res[n][2] for n in _WEIGHTS], *[res[n][3] for n in _WEIGHTS])
```

```python
import functools

import jax
import jax.numpy as jnp
from jax import lax
from jax.experimental import pallas as pl
from jax.experimental.pallas import tpu as pltpu

HI = lax.Precision.HIGHEST
F32 = jnp.float32
BF16 = jnp.bfloat16

NORM_EPS = 1e-6
HG_HEAD = 128
RW_HEAD = 64
RW_LORA = 64
RW_GN_EPS = 64e-5
GRID_W = 64
SUB = 16
STEP = 64
RW_STEP = 64
N_SHARD = 4
N_DEV = 8
LANE = 128

ADAM_LR = 0.001
ADAM_B1 = 0.9
ADAM_B2 = 0.999
ADAM_EPS = 1e-08
ADAM_WD = 0.01
ADAM_STEP = 10

VMEM_LIMIT = 56 * 1024 * 1024


def _params(sem=None):
    return pltpu.CompilerParams(dimension_semantics=sem, vmem_limit_bytes=VMEM_LIMIT)


def _tile(n, cands):
    for c in cands:
        if n % c == 0:
            return c
    return n


def _iota2(n, m, d):
    return lax.broadcasted_iota(jnp.int32, (n, m), d)


def _before(n, rev, strict):
    t, s = _iota2(n, n, 0), _iota2(n, n, 1)
    if rev:
        return (s > t) if strict else (s >= t)
    return (s < t) if strict else (s <= t)


def _bdot(a, b, spec):
    return jnp.einsum(spec, a, b, precision=HI, preferred_element_type=F32)


def _sdot(a, b, spec):
    return jnp.einsum(spec, a, b, precision=lax.Precision.DEFAULT, preferred_element_type=F32)


def _hg_step(s0, qraw, iin, fin, lb2, rev):
    c, w = qraw.shape
    h = w // HG_HEAD
    nsub = c // SUB
    lb = jax.nn.sigmoid(lb2[0:1] - lb2[1:2])
    q = jax.nn.silu(qraw)
    fg = lb + (1.0 - lb) * jax.nn.sigmoid(fin)
    kk = 1.0 - fg
    g = jnp.log(fg)
    bcum = jnp.dot(_before(c, rev, False).astype(F32), g, precision=HI, preferred_element_type=F32)
    def heads(a):
        return jnp.swapaxes(a.reshape(a.shape[0], h, HG_HEAD), 0, 1)

    def unheads(a):
        return jnp.swapaxes(a, 0, 1).reshape(a.shape[1], w)

    blocks = [slice(j * SUB, (j + 1) * SUB) for j in range(nsub)]
    outs = []
    for sl in blocks:
        qs, ks, vs, bc = [a[sl].reshape(SUB, h, HG_HEAD) for a in (q, kk, iin, bcum)]
        o = jnp.zeros((SUB, h, HG_HEAD), F32)
        for si in range(SUB):
            after = slice(0, si + 1) if rev else slice(si, SUB)
            dec = jnp.exp(jnp.minimum(bc[after] - bc[si:si + 1], 0.0))
            a = jnp.sum(qs[after] * ks[si:si + 1] * dec, axis=-1, keepdims=True)
            term = a * vs[si:si + 1]
            n_rest = SUB - 1 - si if rev else si
            if n_rest:
                rest = jnp.zeros((n_rest, h, HG_HEAD), F32)
                term = jnp.concatenate([term, rest] if rev else [rest, term], axis=0)
            o = o + term
        outs.append(o.reshape(SUB, w))
    order = list(range(nsub - 1, -1, -1)) if rev else list(range(nsub))
    for pos in range(1, nsub):
        j, before = order[pos], order[:pos]
        first = (j + 1) * SUB - 1 if rev else j * SUB
        bstart = bcum[first:first + 1] - g[first:first + 1]
        qp = heads(q[blocks[j]] * jnp.exp(bcum[blocks[j]] - bstart))
        kp = heads(jnp.concatenate([kk[blocks[p]] * jnp.exp(bstart - bcum[blocks[p]]) for p in before], axis=0))
        vp = heads(jnp.concatenate([iin[blocks[p]] for p in before], axis=0))
        outs[j] = outs[j] + unheads(_sdot(_sdot(qp, kp, 'htk,hsk->hts'), vp, 'hts,hsv->htv'))
    o_state = unheads(_sdot(heads(q * jnp.exp(bcum)), s0, 'htk,hvk->htv'))
    last = 0 if rev else c - 1
    blast = bcum[last:last + 1]
    s_new = heads(jnp.exp(blast)) * s0 + _sdot(heads(iin), heads(kk * jnp.exp(blast - bcum)), 'hsv,hsk->hvk')
    return jnp.concatenate(outs, axis=0) + o_state, s_new


def _tri_solve(lmat, rhs, rev):
    hh, c, _ = lmat.shape
    nb = c // SUB
    diag = jnp.concatenate([lmat[:, i * SUB:(i + 1) * SUB, i * SUB:(i + 1) * SUB] for i in range(nb)], axis=0)
    dt = jnp.transpose(diag, (1, 2, 0))
    col = lax.broadcasted_iota(jnp.int32, (SUB, 1), 0)
    inv_rows = [None] * SUB
    order = list(range(SUB - 1, -1, -1)) if rev else list(range(SUB))
    for pos, t in enumerate(order):
        row = jnp.broadcast_to((col == t).astype(F32), (SUB, dt.shape[2]))
        for s in order[:pos]:
            row = row - dt[t, s:s + 1, :] * inv_rows[s]
        inv_rows[t] = row
    tinv = jnp.transpose(jnp.concatenate([r[None] for r in inv_rows], axis=0), (2, 0, 1))
    p = [None] * nb
    done = []
    for i in (range(nb - 1, -1, -1) if rev else range(nb)):
        r = rhs[:, i * SUB:(i + 1) * SUB]
        if done:
            lrow = jnp.concatenate([lmat[:, i * SUB:(i + 1) * SUB, m * SUB:(m + 1) * SUB] for m in done], axis=2)
            r = r - _sdot(lrow, jnp.concatenate([p[m] for m in done], axis=1), 'hts,hsv->htv')
        p[i] = _sdot(tinv[i * hh:(i + 1) * hh], r, 'hts,hsv->htv')
        done.append(i)
    return jnp.concatenate(p, axis=1)


def _rw_step(s0, r, k, v, wlo, alo, w0h, w2h, a0h, a2h, kkh, kah, rev):
    hh, c, _ = r.shape
    tl = jnp.broadcast_to(jnp.tanh(wlo)[None], (hh, c, wlo.shape[1]))
    al = jnp.broadcast_to(alo[None], (hh, c, alo.shape[1]))
    wlog = -jax.nn.softplus(-(w0h + _sdot(tl, w2h, 'hcl,hlj->hcj'))) - 0.5
    lw = -jnp.exp(wlog)
    a = jax.nn.sigmoid(a0h + _sdot(al, a2h, 'hcl,hlj->hcj'))
    kk = k * kkh
    kk = kk * lax.rsqrt(jnp.sum(kk * kk, axis=-1, keepdims=True) + 1e-12)
    kd = k * (1.0 + (a - 1.0) * kah)
    b = kk * a
    incl = jnp.broadcast_to(_before(c, rev, False).astype(F32)[None], (hh, c, c))
    cum = _bdot(incl, lw, 'hts,hsk->htk')
    ecum, encum = jnp.exp(cum), jnp.exp(-cum)
    alpha = jnp.exp(cum - lw) * kk
    beta = b * encum
    kappa = kd * encum
    rho = r * ecum
    m_lt = _before(c, rev, True)[None]
    m_le = _before(c, rev, False)[None]
    ar = jnp.concatenate([alpha, rho], axis=1)
    kb = jnp.concatenate([kappa, beta], axis=1)
    gram = _sdot(ar, kb, 'htk,hsk->hts')
    a_kap = jnp.where(m_lt, gram[:, :c, :c], 0.0)
    a_bet = jnp.where(m_lt, gram[:, :c, c:], 0.0)
    b_kap = jnp.where(m_le, gram[:, c:, :c], 0.0)
    b_bet = jnp.where(m_le, gram[:, c:, c:], 0.0)
    from_state = _sdot(ar, s0, 'htk,hvk->htv')
    p = _tri_solve(a_bet, from_state[:, :c] + _sdot(a_kap, v, 'hts,hsv->htv'), rev)
    vp = jnp.concatenate([v, -p], axis=1)
    y = from_state[:, c:] + _sdot(jnp.concatenate([b_kap, b_bet], axis=2), vp, 'hts,hsv->htv')
    stil = s0 + _sdot(vp, kb, 'hsv,hsk->hvk')
    last = 0 if rev else c - 1
    return y, stil * ecum[:, last:last + 1, :]


def _fn_h(s, norm_g, scale, shift):
    return s * lax.rsqrt(jnp.mean(s * s, axis=-1, keepdims=True) + NORM_EPS) * norm_g * (1.0 + scale) + shift


def _fn_hgpost(of, ob, z, g):
    tm, w = of.shape
    o = (of + ob).reshape(tm, w // HG_HEAD, HG_HEAD)
    o = o * lax.rsqrt(jnp.mean(o * o, axis=-1, keepdims=True) + NORM_EPS)
    return o.reshape(tm, w) * g * jax.nn.silu(z)


def _fn_rwpost(y0, y1, r, k, v, alo, z, a0, a2, k_a, r_k, gn_g, gn_b):
    tm, w = r.shape
    nh = w // RW_HEAD
    asum = 0.0
    for d in range(2):
        asum = asum + jax.nn.sigmoid(a0[d:d + 1] + jnp.dot(alo[:, d * RW_LORA:(d + 1) * RW_LORA], a2[d],
                                                           precision=HI, preferred_element_type=F32))
    k_sum = k * (2.0 + (asum - 2.0) * k_a)
    ys = (y0 + y1).reshape(tm, nh, RW_HEAD)
    mean = jnp.mean(ys, axis=-1, keepdims=True)
    var = jnp.mean(jnp.square(ys - mean), axis=-1, keepdims=True)
    y = ((ys - mean) * lax.rsqrt(var + RW_GN_EPS)).reshape(tm, w) * gn_g + gn_b
    bonus = jnp.sum((r * k_sum * r_k).reshape(tm, nh, RW_HEAD), axis=-1, keepdims=True) * v.reshape(tm, nh, RW_HEAD)
    return (y + bonus.reshape(tm, w)) * jax.nn.silu(z)


def _fn_merge(a, b, ghg, grw):
    return jax.nn.sigmoid(ghg) * a + jax.nn.sigmoid(grw) * b


def _fn_final(xs, o, gate, final_g, tgt):
    x2 = xs + gate * o
    y = x2 * lax.rsqrt(jnp.mean(x2 * x2, axis=-1, keepdims=True) + NORM_EPS) * final_g
    return 0.5 * jnp.sum(jnp.mean(jnp.square(y - tgt), axis=-1))


def _row_call(name, fn, n_tiles, tm, row_ins, full_ins, row_outs, acc_outs):
    n_ri, n_fi, n_ro = len(row_ins), len(full_ins), len(row_outs)

    def body(*refs):
        i = pl.program_id(0)
        rvals = [r[...] for r in refs[:n_ri]]
        fvals = [r[...] for r in refs[n_ri:n_ri + n_fi]]
        outs = refs[n_ri + n_fi:]
        ro, ao = fn(i, rvals, fvals)
        for ref, val in zip(outs[:n_ro], ro):
            ref[...] = val.astype(ref.dtype)
        for ref, val in zip(outs[n_ro:], ao):
            @pl.when(i == 0)
            def _(ref=ref):
                ref[...] = jnp.zeros_like(ref)
            ref[...] += val.astype(ref.dtype)

    def rspec(width, cb, off, rows):
        return pl.BlockSpec((tm, width), lambda i: (jnp.clip(i - off, 0, rows // tm - 1), cb))

    def fspec(shape):
        nd = len(shape)
        return pl.BlockSpec(shape, lambda i: (0,) * nd)

    in_specs = [rspec(w, cb, off, a.shape[0]) for (a, cb, w, off) in row_ins] + [fspec(a.shape) for a in full_ins]
    out_specs = [rspec(w, 0, off, rows) for (rows, w, _, off) in row_outs] + [fspec(s) for (s, _) in acc_outs]
    out_shape = [jax.ShapeDtypeStruct((rows, w), dt) for (rows, w, dt, _) in row_outs] + \
                [jax.ShapeDtypeStruct(s, dt) for (s, dt) in acc_outs]
    res = pl.pallas_call(
        body, name=name, grid=(n_tiles,), in_specs=in_specs, out_specs=out_specs, out_shape=out_shape,
        compiler_params=_params(("arbitrary",)),
    )(*[a for (a, _, _, _) in row_ins], *full_ins)
    return list(res)


def _mm(name, a, b, m, n, k_steps, tm, tn, a_block, a_map, b_block, b_map, o_shape, o_block, o_map,
        contract, out_dtype=F32, scatter=()):
    ns = len(scatter)
    grid = (m // tm, n // tn, k_steps)

    def body(*refs):
        a_ref, b_ref, o_ref, acc_ref = refs[0], refs[1], refs[2 + ns], refs[3 + 2 * ns]
        kk = pl.program_id(2)
        if ns:
            sc_refs = (refs[2:2 + ns], refs[3 + ns:3 + 2 * ns]) + tuple(refs[4 + 2 * ns:])
            at = (pl.program_id(0) * grid[1] + pl.program_id(1)) * grid[2] + kk
            pl.when(at == 0)(lambda: _scatter_start(scatter, *sc_refs))

        @pl.when(kk == 0)
        def _():
            acc_ref[...] = jnp.zeros_like(acc_ref)

        acc_ref[...] += lax.dot_general(a_ref[...].astype(BF16), b_ref[...].astype(BF16),
                                        (contract, ((), ())), preferred_element_type=F32)

        @pl.when(kk == k_steps - 1)
        def _():
            o_ref[...] = acc_ref[...].astype(o_ref.dtype)

        if ns:
            pl.when(at == grid[0] * grid[1] * grid[2] - 1)(lambda: _scatter_wait(scatter, *sc_refs))

    hbm = pl.BlockSpec(memory_space=pl.ANY)
    sems = [pltpu.SemaphoreType.DMA((ns, _PEER_CHIPS))] * 2 if ns else []
    res = pl.pallas_call(
        body, name=name, grid=grid,
        in_specs=[pl.BlockSpec(a_block, a_map), pl.BlockSpec(b_block, b_map)] + [hbm] * ns,
        out_specs=[pl.BlockSpec(o_block, o_map)] + [hbm] * ns,
        out_shape=[jax.ShapeDtypeStruct(o_shape, out_dtype)] + [jax.ShapeDtypeStruct(s.shape, s.dtype) for s in scatter],
        scratch_shapes=[pltpu.VMEM((tm, tn), F32)] + sems,
        compiler_params=_params(("arbitrary",) * 3 if ns else ("parallel", "parallel", "arbitrary")),
    )(a, b, *scatter)
    return (res[0], list(res[1:])) if ns else res[0]


_TM = (768, 512, 256, 128, 64, 32, 16, 8)
_TN = (512, 256, 128)
_TK = (1024, 768, 512, 256, 128)
_TK_WIDE = (768, 512, 256, 128)
WIDE_OUT_BYTES = 32 << 20


def _tm_wide(m, ns):
    for tm in _TM:
        if m % tm == 0 and 3 * 4 * tm * ns <= WIDE_OUT_BYTES:
            return tm
    return m


def _mm_nn(name, a, b, out_dtype=F32):
    m, k = a.shape
    n = b.shape[1]
    tm, tn, tk = _tile(m, _TM), _tile(n, _TN), _tile(k, _TK)
    return _mm(name, a, b, m, n, k // tk, tm, tn, (tm, tk), lambda i, j, s: (i, s), (tk, tn), lambda i, j, s: (s, j),
               (m, n), (tm, tn), lambda i, j, s: (i, j), ((1,), (0,)), out_dtype)


def _mm_nt(name, a, b, out_dtype=F32):
    m, k = a.shape
    n = b.shape[0]
    tm, tn, tk = _tile(m, _TM), _tile(n, _TN), _tile(k, _TK)
    return _mm(name, a, b, m, n, k // tk, tm, tn, (tm, tk), lambda i, j, s: (i, s), (tn, tk), lambda i, j, s: (j, s),
               (m, n), (tm, tn), lambda i, j, s: (i, j), ((1,), (1,)), out_dtype)


def _mm_tn(name, a, b, out_dtype=F32):
    k, m = a.shape
    n = b.shape[1]
    tm, tn, tk = _tile(m, _TM), _tile(n, _TN), _tile(k, _TK)
    return _mm(name, a, b, m, n, k // tk, tm, tn, (tk, tm), lambda i, j, s: (s, i), (tk, tn), lambda i, j, s: (s, j),
               (m, n), (tm, tn), lambda i, j, s: (i, j), ((0,), (0,)), out_dtype)


def _mm_n_st(name, a, bst, out_dtype=F32, joined=False):
    m, k = a.shape
    ns_, _, ns = bst.shape
    tm, tk = _tm_wide(m, ns), _tile(k, (512, 256, 128))
    out = ((m, ns_ * ns), (tm, ns), lambda i, j, s: (i, j)) if joined else \
          ((ns_, m, ns), (None, tm, ns), lambda i, j, s: (j, i, 0))
    return _mm(name, a, bst, m, ns_ * ns, k // tk, tm, ns,
               (tm, tk), lambda i, j, s: (i, s), (None, tk, ns), lambda i, j, s: (j, s, 0), *out, ((1,), (0,)), out_dtype)


def _mm_st_t(name, ast, bst, out_dtype=F32, scatter=()):
    ns_, n, ns = bst.shape
    m = ast.shape[-2]
    tm, tn = _tile(m, _TM), _tile(n, _TN)
    a_side = ((None, tm, ns), lambda i, j, s: (s, i, 0)) if ast.ndim == 3 else ((tm, ns), lambda i, j, s: (i, s))
    return _mm(name, ast, bst, m, n, ns_, tm, tn, *a_side, (None, tn, ns), lambda i, j, s: (s, j, 0),
               (m, n), (tm, tn), lambda i, j, s: (i, j), ((1,), (1,)), out_dtype, scatter)


def _mm_t_st(name, a, bst, out_dtype=F32, scatter=(), n_shard=N_SHARD):
    k, m = a.shape
    ns = bst.shape[-1] if bst.ndim == 3 else bst.shape[-1] // n_shard
    tm, tk = _tile(m, _TN), _tile(k, _TK_WIDE)
    b_side = ((None, tk, ns), lambda i, j, s: (j, s, 0)) if bst.ndim == 3 else ((tk, ns), lambda i, j, s: (s, j))
    return _mm(name, a, bst, m, n_shard * ns, k // tk, tm, ns, (tk, tm), lambda i, j, s: (s, i), *b_side,
               (n_shard, m, ns), (None, tm, ns), lambda i, j, s: (j, i, 0), ((0,), (0,)), out_dtype, scatter)


def _scan_order(j, n_ctx, n_all, rev):
    if not rev:
        return j
    return jnp.where(j < n_ctx, n_ctx - 1 - j, n_all - 1 - (j - n_ctx))


def _hg_scan_fwd(name, p_hg, lb2, d, n_ctx):
    t, w5 = p_hg.shape
    w = w5 // 5
    h = w // HG_HEAD
    n = t // STEP
    rev = d == 1

    def body(q_ref, i_ref, f_ref, lb_ref, o_ref, st_ref, s_ref):
        j = pl.program_id(0)

        @pl.when(j == 0)
        def _():
            s_ref[...] = jnp.zeros_like(s_ref)

        s0 = s_ref[...]
        st_ref[...] = s0
        o, s1 = _hg_step(s0, q_ref[...], i_ref[...], f_ref[...], lb_ref[...], rev)
        o_ref[...] = o
        s_ref[...] = s1

    def rows(cb):
        return pl.BlockSpec((STEP, w), lambda j: (_scan_order(j, n_ctx, n, rev), cb))

    return pl.pallas_call(
        body, name=name, grid=(n,),
        in_specs=[rows(0), rows(1), rows(2 + d), pl.BlockSpec((2, w), lambda j: (0, 0))],
        out_specs=[rows(0), pl.BlockSpec((None, h, HG_HEAD, HG_HEAD), lambda j: (j, 0, 0, 0))],
        out_shape=[jax.ShapeDtypeStruct((t, w), F32), jax.ShapeDtypeStruct((n, h, HG_HEAD, HG_HEAD), F32)],
        scratch_shapes=[pltpu.VMEM((h, HG_HEAD, HG_HEAD), F32)],
        compiler_params=_params(("arbitrary",)),
    )(p_hg, p_hg, p_hg, lb2)


def _hg_scan_bwd(name, p_hg, lb2, states, do, d, n_ctx, other=()):
    t, w5 = p_hg.shape
    w = w5 // 5
    h = w // HG_HEAD
    n = t // STEP
    rev = d == 1
    no = len(other)

    def body(q_ref, i_ref, f_ref, lb_ref, st_ref, do_ref, *refs):
        dq_ref, di_ref, df_ref, dlb_ref, ds_ref = refs[no:]
        step = pl.program_id(0)

        @pl.when(step == 0)
        def _():
            ds_ref[...] = jnp.zeros_like(ds_ref)
            dlb_ref[...] = jnp.zeros_like(dlb_ref)

        _, vjp = jax.vjp(lambda s0, q, i, f, lb: _hg_step(s0, q, i, f, lb, rev),
                         st_ref[...], q_ref[...], i_ref[...], f_ref[...], lb_ref[...])
        ds0, dq, di, df, dlb = vjp((do_ref[...], ds_ref[...]))
        if no:
            dq, di = refs[0][...] + dq, refs[1][...] + di
        dq_ref[...] = dq.astype(dq_ref.dtype)
        di_ref[...] = di.astype(di_ref.dtype)
        df_ref[...] = df.astype(df_ref.dtype)
        dlb_ref[...] += dlb
        ds_ref[...] = ds0

    def rows(cb):
        return pl.BlockSpec((STEP, w), lambda s: (_scan_order(n - 1 - s, n_ctx, n, rev), cb))

    qi = BF16 if no else F32
    return pl.pallas_call(
        body, name=name, grid=(n,),
        in_specs=[rows(0), rows(1), rows(2 + d), pl.BlockSpec((2, w), lambda s: (0, 0)),
                  pl.BlockSpec((None, h, HG_HEAD, HG_HEAD), lambda s: (n - 1 - s, 0, 0, 0)), rows(0)] + [rows(0)] * no,
        out_specs=[rows(0), rows(0), rows(0), pl.BlockSpec((2, w), lambda s: (0, 0))],
        out_shape=[jax.ShapeDtypeStruct((t, w), qi)] * 2 + [jax.ShapeDtypeStruct((t, w), BF16),
                                                            jax.ShapeDtypeStruct((2, w), F32)],
        scratch_shapes=[pltpu.VMEM((h, HG_HEAD, HG_HEAD), F32)],
        compiler_params=_params(("arbitrary",)),
    )(p_hg, p_hg, p_hg, lb2, states, do, *other)


def _to_heads(a, nh):
    return jnp.stack([a[:, i * RW_HEAD:(i + 1) * RW_HEAD] for i in range(nh)], axis=0)


def _from_heads(a):
    return jnp.concatenate([a[i] for i in range(a.shape[0])], axis=-1)


def _rw_scan_fwd(name, sh, hp, d, n_ctx):
    t = sh.shape[0]
    w = (sh.shape[1] - 4 * RW_LORA) // 3
    nh = w // RW_HEAD
    n = t // RW_STEP
    rev = d == 1
    lo = 3 * w // LANE

    def body(r_ref, k_ref, v_ref, wl_ref, al_ref, w0_ref, w2_ref, a0_ref, a2_ref, kk_ref, ka_ref,
             y_ref, st_ref, s_ref):
        j = pl.program_id(0)

        @pl.when(j == 0)
        def _():
            s_ref[...] = jnp.zeros_like(s_ref)

        s0 = s_ref[...]
        st_ref[...] = s0
        wl = wl_ref[...][:, d * RW_LORA:(d + 1) * RW_LORA]
        al = al_ref[...][:, d * RW_LORA:(d + 1) * RW_LORA]
        y, s1 = _rw_step(s0, _to_heads(r_ref[...], nh), _to_heads(k_ref[...], nh), _to_heads(v_ref[...], nh), wl, al,
                         w0_ref[...], w2_ref[...], a0_ref[...], a2_ref[...], kk_ref[...], ka_ref[...], rev)
        y_ref[...] = _from_heads(y)
        s_ref[...] = s1

    def rows(cb, width=w):
        return pl.BlockSpec((RW_STEP, width), lambda j: (_scan_order(j, n_ctx, n, rev), cb))

    def whole(a):
        nd = a.ndim
        return pl.BlockSpec(a.shape, lambda j: (0,) * nd)

    return pl.pallas_call(
        body, name=name, grid=(n,),
        in_specs=[rows(0), rows(1), rows(2), rows(lo, LANE), rows(lo + 1, LANE)] + [whole(a) for a in hp],
        out_specs=[rows(0), pl.BlockSpec((None, nh, RW_HEAD, RW_HEAD), lambda j: (j, 0, 0, 0))],
        out_shape=[jax.ShapeDtypeStruct((t, w), F32), jax.ShapeDtypeStruct((n, nh, RW_HEAD, RW_HEAD), F32)],
        scratch_shapes=[pltpu.VMEM((nh, RW_HEAD, RW_HEAD), F32)],
        compiler_params=_params(("arbitrary",)),
    )(sh, sh, sh, sh, sh, *hp)


def _rw_scan_bwd_both(name, sh, hps, states, dy, n_ctx):
    t = sh.shape[0]
    w = (sh.shape[1] - 4 * RW_LORA) // 3
    nh = w // RW_HEAD
    n = t // RW_STEP
    lo = 3 * w // LANE
    n_in, n_p = 13, 6

    def body(*refs):
        step = pl.program_id(0)
        ins = [refs[d * n_in:(d + 1) * n_in] for d in range(2)]
        outs = [refs[2 * n_in + d * (1 + n_p):2 * n_in + (d + 1) * (1 + n_p)] for d in range(2)]
        ds_refs = refs[2 * n_in + 2 * (1 + n_p):]

        @pl.when(step == 0)
        def _():
            for d in range(2):
                ds_refs[d][...] = jnp.zeros_like(ds_refs[d])
                for ref in outs[d][1:]:
                    ref[...] = jnp.zeros_like(ref)

        for d in range(2):
            r_ref, k_ref, v_ref, wl_ref, al_ref = ins[d][:5]
            hp_refs, st_ref, dy_ref = ins[d][5:11], ins[d][11], ins[d][12]
            wl = wl_ref[...][:, d * RW_LORA:(d + 1) * RW_LORA]
            al = al_ref[...][:, d * RW_LORA:(d + 1) * RW_LORA]
            _, vjp = jax.vjp(functools.partial(_rw_step, rev=d == 1),
                             st_ref[...], _to_heads(r_ref[...], nh), _to_heads(k_ref[...], nh), _to_heads(v_ref[...], nh),
                             wl, al, *[p[...] for p in hp_refs])
            g = vjp((_to_heads(dy_ref[...], nh), ds_refs[d][...]))
            ds_refs[d][...] = g[0]
            zero = jnp.zeros_like(g[4])
            lora = [zero] * 4
            lora[d], lora[2 + d] = g[4], g[5]
            outs[d][0][...] = jnp.concatenate([_from_heads(g[1]), _from_heads(g[2]), _from_heads(g[3])] + lora, axis=-1)
            for ref, val in zip(outs[d][1:], g[6:]):
                ref[...] += val

    def rows(d, cb, width=w):
        return pl.BlockSpec((RW_STEP, width), lambda s: (_scan_order(n - 1 - s, n_ctx, n, d == 1), cb))

    def whole(a):
        nd = a.ndim
        return pl.BlockSpec(a.shape, lambda s: (0,) * nd)

    in_specs, operands, out_specs, out_shape = [], [], [], []
    for d in range(2):
        in_specs += [rows(d, 0), rows(d, 1), rows(d, 2), rows(d, lo, LANE), rows(d, lo + 1, LANE)]
        in_specs += [whole(a) for a in hps[d]]
        in_specs += [pl.BlockSpec((None, nh, RW_HEAD, RW_HEAD), lambda s: (n - 1 - s, 0, 0, 0)), rows(d, 0)]
        operands += [sh] * 5 + list(hps[d]) + [states[d], dy]
        out_specs += [rows(d, 0, sh.shape[1])] + [whole(a) for a in hps[d]]
        out_shape += [jax.ShapeDtypeStruct(sh.shape, F32)] + [jax.ShapeDtypeStruct(a.shape, F32) for a in hps[d]]
    res = pl.pallas_call(
        body, name=name, grid=(n,), in_specs=in_specs, out_specs=out_specs, out_shape=out_shape,
        scratch_shapes=[pltpu.VMEM((nh, RW_HEAD, RW_HEAD), F32)] * 2, compiler_params=_params(("arbitrary",)),
    )(*operands)
    return [res[0], res[1 + n_p]], [res[1:1 + n_p], res[2 + n_p:]]


def _shift_masks(t, n_ctx_rows):
    row = lax.broadcasted_iota(jnp.int32, (t, 1), 0)
    isx = row >= n_ctx_rows
    pos = jnp.where(isx, row - n_ctx_rows, row)
    col = jnp.where(isx, jnp.bitwise_and(pos, GRID_W - 1), pos)
    ncol = jnp.where(isx, GRID_W, n_ctx_rows)
    n_x = t - n_ctx_rows
    ml = col != 0
    mr = col != ncol - 1
    mu = isx & (pos >= GRID_W)
    md = isx & (pos < n_x - GRID_W)
    return ml, mr, mu, md, isx


def _shift_fwd(name, p, mu, n_ctx_rows):
    t, c = p.shape
    cw = LANE

    def body(p_ref, mu_ref, o_ref):
        x = p_ref[...]
        m = mu_ref[...]
        ml, mr, mup, mdn, isx = _shift_masks(t, n_ctx_rows)
        left = jnp.where(ml, pltpu.roll(x, 1, 0), 0.0)
        right = jnp.where(mr, pltpu.roll(x, t - 1, 0), 0.0)
        up = jnp.where(mup, pltpu.roll(x, GRID_W, 0), 0.0)
        down = jnp.where(mdn, pltpu.roll(x, t - GRID_W, 0), 0.0)
        out = x + m[0:1] * (left - x) + m[1:2] * (right - x)
        vert = m[2:3] * (up - x) + m[3:4] * (down - x)
        o_ref[...] = out + jnp.where(isx, vert, 0.0)

    return pl.pallas_call(
        body, name=name, grid=(c // cw,),
        in_specs=[pl.BlockSpec((t, cw), lambda j: (0, j)), pl.BlockSpec((4, cw), lambda j: (0, j))],
        out_specs=pl.BlockSpec((t, cw), lambda j: (0, j)),
        out_shape=jax.ShapeDtypeStruct((t, c), F32),
        compiler_params=_params(("parallel",)),
    )(p, mu)


def _shift_bwd(name, p, mu, dparts, n_ctx_rows):
    t, c = p.shape
    cw = LANE
    npart = len(dparts)

    def body(*refs):
        p_ref, mu_ref = refs[0], refs[1]
        dp_ref, dmu_ref = refs[2 + npart], refs[3 + npart]
        x = p_ref[...]
        m = mu_ref[...]
        g = refs[2][...]
        for r in refs[3:2 + npart]:
            g = g + r[...]
        ml, mr, mup, mdn, isx = _shift_masks(t, n_ctx_rows)
        left = jnp.where(ml, pltpu.roll(x, 1, 0), 0.0)
        right = jnp.where(mr, pltpu.roll(x, t - 1, 0), 0.0)
        up = jnp.where(mup, pltpu.roll(x, GRID_W, 0), 0.0)
        down = jnp.where(mdn, pltpu.roll(x, t - GRID_W, 0), 0.0)
        gx = jnp.where(isx, g, 0.0)
        dmu_ref[...] = jnp.concatenate([
            jnp.sum(g * (left - x), axis=0, keepdims=True), jnp.sum(g * (right - x), axis=0, keepdims=True),
            jnp.sum(gx * (up - x), axis=0, keepdims=True), jnp.sum(gx * (down - x), axis=0, keepdims=True)], axis=0)
        coef = 1.0 - m[0:1] - m[1:2] - jnp.where(isx, m[2:3] + m[3:4], 0.0)
        dp = coef * g
        dp = dp + m[0:1] * pltpu.roll(jnp.where(ml, g, 0.0), t - 1, 0)
        dp = dp + m[1:2] * pltpu.roll(jnp.where(mr, g, 0.0), 1, 0)
        dp = dp + m[2:3] * pltpu.roll(jnp.where(mup, g, 0.0), t - GRID_W, 0)
        dp = dp + m[3:4] * pltpu.roll(jnp.where(mdn, g, 0.0), GRID_W, 0)
        dp_ref[...] = dp.astype(dp_ref.dtype)

    col = pl.BlockSpec((t, cw), lambda j: (0, j))
    par = pl.BlockSpec((4, cw), lambda j: (0, j))
    return pl.pallas_call(
        body, name=name, grid=(c // cw,),
        in_specs=[col, par] + [col] * npart,
        out_specs=[col, par],
        out_shape=[jax.ShapeDtypeStruct((t, c), BF16), jax.ShapeDtypeStruct((4, c), F32)],
        compiler_params=_params(("parallel",)),
    )(p, mu, *dparts)


def _local_step(x, c, ctx, c_ctx, ada_st, ada_b, norm_g, w_in_st, hg_lb, hg_norm_g, rw_mu, rw_w0, rw_w2, rw_a0, rw_a2,
                rw_kk, rw_ka, rw_rk, rw_gn_g, rw_gn_b, w_hg_st, w_rw_st, w_out, final_g, tgt, my_core):
    seq, dm = x.shape
    n_ctx_rows = ctx.shape[0]
    t = seq + n_ctx_rows
    hw = hg_norm_g.shape[-1]
    rw = rw_kk.shape[-1]
    nh_rw = rw // RW_HEAD
    n_ctx = n_ctx_rows // STEP
    tm = _tile(n_ctx_rows, (256, 128, 64))
    nt = t // tm
    nct = n_ctx_rows // tm
    n_sh_cols = 3 * rw + 4 * RW_LORA

    cond = jnp.concatenate([c.reshape(1, dm), c_ctx.reshape(1, dm), jnp.zeros((6, dm), F32)], axis=0)
    final_g2 = final_g.reshape(1, dm)

    def unstack(a_st):
        return jnp.swapaxes(a_st, 0, 1).reshape(a_st.shape[1], -1)

    def restack(a, ns=N_SHARD):
        return jnp.swapaxes(a.reshape(a.shape[0], ns, -1), 0, 1)

    (sc,) = _row_call("cond_silu", lambda i, r, f: ([jax.nn.silu(r[0])], []), 1, 8, [(cond, 0, dm, 0)], [],
                      [(8, dm, F32, 0)], [])
    mod_mm = _mm_n_st("mod_mm", sc, ada_st, joined=True)

    def add(name, terms, shape):
        flat2 = [a.reshape(-1, a.shape[-1]) for a in terms]
        return _rowwise(name, lambda *v: _slot_sum(list(v)), flat2, F32).reshape(shape)

    mod = add("mod_bias", [mod_mm, jnp.broadcast_to(ada_b, (8, 3 * dm))], (8, 3 * dm))
    mod3 = mod.reshape(8, 3, dm)

    def pick(i, m3):
        r = jnp.where(i < nct, m3[1], m3[0])
        return r[0:1], r[1:2]

    tokens = [(ctx, 0, dm, 0), (x, 0, dm, nct)]

    def h_fn(i, r, f):
        shift, scale = pick(i, f[1])
        return [_fn_h(jnp.where(i < nct, r[0], r[1]), f[0], scale, shift)], []

    (h,) = _row_call("h_fwd", h_fn, nt, tm, tokens, [norm_g, mod3], [(t, dm, BF16, 0)], [])
    proj = unstack(_mm_n_st("proj_mm", h, w_in_st))
    p_hg = proj[:, :5 * hw]
    p_rs = proj[:, 5 * hw:5 * hw + n_sh_cols]
    p_zr = proj[:, 5 * hw + n_sh_cols:5 * hw + n_sh_cols + rw]
    p_gt = proj[:, 5 * hw + n_sh_cols + rw:]

    o_hg, st_hg = [], []
    for d in range(2):
        o, st = _hg_scan_fwd(f"hg_scan_fwd{d}", p_hg, hg_lb[d], d, n_ctx)
        o_hg.append(o)
        st_hg.append(st)

    def hgpost_fn(i, r, f):
        return [_fn_hgpost(r[0], r[1], r[2], f[0])], []

    hg_in = [(o_hg[0], 0, hw, 0), (o_hg[1], 0, hw, 0), (p_hg, 4, hw, 0)]
    (y_hg,) = _row_call("hg_post", hgpost_fn, nt, tm, hg_in, [hg_norm_g], [(t, hw, BF16, 0)], [])

    sh = _shift_fwd("rw_shift", p_rs, rw_mu, n_ctx_rows)
    hps = []
    for d in range(2):
        hps.append([rw_w0[d].reshape(nh_rw, 1, RW_HEAD), jnp.swapaxes(rw_w2[d].reshape(RW_LORA, nh_rw, RW_HEAD), 0, 1),
                    rw_a0[d].reshape(nh_rw, 1, RW_HEAD), jnp.swapaxes(rw_a2[d].reshape(RW_LORA, nh_rw, RW_HEAD), 0, 1),
                    rw_kk.reshape(nh_rw, 1, RW_HEAD), rw_ka.reshape(nh_rw, 1, RW_HEAD)])
    y_rw_d, st_rw = [], []
    for d in range(2):
        y, st = _rw_scan_fwd(f"rw_scan_fwd{d}", sh, hps[d], d, n_ctx_rows // RW_STEP)
        y_rw_d.append(y)
        st_rw.append(st)

    rw_full = [rw_a0, rw_a2, rw_ka, rw_rk, rw_gn_g, rw_gn_b]
    lo = 3 * rw // LANE
    rw_in = [(y_rw_d[0], 0, rw, 0), (y_rw_d[1], 0, rw, 0), (sh, 0, rw, 0), (sh, 1, rw, 0), (sh, 2, rw, 0),
             (sh, lo + 1, LANE, 0), (p_zr, 0, rw, 0)]

    def rwpost_fn(i, r, f):
        return [_fn_rwpost(*r, *f)], []

    (y_rw,) = _row_call("rw_post", rwpost_fn, nt, tm, rw_in, rw_full, [(t, rw, BF16, 0)], [])

    a_hg = _mm_n_st("hg_out_mm", y_hg, w_hg_st, joined=True)
    a_rw = _mm_n_st("rw_out_mm", y_rw, w_rw_st, joined=True)
    mg_in = [(a_hg, 0, dm, 0), (a_rw, 0, dm, 0), (p_gt, 0, dm, 0), (p_gt, 1, dm, 0)]
    (merged,) = _row_call("merge", lambda i, r, f: ([_fn_merge(*r)], []), nt, tm, mg_in, [], [(t, dm, BF16, 0)], [])
    o_out = _mm_nn("out_mm", merged, w_out)

    def final_fn(i, r, f):
        gate = f[0][0][2:3]
        loss, vjp = jax.vjp(_fn_final, r[0], r[1], gate, f[1], r[2])
        dx, do, dgate, dfg, _ = vjp(jnp.ones((), F32))
        live = i >= nct
        zero = lambda a: jnp.where(live, a, 0.0)
        dmod = jnp.concatenate([jnp.concatenate([jnp.zeros((1, 2 * dm), F32), zero(dgate)], axis=1),
                                jnp.zeros((7, 3 * dm), F32)], axis=0)
        return [zero(dx), zero(do)], [jnp.broadcast_to(zero(loss), (8, LANE)), dmod, zero(dfg)]

    fin_in = [(x, 0, dm, nct), (o_out, 0, dm, 0), (tgt, 0, dm, nct)]
    dx_res, d_o, loss_acc, dmod_gate, d_final_g = _row_call(
        "final", final_fn, nt, tm, fin_in, [mod3, final_g2], [(t, dm, F32, 0), (t, dm, BF16, 0)],
        [((8, LANE), F32), ((8, 3 * dm), F32), ((1, dm), F32)])

    g_w_out = _mm_tn("d_w_out", merged, d_o)
    d_merged = _mm_nt("d_merged", d_o, w_out)

    def merge_bwd(i, r, f):
        _, vjp = jax.vjp(_fn_merge, r[0], r[1], r[2], r[3])
        da, db, dgh, dgr = vjp(r[4])
        return [da, db, jnp.concatenate([dgh, dgr], axis=1)], []

    da_hg, da_rw, dp_gt = _row_call("merge_bwd", merge_bwd, nt, tm, mg_in + [(d_merged, 0, dm, 0)], [],
                                    [(t, dm, BF16, 0), (t, dm, BF16, 0), (t, 2 * dm, BF16, 0)], [])
    g_w_hg_st = _mm_t_st("d_w_hg", y_hg, da_hg)
    g_w_rw_st = _mm_t_st("d_w_rw", y_rw, da_rw)
    dy_hg = _mm_st_t("d_y_hg", da_hg, w_hg_st)
    dy_rw = _mm_st_t("d_y_rw", da_rw, w_rw_st)

    def hgpost_bwd(i, r, f):
        _, vjp = jax.vjp(_fn_hgpost, r[0], r[1], r[2], f[0])
        dof, _, dz, dg = vjp(r[3])
        return [dof, dz], [dg]

    do_hg, dz_hg, g_hg_norm = _row_call("hg_post_bwd", hgpost_bwd, nt, tm, hg_in + [(dy_hg, 0, hw, 0)], [hg_norm_g],
                                        [(t, hw, F32, 0), (t, hw, BF16, 0)], [((1, hw), F32)])
    dq0, di0, df0, dlb0 = _hg_scan_bwd("hg_scan_bwd0", p_hg, hg_lb[0], st_hg[0], do_hg, 0, n_ctx)
    dq, di, df1, dlb1 = _hg_scan_bwd("hg_scan_bwd1", p_hg, hg_lb[1], st_hg[1], do_hg, 1, n_ctx, other=(dq0, di0))
    g_hg_lb = jnp.stack([dlb0, dlb1], axis=0)

    def rwpost_bwd(i, r, f):
        _, vjp = jax.vjp(_fn_rwpost, *r[:7], *f)
        g = vjp(r[7])
        zl = jnp.zeros((g[5].shape[0], 2 * RW_LORA), F32)
        return [g[0], jnp.concatenate([g[2], g[3], g[4], zl, g[5]], axis=1), g[6]], list(g[7:])

    dy_sum, dsh_p, dz_rw, g_a0_p, g_a2_p, g_ka_p, g_rk, g_gn_g, g_gn_b = _row_call(
        "rw_post_bwd", rwpost_bwd, nt, tm, rw_in + [(dy_rw, 0, rw, 0)], rw_full,
        [(t, rw, F32, 0), (t, n_sh_cols, F32, 0), (t, rw, BF16, 0)], [(a.shape, F32) for a in rw_full])
    dsh_dirs, hp_grads = _rw_scan_bwd_both("rw_scan_bwd", sh, hps, st_rw, dy_sum, n_ctx_rows // RW_STEP)
    dp_rs, g_mu = _shift_bwd("rw_shift_bwd", p_rs, rw_mu, [dsh_p] + dsh_dirs, n_ctx_rows)

    def flat(a):
        if a.shape[1] == 1:
            return a.reshape(rw)
        return jnp.swapaxes(a, 0, 1).reshape(RW_LORA, rw)

    g_w0 = jnp.stack([flat(hp_grads[d][0]) for d in range(2)], axis=0)
    g_w2 = jnp.stack([flat(hp_grads[d][1]) for d in range(2)], axis=0)
    g_a0 = add("g_a0", [jnp.stack([flat(hp_grads[d][2]) for d in range(2)], axis=0), g_a0_p], (2, rw))
    g_a2 = add("g_a2", [jnp.stack([flat(hp_grads[d][3]) for d in range(2)], axis=0), g_a2_p], (2, RW_LORA, rw))
    g_kk = add("g_kk", [flat(hp_grads[0][4]).reshape(1, rw), flat(hp_grads[1][4]).reshape(1, rw)], (1, rw))
    g_ka = add("g_ka", [flat(hp_grads[0][5]).reshape(1, rw), flat(hp_grads[1][5]).reshape(1, rw), g_ka_p], (1, rw))

    dproj = jnp.concatenate([dq, di, df0, df1, dz_hg, dp_rs, dz_rw, dp_gt], axis=1)
    dproj_st = restack(dproj)
    early = {"w_hg_out": g_w_hg_st, "w_rw_out": g_w_rw_st, "w_out": g_w_out.reshape(N_SHARD, dm // N_SHARD, dm)}
    early_chip = [_pair_exchange(f"grads_pair_sum_{n}", a, my_core, True, BF16) for n, a in early.items()]
    g_w_in_st, early_landed = _mm_t_st("d_w_in", h, dproj_st, scatter=tuple(early_chip))
    w_in_chip = _pair_exchange("grads_pair_sum_w_in", g_w_in_st, my_core, True, BF16)
    dh, (w_in_landed,) = _mm_st_t("d_h", dproj_st, w_in_st, scatter=(w_in_chip,))

    def h_bwd(i, r, f):
        shift, scale = pick(i, f[1])
        is_ctx = i < nct
        _, vjp = jax.vjp(_fn_h, jnp.where(is_ctx, r[0], r[1]), f[0], scale, shift)
        ds, dg, dscale, dshift = vjp(r[2])
        row = jnp.concatenate([dshift, dscale, jnp.zeros((1, dm), F32)], axis=1)
        z = jnp.zeros_like(row)
        dmod = jnp.concatenate([jnp.where(is_ctx, z, row), jnp.where(is_ctx, row, z), jnp.zeros((6, 3 * dm), F32)], axis=0)
        return [ds + r[3]], [dg, dmod]

    grad_x, g_norm_g, dmod_h = _row_call(
        "h_bwd", h_bwd, nt, tm, tokens + [(dh, 0, dm, 0), (dx_res, 0, dm, 0)], [norm_g, mod3],
        [(seq, dm, F32, nct)], [((1, dm), F32), ((8, 3 * dm), F32)])
    dmod = add("d_mod", [dmod_h, dmod_gate], (8, 3 * dm))
    g_ada_b = add("g_ada_b", [dmod[0:1], dmod[1:2]], (1, 3 * dm))
    g_ada_st = _mm_t_st("d_ada_w", sc, dmod)
    d_sc = _mm_st_t("d_cond", dmod, ada_st)

    def cond_bwd(i, r, f):
        _, vjp = jax.vjp(jax.nn.silu, r[0])
        return [vjp(r[1])[0]], []

    (d_cond,) = _row_call("cond_bwd", cond_bwd, 1, 8, [(cond, 0, dm, 0), (d_sc, 0, dm, 0)], [], [(8, dm, F32, 0)], [])

    grads = dict(
        c_ctx=d_cond[1], ada_w=g_ada_st, ada_b=g_ada_b, norm_g=g_norm_g, w_in=(w_in_chip, w_in_landed), hg_lb=g_hg_lb,
        hg_norm_g=g_hg_norm, rw_mu=g_mu, rw_w0=g_w0, rw_w2=g_w2, rw_a0=g_a0, rw_a2=g_a2, rw_kk=g_kk, rw_ka=g_ka,
        rw_rk=g_rk, rw_gn_g=g_gn_g, rw_gn_b=g_gn_b, final_g=d_final_g.reshape(dm))
    grads.update(zip(early, zip(early_chip, early_landed)))
    return loss_acc[0:1, 0:1], grad_x, grads


def _my_place():
    return lax.axis_index("x"), lax.axis_index("y"), lax.axis_index("c")


MIN_CHUNK_BYTES = 1 << 18
ROW_ALIGN = 16


def _n_chunks(rows, row_bytes):
    for n in (8, 4, 2):
        if rows % (n * ROW_ALIGN) == 0 and rows // n * row_bytes >= MIN_CHUNK_BYTES:
            return n
    return 1


def _row_bytes(a, lead=1):
    n = a.dtype.itemsize
    for d in a.shape[lead:]:
        n *= d
    return n


def _rows(ref, start, size):
    return ref.at[pl.ds(start, size)]


def _chunked(make, start, size, n):
    cs = size // n
    return [make(start + j * cs, cs) for j in range(n)]


_PEER_CHIPS = 3


def _weights_gather(name, big, small):
    nb, na = len(big), len(big) + len(small)
    arrays = list(big) + list(small)
    n_ici = 6

    def body(*refs):
        outs = refs[na:2 * na]
        send_sems, recv_sems, fsend_sems, frecv_sems = refs[2 * na:]
        x, y, c = _my_place()
        me, sx, sy, sd = 2 * x + y, 2 * (1 - x) + y, 2 * x + (1 - y), 2 * (1 - x) + (1 - y)
        kx, ky, kd = (1 - x, y, c), (x, 1 - y, c), (1 - x, 1 - y, c)

        def ici(a, j, src_slot, dst_slot, to, r0, nr):
            return pltpu.make_async_remote_copy(
                src_ref=_rows(outs[a].at[src_slot], r0, nr), dst_ref=_rows(outs[a].at[dst_slot], r0, nr),
                send_sem=send_sems.at[a, j], recv_sem=recv_sems.at[a, j], device_id=to,
                device_id_type=pl.DeviceIdType.MESH)

        def to_sibling(a, k, slot, r0, nr):
            rows = _rows(outs[a].at[slot], r0, nr)
            return pltpu.make_async_remote_copy(
                src_ref=rows, dst_ref=rows, send_sem=fsend_sems.at[a, k], recv_sem=frecv_sems.at[a, k],
                device_id=(x, y, 1 - c), device_id_type=pl.DeviceIdType.MESH)

        def start(copies):
            for cp in copies:
                cp.start()

        geo = []
        for a in range(nb):
            half = arrays[a].shape[1] // 2
            geo.append((pl.multiple_of(c * half, ROW_ALIGN), pl.multiple_of((1 - c) * half, ROW_ALIGN), half // 2,
                        _n_chunks(half // 2, _row_bytes(arrays[a], 2))))
        plan = [(me, sx, kx, 0), (me, sx, kx, 1), (me, sy, ky, 0), (me, sy, ky, 1), (sx, sd, ky, 0), (sy, sd, kx, 1)]

        def piece(a, j):
            return geo[a][0] + plan[j][3] * geo[a][2]

        for a in range(nb):
            for j in range(4):
                start(_chunked(lambda r0, cs: ici(a, j, me, me, plan[j][2], r0, cs), piece(a, j), geo[a][2], geo[a][3]))
        for a in range(nb, na):
            rows = arrays[a].shape[1]
            for j, to in ((0, kx), (2, ky), (1, kd)):
                ici(a, j, me, me, to, 0, rows).start()
        for a in range(nb):
            for j, first in ((4, 0), (5, 3)):
                src_slot, _, to, _ = plan[j]
                ici(a, first, me, plan[first][1], plan[first][2], piece(a, first), geo[a][2]).wait_recv()
                start(_chunked(lambda r0, cs: ici(a, j, src_slot, src_slot, to, r0, cs), piece(a, j), geo[a][2], geo[a][3]))
        for a in range(nb):
            for j in (1, 2):
                ici(a, j, me, plan[j][1], plan[j][2], piece(a, j), geo[a][2]).wait_recv()
            for k, slot in ((0, sx), (1, sy)):
                start(_chunked(lambda r0, cs: to_sibling(a, k, slot, r0, cs), geo[a][0], 2 * geo[a][2], geo[a][3]))
        for a in range(nb):
            for j in (4, 5):
                ici(a, j, me, sd, plan[j][2], piece(a, j), geo[a][2]).wait_recv()
            start(_chunked(lambda r0, cs: to_sibling(a, 2, sd, r0, cs), geo[a][0], 2 * geo[a][2], geo[a][3]))
        for a in range(nb, na):
            rows = arrays[a].shape[1]
            for j, slot, to in ((0, sx, kx), (2, sy, ky), (1, sd, kd)):
                ici(a, j, me, slot, to, 0, rows).wait_recv()
        for a in range(nb):
            for k, slot in ((0, sx), (1, sy), (2, sd)):
                to_sibling(a, k, slot, geo[a][1], 2 * geo[a][2]).wait_recv()
        for a in range(nb):
            for j in range(n_ici):
                ici(a, j, me, me, plan[j][2], piece(a, j), geo[a][2]).wait_send()
            for k, slot in ((0, sx), (1, sy), (2, sd)):
                to_sibling(a, k, slot, geo[a][0], 2 * geo[a][2]).wait_send()
        for a in range(nb, na):
            rows = arrays[a].shape[1]
            for j, to in ((0, kx), (2, ky), (1, kd)):
                ici(a, j, me, me, to, 0, rows).wait_send()

    hbm = pl.BlockSpec(memory_space=pl.ANY)
    ici_sems = pltpu.SemaphoreType.DMA((na, n_ici))
    pair_sems = pltpu.SemaphoreType.DMA((na, _PEER_CHIPS))
    return pl.pallas_call(
        body, name=name, in_specs=[hbm] * na, out_specs=[hbm] * na,
        out_shape=[jax.ShapeDtypeStruct(a.shape, a.dtype) for a in arrays],
        input_output_aliases={a: a for a in range(na)}, scratch_shapes=[ici_sems, ici_sems, pair_sems, pair_sems],
    )(*arrays)


def _scatter_copy(arrays, ins, outs, send_sems, recv_sems, a, k, slot, r0, nr):
    x, y, c = _my_place()
    px, py = [(1 - x, y), (x, 1 - y), (1 - x, 1 - y)][k]
    return pltpu.make_async_remote_copy(
        src_ref=_rows(ins[a].at[2 * px + py], r0, nr), dst_ref=_rows(outs[a].at[slot], r0, nr),
        send_sem=send_sems.at[a, k], recv_sem=recv_sems.at[a, k], device_id=(px, py, c),
        device_id_type=pl.DeviceIdType.MESH)


def _scatter_start(arrays, ins, outs, send_sems, recv_sems):
    x, y, _ = _my_place()
    for a in range(len(arrays)):
        rows = arrays[a].shape[1]
        for k in range(_PEER_CHIPS):
            for cp in _chunked(lambda r0, cs: _scatter_copy(arrays, ins, outs, send_sems, recv_sems, a, k, 2 * x + y, r0, cs),
                               0, rows, _n_chunks(rows, _row_bytes(arrays[a], 2))):
                cp.start()


def _scatter_wait(arrays, ins, outs, send_sems, recv_sems):
    x, y, _ = _my_place()
    peer_slot = [2 * (1 - x) + y, 2 * x + (1 - y), 2 * (1 - x) + (1 - y)]
    for k in range(_PEER_CHIPS):
        for a in range(len(arrays)):
            _scatter_copy(arrays, ins, outs, send_sems, recv_sems, a, k, peer_slot[k], 0, arrays[a].shape[1]).wait_recv()
    for a in range(len(arrays)):
        for k in range(_PEER_CHIPS):
            _scatter_copy(arrays, ins, outs, send_sems, recv_sems, a, k, 2 * x + y, 0, arrays[a].shape[1]).wait_send()


PAIR_TILE_BYTES = 4 << 20


def _pair_exchange(name, a, place, reduce, out_dtype):
    rows, cols = a.shape[-2], a.shape[-1]
    half = rows // 2 if reduce else rows
    tr = _row_tile_for(half, cols, budget=PAIR_TILE_BYTES)
    nh = half // tr
    n_steps = (N_SHARD if reduce else 1) * nh

    def body(pc_ref, *refs):
        if reduce:
            keep_ref, send_ref, o_ref, land, send_sems, recv_sems, credit, wire = refs
            wire[...] = send_ref[...].astype(BF16)
            src = wire
        else:
            send_ref, o_ref, land, send_sems, recv_sems, credit = refs
            src = send_ref
        x, y, c = _my_place()
        other = (x, y, 1 - c)
        t = pl.program_id(0) * nh + pl.program_id(1) if reduce else pl.program_id(0)
        slot = t % 2

        @pl.when(t >= 2)
        def _():
            pl.semaphore_wait(credit, 1)

        copy = pltpu.make_async_remote_copy(
            src_ref=src, dst_ref=land.at[slot], send_sem=send_sems.at[slot], recv_sem=recv_sems.at[slot],
            device_id=other, device_id_type=pl.DeviceIdType.MESH)
        copy.start()
        copy.wait_recv()
        got = land[slot]
        o_ref[...] = ((keep_ref[...] + got.astype(F32)) if reduce else got).astype(out_dtype)
        copy.wait_send()

        @pl.when(t < n_steps - 2)
        def _():
            pl.semaphore_signal(credit, inc=1, device_id=other, device_id_type=pl.DeviceIdType.MESH)

    if reduce:
        grid = (N_SHARD, nh)
        in_specs = [pl.BlockSpec((None, tr, cols), lambda j, i, pc: (j, pc[0] * nh + i, 0)),
                    pl.BlockSpec((None, tr, cols), lambda j, i, pc: (j, (1 - pc[0]) * nh + i, 0))]
        out_spec = pl.BlockSpec((None, tr, cols), lambda j, i, pc: (j, i, 0))
        out_shape = jax.ShapeDtypeStruct((N_SHARD, half, cols), out_dtype)
        operands = (a, a)
        sem = ("arbitrary", "arbitrary")
    else:
        grid = (nh,)
        in_specs = [pl.BlockSpec((tr, cols), lambda i, pc: (i, 0))]
        out_spec = pl.BlockSpec((tr, cols), lambda i, pc: (i, 0))
        out_shape = jax.ShapeDtypeStruct((half, cols), out_dtype)
        operands = (a,)
        sem = ("arbitrary",)
    return pl.pallas_call(
        body, name=name,
        grid_spec=pltpu.PrefetchScalarGridSpec(
            num_scalar_prefetch=1, grid=grid, in_specs=in_specs, out_specs=out_spec,
            scratch_shapes=[pltpu.VMEM((2, tr, cols), BF16 if reduce else a.dtype), pltpu.SemaphoreType.DMA((2,)),
                            pltpu.SemaphoreType.DMA((2,)), pltpu.SemaphoreType.REGULAR] +
                           ([pltpu.VMEM((tr, cols), BF16)] if reduce else [])),
        out_shape=out_shape, compiler_params=_params(sem),
    )(place, *operands)


def _cast_into_slot(name, a, chip):
    rows, cols = a.shape
    tm = _row_tile_for(rows, cols)

    def body(pc_ref, a_ref, o_ref):
        o_ref[...] = a_ref[...].astype(BF16)

    return pl.pallas_call(
        body, name=name,
        grid_spec=pltpu.PrefetchScalarGridSpec(
            num_scalar_prefetch=1, grid=(rows // tm,), in_specs=[pl.BlockSpec((tm, cols), lambda i, pc: (i, 0))],
            out_specs=pl.BlockSpec((None, tm, cols), lambda i, pc: (pc[0], i, 0))),
        out_shape=jax.ShapeDtypeStruct((N_SHARD, rows, cols), BF16), compiler_params=_params(("parallel",)),
    )(chip, a)


def _sum_landed(name, landed, sent, chip):
    ns, rows, cols = landed.shape
    tm = _row_tile_for(rows, cols)

    def body(pc_ref, *refs):
        own_ref, o_ref = refs[ns], refs[ns + 1]
        me = pc_ref[0]
        terms = [jnp.where(me == j, own_ref[...], refs[j][...]).astype(F32) for j in range(ns)]
        o_ref[...] = _slot_sum(terms)

    def landed_spec(j):
        return pl.BlockSpec((None, tm, cols), lambda i, pc: (jnp.where(pc[0] == j, (j + 1) % ns, j), i, 0))

    return pl.pallas_call(
        body, name=name,
        grid_spec=pltpu.PrefetchScalarGridSpec(
            num_scalar_prefetch=1, grid=(rows // tm,),
            in_specs=[landed_spec(j) for j in range(ns)] + [pl.BlockSpec((None, tm, cols), lambda i, pc: (pc[0], i, 0))],
            out_specs=pl.BlockSpec((tm, cols), lambda i, pc: (i, 0))),
        out_shape=jax.ShapeDtypeStruct((rows, cols), F32), compiler_params=_params(("parallel",)),
    )(chip, *([landed] * ns), sent)


def _gather_all(name, a):
    def body(in_ref, out_ref, send_sems, recv_sems, local_sem):
        x, y, c = _my_place()
        me = 4 * x + 2 * y + c

        def peer(k):
            return (x ^ (k >> 2), y ^ ((k >> 1) & 1), c ^ (k & 1))

        def remote(k, land):
            return pltpu.make_async_remote_copy(
                src_ref=in_ref, dst_ref=out_ref.at[land], send_sem=send_sems.at[k - 1], recv_sem=recv_sems.at[k - 1],
                device_id=peer(k), device_id_type=pl.DeviceIdType.MESH)

        local = pltpu.make_async_copy(in_ref, out_ref.at[me], local_sem)
        local.start()
        for k in range(1, N_DEV):
            remote(k, me).start()
        for k in range(1, N_DEV):
            px, py, pc = peer(k)
            remote(k, 4 * px + 2 * py + pc).wait_recv()
        for k in range(1, N_DEV):
            remote(k, me).wait_send()
        local.wait()

    hbm = pl.BlockSpec(memory_space=pl.ANY)
    return pl.pallas_call(
        body, name=name, in_specs=[hbm], out_specs=hbm,
        out_shape=jax.ShapeDtypeStruct((N_DEV,) + a.shape, a.dtype),
        scratch_shapes=[pltpu.SemaphoreType.DMA((N_DEV - 1,)), pltpu.SemaphoreType.DMA((N_DEV - 1,)), pltpu.SemaphoreType.DMA],
    )(a)


def _row_tile_for(rows, cols, budget=1 << 20):
    if rows * cols * 4 <= budget:
        return rows
    for tm in (1024, 512, 256, 128, 64, 32, 16, 8):
        if rows % tm == 0 and tm * cols * 4 <= budget:
            return tm
    return rows


def _slot_sum(vals):
    g = vals[0]
    for v in vals[1:]:
        g = g + v
    return g


def _rowwise(name, fn, arrays, out_dtype):
    rows, cols = arrays[0].shape
    tm = _row_tile_for(rows, cols)

    def body(*refs):
        refs[-1][...] = fn(*[r[...] for r in refs[:-1]]).astype(out_dtype)

    blk = pl.BlockSpec((tm, cols), lambda i: (i, 0))
    return pl.pallas_call(
        body, name=name, grid=(rows // tm,), in_specs=[blk] * len(arrays), out_specs=blk,
        out_shape=jax.ShapeDtypeStruct((rows, cols), out_dtype), compiler_params=_params(("parallel",)),
    )(*arrays)


def _sum_slots(name, st):
    ns, rows, cols = st.shape
    tm = _row_tile_for(rows, cols)

    def body(s_ref, o_ref):
        o_ref[...] = _slot_sum([s_ref[j].astype(F32) for j in range(ns)])

    return pl.pallas_call(
        body, name=name, grid=(rows // tm,),
        in_specs=[pl.BlockSpec((ns, tm, cols), lambda i: (0, i, 0))],
        out_specs=pl.BlockSpec((tm, cols), lambda i: (i, 0)),
        out_shape=jax.ShapeDtypeStruct((rows, cols), F32),
        compiler_params=_params(("parallel",)),
    )(st)


ADAM_TILE_BYTES = 1 << 20


def _adam_update(g, p_ref, m_ref, v_ref, go_ref, d_ref, mo_ref, vo_ref):
    mn = ADAM_B1 * m_ref[...] + (1.0 - ADAM_B1) * g
    vn = ADAM_B2 * v_ref[...] + (1.0 - ADAM_B2) * jnp.square(g)
    m_hat = mn / (1.0 - ADAM_B1 ** ADAM_STEP)
    v_hat = vn / (1.0 - ADAM_B2 ** ADAM_STEP)
    go_ref[...] = g
    d_ref[...] = -ADAM_LR * (m_hat / (jnp.sqrt(v_hat) + ADAM_EPS) + ADAM_WD * p_ref[...])
    mo_ref[...] = mn
    vo_ref[...] = vn


def _adamw(name, p, m, v, gst):
    rows, cols = p.shape
    ns = gst.shape[0]
    tm = _row_tile_for(rows, cols, budget=ADAM_TILE_BYTES)

    def body(p_ref, m_ref, v_ref, g_ref, *outs):
        _adam_update(_slot_sum([g_ref[j] for j in range(ns)]), p_ref, m_ref, v_ref, *outs)

    blk = pl.BlockSpec((tm, cols), lambda i: (i, 0))
    return pl.pallas_call(
        body, name=name, grid=(rows // tm,),
        in_specs=[blk, blk, blk, pl.BlockSpec((ns, tm, cols), lambda i: (0, i, 0))],
        out_specs=[blk] * 4, out_shape=[jax.ShapeDtypeStruct((rows, cols), F32)] * 4,
        compiler_params=_params(("parallel",)),
    )(p, m, v, gst)


def _adamw_halves(name, p, m, v, mine, theirs, place, scatter=()):
    rows, cols = p.shape
    half = rows // 2
    tm = _row_tile_for(half, cols, budget=ADAM_TILE_BYTES)
    nh = half // tm
    ns = len(scatter)

    def body(pc_ref, p_ref, m_ref, v_ref, mine_ref, theirs_ref, *refs):
        if ns:
            sc_refs = (refs[:ns], refs[ns + 4:2 * ns + 4]) + tuple(refs[2 * ns + 4:])
            at = pl.program_id(0) * nh + pl.program_id(1)
            pl.when(at == 0)(lambda: _scatter_start(scatter, *sc_refs))
        g = jnp.where(pl.program_id(0) == pc_ref[0], mine_ref[...], theirs_ref[...])
        _adam_update(g, p_ref, m_ref, v_ref, *refs[ns:ns + 4])
        if ns:
            pl.when(at == 2 * nh - 1)(lambda: _scatter_wait(scatter, *sc_refs))

    blk = pl.BlockSpec((tm, cols), lambda h, i, pc: (h * nh + i, 0))
    hblk = pl.BlockSpec((tm, cols), lambda h, i, pc: (i, 0))
    hbm = pl.BlockSpec(memory_space=pl.ANY)
    res = pl.pallas_call(
        body, name=name,
        grid_spec=pltpu.PrefetchScalarGridSpec(
            num_scalar_prefetch=1, grid=(2, nh), in_specs=[blk, blk, blk, hblk, hblk] + [hbm] * ns,
            out_specs=[blk] * 4 + [hbm] * ns,
            scratch_shapes=[pltpu.SemaphoreType.DMA((ns, _PEER_CHIPS))] * 2 if ns else []),
        out_shape=[jax.ShapeDtypeStruct((rows, cols), F32)] * 4 + [jax.ShapeDtypeStruct(s.shape, s.dtype) for s in scatter],
        compiler_params=_params(("arbitrary", "arbitrary") if ns else ("parallel", "parallel")),
    )(place, p, m, v, mine, theirs, *scatter)
    return (list(res[:4]), list(res[4:])) if ns else res


def _pack(parts, width=LANE, mult=8):
    flat = jnp.concatenate([a.reshape(-1) for a in parts])
    n = flat.shape[0]
    per = width * mult
    total = -(-n // per) * per
    return jnp.pad(flat, (0, total - n)).reshape(total // width, width)


def _unpack(packed, shapes):
    flat = packed.reshape(-1)
    out, off = [], 0
    for s in shapes:
        n = 1
        for d in s:
            n *= d
        out.append(flat[off:off + n].reshape(s))
        off += n
    return out


_SMALL_SHARDED = ("hg_lb", "rw_mu", "rw_w0", "rw_w2", "rw_a0", "rw_a2")
_REPLICATED = ("c_ctx", "ada_b", "norm_g", "hg_norm_g", "rw_kk", "rw_ka", "rw_rk", "rw_gn_g", "rw_gn_b", "final_g")
_BIG = ("ada_w", "w_in", "w_hg_out", "w_rw_out", "w_out")
_WEIGHTS = ("c_ctx", "ada_w", "ada_b", "norm_g", "w_in", "hg_lb", "hg_norm_g", "rw_mu", "rw_w0", "rw_w2", "rw_a0", "rw_a2",
            "rw_kk", "rw_ka", "rw_rk", "rw_gn_g", "rw_gn_b", "w_hg_out", "w_rw_out", "w_out", "final_g")


def _join_shards(st):
    a = jnp.moveaxis(st, 0, -2)
    return a.reshape(a.shape[:-2] + (a.shape[-2] * a.shape[-1],))


def _split_shards(a):
    s = a.reshape(a.shape[:-1] + (N_SHARD, a.shape[-1] // N_SHARD))
    return jnp.moveaxis(s, -2, 0)


def kernel(x, c, ctx, c_ctx, ada_w, ada_b, norm_g, w_in, hg_lb, hg_norm_g, rw_mu, rw_w0, rw_w2, rw_a0, rw_a2, rw_kk, rw_ka, rw_rk, rw_gn_g, rw_gn_b, w_hg_out, w_rw_out, w_out, final_g, loss_target, m_c_ctx, m_ada_w, m_ada_b, m_norm_g, m_w_in, m_hg_lb, m_hg_norm_g, m_rw_mu, m_rw_w0, m_rw_w2, m_rw_a0, m_rw_a2, m_rw_kk, m_rw_ka, m_rw_rk, m_rw_gn_g, m_rw_gn_b, m_w_hg_out, m_w_rw_out, m_w_out, m_final_g, v_c_ctx, v_ada_w, v_ada_b, v_norm_g, v_w_in, v_hg_lb, v_hg_norm_g, v_rw_mu, v_rw_w0, v_rw_w2, v_rw_a0, v_rw_a2, v_rw_kk, v_rw_ka, v_rw_rk, v_rw_gn_g, v_rw_gn_b, v_w_hg_out, v_w_rw_out, v_w_out, v_final_g):
    w = dict(c_ctx=c_ctx, ada_w=ada_w, ada_b=ada_b, norm_g=norm_g, w_in=w_in, hg_lb=hg_lb, hg_norm_g=hg_norm_g, rw_mu=rw_mu,
             rw_w0=rw_w0, rw_w2=rw_w2, rw_a0=rw_a0, rw_a2=rw_a2, rw_kk=rw_kk, rw_ka=rw_ka, rw_rk=rw_rk, rw_gn_g=rw_gn_g,
             rw_gn_b=rw_gn_b, w_hg_out=w_hg_out, w_rw_out=w_rw_out, w_out=w_out, final_g=final_g)
    m = dict(c_ctx=m_c_ctx, ada_w=m_ada_w, ada_b=m_ada_b, norm_g=m_norm_g, w_in=m_w_in, hg_lb=m_hg_lb, hg_norm_g=m_hg_norm_g,
             rw_mu=m_rw_mu, rw_w0=m_rw_w0, rw_w2=m_rw_w2, rw_a0=m_rw_a0, rw_a2=m_rw_a2, rw_kk=m_rw_kk, rw_ka=m_rw_ka,
             rw_rk=m_rw_rk, rw_gn_g=m_rw_gn_g, rw_gn_b=m_rw_gn_b, w_hg_out=m_w_hg_out, w_rw_out=m_w_rw_out, w_out=m_w_out,
             final_g=m_final_g)
    v = dict(c_ctx=v_c_ctx, ada_w=v_ada_w, ada_b=v_ada_b, norm_g=v_norm_g, w_in=v_w_in, hg_lb=v_hg_lb, hg_norm_g=v_hg_norm_g,
             rw_mu=v_rw_mu, rw_w0=v_rw_w0, rw_w2=v_rw_w2, rw_a0=v_rw_a0, rw_a2=v_rw_a2, rw_kk=v_rw_kk, rw_ka=v_rw_ka,
             rw_rk=v_rw_rk, rw_gn_g=v_rw_gn_g, rw_gn_b=v_rw_gn_b, w_hg_out=v_w_hg_out, w_rw_out=v_w_rw_out, w_out=v_w_out,
             final_g=v_final_g)

    def mat(a):
        return a.reshape(a.shape[-2], a.shape[-1])

    def pack_small(d):
        return _pack([d[n] for n in _SMALL_SHARDED], mult=2 * ROW_ALIGN)

    my_core = lax.axis_index("c").astype(jnp.int32).reshape(1)
    my_chip = (2 * lax.axis_index("x") + lax.axis_index("y")).astype(jnp.int32).reshape(1)

    small_shapes = [w[n].shape for n in _SMALL_SHARDED]
    big_bf = [_cast_into_slot(f"to_bf16_{n}", mat(w[n]), my_chip) for n in _BIG]
    small_mine = pack_small(w)
    small_slots = lax.dynamic_update_slice(jnp.zeros((N_SHARD,) + small_mine.shape, F32), small_mine[None], (my_chip[0], 0, 0))
    gathered = _weights_gather("weights_gather", big_bf, [small_slots])
    ada_st, w_in_st, w_hg_st, w_rw_st, w_out_st, small_st = gathered
    full_small = {}
    per_chip = [_unpack(small_st[j], small_shapes) for j in range(N_SHARD)]
    for i, n in enumerate(_SMALL_SHARDED):
        full_small[n] = _join_shards(jnp.stack([per_chip[j][i] for j in range(N_SHARD)], axis=0))
    dm = x.shape[-1]
    w_out_full = w_out_st.reshape(dm, dm)

    loss_b, grad_x, g = _local_step(
        x[0], c, ctx[0], c_ctx, ada_st, ada_b, norm_g, w_in_st, full_small["hg_lb"], hg_norm_g, full_small["rw_mu"][0],
        full_small["rw_w0"][0], full_small["rw_w2"][0], full_small["rw_a0"][0], full_small["rw_a2"][0], rw_kk, rw_ka, rw_rk,
        rw_gn_g, rw_gn_b, w_hg_st, w_rw_st, w_out_full, final_g, loss_target[0], my_core)
    loss = lax.psum(loss_b[0, 0], ("x", "y", "c"))

    g_small = {"hg_lb": g["hg_lb"], "rw_mu": g["rw_mu"][None], "rw_w0": g["rw_w0"][None], "rw_w2": g["rw_w2"][None],
               "rw_a0": g["rw_a0"][None], "rw_a2": g["rw_a2"][None]}
    split = {n: _split_shards(g_small[n]) for n in _SMALL_SHARDED}
    small_parts = jnp.stack([pack_small({n: split[n][j] for n in _SMALL_SHARDED}) for j in range(N_SHARD)], axis=0)
    def finish(name, chip_sum, landed):
        half = _sum_landed(f"grads_sum_{name}", landed, chip_sum, my_chip)
        return half, _pair_exchange(f"grads_pair_swap_{name}", half, my_core, False, F32)

    res = {}
    later = {"ada_w": g["ada_w"], "small": small_parts}
    later_chip = [_pair_exchange(f"grads_pair_sum_{n}", a, my_core, True, BF16) for n, a in later.items()]
    outs, later_landed = _adamw_halves("adamw_w_in", mat(w["w_in"]), mat(m["w_in"]), mat(v["w_in"]),
                                       *finish("w_in", *g["w_in"]), my_core, scatter=tuple(later_chip))
    res["w_in"] = [o.reshape(w["w_in"].shape) for o in outs]
    pending = {n: g[n] for n in ("w_hg_out", "w_rw_out", "w_out")}
    pending.update(zip(later, zip(later_chip, later_landed)))
    rep_shapes = [w[n].shape for n in _REPLICATED]
    rep_all = _gather_all("grads_replicated", _pack([g[n].reshape(w[n].shape) for n in _REPLICATED]))

    for n in ("ada_w", "w_hg_out", "w_rw_out", "w_out"):
        outs = _adamw_halves(f"adamw_{n}", mat(w[n]), mat(m[n]), mat(v[n]), *finish(n, *pending[n]), my_core)
        res[n] = [o.reshape(w[n].shape) for o in outs]
    outs = _adamw_halves("adamw_small", small_mine, pack_small(m), pack_small(v), *finish("small", *pending["small"]), my_core)
    for i, vals in enumerate(zip(*[_unpack(o, small_shapes) for o in outs])):
        res[_SMALL_SHARDED[i]] = list(vals)
    outs = _adamw("adamw_replicated", _pack([w[n] for n in _REPLICATED]), _pack([m[n] for n in _REPLICATED]),
                  _pack([v[n] for n in _REPLICATED]), rep_all)
    for i, vals in enumerate(zip(*[_unpack(o, rep_shapes) for o in outs])):
        res[_REPLICATED[i]] = list(vals)

    return (loss, grad_x[None], *[res[n][0] for n in _WEIGHTS], *[res[n][1] for n in _WEIGHTS],
            *[res[n][2] for n in _WEIGHTS], *[res[n][3] for n in _WEIGHTS])
```

```python
import functools

import jax
import jax.numpy as jnp
from jax import lax
from jax.experimental import pallas as pl
from jax.experimental.pallas import tpu as pltpu

HI = lax.Precision.HIGHEST
F32 = jnp.float32
BF16 = jnp.bfloat16

NORM_EPS = 1e-6
HG_HEAD = 128
RW_HEAD = 64
RW_LORA = 64
RW_GN_EPS = 64e-5
GRID_W = 64
SUB = 16
STEP = 64
RW_STEP = 64
N_SHARD = 4
N_DEV = 8
LANE = 128

ADAM_LR = 0.001
ADAM_B1 = 0.9
ADAM_B2 = 0.999
ADAM_EPS = 1e-08
ADAM_WD = 0.01
ADAM_STEP = 10

VMEM_LIMIT = 56 * 1024 * 1024


def _params(sem=None):
    return pltpu.CompilerParams(dimension_semantics=sem, vmem_limit_bytes=VMEM_LIMIT)


def _tile(n, cands):
    for c in cands:
        if n % c == 0:
            return c
    return n


def _iota2(n, m, d):
    return lax.broadcasted_iota(jnp.int32, (n, m), d)


def _before(n, rev, strict):
    t, s = _iota2(n, n, 0), _iota2(n, n, 1)
    if rev:
        return (s > t) if strict else (s >= t)
    return (s < t) if strict else (s <= t)


def _running_sum(a, axis, rev):
    n = a.shape[axis]
    shift = 1
    while shift < n:
        pad = list(a.shape)
        pad[axis] = shift
        zeros = jnp.zeros(pad, a.dtype)
        if rev:
            moved = jnp.concatenate([lax.slice_in_dim(a, shift, n, axis=axis), zeros], axis=axis)
        else:
            moved = jnp.concatenate([zeros, lax.slice_in_dim(a, 0, n - shift, axis=axis)], axis=axis)
        a = a + moved
        shift *= 2
    return a


def _sdot(a, b, spec):
    return jnp.einsum(spec, a, b, precision=lax.Precision.DEFAULT, preferred_element_type=F32)


def _hg_step(s0, qraw, iin, fin, lb2, rev):
    c, w = qraw.shape
    h = w // HG_HEAD
    nsub = c // SUB
    lb = jax.nn.sigmoid(lb2[0:1] - lb2[1:2])
    q = jax.nn.silu(qraw)
    fg = lb + (1.0 - lb) * jax.nn.sigmoid(fin)
    kk = 1.0 - fg
    g = jnp.log(fg)
    bcum = _running_sum(g, 0, rev)
    def heads(a):
        return jnp.swapaxes(a.reshape(a.shape[0], h, HG_HEAD), 0, 1)

    def unheads(a):
        return jnp.swapaxes(a, 0, 1).reshape(a.shape[1], w)

    blocks = [slice(j * SUB, (j + 1) * SUB) for j in range(nsub)]
    outs = []
    for sl in blocks:
        qs, ks, vs, bc = [a[sl].reshape(SUB, h, HG_HEAD) for a in (q, kk, iin, bcum)]
        o = jnp.zeros((SUB, h, HG_HEAD), F32)
        for si in range(SUB):
            after = slice(0, si + 1) if rev else slice(si, SUB)
            dec = jnp.exp(jnp.minimum(bc[after] - bc[si:si + 1], 0.0))
            a = jnp.sum(qs[after] * ks[si:si + 1] * dec, axis=-1, keepdims=True)
            term = a * vs[si:si + 1]
            n_rest = SUB - 1 - si if rev else si
            if n_rest:
                rest = jnp.zeros((n_rest, h, HG_HEAD), F32)
                term = jnp.concatenate([term, rest] if rev else [rest, term], axis=0)
            o = o + term
        outs.append(o.reshape(SUB, w))
    order = list(range(nsub - 1, -1, -1)) if rev else list(range(nsub))
    for pos in range(1, nsub):
        j, before = order[pos], order[:pos]
        first = (j + 1) * SUB - 1 if rev else j * SUB
        bstart = bcum[first:first + 1] - g[first:first + 1]
        qp = heads(q[blocks[j]] * jnp.exp(bcum[blocks[j]] - bstart))
        kp = heads(jnp.concatenate([kk[blocks[p]] * jnp.exp(bstart - bcum[blocks[p]]) for p in before], axis=0))
        vp = heads(jnp.concatenate([iin[blocks[p]] for p in before], axis=0))
        outs[j] = outs[j] + unheads(_sdot(_sdot(qp, kp, 'htk,hsk->hts'), vp, 'hts,hsv->htv'))
    o_state = unheads(_sdot(heads(q * jnp.exp(bcum)), s0, 'htk,hvk->htv'))
    last = 0 if rev else c - 1
    blast = bcum[last:last + 1]
    s_new = heads(jnp.exp(blast)) * s0 + _sdot(heads(iin), heads(kk * jnp.exp(blast - bcum)), 'hsv,hsk->hvk')
    return jnp.concatenate(outs, axis=0) + o_state, s_new


def _tri_solve(lmat, rhs, rev):
    hh, c, _ = lmat.shape
    nb = c // SUB
    diag = jnp.concatenate([lmat[:, i * SUB:(i + 1) * SUB, i * SUB:(i + 1) * SUB] for i in range(nb)], axis=0)
    dt = jnp.transpose(diag, (1, 2, 0))
    col = lax.broadcasted_iota(jnp.int32, (SUB, 1), 0)
    inv_rows = [None] * SUB
    order = list(range(SUB - 1, -1, -1)) if rev else list(range(SUB))
    for pos, t in enumerate(order):
        row = jnp.broadcast_to((col == t).astype(F32), (SUB, dt.shape[2]))
        for s in order[:pos]:
            row = row - dt[t, s:s + 1, :] * inv_rows[s]
        inv_rows[t] = row
    tinv = jnp.transpose(jnp.concatenate([r[None] for r in inv_rows], axis=0), (2, 0, 1))
    p = [None] * nb
    done = []
    for i in (range(nb - 1, -1, -1) if rev else range(nb)):
        r = rhs[:, i * SUB:(i + 1) * SUB]
        if done:
            lrow = jnp.concatenate([lmat[:, i * SUB:(i + 1) * SUB, m * SUB:(m + 1) * SUB] for m in done], axis=2)
            r = r - _sdot(lrow, jnp.concatenate([p[m] for m in done], axis=1), 'hts,hsv->htv')
        p[i] = _sdot(tinv[i * hh:(i + 1) * hh], r, 'hts,hsv->htv')
        done.append(i)
    return jnp.concatenate(p, axis=1)


def _rw_step(s0, r, k, v, wlo, alo, w0h, w2h, a0h, a2h, kkh, kah, rev):
    hh, c, _ = r.shape
    tl = jnp.broadcast_to(jnp.tanh(wlo)[None], (hh, c, wlo.shape[1]))
    al = jnp.broadcast_to(alo[None], (hh, c, alo.shape[1]))
    wlog = -jax.nn.softplus(-(w0h + _sdot(tl, w2h, 'hcl,hlj->hcj'))) - 0.5
    lw = -jnp.exp(wlog)
    a = jax.nn.sigmoid(a0h + _sdot(al, a2h, 'hcl,hlj->hcj'))
    kk = k * kkh
    kk = kk * lax.rsqrt(jnp.sum(kk * kk, axis=-1, keepdims=True) + 1e-12)
    kd = k * (1.0 + (a - 1.0) * kah)
    b = kk * a
    cum = _running_sum(lw, 1, rev)
    ecum, encum = jnp.exp(cum), jnp.exp(-cum)
    alpha = jnp.exp(cum - lw) * kk
    beta = b * encum
    kappa = kd * encum
    rho = r * ecum
    m_lt = _before(c, rev, True)[None]
    m_le = _before(c, rev, False)[None]
    ar = jnp.concatenate([alpha, rho], axis=1)
    kb = jnp.concatenate([kappa, beta], axis=1)
    gram = _sdot(ar, kb, 'htk,hsk->hts')
    a_kap = jnp.where(m_lt, gram[:, :c, :c], 0.0)
    a_bet = jnp.where(m_lt, gram[:, :c, c:], 0.0)
    b_kap = jnp.where(m_le, gram[:, c:, :c], 0.0)
    b_bet = jnp.where(m_le, gram[:, c:, c:], 0.0)
    from_state = _sdot(ar, s0, 'htk,hvk->htv')
    p = _tri_solve(a_bet, from_state[:, :c] + _sdot(a_kap, v, 'hts,hsv->htv'), rev)
    vp = jnp.concatenate([v, -p], axis=1)
    y = from_state[:, c:] + _sdot(jnp.concatenate([b_kap, b_bet], axis=2), vp, 'hts,hsv->htv')
    stil = s0 + _sdot(vp, kb, 'hsv,hsk->hvk')
    last = 0 if rev else c - 1
    return y, stil * ecum[:, last:last + 1, :]


def _fn_h(s, norm_g, scale, shift):
    return s * lax.rsqrt(jnp.mean(s * s, axis=-1, keepdims=True) + NORM_EPS) * norm_g * (1.0 + scale) + shift


def _fn_hgpost(of, ob, z, g):
    tm, w = of.shape
    o = (of + ob).reshape(tm, w // HG_HEAD, HG_HEAD)
    o = o * lax.rsqrt(jnp.mean(o * o, axis=-1, keepdims=True) + NORM_EPS)
    return o.reshape(tm, w) * g * jax.nn.silu(z)


def _fn_rwpost(y0, y1, r, k, v, alo, z, a0, a2, k_a, r_k, gn_g, gn_b):
    tm, w = r.shape
    nh = w // RW_HEAD
    asum = 0.0
    for d in range(2):
        asum = asum + jax.nn.sigmoid(a0[d:d + 1] + jnp.dot(alo[:, d * RW_LORA:(d + 1) * RW_LORA], a2[d],
                                                           precision=HI, preferred_element_type=F32))
    k_sum = k * (2.0 + (asum - 2.0) * k_a)
    ys = (y0 + y1).reshape(tm, nh, RW_HEAD)
    mean = jnp.mean(ys, axis=-1, keepdims=True)
    var = jnp.mean(jnp.square(ys - mean), axis=-1, keepdims=True)
    y = ((ys - mean) * lax.rsqrt(var + RW_GN_EPS)).reshape(tm, w) * gn_g + gn_b
    bonus = jnp.sum((r * k_sum * r_k).reshape(tm, nh, RW_HEAD), axis=-1, keepdims=True) * v.reshape(tm, nh, RW_HEAD)
    return (y + bonus.reshape(tm, w)) * jax.nn.silu(z)


def _fn_merge(a, b, ghg, grw):
    return jax.nn.sigmoid(ghg) * a + jax.nn.sigmoid(grw) * b


def _fn_final(xs, o, gate, final_g, tgt):
    x2 = xs + gate * o
    y = x2 * lax.rsqrt(jnp.mean(x2 * x2, axis=-1, keepdims=True) + NORM_EPS) * final_g
    return 0.5 * jnp.sum(jnp.mean(jnp.square(y - tgt), axis=-1))


def _row_call(name, fn, n_tiles, tm, row_ins, full_ins, row_outs, acc_outs):
    n_ri, n_fi, n_ro = len(row_ins), len(full_ins), len(row_outs)

    def body(*refs):
        i = pl.program_id(0)
        rvals = [r[...] for r in refs[:n_ri]]
        fvals = [r[...] for r in refs[n_ri:n_ri + n_fi]]
        outs = refs[n_ri + n_fi:]
        ro, ao = fn(i, rvals, fvals)
        for ref, val in zip(outs[:n_ro], ro):
            ref[...] = val.astype(ref.dtype)
        for ref, val in zip(outs[n_ro:], ao):
            @pl.when(i == 0)
            def _(ref=ref):
                ref[...] = jnp.zeros_like(ref)
            ref[...] += val.astype(ref.dtype)

    def rspec(width, cb, off, rows):
        return pl.BlockSpec((tm, width), lambda i: (jnp.clip(i - off, 0, rows // tm - 1), cb))

    def fspec(shape):
        nd = len(shape)
        return pl.BlockSpec(shape, lambda i: (0,) * nd)

    in_specs = [rspec(w, cb, off, a.shape[0]) for (a, cb, w, off) in row_ins] + [fspec(a.shape) for a in full_ins]
    out_specs = [rspec(w, 0, off, rows) for (rows, w, _, off) in row_outs] + [fspec(s) for (s, _) in acc_outs]
    out_shape = [jax.ShapeDtypeStruct((rows, w), dt) for (rows, w, dt, _) in row_outs] + \
                [jax.ShapeDtypeStruct(s, dt) for (s, dt) in acc_outs]
    res = pl.pallas_call(
        body, name=name, grid=(n_tiles,), in_specs=in_specs, out_specs=out_specs, out_shape=out_shape,
        compiler_params=_params(("arbitrary",)),
    )(*[a for (a, _, _, _) in row_ins], *full_ins)
    return list(res)


def _mm(name, a, b, m, n, k_steps, tm, tn, a_block, a_map, b_block, b_map, o_shape, o_block, o_map,
        contract, out_dtype=F32, scatter=()):
    ns = len(scatter)
    grid = (m // tm, n // tn, k_steps)

    def body(*refs):
        a_ref, b_ref, o_ref, acc_ref = refs[0], refs[1], refs[2 + ns], refs[3 + 2 * ns]
        kk = pl.program_id(2)
        if ns:
            sc_refs = (refs[2:2 + ns], refs[3 + ns:3 + 2 * ns]) + tuple(refs[4 + 2 * ns:])
            at = (pl.program_id(0) * grid[1] + pl.program_id(1)) * grid[2] + kk
            pl.when(at == 0)(lambda: _scatter_start(scatter, *sc_refs))

        @pl.when(kk == 0)
        def _():
            acc_ref[...] = jnp.zeros_like(acc_ref)

        acc_ref[...] += lax.dot_general(a_ref[...].astype(BF16), b_ref[...].astype(BF16),
                                        (contract, ((), ())), preferred_element_type=F32)

        @pl.when(kk == k_steps - 1)
        def _():
            o_ref[...] = acc_ref[...].astype(o_ref.dtype)

        if ns:
            pl.when(at == grid[0] * grid[1] * grid[2] - 1)(lambda: _scatter_wait(scatter, *sc_refs))

    hbm = pl.BlockSpec(memory_space=pl.ANY)
    sems = [pltpu.SemaphoreType.DMA((ns, _PEER_CHIPS))] * 2 if ns else []
    res = pl.pallas_call(
        body, name=name, grid=grid,
        in_specs=[pl.BlockSpec(a_block, a_map), pl.BlockSpec(b_block, b_map)] + [hbm] * ns,
        out_specs=[pl.BlockSpec(o_block, o_map)] + [hbm] * ns,
        out_shape=[jax.ShapeDtypeStruct(o_shape, out_dtype)] + [jax.ShapeDtypeStruct(s.shape, s.dtype) for s in scatter],
        scratch_shapes=[pltpu.VMEM((tm, tn), F32)] + sems,
        compiler_params=_params(("arbitrary",) * 3 if ns else ("parallel", "parallel", "arbitrary")),
    )(a, b, *scatter)
    return (res[0], list(res[1:])) if ns else res[0]


_TM = (768, 512, 256, 128, 64, 32, 16, 8)
_TN = (512, 256, 128)
_TK = (1024, 768, 512, 256, 128)
_TK_WIDE = (768, 512, 256, 128)
WIDE_OUT_BYTES = 32 << 20


def _tm_wide(m, ns):
    for tm in _TM:
        if m % tm == 0 and 3 * 4 * tm * ns <= WIDE_OUT_BYTES:
            return tm
    return m


def _mm_nn(name, a, b, out_dtype=F32):
    m, k = a.shape
    n = b.shape[1]
    tm, tn, tk = _tile(m, _TM), _tile(n, _TN), _tile(k, _TK)
    return _mm(name, a, b, m, n, k // tk, tm, tn, (tm, tk), lambda i, j, s: (i, s), (tk, tn), lambda i, j, s: (s, j),
               (m, n), (tm, tn), lambda i, j, s: (i, j), ((1,), (0,)), out_dtype)


def _mm_nt(name, a, b, out_dtype=F32):
    m, k = a.shape
    n = b.shape[0]
    tm, tn, tk = _tile(m, _TM), _tile(n, _TN), _tile(k, _TK)
    return _mm(name, a, b, m, n, k // tk, tm, tn, (tm, tk), lambda i, j, s: (i, s), (tn, tk), lambda i, j, s: (j, s),
               (m, n), (tm, tn), lambda i, j, s: (i, j), ((1,), (1,)), out_dtype)


def _mm_tn(name, a, b, out_dtype=F32):
    k, m = a.shape
    n = b.shape[1]
    tm, tn, tk = _tile(m, _TM), _tile(n, _TN), _tile(k, _TK)
    return _mm(name, a, b, m, n, k // tk, tm, tn, (tk, tm), lambda i, j, s: (s, i), (tk, tn), lambda i, j, s: (s, j),
               (m, n), (tm, tn), lambda i, j, s: (i, j), ((0,), (0,)), out_dtype)


def _mm_n_st(name, a, bst, out_dtype=F32, joined=False):
    m, k = a.shape
    ns_, _, ns = bst.shape
    tm, tk = _tm_wide(m, ns), _tile(k, (512, 256, 128))
    out = ((m, ns_ * ns), (tm, ns), lambda i, j, s: (i, j)) if joined else \
          ((ns_, m, ns), (None, tm, ns), lambda i, j, s: (j, i, 0))
    return _mm(name, a, bst, m, ns_ * ns, k // tk, tm, ns,
               (tm, tk), lambda i, j, s: (i, s), (None, tk, ns), lambda i, j, s: (j, s, 0), *out, ((1,), (0,)), out_dtype)


def _mm_st_t(name, ast, bst, out_dtype=F32, scatter=()):
    ns_, n, ns = bst.shape
    m = ast.shape[-2]
    tm, tn = _tile(m, _TM), _tile(n, _TN)
    a_side = ((None, tm, ns), lambda i, j, s: (s, i, 0)) if ast.ndim == 3 else ((tm, ns), lambda i, j, s: (i, s))
    return _mm(name, ast, bst, m, n, ns_, tm, tn, *a_side, (None, tn, ns), lambda i, j, s: (s, j, 0),
               (m, n), (tm, tn), lambda i, j, s: (i, j), ((1,), (1,)), out_dtype, scatter)


def _mm_t_st(name, a, bst, out_dtype=F32, scatter=(), n_shard=N_SHARD):
    k, m = a.shape
    ns = bst.shape[-1] if bst.ndim == 3 else bst.shape[-1] // n_shard
    tm, tk = _tile(m, _TN), _tile(k, _TK_WIDE)
    b_side = ((None, tk, ns), lambda i, j, s: (j, s, 0)) if bst.ndim == 3 else ((tk, ns), lambda i, j, s: (s, j))
    return _mm(name, a, bst, m, n_shard * ns, k // tk, tm, ns, (tk, tm), lambda i, j, s: (s, i), *b_side,
               (n_shard, m, ns), (None, tm, ns), lambda i, j, s: (j, i, 0), ((0,), (0,)), out_dtype, scatter)


def _scan_order(j, n_ctx, n_all, rev):
    if not rev:
        return j
    return jnp.where(j < n_ctx, n_ctx - 1 - j, n_all - 1 - (j - n_ctx))


def _hg_scan_fwd(name, p_hg, lb2, d, n_ctx):
    t, w5 = p_hg.shape
    w = w5 // 5
    h = w // HG_HEAD
    n = t // STEP
    rev = d == 1

    def body(q_ref, i_ref, f_ref, lb_ref, o_ref, st_ref, s_ref):
        j = pl.program_id(0)

        @pl.when(j == 0)
        def _():
            s_ref[...] = jnp.zeros_like(s_ref)

        s0 = s_ref[...]
        st_ref[...] = s0
        o, s1 = _hg_step(s0, q_ref[...], i_ref[...], f_ref[...], lb_ref[...], rev)
        o_ref[...] = o
        s_ref[...] = s1

    def rows(cb):
        return pl.BlockSpec((STEP, w), lambda j: (_scan_order(j, n_ctx, n, rev), cb))

    return pl.pallas_call(
        body, name=name, grid=(n,),
        in_specs=[rows(0), rows(1), rows(2 + d), pl.BlockSpec((2, w), lambda j: (0, 0))],
        out_specs=[rows(0), pl.BlockSpec((None, h, HG_HEAD, HG_HEAD), lambda j: (j, 0, 0, 0))],
        out_shape=[jax.ShapeDtypeStruct((t, w), F32), jax.ShapeDtypeStruct((n, h, HG_HEAD, HG_HEAD), F32)],
        scratch_shapes=[pltpu.VMEM((h, HG_HEAD, HG_HEAD), F32)],
        compiler_params=_params(("arbitrary",)),
    )(p_hg, p_hg, p_hg, lb2)


def _hg_scan_bwd(name, p_hg, lb2, states, do, d, n_ctx, other=()):
    t, w5 = p_hg.shape
    w = w5 // 5
    h = w // HG_HEAD
    n = t // STEP
    rev = d == 1
    no = len(other)

    def body(q_ref, i_ref, f_ref, lb_ref, st_ref, do_ref, *refs):
        dq_ref, di_ref, df_ref, dlb_ref, ds_ref = refs[no:]
        step = pl.program_id(0)

        @pl.when(step == 0)
        def _():
            ds_ref[...] = jnp.zeros_like(ds_ref)
            dlb_ref[...] = jnp.zeros_like(dlb_ref)

        _, vjp = jax.vjp(lambda s0, q, i, f, lb: _hg_step(s0, q, i, f, lb, rev),
                         st_ref[...], q_ref[...], i_ref[...], f_ref[...], lb_ref[...])
        ds0, dq, di, df, dlb = vjp((do_ref[...], ds_ref[...]))
        if no:
            dq, di = refs[0][...] + dq, refs[1][...] + di
        dq_ref[...] = dq.astype(dq_ref.dtype)
        di_ref[...] = di.astype(di_ref.dtype)
        df_ref[...] = df.astype(df_ref.dtype)
        dlb_ref[...] += dlb
        ds_ref[...] = ds0

    def rows(cb):
        return pl.BlockSpec((STEP, w), lambda s: (_scan_order(n - 1 - s, n_ctx, n, rev), cb))

    qi = BF16 if no else F32
    return pl.pallas_call(
        body, name=name, grid=(n,),
        in_specs=[rows(0), rows(1), rows(2 + d), pl.BlockSpec((2, w), lambda s: (0, 0)),
                  pl.BlockSpec((None, h, HG_HEAD, HG_HEAD), lambda s: (n - 1 - s, 0, 0, 0)), rows(0)] + [rows(0)] * no,
        out_specs=[rows(0), rows(0), rows(0), pl.BlockSpec((2, w), lambda s: (0, 0))],
        out_shape=[jax.ShapeDtypeStruct((t, w), qi)] * 2 + [jax.ShapeDtypeStruct((t, w), BF16),
                                                            jax.ShapeDtypeStruct((2, w), F32)],
        scratch_shapes=[pltpu.VMEM((h, HG_HEAD, HG_HEAD), F32)],
        compiler_params=_params(("arbitrary",)),
    )(p_hg, p_hg, p_hg, lb2, states, do, *other)


def _to_heads(a, nh):
    return jnp.stack([a[:, i * RW_HEAD:(i + 1) * RW_HEAD] for i in range(nh)], axis=0)


def _from_heads(a):
    return jnp.concatenate([a[i] for i in range(a.shape[0])], axis=-1)


def _rw_scan_fwd(name, sh, hp, d, n_ctx):
    t = sh.shape[0]
    w = (sh.shape[1] - 4 * RW_LORA) // 3
    nh = w // RW_HEAD
    n = t // RW_STEP
    rev = d == 1
    lo = 3 * w // LANE

    def body(r_ref, k_ref, v_ref, wl_ref, al_ref, w0_ref, w2_ref, a0_ref, a2_ref, kk_ref, ka_ref,
             y_ref, st_ref, s_ref):
        j = pl.program_id(0)

        @pl.when(j == 0)
        def _():
            s_ref[...] = jnp.zeros_like(s_ref)

        s0 = s_ref[...]
        st_ref[...] = s0
        wl = wl_ref[...][:, d * RW_LORA:(d + 1) * RW_LORA]
        al = al_ref[...][:, d * RW_LORA:(d + 1) * RW_LORA]
        y, s1 = _rw_step(s0, _to_heads(r_ref[...], nh), _to_heads(k_ref[...], nh), _to_heads(v_ref[...], nh), wl, al,
                         w0_ref[...], w2_ref[...], a0_ref[...], a2_ref[...], kk_ref[...], ka_ref[...], rev)
        y_ref[...] = _from_heads(y)
        s_ref[...] = s1

    def rows(cb, width=w):
        return pl.BlockSpec((RW_STEP, width), lambda j: (_scan_order(j, n_ctx, n, rev), cb))

    def whole(a):
        nd = a.ndim
        return pl.BlockSpec(a.shape, lambda j: (0,) * nd)

    return pl.pallas_call(
        body, name=name, grid=(n,),
        in_specs=[rows(0), rows(1), rows(2), rows(lo, LANE), rows(lo + 1, LANE)] + [whole(a) for a in hp],
        out_specs=[rows(0), pl.BlockSpec((None, nh, RW_HEAD, RW_HEAD), lambda j: (j, 0, 0, 0))],
        out_shape=[jax.ShapeDtypeStruct((t, w), F32), jax.ShapeDtypeStruct((n, nh, RW_HEAD, RW_HEAD), F32)],
        scratch_shapes=[pltpu.VMEM((nh, RW_HEAD, RW_HEAD), F32)],
        compiler_params=_params(("arbitrary",)),
    )(sh, sh, sh, sh, sh, *hp)


def _rw_scan_bwd_both(name, sh, hps, states, dy, n_ctx):
    t = sh.shape[0]
    w = (sh.shape[1] - 4 * RW_LORA) // 3
    nh = w // RW_HEAD
    n = t // RW_STEP
    lo = 3 * w // LANE
    n_in, n_p = 13, 6

    def body(*refs):
        step = pl.program_id(0)
        ins = [refs[d * n_in:(d + 1) * n_in] for d in range(2)]
        outs = [refs[2 * n_in + d * (1 + n_p):2 * n_in + (d + 1) * (1 + n_p)] for d in range(2)]
        ds_refs = refs[2 * n_in + 2 * (1 + n_p):]

        @pl.when(step == 0)
        def _():
            for d in range(2):
                ds_refs[d][...] = jnp.zeros_like(ds_refs[d])
                for ref in outs[d][1:]:
                    ref[...] = jnp.zeros_like(ref)

        for d in range(2):
            r_ref, k_ref, v_ref, wl_ref, al_ref = ins[d][:5]
            hp_refs, st_ref, dy_ref = ins[d][5:11], ins[d][11], ins[d][12]
            wl = wl_ref[...][:, d * RW_LORA:(d + 1) * RW_LORA]
            al = al_ref[...][:, d * RW_LORA:(d + 1) * RW_LORA]
            _, vjp = jax.vjp(functools.partial(_rw_step, rev=d == 1),
                             st_ref[...], _to_heads(r_ref[...], nh), _to_heads(k_ref[...], nh), _to_heads(v_ref[...], nh),
                             wl, al, *[p[...] for p in hp_refs])
            g = vjp((_to_heads(dy_ref[...], nh), ds_refs[d][...]))
            ds_refs[d][...] = g[0]
            zero = jnp.zeros_like(g[4])
            lora = [zero] * 4
            lora[d], lora[2 + d] = g[4], g[5]
            outs[d][0][...] = jnp.concatenate([_from_heads(g[1]), _from_heads(g[2]), _from_heads(g[3])] + lora, axis=-1)
            for ref, val in zip(outs[d][1:], g[6:]):
                ref[...] += val

    def rows(d, cb, width=w):
        return pl.BlockSpec((RW_STEP, width), lambda s: (_scan_order(n - 1 - s, n_ctx, n, d == 1), cb))

    def whole(a):
        nd = a.ndim
        return pl.BlockSpec(a.shape, lambda s: (0,) * nd)

    in_specs, operands, out_specs, out_shape = [], [], [], []
    for d in range(2):
        in_specs += [rows(d, 0), rows(d, 1), rows(d, 2), rows(d, lo, LANE), rows(d, lo + 1, LANE)]
        in_specs += [whole(a) for a in hps[d]]
        in_specs += [pl.BlockSpec((None, nh, RW_HEAD, RW_HEAD), lambda s: (n - 1 - s, 0, 0, 0)), rows(d, 0)]
        operands += [sh] * 5 + list(hps[d]) + [states[d], dy]
        out_specs += [rows(d, 0, sh.shape[1])] + [whole(a) for a in hps[d]]
        out_shape += [jax.ShapeDtypeStruct(sh.shape, F32)] + [jax.ShapeDtypeStruct(a.shape, F32) for a in hps[d]]
    res = pl.pallas_call(
        body, name=name, grid=(n,), in_specs=in_specs, out_specs=out_specs, out_shape=out_shape,
        scratch_shapes=[pltpu.VMEM((nh, RW_HEAD, RW_HEAD), F32)] * 2, compiler_params=_params(("arbitrary",)),
    )(*operands)
    return [res[0], res[1 + n_p]], [res[1:1 + n_p], res[2 + n_p:]]


def _shift_masks(t, n_ctx_rows):
    row = lax.broadcasted_iota(jnp.int32, (t, 1), 0)
    isx = row >= n_ctx_rows
    pos = jnp.where(isx, row - n_ctx_rows, row)
    col = jnp.where(isx, jnp.bitwise_and(pos, GRID_W - 1), pos)
    ncol = jnp.where(isx, GRID_W, n_ctx_rows)
    n_x = t - n_ctx_rows
    ml = col != 0
    mr = col != ncol - 1
    mu = isx & (pos >= GRID_W)
    md = isx & (pos < n_x - GRID_W)
    return ml, mr, mu, md, isx


def _shift_fwd(name, p, mu, n_ctx_rows):
    t, c = p.shape
    cw = LANE

    def body(p_ref, mu_ref, o_ref):
        x = p_ref[...]
        m = mu_ref[...]
        ml, mr, mup, mdn, isx = _shift_masks(t, n_ctx_rows)
        left = jnp.where(ml, pltpu.roll(x, 1, 0), 0.0)
        right = jnp.where(mr, pltpu.roll(x, t - 1, 0), 0.0)
        up = jnp.where(mup, pltpu.roll(x, GRID_W, 0), 0.0)
        down = jnp.where(mdn, pltpu.roll(x, t - GRID_W, 0), 0.0)
        out = x + m[0:1] * (left - x) + m[1:2] * (right - x)
        vert = m[2:3] * (up - x) + m[3:4] * (down - x)
        o_ref[...] = out + jnp.where(isx, vert, 0.0)

    return pl.pallas_call(
        body, name=name, grid=(c // cw,),
        in_specs=[pl.BlockSpec((t, cw), lambda j: (0, j)), pl.BlockSpec((4, cw), lambda j: (0, j))],
        out_specs=pl.BlockSpec((t, cw), lambda j: (0, j)),
        out_shape=jax.ShapeDtypeStruct((t, c), F32),
        compiler_params=_params(("parallel",)),
    )(p, mu)


def _shift_bwd(name, p, mu, dparts, n_ctx_rows):
    t, c = p.shape
    cw = LANE
    npart = len(dparts)

    def body(*refs):
        p_ref, mu_ref = refs[0], refs[1]
        dp_ref, dmu_ref = refs[2 + npart], refs[3 + npart]
        x = p_ref[...]
        m = mu_ref[...]
        g = refs[2][...]
        for r in refs[3:2 + npart]:
            g = g + r[...]
        ml, mr, mup, mdn, isx = _shift_masks(t, n_ctx_rows)
        left = jnp.where(ml, pltpu.roll(x, 1, 0), 0.0)
        right = jnp.where(mr, pltpu.roll(x, t - 1, 0), 0.0)
        up = jnp.where(mup, pltpu.roll(x, GRID_W, 0), 0.0)
        down = jnp.where(mdn, pltpu.roll(x, t - GRID_W, 0), 0.0)
        gx = jnp.where(isx, g, 0.0)
        dmu_ref[...] = jnp.concatenate([
            jnp.sum(g * (left - x), axis=0, keepdims=True), jnp.sum(g * (right - x), axis=0, keepdims=True),
            jnp.sum(gx * (up - x), axis=0, keepdims=True), jnp.sum(gx * (down - x), axis=0, keepdims=True)], axis=0)
        coef = 1.0 - m[0:1] - m[1:2] - jnp.where(isx, m[2:3] + m[3:4], 0.0)
        dp = coef * g
        dp = dp + m[0:1] * pltpu.roll(jnp.where(ml, g, 0.0), t - 1, 0)
        dp = dp + m[1:2] * pltpu.roll(jnp.where(mr, g, 0.0), 1, 0)
        dp = dp + m[2:3] * pltpu.roll(jnp.where(mup, g, 0.0), t - GRID_W, 0)
        dp = dp + m[3:4] * pltpu.roll(jnp.where(mdn, g, 0.0), GRID_W, 0)
        dp_ref[...] = dp.astype(dp_ref.dtype)

    col = pl.BlockSpec((t, cw), lambda j: (0, j))
    par = pl.BlockSpec((4, cw), lambda j: (0, j))
    return pl.pallas_call(
        body, name=name, grid=(c // cw,),
        in_specs=[col, par] + [col] * npart,
        out_specs=[col, par],
        out_shape=[jax.ShapeDtypeStruct((t, c), BF16), jax.ShapeDtypeStruct((4, c), F32)],
        compiler_params=_params(("parallel",)),
    )(p, mu, *dparts)


def _local_step(x, c, ctx, c_ctx, ada_st, ada_b, norm_g, w_in_st, hg_lb, hg_norm_g, rw_mu, rw_w0, rw_w2, rw_a0, rw_a2,
                rw_kk, rw_ka, rw_rk, rw_gn_g, rw_gn_b, w_hg_st, w_rw_st, w_out, final_g, tgt, my_core):
    seq, dm = x.shape
    n_ctx_rows = ctx.shape[0]
    t = seq + n_ctx_rows
    hw = hg_norm_g.shape[-1]
    rw = rw_kk.shape[-1]
    nh_rw = rw // RW_HEAD
    n_ctx = n_ctx_rows // STEP
    tm = _tile(n_ctx_rows, (256, 128, 64))
    nt = t // tm
    nct = n_ctx_rows // tm
    n_sh_cols = 3 * rw + 4 * RW_LORA

    cond = jnp.concatenate([c.reshape(1, dm), c_ctx.reshape(1, dm), jnp.zeros((6, dm), F32)], axis=0)
    final_g2 = final_g.reshape(1, dm)

    def unstack(a_st):
        return jnp.swapaxes(a_st, 0, 1).reshape(a_st.shape[1], -1)

    def restack(a, ns=N_SHARD):
        return jnp.swapaxes(a.reshape(a.shape[0], ns, -1), 0, 1)

    (sc,) = _row_call("cond_silu", lambda i, r, f: ([jax.nn.silu(r[0])], []), 1, 8, [(cond, 0, dm, 0)], [],
                      [(8, dm, F32, 0)], [])
    mod_mm = _mm_n_st("mod_mm", sc, ada_st, joined=True)

    def add(name, terms, shape):
        flat2 = [a.reshape(-1, a.shape[-1]) for a in terms]
        return _rowwise(name, lambda *v: _slot_sum(list(v)), flat2, F32).reshape(shape)

    mod = add("mod_bias", [mod_mm, jnp.broadcast_to(ada_b, (8, 3 * dm))], (8, 3 * dm))
    mod3 = mod.reshape(8, 3, dm)

    def pick(i, m3):
        r = jnp.where(i < nct, m3[1], m3[0])
        return r[0:1], r[1:2]

    tokens = [(ctx, 0, dm, 0), (x, 0, dm, nct)]

    def h_fn(i, r, f):
        shift, scale = pick(i, f[1])
        return [_fn_h(jnp.where(i < nct, r[0], r[1]), f[0], scale, shift)], []

    (h,) = _row_call("h_fwd", h_fn, nt, tm, tokens, [norm_g, mod3], [(t, dm, BF16, 0)], [])
    proj = unstack(_mm_n_st("proj_mm", h, w_in_st))
    p_hg = proj[:, :5 * hw]
    p_rs = proj[:, 5 * hw:5 * hw + n_sh_cols]
    p_zr = proj[:, 5 * hw + n_sh_cols:5 * hw + n_sh_cols + rw]
    p_gt = proj[:, 5 * hw + n_sh_cols + rw:]

    o_hg, st_hg = [], []
    for d in range(2):
        o, st = _hg_scan_fwd(f"hg_scan_fwd{d}", p_hg, hg_lb[d], d, n_ctx)
        o_hg.append(o)
        st_hg.append(st)

    def hgpost_fn(i, r, f):
        return [_fn_hgpost(r[0], r[1], r[2], f[0])], []

    hg_in = [(o_hg[0], 0, hw, 0), (o_hg[1], 0, hw, 0), (p_hg, 4, hw, 0)]
    (y_hg,) = _row_call("hg_post", hgpost_fn, nt, tm, hg_in, [hg_norm_g], [(t, hw, BF16, 0)], [])

    sh = _shift_fwd("rw_shift", p_rs, rw_mu, n_ctx_rows)
    hps = []
    for d in range(2):
        hps.append([rw_w0[d].reshape(nh_rw, 1, RW_HEAD), jnp.swapaxes(rw_w2[d].reshape(RW_LORA, nh_rw, RW_HEAD), 0, 1),
                    rw_a0[d].reshape(nh_rw, 1, RW_HEAD), jnp.swapaxes(rw_a2[d].reshape(RW_LORA, nh_rw, RW_HEAD), 0, 1),
                    rw_kk.reshape(nh_rw, 1, RW_HEAD), rw_ka.reshape(nh_rw, 1, RW_HEAD)])
    y_rw_d, st_rw = [], []
    for d in range(2):
        y, st = _rw_scan_fwd(f"rw_scan_fwd{d}", sh, hps[d], d, n_ctx_rows // RW_STEP)
        y_rw_d.append(y)
        st_rw.append(st)

    rw_full = [rw_a0, rw_a2, rw_ka, rw_rk, rw_gn_g, rw_gn_b]
    lo = 3 * rw // LANE
    rw_in = [(y_rw_d[0], 0, rw, 0), (y_rw_d[1], 0, rw, 0), (sh, 0, rw, 0), (sh, 1, rw, 0), (sh, 2, rw, 0),
             (sh, lo + 1, LANE, 0), (p_zr, 0, rw, 0)]

    def rwpost_fn(i, r, f):
        return [_fn_rwpost(*r, *f)], []

    (y_rw,) = _row_call("rw_post", rwpost_fn, nt, tm, rw_in, rw_full, [(t, rw, BF16, 0)], [])

    a_hg = _mm_n_st("hg_out_mm", y_hg, w_hg_st, joined=True)
    a_rw = _mm_n_st("rw_out_mm", y_rw, w_rw_st, joined=True)
    mg_in = [(a_hg, 0, dm, 0), (a_rw, 0, dm, 0), (p_gt, 0, dm, 0), (p_gt, 1, dm, 0)]
    (merged,) = _row_call("merge", lambda i, r, f: ([_fn_merge(*r)], []), nt, tm, mg_in, [], [(t, dm, BF16, 0)], [])
    o_out = _mm_nn("out_mm", merged, w_out)

    def final_fn(i, r, f):
        gate = f[0][0][2:3]
        loss, vjp = jax.vjp(_fn_final, r[0], r[1], gate, f[1], r[2])
        dx, do, dgate, dfg, _ = vjp(jnp.ones((), F32))
        live = i >= nct
        zero = lambda a: jnp.where(live, a, 0.0)
        dmod = jnp.concatenate([jnp.concatenate([jnp.zeros((1, 2 * dm), F32), zero(dgate)], axis=1),
                                jnp.zeros((7, 3 * dm), F32)], axis=0)
        return [zero(dx), zero(do)], [jnp.broadcast_to(zero(loss), (8, LANE)), dmod, zero(dfg)]

    fin_in = [(x, 0, dm, nct), (o_out, 0, dm, 0), (tgt, 0, dm, nct)]
    dx_res, d_o, loss_acc, dmod_gate, d_final_g = _row_call(
        "final", final_fn, nt, tm, fin_in, [mod3, final_g2], [(t, dm, F32, 0), (t, dm, BF16, 0)],
        [((8, LANE), F32), ((8, 3 * dm), F32), ((1, dm), F32)])

    g_w_out = _mm_tn("d_w_out", merged, d_o)
    d_merged = _mm_nt("d_merged", d_o, w_out)

    def merge_bwd(i, r, f):
        _, vjp = jax.vjp(_fn_merge, r[0], r[1], r[2], r[3])
        da, db, dgh, dgr = vjp(r[4])
        return [da, db, jnp.concatenate([dgh, dgr], axis=1)], []

    da_hg, da_rw, dp_gt = _row_call("merge_bwd", merge_bwd, nt, tm, mg_in + [(d_merged, 0, dm, 0)], [],
                                    [(t, dm, BF16, 0), (t, dm, BF16, 0), (t, 2 * dm, BF16, 0)], [])
    g_w_hg_st = _mm_t_st("d_w_hg", y_hg, da_hg)
    g_w_rw_st = _mm_t_st("d_w_rw", y_rw, da_rw)
    dy_hg = _mm_st_t("d_y_hg", da_hg, w_hg_st)
    dy_rw = _mm_st_t("d_y_rw", da_rw, w_rw_st)

    def hgpost_bwd(i, r, f):
        _, vjp = jax.vjp(_fn_hgpost, r[0], r[1], r[2], f[0])
        dof, _, dz, dg = vjp(r[3])
        return [dof, dz], [dg]

    do_hg, dz_hg, g_hg_norm = _row_call("hg_post_bwd", hgpost_bwd, nt, tm, hg_in + [(dy_hg, 0, hw, 0)], [hg_norm_g],
                                        [(t, hw, F32, 0), (t, hw, BF16, 0)], [((1, hw), F32)])
    dq0, di0, df0, dlb0 = _hg_scan_bwd("hg_scan_bwd0", p_hg, hg_lb[0], st_hg[0], do_hg, 0, n_ctx)
    dq, di, df1, dlb1 = _hg_scan_bwd("hg_scan_bwd1", p_hg, hg_lb[1], st_hg[1], do_hg, 1, n_ctx, other=(dq0, di0))
    g_hg_lb = jnp.stack([dlb0, dlb1], axis=0)

    def rwpost_bwd(i, r, f):
        _, vjp = jax.vjp(_fn_rwpost, *r[:7], *f)
        g = vjp(r[7])
        zl = jnp.zeros((g[5].shape[0], 2 * RW_LORA), F32)
        return [g[0], jnp.concatenate([g[2], g[3], g[4], zl, g[5]], axis=1), g[6]], list(g[7:])

    dy_sum, dsh_p, dz_rw, g_a0_p, g_a2_p, g_ka_p, g_rk, g_gn_g, g_gn_b = _row_call(
        "rw_post_bwd", rwpost_bwd, nt, tm, rw_in + [(dy_rw, 0, rw, 0)], rw_full,
        [(t, rw, F32, 0), (t, n_sh_cols, F32, 0), (t, rw, BF16, 0)], [(a.shape, F32) for a in rw_full])
    dsh_dirs, hp_grads = _rw_scan_bwd_both("rw_scan_bwd", sh, hps, st_rw, dy_sum, n_ctx_rows // RW_STEP)
    dp_rs, g_mu = _shift_bwd("rw_shift_bwd", p_rs, rw_mu, [dsh_p] + dsh_dirs, n_ctx_rows)

    def flat(a):
        if a.shape[1] == 1:
            return a.reshape(rw)
        return jnp.swapaxes(a, 0, 1).reshape(RW_LORA, rw)

    g_w0 = jnp.stack([flat(hp_grads[d][0]) for d in range(2)], axis=0)
    g_w2 = jnp.stack([flat(hp_grads[d][1]) for d in range(2)], axis=0)
    g_a0 = add("g_a0", [jnp.stack([flat(hp_grads[d][2]) for d in range(2)], axis=0), g_a0_p], (2, rw))
    g_a2 = add("g_a2", [jnp.stack([flat(hp_grads[d][3]) for d in range(2)], axis=0), g_a2_p], (2, RW_LORA, rw))
    g_kk = add("g_kk", [flat(hp_grads[0][4]).reshape(1, rw), flat(hp_grads[1][4]).reshape(1, rw)], (1, rw))
    g_ka = add("g_ka", [flat(hp_grads[0][5]).reshape(1, rw), flat(hp_grads[1][5]).reshape(1, rw), g_ka_p], (1, rw))

    dproj = jnp.concatenate([dq, di, df0, df1, dz_hg, dp_rs, dz_rw, dp_gt], axis=1)
    dproj_st = restack(dproj)
    early = {"w_hg_out": g_w_hg_st, "w_rw_out": g_w_rw_st, "w_out": g_w_out.reshape(N_SHARD, dm // N_SHARD, dm)}
    early_chip = [_pair_exchange(f"grads_pair_sum_{n}", a, my_core, True, BF16) for n, a in early.items()]
    g_w_in_st, early_landed = _mm_t_st("d_w_in", h, dproj_st, scatter=tuple(early_chip))
    w_in_chip = _pair_exchange("grads_pair_sum_w_in", g_w_in_st, my_core, True, BF16)
    dh, (w_in_landed,) = _mm_st_t("d_h", dproj_st, w_in_st, scatter=(w_in_chip,))

    def h_bwd(i, r, f):
        shift, scale = pick(i, f[1])
        is_ctx = i < nct
        _, vjp = jax.vjp(_fn_h, jnp.where(is_ctx, r[0], r[1]), f[0], scale, shift)
        ds, dg, dscale, dshift = vjp(r[2])
        row = jnp.concatenate([dshift, dscale, jnp.zeros((1, dm), F32)], axis=1)
        z = jnp.zeros_like(row)
        dmod = jnp.concatenate([jnp.where(is_ctx, z, row), jnp.where(is_ctx, row, z), jnp.zeros((6, 3 * dm), F32)], axis=0)
        return [ds + r[3]], [dg, dmod]

    grad_x, g_norm_g, dmod_h = _row_call(
        "h_bwd", h_bwd, nt, tm, tokens + [(dh, 0, dm, 0), (dx_res, 0, dm, 0)], [norm_g, mod3],
        [(seq, dm, F32, nct)], [((1, dm), F32), ((8, 3 * dm), F32)])
    dmod = add("d_mod", [dmod_h, dmod_gate], (8, 3 * dm))
    g_ada_b = add("g_ada_b", [dmod[0:1], dmod[1:2]], (1, 3 * dm))
    g_ada_st = _mm_t_st("d_ada_w", sc, dmod)
    d_sc = _mm_st_t("d_cond", dmod, ada_st)

    def cond_bwd(i, r, f):
        _, vjp = jax.vjp(jax.nn.silu, r[0])
        return [vjp(r[1])[0]], []

    (d_cond,) = _row_call("cond_bwd", cond_bwd, 1, 8, [(cond, 0, dm, 0), (d_sc, 0, dm, 0)], [], [(8, dm, F32, 0)], [])

    grads = dict(
        c_ctx=d_cond[1], ada_w=g_ada_st, ada_b=g_ada_b, norm_g=g_norm_g, w_in=(w_in_chip, w_in_landed), hg_lb=g_hg_lb,
        hg_norm_g=g_hg_norm, rw_mu=g_mu, rw_w0=g_w0, rw_w2=g_w2, rw_a0=g_a0, rw_a2=g_a2, rw_kk=g_kk, rw_ka=g_ka,
        rw_rk=g_rk, rw_gn_g=g_gn_g, rw_gn_b=g_gn_b, final_g=d_final_g.reshape(dm))
    grads.update(zip(early, zip(early_chip, early_landed)))
    return loss_acc[0:1, 0:1], grad_x, grads


def _my_place():
    return lax.axis_index("x"), lax.axis_index("y"), lax.axis_index("c")


MIN_CHUNK_BYTES = 1 << 18
ROW_ALIGN = 16


def _n_chunks(rows, row_bytes):
    for n in (8, 4, 2):
        if rows % (n * ROW_ALIGN) == 0 and rows // n * row_bytes >= MIN_CHUNK_BYTES:
            return n
    return 1


def _row_bytes(a, lead=1):
    n = a.dtype.itemsize
    for d in a.shape[lead:]:
        n *= d
    return n


def _rows(ref, start, size):
    return ref.at[pl.ds(start, size)]


def _chunked(make, start, size, n):
    cs = size // n
    return [make(start + j * cs, cs) for j in range(n)]


_PEER_CHIPS = 3


def _weights_gather(name, big, small):
    nb, na = len(big), len(big) + len(small)
    arrays = list(big) + list(small)
    n_ici = 6

    def body(*refs):
        outs = refs[na:2 * na]
        send_sems, recv_sems, fsend_sems, frecv_sems = refs[2 * na:]
        x, y, c = _my_place()
        me, sx, sy, sd = 2 * x + y, 2 * (1 - x) + y, 2 * x + (1 - y), 2 * (1 - x) + (1 - y)
        kx, ky, kd = (1 - x, y, c), (x, 1 - y, c), (1 - x, 1 - y, c)

        def ici(a, j, src_slot, dst_slot, to, r0, nr):
            return pltpu.make_async_remote_copy(
                src_ref=_rows(outs[a].at[src_slot], r0, nr), dst_ref=_rows(outs[a].at[dst_slot], r0, nr),
                send_sem=send_sems.at[a, j], recv_sem=recv_sems.at[a, j], device_id=to,
                device_id_type=pl.DeviceIdType.MESH)

        def to_sibling(a, k, slot, r0, nr):
            rows = _rows(outs[a].at[slot], r0, nr)
            return pltpu.make_async_remote_copy(
                src_ref=rows, dst_ref=rows, send_sem=fsend_sems.at[a, k], recv_sem=frecv_sems.at[a, k],
                device_id=(x, y, 1 - c), device_id_type=pl.DeviceIdType.MESH)

        def start(copies):
            for cp in copies:
                cp.start()

        geo = []
        for a in range(nb):
            half = arrays[a].shape[1] // 2
            geo.append((pl.multiple_of(c * half, ROW_ALIGN), pl.multiple_of((1 - c) * half, ROW_ALIGN), half // 2,
                        _n_chunks(half // 2, _row_bytes(arrays[a], 2))))
        plan = [(me, sx, kx, 0), (me, sx, kx, 1), (me, sy, ky, 0), (me, sy, ky, 1), (sx, sd, ky, 0), (sy, sd, kx, 1)]

        def piece(a, j):
            return geo[a][0] + plan[j][3] * geo[a][2]

        for a in range(nb):
            for j in range(4):
                start(_chunked(lambda r0, cs: ici(a, j, me, me, plan[j][2], r0, cs), piece(a, j), geo[a][2], geo[a][3]))
        for a in range(nb, na):
            rows = arrays[a].shape[1]
            for j, to in ((0, kx), (2, ky), (1, kd)):
                ici(a, j, me, me, to, 0, rows).start()
        for a in range(nb):
            for j, first in ((4, 0), (5, 3)):
                src_slot, _, to, _ = plan[j]
                ici(a, first, me, plan[first][1], plan[first][2], piece(a, first), geo[a][2]).wait_recv()
                start(_chunked(lambda r0, cs: ici(a, j, src_slot, src_slot, to, r0, cs), piece(a, j), geo[a][2], geo[a][3]))
        for a in range(nb):
            for j in (1, 2):
                ici(a, j, me, plan[j][1], plan[j][2], piece(a, j), geo[a][2]).wait_recv()
            for k, slot in ((0, sx), (1, sy)):
                start(_chunked(lambda r0, cs: to_sibling(a, k, slot, r0, cs), geo[a][0], 2 * geo[a][2], geo[a][3]))
        for a in range(nb):
            for j in (4, 5):
                ici(a, j, me, sd, plan[j][2], piece(a, j), geo[a][2]).wait_recv()
            start(_chunked(lambda r0, cs: to_sibling(a, 2, sd, r0, cs), geo[a][0], 2 * geo[a][2], geo[a][3]))
        for a in range(nb, na):
            rows = arrays[a].shape[1]
            for j, slot, to in ((0, sx, kx), (2, sy, ky), (1, sd, kd)):
                ici(a, j, me, slot, to, 0, rows).wait_recv()
        for a in range(nb):
            for k, slot in ((0, sx), (1, sy), (2, sd)):
                to_sibling(a, k, slot, geo[a][1], 2 * geo[a][2]).wait_recv()
        for a in range(nb):
            for j in range(n_ici):
                ici(a, j, me, me, plan[j][2], piece(a, j), geo[a][2]).wait_send()
            for k, slot in ((0, sx), (1, sy), (2, sd)):
                to_sibling(a, k, slot, geo[a][0], 2 * geo[a][2]).wait_send()
        for a in range(nb, na):
            rows = arrays[a].shape[1]
            for j, to in ((0, kx), (2, ky), (1, kd)):
                ici(a, j, me, me, to, 0, rows).wait_send()

    hbm = pl.BlockSpec(memory_space=pl.ANY)
    ici_sems = pltpu.SemaphoreType.DMA((na, n_ici))
    pair_sems = pltpu.SemaphoreType.DMA((na, _PEER_CHIPS))
    return pl.pallas_call(
        body, name=name, in_specs=[hbm] * na, out_specs=[hbm] * na,
        out_shape=[jax.ShapeDtypeStruct(a.shape, a.dtype) for a in arrays],
        input_output_aliases={a: a for a in range(na)}, scratch_shapes=[ici_sems, ici_sems, pair_sems, pair_sems],
    )(*arrays)


def _scatter_copy(arrays, ins, outs, send_sems, recv_sems, a, k, slot, r0, nr):
    x, y, c = _my_place()
    px, py = [(1 - x, y), (x, 1 - y), (1 - x, 1 - y)][k]
    return pltpu.make_async_remote_copy(
        src_ref=_rows(ins[a].at[2 * px + py], r0, nr), dst_ref=_rows(outs[a].at[slot], r0, nr),
        send_sem=send_sems.at[a, k], recv_sem=recv_sems.at[a, k], device_id=(px, py, c),
        device_id_type=pl.DeviceIdType.MESH)


def _scatter_start(arrays, ins, outs, send_sems, recv_sems):
    x, y, _ = _my_place()
    for a in range(len(arrays)):
        rows = arrays[a].shape[1]
        for k in range(_PEER_CHIPS):
            for cp in _chunked(lambda r0, cs: _scatter_copy(arrays, ins, outs, send_sems, recv_sems, a, k, 2 * x + y, r0, cs),
                               0, rows, _n_chunks(rows, _row_bytes(arrays[a], 2))):
                cp.start()


def _scatter_wait(arrays, ins, outs, send_sems, recv_sems):
    x, y, _ = _my_place()
    peer_slot = [2 * (1 - x) + y, 2 * x + (1 - y), 2 * (1 - x) + (1 - y)]
    for k in range(_PEER_CHIPS):
        for a in range(len(arrays)):
            _scatter_copy(arrays, ins, outs, send_sems, recv_sems, a, k, peer_slot[k], 0, arrays[a].shape[1]).wait_recv()
    for a in range(len(arrays)):
        for k in range(_PEER_CHIPS):
            _scatter_copy(arrays, ins, outs, send_sems, recv_sems, a, k, 2 * x + y, 0, arrays[a].shape[1]).wait_send()


PAIR_TILE_BYTES = 4 << 20


def _pair_exchange(name, a, place, reduce, out_dtype):
    rows, cols = a.shape[-2], a.shape[-1]
    half = rows // 2 if reduce else rows
    tr = _row_tile_for(half, cols, budget=PAIR_TILE_BYTES)
    nh = half // tr
    n_steps = (N_SHARD if reduce else 1) * nh

    def body(pc_ref, *refs):
        if reduce:
            keep_ref, send_ref, o_ref, land, send_sems, recv_sems, credit, wire = refs
            wire[...] = send_ref[...].astype(BF16)
            src = wire
        else:
            send_ref, o_ref, land, send_sems, recv_sems, credit = refs
            src = send_ref
        x, y, c = _my_place()
        other = (x, y, 1 - c)
        t = pl.program_id(0) * nh + pl.program_id(1) if reduce else pl.program_id(0)
        slot = t % 2

        @pl.when(t >= 2)
        def _():
            pl.semaphore_wait(credit, 1)

        copy = pltpu.make_async_remote_copy(
            src_ref=src, dst_ref=land.at[slot], send_sem=send_sems.at[slot], recv_sem=recv_sems.at[slot],
            device_id=other, device_id_type=pl.DeviceIdType.MESH)
        copy.start()
        copy.wait_recv()
        got = land[slot]
        o_ref[...] = ((keep_ref[...] + got.astype(F32)) if reduce else got).astype(out_dtype)
        copy.wait_send()

        @pl.when(t < n_steps - 2)
        def _():
            pl.semaphore_signal(credit, inc=1, device_id=other, device_id_type=pl.DeviceIdType.MESH)

    if reduce:
        grid = (N_SHARD, nh)
        in_specs = [pl.BlockSpec((None, tr, cols), lambda j, i, pc: (j, pc[0] * nh + i, 0)),
                    pl.BlockSpec((None, tr, cols), lambda j, i, pc: (j, (1 - pc[0]) * nh + i, 0))]
        out_spec = pl.BlockSpec((None, tr, cols), lambda j, i, pc: (j, i, 0))
        out_shape = jax.ShapeDtypeStruct((N_SHARD, half, cols), out_dtype)
        operands = (a, a)
        sem = ("arbitrary", "arbitrary")
    else:
        grid = (nh,)
        in_specs = [pl.BlockSpec((tr, cols), lambda i, pc: (i, 0))]
        out_spec = pl.BlockSpec((tr, cols), lambda i, pc: (i, 0))
        out_shape = jax.ShapeDtypeStruct((half, cols), out_dtype)
        operands = (a,)
        sem = ("arbitrary",)
    return pl.pallas_call(
        body, name=name,
        grid_spec=pltpu.PrefetchScalarGridSpec(
            num_scalar_prefetch=1, grid=grid, in_specs=in_specs, out_specs=out_spec,
            scratch_shapes=[pltpu.VMEM((2, tr, cols), BF16 if reduce else a.dtype), pltpu.SemaphoreType.DMA((2,)),
                            pltpu.SemaphoreType.DMA((2,)), pltpu.SemaphoreType.REGULAR] +
                           ([pltpu.VMEM((tr, cols), BF16)] if reduce else [])),
        out_shape=out_shape, compiler_params=_params(sem),
    )(place, *operands)


def _cast_into_slot(name, a, chip):
    rows, cols = a.shape
    tm = _row_tile_for(rows, cols)

    def body(pc_ref, a_ref, o_ref):
        o_ref[...] = a_ref[...].astype(BF16)

    return pl.pallas_call(
        body, name=name,
        grid_spec=pltpu.PrefetchScalarGridSpec(
            num_scalar_prefetch=1, grid=(rows // tm,), in_specs=[pl.BlockSpec((tm, cols), lambda i, pc: (i, 0))],
            out_specs=pl.BlockSpec((None, tm, cols), lambda i, pc: (pc[0], i, 0))),
        out_shape=jax.ShapeDtypeStruct((N_SHARD, rows, cols), BF16), compiler_params=_params(("parallel",)),
    )(chip, a)


def _sum_landed(name, landed, sent, chip):
    ns, rows, cols = landed.shape
    tm = _row_tile_for(rows, cols)

    def body(pc_ref, *refs):
        own_ref, o_ref = refs[ns], refs[ns + 1]
        me = pc_ref[0]
        terms = [jnp.where(me == j, own_ref[...], refs[j][...]).astype(F32) for j in range(ns)]
        o_ref[...] = _slot_sum(terms)

    def landed_spec(j):
        return pl.BlockSpec((None, tm, cols), lambda i, pc: (jnp.where(pc[0] == j, (j + 1) % ns, j), i, 0))

    return pl.pallas_call(
        body, name=name,
        grid_spec=pltpu.PrefetchScalarGridSpec(
            num_scalar_prefetch=1, grid=(rows // tm,),
            in_specs=[landed_spec(j) for j in range(ns)] + [pl.BlockSpec((None, tm, cols), lambda i, pc: (pc[0], i, 0))],
            out_specs=pl.BlockSpec((tm, cols), lambda i, pc: (i, 0))),
        out_shape=jax.ShapeDtypeStruct((rows, cols), F32), compiler_params=_params(("parallel",)),
    )(chip, *([landed] * ns), sent)


def _gather_all(name, a):
    def body(in_ref, out_ref, send_sems, recv_sems, local_sem):
        x, y, c = _my_place()
        me = 4 * x + 2 * y + c

        def peer(k):
            return (x ^ (k >> 2), y ^ ((k >> 1) & 1), c ^ (k & 1))

        def remote(k, land):
            return pltpu.make_async_remote_copy(
                src_ref=in_ref, dst_ref=out_ref.at[land], send_sem=send_sems.at[k - 1], recv_sem=recv_sems.at[k - 1],
                device_id=peer(k), device_id_type=pl.DeviceIdType.MESH)

        local = pltpu.make_async_copy(in_ref, out_ref.at[me], local_sem)
        local.start()
        for k in range(1, N_DEV):
            remote(k, me).start()
        for k in range(1, N_DEV):
            px, py, pc = peer(k)
            remote(k, 4 * px + 2 * py + pc).wait_recv()
        for k in range(1, N_DEV):
            remote(k, me).wait_send()
        local.wait()

    hbm = pl.BlockSpec(memory_space=pl.ANY)
    return pl.pallas_call(
        body, name=name, in_specs=[hbm], out_specs=hbm,
        out_shape=jax.ShapeDtypeStruct((N_DEV,) + a.shape, a.dtype),
        scratch_shapes=[pltpu.SemaphoreType.DMA((N_DEV - 1,)), pltpu.SemaphoreType.DMA((N_DEV - 1,)), pltpu.SemaphoreType.DMA],
    )(a)


def _row_tile_for(rows, cols, budget=1 << 20):
    if rows * cols * 4 <= budget:
        return rows
    for tm in (1024, 512, 256, 128, 64, 32, 16, 8):
        if rows % tm == 0 and tm * cols * 4 <= budget:
            return tm
    return rows


def _slot_sum(vals):
    g = vals[0]
    for v in vals[1:]:
        g = g + v
    return g


def _rowwise(name, fn, arrays, out_dtype):
    rows, cols = arrays[0].shape
    tm = _row_tile_for(rows, cols)

    def body(*refs):
        refs[-1][...] = fn(*[r[...] for r in refs[:-1]]).astype(out_dtype)

    blk = pl.BlockSpec((tm, cols), lambda i: (i, 0))
    return pl.pallas_call(
        body, name=name, grid=(rows // tm,), in_specs=[blk] * len(arrays), out_specs=blk,
        out_shape=jax.ShapeDtypeStruct((rows, cols), out_dtype), compiler_params=_params(("parallel",)),
    )(*arrays)


def _sum_slots(name, st):
    ns, rows, cols = st.shape
    tm = _row_tile_for(rows, cols)

    def body(s_ref, o_ref):
        o_ref[...] = _slot_sum([s_ref[j].astype(F32) for j in range(ns)])

    return pl.pallas_call(
        body, name=name, grid=(rows // tm,),
        in_specs=[pl.BlockSpec((ns, tm, cols), lambda i: (0, i, 0))],
        out_specs=pl.BlockSpec((tm, cols), lambda i: (i, 0)),
        out_shape=jax.ShapeDtypeStruct((rows, cols), F32),
        compiler_params=_params(("parallel",)),
    )(st)


ADAM_TILE_BYTES = 1 << 20


def _adam_update(g, p_ref, m_ref, v_ref, go_ref, d_ref, mo_ref, vo_ref):
    mn = ADAM_B1 * m_ref[...] + (1.0 - ADAM_B1) * g
    vn = ADAM_B2 * v_ref[...] + (1.0 - ADAM_B2) * jnp.square(g)
    m_hat = mn / (1.0 - ADAM_B1 ** ADAM_STEP)
    v_hat = vn / (1.0 - ADAM_B2 ** ADAM_STEP)
    go_ref[...] = g
    d_ref[...] = -ADAM_LR * (m_hat / (jnp.sqrt(v_hat) + ADAM_EPS) + ADAM_WD * p_ref[...])
    mo_ref[...] = mn
    vo_ref[...] = vn


def _adamw(name, p, m, v, gst):
    rows, cols = p.shape
    ns = gst.shape[0]
    tm = _row_tile_for(rows, cols, budget=ADAM_TILE_BYTES)

    def body(p_ref, m_ref, v_ref, g_ref, *outs):
        _adam_update(_slot_sum([g_ref[j] for j in range(ns)]), p_ref, m_ref, v_ref, *outs)

    blk = pl.BlockSpec((tm, cols), lambda i: (i, 0))
    return pl.pallas_call(
        body, name=name, grid=(rows // tm,),
        in_specs=[blk, blk, blk, pl.BlockSpec((ns, tm, cols), lambda i: (0, i, 0))],
        out_specs=[blk] * 4, out_shape=[jax.ShapeDtypeStruct((rows, cols), F32)] * 4,
        compiler_params=_params(("parallel",)),
    )(p, m, v, gst)


def _adamw_halves(name, p, m, v, mine, theirs, place, scatter=()):
    rows, cols = p.shape
    half = rows // 2
    tm = _row_tile_for(half, cols, budget=ADAM_TILE_BYTES)
    nh = half // tm
    ns = len(scatter)

    def body(pc_ref, p_ref, m_ref, v_ref, mine_ref, theirs_ref, *refs):
        if ns:
            sc_refs = (refs[:ns], refs[ns + 4:2 * ns + 4]) + tuple(refs[2 * ns + 4:])
            at = pl.program_id(0) * nh + pl.program_id(1)
            pl.when(at == 0)(lambda: _scatter_start(scatter, *sc_refs))
        g = jnp.where(pl.program_id(0) == pc_ref[0], mine_ref[...], theirs_ref[...])
        _adam_update(g, p_ref, m_ref, v_ref, *refs[ns:ns + 4])
        if ns:
            pl.when(at == 2 * nh - 1)(lambda: _scatter_wait(scatter, *sc_refs))

    blk = pl.BlockSpec((tm, cols), lambda h, i, pc: (h * nh + i, 0))
    hblk = pl.BlockSpec((tm, cols), lambda h, i, pc: (i, 0))
    hbm = pl.BlockSpec(memory_space=pl.ANY)
    res = pl.pallas_call(
        body, name=name,
        grid_spec=pltpu.PrefetchScalarGridSpec(
            num_scalar_prefetch=1, grid=(2, nh), in_specs=[blk, blk, blk, hblk, hblk] + [hbm] * ns,
            out_specs=[blk] * 4 + [hbm] * ns,
            scratch_shapes=[pltpu.SemaphoreType.DMA((ns, _PEER_CHIPS))] * 2 if ns else []),
        out_shape=[jax.ShapeDtypeStruct((rows, cols), F32)] * 4 + [jax.ShapeDtypeStruct(s.shape, s.dtype) for s in scatter],
        compiler_params=_params(("arbitrary", "arbitrary") if ns else ("parallel", "parallel")),
    )(place, p, m, v, mine, theirs, *scatter)
    return (list(res[:4]), list(res[4:])) if ns else res


def _pack(parts, width=LANE, mult=8):
    flat = jnp.concatenate([a.reshape(-1) for a in parts])
    n = flat.shape[0]
    per = width * mult
    total = -(-n // per) * per
    return jnp.pad(flat, (0, total - n)).reshape(total // width, width)


def _unpack(packed, shapes):
    flat = packed.reshape(-1)
    out, off = [], 0
    for s in shapes:
        n = 1
        for d in s:
            n *= d
        out.append(flat[off:off + n].reshape(s))
        off += n
    return out


_SMALL_SHARDED = ("hg_lb", "rw_mu", "rw_w0", "rw_w2", "rw_a0", "rw_a2")
_REPLICATED = ("c_ctx", "ada_b", "norm_g", "hg_norm_g", "rw_kk", "rw_ka", "rw_rk", "rw_gn_g", "rw_gn_b", "final_g")
_BIG = ("ada_w", "w_in", "w_hg_out", "w_rw_out", "w_out")
_WEIGHTS = ("c_ctx", "ada_w", "ada_b", "norm_g", "w_in", "hg_lb", "hg_norm_g", "rw_mu", "rw_w0", "rw_w2", "rw_a0", "rw_a2",
            "rw_kk", "rw_ka", "rw_rk", "rw_gn_g", "rw_gn_b", "w_hg_out", "w_rw_out", "w_out", "final_g")


def _join_shards(st):
    a = jnp.moveaxis(st, 0, -2)
    return a.reshape(a.shape[:-2] + (a.shape[-2] * a.shape[-1],))


def _split_shards(a):
    s = a.reshape(a.shape[:-1] + (N_SHARD, a.shape[-1] // N_SHARD))
    return jnp.moveaxis(s, -2, 0)


def kernel(x, c, ctx, c_ctx, ada_w, ada_b, norm_g, w_in, hg_lb, hg_norm_g, rw_mu, rw_w0, rw_w2, rw_a0, rw_a2, rw_kk, rw_ka, rw_rk, rw_gn_g, rw_gn_b, w_hg_out, w_rw_out, w_out, final_g, loss_target, m_c_ctx, m_ada_w, m_ada_b, m_norm_g, m_w_in, m_hg_lb, m_hg_norm_g, m_rw_mu, m_rw_w0, m_rw_w2, m_rw_a0, m_rw_a2, m_rw_kk, m_rw_ka, m_rw_rk, m_rw_gn_g, m_rw_gn_b, m_w_hg_out, m_w_rw_out, m_w_out, m_final_g, v_c_ctx, v_ada_w, v_ada_b, v_norm_g, v_w_in, v_hg_lb, v_hg_norm_g, v_rw_mu, v_rw_w0, v_rw_w2, v_rw_a0, v_rw_a2, v_rw_kk, v_rw_ka, v_rw_rk, v_rw_gn_g, v_rw_gn_b, v_w_hg_out, v_w_rw_out, v_w_out, v_final_g):
    w = dict(c_ctx=c_ctx, ada_w=ada_w, ada_b=ada_b, norm_g=norm_g, w_in=w_in, hg_lb=hg_lb, hg_norm_g=hg_norm_g, rw_mu=rw_mu,
             rw_w0=rw_w0, rw_w2=rw_w2, rw_a0=rw_a0, rw_a2=rw_a2, rw_kk=rw_kk, rw_ka=rw_ka, rw_rk=rw_rk, rw_gn_g=rw_gn_g,
             rw_gn_b=rw_gn_b, w_hg_out=w_hg_out, w_rw_out=w_rw_out, w_out=w_out, final_g=final_g)
    m = dict(c_ctx=m_c_ctx, ada_w=m_ada_w, ada_b=m_ada_b, norm_g=m_norm_g, w_in=m_w_in, hg_lb=m_hg_lb, hg_norm_g=m_hg_norm_g,
             rw_mu=m_rw_mu, rw_w0=m_rw_w0, rw_w2=m_rw_w2, rw_a0=m_rw_a0, rw_a2=m_rw_a2, rw_kk=m_rw_kk, rw_ka=m_rw_ka,
             rw_rk=m_rw_rk, rw_gn_g=m_rw_gn_g, rw_gn_b=m_rw_gn_b, w_hg_out=m_w_hg_out, w_rw_out=m_w_rw_out, w_out=m_w_out,
             final_g=m_final_g)
    v = dict(c_ctx=v_c_ctx, ada_w=v_ada_w, ada_b=v_ada_b, norm_g=v_norm_g, w_in=v_w_in, hg_lb=v_hg_lb, hg_norm_g=v_hg_norm_g,
             rw_mu=v_rw_mu, rw_w0=v_rw_w0, rw_w2=v_rw_w2, rw_a0=v_rw_a0, rw_a2=v_rw_a2, rw_kk=v_rw_kk, rw_ka=v_rw_ka,
             rw_rk=v_rw_rk, rw_gn_g=v_rw_gn_g, rw_gn_b=v_rw_gn_b, w_hg_out=v_w_hg_out, w_rw_out=v_w_rw_out, w_out=v_w_out,
             final_g=v_final_g)

    def mat(a):
        return a.reshape(a.shape[-2], a.shape[-1])

    def pack_small(d):
        return _pack([d[n] for n in _SMALL_SHARDED], mult=2 * ROW_ALIGN)

    my_core = lax.axis_index("c").astype(jnp.int32).reshape(1)
    my_chip = (2 * lax.axis_index("x") + lax.axis_index("y")).astype(jnp.int32).reshape(1)

    small_shapes = [w[n].shape for n in _SMALL_SHARDED]
    big_bf = [_cast_into_slot(f"to_bf16_{n}", mat(w[n]), my_chip) for n in _BIG]
    small_mine = pack_small(w)
    small_slots = lax.dynamic_update_slice(jnp.zeros((N_SHARD,) + small_mine.shape, F32), small_mine[None], (my_chip[0], 0, 0))
    gathered = _weights_gather("weights_gather", big_bf, [small_slots])
    ada_st, w_in_st, w_hg_st, w_rw_st, w_out_st, small_st = gathered
    full_small = {}
    per_chip = [_unpack(small_st[j], small_shapes) for j in range(N_SHARD)]
    for i, n in enumerate(_SMALL_SHARDED):
        full_small[n] = _join_shards(jnp.stack([per_chip[j][i] for j in range(N_SHARD)], axis=0))
    dm = x.shape[-1]
    w_out_full = w_out_st.reshape(dm, dm)

    loss_b, grad_x, g = _local_step(
        x[0], c, ctx[0], c_ctx, ada_st, ada_b, norm_g, w_in_st, full_small["hg_lb"], hg_norm_g, full_small["rw_mu"][0],
        full_small["rw_w0"][0], full_small["rw_w2"][0], full_small["rw_a0"][0], full_small["rw_a2"][0], rw_kk, rw_ka, rw_rk,
        rw_gn_g, rw_gn_b, w_hg_st, w_rw_st, w_out_full, final_g, loss_target[0], my_core)
    loss = lax.psum(loss_b[0, 0], ("x", "y", "c"))

    g_small = {"hg_lb": g["hg_lb"], "rw_mu": g["rw_mu"][None], "rw_w0": g["rw_w0"][None], "rw_w2": g["rw_w2"][None],
               "rw_a0": g["rw_a0"][None], "rw_a2": g["rw_a2"][None]}
    split = {n: _split_shards(g_small[n]) for n in _SMALL_SHARDED}
    small_parts = jnp.stack([pack_small({n: split[n][j] for n in _SMALL_SHARDED}) for j in range(N_SHARD)], axis=0)
    def finish(name, chip_sum, landed):
        half = _sum_landed(f"grads_sum_{name}", landed, chip_sum, my_chip)
        return half, _pair_exchange(f"grads_pair_swap_{name}", half, my_core, False, F32)

    res = {}
    later = {"ada_w": g["ada_w"], "small": small_parts}
    later_chip = [_pair_exchange(f"grads_pair_sum_{n}", a, my_core, True, BF16) for n, a in later.items()]
    outs, later_landed = _adamw_halves("adamw_w_in", mat(w["w_in"]), mat(m["w_in"]), mat(v["w_in"]),
                                       *finish("w_in", *g["w_in"]), my_core, scatter=tuple(later_chip))
    res["w_in"] = [o.reshape(w["w_in"].shape) for o in outs]
    pending = {n: g[n] for n in ("w_hg_out", "w_rw_out", "w_out")}
    pending.update(zip(later, zip(later_chip, later_landed)))
    rep_shapes = [w[n].shape for n in _REPLICATED]
    rep_all = _gather_all("grads_replicated", _pack([g[n].reshape(w[n].shape) for n in _REPLICATED]))

    for n in ("ada_w", "w_hg_out", "w_rw_out", "w_out"):
        outs = _adamw_halves(f"adamw_{n}", mat(w[n]), mat(m[n]), mat(v[n]), *finish(n, *pending[n]), my_core)
        res[n] = [o.reshape(w[n].shape) for o in outs]
    outs = _adamw_halves("adamw_small", small_mine, pack_small(m), pack_small(v), *finish("small", *pending["small"]), my_core)
    for i, vals in enumerate(zip(*[_unpack(o, small_shapes) for o in outs])):
        res[_SMALL_SHARDED[i]] = list(vals)
    outs = _adamw("adamw_replicated", _pack([w[n] for n in _REPLICATED]), _pack([m[n] for n in _REPLICATED]),
                  _pack([v[n] for n in _REPLICATED]), rep_all)
    for i, vals in enumerate(zip(*[_unpack(o, rep_shapes) for o in outs])):
        res[_REPLICATED[i]] = list(vals)

    return (loss, grad_x[None], *[res[n][0] for n in _WEIGHTS], *[res[n][1] for n in _WEIGHTS],
            *[res[n][2] for n in _WEIGHTS], *[res[n][3] for n in _WEIGHTS])
```

```python
import functools

import jax
import jax.numpy as jnp
from jax import lax
from jax.experimental import pallas as pl
from jax.experimental.pallas import tpu as pltpu

HI = lax.Precision.HIGHEST
F32 = jnp.float32
BF16 = jnp.bfloat16

NORM_EPS = 1e-6
HG_HEAD = 128
RW_HEAD = 64
RW_LORA = 64
RW_GN_EPS = 64e-5
GRID_W = 64
SUB = 16
STEP = 64
RW_STEP = 64
N_SHARD = 4
N_DEV = 8
LANE = 128

ADAM_LR = 0.001
ADAM_B1 = 0.9
ADAM_B2 = 0.999
ADAM_EPS = 1e-08
ADAM_WD = 0.01
ADAM_STEP = 10

VMEM_LIMIT = 56 * 1024 * 1024


def _params(sem=None):
    return pltpu.CompilerParams(dimension_semantics=sem, vmem_limit_bytes=VMEM_LIMIT)


def _tile(n, cands):
    for c in cands:
        if n % c == 0:
            return c
    return n


def _iota2(n, m, d):
    return lax.broadcasted_iota(jnp.int32, (n, m), d)


def _before(n, rev, strict):
    t, s = _iota2(n, n, 0), _iota2(n, n, 1)
    if rev:
        return (s > t) if strict else (s >= t)
    return (s < t) if strict else (s <= t)


def _running_sum(a, axis, rev):
    n = a.shape[axis]
    shift = 1
    while shift < n:
        pad = list(a.shape)
        pad[axis] = shift
        zeros = jnp.zeros(pad, a.dtype)
        if rev:
            moved = jnp.concatenate([lax.slice_in_dim(a, shift, n, axis=axis), zeros], axis=axis)
        else:
            moved = jnp.concatenate([zeros, lax.slice_in_dim(a, 0, n - shift, axis=axis)], axis=axis)
        a = a + moved
        shift *= 2
    return a


def _sdot(a, b, spec):
    return jnp.einsum(spec, a, b, precision=lax.Precision.DEFAULT, preferred_element_type=F32)


def _hg_step(s0, qraw, iin, fin, lb2, rev):
    c, w = qraw.shape
    h = w // HG_HEAD
    nsub = c // SUB
    lb = jax.nn.sigmoid(lb2[0:1] - lb2[1:2])
    q = jax.nn.silu(qraw)
    fg = lb + (1.0 - lb) * jax.nn.sigmoid(fin)
    kk = 1.0 - fg
    g = jnp.log(fg)
    bcum = _running_sum(g, 0, rev)
    def heads(a):
        return jnp.swapaxes(a.reshape(a.shape[0], h, HG_HEAD), 0, 1)

    def unheads(a):
        return jnp.swapaxes(a, 0, 1).reshape(a.shape[1], w)

    blocks = [slice(j * SUB, (j + 1) * SUB) for j in range(nsub)]
    outs = []
    for sl in blocks:
        qs, ks, vs, bc = [a[sl].reshape(SUB, h, HG_HEAD) for a in (q, kk, iin, bcum)]
        o = jnp.zeros((SUB, h, HG_HEAD), F32)
        for si in range(SUB):
            after = slice(0, si + 1) if rev else slice(si, SUB)
            dec = jnp.exp(jnp.minimum(bc[after] - bc[si:si + 1], 0.0))
            a = jnp.sum(qs[after] * ks[si:si + 1] * dec, axis=-1, keepdims=True)
            term = a * vs[si:si + 1]
            n_rest = SUB - 1 - si if rev else si
            if n_rest:
                rest = jnp.zeros((n_rest, h, HG_HEAD), F32)
                term = jnp.concatenate([term, rest] if rev else [rest, term], axis=0)
            o = o + term
        outs.append(o.reshape(SUB, w))
    order = list(range(nsub - 1, -1, -1)) if rev else list(range(nsub))
    for pos in range(1, nsub):
        j, before = order[pos], order[:pos]
        first = (j + 1) * SUB - 1 if rev else j * SUB
        bstart = bcum[first:first + 1] - g[first:first + 1]
        qp = heads(q[blocks[j]] * jnp.exp(bcum[blocks[j]] - bstart))
        kp = heads(jnp.concatenate([kk[blocks[p]] * jnp.exp(bstart - bcum[blocks[p]]) for p in before], axis=0))
        vp = heads(jnp.concatenate([iin[blocks[p]] for p in before], axis=0))
        outs[j] = outs[j] + unheads(_sdot(_sdot(qp, kp, 'htk,hsk->hts'), vp, 'hts,hsv->htv'))
    o_state = unheads(_sdot(heads(q * jnp.exp(bcum)), s0, 'htk,hvk->htv'))
    last = 0 if rev else c - 1
    blast = bcum[last:last + 1]
    s_new = heads(jnp.exp(blast)) * s0 + _sdot(heads(iin), heads(kk * jnp.exp(blast - bcum)), 'hsv,hsk->hvk')
    return jnp.concatenate(outs, axis=0) + o_state, s_new


def _tri_solve(lmat, rhs, rev):
    hh, c, _ = lmat.shape
    nb = c // SUB
    diag = jnp.concatenate([lmat[:, i * SUB:(i + 1) * SUB, i * SUB:(i + 1) * SUB] for i in range(nb)], axis=0)
    dt = jnp.transpose(diag, (1, 2, 0))
    col = lax.broadcasted_iota(jnp.int32, (SUB, 1), 0)
    inv_rows = [None] * SUB
    order = list(range(SUB - 1, -1, -1)) if rev else list(range(SUB))
    for pos, t in enumerate(order):
        row = jnp.broadcast_to((col == t).astype(F32), (SUB, dt.shape[2]))
        for s in order[:pos]:
            row = row - dt[t, s:s + 1, :] * inv_rows[s]
        inv_rows[t] = row
    tinv = jnp.transpose(jnp.concatenate([r[None] for r in inv_rows], axis=0), (2, 0, 1))
    p = [None] * nb
    done = []
    for i in (range(nb - 1, -1, -1) if rev else range(nb)):
        r = rhs[:, i * SUB:(i + 1) * SUB]
        if done:
            lrow = jnp.concatenate([lmat[:, i * SUB:(i + 1) * SUB, m * SUB:(m + 1) * SUB] for m in done], axis=2)
            r = r - _sdot(lrow, jnp.concatenate([p[m] for m in done], axis=1), 'hts,hsv->htv')
        p[i] = _sdot(tinv[i * hh:(i + 1) * hh], r, 'hts,hsv->htv')
        done.append(i)
    return jnp.concatenate(p, axis=1)


def _rw_step(s0, r, k, v, wlo, alo, w0h, w2h, a0h, a2h, kkh, kah, rev):
    hh, c, _ = r.shape
    tl = jnp.broadcast_to(jnp.tanh(wlo)[None], (hh, c, wlo.shape[1]))
    al = jnp.broadcast_to(alo[None], (hh, c, alo.shape[1]))
    wlog = -jax.nn.softplus(-(w0h + _sdot(tl, w2h, 'hcl,hlj->hcj'))) - 0.5
    lw = -jnp.exp(wlog)
    a = jax.nn.sigmoid(a0h + _sdot(al, a2h, 'hcl,hlj->hcj'))
    kk = k * kkh
    kk = kk * lax.rsqrt(jnp.sum(kk * kk, axis=-1, keepdims=True) + 1e-12)
    kd = k * (1.0 + (a - 1.0) * kah)
    b = kk * a
    cum = _running_sum(lw, 1, rev)
    ecum, encum = jnp.exp(cum), jnp.exp(-cum)
    alpha = jnp.exp(cum - lw) * kk
    beta = b * encum
    kappa = kd * encum
    rho = r * ecum
    m_lt = _before(c, rev, True)[None]
    m_le = _before(c, rev, False)[None]
    ar = jnp.concatenate([alpha, rho], axis=1)
    kb = jnp.concatenate([kappa, beta], axis=1)
    gram = _sdot(ar, kb, 'htk,hsk->hts')
    a_kap = jnp.where(m_lt, gram[:, :c, :c], 0.0)
    a_bet = jnp.where(m_lt, gram[:, :c, c:], 0.0)
    b_kap = jnp.where(m_le, gram[:, c:, :c], 0.0)
    b_bet = jnp.where(m_le, gram[:, c:, c:], 0.0)
    from_state = _sdot(ar, s0, 'htk,hvk->htv')
    p = _tri_solve(a_bet, from_state[:, :c] + _sdot(a_kap, v, 'hts,hsv->htv'), rev)
    vp = jnp.concatenate([v, -p], axis=1)
    y = from_state[:, c:] + _sdot(jnp.concatenate([b_kap, b_bet], axis=2), vp, 'hts,hsv->htv')
    stil = s0 + _sdot(vp, kb, 'hsv,hsk->hvk')
    last = 0 if rev else c - 1
    return y, stil * ecum[:, last:last + 1, :]


def _fn_h(s, norm_g, scale, shift):
    return s * lax.rsqrt(jnp.mean(s * s, axis=-1, keepdims=True) + NORM_EPS) * norm_g * (1.0 + scale) + shift


def _fn_hgpost(of, ob, z, g):
    tm, w = of.shape
    o = (of + ob).reshape(tm, w // HG_HEAD, HG_HEAD)
    o = o * lax.rsqrt(jnp.mean(o * o, axis=-1, keepdims=True) + NORM_EPS)
    return o.reshape(tm, w) * g * jax.nn.silu(z)


def _fn_rwpost(y0, y1, r, k, v, alo, z, a0, a2, k_a, r_k, gn_g, gn_b):
    tm, w = r.shape
    nh = w // RW_HEAD
    asum = 0.0
    for d in range(2):
        asum = asum + jax.nn.sigmoid(a0[d:d + 1] + jnp.dot(alo[:, d * RW_LORA:(d + 1) * RW_LORA], a2[d],
                                                           precision=HI, preferred_element_type=F32))
    k_sum = k * (2.0 + (asum - 2.0) * k_a)
    ys = (y0 + y1).reshape(tm, nh, RW_HEAD)
    mean = jnp.mean(ys, axis=-1, keepdims=True)
    var = jnp.mean(jnp.square(ys - mean), axis=-1, keepdims=True)
    y = ((ys - mean) * lax.rsqrt(var + RW_GN_EPS)).reshape(tm, w) * gn_g + gn_b
    bonus = jnp.sum((r * k_sum * r_k).reshape(tm, nh, RW_HEAD), axis=-1, keepdims=True) * v.reshape(tm, nh, RW_HEAD)
    return (y + bonus.reshape(tm, w)) * jax.nn.silu(z)


def _fn_merge(a, b, ghg, grw):
    return jax.nn.sigmoid(ghg) * a + jax.nn.sigmoid(grw) * b


def _fn_final(xs, o, gate, final_g, tgt):
    x2 = xs + gate * o
    y = x2 * lax.rsqrt(jnp.mean(x2 * x2, axis=-1, keepdims=True) + NORM_EPS) * final_g
    return 0.5 * jnp.sum(jnp.mean(jnp.square(y - tgt), axis=-1))


def _row_call(name, fn, n_tiles, tm, row_ins, full_ins, row_outs, acc_outs):
    n_ri, n_fi, n_ro = len(row_ins), len(full_ins), len(row_outs)

    def body(*refs):
        i = pl.program_id(0)
        rvals = [r[...] for r in refs[:n_ri]]
        fvals = [r[...] for r in refs[n_ri:n_ri + n_fi]]
        outs = refs[n_ri + n_fi:]
        ro, ao = fn(i, rvals, fvals)
        for ref, val in zip(outs[:n_ro], ro):
            ref[...] = val.astype(ref.dtype)
        for ref, val in zip(outs[n_ro:], ao):
            @pl.when(i == 0)
            def _(ref=ref):
                ref[...] = jnp.zeros_like(ref)
            ref[...] += val.astype(ref.dtype)

    def rspec(width, cb, off, rows):
        return pl.BlockSpec((tm, width), lambda i: (jnp.clip(i - off, 0, rows // tm - 1), cb))

    def fspec(shape):
        nd = len(shape)
        return pl.BlockSpec(shape, lambda i: (0,) * nd)

    in_specs = [rspec(w, cb, off, a.shape[0]) for (a, cb, w, off) in row_ins] + [fspec(a.shape) for a in full_ins]
    out_specs = [rspec(w, 0, off, rows) for (rows, w, _, off) in row_outs] + [fspec(s) for (s, _) in acc_outs]
    out_shape = [jax.ShapeDtypeStruct((rows, w), dt) for (rows, w, dt, _) in row_outs] + \
                [jax.ShapeDtypeStruct(s, dt) for (s, dt) in acc_outs]
    res = pl.pallas_call(
        body, name=name, grid=(n_tiles,), in_specs=in_specs, out_specs=out_specs, out_shape=out_shape,
        compiler_params=_params(("arbitrary",)),
    )(*[a for (a, _, _, _) in row_ins], *full_ins)
    return list(res)


def _mm(name, a, b, m, n, k_steps, tm, tn, a_block, a_map, b_block, b_map, o_shape, o_block, o_map,
        contract, out_dtype=F32, scatter=()):
    ns = len(scatter)
    grid = (m // tm, n // tn, k_steps)

    def body(*refs):
        a_ref, b_ref, o_ref, acc_ref = refs[0], refs[1], refs[2 + ns], refs[3 + 2 * ns]
        kk = pl.program_id(2)
        if ns:
            sc_refs = (refs[2:2 + ns], refs[3 + ns:3 + 2 * ns]) + tuple(refs[4 + 2 * ns:])
            at = (pl.program_id(0) * grid[1] + pl.program_id(1)) * grid[2] + kk
            pl.when(at == 0)(lambda: _scatter_start(scatter, *sc_refs))

        @pl.when(kk == 0)
        def _():
            acc_ref[...] = jnp.zeros_like(acc_ref)

        acc_ref[...] += lax.dot_general(a_ref[...].astype(BF16), b_ref[...].astype(BF16),
                                        (contract, ((), ())), preferred_element_type=F32)

        @pl.when(kk == k_steps - 1)
        def _():
            o_ref[...] = acc_ref[...].astype(o_ref.dtype)

        if ns:
            pl.when(at == grid[0] * grid[1] * grid[2] - 1)(lambda: _scatter_wait(scatter, *sc_refs))

    hbm = pl.BlockSpec(memory_space=pl.ANY)
    sems = [pltpu.SemaphoreType.DMA((ns, _PEER_CHIPS))] * 2 if ns else []
    res = pl.pallas_call(
        body, name=name, grid=grid,
        in_specs=[pl.BlockSpec(a_block, a_map), pl.BlockSpec(b_block, b_map)] + [hbm] * ns,
        out_specs=[pl.BlockSpec(o_block, o_map)] + [hbm] * ns,
        out_shape=[jax.ShapeDtypeStruct(o_shape, out_dtype)] + [jax.ShapeDtypeStruct(s.shape, s.dtype) for s in scatter],
        scratch_shapes=[pltpu.VMEM((tm, tn), F32)] + sems,
        compiler_params=_params(("arbitrary",) * 3 if ns else ("parallel", "parallel", "arbitrary")),
    )(a, b, *scatter)
    return (res[0], list(res[1:])) if ns else res[0]


_TM = (768, 512, 256, 128, 64, 32, 16, 8)
_TN = (512, 256, 128)
_TK = (1024, 768, 512, 256, 128)
_TK_WIDE = (768, 512, 256, 128)
WIDE_OUT_BYTES = 32 << 20


def _tm_wide(m, ns):
    for tm in _TM:
        if m % tm == 0 and 3 * 4 * tm * ns <= WIDE_OUT_BYTES:
            return tm
    return m


def _mm_nn(name, a, b, out_dtype=F32):
    m, k = a.shape
    n = b.shape[1]
    tm, tn, tk = _tile(m, _TM), _tile(n, _TN), _tile(k, _TK)
    return _mm(name, a, b, m, n, k // tk, tm, tn, (tm, tk), lambda i, j, s: (i, s), (tk, tn), lambda i, j, s: (s, j),
               (m, n), (tm, tn), lambda i, j, s: (i, j), ((1,), (0,)), out_dtype)


def _mm_nt(name, a, b, out_dtype=F32):
    m, k = a.shape
    n = b.shape[0]
    tm, tn, tk = _tile(m, _TM), _tile(n, _TN), _tile(k, _TK)
    return _mm(name, a, b, m, n, k // tk, tm, tn, (tm, tk), lambda i, j, s: (i, s), (tn, tk), lambda i, j, s: (j, s),
               (m, n), (tm, tn), lambda i, j, s: (i, j), ((1,), (1,)), out_dtype)


def _mm_tn(name, a, b, out_dtype=F32):
    k, m = a.shape
    n = b.shape[1]
    tm, tn, tk = _tile(m, _TM), _tile(n, _TN), _tile(k, _TK)
    return _mm(name, a, b, m, n, k // tk, tm, tn, (tk, tm), lambda i, j, s: (s, i), (tk, tn), lambda i, j, s: (s, j),
               (m, n), (tm, tn), lambda i, j, s: (i, j), ((0,), (0,)), out_dtype)


def _mm_n_st(name, a, bst, out_dtype=F32, joined=False):
    m, k = a.shape
    ns_, _, ns = bst.shape
    tm, tk = _tm_wide(m, ns), _tile(k, (512, 256, 128))
    out = ((m, ns_ * ns), (tm, ns), lambda i, j, s: (i, j)) if joined else \
          ((ns_, m, ns), (None, tm, ns), lambda i, j, s: (j, i, 0))
    return _mm(name, a, bst, m, ns_ * ns, k // tk, tm, ns,
               (tm, tk), lambda i, j, s: (i, s), (None, tk, ns), lambda i, j, s: (j, s, 0), *out, ((1,), (0,)), out_dtype)


def _mm_st_t(name, ast, bst, out_dtype=F32, scatter=()):
    ns_, n, ns = bst.shape
    m = ast.shape[-2]
    tm, tn = _tile(m, _TM), _tile(n, _TN)
    a_side = ((None, tm, ns), lambda i, j, s: (s, i, 0)) if ast.ndim == 3 else ((tm, ns), lambda i, j, s: (i, s))
    return _mm(name, ast, bst, m, n, ns_, tm, tn, *a_side, (None, tn, ns), lambda i, j, s: (s, j, 0),
               (m, n), (tm, tn), lambda i, j, s: (i, j), ((1,), (1,)), out_dtype, scatter)


def _mm_t_st(name, a, bst, out_dtype=F32, scatter=(), n_shard=N_SHARD):
    k, m = a.shape
    ns = bst.shape[-1] if bst.ndim == 3 else bst.shape[-1] // n_shard
    tm, tk = _tile(m, _TN), _tile(k, _TK_WIDE)
    b_side = ((None, tk, ns), lambda i, j, s: (j, s, 0)) if bst.ndim == 3 else ((tk, ns), lambda i, j, s: (s, j))
    return _mm(name, a, bst, m, n_shard * ns, k // tk, tm, ns, (tk, tm), lambda i, j, s: (s, i), *b_side,
               (n_shard, m, ns), (None, tm, ns), lambda i, j, s: (j, i, 0), ((0,), (0,)), out_dtype, scatter)


def _scan_order(j, n_ctx, n_all, rev):
    if not rev:
        return j
    return jnp.where(j < n_ctx, n_ctx - 1 - j, n_all - 1 - (j - n_ctx))


def _hg_scan_fwd(name, p_hg, lb2, d, n_ctx):
    t, w = p_hg.shape[0], lb2.shape[1]
    h = w // HG_HEAD
    n = t // STEP
    rev = d == 1

    def body(q_ref, i_ref, f_ref, lb_ref, o_ref, st_ref, s_ref):
        j = pl.program_id(0)

        @pl.when(j == 0)
        def _():
            s_ref[...] = jnp.zeros_like(s_ref)

        s0 = s_ref[...]
        st_ref[...] = s0
        o, s1 = _hg_step(s0, q_ref[...], i_ref[...], f_ref[...], lb_ref[...], rev)
        o_ref[...] = o
        s_ref[...] = s1

    def rows(cb):
        return pl.BlockSpec((STEP, w), lambda j: (_scan_order(j, n_ctx, n, rev), cb))

    return pl.pallas_call(
        body, name=name, grid=(n,),
        in_specs=[rows(0), rows(1), rows(2 + d), pl.BlockSpec((2, w), lambda j: (0, 0))],
        out_specs=[rows(0), pl.BlockSpec((None, h, HG_HEAD, HG_HEAD), lambda j: (j, 0, 0, 0))],
        out_shape=[jax.ShapeDtypeStruct((t, w), F32), jax.ShapeDtypeStruct((n, h, HG_HEAD, HG_HEAD), F32)],
        scratch_shapes=[pltpu.VMEM((h, HG_HEAD, HG_HEAD), F32)],
        compiler_params=_params(("arbitrary",)),
    )(p_hg, p_hg, p_hg, lb2)


def _hg_scan_bwd(name, p_hg, lb2, states, do, d, n_ctx, other=()):
    t, w = p_hg.shape[0], lb2.shape[1]
    h = w // HG_HEAD
    n = t // STEP
    rev = d == 1
    no = len(other)

    def body(q_ref, i_ref, f_ref, lb_ref, st_ref, do_ref, *refs):
        dq_ref, di_ref, df_ref, dlb_ref, ds_ref = refs[no:]
        step = pl.program_id(0)

        @pl.when(step == 0)
        def _():
            ds_ref[...] = jnp.zeros_like(ds_ref)
            dlb_ref[...] = jnp.zeros_like(dlb_ref)

        _, vjp = jax.vjp(lambda s0, q, i, f, lb: _hg_step(s0, q, i, f, lb, rev),
                         st_ref[...], q_ref[...], i_ref[...], f_ref[...], lb_ref[...])
        ds0, dq, di, df, dlb = vjp((do_ref[...], ds_ref[...]))
        if no:
            dq, di = refs[0][...] + dq, refs[1][...] + di
        dq_ref[...] = dq.astype(dq_ref.dtype)
        di_ref[...] = di.astype(di_ref.dtype)
        df_ref[...] = df.astype(df_ref.dtype)
        dlb_ref[...] += dlb
        ds_ref[...] = ds0

    def rows(cb):
        return pl.BlockSpec((STEP, w), lambda s: (_scan_order(n - 1 - s, n_ctx, n, rev), cb))

    qi = BF16 if no else F32
    return pl.pallas_call(
        body, name=name, grid=(n,),
        in_specs=[rows(0), rows(1), rows(2 + d), pl.BlockSpec((2, w), lambda s: (0, 0)),
                  pl.BlockSpec((None, h, HG_HEAD, HG_HEAD), lambda s: (n - 1 - s, 0, 0, 0)), rows(0)] + [rows(0)] * no,
        out_specs=[rows(0), rows(0), rows(0), pl.BlockSpec((2, w), lambda s: (0, 0))],
        out_shape=[jax.ShapeDtypeStruct((t, w), qi)] * 2 + [jax.ShapeDtypeStruct((t, w), BF16),
                                                            jax.ShapeDtypeStruct((2, w), F32)],
        scratch_shapes=[pltpu.VMEM((h, HG_HEAD, HG_HEAD), F32)],
        compiler_params=_params(("arbitrary",)),
    )(p_hg, p_hg, p_hg, lb2, states, do, *other)


def _to_heads(a, nh):
    return jnp.stack([a[:, i * RW_HEAD:(i + 1) * RW_HEAD] for i in range(nh)], axis=0)


def _from_heads(a):
    return jnp.concatenate([a[i] for i in range(a.shape[0])], axis=-1)


def _rw_scan_fwd(name, sh, hp, d, n_ctx):
    t = sh.shape[0]
    w = (sh.shape[1] - 4 * RW_LORA) // 3
    nh = w // RW_HEAD
    n = t // RW_STEP
    rev = d == 1
    lo = 3 * w // LANE

    def body(r_ref, k_ref, v_ref, wl_ref, al_ref, w0_ref, w2_ref, a0_ref, a2_ref, kk_ref, ka_ref,
             y_ref, st_ref, s_ref):
        j = pl.program_id(0)

        @pl.when(j == 0)
        def _():
            s_ref[...] = jnp.zeros_like(s_ref)

        s0 = s_ref[...]
        st_ref[...] = s0
        wl = wl_ref[...][:, d * RW_LORA:(d + 1) * RW_LORA]
        al = al_ref[...][:, d * RW_LORA:(d + 1) * RW_LORA]
        y, s1 = _rw_step(s0, _to_heads(r_ref[...], nh), _to_heads(k_ref[...], nh), _to_heads(v_ref[...], nh), wl, al,
                         w0_ref[...], w2_ref[...], a0_ref[...], a2_ref[...], kk_ref[...], ka_ref[...], rev)
        y_ref[...] = _from_heads(y)
        s_ref[...] = s1

    def rows(cb, width=w):
        return pl.BlockSpec((RW_STEP, width), lambda j: (_scan_order(j, n_ctx, n, rev), cb))

    def whole(a):
        nd = a.ndim
        return pl.BlockSpec(a.shape, lambda j: (0,) * nd)

    return pl.pallas_call(
        body, name=name, grid=(n,),
        in_specs=[rows(0), rows(1), rows(2), rows(lo, LANE), rows(lo + 1, LANE)] + [whole(a) for a in hp],
        out_specs=[rows(0), pl.BlockSpec((None, nh, RW_HEAD, RW_HEAD), lambda j: (j, 0, 0, 0))],
        out_shape=[jax.ShapeDtypeStruct((t, w), F32), jax.ShapeDtypeStruct((n, nh, RW_HEAD, RW_HEAD), F32)],
        scratch_shapes=[pltpu.VMEM((nh, RW_HEAD, RW_HEAD), F32)],
        compiler_params=_params(("arbitrary",)),
    )(sh, sh, sh, sh, sh, *hp)


def _rw_scan_bwd_both(name, sh, hps, states, dy, n_ctx):
    t = sh.shape[0]
    w = (sh.shape[1] - 4 * RW_LORA) // 3
    nh = w // RW_HEAD
    n = t // RW_STEP
    lo = 3 * w // LANE
    n_in, n_p = 13, 6

    def body(*refs):
        step = pl.program_id(0)
        ins = [refs[d * n_in:(d + 1) * n_in] for d in range(2)]
        outs = [refs[2 * n_in + d * (1 + n_p):2 * n_in + (d + 1) * (1 + n_p)] for d in range(2)]
        ds_refs = refs[2 * n_in + 2 * (1 + n_p):]

        @pl.when(step == 0)
        def _():
            for d in range(2):
                ds_refs[d][...] = jnp.zeros_like(ds_refs[d])
                for ref in outs[d][1:]:
                    ref[...] = jnp.zeros_like(ref)

        for d in range(2):
            r_ref, k_ref, v_ref, wl_ref, al_ref = ins[d][:5]
            hp_refs, st_ref, dy_ref = ins[d][5:11], ins[d][11], ins[d][12]
            wl = wl_ref[...][:, d * RW_LORA:(d + 1) * RW_LORA]
            al = al_ref[...][:, d * RW_LORA:(d + 1) * RW_LORA]
            _, vjp = jax.vjp(functools.partial(_rw_step, rev=d == 1),
                             st_ref[...], _to_heads(r_ref[...], nh), _to_heads(k_ref[...], nh), _to_heads(v_ref[...], nh),
                             wl, al, *[p[...] for p in hp_refs])
            g = vjp((_to_heads(dy_ref[...], nh), ds_refs[d][...]))
            ds_refs[d][...] = g[0]
            zero = jnp.zeros_like(g[4])
            lora = [zero] * 4
            lora[d], lora[2 + d] = g[4], g[5]
            outs[d][0][...] = jnp.concatenate([_from_heads(g[1]), _from_heads(g[2]), _from_heads(g[3])] + lora, axis=-1)
            for ref, val in zip(outs[d][1:], g[6:]):
                ref[...] += val

    def rows(d, cb, width=w):
        return pl.BlockSpec((RW_STEP, width), lambda s: (_scan_order(n - 1 - s, n_ctx, n, d == 1), cb))

    def whole(a):
        nd = a.ndim
        return pl.BlockSpec(a.shape, lambda s: (0,) * nd)

    in_specs, operands, out_specs, out_shape = [], [], [], []
    for d in range(2):
        in_specs += [rows(d, 0), rows(d, 1), rows(d, 2), rows(d, lo, LANE), rows(d, lo + 1, LANE)]
        in_specs += [whole(a) for a in hps[d]]
        in_specs += [pl.BlockSpec((None, nh, RW_HEAD, RW_HEAD), lambda s: (n - 1 - s, 0, 0, 0)), rows(d, 0)]
        operands += [sh] * 5 + list(hps[d]) + [states[d], dy]
        out_specs += [rows(d, 0, sh.shape[1])] + [whole(a) for a in hps[d]]
        out_shape += [jax.ShapeDtypeStruct(sh.shape, F32)] + [jax.ShapeDtypeStruct(a.shape, F32) for a in hps[d]]
    res = pl.pallas_call(
        body, name=name, grid=(n,), in_specs=in_specs, out_specs=out_specs, out_shape=out_shape,
        scratch_shapes=[pltpu.VMEM((nh, RW_HEAD, RW_HEAD), F32)] * 2, compiler_params=_params(("arbitrary",)),
    )(*operands)
    return [res[0], res[1 + n_p]], [res[1:1 + n_p], res[2 + n_p:]]


def _shift_masks(t, n_ctx_rows):
    row = lax.broadcasted_iota(jnp.int32, (t, 1), 0)
    isx = row >= n_ctx_rows
    pos = jnp.where(isx, row - n_ctx_rows, row)
    col = jnp.where(isx, jnp.bitwise_and(pos, GRID_W - 1), pos)
    ncol = jnp.where(isx, GRID_W, n_ctx_rows)
    n_x = t - n_ctx_rows
    ml = col != 0
    mr = col != ncol - 1
    mu = isx & (pos >= GRID_W)
    md = isx & (pos < n_x - GRID_W)
    return ml, mr, mu, md, isx


def _shift_fwd(name, p, col0, mu, n_ctx_rows):
    t, c = p.shape[0], mu.shape[1]
    cw = LANE

    def body(p_ref, mu_ref, o_ref):
        x = p_ref[...]
        m = mu_ref[...]
        ml, mr, mup, mdn, isx = _shift_masks(t, n_ctx_rows)
        left = jnp.where(ml, pltpu.roll(x, 1, 0), 0.0)
        right = jnp.where(mr, pltpu.roll(x, t - 1, 0), 0.0)
        up = jnp.where(mup, pltpu.roll(x, GRID_W, 0), 0.0)
        down = jnp.where(mdn, pltpu.roll(x, t - GRID_W, 0), 0.0)
        out = x + m[0:1] * (left - x) + m[1:2] * (right - x)
        vert = m[2:3] * (up - x) + m[3:4] * (down - x)
        o_ref[...] = out + jnp.where(isx, vert, 0.0)

    return pl.pallas_call(
        body, name=name, grid=(c // cw,),
        in_specs=[pl.BlockSpec((t, cw), lambda j: (0, col0 + j)), pl.BlockSpec((4, cw), lambda j: (0, j))],
        out_specs=pl.BlockSpec((t, cw), lambda j: (0, j)),
        out_shape=jax.ShapeDtypeStruct((t, c), F32),
        compiler_params=_params(("parallel",)),
    )(p, mu)


def _shift_bwd(name, p, col0, mu, dparts, n_ctx_rows):
    t, c = p.shape[0], mu.shape[1]
    cw = LANE
    npart = len(dparts)

    def body(*refs):
        p_ref, mu_ref = refs[0], refs[1]
        dp_ref, dmu_ref = refs[2 + npart], refs[3 + npart]
        x = p_ref[...]
        m = mu_ref[...]
        g = refs[2][...]
        for r in refs[3:2 + npart]:
            g = g + r[...]
        ml, mr, mup, mdn, isx = _shift_masks(t, n_ctx_rows)
        left = jnp.where(ml, pltpu.roll(x, 1, 0), 0.0)
        right = jnp.where(mr, pltpu.roll(x, t - 1, 0), 0.0)
        up = jnp.where(mup, pltpu.roll(x, GRID_W, 0), 0.0)
        down = jnp.where(mdn, pltpu.roll(x, t - GRID_W, 0), 0.0)
        gx = jnp.where(isx, g, 0.0)
        dmu_ref[...] = jnp.concatenate([
            jnp.sum(g * (left - x), axis=0, keepdims=True), jnp.sum(g * (right - x), axis=0, keepdims=True),
            jnp.sum(gx * (up - x), axis=0, keepdims=True), jnp.sum(gx * (down - x), axis=0, keepdims=True)], axis=0)
        coef = 1.0 - m[0:1] - m[1:2] - jnp.where(isx, m[2:3] + m[3:4], 0.0)
        dp = coef * g
        dp = dp + m[0:1] * pltpu.roll(jnp.where(ml, g, 0.0), t - 1, 0)
        dp = dp + m[1:2] * pltpu.roll(jnp.where(mr, g, 0.0), 1, 0)
        dp = dp + m[2:3] * pltpu.roll(jnp.where(mup, g, 0.0), t - GRID_W, 0)
        dp = dp + m[3:4] * pltpu.roll(jnp.where(mdn, g, 0.0), GRID_W, 0)
        dp_ref[...] = dp.astype(dp_ref.dtype)

    col = pl.BlockSpec((t, cw), lambda j: (0, j))
    par = pl.BlockSpec((4, cw), lambda j: (0, j))
    return pl.pallas_call(
        body, name=name, grid=(c // cw,),
        in_specs=[pl.BlockSpec((t, cw), lambda j: (0, col0 + j)), par] + [col] * npart,
        out_specs=[col, par],
        out_shape=[jax.ShapeDtypeStruct((t, c), BF16), jax.ShapeDtypeStruct((4, c), F32)],
        compiler_params=_params(("parallel",)),
    )(p, mu, *dparts)


def _local_step(x, c, ctx, c_ctx, ada_st, ada_b, norm_g, w_in_st, hg_lb, hg_norm_g, rw_mu, rw_w0, rw_w2, rw_a0, rw_a2,
                rw_kk, rw_ka, rw_rk, rw_gn_g, rw_gn_b, w_hg_st, w_rw_st, w_out, final_g, tgt, my_core):
    seq, dm = x.shape
    n_ctx_rows = ctx.shape[0]
    t = seq + n_ctx_rows
    hw = hg_norm_g.shape[-1]
    rw = rw_kk.shape[-1]
    nh_rw = rw // RW_HEAD
    n_ctx = n_ctx_rows // STEP
    tm = _tile(n_ctx_rows, (256, 128, 64))
    nt = t // tm
    nct = n_ctx_rows // tm
    n_sh_cols = 3 * rw + 4 * RW_LORA

    cond = jnp.concatenate([c.reshape(1, dm), c_ctx.reshape(1, dm), jnp.zeros((6, dm), F32)], axis=0)
    final_g2 = final_g.reshape(1, dm)

    def unstack(a_st):
        return jnp.swapaxes(a_st, 0, 1).reshape(a_st.shape[1], -1)

    def restack(a, ns=N_SHARD):
        return jnp.swapaxes(a.reshape(a.shape[0], ns, -1), 0, 1)

    (sc,) = _row_call("cond_silu", lambda i, r, f: ([jax.nn.silu(r[0])], []), 1, 8, [(cond, 0, dm, 0)], [],
                      [(8, dm, F32, 0)], [])
    mod_mm = _mm_n_st("mod_mm", sc, ada_st, joined=True)

    def add(name, terms, shape):
        flat2 = [a.reshape(-1, a.shape[-1]) for a in terms]
        return _rowwise(name, lambda *v: _slot_sum(list(v)), flat2, F32).reshape(shape)

    mod = add("mod_bias", [mod_mm, jnp.broadcast_to(ada_b, (8, 3 * dm))], (8, 3 * dm))
    mod3 = mod.reshape(8, 3, dm)

    def pick(i, m3):
        r = jnp.where(i < nct, m3[1], m3[0])
        return r[0:1], r[1:2]

    tokens = [(ctx, 0, dm, 0), (x, 0, dm, nct)]

    def h_fn(i, r, f):
        shift, scale = pick(i, f[1])
        return [_fn_h(jnp.where(i < nct, r[0], r[1]), f[0], scale, shift)], []

    (h,) = _row_call("h_fwd", h_fn, nt, tm, tokens, [norm_g, mod3], [(t, dm, BF16, 0)], [])
    proj = unstack(_mm_n_st("proj_mm", h, w_in_st))
    p_hg = proj
    rs_tile0 = 5 * hw // LANE
    p_zr = proj[:, 5 * hw + n_sh_cols:5 * hw + n_sh_cols + rw]
    p_gt = proj[:, 5 * hw + n_sh_cols + rw:]

    o_hg, st_hg = [], []
    for d in range(2):
        o, st = _hg_scan_fwd(f"hg_scan_fwd{d}", p_hg, hg_lb[d], d, n_ctx)
        o_hg.append(o)
        st_hg.append(st)

    def hgpost_fn(i, r, f):
        return [_fn_hgpost(r[0], r[1], r[2], f[0])], []

    hg_in = [(o_hg[0], 0, hw, 0), (o_hg[1], 0, hw, 0), (p_hg, 4, hw, 0)]
    (y_hg,) = _row_call("hg_post", hgpost_fn, nt, tm, hg_in, [hg_norm_g], [(t, hw, BF16, 0)], [])

    sh = _shift_fwd("rw_shift", proj, rs_tile0, rw_mu, n_ctx_rows)
    hps = []
    for d in range(2):
        hps.append([rw_w0[d].reshape(nh_rw, 1, RW_HEAD), jnp.swapaxes(rw_w2[d].reshape(RW_LORA, nh_rw, RW_HEAD), 0, 1),
                    rw_a0[d].reshape(nh_rw, 1, RW_HEAD), jnp.swapaxes(rw_a2[d].reshape(RW_LORA, nh_rw, RW_HEAD), 0, 1),
                    rw_kk.reshape(nh_rw, 1, RW_HEAD), rw_ka.reshape(nh_rw, 1, RW_HEAD)])
    y_rw_d, st_rw = [], []
    for d in range(2):
        y, st = _rw_scan_fwd(f"rw_scan_fwd{d}", sh, hps[d], d, n_ctx_rows // RW_STEP)
        y_rw_d.append(y)
        st_rw.append(st)

    rw_full = [rw_a0, rw_a2, rw_ka, rw_rk, rw_gn_g, rw_gn_b]
    lo = 3 * rw // LANE
    rw_in = [(y_rw_d[0], 0, rw, 0), (y_rw_d[1], 0, rw, 0), (sh, 0, rw, 0), (sh, 1, rw, 0), (sh, 2, rw, 0),
             (sh, lo + 1, LANE, 0), (p_zr, 0, rw, 0)]

    def rwpost_fn(i, r, f):
        return [_fn_rwpost(*r, *f)], []

    (y_rw,) = _row_call("rw_post", rwpost_fn, nt, tm, rw_in, rw_full, [(t, rw, BF16, 0)], [])

    a_hg = _mm_n_st("hg_out_mm", y_hg, w_hg_st, joined=True)
    a_rw = _mm_n_st("rw_out_mm", y_rw, w_rw_st, joined=True)
    mg_in = [(a_hg, 0, dm, 0), (a_rw, 0, dm, 0), (p_gt, 0, dm, 0), (p_gt, 1, dm, 0)]
    (merged,) = _row_call("merge", lambda i, r, f: ([_fn_merge(*r)], []), nt, tm, mg_in, [], [(t, dm, BF16, 0)], [])
    o_out = _mm_nn("out_mm", merged, w_out)

    def final_fn(i, r, f):
        gate = f[0][0][2:3]
        loss, vjp = jax.vjp(_fn_final, r[0], r[1], gate, f[1], r[2])
        dx, do, dgate, dfg, _ = vjp(jnp.ones((), F32))
        live = i >= nct
        zero = lambda a: jnp.where(live, a, 0.0)
        dmod = jnp.concatenate([jnp.concatenate([jnp.zeros((1, 2 * dm), F32), zero(dgate)], axis=1),
                                jnp.zeros((7, 3 * dm), F32)], axis=0)
        return [zero(dx), zero(do)], [jnp.broadcast_to(zero(loss), (8, LANE)), dmod, zero(dfg)]

    fin_in = [(x, 0, dm, nct), (o_out, 0, dm, 0), (tgt, 0, dm, nct)]
    dx_res, d_o, loss_acc, dmod_gate, d_final_g = _row_call(
        "final", final_fn, nt, tm, fin_in, [mod3, final_g2], [(t, dm, F32, 0), (t, dm, BF16, 0)],
        [((8, LANE), F32), ((8, 3 * dm), F32), ((1, dm), F32)])

    g_w_out = _mm_tn("d_w_out", merged, d_o)
    d_merged = _mm_nt("d_merged", d_o, w_out)

    def merge_bwd(i, r, f):
        _, vjp = jax.vjp(_fn_merge, r[0], r[1], r[2], r[3])
        da, db, dgh, dgr = vjp(r[4])
        return [da, db, jnp.concatenate([dgh, dgr], axis=1)], []

    da_hg, da_rw, dp_gt = _row_call("merge_bwd", merge_bwd, nt, tm, mg_in + [(d_merged, 0, dm, 0)], [],
                                    [(t, dm, BF16, 0), (t, dm, BF16, 0), (t, 2 * dm, BF16, 0)], [])
    g_w_hg_st = _mm_t_st("d_w_hg", y_hg, da_hg)
    g_w_rw_st = _mm_t_st("d_w_rw", y_rw, da_rw)
    dy_hg = _mm_st_t("d_y_hg", da_hg, w_hg_st)
    dy_rw = _mm_st_t("d_y_rw", da_rw, w_rw_st)

    def hgpost_bwd(i, r, f):
        _, vjp = jax.vjp(_fn_hgpost, r[0], r[1], r[2], f[0])
        dof, _, dz, dg = vjp(r[3])
        return [dof, dz], [dg]

    do_hg, dz_hg, g_hg_norm = _row_call("hg_post_bwd", hgpost_bwd, nt, tm, hg_in + [(dy_hg, 0, hw, 0)], [hg_norm_g],
                                        [(t, hw, F32, 0), (t, hw, BF16, 0)], [((1, hw), F32)])
    dq0, di0, df0, dlb0 = _hg_scan_bwd("hg_scan_bwd0", p_hg, hg_lb[0], st_hg[0], do_hg, 0, n_ctx)
    dq, di, df1, dlb1 = _hg_scan_bwd("hg_scan_bwd1", p_hg, hg_lb[1], st_hg[1], do_hg, 1, n_ctx, other=(dq0, di0))
    g_hg_lb = jnp.stack([dlb0, dlb1], axis=0)

    def rwpost_bwd(i, r, f):
        _, vjp = jax.vjp(_fn_rwpost, *r[:7], *f)
        g = vjp(r[7])
        zl = jnp.zeros((g[5].shape[0], 2 * RW_LORA), F32)
        return [g[0], jnp.concatenate([g[2], g[3], g[4], zl, g[5]], axis=1), g[6]], list(g[7:])

    dy_sum, dsh_p, dz_rw, g_a0_p, g_a2_p, g_ka_p, g_rk, g_gn_g, g_gn_b = _row_call(
        "rw_post_bwd", rwpost_bwd, nt, tm, rw_in + [(dy_rw, 0, rw, 0)], rw_full,
        [(t, rw, F32, 0), (t, n_sh_cols, F32, 0), (t, rw, BF16, 0)], [(a.shape, F32) for a in rw_full])
    dsh_dirs, hp_grads = _rw_scan_bwd_both("rw_scan_bwd", sh, hps, st_rw, dy_sum, n_ctx_rows // RW_STEP)
    dp_rs, g_mu = _shift_bwd("rw_shift_bwd", proj, rs_tile0, rw_mu, [dsh_p] + dsh_dirs, n_ctx_rows)

    def flat(a):
        if a.shape[1] == 1:
            return a.reshape(rw)
        return jnp.swapaxes(a, 0, 1).reshape(RW_LORA, rw)

    g_w0 = jnp.stack([flat(hp_grads[d][0]) for d in range(2)], axis=0)
    g_w2 = jnp.stack([flat(hp_grads[d][1]) for d in range(2)], axis=0)
    g_a0 = add("g_a0", [jnp.stack([flat(hp_grads[d][2]) for d in range(2)], axis=0), g_a0_p], (2, rw))
    g_a2 = add("g_a2", [jnp.stack([flat(hp_grads[d][3]) for d in range(2)], axis=0), g_a2_p], (2, RW_LORA, rw))
    g_kk = add("g_kk", [flat(hp_grads[0][4]).reshape(1, rw), flat(hp_grads[1][4]).reshape(1, rw)], (1, rw))
    g_ka = add("g_ka", [flat(hp_grads[0][5]).reshape(1, rw), flat(hp_grads[1][5]).reshape(1, rw), g_ka_p], (1, rw))

    dproj = jnp.concatenate([dq, di, df0, df1, dz_hg, dp_rs, dz_rw, dp_gt], axis=1)
    dproj_st = restack(dproj)
    early = {"w_hg_out": g_w_hg_st, "w_rw_out": g_w_rw_st, "w_out": g_w_out.reshape(N_SHARD, dm // N_SHARD, dm)}
    early_chip = [_pair_exchange(f"grads_pair_sum_{n}", a, my_core, True, BF16) for n, a in early.items()]
    g_w_in_st, early_landed = _mm_t_st("d_w_in", h, dproj_st, scatter=tuple(early_chip))
    w_in_chip = _pair_exchange("grads_pair_sum_w_in", g_w_in_st, my_core, True, BF16)
    dh, (w_in_landed,) = _mm_st_t("d_h", dproj_st, w_in_st, scatter=(w_in_chip,))

    def h_bwd(i, r, f):
        shift, scale = pick(i, f[1])
        is_ctx = i < nct
        _, vjp = jax.vjp(_fn_h, jnp.where(is_ctx, r[0], r[1]), f[0], scale, shift)
        ds, dg, dscale, dshift = vjp(r[2])
        row = jnp.concatenate([dshift, dscale, jnp.zeros((1, dm), F32)], axis=1)
        z = jnp.zeros_like(row)
        dmod = jnp.concatenate([jnp.where(is_ctx, z, row), jnp.where(is_ctx, row, z), jnp.zeros((6, 3 * dm), F32)], axis=0)
        return [ds + r[3]], [dg, dmod]

    grad_x, g_norm_g, dmod_h = _row_call(
        "h_bwd", h_bwd, nt, tm, tokens + [(dh, 0, dm, 0), (dx_res, 0, dm, 0)], [norm_g, mod3],
        [(seq, dm, F32, nct)], [((1, dm), F32), ((8, 3 * dm), F32)])
    dmod = add("d_mod", [dmod_h, dmod_gate], (8, 3 * dm))
    g_ada_b = add("g_ada_b", [dmod[0:1], dmod[1:2]], (1, 3 * dm))
    g_ada_st = _mm_t_st("d_ada_w", sc, dmod)
    d_sc = _mm_st_t("d_cond", dmod, ada_st)

    def cond_bwd(i, r, f):
        _, vjp = jax.vjp(jax.nn.silu, r[0])
        return [vjp(r[1])[0]], []

    (d_cond,) = _row_call("cond_bwd", cond_bwd, 1, 8, [(cond, 0, dm, 0), (d_sc, 0, dm, 0)], [], [(8, dm, F32, 0)], [])

    grads = dict(
        c_ctx=d_cond[1], ada_w=g_ada_st, ada_b=g_ada_b, norm_g=g_norm_g, w_in=(w_in_chip, w_in_landed), hg_lb=g_hg_lb,
        hg_norm_g=g_hg_norm, rw_mu=g_mu, rw_w0=g_w0, rw_w2=g_w2, rw_a0=g_a0, rw_a2=g_a2, rw_kk=g_kk, rw_ka=g_ka,
        rw_rk=g_rk, rw_gn_g=g_gn_g, rw_gn_b=g_gn_b, final_g=d_final_g.reshape(dm))
    grads.update(zip(early, zip(early_chip, early_landed)))
    return loss_acc[0:1, 0:1], grad_x, grads


def _my_place():
    return lax.axis_index("x"), lax.axis_index("y"), lax.axis_index("c")


MIN_CHUNK_BYTES = 1 << 18
ROW_ALIGN = 16


def _n_chunks(rows, row_bytes):
    for n in (8, 4, 2):
        if rows % (n * ROW_ALIGN) == 0 and rows // n * row_bytes >= MIN_CHUNK_BYTES:
            return n
    return 1


def _row_bytes(a, lead=1):
    n = a.dtype.itemsize
    for d in a.shape[lead:]:
        n *= d
    return n


def _rows(ref, start, size):
    return ref.at[pl.ds(start, size)]


def _chunked(make, start, size, n):
    cs = size // n
    return [make(start + j * cs, cs) for j in range(n)]


_PEER_CHIPS = 3


def _weights_gather(name, big, small):
    nb, na = len(big), len(big) + len(small)
    arrays = list(big) + list(small)
    n_ici = 6

    def body(*refs):
        outs = refs[na:2 * na]
        send_sems, recv_sems, fsend_sems, frecv_sems = refs[2 * na:]
        x, y, c = _my_place()
        me, sx, sy, sd = 2 * x + y, 2 * (1 - x) + y, 2 * x + (1 - y), 2 * (1 - x) + (1 - y)
        kx, ky, kd = (1 - x, y, c), (x, 1 - y, c), (1 - x, 1 - y, c)

        def ici(a, j, src_slot, dst_slot, to, r0, nr):
            return pltpu.make_async_remote_copy(
                src_ref=_rows(outs[a].at[src_slot], r0, nr), dst_ref=_rows(outs[a].at[dst_slot], r0, nr),
                send_sem=send_sems.at[a, j], recv_sem=recv_sems.at[a, j], device_id=to,
                device_id_type=pl.DeviceIdType.MESH)

        def to_sibling(a, k, slot, r0, nr):
            rows = _rows(outs[a].at[slot], r0, nr)
            return pltpu.make_async_remote_copy(
                src_ref=rows, dst_ref=rows, send_sem=fsend_sems.at[a, k], recv_sem=frecv_sems.at[a, k],
                device_id=(x, y, 1 - c), device_id_type=pl.DeviceIdType.MESH)

        def start(copies):
            for cp in copies:
                cp.start()

        geo = []
        for a in range(nb):
            half = arrays[a].shape[1] // 2
            geo.append((pl.multiple_of(c * half, ROW_ALIGN), pl.multiple_of((1 - c) * half, ROW_ALIGN), half // 2,
                        _n_chunks(half // 2, _row_bytes(arrays[a], 2))))
        plan = [(me, sx, kx, 0), (me, sx, kx, 1), (me, sy, ky, 0), (me, sy, ky, 1), (sx, sd, ky, 0), (sy, sd, kx, 1)]

        def piece(a, j):
            return geo[a][0] + plan[j][3] * geo[a][2]

        for a in range(nb):
            for j in range(4):
                start(_chunked(lambda r0, cs: ici(a, j, me, me, plan[j][2], r0, cs), piece(a, j), geo[a][2], geo[a][3]))
        for a in range(nb, na):
            rows = arrays[a].shape[1]
            for j, to in ((0, kx), (2, ky), (1, kd)):
                ici(a, j, me, me, to, 0, rows).start()
        for a in range(nb):
            for j, first in ((4, 0), (5, 3)):
                src_slot, _, to, _ = plan[j]
                ici(a, first, me, plan[first][1], plan[first][2], piece(a, first), geo[a][2]).wait_recv()
                start(_chunked(lambda r0, cs: ici(a, j, src_slot, src_slot, to, r0, cs), piece(a, j), geo[a][2], geo[a][3]))
        for a in range(nb):
            for j in (1, 2):
                ici(a, j, me, plan[j][1], plan[j][2], piece(a, j), geo[a][2]).wait_recv()
            for k, slot in ((0, sx), (1, sy)):
                start(_chunked(lambda r0, cs: to_sibling(a, k, slot, r0, cs), geo[a][0], 2 * geo[a][2], geo[a][3]))
        for a in range(nb):
            for j in (4, 5):
                ici(a, j, me, sd, plan[j][2], piece(a, j), geo[a][2]).wait_recv()
            start(_chunked(lambda r0, cs: to_sibling(a, 2, sd, r0, cs), geo[a][0], 2 * geo[a][2], geo[a][3]))
        for a in range(nb, na):
            rows = arrays[a].shape[1]
            for j, slot, to in ((0, sx, kx), (2, sy, ky), (1, sd, kd)):
                ici(a, j, me, slot, to, 0, rows).wait_recv()
        for a in range(nb):
            for k, slot in ((0, sx), (1, sy), (2, sd)):
                to_sibling(a, k, slot, geo[a][1], 2 * geo[a][2]).wait_recv()
        for a in range(nb):
            for j in range(n_ici):
                ici(a, j, me, me, plan[j][2], piece(a, j), geo[a][2]).wait_send()
            for k, slot in ((0, sx), (1, sy), (2, sd)):
                to_sibling(a, k, slot, geo[a][0], 2 * geo[a][2]).wait_send()
        for a in range(nb, na):
            rows = arrays[a].shape[1]
            for j, to in ((0, kx), (2, ky), (1, kd)):
                ici(a, j, me, me, to, 0, rows).wait_send()

    hbm = pl.BlockSpec(memory_space=pl.ANY)
    ici_sems = pltpu.SemaphoreType.DMA((na, n_ici))
    pair_sems = pltpu.SemaphoreType.DMA((na, _PEER_CHIPS))
    return pl.pallas_call(
        body, name=name, in_specs=[hbm] * na, out_specs=[hbm] * na,
        out_shape=[jax.ShapeDtypeStruct(a.shape, a.dtype) for a in arrays],
        input_output_aliases={a: a for a in range(na)}, scratch_shapes=[ici_sems, ici_sems, pair_sems, pair_sems],
    )(*arrays)


def _scatter_copy(arrays, ins, outs, send_sems, recv_sems, a, k, slot, r0, nr):
    x, y, c = _my_place()
    px, py = [(1 - x, y), (x, 1 - y), (1 - x, 1 - y)][k]
    return pltpu.make_async_remote_copy(
        src_ref=_rows(ins[a].at[2 * px + py], r0, nr), dst_ref=_rows(outs[a].at[slot], r0, nr),
        send_sem=send_sems.at[a, k], recv_sem=recv_sems.at[a, k], device_id=(px, py, c),
        device_id_type=pl.DeviceIdType.MESH)


def _scatter_start(arrays, ins, outs, send_sems, recv_sems):
    x, y, _ = _my_place()
    for a in range(len(arrays)):
        rows = arrays[a].shape[1]
        for k in range(_PEER_CHIPS):
            for cp in _chunked(lambda r0, cs: _scatter_copy(arrays, ins, outs, send_sems, recv_sems, a, k, 2 * x + y, r0, cs),
                               0, rows, _n_chunks(rows, _row_bytes(arrays[a], 2))):
                cp.start()


def _scatter_wait(arrays, ins, outs, send_sems, recv_sems):
    x, y, _ = _my_place()
    peer_slot = [2 * (1 - x) + y, 2 * x + (1 - y), 2 * (1 - x) + (1 - y)]
    for k in range(_PEER_CHIPS):
        for a in range(len(arrays)):
            _scatter_copy(arrays, ins, outs, send_sems, recv_sems, a, k, peer_slot[k], 0, arrays[a].shape[1]).wait_recv()
    for a in range(len(arrays)):
        for k in range(_PEER_CHIPS):
            _scatter_copy(arrays, ins, outs, send_sems, recv_sems, a, k, 2 * x + y, 0, arrays[a].shape[1]).wait_send()


PAIR_TILE_BYTES = 4 << 20


def _pair_exchange(name, a, place, reduce, out_dtype):
    rows, cols = a.shape[-2], a.shape[-1]
    half = rows // 2 if reduce else rows
    tr = _row_tile_for(half, cols, budget=PAIR_TILE_BYTES)
    nh = half // tr
    n_steps = (N_SHARD if reduce else 1) * nh

    def body(pc_ref, *refs):
        if reduce:
            keep_ref, send_ref, o_ref, land, send_sems, recv_sems, credit, wire = refs
            wire[...] = send_ref[...].astype(BF16)
            src = wire
        else:
            send_ref, o_ref, land, send_sems, recv_sems, credit = refs
            src = send_ref
        x, y, c = _my_place()
        other = (x, y, 1 - c)
        t = pl.program_id(0) * nh + pl.program_id(1) if reduce else pl.program_id(0)
        slot = t % 2

        @pl.when(t >= 2)
        def _():
            pl.semaphore_wait(credit, 1)

        copy = pltpu.make_async_remote_copy(
            src_ref=src, dst_ref=land.at[slot], send_sem=send_sems.at[slot], recv_sem=recv_sems.at[slot],
            device_id=other, device_id_type=pl.DeviceIdType.MESH)
        copy.start()
        copy.wait_recv()
        got = land[slot]
        o_ref[...] = ((keep_ref[...] + got.astype(F32)) if reduce else got).astype(out_dtype)
        copy.wait_send()

        @pl.when(t < n_steps - 2)
        def _():
            pl.semaphore_signal(credit, inc=1, device_id=other, device_id_type=pl.DeviceIdType.MESH)

    if reduce:
        grid = (N_SHARD, nh)
        in_specs = [pl.BlockSpec((None, tr, cols), lambda j, i, pc: (j, pc[0] * nh + i, 0)),
                    pl.BlockSpec((None, tr, cols), lambda j, i, pc: (j, (1 - pc[0]) * nh + i, 0))]
        out_spec = pl.BlockSpec((None, tr, cols), lambda j, i, pc: (j, i, 0))
        out_shape = jax.ShapeDtypeStruct((N_SHARD, half, cols), out_dtype)
        operands = (a, a)
        sem = ("arbitrary", "arbitrary")
    else:
        grid = (nh,)
        in_specs = [pl.BlockSpec((tr, cols), lambda i, pc: (i, 0))]
        out_spec = pl.BlockSpec((tr, cols), lambda i, pc: (i, 0))
        out_shape = jax.ShapeDtypeStruct((half, cols), out_dtype)
        operands = (a,)
        sem = ("arbitrary",)
    return pl.pallas_call(
        body, name=name,
        grid_spec=pltpu.PrefetchScalarGridSpec(
            num_scalar_prefetch=1, grid=grid, in_specs=in_specs, out_specs=out_spec,
            scratch_shapes=[pltpu.VMEM((2, tr, cols), BF16 if reduce else a.dtype), pltpu.SemaphoreType.DMA((2,)),
                            pltpu.SemaphoreType.DMA((2,)), pltpu.SemaphoreType.REGULAR] +
                           ([pltpu.VMEM((tr, cols), BF16)] if reduce else [])),
        out_shape=out_shape, compiler_params=_params(sem),
    )(place, *operands)


def _cast_into_slot(name, a, chip):
    rows, cols = a.shape
    tm = _row_tile_for(rows, cols)

    def body(pc_ref, a_ref, o_ref):
        o_ref[...] = a_ref[...].astype(BF16)

    return pl.pallas_call(
        body, name=name,
        grid_spec=pltpu.PrefetchScalarGridSpec(
            num_scalar_prefetch=1, grid=(rows // tm,), in_specs=[pl.BlockSpec((tm, cols), lambda i, pc: (i, 0))],
            out_specs=pl.BlockSpec((None, tm, cols), lambda i, pc: (pc[0], i, 0))),
        out_shape=jax.ShapeDtypeStruct((N_SHARD, rows, cols), BF16), compiler_params=_params(("parallel",)),
    )(chip, a)


def _sum_landed(name, landed, sent, chip):
    ns, rows, cols = landed.shape
    tm = _row_tile_for(rows, cols)

    def body(pc_ref, *refs):
        own_ref, o_ref = refs[ns], refs[ns + 1]
        me = pc_ref[0]
        terms = [jnp.where(me == j, own_ref[...], refs[j][...]).astype(F32) for j in range(ns)]
        o_ref[...] = _slot_sum(terms)

    def landed_spec(j):
        return pl.BlockSpec((None, tm, cols), lambda i, pc: (jnp.where(pc[0] == j, (j + 1) % ns, j), i, 0))

    return pl.pallas_call(
        body, name=name,
        grid_spec=pltpu.PrefetchScalarGridSpec(
            num_scalar_prefetch=1, grid=(rows // tm,),
            in_specs=[landed_spec(j) for j in range(ns)] + [pl.BlockSpec((None, tm, cols), lambda i, pc: (pc[0], i, 0))],
            out_specs=pl.BlockSpec((tm, cols), lambda i, pc: (i, 0))),
        out_shape=jax.ShapeDtypeStruct((rows, cols), F32), compiler_params=_params(("parallel",)),
    )(chip, *([landed] * ns), sent)


def _gather_all(name, a):
    def body(in_ref, out_ref, send_sems, recv_sems, local_sem):
        x, y, c = _my_place()
        me = 4 * x + 2 * y + c

        def peer(k):
            return (x ^ (k >> 2), y ^ ((k >> 1) & 1), c ^ (k & 1))

        def remote(k, land):
            return pltpu.make_async_remote_copy(
                src_ref=in_ref, dst_ref=out_ref.at[land], send_sem=send_sems.at[k - 1], recv_sem=recv_sems.at[k - 1],
                device_id=peer(k), device_id_type=pl.DeviceIdType.MESH)

        local = pltpu.make_async_copy(in_ref, out_ref.at[me], local_sem)
        local.start()
        for k in range(1, N_DEV):
            remote(k, me).start()
        for k in range(1, N_DEV):
            px, py, pc = peer(k)
            remote(k, 4 * px + 2 * py + pc).wait_recv()
        for k in range(1, N_DEV):
            remote(k, me).wait_send()
        local.wait()

    hbm = pl.BlockSpec(memory_space=pl.ANY)
    return pl.pallas_call(
        body, name=name, in_specs=[hbm], out_specs=hbm,
        out_shape=jax.ShapeDtypeStruct((N_DEV,) + a.shape, a.dtype),
        scratch_shapes=[pltpu.SemaphoreType.DMA((N_DEV - 1,)), pltpu.SemaphoreType.DMA((N_DEV - 1,)), pltpu.SemaphoreType.DMA],
    )(a)


def _row_tile_for(rows, cols, budget=1 << 20):
    if rows * cols * 4 <= budget:
        return rows
    for tm in (1024, 512, 256, 128, 64, 32, 16, 8):
        if rows % tm == 0 and tm * cols * 4 <= budget:
            return tm
    return rows


def _slot_sum(vals):
    g = vals[0]
    for v in vals[1:]:
        g = g + v
    return g


def _rowwise(name, fn, arrays, out_dtype):
    rows, cols = arrays[0].shape
    tm = _row_tile_for(rows, cols)

    def body(*refs):
        refs[-1][...] = fn(*[r[...] for r in refs[:-1]]).astype(out_dtype)

    blk = pl.BlockSpec((tm, cols), lambda i: (i, 0))
    return pl.pallas_call(
        body, name=name, grid=(rows // tm,), in_specs=[blk] * len(arrays), out_specs=blk,
        out_shape=jax.ShapeDtypeStruct((rows, cols), out_dtype), compiler_params=_params(("parallel",)),
    )(*arrays)


def _sum_slots(name, st):
    ns, rows, cols = st.shape
    tm = _row_tile_for(rows, cols)

    def body(s_ref, o_ref):
        o_ref[...] = _slot_sum([s_ref[j].astype(F32) for j in range(ns)])

    return pl.pallas_call(
        body, name=name, grid=(rows // tm,),
        in_specs=[pl.BlockSpec((ns, tm, cols), lambda i: (0, i, 0))],
        out_specs=pl.BlockSpec((tm, cols), lambda i: (i, 0)),
        out_shape=jax.ShapeDtypeStruct((rows, cols), F32),
        compiler_params=_params(("parallel",)),
    )(st)


ADAM_TILE_BYTES = 1 << 20


def _adam_update(g, p_ref, m_ref, v_ref, go_ref, d_ref, mo_ref, vo_ref):
    mn = ADAM_B1 * m_ref[...] + (1.0 - ADAM_B1) * g
    vn = ADAM_B2 * v_ref[...] + (1.0 - ADAM_B2) * jnp.square(g)
    m_hat = mn / (1.0 - ADAM_B1 ** ADAM_STEP)
    v_hat = vn / (1.0 - ADAM_B2 ** ADAM_STEP)
    go_ref[...] = g
    d_ref[...] = -ADAM_LR * (m_hat / (jnp.sqrt(v_hat) + ADAM_EPS) + ADAM_WD * p_ref[...])
    mo_ref[...] = mn
    vo_ref[...] = vn


def _adamw(name, p, m, v, gst):
    rows, cols = p.shape
    ns = gst.shape[0]
    tm = _row_tile_for(rows, cols, budget=ADAM_TILE_BYTES)

    def body(p_ref, m_ref, v_ref, g_ref, *outs):
        _adam_update(_slot_sum([g_ref[j] for j in range(ns)]), p_ref, m_ref, v_ref, *outs)

    blk = pl.BlockSpec((tm, cols), lambda i: (i, 0))
    return pl.pallas_call(
        body, name=name, grid=(rows // tm,),
        in_specs=[blk, blk, blk, pl.BlockSpec((ns, tm, cols), lambda i: (0, i, 0))],
        out_specs=[blk] * 4, out_shape=[jax.ShapeDtypeStruct((rows, cols), F32)] * 4,
        compiler_params=_params(("parallel",)),
    )(p, m, v, gst)


def _adamw_halves(name, p, m, v, mine, theirs, place, scatter=()):
    rows, cols = p.shape
    half = rows // 2
    tm = _row_tile_for(half, cols, budget=ADAM_TILE_BYTES)
    nh = half // tm
    ns = len(scatter)

    def body(pc_ref, p_ref, m_ref, v_ref, mine_ref, theirs_ref, *refs):
        if ns:
            sc_refs = (refs[:ns], refs[ns + 4:2 * ns + 4]) + tuple(refs[2 * ns + 4:])
            at = pl.program_id(0) * nh + pl.program_id(1)
            pl.when(at == 0)(lambda: _scatter_start(scatter, *sc_refs))
        g = jnp.where(pl.program_id(0) == pc_ref[0], mine_ref[...], theirs_ref[...])
        _adam_update(g, p_ref, m_ref, v_ref, *refs[ns:ns + 4])
        if ns:
            pl.when(at == 2 * nh - 1)(lambda: _scatter_wait(scatter, *sc_refs))

    blk = pl.BlockSpec((tm, cols), lambda h, i, pc: (h * nh + i, 0))
    hblk = pl.BlockSpec((tm, cols), lambda h, i, pc: (i, 0))
    hbm = pl.BlockSpec(memory_space=pl.ANY)
    res = pl.pallas_call(
        body, name=name,
        grid_spec=pltpu.PrefetchScalarGridSpec(
            num_scalar_prefetch=1, grid=(2, nh), in_specs=[blk, blk, blk, hblk, hblk] + [hbm] * ns,
            out_specs=[blk] * 4 + [hbm] * ns,
            scratch_shapes=[pltpu.SemaphoreType.DMA((ns, _PEER_CHIPS))] * 2 if ns else []),
        out_shape=[jax.ShapeDtypeStruct((rows, cols), F32)] * 4 + [jax.ShapeDtypeStruct(s.shape, s.dtype) for s in scatter],
        compiler_params=_params(("arbitrary", "arbitrary") if ns else ("parallel", "parallel")),
    )(place, p, m, v, mine, theirs, *scatter)
    return (list(res[:4]), list(res[4:])) if ns else res


def _pack(parts, width=LANE, mult=8):
    flat = jnp.concatenate([a.reshape(-1) for a in parts])
    n = flat.shape[0]
    per = width * mult
    total = -(-n // per) * per
    return jnp.pad(flat, (0, total - n)).reshape(total // width, width)


def _unpack(packed, shapes):
    flat = packed.reshape(-1)
    out, off = [], 0
    for s in shapes:
        n = 1
        for d in s:
            n *= d
        out.append(flat[off:off + n].reshape(s))
        off += n
    return out


_SMALL_SHARDED = ("hg_lb", "rw_mu", "rw_w0", "rw_w2", "rw_a0", "rw_a2")
_REPLICATED = ("c_ctx", "ada_b", "norm_g", "hg_norm_g", "rw_kk", "rw_ka", "rw_rk", "rw_gn_g", "rw_gn_b", "final_g")
_BIG = ("ada_w", "w_in", "w_hg_out", "w_rw_out", "w_out")
_WEIGHTS = ("c_ctx", "ada_w", "ada_b", "norm_g", "w_in", "hg_lb", "hg_norm_g", "rw_mu", "rw_w0", "rw_w2", "rw_a0", "rw_a2",
            "rw_kk", "rw_ka", "rw_rk", "rw_gn_g", "rw_gn_b", "w_hg_out", "w_rw_out", "w_out", "final_g")


def _join_shards(st):
    a = jnp.moveaxis(st, 0, -2)
    return a.reshape(a.shape[:-2] + (a.shape[-2] * a.shape[-1],))


def _split_shards(a):
    s = a.reshape(a.shape[:-1] + (N_SHARD, a.shape[-1] // N_SHARD))
    return jnp.moveaxis(s, -2, 0)


def kernel(x, c, ctx, c_ctx, ada_w, ada_b, norm_g, w_in, hg_lb, hg_norm_g, rw_mu, rw_w0, rw_w2, rw_a0, rw_a2, rw_kk, rw_ka, rw_rk, rw_gn_g, rw_gn_b, w_hg_out, w_rw_out, w_out, final_g, loss_target, m_c_ctx, m_ada_w, m_ada_b, m_norm_g, m_w_in, m_hg_lb, m_hg_norm_g, m_rw_mu, m_rw_w0, m_rw_w2, m_rw_a0, m_rw_a2, m_rw_kk, m_rw_ka, m_rw_rk, m_rw_gn_g, m_rw_gn_b, m_w_hg_out, m_w_rw_out, m_w_out, m_final_g, v_c_ctx, v_ada_w, v_ada_b, v_norm_g, v_w_in, v_hg_lb, v_hg_norm_g, v_rw_mu, v_rw_w0, v_rw_w2, v_rw_a0, v_rw_a2, v_rw_kk, v_rw_ka, v_rw_rk, v_rw_gn_g, v_rw_gn_b, v_w_hg_out, v_w_rw_out, v_w_out, v_final_g):
    w = dict(c_ctx=c_ctx, ada_w=ada_w, ada_b=ada_b, norm_g=norm_g, w_in=w_in, hg_lb=hg_lb, hg_norm_g=hg_norm_g, rw_mu=rw_mu,
             rw_w0=rw_w0, rw_w2=rw_w2, rw_a0=rw_a0, rw_a2=rw_a2, rw_kk=rw_kk, rw_ka=rw_ka, rw_rk=rw_rk, rw_gn_g=rw_gn_g,
             rw_gn_b=rw_gn_b, w_hg_out=w_hg_out, w_rw_out=w_rw_out, w_out=w_out, final_g=final_g)
    m = dict(c_ctx=m_c_ctx, ada_w=m_ada_w, ada_b=m_ada_b, norm_g=m_norm_g, w_in=m_w_in, hg_lb=m_hg_lb, hg_norm_g=m_hg_norm_g,
             rw_mu=m_rw_mu, rw_w0=m_rw_w0, rw_w2=m_rw_w2, rw_a0=m_rw_a0, rw_a2=m_rw_a2, rw_kk=m_rw_kk, rw_ka=m_rw_ka,
             rw_rk=m_rw_rk, rw_gn_g=m_rw_gn_g, rw_gn_b=m_rw_gn_b, w_hg_out=m_w_hg_out, w_rw_out=m_w_rw_out, w_out=m_w_out,
             final_g=m_final_g)
    v = dict(c_ctx=v_c_ctx, ada_w=v_ada_w, ada_b=v_ada_b, norm_g=v_norm_g, w_in=v_w_in, hg_lb=v_hg_lb, hg_norm_g=v_hg_norm_g,
             rw_mu=v_rw_mu, rw_w0=v_rw_w0, rw_w2=v_rw_w2, rw_a0=v_rw_a0, rw_a2=v_rw_a2, rw_kk=v_rw_kk, rw_ka=v_rw_ka,
             rw_rk=v_rw_rk, rw_gn_g=v_rw_gn_g, rw_gn_b=v_rw_gn_b, w_hg_out=v_w_hg_out, w_rw_out=v_w_rw_out, w_out=v_w_out,
             final_g=v_final_g)

    def mat(a):
        return a.reshape(a.shape[-2], a.shape[-1])

    def pack_small(d):
        return _pack([d[n] for n in _SMALL_SHARDED], mult=2 * ROW_ALIGN)

    my_core = lax.axis_index("c").astype(jnp.int32).reshape(1)
    my_chip = (2 * lax.axis_index("x") + lax.axis_index("y")).astype(jnp.int32).reshape(1)

    small_shapes = [w[n].shape for n in _SMALL_SHARDED]
    big_bf = [_cast_into_slot(f"to_bf16_{n}", mat(w[n]), my_chip) for n in _BIG]
    small_mine = pack_small(w)
    small_slots = lax.dynamic_update_slice(jnp.zeros((N_SHARD,) + small_mine.shape, F32), small_mine[None], (my_chip[0], 0, 0))
    gathered = _weights_gather("weights_gather", big_bf, [small_slots])
    ada_st, w_in_st, w_hg_st, w_rw_st, w_out_st, small_st = gathered
    full_small = {}
    per_chip = [_unpack(small_st[j], small_shapes) for j in range(N_SHARD)]
    for i, n in enumerate(_SMALL_SHARDED):
        full_small[n] = _join_shards(jnp.stack([per_chip[j][i] for j in range(N_SHARD)], axis=0))
    dm = x.shape[-1]
    w_out_full = w_out_st.reshape(dm, dm)

    loss_b, grad_x, g = _local_step(
        x[0], c, ctx[0], c_ctx, ada_st, ada_b, norm_g, w_in_st, full_small["hg_lb"], hg_norm_g, full_small["rw_mu"][0],
        full_small["rw_w0"][0], full_small["rw_w2"][0], full_small["rw_a0"][0], full_small["rw_a2"][0], rw_kk, rw_ka, rw_rk,
        rw_gn_g, rw_gn_b, w_hg_st, w_rw_st, w_out_full, final_g, loss_target[0], my_core)
    loss = lax.psum(loss_b[0, 0], ("x", "y", "c"))

    g_small = {"hg_lb": g["hg_lb"], "rw_mu": g["rw_mu"][None], "rw_w0": g["rw_w0"][None], "rw_w2": g["rw_w2"][None],
               "rw_a0": g["rw_a0"][None], "rw_a2": g["rw_a2"][None]}
    split = {n: _split_shards(g_small[n]) for n in _SMALL_SHARDED}
    small_parts = jnp.stack([pack_small({n: split[n][j] for n in _SMALL_SHARDED}) for j in range(N_SHARD)], axis=0)
    def finish(name, chip_sum, landed):
        half = _sum_landed(f"grads_sum_{name}", landed, chip_sum, my_chip)
        return half, _pair_exchange(f"grads_pair_swap_{name}", half, my_core, False, F32)

    res = {}
    later = {"ada_w": g["ada_w"], "small": small_parts}
    later_chip = [_pair_exchange(f"grads_pair_sum_{n}", a, my_core, True, BF16) for n, a in later.items()]
    outs, later_landed = _adamw_halves("adamw_w_in", mat(w["w_in"]), mat(m["w_in"]), mat(v["w_in"]),
                                       *finish("w_in", *g["w_in"]), my_core, scatter=tuple(later_chip))
    res["w_in"] = [o.reshape(w["w_in"].shape) for o in outs]
    pending = {n: g[n] for n in ("w_hg_out", "w_rw_out", "w_out")}
    pending.update(zip(later, zip(later_chip, later_landed)))
    rep_shapes = [w[n].shape for n in _REPLICATED]
    rep_all = _gather_all("grads_replicated", _pack([g[n].reshape(w[n].shape) for n in _REPLICATED]))

    for n in ("ada_w", "w_hg_out", "w_rw_out", "w_out"):
        outs = _adamw_halves(f"adamw_{n}", mat(w[n]), mat(m[n]), mat(v[n]), *finish(n, *pending[n]), my_core)
        res[n] = [o.reshape(w[n].shape) for o in outs]
    outs = _adamw_halves("adamw_small", small_mine, pack_small(m), pack_small(v), *finish("small", *pending["small"]), my_core)
    for i, vals in enumerate(zip(*[_unpack(o, small_shapes) for o in outs])):
        res[_SMALL_SHARDED[i]] = list(vals)
    outs = _adamw("adamw_replicated", _pack([w[n] for n in _REPLICATED]), _pack([m[n] for n in _REPLICATED]),
                  _pack([v[n] for n in _REPLICATED]), rep_all)
    for i, vals in enumerate(zip(*[_unpack(o, rep_shapes) for o in outs])):
        res[_REPLICATED[i]] = list(vals)

    return (loss, grad_x[None], *[res[n][0] for n in _WEIGHTS], *[res[n][1] for n in _WEIGHTS],
            *[res[n][2] for n in _WEIGHTS], *[res[n][3] for n in _WEIGHTS])
```

```python
import functools

import jax
import jax.numpy as jnp
from jax import lax
from jax.experimental import pallas as pl
from jax.experimental.pallas import tpu as pltpu

HI = lax.Precision.HIGHEST
F32 = jnp.float32
BF16 = jnp.bfloat16

NORM_EPS = 1e-6
HG_HEAD = 128
RW_HEAD = 64
RW_LORA = 64
RW_GN_EPS = 64e-5
GRID_W = 64
SUB = 16
RW_SUB = 16
STEP = 64
RW_STEP = 64
N_SHARD = 4
N_DEV = 8
LANE = 128

ADAM_LR = 0.001
ADAM_B1 = 0.9
ADAM_B2 = 0.999
ADAM_EPS = 1e-08
ADAM_WD = 0.01
ADAM_STEP = 10

VMEM_LIMIT = 56 * 1024 * 1024


def _params(sem=None):
    return pltpu.CompilerParams(dimension_semantics=sem, vmem_limit_bytes=VMEM_LIMIT)


def _tile(n, cands):
    for c in cands:
        if n % c == 0:
            return c
    return n


def _iota2(n, m, d):
    return lax.broadcasted_iota(jnp.int32, (n, m), d)


def _before(n, rev, strict):
    t, s = _iota2(n, n, 0), _iota2(n, n, 1)
    if rev:
        return (s > t) if strict else (s >= t)
    return (s < t) if strict else (s <= t)


def _running_sum(a, axis, rev):
    n = a.shape[axis]
    shift = 1
    while shift < n:
        pad = list(a.shape)
        pad[axis] = shift
        zeros = jnp.zeros(pad, a.dtype)
        if rev:
            moved = jnp.concatenate([lax.slice_in_dim(a, shift, n, axis=axis), zeros], axis=axis)
        else:
            moved = jnp.concatenate([zeros, lax.slice_in_dim(a, 0, n - shift, axis=axis)], axis=axis)
        a = a + moved
        shift *= 2
    return a


def _sdot(a, b, spec):
    return jnp.einsum(spec, a, b, precision=lax.Precision.DEFAULT, preferred_element_type=F32)


def _hg_step(s0, qraw, iin, fin, lb2, rev):
    c, w = qraw.shape
    h = w // HG_HEAD
    nsub = c // SUB
    lb = jax.nn.sigmoid(lb2[0:1] - lb2[1:2])
    q = jax.nn.silu(qraw)
    fg = lb + (1.0 - lb) * jax.nn.sigmoid(fin)
    kk = 1.0 - fg
    g = jnp.log(fg)
    bcum = _running_sum(g, 0, rev)
    def heads(a):
        return jnp.swapaxes(a.reshape(a.shape[0], h, HG_HEAD), 0, 1)

    def unheads(a):
        return jnp.swapaxes(a, 0, 1).reshape(a.shape[1], w)

    blocks = [slice(j * SUB, (j + 1) * SUB) for j in range(nsub)]
    outs = []
    for sl in blocks:
        qs, ks, vs, bc = [a[sl].reshape(SUB, h, HG_HEAD) for a in (q, kk, iin, bcum)]
        o = jnp.zeros((SUB, h, HG_HEAD), F32)
        for si in range(SUB):
            after = slice(0, si + 1) if rev else slice(si, SUB)
            dec = jnp.exp(jnp.minimum(bc[after] - bc[si:si + 1], 0.0))
            a = jnp.sum(qs[after] * ks[si:si + 1] * dec, axis=-1, keepdims=True)
            term = a * vs[si:si + 1]
            n_rest = SUB - 1 - si if rev else si
            if n_rest:
                rest = jnp.zeros((n_rest, h, HG_HEAD), F32)
                term = jnp.concatenate([term, rest] if rev else [rest, term], axis=0)
            o = o + term
        outs.append(o.reshape(SUB, w))
    order = list(range(nsub - 1, -1, -1)) if rev else list(range(nsub))
    for pos in range(1, nsub):
        j, before = order[pos], order[:pos]
        first = (j + 1) * SUB - 1 if rev else j * SUB
        bstart = bcum[first:first + 1] - g[first:first + 1]
        qp = heads(q[blocks[j]] * jnp.exp(bcum[blocks[j]] - bstart))
        kp = heads(jnp.concatenate([kk[blocks[p]] * jnp.exp(bstart - bcum[blocks[p]]) for p in before], axis=0))
        vp = heads(jnp.concatenate([iin[blocks[p]] for p in before], axis=0))
        outs[j] = outs[j] + unheads(_sdot(_sdot(qp, kp, 'htk,hsk->hts'), vp, 'hts,hsv->htv'))
    o_state = unheads(_sdot(heads(q * jnp.exp(bcum)), s0, 'htk,hvk->htv'))
    last = 0 if rev else c - 1
    blast = bcum[last:last + 1]
    s_new = heads(jnp.exp(blast)) * s0 + _sdot(heads(iin), heads(kk * jnp.exp(blast - bcum)), 'hsv,hsk->hvk')
    return jnp.concatenate(outs, axis=0) + o_state, s_new


def _tri_solve(lmat, rhs, rev):
    hh, c, _ = lmat.shape
    sub = RW_SUB
    nb = c // sub
    diag = jnp.concatenate([lmat[:, i * sub:(i + 1) * sub, i * sub:(i + 1) * sub] for i in range(nb)], axis=0)
    dt = jnp.transpose(diag, (1, 2, 0))
    col = lax.broadcasted_iota(jnp.int32, (sub, 1), 0)
    inv_rows = [None] * sub
    order = list(range(sub - 1, -1, -1)) if rev else list(range(sub))
    for pos, t in enumerate(order):
        row = jnp.broadcast_to((col == t).astype(F32), (sub, dt.shape[2]))
        for s in order[:pos]:
            row = row - dt[t, s:s + 1, :] * inv_rows[s]
        inv_rows[t] = row
    tinv = jnp.transpose(jnp.concatenate([r[None] for r in inv_rows], axis=0), (2, 0, 1))
    p = [None] * nb
    done = []
    for i in (range(nb - 1, -1, -1) if rev else range(nb)):
        r = rhs[:, i * sub:(i + 1) * sub]
        if done:
            lrow = jnp.concatenate([lmat[:, i * sub:(i + 1) * sub, m * sub:(m + 1) * sub] for m in done], axis=2)
            r = r - _sdot(lrow, jnp.concatenate([p[m] for m in done], axis=1), 'hts,hsv->htv')
        p[i] = _sdot(tinv[i * hh:(i + 1) * hh], r, 'hts,hsv->htv')
        done.append(i)
    return jnp.concatenate(p, axis=1)


def _rw_step(s0, r, k, v, wlo, alo, w0h, w2h, a0h, a2h, kkh, kah, rev):
    hh, c, _ = r.shape
    tl = jnp.broadcast_to(jnp.tanh(wlo)[None], (hh, c, wlo.shape[1]))
    al = jnp.broadcast_to(alo[None], (hh, c, alo.shape[1]))
    wlog = -jax.nn.softplus(-(w0h + _sdot(tl, w2h, 'hcl,hlj->hcj'))) - 0.5
    lw = -jnp.exp(wlog)
    a = jax.nn.sigmoid(a0h + _sdot(al, a2h, 'hcl,hlj->hcj'))
    kk = k * kkh
    kk = kk * lax.rsqrt(jnp.sum(kk * kk, axis=-1, keepdims=True) + 1e-12)
    kd = k * (1.0 + (a - 1.0) * kah)
    b = kk * a
    cum = _running_sum(lw, 1, rev)
    ecum, encum = jnp.exp(cum), jnp.exp(-cum)
    alpha = jnp.exp(cum - lw) * kk
    beta = b * encum
    kappa = kd * encum
    rho = r * ecum
    m_lt = _before(c, rev, True)[None]
    m_le = _before(c, rev, False)[None]
    ar = jnp.concatenate([alpha, rho], axis=1)
    kb = jnp.concatenate([kappa, beta], axis=1)
    gram = _sdot(ar, kb, 'htk,hsk->hts')
    a_kap = jnp.where(m_lt, gram[:, :c, :c], 0.0)
    a_bet = jnp.where(m_lt, gram[:, :c, c:], 0.0)
    b_kap = jnp.where(m_le, gram[:, c:, :c], 0.0)
    b_bet = jnp.where(m_le, gram[:, c:, c:], 0.0)
    from_state = _sdot(ar, s0, 'htk,hvk->htv')
    p = _tri_solve(a_bet, from_state[:, :c] + _sdot(a_kap, v, 'hts,hsv->htv'), rev)
    vp = jnp.concatenate([v, -p], axis=1)
    y = from_state[:, c:] + _sdot(jnp.concatenate([b_kap, b_bet], axis=2), vp, 'hts,hsv->htv')
    stil = s0 + _sdot(vp, kb, 'hsv,hsk->hvk')
    last = 0 if rev else c - 1
    return y, stil * ecum[:, last:last + 1, :]


def _fn_h(s, norm_g, scale, shift):
    return s * lax.rsqrt(jnp.mean(s * s, axis=-1, keepdims=True) + NORM_EPS) * norm_g * (1.0 + scale) + shift


def _fn_hgpost(of, ob, z, g):
    tm, w = of.shape
    o = (of + ob).reshape(tm, w // HG_HEAD, HG_HEAD)
    o = o * lax.rsqrt(jnp.mean(o * o, axis=-1, keepdims=True) + NORM_EPS)
    return o.reshape(tm, w) * g * jax.nn.silu(z)


def _fn_rwpost(y0, y1, r, k, v, alo, z, a0, a2, k_a, r_k, gn_g, gn_b):
    tm, w = r.shape
    nh = w // RW_HEAD
    asum = 0.0
    for d in range(2):
        asum = asum + jax.nn.sigmoid(a0[d:d + 1] + jnp.dot(alo[:, d * RW_LORA:(d + 1) * RW_LORA], a2[d],
                                                           precision=HI, preferred_element_type=F32))
    k_sum = k * (2.0 + (asum - 2.0) * k_a)
    ys = (y0 + y1).reshape(tm, nh, RW_HEAD)
    mean = jnp.mean(ys, axis=-1, keepdims=True)
    var = jnp.mean(jnp.square(ys - mean), axis=-1, keepdims=True)
    y = ((ys - mean) * lax.rsqrt(var + RW_GN_EPS)).reshape(tm, w) * gn_g + gn_b
    bonus = jnp.sum((r * k_sum * r_k).reshape(tm, nh, RW_HEAD), axis=-1, keepdims=True) * v.reshape(tm, nh, RW_HEAD)
    return (y + bonus.reshape(tm, w)) * jax.nn.silu(z)


def _fn_merge(a, b, ghg, grw):
    return jax.nn.sigmoid(ghg) * a + jax.nn.sigmoid(grw) * b


def _fn_final(xs, o, gate, final_g, tgt):
    x2 = xs + gate * o
    y = x2 * lax.rsqrt(jnp.mean(x2 * x2, axis=-1, keepdims=True) + NORM_EPS) * final_g
    return 0.5 * jnp.sum(jnp.mean(jnp.square(y - tgt), axis=-1))


def _row_call(name, fn, n_tiles, tm, row_ins, full_ins, row_outs, acc_outs):
    n_ri, n_fi, n_ro = len(row_ins), len(full_ins), len(row_outs)

    def body(*refs):
        i = pl.program_id(0)
        rvals = [r[...] for r in refs[:n_ri]]
        fvals = [r[...] for r in refs[n_ri:n_ri + n_fi]]
        outs = refs[n_ri + n_fi:]
        ro, ao = fn(i, rvals, fvals)
        for ref, val in zip(outs[:n_ro], ro):
            ref[...] = val.astype(ref.dtype)
        for ref, val in zip(outs[n_ro:], ao):
            @pl.when(i == 0)
            def _(ref=ref):
                ref[...] = jnp.zeros_like(ref)
            ref[...] += val.astype(ref.dtype)

    def rspec(width, cb, off, rows):
        return pl.BlockSpec((tm, width), lambda i: (jnp.clip(i - off, 0, rows // tm - 1), cb))

    def fspec(shape):
        nd = len(shape)
        return pl.BlockSpec(shape, lambda i: (0,) * nd)

    in_specs = [rspec(w, cb, off, a.shape[0]) for (a, cb, w, off) in row_ins] + [fspec(a.shape) for a in full_ins]
    out_specs = [rspec(w, 0, off, rows) for (rows, w, _, off) in row_outs] + [fspec(s) for (s, _) in acc_outs]
    out_shape = [jax.ShapeDtypeStruct((rows, w), dt) for (rows, w, dt, _) in row_outs] + \
                [jax.ShapeDtypeStruct(s, dt) for (s, dt) in acc_outs]
    res = pl.pallas_call(
        body, name=name, grid=(n_tiles,), in_specs=in_specs, out_specs=out_specs, out_shape=out_shape,
        compiler_params=_params(("arbitrary",)),
    )(*[a for (a, _, _, _) in row_ins], *full_ins)
    return list(res)


def _mm(name, a, b, m, n, k_steps, tm, tn, a_block, a_map, b_block, b_map, o_shape, o_block, o_map,
        contract, out_dtype=F32, scatter=()):
    ns = len(scatter)
    grid = (m // tm, n // tn, k_steps)

    def body(*refs):
        a_ref, b_ref, o_ref, acc_ref = refs[0], refs[1], refs[2 + ns], refs[3 + 2 * ns]
        kk = pl.program_id(2)
        if ns:
            sc_refs = (refs[2:2 + ns], refs[3 + ns:3 + 2 * ns]) + tuple(refs[4 + 2 * ns:])
            at = (pl.program_id(0) * grid[1] + pl.program_id(1)) * grid[2] + kk
            pl.when(at == 0)(lambda: _scatter_start(scatter, *sc_refs))

        @pl.when(kk == 0)
        def _():
            acc_ref[...] = jnp.zeros_like(acc_ref)

        acc_ref[...] += lax.dot_general(a_ref[...].astype(BF16), b_ref[...].astype(BF16),
                                        (contract, ((), ())), preferred_element_type=F32)

        @pl.when(kk == k_steps - 1)
        def _():
            o_ref[...] = acc_ref[...].astype(o_ref.dtype)

        if ns:
            pl.when(at == grid[0] * grid[1] * grid[2] - 1)(lambda: _scatter_wait(scatter, *sc_refs))

    hbm = pl.BlockSpec(memory_space=pl.ANY)
    sems = [pltpu.SemaphoreType.DMA((ns, _PEER_CHIPS))] * 2 if ns else []
    res = pl.pallas_call(
        body, name=name, grid=grid,
        in_specs=[pl.BlockSpec(a_block, a_map), pl.BlockSpec(b_block, b_map)] + [hbm] * ns,
        out_specs=[pl.BlockSpec(o_block, o_map)] + [hbm] * ns,
        out_shape=[jax.ShapeDtypeStruct(o_shape, out_dtype)] + [jax.ShapeDtypeStruct(s.shape, s.dtype) for s in scatter],
        scratch_shapes=[pltpu.VMEM((tm, tn), F32)] + sems,
        compiler_params=_params(("arbitrary",) * 3 if ns else ("parallel", "parallel", "arbitrary")),
    )(a, b, *scatter)
    return (res[0], list(res[1:])) if ns else res[0]


_TM = (768, 512, 256, 128, 64, 32, 16, 8)
_TN = (512, 256, 128)
_TK = (1024, 768, 512, 256, 128)
_TK_WIDE = (768, 512, 256, 128)
WIDE_OUT_BYTES = 32 << 20


def _tm_wide(m, ns):
    for tm in _TM:
        if m % tm == 0 and 3 * 4 * tm * ns <= WIDE_OUT_BYTES:
            return tm
    return m


def _mm_nn(name, a, b, out_dtype=F32):
    m, k = a.shape
    n = b.shape[1]
    tm, tn, tk = _tile(m, _TM), _tile(n, _TN), _tile(k, _TK)
    return _mm(name, a, b, m, n, k // tk, tm, tn, (tm, tk), lambda i, j, s: (i, s), (tk, tn), lambda i, j, s: (s, j),
               (m, n), (tm, tn), lambda i, j, s: (i, j), ((1,), (0,)), out_dtype)


def _mm_nt(name, a, b, out_dtype=F32):
    m, k = a.shape
    n = b.shape[0]
    tm, tn, tk = _tile(m, _TM), _tile(n, _TN), _tile(k, _TK)
    return _mm(name, a, b, m, n, k // tk, tm, tn, (tm, tk), lambda i, j, s: (i, s), (tn, tk), lambda i, j, s: (j, s),
               (m, n), (tm, tn), lambda i, j, s: (i, j), ((1,), (1,)), out_dtype)


def _mm_tn(name, a, b, out_dtype=F32):
    k, m = a.shape
    n = b.shape[1]
    tm, tn, tk = _tile(m, _TM), _tile(n, _TN), _tile(k, _TK)
    return _mm(name, a, b, m, n, k // tk, tm, tn, (tk, tm), lambda i, j, s: (s, i), (tk, tn), lambda i, j, s: (s, j),
               (m, n), (tm, tn), lambda i, j, s: (i, j), ((0,), (0,)), out_dtype)


def _mm_n_st(name, a, bst, out_dtype=F32, joined=False):
    m, k = a.shape
    ns_, _, ns = bst.shape
    tm, tk = _tm_wide(m, ns), _tile(k, (512, 256, 128))
    out = ((m, ns_ * ns), (tm, ns), lambda i, j, s: (i, j)) if joined else \
          ((ns_, m, ns), (None, tm, ns), lambda i, j, s: (j, i, 0))
    return _mm(name, a, bst, m, ns_ * ns, k // tk, tm, ns,
               (tm, tk), lambda i, j, s: (i, s), (None, tk, ns), lambda i, j, s: (j, s, 0), *out, ((1,), (0,)), out_dtype)


def _mm_st_t(name, ast, bst, out_dtype=F32, scatter=()):
    ns_, n, ns = bst.shape
    m = ast.shape[-2]
    tm, tn = _tile(m, _TM), _tile(n, _TN)
    a_side = ((None, tm, ns), lambda i, j, s: (s, i, 0)) if ast.ndim == 3 else ((tm, ns), lambda i, j, s: (i, s))
    return _mm(name, ast, bst, m, n, ns_, tm, tn, *a_side, (None, tn, ns), lambda i, j, s: (s, j, 0),
               (m, n), (tm, tn), lambda i, j, s: (i, j), ((1,), (1,)), out_dtype, scatter)


def _mm_t_st(name, a, bst, out_dtype=F32, scatter=(), n_shard=N_SHARD):
    k, m = a.shape
    ns = bst.shape[-1] if bst.ndim == 3 else bst.shape[-1] // n_shard
    tm, tk = _tile(m, _TN), _tile(k, _TK_WIDE)
    b_side = ((None, tk, ns), lambda i, j, s: (j, s, 0)) if bst.ndim == 3 else ((tk, ns), lambda i, j, s: (s, j))
    return _mm(name, a, bst, m, n_shard * ns, k // tk, tm, ns, (tk, tm), lambda i, j, s: (s, i), *b_side,
               (n_shard, m, ns), (None, tm, ns), lambda i, j, s: (j, i, 0), ((0,), (0,)), out_dtype, scatter)


def _scan_order(j, n_ctx, n_all, rev):
    if not rev:
        return j
    return jnp.where(j < n_ctx, n_ctx - 1 - j, n_all - 1 - (j - n_ctx))


def _hg_scan_fwd(name, p_hg, lb2, d, n_ctx):
    t, w = p_hg.shape[0], lb2.shape[1]
    h = w // HG_HEAD
    n = t // STEP
    rev = d == 1

    def body(q_ref, i_ref, f_ref, lb_ref, o_ref, st_ref, s_ref):
        j = pl.program_id(0)

        @pl.when(j == 0)
        def _():
            s_ref[...] = jnp.zeros_like(s_ref)

        s0 = s_ref[...]
        st_ref[...] = s0
        o, s1 = _hg_step(s0, q_ref[...], i_ref[...], f_ref[...], lb_ref[...], rev)
        o_ref[...] = o
        s_ref[...] = s1

    def rows(cb):
        return pl.BlockSpec((STEP, w), lambda j: (_scan_order(j, n_ctx, n, rev), cb))

    return pl.pallas_call(
        body, name=name, grid=(n,),
        in_specs=[rows(0), rows(1), rows(2 + d), pl.BlockSpec((2, w), lambda j: (0, 0))],
        out_specs=[rows(0), pl.BlockSpec((None, h, HG_HEAD, HG_HEAD), lambda j: (j, 0, 0, 0))],
        out_shape=[jax.ShapeDtypeStruct((t, w), F32), jax.ShapeDtypeStruct((n, h, HG_HEAD, HG_HEAD), F32)],
        scratch_shapes=[pltpu.VMEM((h, HG_HEAD, HG_HEAD), F32)],
        compiler_params=_params(("arbitrary",)),
    )(p_hg, p_hg, p_hg, lb2)


def _hg_scan_bwd(name, p_hg, lb2, states, do, d, n_ctx, other=()):
    t, w = p_hg.shape[0], lb2.shape[1]
    h = w // HG_HEAD
    n = t // STEP
    rev = d == 1
    no = len(other)

    def body(q_ref, i_ref, f_ref, lb_ref, st_ref, do_ref, *refs):
        dq_ref, di_ref, df_ref, dlb_ref, ds_ref = refs[no:]
        step = pl.program_id(0)

        @pl.when(step == 0)
        def _():
            ds_ref[...] = jnp.zeros_like(ds_ref)
            dlb_ref[...] = jnp.zeros_like(dlb_ref)

        _, vjp = jax.vjp(lambda s0, q, i, f, lb: _hg_step(s0, q, i, f, lb, rev),
                         st_ref[...], q_ref[...], i_ref[...], f_ref[...], lb_ref[...])
        ds0, dq, di, df, dlb = vjp((do_ref[...], ds_ref[...]))
        if no:
            dq, di = refs[0][...] + dq, refs[1][...] + di
        dq_ref[...] = dq.astype(dq_ref.dtype)
        di_ref[...] = di.astype(di_ref.dtype)
        df_ref[...] = df.astype(df_ref.dtype)
        dlb_ref[...] += dlb
        ds_ref[...] = ds0

    def rows(cb):
        return pl.BlockSpec((STEP, w), lambda s: (_scan_order(n - 1 - s, n_ctx, n, rev), cb))

    qi = BF16 if no else F32
    return pl.pallas_call(
        body, name=name, grid=(n,),
        in_specs=[rows(0), rows(1), rows(2 + d), pl.BlockSpec((2, w), lambda s: (0, 0)),
                  pl.BlockSpec((None, h, HG_HEAD, HG_HEAD), lambda s: (n - 1 - s, 0, 0, 0)), rows(0)] + [rows(0)] * no,
        out_specs=[rows(0), rows(0), rows(0), pl.BlockSpec((2, w), lambda s: (0, 0))],
        out_shape=[jax.ShapeDtypeStruct((t, w), qi)] * 2 + [jax.ShapeDtypeStruct((t, w), BF16),
                                                            jax.ShapeDtypeStruct((2, w), F32)],
        scratch_shapes=[pltpu.VMEM((h, HG_HEAD, HG_HEAD), F32)],
        compiler_params=_params(("arbitrary",)),
    )(p_hg, p_hg, p_hg, lb2, states, do, *other)


def _to_heads(a, nh):
    return jnp.stack([a[:, i * RW_HEAD:(i + 1) * RW_HEAD] for i in range(nh)], axis=0)


def _from_heads(a):
    return jnp.concatenate([a[i] for i in range(a.shape[0])], axis=-1)


def _rw_scan_fwd(name, sh, hp, d, n_ctx):
    t = sh.shape[0]
    w = (sh.shape[1] - 4 * RW_LORA) // 3
    nh = w // RW_HEAD
    n = t // RW_STEP
    rev = d == 1
    lo = 3 * w // LANE

    def body(r_ref, k_ref, v_ref, wl_ref, al_ref, w0_ref, w2_ref, a0_ref, a2_ref, kk_ref, ka_ref,
             y_ref, st_ref, s_ref):
        j = pl.program_id(0)

        @pl.when(j == 0)
        def _():
            s_ref[...] = jnp.zeros_like(s_ref)

        s0 = s_ref[...]
        st_ref[...] = s0
        wl = wl_ref[...][:, d * RW_LORA:(d + 1) * RW_LORA]
        al = al_ref[...][:, d * RW_LORA:(d + 1) * RW_LORA]
        y, s1 = _rw_step(s0, _to_heads(r_ref[...], nh), _to_heads(k_ref[...], nh), _to_heads(v_ref[...], nh), wl, al,
                         w0_ref[...], w2_ref[...], a0_ref[...], a2_ref[...], kk_ref[...], ka_ref[...], rev)
        y_ref[...] = _from_heads(y)
        s_ref[...] = s1

    def rows(cb, width=w):
        return pl.BlockSpec((RW_STEP, width), lambda j: (_scan_order(j, n_ctx, n, rev), cb))

    def whole(a):
        nd = a.ndim
        return pl.BlockSpec(a.shape, lambda j: (0,) * nd)

    return pl.pallas_call(
        body, name=name, grid=(n,),
        in_specs=[rows(0), rows(1), rows(2), rows(lo, LANE), rows(lo + 1, LANE)] + [whole(a) for a in hp],
        out_specs=[rows(0), pl.BlockSpec((None, nh, RW_HEAD, RW_HEAD), lambda j: (j, 0, 0, 0))],
        out_shape=[jax.ShapeDtypeStruct((t, w), F32), jax.ShapeDtypeStruct((n, nh, RW_HEAD, RW_HEAD), F32)],
        scratch_shapes=[pltpu.VMEM((nh, RW_HEAD, RW_HEAD), F32)],
        compiler_params=_params(("arbitrary",)),
    )(sh, sh, sh, sh, sh, *hp)


def _rw_scan_bwd_both(name, sh, hps, states, dy, n_ctx):
    t = sh.shape[0]
    w = (sh.shape[1] - 4 * RW_LORA) // 3
    nh = w // RW_HEAD
    n = t // RW_STEP
    lo = 3 * w // LANE
    n_in, n_p = 13, 6

    def body(*refs):
        step = pl.program_id(0)
        ins = [refs[d * n_in:(d + 1) * n_in] for d in range(2)]
        outs = [refs[2 * n_in + d * (1 + n_p):2 * n_in + (d + 1) * (1 + n_p)] for d in range(2)]
        ds_refs = refs[2 * n_in + 2 * (1 + n_p):]

        @pl.when(step == 0)
        def _():
            for d in range(2):
                ds_refs[d][...] = jnp.zeros_like(ds_refs[d])
                for ref in outs[d][1:]:
                    ref[...] = jnp.zeros_like(ref)

        for d in range(2):
            r_ref, k_ref, v_ref, wl_ref, al_ref = ins[d][:5]
            hp_refs, st_ref, dy_ref = ins[d][5:11], ins[d][11], ins[d][12]
            wl = wl_ref[...][:, d * RW_LORA:(d + 1) * RW_LORA]
            al = al_ref[...][:, d * RW_LORA:(d + 1) * RW_LORA]
            _, vjp = jax.vjp(functools.partial(_rw_step, rev=d == 1),
                             st_ref[...], _to_heads(r_ref[...], nh), _to_heads(k_ref[...], nh), _to_heads(v_ref[...], nh),
                             wl, al, *[p[...] for p in hp_refs])
            g = vjp((_to_heads(dy_ref[...], nh), ds_refs[d][...]))
            ds_refs[d][...] = g[0]
            zero = jnp.zeros_like(g[4])
            lora = [zero] * 4
            lora[d], lora[2 + d] = g[4], g[5]
            outs[d][0][...] = jnp.concatenate([_from_heads(g[1]), _from_heads(g[2]), _from_heads(g[3])] + lora, axis=-1)
            for ref, val in zip(outs[d][1:], g[6:]):
                ref[...] += val

    def rows(d, cb, width=w):
        return pl.BlockSpec((RW_STEP, width), lambda s: (_scan_order(n - 1 - s, n_ctx, n, d == 1), cb))

    def whole(a):
        nd = a.ndim
        return pl.BlockSpec(a.shape, lambda s: (0,) * nd)

    in_specs, operands, out_specs, out_shape = [], [], [], []
    for d in range(2):
        in_specs += [rows(d, 0), rows(d, 1), rows(d, 2), rows(d, lo, LANE), rows(d, lo + 1, LANE)]
        in_specs += [whole(a) for a in hps[d]]
        in_specs += [pl.BlockSpec((None, nh, RW_HEAD, RW_HEAD), lambda s: (n - 1 - s, 0, 0, 0)), rows(d, 0)]
        operands += [sh] * 5 + list(hps[d]) + [states[d], dy]
        out_specs += [rows(d, 0, sh.shape[1])] + [whole(a) for a in hps[d]]
        out_shape += [jax.ShapeDtypeStruct(sh.shape, F32)] + [jax.ShapeDtypeStruct(a.shape, F32) for a in hps[d]]
    res = pl.pallas_call(
        body, name=name, grid=(n,), in_specs=in_specs, out_specs=out_specs, out_shape=out_shape,
        scratch_shapes=[pltpu.VMEM((nh, RW_HEAD, RW_HEAD), F32)] * 2, compiler_params=_params(("arbitrary",)),
    )(*operands)
    return [res[0], res[1 + n_p]], [res[1:1 + n_p], res[2 + n_p:]]


def _shift_masks(t, n_ctx_rows):
    row = lax.broadcasted_iota(jnp.int32, (t, 1), 0)
    isx = row >= n_ctx_rows
    pos = jnp.where(isx, row - n_ctx_rows, row)
    col = jnp.where(isx, jnp.bitwise_and(pos, GRID_W - 1), pos)
    ncol = jnp.where(isx, GRID_W, n_ctx_rows)
    n_x = t - n_ctx_rows
    ml = col != 0
    mr = col != ncol - 1
    mu = isx & (pos >= GRID_W)
    md = isx & (pos < n_x - GRID_W)
    return ml, mr, mu, md, isx


def _shift_fwd(name, p, col0, mu, n_ctx_rows):
    t, c = p.shape[0], mu.shape[1]
    cw = LANE

    def body(p_ref, mu_ref, o_ref):
        x = p_ref[...]
        m = mu_ref[...]
        ml, mr, mup, mdn, isx = _shift_masks(t, n_ctx_rows)
        left = jnp.where(ml, pltpu.roll(x, 1, 0), 0.0)
        right = jnp.where(mr, pltpu.roll(x, t - 1, 0), 0.0)
        up = jnp.where(mup, pltpu.roll(x, GRID_W, 0), 0.0)
        down = jnp.where(mdn, pltpu.roll(x, t - GRID_W, 0), 0.0)
        out = x + m[0:1] * (left - x) + m[1:2] * (right - x)
        vert = m[2:3] * (up - x) + m[3:4] * (down - x)
        o_ref[...] = out + jnp.where(isx, vert, 0.0)

    return pl.pallas_call(
        body, name=name, grid=(c // cw,),
        in_specs=[pl.BlockSpec((t, cw), lambda j: (0, col0 + j)), pl.BlockSpec((4, cw), lambda j: (0, j))],
        out_specs=pl.BlockSpec((t, cw), lambda j: (0, j)),
        out_shape=jax.ShapeDtypeStruct((t, c), F32),
        compiler_params=_params(("parallel",)),
    )(p, mu)


def _shift_bwd(name, p, col0, mu, dparts, n_ctx_rows):
    t, c = p.shape[0], mu.shape[1]
    cw = LANE
    npart = len(dparts)

    def body(*refs):
        p_ref, mu_ref = refs[0], refs[1]
        dp_ref, dmu_ref = refs[2 + npart], refs[3 + npart]
        x = p_ref[...]
        m = mu_ref[...]
        g = refs[2][...]
        for r in refs[3:2 + npart]:
            g = g + r[...]
        ml, mr, mup, mdn, isx = _shift_masks(t, n_ctx_rows)
        left = jnp.where(ml, pltpu.roll(x, 1, 0), 0.0)
        right = jnp.where(mr, pltpu.roll(x, t - 1, 0), 0.0)
        up = jnp.where(mup, pltpu.roll(x, GRID_W, 0), 0.0)
        down = jnp.where(mdn, pltpu.roll(x, t - GRID_W, 0), 0.0)
        gx = jnp.where(isx, g, 0.0)
        dmu_ref[...] = jnp.concatenate([
            jnp.sum(g * (left - x), axis=0, keepdims=True), jnp.sum(g * (right - x), axis=0, keepdims=True),
            jnp.sum(gx * (up - x), axis=0, keepdims=True), jnp.sum(gx * (down - x), axis=0, keepdims=True)], axis=0)
        coef = 1.0 - m[0:1] - m[1:2] - jnp.where(isx, m[2:3] + m[3:4], 0.0)
        dp = coef * g
        dp = dp + m[0:1] * pltpu.roll(jnp.where(ml, g, 0.0), t - 1, 0)
        dp = dp + m[1:2] * pltpu.roll(jnp.where(mr, g, 0.0), 1, 0)
        dp = dp + m[2:3] * pltpu.roll(jnp.where(mup, g, 0.0), t - GRID_W, 0)
        dp = dp + m[3:4] * pltpu.roll(jnp.where(mdn, g, 0.0), GRID_W, 0)
        dp_ref[...] = dp.astype(dp_ref.dtype)

    col = pl.BlockSpec((t, cw), lambda j: (0, j))
    par = pl.BlockSpec((4, cw), lambda j: (0, j))
    return pl.pallas_call(
        body, name=name, grid=(c // cw,),
        in_specs=[pl.BlockSpec((t, cw), lambda j: (0, col0 + j)), par] + [col] * npart,
        out_specs=[col, par],
        out_shape=[jax.ShapeDtypeStruct((t, c), BF16), jax.ShapeDtypeStruct((4, c), F32)],
        compiler_params=_params(("parallel",)),
    )(p, mu, *dparts)


def _add_small(name, terms, shape):
    flat2 = [a.reshape(-1, a.shape[-1]) for a in terms]
    return _rowwise(name, lambda *v: _slot_sum(list(v)), flat2, F32).reshape(shape)


def _local_step(x, ctx, mod, norm_g, w_in_st, hg_lb, hg_norm_g, rw_mu, rw_w0, rw_w2, rw_a0, rw_a2,
                rw_kk, rw_ka, rw_rk, rw_gn_g, rw_gn_b, w_hg_st, w_rw_st, w_out, final_g, tgt, my_core):
    seq, dm = x.shape
    n_ctx_rows = ctx.shape[0]
    t = seq + n_ctx_rows
    hw = hg_norm_g.shape[-1]
    rw = rw_kk.shape[-1]
    nh_rw = rw // RW_HEAD
    n_ctx = n_ctx_rows // STEP
    tm = _tile(n_ctx_rows, (256, 128, 64))
    nt = t // tm
    nct = n_ctx_rows // tm
    n_sh_cols = 3 * rw + 4 * RW_LORA

    final_g2 = final_g.reshape(1, dm)
    add = _add_small

    def unstack(a_st):
        return jnp.swapaxes(a_st, 0, 1).reshape(a_st.shape[1], -1)

    def restack(a, ns=N_SHARD):
        return jnp.swapaxes(a.reshape(a.shape[0], ns, -1), 0, 1)

    mod3 = mod.reshape(8, 3, dm)

    def pick(i, m3):
        r = jnp.where(i < nct, m3[1], m3[0])
        return r[0:1], r[1:2]

    tokens = [(ctx, 0, dm, 0), (x, 0, dm, nct)]

    def h_fn(i, r, f):
        shift, scale = pick(i, f[1])
        return [_fn_h(jnp.where(i < nct, r[0], r[1]), f[0], scale, shift)], []

    (h,) = _row_call("h_fwd", h_fn, nt, tm, tokens, [norm_g, mod3], [(t, dm, BF16, 0)], [])
    proj = unstack(_mm_n_st("proj_mm", h, w_in_st))
    p_hg = proj
    rs_tile0 = 5 * hw // LANE
    p_zr = proj[:, 5 * hw + n_sh_cols:5 * hw + n_sh_cols + rw]
    p_gt = proj[:, 5 * hw + n_sh_cols + rw:]

    o_hg, st_hg = [], []
    for d in range(2):
        o, st = _hg_scan_fwd(f"hg_scan_fwd{d}", p_hg, hg_lb[d], d, n_ctx)
        o_hg.append(o)
        st_hg.append(st)

    def hgpost_fn(i, r, f):
        return [_fn_hgpost(r[0], r[1], r[2], f[0])], []

    hg_in = [(o_hg[0], 0, hw, 0), (o_hg[1], 0, hw, 0), (p_hg, 4, hw, 0)]
    (y_hg,) = _row_call("hg_post", hgpost_fn, nt, tm, hg_in, [hg_norm_g], [(t, hw, BF16, 0)], [])

    sh = _shift_fwd("rw_shift", proj, rs_tile0, rw_mu, n_ctx_rows)
    hps = []
    for d in range(2):
        hps.append([rw_w0[d].reshape(nh_rw, 1, RW_HEAD), jnp.swapaxes(rw_w2[d].reshape(RW_LORA, nh_rw, RW_HEAD), 0, 1),
                    rw_a0[d].reshape(nh_rw, 1, RW_HEAD), jnp.swapaxes(rw_a2[d].reshape(RW_LORA, nh_rw, RW_HEAD), 0, 1),
                    rw_kk.reshape(nh_rw, 1, RW_HEAD), rw_ka.reshape(nh_rw, 1, RW_HEAD)])
    y_rw_d, st_rw = [], []
    for d in range(2):
        y, st = _rw_scan_fwd(f"rw_scan_fwd{d}", sh, hps[d], d, n_ctx_rows // RW_STEP)
        y_rw_d.append(y)
        st_rw.append(st)

    rw_full = [rw_a0, rw_a2, rw_ka, rw_rk, rw_gn_g, rw_gn_b]
    lo = 3 * rw // LANE
    rw_in = [(y_rw_d[0], 0, rw, 0), (y_rw_d[1], 0, rw, 0), (sh, 0, rw, 0), (sh, 1, rw, 0), (sh, 2, rw, 0),
             (sh, lo + 1, LANE, 0), (p_zr, 0, rw, 0)]

    def rwpost_fn(i, r, f):
        return [_fn_rwpost(*r, *f)], []

    (y_rw,) = _row_call("rw_post", rwpost_fn, nt, tm, rw_in, rw_full, [(t, rw, BF16, 0)], [])

    a_hg = _mm_n_st("hg_out_mm", y_hg, w_hg_st, joined=True)
    a_rw = _mm_n_st("rw_out_mm", y_rw, w_rw_st, joined=True)
    mg_in = [(a_hg, 0, dm, 0), (a_rw, 0, dm, 0), (p_gt, 0, dm, 0), (p_gt, 1, dm, 0)]
    (merged,) = _row_call("merge", lambda i, r, f: ([_fn_merge(*r)], []), nt, tm, mg_in, [], [(t, dm, BF16, 0)], [])
    o_out = _mm_nn("out_mm", merged, w_out)

    def final_fn(i, r, f):
        gate = f[0][0][2:3]
        loss, vjp = jax.vjp(_fn_final, r[0], r[1], gate, f[1], r[2])
        dx, do, dgate, dfg, _ = vjp(jnp.ones((), F32))
        live = i >= nct
        zero = lambda a: jnp.where(live, a, 0.0)
        dmod = jnp.concatenate([jnp.concatenate([jnp.zeros((1, 2 * dm), F32), zero(dgate)], axis=1),
                                jnp.zeros((7, 3 * dm), F32)], axis=0)
        return [zero(dx), zero(do)], [jnp.broadcast_to(zero(loss), (8, LANE)), dmod, zero(dfg)]

    fin_in = [(x, 0, dm, nct), (o_out, 0, dm, 0), (tgt, 0, dm, nct)]
    dx_res, d_o, loss_acc, dmod_gate, d_final_g = _row_call(
        "final", final_fn, nt, tm, fin_in, [mod3, final_g2], [(t, dm, F32, 0), (t, dm, BF16, 0)],
        [((8, LANE), F32), ((8, 3 * dm), F32), ((1, dm), F32)])

    g_w_out = _mm_tn("d_w_out", merged, d_o)
    d_merged = _mm_nt("d_merged", d_o, w_out)

    def merge_bwd(i, r, f):
        _, vjp = jax.vjp(_fn_merge, r[0], r[1], r[2], r[3])
        da, db, dgh, dgr = vjp(r[4])
        return [da, db, jnp.concatenate([dgh, dgr], axis=1)], []

    da_hg, da_rw, dp_gt = _row_call("merge_bwd", merge_bwd, nt, tm, mg_in + [(d_merged, 0, dm, 0)], [],
                                    [(t, dm, BF16, 0), (t, dm, BF16, 0), (t, 2 * dm, BF16, 0)], [])
    g_w_hg_st = _mm_t_st("d_w_hg", y_hg, da_hg)
    g_w_rw_st = _mm_t_st("d_w_rw", y_rw, da_rw)
    dy_hg = _mm_st_t("d_y_hg", da_hg, w_hg_st)
    dy_rw = _mm_st_t("d_y_rw", da_rw, w_rw_st)

    def hgpost_bwd(i, r, f):
        _, vjp = jax.vjp(_fn_hgpost, r[0], r[1], r[2], f[0])
        dof, _, dz, dg = vjp(r[3])
        return [dof, dz], [dg]

    do_hg, dz_hg, g_hg_norm = _row_call("hg_post_bwd", hgpost_bwd, nt, tm, hg_in + [(dy_hg, 0, hw, 0)], [hg_norm_g],
                                        [(t, hw, F32, 0), (t, hw, BF16, 0)], [((1, hw), F32)])
    dq0, di0, df0, dlb0 = _hg_scan_bwd("hg_scan_bwd0", p_hg, hg_lb[0], st_hg[0], do_hg, 0, n_ctx)
    dq, di, df1, dlb1 = _hg_scan_bwd("hg_scan_bwd1", p_hg, hg_lb[1], st_hg[1], do_hg, 1, n_ctx, other=(dq0, di0))
    g_hg_lb = jnp.stack([dlb0, dlb1], axis=0)

    def rwpost_bwd(i, r, f):
        _, vjp = jax.vjp(_fn_rwpost, *r[:7], *f)
        g = vjp(r[7])
        zl = jnp.zeros((g[5].shape[0], 2 * RW_LORA), F32)
        return [g[0], jnp.concatenate([g[2], g[3], g[4], zl, g[5]], axis=1), g[6]], list(g[7:])

    dy_sum, dsh_p, dz_rw, g_a0_p, g_a2_p, g_ka_p, g_rk, g_gn_g, g_gn_b = _row_call(
        "rw_post_bwd", rwpost_bwd, nt, tm, rw_in + [(dy_rw, 0, rw, 0)], rw_full,
        [(t, rw, F32, 0), (t, n_sh_cols, F32, 0), (t, rw, BF16, 0)], [(a.shape, F32) for a in rw_full])
    dsh_dirs, hp_grads = _rw_scan_bwd_both("rw_scan_bwd", sh, hps, st_rw, dy_sum, n_ctx_rows // RW_STEP)
    dp_rs, g_mu = _shift_bwd("rw_shift_bwd", proj, rs_tile0, rw_mu, [dsh_p] + dsh_dirs, n_ctx_rows)

    def flat(a):
        if a.shape[1] == 1:
            return a.reshape(rw)
        return jnp.swapaxes(a, 0, 1).reshape(RW_LORA, rw)

    g_w0 = jnp.stack([flat(hp_grads[d][0]) for d in range(2)], axis=0)
    g_w2 = jnp.stack([flat(hp_grads[d][1]) for d in range(2)], axis=0)
    g_a0 = add("g_a0", [jnp.stack([flat(hp_grads[d][2]) for d in range(2)], axis=0), g_a0_p], (2, rw))
    g_a2 = add("g_a2", [jnp.stack([flat(hp_grads[d][3]) for d in range(2)], axis=0), g_a2_p], (2, RW_LORA, rw))
    g_kk = add("g_kk", [flat(hp_grads[0][4]).reshape(1, rw), flat(hp_grads[1][4]).reshape(1, rw)], (1, rw))
    g_ka = add("g_ka", [flat(hp_grads[0][5]).reshape(1, rw), flat(hp_grads[1][5]).reshape(1, rw), g_ka_p], (1, rw))

    dproj = jnp.concatenate([dq, di, df0, df1, dz_hg, dp_rs, dz_rw, dp_gt], axis=1)
    dproj_st = restack(dproj)
    early = {"w_hg_out": g_w_hg_st, "w_rw_out": g_w_rw_st, "w_out": g_w_out.reshape(N_SHARD, dm // N_SHARD, dm)}
    early_chip = [_pair_exchange(f"grads_pair_sum_{n}", a, my_core, True, BF16) for n, a in early.items()]
    g_w_in_st, early_landed = _mm_t_st("d_w_in", h, dproj_st, scatter=tuple(early_chip))
    w_in_chip = _pair_exchange("grads_pair_sum_w_in", g_w_in_st, my_core, True, BF16)
    dh, (w_in_landed,) = _mm_st_t("d_h", dproj_st, w_in_st, scatter=(w_in_chip,))

    def h_bwd(i, r, f):
        shift, scale = pick(i, f[1])
        is_ctx = i < nct
        _, vjp = jax.vjp(_fn_h, jnp.where(is_ctx, r[0], r[1]), f[0], scale, shift)
        ds, dg, dscale, dshift = vjp(r[2])
        row = jnp.concatenate([dshift, dscale, jnp.zeros((1, dm), F32)], axis=1)
        z = jnp.zeros_like(row)
        dmod = jnp.concatenate([jnp.where(is_ctx, z, row), jnp.where(is_ctx, row, z), jnp.zeros((6, 3 * dm), F32)], axis=0)
        return [ds + r[3]], [dg, dmod]

    grad_x, g_norm_g, dmod_h = _row_call(
        "h_bwd", h_bwd, nt, tm, tokens + [(dh, 0, dm, 0), (dx_res, 0, dm, 0)], [norm_g, mod3],
        [(seq, dm, F32, nct)], [((1, dm), F32), ((8, 3 * dm), F32)])
    dmod = add("d_mod", [dmod_h, dmod_gate], (8, 3 * dm))
    grads = dict(
        norm_g=g_norm_g, w_in=(w_in_chip, w_in_landed), hg_lb=g_hg_lb,
        hg_norm_g=g_hg_norm, rw_mu=g_mu, rw_w0=g_w0, rw_w2=g_w2, rw_a0=g_a0, rw_a2=g_a2, rw_kk=g_kk, rw_ka=g_ka,
        rw_rk=g_rk, rw_gn_g=g_gn_g, rw_gn_b=g_gn_b, final_g=d_final_g.reshape(dm))
    grads.update(zip(early, zip(early_chip, early_landed)))
    return loss_acc[0:1, 0:1], grad_x, dmod, grads


def _my_place():
    return lax.axis_index("x"), lax.axis_index("y"), lax.axis_index("c")


MIN_CHUNK_BYTES = 1 << 18
ROW_ALIGN = 16


def _n_chunks(rows, row_bytes):
    for n in (8, 4, 2):
        if rows % (n * ROW_ALIGN) == 0 and rows // n * row_bytes >= MIN_CHUNK_BYTES:
            return n
    return 1


def _row_bytes(a, lead=1):
    n = a.dtype.itemsize
    for d in a.shape[lead:]:
        n *= d
    return n


def _rows(ref, start, size):
    return ref.at[pl.ds(start, size)]


def _chunked(make, start, size, n):
    cs = size // n
    return [make(start + j * cs, cs) for j in range(n)]


_PEER_CHIPS = 3


def _weights_gather(name, big, small):
    nb, na = len(big), len(big) + len(small)
    arrays = list(big) + list(small)
    n_ici = 6

    def body(*refs):
        outs = refs[na:2 * na]
        send_sems, recv_sems, fsend_sems, frecv_sems = refs[2 * na:]
        x, y, c = _my_place()
        me, sx, sy, sd = 2 * x + y, 2 * (1 - x) + y, 2 * x + (1 - y), 2 * (1 - x) + (1 - y)
        kx, ky, kd = (1 - x, y, c), (x, 1 - y, c), (1 - x, 1 - y, c)

        def ici(a, j, src_slot, dst_slot, to, r0, nr):
            return pltpu.make_async_remote_copy(
                src_ref=_rows(outs[a].at[src_slot], r0, nr), dst_ref=_rows(outs[a].at[dst_slot], r0, nr),
                send_sem=send_sems.at[a, j], recv_sem=recv_sems.at[a, j], device_id=to,
                device_id_type=pl.DeviceIdType.MESH)

        def to_sibling(a, k, slot, r0, nr):
            rows = _rows(outs[a].at[slot], r0, nr)
            return pltpu.make_async_remote_copy(
                src_ref=rows, dst_ref=rows, send_sem=fsend_sems.at[a, k], recv_sem=frecv_sems.at[a, k],
                device_id=(x, y, 1 - c), device_id_type=pl.DeviceIdType.MESH)

        def start(copies):
            for cp in copies:
                cp.start()

        geo = []
        for a in range(nb):
            half = arrays[a].shape[1] // 2
            geo.append((pl.multiple_of(c * half, ROW_ALIGN), pl.multiple_of((1 - c) * half, ROW_ALIGN), half // 2,
                        _n_chunks(half // 2, _row_bytes(arrays[a], 2))))
        plan = [(me, sx, kx, 0), (me, sx, kx, 1), (me, sy, ky, 0), (me, sy, ky, 1), (sx, sd, ky, 0), (sy, sd, kx, 1)]

        def piece(a, j):
            return geo[a][0] + plan[j][3] * geo[a][2]

        for a in range(nb):
            for j in range(4):
                start(_chunked(lambda r0, cs: ici(a, j, me, me, plan[j][2], r0, cs), piece(a, j), geo[a][2], geo[a][3]))
        for a in range(nb, na):
            rows = arrays[a].shape[1]
            for j, to in ((0, kx), (2, ky), (1, kd)):
                ici(a, j, me, me, to, 0, rows).start()
        for a in range(nb):
            for j, first in ((4, 0), (5, 3)):
                src_slot, _, to, _ = plan[j]
                ici(a, first, me, plan[first][1], plan[first][2], piece(a, first), geo[a][2]).wait_recv()
                start(_chunked(lambda r0, cs: ici(a, j, src_slot, src_slot, to, r0, cs), piece(a, j), geo[a][2], geo[a][3]))
        for a in range(nb):
            for j in (1, 2):
                ici(a, j, me, plan[j][1], plan[j][2], piece(a, j), geo[a][2]).wait_recv()
            for k, slot in ((0, sx), (1, sy)):
                start(_chunked(lambda r0, cs: to_sibling(a, k, slot, r0, cs), geo[a][0], 2 * geo[a][2], geo[a][3]))
        for a in range(nb):
            for j in (4, 5):
                ici(a, j, me, sd, plan[j][2], piece(a, j), geo[a][2]).wait_recv()
            start(_chunked(lambda r0, cs: to_sibling(a, 2, sd, r0, cs), geo[a][0], 2 * geo[a][2], geo[a][3]))
        for a in range(nb, na):
            rows = arrays[a].shape[1]
            for j, slot, to in ((0, sx, kx), (2, sy, ky), (1, sd, kd)):
                ici(a, j, me, slot, to, 0, rows).wait_recv()
        for a in range(nb):
            for k, slot in ((0, sx), (1, sy), (2, sd)):
                to_sibling(a, k, slot, geo[a][1], 2 * geo[a][2]).wait_recv()
        for a in range(nb):
            for j in range(n_ici):
                ici(a, j, me, me, plan[j][2], piece(a, j), geo[a][2]).wait_send()
            for k, slot in ((0, sx), (1, sy), (2, sd)):
                to_sibling(a, k, slot, geo[a][0], 2 * geo[a][2]).wait_send()
        for a in range(nb, na):
            rows = arrays[a].shape[1]
            for j, to in ((0, kx), (2, ky), (1, kd)):
                ici(a, j, me, me, to, 0, rows).wait_send()

    hbm = pl.BlockSpec(memory_space=pl.ANY)
    ici_sems = pltpu.SemaphoreType.DMA((na, n_ici))
    pair_sems = pltpu.SemaphoreType.DMA((na, _PEER_CHIPS))
    return pl.pallas_call(
        body, name=name, in_specs=[hbm] * na, out_specs=[hbm] * na,
        out_shape=[jax.ShapeDtypeStruct(a.shape, a.dtype) for a in arrays],
        input_output_aliases={a: a for a in range(na)}, scratch_shapes=[ici_sems, ici_sems, pair_sems, pair_sems],
    )(*arrays)


def _scatter_copy(arrays, ins, outs, send_sems, recv_sems, a, k, slot, r0, nr):
    x, y, c = _my_place()
    px, py = [(1 - x, y), (x, 1 - y), (1 - x, 1 - y)][k]
    return pltpu.make_async_remote_copy(
        src_ref=_rows(ins[a].at[2 * px + py], r0, nr), dst_ref=_rows(outs[a].at[slot], r0, nr),
        send_sem=send_sems.at[a, k], recv_sem=recv_sems.at[a, k], device_id=(px, py, c),
        device_id_type=pl.DeviceIdType.MESH)


def _scatter_start(arrays, ins, outs, send_sems, recv_sems):
    x, y, _ = _my_place()
    for a in range(len(arrays)):
        rows = arrays[a].shape[1]
        for k in range(_PEER_CHIPS):
            for cp in _chunked(lambda r0, cs: _scatter_copy(arrays, ins, outs, send_sems, recv_sems, a, k, 2 * x + y, r0, cs),
                               0, rows, _n_chunks(rows, _row_bytes(arrays[a], 2))):
                cp.start()


def _scatter_wait(arrays, ins, outs, send_sems, recv_sems):
    x, y, _ = _my_place()
    peer_slot = [2 * (1 - x) + y, 2 * x + (1 - y), 2 * (1 - x) + (1 - y)]
    for k in range(_PEER_CHIPS):
        for a in range(len(arrays)):
            _scatter_copy(arrays, ins, outs, send_sems, recv_sems, a, k, peer_slot[k], 0, arrays[a].shape[1]).wait_recv()
    for a in range(len(arrays)):
        for k in range(_PEER_CHIPS):
            _scatter_copy(arrays, ins, outs, send_sems, recv_sems, a, k, 2 * x + y, 0, arrays[a].shape[1]).wait_send()


PAIR_TILE_BYTES = 4 << 20


def _pair_exchange(name, a, place, reduce, out_dtype):
    rows, cols = a.shape[-2], a.shape[-1]
    half = rows // 2 if reduce else rows
    tr = _row_tile_for(half, cols, budget=PAIR_TILE_BYTES)
    nh = half // tr
    n_steps = (N_SHARD if reduce else 1) * nh

    def body(pc_ref, *refs):
        if reduce:
            keep_ref, send_ref, o_ref, land, send_sems, recv_sems, credit, wire = refs
            wire[...] = send_ref[...].astype(BF16)
            src = wire
        else:
            send_ref, o_ref, land, send_sems, recv_sems, credit = refs
            src = send_ref
        x, y, c = _my_place()
        other = (x, y, 1 - c)
        t = pl.program_id(0) * nh + pl.program_id(1) if reduce else pl.program_id(0)
        slot = t % 2

        @pl.when(t >= 2)
        def _():
            pl.semaphore_wait(credit, 1)

        copy = pltpu.make_async_remote_copy(
            src_ref=src, dst_ref=land.at[slot], send_sem=send_sems.at[slot], recv_sem=recv_sems.at[slot],
            device_id=other, device_id_type=pl.DeviceIdType.MESH)
        copy.start()
        copy.wait_recv()
        got = land[slot]
        o_ref[...] = ((keep_ref[...] + got.astype(F32)) if reduce else got).astype(out_dtype)
        copy.wait_send()

        @pl.when(t < n_steps - 2)
        def _():
            pl.semaphore_signal(credit, inc=1, device_id=other, device_id_type=pl.DeviceIdType.MESH)

    if reduce:
        grid = (N_SHARD, nh)
        in_specs = [pl.BlockSpec((None, tr, cols), lambda j, i, pc: (j, pc[0] * nh + i, 0)),
                    pl.BlockSpec((None, tr, cols), lambda j, i, pc: (j, (1 - pc[0]) * nh + i, 0))]
        out_spec = pl.BlockSpec((None, tr, cols), lambda j, i, pc: (j, i, 0))
        out_shape = jax.ShapeDtypeStruct((N_SHARD, half, cols), out_dtype)
        operands = (a, a)
        sem = ("arbitrary", "arbitrary")
    else:
        grid = (nh,)
        in_specs = [pl.BlockSpec((tr, cols), lambda i, pc: (i, 0))]
        out_spec = pl.BlockSpec((tr, cols), lambda i, pc: (i, 0))
        out_shape = jax.ShapeDtypeStruct((half, cols), out_dtype)
        operands = (a,)
        sem = ("arbitrary",)
    return pl.pallas_call(
        body, name=name,
        grid_spec=pltpu.PrefetchScalarGridSpec(
            num_scalar_prefetch=1, grid=grid, in_specs=in_specs, out_specs=out_spec,
            scratch_shapes=[pltpu.VMEM((2, tr, cols), BF16 if reduce else a.dtype), pltpu.SemaphoreType.DMA((2,)),
                            pltpu.SemaphoreType.DMA((2,)), pltpu.SemaphoreType.REGULAR] +
                           ([pltpu.VMEM((tr, cols), BF16)] if reduce else [])),
        out_shape=out_shape, compiler_params=_params(sem),
    )(place, *operands)


def _cast_into_slot(name, a, chip):
    rows, cols = a.shape
    tm = _row_tile_for(rows, cols)

    def body(pc_ref, a_ref, o_ref):
        o_ref[...] = a_ref[...].astype(BF16)

    return pl.pallas_call(
        body, name=name,
        grid_spec=pltpu.PrefetchScalarGridSpec(
            num_scalar_prefetch=1, grid=(rows // tm,), in_specs=[pl.BlockSpec((tm, cols), lambda i, pc: (i, 0))],
            out_specs=pl.BlockSpec((None, tm, cols), lambda i, pc: (pc[0], i, 0))),
        out_shape=jax.ShapeDtypeStruct((N_SHARD, rows, cols), BF16), compiler_params=_params(("parallel",)),
    )(chip, a)


def _sum_landed(name, landed, sent, chip):
    ns, rows, cols = landed.shape
    tm = _row_tile_for(rows, cols)

    def body(pc_ref, *refs):
        own_ref, o_ref = refs[ns], refs[ns + 1]
        me = pc_ref[0]
        terms = [jnp.where(me == j, own_ref[...], refs[j][...]).astype(F32) for j in range(ns)]
        o_ref[...] = _slot_sum(terms)

    def landed_spec(j):
        return pl.BlockSpec((None, tm, cols), lambda i, pc: (jnp.where(pc[0] == j, (j + 1) % ns, j), i, 0))

    return pl.pallas_call(
        body, name=name,
        grid_spec=pltpu.PrefetchScalarGridSpec(
            num_scalar_prefetch=1, grid=(rows // tm,),
            in_specs=[landed_spec(j) for j in range(ns)] + [pl.BlockSpec((None, tm, cols), lambda i, pc: (pc[0], i, 0))],
            out_specs=pl.BlockSpec((tm, cols), lambda i, pc: (i, 0))),
        out_shape=jax.ShapeDtypeStruct((rows, cols), F32), compiler_params=_params(("parallel",)),
    )(chip, *([landed] * ns), sent)


def _gather_all(name, a):
    def body(in_ref, out_ref, send_sems, recv_sems, local_sem):
        x, y, c = _my_place()
        me = 4 * x + 2 * y + c

        def peer(k):
            return (x ^ (k >> 2), y ^ ((k >> 1) & 1), c ^ (k & 1))

        def remote(k, land):
            return pltpu.make_async_remote_copy(
                src_ref=in_ref, dst_ref=out_ref.at[land], send_sem=send_sems.at[k - 1], recv_sem=recv_sems.at[k - 1],
                device_id=peer(k), device_id_type=pl.DeviceIdType.MESH)

        local = pltpu.make_async_copy(in_ref, out_ref.at[me], local_sem)
        local.start()
        for k in range(1, N_DEV):
            remote(k, me).start()
        for k in range(1, N_DEV):
            px, py, pc = peer(k)
            remote(k, 4 * px + 2 * py + pc).wait_recv()
        for k in range(1, N_DEV):
            remote(k, me).wait_send()
        local.wait()

    hbm = pl.BlockSpec(memory_space=pl.ANY)
    return pl.pallas_call(
        body, name=name, in_specs=[hbm], out_specs=hbm,
        out_shape=jax.ShapeDtypeStruct((N_DEV,) + a.shape, a.dtype),
        scratch_shapes=[pltpu.SemaphoreType.DMA((N_DEV - 1,)), pltpu.SemaphoreType.DMA((N_DEV - 1,)), pltpu.SemaphoreType.DMA],
    )(a)


def _row_tile_for(rows, cols, budget=1 << 20):
    if rows * cols * 4 <= budget:
        return rows
    for tm in (1024, 512, 256, 128, 64, 32, 16, 8):
        if rows % tm == 0 and tm * cols * 4 <= budget:
            return tm
    return rows


def _slot_sum(vals):
    g = vals[0]
    for v in vals[1:]:
        g = g + v
    return g


def _rowwise(name, fn, arrays, out_dtype):
    rows, cols = arrays[0].shape
    tm = _row_tile_for(rows, cols)

    def body(*refs):
        refs[-1][...] = fn(*[r[...] for r in refs[:-1]]).astype(out_dtype)

    blk = pl.BlockSpec((tm, cols), lambda i: (i, 0))
    return pl.pallas_call(
        body, name=name, grid=(rows // tm,), in_specs=[blk] * len(arrays), out_specs=blk,
        out_shape=jax.ShapeDtypeStruct((rows, cols), out_dtype), compiler_params=_params(("parallel",)),
    )(*arrays)


def _sum_slots(name, st):
    ns, rows, cols = st.shape
    tm = _row_tile_for(rows, cols)

    def body(s_ref, o_ref):
        o_ref[...] = _slot_sum([s_ref[j].astype(F32) for j in range(ns)])

    return pl.pallas_call(
        body, name=name, grid=(rows // tm,),
        in_specs=[pl.BlockSpec((ns, tm, cols), lambda i: (0, i, 0))],
        out_specs=pl.BlockSpec((tm, cols), lambda i: (i, 0)),
        out_shape=jax.ShapeDtypeStruct((rows, cols), F32),
        compiler_params=_params(("parallel",)),
    )(st)


ADAM_TILE_BYTES = 1 << 20


def _adam_update(g, p_ref, m_ref, v_ref, go_ref, d_ref, mo_ref, vo_ref):
    mn = ADAM_B1 * m_ref[...] + (1.0 - ADAM_B1) * g
    vn = ADAM_B2 * v_ref[...] + (1.0 - ADAM_B2) * jnp.square(g)
    m_hat = mn / (1.0 - ADAM_B1 ** ADAM_STEP)
    v_hat = vn / (1.0 - ADAM_B2 ** ADAM_STEP)
    go_ref[...] = g
    d_ref[...] = -ADAM_LR * (m_hat / (jnp.sqrt(v_hat) + ADAM_EPS) + ADAM_WD * p_ref[...])
    mo_ref[...] = mn
    vo_ref[...] = vn


def _adamw(name, p, m, v, gst):
    rows, cols = p.shape
    ns = gst.shape[0]
    tm = _row_tile_for(rows, cols, budget=ADAM_TILE_BYTES)

    def body(p_ref, m_ref, v_ref, g_ref, *outs):
        _adam_update(_slot_sum([g_ref[j] for j in range(ns)]), p_ref, m_ref, v_ref, *outs)

    blk = pl.BlockSpec((tm, cols), lambda i: (i, 0))
    return pl.pallas_call(
        body, name=name, grid=(rows // tm,),
        in_specs=[blk, blk, blk, pl.BlockSpec((ns, tm, cols), lambda i: (0, i, 0))],
        out_specs=[blk] * 4, out_shape=[jax.ShapeDtypeStruct((rows, cols), F32)] * 4,
        compiler_params=_params(("parallel",)),
    )(p, m, v, gst)


def _adamw_halves(name, p, m, v, mine, theirs, place, scatter=()):
    rows, cols = p.shape
    half = rows // 2
    tm = _row_tile_for(half, cols, budget=ADAM_TILE_BYTES)
    nh = half // tm
    ns = len(scatter)

    def body(pc_ref, p_ref, m_ref, v_ref, mine_ref, theirs_ref, *refs):
        if ns:
            sc_refs = (refs[:ns], refs[ns + 4:2 * ns + 4]) + tuple(refs[2 * ns + 4:])
            at = pl.program_id(0) * nh + pl.program_id(1)
            pl.when(at == 0)(lambda: _scatter_start(scatter, *sc_refs))
        g = jnp.where(pl.program_id(0) == pc_ref[0], mine_ref[...], theirs_ref[...])
        _adam_update(g, p_ref, m_ref, v_ref, *refs[ns:ns + 4])
        if ns:
            pl.when(at == 2 * nh - 1)(lambda: _scatter_wait(scatter, *sc_refs))

    blk = pl.BlockSpec((tm, cols), lambda h, i, pc: (h * nh + i, 0))
    hblk = pl.BlockSpec((tm, cols), lambda h, i, pc: (i, 0))
    hbm = pl.BlockSpec(memory_space=pl.ANY)
    res = pl.pallas_call(
        body, name=name,
        grid_spec=pltpu.PrefetchScalarGridSpec(
            num_scalar_prefetch=1, grid=(2, nh), in_specs=[blk, blk, blk, hblk, hblk] + [hbm] * ns,
            out_specs=[blk] * 4 + [hbm] * ns,
            scratch_shapes=[pltpu.SemaphoreType.DMA((ns, _PEER_CHIPS))] * 2 if ns else []),
        out_shape=[jax.ShapeDtypeStruct((rows, cols), F32)] * 4 + [jax.ShapeDtypeStruct(s.shape, s.dtype) for s in scatter],
        compiler_params=_params(("arbitrary", "arbitrary") if ns else ("parallel", "parallel")),
    )(place, p, m, v, mine, theirs, *scatter)
    return (list(res[:4]), list(res[4:])) if ns else res


def _pack(parts, width=LANE, mult=8):
    flat = jnp.concatenate([a.reshape(-1) for a in parts])
    n = flat.shape[0]
    per = width * mult
    total = -(-n // per) * per
    return jnp.pad(flat, (0, total - n)).reshape(total // width, width)


def _unpack(packed, shapes):
    flat = packed.reshape(-1)
    out, off = [], 0
    for s in shapes:
        n = 1
        for d in s:
            n *= d
        out.append(flat[off:off + n].reshape(s))
        off += n
    return out


_SMALL_SHARDED = ("hg_lb", "rw_mu", "rw_w0", "rw_w2", "rw_a0", "rw_a2")
_REPLICATED = ("c_ctx", "ada_b", "norm_g", "hg_norm_g", "rw_kk", "rw_ka", "rw_rk", "rw_gn_g", "rw_gn_b", "final_g")
_GATHERED = ("w_in", "w_hg_out", "w_rw_out", "w_out")
_WEIGHTS = ("c_ctx", "ada_w", "ada_b", "norm_g", "w_in", "hg_lb", "hg_norm_g", "rw_mu", "rw_w0", "rw_w2", "rw_a0", "rw_a2",
            "rw_kk", "rw_ka", "rw_rk", "rw_gn_g", "rw_gn_b", "w_hg_out", "w_rw_out", "w_out", "final_g")


def _join_shards(st):
    a = jnp.moveaxis(st, 0, -2)
    return a.reshape(a.shape[:-2] + (a.shape[-2] * a.shape[-1],))


def _split_shards(a):
    s = a.reshape(a.shape[:-1] + (N_SHARD, a.shape[-1] // N_SHARD))
    return jnp.moveaxis(s, -2, 0)


def kernel(x, c, ctx, c_ctx, ada_w, ada_b, norm_g, w_in, hg_lb, hg_norm_g, rw_mu, rw_w0, rw_w2, rw_a0, rw_a2, rw_kk, rw_ka, rw_rk, rw_gn_g, rw_gn_b, w_hg_out, w_rw_out, w_out, final_g, loss_target, m_c_ctx, m_ada_w, m_ada_b, m_norm_g, m_w_in, m_hg_lb, m_hg_norm_g, m_rw_mu, m_rw_w0, m_rw_w2, m_rw_a0, m_rw_a2, m_rw_kk, m_rw_ka, m_rw_rk, m_rw_gn_g, m_rw_gn_b, m_w_hg_out, m_w_rw_out, m_w_out, m_final_g, v_c_ctx, v_ada_w, v_ada_b, v_norm_g, v_w_in, v_hg_lb, v_hg_norm_g, v_rw_mu, v_rw_w0, v_rw_w2, v_rw_a0, v_rw_a2, v_rw_kk, v_rw_ka, v_rw_rk, v_rw_gn_g, v_rw_gn_b, v_w_hg_out, v_w_rw_out, v_w_out, v_final_g):
    w = dict(c_ctx=c_ctx, ada_w=ada_w, ada_b=ada_b, norm_g=norm_g, w_in=w_in, hg_lb=hg_lb, hg_norm_g=hg_norm_g, rw_mu=rw_mu,
             rw_w0=rw_w0, rw_w2=rw_w2, rw_a0=rw_a0, rw_a2=rw_a2, rw_kk=rw_kk, rw_ka=rw_ka, rw_rk=rw_rk, rw_gn_g=rw_gn_g,
             rw_gn_b=rw_gn_b, w_hg_out=w_hg_out, w_rw_out=w_rw_out, w_out=w_out, final_g=final_g)
    m = dict(c_ctx=m_c_ctx, ada_w=m_ada_w, ada_b=m_ada_b, norm_g=m_norm_g, w_in=m_w_in, hg_lb=m_hg_lb, hg_norm_g=m_hg_norm_g,
             rw_mu=m_rw_mu, rw_w0=m_rw_w0, rw_w2=m_rw_w2, rw_a0=m_rw_a0, rw_a2=m_rw_a2, rw_kk=m_rw_kk, rw_ka=m_rw_ka,
             rw_rk=m_rw_rk, rw_gn_g=m_rw_gn_g, rw_gn_b=m_rw_gn_b, w_hg_out=m_w_hg_out, w_rw_out=m_w_rw_out, w_out=m_w_out,
             final_g=m_final_g)
    v = dict(c_ctx=v_c_ctx, ada_w=v_ada_w, ada_b=v_ada_b, norm_g=v_norm_g, w_in=v_w_in, hg_lb=v_hg_lb, hg_norm_g=v_hg_norm_g,
             rw_mu=v_rw_mu, rw_w0=v_rw_w0, rw_w2=v_rw_w2, rw_a0=v_rw_a0, rw_a2=v_rw_a2, rw_kk=v_rw_kk, rw_ka=v_rw_ka,
             rw_rk=v_rw_rk, rw_gn_g=v_rw_gn_g, rw_gn_b=v_rw_gn_b, w_hg_out=v_w_hg_out, w_rw_out=v_w_rw_out, w_out=v_w_out,
             final_g=v_final_g)

    def mat(a):
        return a.reshape(a.shape[-2], a.shape[-1])

    def pack_small(d):
        return _pack([d[n] for n in _SMALL_SHARDED], mult=2 * ROW_ALIGN)

    my_core = lax.axis_index("c").astype(jnp.int32).reshape(1)
    my_chip = (2 * lax.axis_index("x") + lax.axis_index("y")).astype(jnp.int32).reshape(1)

    my_dev = 2 * my_chip[0] + my_core[0]
    dm = x.shape[-1]
    ada_cols = ada_w.shape[-1]

    c_all = _gather_all("cond_gather", c.reshape(1, dm)).reshape(N_DEV, dm)
    cond16 = jnp.concatenate([c_all, c_ctx.reshape(1, dm), jnp.zeros((7, dm), F32)], axis=0)
    (sc16,) = _row_call("cond_silu", lambda i, r, f: ([jax.nn.silu(r[0])], []), 1, 16, [(cond16, 0, dm, 0)], [],
                        [(16, dm, F32, 0)], [])
    mod_here = _mm_nn("mod_mm", sc16, mat(ada_w))
    mod_all = _gather_all("mod_gather", mod_here)
    mod_rows = jnp.concatenate([mod_all[2 * j] for j in range(N_SHARD)], axis=1)
    mine = lax.dynamic_slice_in_dim(mod_rows, my_dev, 1, axis=0)
    mod = _add_small("mod_bias", [jnp.concatenate([mine, mod_rows[N_DEV:N_DEV + 1], jnp.zeros((6, 3 * dm), F32)], axis=0),
                                  jnp.broadcast_to(ada_b, (8, 3 * dm))], (8, 3 * dm))

    small_shapes = [w[n].shape for n in _SMALL_SHARDED]
    big_bf = [_cast_into_slot(f"to_bf16_{n}", mat(w[n]), my_chip) for n in _GATHERED]
    small_mine = pack_small(w)
    small_slots = lax.dynamic_update_slice(jnp.zeros((N_SHARD,) + small_mine.shape, F32), small_mine[None], (my_chip[0], 0, 0))
    gathered = _weights_gather("weights_gather", big_bf, [small_slots])
    w_in_st, w_hg_st, w_rw_st, w_out_st, small_st = gathered
    full_small = {}
    per_chip = [_unpack(small_st[j], small_shapes) for j in range(N_SHARD)]
    for i, n in enumerate(_SMALL_SHARDED):
        full_small[n] = _join_shards(jnp.stack([per_chip[j][i] for j in range(N_SHARD)], axis=0))
    w_out_full = w_out_st.reshape(dm, dm)

    loss_b, grad_x, dmod, g = _local_step(
        x[0], ctx[0], mod, norm_g, w_in_st, full_small["hg_lb"], hg_norm_g, full_small["rw_mu"][0],
        full_small["rw_w0"][0], full_small["rw_w2"][0], full_small["rw_a0"][0], full_small["rw_a2"][0], rw_kk, rw_ka, rw_rk,
        rw_gn_g, rw_gn_b, w_hg_st, w_rw_st, w_out_full, final_g, loss_target[0], my_core)
    loss = lax.psum(loss_b[0, 0], ("x", "y", "c"))

    dmod_all = _gather_all("dmod_gather", dmod[0:2])
    dmod_here = lax.dynamic_slice_in_dim(dmod_all, my_chip[0] * ada_cols, ada_cols, axis=2)
    d_ctx_row = _add_small("d_mod_ctx", [dmod_here[j, 1:2] for j in range(N_DEV)], (1, ada_cols))
    dm16 = jnp.concatenate([dmod_here[:, 0], d_ctx_row, jnp.zeros((7, ada_cols), F32)], axis=0)
    g_ada_here = _mm_tn("d_ada_w", sc16, dm16)
    d_sc16 = _mm_nt("d_cond", dm16, mat(ada_w))

    def cond_bwd(i, r, f):
        _, vjp = jax.vjp(jax.nn.silu, r[0])
        return [vjp(r[1])[0]], []

    (d_cond16,) = _row_call("cond_bwd", cond_bwd, 1, 16, [(cond16, 0, dm, 0), (d_sc16, 0, dm, 0)], [], [(16, dm, F32, 0)], [])
    g["c_ctx"] = jnp.where(my_core[0] == 0, d_cond16[N_DEV], 0.0)
    g["ada_b"] = _add_small("g_ada_b", [dmod[0:1], dmod[1:2]], (1, 3 * dm))

    g_small = {"hg_lb": g["hg_lb"], "rw_mu": g["rw_mu"][None], "rw_w0": g["rw_w0"][None], "rw_w2": g["rw_w2"][None],
               "rw_a0": g["rw_a0"][None], "rw_a2": g["rw_a2"][None]}
    split = {n: _split_shards(g_small[n]) for n in _SMALL_SHARDED}
    small_parts = jnp.stack([pack_small({n: split[n][j] for n in _SMALL_SHARDED}) for j in range(N_SHARD)], axis=0)
    def finish(name, chip_sum, landed):
        half = _sum_landed(f"grads_sum_{name}", landed, chip_sum, my_chip)
        return half, _pair_exchange(f"grads_pair_swap_{name}", half, my_core, False, F32)

    res = {}
    later = {"small": small_parts}
    later_chip = [_pair_exchange(f"grads_pair_sum_{n}", a, my_core, True, BF16) for n, a in later.items()]
    outs, later_landed = _adamw_halves("adamw_w_in", mat(w["w_in"]), mat(m["w_in"]), mat(v["w_in"]),
                                       *finish("w_in", *g["w_in"]), my_core, scatter=tuple(later_chip))
    res["w_in"] = [o.reshape(w["w_in"].shape) for o in outs]
    pending = {n: g[n] for n in ("w_hg_out", "w_rw_out", "w_out")}
    pending.update(zip(later, zip(later_chip, later_landed)))
    rep_shapes = [w[n].shape for n in _REPLICATED]
    rep_all = _gather_all("grads_replicated", _pack([g[n].reshape(w[n].shape) for n in _REPLICATED]))

    outs = _adamw("adamw_ada_w", mat(ada_w), mat(m["ada_w"]), mat(v["ada_w"]), g_ada_here[None])
    res["ada_w"] = [o.reshape(ada_w.shape) for o in outs]
    for n in ("w_hg_out", "w_rw_out", "w_out"):
        outs = _adamw_halves(f"adamw_{n}", mat(w[n]), mat(m[n]), mat(v[n]), *finish(n, *pending[n]), my_core)
        res[n] = [o.reshape(w[n].shape) for o in outs]
    outs = _adamw_halves("adamw_small", small_mine, pack_small(m), pack_small(v), *finish("small", *pending["small"]), my_core)
    for i, vals in enumerate(zip(*[_unpack(o, small_shapes) for o in outs])):
        res[_SMALL_SHARDED[i]] = list(vals)
    outs = _adamw("adamw_replicated", _pack([w[n] for n in _REPLICATED]), _pack([m[n] for n in _REPLICATED]),
                  _pack([v[n] for n in _REPLICATED]), rep_all)
    for i, vals in enumerate(zip(*[_unpack(o, rep_shapes) for o in outs])):
        res[_REPLICATED[i]] = list(vals)

    return (loss, grad_x[None], *[res[n][0] for n in _WEIGHTS], *[res[n][1] for n in _WEIGHTS],
            *[res[n][2] for n in _WEIGHTS], *[res[n][3] for n in _WEIGHTS])
```

```python
import functools

import jax
import jax.numpy as jnp
from jax import lax
from jax.experimental import pallas as pl
from jax.experimental.pallas import tpu as pltpu

HI = lax.Precision.HIGHEST
F32 = jnp.float32
BF16 = jnp.bfloat16

NORM_EPS = 1e-6
HG_HEAD = 128
RW_HEAD = 64
RW_LORA = 64
RW_GN_EPS = 64e-5
GRID_W = 64
SUB = 16
RW_SUB = 16
STEP = 64
RW_STEP = 64
N_SHARD = 4
N_DEV = 8
LANE = 128

ADAM_LR = 0.001
ADAM_B1 = 0.9
ADAM_B2 = 0.999
ADAM_EPS = 1e-08
ADAM_WD = 0.01
ADAM_STEP = 10

VMEM_LIMIT = 56 * 1024 * 1024


def _params(sem=None):
    return pltpu.CompilerParams(dimension_semantics=sem, vmem_limit_bytes=VMEM_LIMIT)


def _tile(n, cands):
    for c in cands:
        if n % c == 0:
            return c
    return n


def _iota2(n, m, d):
    return lax.broadcasted_iota(jnp.int32, (n, m), d)


def _before(n, rev, strict):
    t, s = _iota2(n, n, 0), _iota2(n, n, 1)
    if rev:
        return (s > t) if strict else (s >= t)
    return (s < t) if strict else (s <= t)


def _running_sum(a, axis, rev):
    n = a.shape[axis]
    shift = 1
    while shift < n:
        pad = list(a.shape)
        pad[axis] = shift
        zeros = jnp.zeros(pad, a.dtype)
        if rev:
            moved = jnp.concatenate([lax.slice_in_dim(a, shift, n, axis=axis), zeros], axis=axis)
        else:
            moved = jnp.concatenate([zeros, lax.slice_in_dim(a, 0, n - shift, axis=axis)], axis=axis)
        a = a + moved
        shift *= 2
    return a


def _sdot(a, b, spec):
    return jnp.einsum(spec, a, b, precision=lax.Precision.DEFAULT, preferred_element_type=F32)


def _hg_step(s0, qraw, iin, fin, lb2, rev):
    c, w = qraw.shape
    h = w // HG_HEAD
    nsub = c // SUB
    lb = jax.nn.sigmoid(lb2[0:1] - lb2[1:2])
    q = jax.nn.silu(qraw)
    fg = lb + (1.0 - lb) * jax.nn.sigmoid(fin)
    kk = 1.0 - fg
    g = jnp.log(fg)
    bcum = _running_sum(g, 0, rev)
    def heads(a):
        return jnp.swapaxes(a.reshape(a.shape[0], h, HG_HEAD), 0, 1)

    def unheads(a):
        return jnp.swapaxes(a, 0, 1).reshape(a.shape[1], w)

    blocks = [slice(j * SUB, (j + 1) * SUB) for j in range(nsub)]
    outs = []
    for sl in blocks:
        qs, ks, vs, bc = [a[sl].reshape(SUB, h, HG_HEAD) for a in (q, kk, iin, bcum)]
        o = jnp.zeros((SUB, h, HG_HEAD), F32)
        for si in range(SUB):
            after = slice(0, si + 1) if rev else slice(si, SUB)
            dec = jnp.exp(jnp.minimum(bc[after] - bc[si:si + 1], 0.0))
            a = jnp.sum(qs[after] * ks[si:si + 1] * dec, axis=-1, keepdims=True)
            term = a * vs[si:si + 1]
            n_rest = SUB - 1 - si if rev else si
            if n_rest:
                rest = jnp.zeros((n_rest, h, HG_HEAD), F32)
                term = jnp.concatenate([term, rest] if rev else [rest, term], axis=0)
            o = o + term
        outs.append(o.reshape(SUB, w))
    order = list(range(nsub - 1, -1, -1)) if rev else list(range(nsub))
    for pos in range(1, nsub):
        j, before = order[pos], order[:pos]
        first = (j + 1) * SUB - 1 if rev else j * SUB
        bstart = bcum[first:first + 1] - g[first:first + 1]
        qp = heads(q[blocks[j]] * jnp.exp(bcum[blocks[j]] - bstart))
        kp = heads(jnp.concatenate([kk[blocks[p]] * jnp.exp(bstart - bcum[blocks[p]]) for p in before], axis=0))
        vp = heads(jnp.concatenate([iin[blocks[p]] for p in before], axis=0))
        outs[j] = outs[j] + unheads(_sdot(_sdot(qp, kp, 'htk,hsk->hts'), vp, 'hts,hsv->htv'))
    o_state = unheads(_sdot(heads(q * jnp.exp(bcum)), s0, 'htk,hvk->htv'))
    last = 0 if rev else c - 1
    blast = bcum[last:last + 1]
    s_new = heads(jnp.exp(blast)) * s0 + _sdot(heads(iin), heads(kk * jnp.exp(blast - bcum)), 'hsv,hsk->hvk')
    return jnp.concatenate(outs, axis=0) + o_state, s_new


def _tri_solve(lmat, rhs, rev):
    hh, c, _ = lmat.shape
    sub = RW_SUB
    nb = c // sub
    diag = jnp.concatenate([lmat[:, i * sub:(i + 1) * sub, i * sub:(i + 1) * sub] for i in range(nb)], axis=0)
    dt = jnp.transpose(diag, (1, 2, 0))
    col = lax.broadcasted_iota(jnp.int32, (sub, 1), 0)
    inv_rows = [None] * sub
    order = list(range(sub - 1, -1, -1)) if rev else list(range(sub))
    for pos, t in enumerate(order):
        row = jnp.broadcast_to((col == t).astype(F32), (sub, dt.shape[2]))
        for s in order[:pos]:
            row = row - dt[t, s:s + 1, :] * inv_rows[s]
        inv_rows[t] = row
    tinv = jnp.transpose(jnp.concatenate([r[None] for r in inv_rows], axis=0), (2, 0, 1))
    p = [None] * nb
    done = []
    for i in (range(nb - 1, -1, -1) if rev else range(nb)):
        r = rhs[:, i * sub:(i + 1) * sub]
        if done:
            lrow = jnp.concatenate([lmat[:, i * sub:(i + 1) * sub, m * sub:(m + 1) * sub] for m in done], axis=2)
            r = r - _sdot(lrow, jnp.concatenate([p[m] for m in done], axis=1), 'hts,hsv->htv')
        p[i] = _sdot(tinv[i * hh:(i + 1) * hh], r, 'hts,hsv->htv')
        done.append(i)
    return jnp.concatenate(p, axis=1)


def _rw_step(s0, r, k, v, wlo, alo, w0h, w2h, a0h, a2h, kkh, kah, rev):
    hh, c, _ = r.shape
    tl = jnp.broadcast_to(jnp.tanh(wlo)[None], (hh, c, wlo.shape[1]))
    al = jnp.broadcast_to(alo[None], (hh, c, alo.shape[1]))
    wlog = -jax.nn.softplus(-(w0h + _sdot(tl, w2h, 'hcl,hlj->hcj'))) - 0.5
    lw = -jnp.exp(wlog)
    a = jax.nn.sigmoid(a0h + _sdot(al, a2h, 'hcl,hlj->hcj'))
    kk = k * kkh
    kk = kk * lax.rsqrt(jnp.sum(kk * kk, axis=-1, keepdims=True) + 1e-12)
    kd = k * (1.0 + (a - 1.0) * kah)
    b = kk * a
    cum = _running_sum(lw, 1, rev)
    ecum, encum = jnp.exp(cum), jnp.exp(-cum)
    alpha = jnp.exp(cum - lw) * kk
    beta = b * encum
    kappa = kd * encum
    rho = r * ecum
    m_lt = _before(c, rev, True)[None]
    m_le = _before(c, rev, False)[None]
    ar = jnp.concatenate([alpha, rho], axis=1)
    kb = jnp.concatenate([kappa, beta], axis=1)
    gram = _sdot(ar, kb, 'htk,hsk->hts')
    a_kap = jnp.where(m_lt, gram[:, :c, :c], 0.0)
    a_bet = jnp.where(m_lt, gram[:, :c, c:], 0.0)
    b_kap = jnp.where(m_le, gram[:, c:, :c], 0.0)
    b_bet = jnp.where(m_le, gram[:, c:, c:], 0.0)
    from_state = _sdot(ar, s0, 'htk,hvk->htv')
    p = _tri_solve(a_bet, from_state[:, :c] + _sdot(a_kap, v, 'hts,hsv->htv'), rev)
    vp = jnp.concatenate([v, -p], axis=1)
    y = from_state[:, c:] + _sdot(jnp.concatenate([b_kap, b_bet], axis=2), vp, 'hts,hsv->htv')
    stil = s0 + _sdot(vp, kb, 'hsv,hsk->hvk')
    last = 0 if rev else c - 1
    return y, stil * ecum[:, last:last + 1, :]


def _fn_h(s, norm_g, scale, shift):
    return s * lax.rsqrt(jnp.mean(s * s, axis=-1, keepdims=True) + NORM_EPS) * norm_g * (1.0 + scale) + shift


def _fn_hgpost(of, ob, z, g):
    tm, w = of.shape
    o = (of + ob).reshape(tm, w // HG_HEAD, HG_HEAD)
    o = o * lax.rsqrt(jnp.mean(o * o, axis=-1, keepdims=True) + NORM_EPS)
    return o.reshape(tm, w) * g * jax.nn.silu(z)


def _fn_rwpost(y0, y1, r, k, v, alo, z, a0, a2, k_a, r_k, gn_g, gn_b):
    tm, w = r.shape
    nh = w // RW_HEAD
    asum = 0.0
    for d in range(2):
        asum = asum + jax.nn.sigmoid(a0[d:d + 1] + jnp.dot(alo[:, d * RW_LORA:(d + 1) * RW_LORA], a2[d],
                                                           precision=HI, preferred_element_type=F32))
    k_sum = k * (2.0 + (asum - 2.0) * k_a)
    ys = (y0 + y1).reshape(tm, nh, RW_HEAD)
    mean = jnp.mean(ys, axis=-1, keepdims=True)
    var = jnp.mean(jnp.square(ys - mean), axis=-1, keepdims=True)
    y = ((ys - mean) * lax.rsqrt(var + RW_GN_EPS)).reshape(tm, w) * gn_g + gn_b
    bonus = jnp.sum((r * k_sum * r_k).reshape(tm, nh, RW_HEAD), axis=-1, keepdims=True) * v.reshape(tm, nh, RW_HEAD)
    return (y + bonus.reshape(tm, w)) * jax.nn.silu(z)


def _fn_merge(a, b, ghg, grw):
    return jax.nn.sigmoid(ghg) * a + jax.nn.sigmoid(grw) * b


def _fn_final(xs, o, gate, final_g, tgt):
    x2 = xs + gate * o
    y = x2 * lax.rsqrt(jnp.mean(x2 * x2, axis=-1, keepdims=True) + NORM_EPS) * final_g
    return 0.5 * jnp.sum(jnp.mean(jnp.square(y - tgt), axis=-1))


def _row_call(name, fn, n_tiles, tm, row_ins, full_ins, row_outs, acc_outs):
    n_ri, n_fi, n_ro = len(row_ins), len(full_ins), len(row_outs)

    def body(*refs):
        i = pl.program_id(0)
        rvals = [r[...] for r in refs[:n_ri]]
        fvals = [r[...] for r in refs[n_ri:n_ri + n_fi]]
        outs = refs[n_ri + n_fi:]
        ro, ao = fn(i, rvals, fvals)
        for ref, val in zip(outs[:n_ro], ro):
            ref[...] = val.astype(ref.dtype)
        for ref, val in zip(outs[n_ro:], ao):
            @pl.when(i == 0)
            def _(ref=ref):
                ref[...] = jnp.zeros_like(ref)
            ref[...] += val.astype(ref.dtype)

    def rspec(width, cb, off, rows):
        return pl.BlockSpec((tm, width), lambda i: (jnp.clip(i - off, 0, rows // tm - 1), cb))

    def fspec(shape):
        nd = len(shape)
        return pl.BlockSpec(shape, lambda i: (0,) * nd)

    in_specs = [rspec(w, cb, off, a.shape[0]) for (a, cb, w, off) in row_ins] + [fspec(a.shape) for a in full_ins]
    out_specs = [rspec(w, 0, off, rows) for (rows, w, _, off) in row_outs] + [fspec(s) for (s, _) in acc_outs]
    out_shape = [jax.ShapeDtypeStruct((rows, w), dt) for (rows, w, dt, _) in row_outs] + \
                [jax.ShapeDtypeStruct(s, dt) for (s, dt) in acc_outs]
    res = pl.pallas_call(
        body, name=name, grid=(n_tiles,), in_specs=in_specs, out_specs=out_specs, out_shape=out_shape,
        compiler_params=_params(("arbitrary",)),
    )(*[a for (a, _, _, _) in row_ins], *full_ins)
    return list(res)


def _mm(name, a, b, m, n, k_steps, tm, tn, a_block, a_map, b_block, b_map, o_shape, o_block, o_map,
        contract, out_dtype=F32, scatter=()):
    ns = len(scatter)
    grid = (m // tm, n // tn, k_steps)

    def body(*refs):
        a_ref, b_ref, o_ref, acc_ref = refs[0], refs[1], refs[2 + ns], refs[3 + 2 * ns]
        kk = pl.program_id(2)
        if ns:
            sc_refs = (refs[2:2 + ns], refs[3 + ns:3 + 2 * ns]) + tuple(refs[4 + 2 * ns:])
            at = (pl.program_id(0) * grid[1] + pl.program_id(1)) * grid[2] + kk
            pl.when(at == 0)(lambda: _scatter_start(scatter, *sc_refs))

        @pl.when(kk == 0)
        def _():
            acc_ref[...] = jnp.zeros_like(acc_ref)

        acc_ref[...] += lax.dot_general(a_ref[...].astype(BF16), b_ref[...].astype(BF16),
                                        (contract, ((), ())), preferred_element_type=F32)

        @pl.when(kk == k_steps - 1)
        def _():
            o_ref[...] = acc_ref[...].astype(o_ref.dtype)

        if ns:
            pl.when(at == grid[0] * grid[1] * grid[2] - 1)(lambda: _scatter_wait(scatter, *sc_refs))

    hbm = pl.BlockSpec(memory_space=pl.ANY)
    sems = [pltpu.SemaphoreType.DMA((ns, _PEER_CHIPS))] * 2 if ns else []
    res = pl.pallas_call(
        body, name=name, grid=grid,
        in_specs=[pl.BlockSpec(a_block, a_map), pl.BlockSpec(b_block, b_map)] + [hbm] * ns,
        out_specs=[pl.BlockSpec(o_block, o_map)] + [hbm] * ns,
        out_shape=[jax.ShapeDtypeStruct(o_shape, out_dtype)] + [jax.ShapeDtypeStruct(s.shape, s.dtype) for s in scatter],
        scratch_shapes=[pltpu.VMEM((tm, tn), F32)] + sems,
        compiler_params=_params(("arbitrary",) * 3 if ns else ("parallel", "parallel", "arbitrary")),
    )(a, b, *scatter)
    return (res[0], list(res[1:])) if ns else res[0]


_TM = (768, 512, 256, 128, 64, 32, 16, 8)
_TN = (512, 256, 128)
_TK = (1024, 768, 512, 256, 128)
_TK_WIDE = (768, 512, 256, 128)
WIDE_OUT_BYTES = 32 << 20


def _tm_wide(m, ns):
    for tm in _TM:
        if m % tm == 0 and 3 * 4 * tm * ns <= WIDE_OUT_BYTES:
            return tm
    return m


def _mm_nn(name, a, b, out_dtype=F32):
    m, k = a.shape
    n = b.shape[1]
    tm, tn, tk = _tile(m, _TM), _tile(n, _TN), _tile(k, _TK)
    return _mm(name, a, b, m, n, k // tk, tm, tn, (tm, tk), lambda i, j, s: (i, s), (tk, tn), lambda i, j, s: (s, j),
               (m, n), (tm, tn), lambda i, j, s: (i, j), ((1,), (0,)), out_dtype)


def _mm_nt(name, a, b, out_dtype=F32):
    m, k = a.shape
    n = b.shape[0]
    tm, tn, tk = _tile(m, _TM), _tile(n, _TN), _tile(k, _TK)
    return _mm(name, a, b, m, n, k // tk, tm, tn, (tm, tk), lambda i, j, s: (i, s), (tn, tk), lambda i, j, s: (j, s),
               (m, n), (tm, tn), lambda i, j, s: (i, j), ((1,), (1,)), out_dtype)


def _mm_tn(name, a, b, out_dtype=F32):
    k, m = a.shape
    n = b.shape[1]
    tm, tn, tk = _tile(m, _TM), _tile(n, _TN), _tile(k, _TK)
    return _mm(name, a, b, m, n, k // tk, tm, tn, (tk, tm), lambda i, j, s: (s, i), (tk, tn), lambda i, j, s: (s, j),
               (m, n), (tm, tn), lambda i, j, s: (i, j), ((0,), (0,)), out_dtype)


def _mm_n_st(name, a, bst, out_dtype=F32, joined=False):
    m, k = a.shape
    ns_, _, ns = bst.shape
    tm, tk = _tm_wide(m, ns), _tile(k, (512, 256, 128))
    out = ((m, ns_ * ns), (tm, ns), lambda i, j, s: (i, j)) if joined else \
          ((ns_, m, ns), (None, tm, ns), lambda i, j, s: (j, i, 0))
    return _mm(name, a, bst, m, ns_ * ns, k // tk, tm, ns,
               (tm, tk), lambda i, j, s: (i, s), (None, tk, ns), lambda i, j, s: (j, s, 0), *out, ((1,), (0,)), out_dtype)


def _mm_st_t(name, ast, bst, out_dtype=F32, scatter=()):
    ns_, n, ns = bst.shape
    m = ast.shape[-2]
    tm, tn = _tile(m, _TM), _tile(n, _TN)
    a_side = ((None, tm, ns), lambda i, j, s: (s, i, 0)) if ast.ndim == 3 else ((tm, ns), lambda i, j, s: (i, s))
    return _mm(name, ast, bst, m, n, ns_, tm, tn, *a_side, (None, tn, ns), lambda i, j, s: (s, j, 0),
               (m, n), (tm, tn), lambda i, j, s: (i, j), ((1,), (1,)), out_dtype, scatter)


def _mm_t_st(name, a, bst, out_dtype=F32, scatter=(), n_shard=N_SHARD):
    k, m = a.shape
    ns = bst.shape[-1] if bst.ndim == 3 else bst.shape[-1] // n_shard
    tm, tk = _tile(m, _TN), _tile(k, _TK_WIDE)
    b_side = ((None, tk, ns), lambda i, j, s: (j, s, 0)) if bst.ndim == 3 else ((tk, ns), lambda i, j, s: (s, j))
    return _mm(name, a, bst, m, n_shard * ns, k // tk, tm, ns, (tk, tm), lambda i, j, s: (s, i), *b_side,
               (n_shard, m, ns), (None, tm, ns), lambda i, j, s: (j, i, 0), ((0,), (0,)), out_dtype, scatter)


RELAYOUT_TILE_BYTES = 12 << 20


def _join_columns(name, st):
    ns_, t, ns = st.shape
    tm = _row_tile_for(t, ns_ * ns, budget=RELAYOUT_TILE_BYTES)

    def body(s_ref, o_ref):
        o_ref[...] = jnp.concatenate([s_ref[j] for j in range(ns_)], axis=1)

    return pl.pallas_call(
        body, name=name, grid=(t // tm,), in_specs=[pl.BlockSpec((ns_, tm, ns), lambda i: (0, i, 0))],
        out_specs=pl.BlockSpec((tm, ns_ * ns), lambda i: (i, 0)),
        out_shape=jax.ShapeDtypeStruct((t, ns_ * ns), st.dtype), compiler_params=_params(("parallel",)),
    )(st)


def _split_columns(name, pieces, n_shard):
    t = pieces[0].shape[0]
    n = sum(p.shape[1] for p in pieces)
    ns = n // n_shard
    tm = _row_tile_for(t, n, budget=RELAYOUT_TILE_BYTES)
    npc = len(pieces)

    def body(*refs):
        full = jnp.concatenate([r[...] for r in refs[:npc]], axis=1)
        for j in range(n_shard):
            refs[npc][j] = full[:, j * ns:(j + 1) * ns]

    return pl.pallas_call(
        body, name=name, grid=(t // tm,),
        in_specs=[pl.BlockSpec((tm, p.shape[1]), lambda i: (i, 0)) for p in pieces],
        out_specs=pl.BlockSpec((n_shard, tm, ns), lambda i: (0, i, 0)),
        out_shape=jax.ShapeDtypeStruct((n_shard, t, ns), pieces[0].dtype), compiler_params=_params(("parallel",)),
    )(*pieces)


def _scan_order(j, n_ctx, n_all, rev):
    if not rev:
        return j
    return jnp.where(j < n_ctx, n_ctx - 1 - j, n_all - 1 - (j - n_ctx))


def _hg_scan_fwd(name, p_hg, lb2, d, n_ctx):
    t, w = p_hg.shape[0], lb2.shape[1]
    h = w // HG_HEAD
    n = t // STEP
    rev = d == 1

    def body(q_ref, i_ref, f_ref, lb_ref, o_ref, st_ref, s_ref):
        j = pl.program_id(0)

        @pl.when(j == 0)
        def _():
            s_ref[...] = jnp.zeros_like(s_ref)

        s0 = s_ref[...]
        st_ref[...] = s0
        o, s1 = _hg_step(s0, q_ref[...], i_ref[...], f_ref[...], lb_ref[...], rev)
        o_ref[...] = o
        s_ref[...] = s1

    def rows(cb):
        return pl.BlockSpec((STEP, w), lambda j: (_scan_order(j, n_ctx, n, rev), cb))

    return pl.pallas_call(
        body, name=name, grid=(n,),
        in_specs=[rows(0), rows(1), rows(2 + d), pl.BlockSpec((2, w), lambda j: (0, 0))],
        out_specs=[rows(0), pl.BlockSpec((None, h, HG_HEAD, HG_HEAD), lambda j: (j, 0, 0, 0))],
        out_shape=[jax.ShapeDtypeStruct((t, w), F32), jax.ShapeDtypeStruct((n, h, HG_HEAD, HG_HEAD), F32)],
        scratch_shapes=[pltpu.VMEM((h, HG_HEAD, HG_HEAD), F32)],
        compiler_params=_params(("arbitrary",)),
    )(p_hg, p_hg, p_hg, lb2)


def _hg_scan_bwd(name, p_hg, lb2, states, do, d, n_ctx, other=()):
    t, w = p_hg.shape[0], lb2.shape[1]
    h = w // HG_HEAD
    n = t // STEP
    rev = d == 1
    no = len(other)

    def body(q_ref, i_ref, f_ref, lb_ref, st_ref, do_ref, *refs):
        dq_ref, di_ref, df_ref, dlb_ref, ds_ref = refs[no:]
        step = pl.program_id(0)

        @pl.when(step == 0)
        def _():
            ds_ref[...] = jnp.zeros_like(ds_ref)
            dlb_ref[...] = jnp.zeros_like(dlb_ref)

        _, vjp = jax.vjp(lambda s0, q, i, f, lb: _hg_step(s0, q, i, f, lb, rev),
                         st_ref[...], q_ref[...], i_ref[...], f_ref[...], lb_ref[...])
        ds0, dq, di, df, dlb = vjp((do_ref[...], ds_ref[...]))
        if no:
            dq, di = refs[0][...] + dq, refs[1][...] + di
        dq_ref[...] = dq.astype(dq_ref.dtype)
        di_ref[...] = di.astype(di_ref.dtype)
        df_ref[...] = df.astype(df_ref.dtype)
        dlb_ref[...] += dlb
        ds_ref[...] = ds0

    def rows(cb):
        return pl.BlockSpec((STEP, w), lambda s: (_scan_order(n - 1 - s, n_ctx, n, rev), cb))

    qi = BF16 if no else F32
    return pl.pallas_call(
        body, name=name, grid=(n,),
        in_specs=[rows(0), rows(1), rows(2 + d), pl.BlockSpec((2, w), lambda s: (0, 0)),
                  pl.BlockSpec((None, h, HG_HEAD, HG_HEAD), lambda s: (n - 1 - s, 0, 0, 0)), rows(0)] + [rows(0)] * no,
        out_specs=[rows(0), rows(0), rows(0), pl.BlockSpec((2, w), lambda s: (0, 0))],
        out_shape=[jax.ShapeDtypeStruct((t, w), qi)] * 2 + [jax.ShapeDtypeStruct((t, w), BF16),
                                                            jax.ShapeDtypeStruct((2, w), F32)],
        scratch_shapes=[pltpu.VMEM((h, HG_HEAD, HG_HEAD), F32)],
        compiler_params=_params(("arbitrary",)),
    )(p_hg, p_hg, p_hg, lb2, states, do, *other)


def _to_heads(a, nh):
    return jnp.stack([a[:, i * RW_HEAD:(i + 1) * RW_HEAD] for i in range(nh)], axis=0)


def _from_heads(a):
    return jnp.concatenate([a[i] for i in range(a.shape[0])], axis=-1)


def _rw_scan_fwd(name, sh, hp, d, n_ctx):
    t = sh.shape[0]
    w = (sh.shape[1] - 4 * RW_LORA) // 3
    nh = w // RW_HEAD
    n = t // RW_STEP
    rev = d == 1
    lo = 3 * w // LANE

    def body(r_ref, k_ref, v_ref, wl_ref, al_ref, w0_ref, w2_ref, a0_ref, a2_ref, kk_ref, ka_ref,
             y_ref, st_ref, s_ref):
        j = pl.program_id(0)

        @pl.when(j == 0)
        def _():
            s_ref[...] = jnp.zeros_like(s_ref)

        s0 = s_ref[...]
        st_ref[...] = s0
        wl = wl_ref[...][:, d * RW_LORA:(d + 1) * RW_LORA]
        al = al_ref[...][:, d * RW_LORA:(d + 1) * RW_LORA]
        y, s1 = _rw_step(s0, _to_heads(r_ref[...], nh), _to_heads(k_ref[...], nh), _to_heads(v_ref[...], nh), wl, al,
                         w0_ref[...], w2_ref[...], a0_ref[...], a2_ref[...], kk_ref[...], ka_ref[...], rev)
        y_ref[...] = _from_heads(y)
        s_ref[...] = s1

    def rows(cb, width=w):
        return pl.BlockSpec((RW_STEP, width), lambda j: (_scan_order(j, n_ctx, n, rev), cb))

    def whole(a):
        nd = a.ndim
        return pl.BlockSpec(a.shape, lambda j: (0,) * nd)

    return pl.pallas_call(
        body, name=name, grid=(n,),
        in_specs=[rows(0), rows(1), rows(2), rows(lo, LANE), rows(lo + 1, LANE)] + [whole(a) for a in hp],
        out_specs=[rows(0), pl.BlockSpec((None, nh, RW_HEAD, RW_HEAD), lambda j: (j, 0, 0, 0))],
        out_shape=[jax.ShapeDtypeStruct((t, w), F32), jax.ShapeDtypeStruct((n, nh, RW_HEAD, RW_HEAD), F32)],
        scratch_shapes=[pltpu.VMEM((nh, RW_HEAD, RW_HEAD), F32)],
        compiler_params=_params(("arbitrary",)),
    )(sh, sh, sh, sh, sh, *hp)


def _rw_scan_bwd_both(name, sh, hps, states, dy, n_ctx):
    t = sh.shape[0]
    w = (sh.shape[1] - 4 * RW_LORA) // 3
    nh = w // RW_HEAD
    n = t // RW_STEP
    lo = 3 * w // LANE
    n_in, n_p = 13, 6

    def body(*refs):
        step = pl.program_id(0)
        ins = [refs[d * n_in:(d + 1) * n_in] for d in range(2)]
        outs = [refs[2 * n_in + d * (1 + n_p):2 * n_in + (d + 1) * (1 + n_p)] for d in range(2)]
        ds_refs = refs[2 * n_in + 2 * (1 + n_p):]

        @pl.when(step == 0)
        def _():
            for d in range(2):
                ds_refs[d][...] = jnp.zeros_like(ds_refs[d])
                for ref in outs[d][1:]:
                    ref[...] = jnp.zeros_like(ref)

        for d in range(2):
            r_ref, k_ref, v_ref, wl_ref, al_ref = ins[d][:5]
            hp_refs, st_ref, dy_ref = ins[d][5:11], ins[d][11], ins[d][12]
            wl = wl_ref[...][:, d * RW_LORA:(d + 1) * RW_LORA]
            al = al_ref[...][:, d * RW_LORA:(d + 1) * RW_LORA]
            _, vjp = jax.vjp(functools.partial(_rw_step, rev=d == 1),
                             st_ref[...], _to_heads(r_ref[...], nh), _to_heads(k_ref[...], nh), _to_heads(v_ref[...], nh),
                             wl, al, *[p[...] for p in hp_refs])
            g = vjp((_to_heads(dy_ref[...], nh), ds_refs[d][...]))
            ds_refs[d][...] = g[0]
            zero = jnp.zeros_like(g[4])
            lora = [zero] * 4
            lora[d], lora[2 + d] = g[4], g[5]
            outs[d][0][...] = jnp.concatenate([_from_heads(g[1]), _from_heads(g[2]), _from_heads(g[3])] + lora, axis=-1)
            for ref, val in zip(outs[d][1:], g[6:]):
                ref[...] += val

    def rows(d, cb, width=w):
        return pl.BlockSpec((RW_STEP, width), lambda s: (_scan_order(n - 1 - s, n_ctx, n, d == 1), cb))

    def whole(a):
        nd = a.ndim
        return pl.BlockSpec(a.shape, lambda s: (0,) * nd)

    in_specs, operands, out_specs, out_shape = [], [], [], []
    for d in range(2):
        in_specs += [rows(d, 0), rows(d, 1), rows(d, 2), rows(d, lo, LANE), rows(d, lo + 1, LANE)]
        in_specs += [whole(a) for a in hps[d]]
        in_specs += [pl.BlockSpec((None, nh, RW_HEAD, RW_HEAD), lambda s: (n - 1 - s, 0, 0, 0)), rows(d, 0)]
        operands += [sh] * 5 + list(hps[d]) + [states[d], dy]
        out_specs += [rows(d, 0, sh.shape[1])] + [whole(a) for a in hps[d]]
        out_shape += [jax.ShapeDtypeStruct(sh.shape, F32)] + [jax.ShapeDtypeStruct(a.shape, F32) for a in hps[d]]
    res = pl.pallas_call(
        body, name=name, grid=(n,), in_specs=in_specs, out_specs=out_specs, out_shape=out_shape,
        scratch_shapes=[pltpu.VMEM((nh, RW_HEAD, RW_HEAD), F32)] * 2, compiler_params=_params(("arbitrary",)),
    )(*operands)
    return [res[0], res[1 + n_p]], [res[1:1 + n_p], res[2 + n_p:]]


def _shift_masks(t, n_ctx_rows):
    row = lax.broadcasted_iota(jnp.int32, (t, 1), 0)
    isx = row >= n_ctx_rows
    pos = jnp.where(isx, row - n_ctx_rows, row)
    col = jnp.where(isx, jnp.bitwise_and(pos, GRID_W - 1), pos)
    ncol = jnp.where(isx, GRID_W, n_ctx_rows)
    n_x = t - n_ctx_rows
    ml = col != 0
    mr = col != ncol - 1
    mu = isx & (pos >= GRID_W)
    md = isx & (pos < n_x - GRID_W)
    return ml, mr, mu, md, isx


def _shift_fwd(name, p, col0, mu, n_ctx_rows):
    t, c = p.shape[0], mu.shape[1]
    cw = LANE

    def body(p_ref, mu_ref, o_ref):
        x = p_ref[...]
        m = mu_ref[...]
        ml, mr, mup, mdn, isx = _shift_masks(t, n_ctx_rows)
        left = jnp.where(ml, pltpu.roll(x, 1, 0), 0.0)
        right = jnp.where(mr, pltpu.roll(x, t - 1, 0), 0.0)
        up = jnp.where(mup, pltpu.roll(x, GRID_W, 0), 0.0)
        down = jnp.where(mdn, pltpu.roll(x, t - GRID_W, 0), 0.0)
        out = x + m[0:1] * (left - x) + m[1:2] * (right - x)
        vert = m[2:3] * (up - x) + m[3:4] * (down - x)
        o_ref[...] = out + jnp.where(isx, vert, 0.0)

    return pl.pallas_call(
        body, name=name, grid=(c // cw,),
        in_specs=[pl.BlockSpec((t, cw), lambda j: (0, col0 + j)), pl.BlockSpec((4, cw), lambda j: (0, j))],
        out_specs=pl.BlockSpec((t, cw), lambda j: (0, j)),
        out_shape=jax.ShapeDtypeStruct((t, c), F32),
        compiler_params=_params(("parallel",)),
    )(p, mu)


def _shift_bwd(name, p, col0, mu, dparts, n_ctx_rows):
    t, c = p.shape[0], mu.shape[1]
    cw = LANE
    npart = len(dparts)

    def body(*refs):
        p_ref, mu_ref = refs[0], refs[1]
        dp_ref, dmu_ref = refs[2 + npart], refs[3 + npart]
        x = p_ref[...]
        m = mu_ref[...]
        g = refs[2][...]
        for r in refs[3:2 + npart]:
            g = g + r[...]
        ml, mr, mup, mdn, isx = _shift_masks(t, n_ctx_rows)
        left = jnp.where(ml, pltpu.roll(x, 1, 0), 0.0)
        right = jnp.where(mr, pltpu.roll(x, t - 1, 0), 0.0)
        up = jnp.where(mup, pltpu.roll(x, GRID_W, 0), 0.0)
        down = jnp.where(mdn, pltpu.roll(x, t - GRID_W, 0), 0.0)
        gx = jnp.where(isx, g, 0.0)
        dmu_ref[...] = jnp.concatenate([
            jnp.sum(g * (left - x), axis=0, keepdims=True), jnp.sum(g * (right - x), axis=0, keepdims=True),
            jnp.sum(gx * (up - x), axis=0, keepdims=True), jnp.sum(gx * (down - x), axis=0, keepdims=True)], axis=0)
        coef = 1.0 - m[0:1] - m[1:2] - jnp.where(isx, m[2:3] + m[3:4], 0.0)
        dp = coef * g
        dp = dp + m[0:1] * pltpu.roll(jnp.where(ml, g, 0.0), t - 1, 0)
        dp = dp + m[1:2] * pltpu.roll(jnp.where(mr, g, 0.0), 1, 0)
        dp = dp + m[2:3] * pltpu.roll(jnp.where(mup, g, 0.0), t - GRID_W, 0)
        dp = dp + m[3:4] * pltpu.roll(jnp.where(mdn, g, 0.0), GRID_W, 0)
        dp_ref[...] = dp.astype(dp_ref.dtype)

    col = pl.BlockSpec((t, cw), lambda j: (0, j))
    par = pl.BlockSpec((4, cw), lambda j: (0, j))
    return pl.pallas_call(
        body, name=name, grid=(c // cw,),
        in_specs=[pl.BlockSpec((t, cw), lambda j: (0, col0 + j)), par] + [col] * npart,
        out_specs=[col, par],
        out_shape=[jax.ShapeDtypeStruct((t, c), BF16), jax.ShapeDtypeStruct((4, c), F32)],
        compiler_params=_params(("parallel",)),
    )(p, mu, *dparts)


def _add_small(name, terms, shape):
    flat2 = [a.reshape(-1, a.shape[-1]) for a in terms]
    return _rowwise(name, lambda *v: _slot_sum(list(v)), flat2, F32).reshape(shape)


def _local_step(x, ctx, mod, norm_g, w_in_st, hg_lb, hg_norm_g, rw_mu, rw_w0, rw_w2, rw_a0, rw_a2,
                rw_kk, rw_ka, rw_rk, rw_gn_g, rw_gn_b, w_hg_st, w_rw_st, w_out, final_g, tgt, my_core):
    seq, dm = x.shape
    n_ctx_rows = ctx.shape[0]
    t = seq + n_ctx_rows
    hw = hg_norm_g.shape[-1]
    rw = rw_kk.shape[-1]
    nh_rw = rw // RW_HEAD
    n_ctx = n_ctx_rows // STEP
    tm = _tile(n_ctx_rows, (256, 128, 64))
    nt = t // tm
    nct = n_ctx_rows // tm
    n_sh_cols = 3 * rw + 4 * RW_LORA

    final_g2 = final_g.reshape(1, dm)
    add = _add_small
    mod3 = mod.reshape(8, 3, dm)

    def pick(i, m3):
        r = jnp.where(i < nct, m3[1], m3[0])
        return r[0:1], r[1:2]

    tokens = [(ctx, 0, dm, 0), (x, 0, dm, nct)]

    def h_fn(i, r, f):
        shift, scale = pick(i, f[1])
        return [_fn_h(jnp.where(i < nct, r[0], r[1]), f[0], scale, shift)], []

    (h,) = _row_call("h_fwd", h_fn, nt, tm, tokens, [norm_g, mod3], [(t, dm, BF16, 0)], [])
    proj = _join_columns("proj_join", _mm_n_st("proj_mm", h, w_in_st))
    p_hg = proj
    rs_tile0 = 5 * hw // LANE
    p_zr = proj[:, 5 * hw + n_sh_cols:5 * hw + n_sh_cols + rw]
    p_gt = proj[:, 5 * hw + n_sh_cols + rw:]

    o_hg, st_hg = [], []
    for d in range(2):
        o, st = _hg_scan_fwd(f"hg_scan_fwd{d}", p_hg, hg_lb[d], d, n_ctx)
        o_hg.append(o)
        st_hg.append(st)

    def hgpost_fn(i, r, f):
        return [_fn_hgpost(r[0], r[1], r[2], f[0])], []

    hg_in = [(o_hg[0], 0, hw, 0), (o_hg[1], 0, hw, 0), (p_hg, 4, hw, 0)]
    (y_hg,) = _row_call("hg_post", hgpost_fn, nt, tm, hg_in, [hg_norm_g], [(t, hw, BF16, 0)], [])

    sh = _shift_fwd("rw_shift", proj, rs_tile0, rw_mu, n_ctx_rows)
    hps = []
    for d in range(2):
        hps.append([rw_w0[d].reshape(nh_rw, 1, RW_HEAD), jnp.swapaxes(rw_w2[d].reshape(RW_LORA, nh_rw, RW_HEAD), 0, 1),
                    rw_a0[d].reshape(nh_rw, 1, RW_HEAD), jnp.swapaxes(rw_a2[d].reshape(RW_LORA, nh_rw, RW_HEAD), 0, 1),
                    rw_kk.reshape(nh_rw, 1, RW_HEAD), rw_ka.reshape(nh_rw, 1, RW_HEAD)])
    y_rw_d, st_rw = [], []
    for d in range(2):
        y, st = _rw_scan_fwd(f"rw_scan_fwd{d}", sh, hps[d], d, n_ctx_rows // RW_STEP)
        y_rw_d.append(y)
        st_rw.append(st)

    rw_full = [rw_a0, rw_a2, rw_ka, rw_rk, rw_gn_g, rw_gn_b]
    lo = 3 * rw // LANE
    rw_in = [(y_rw_d[0], 0, rw, 0), (y_rw_d[1], 0, rw, 0), (sh, 0, rw, 0), (sh, 1, rw, 0), (sh, 2, rw, 0),
             (sh, lo + 1, LANE, 0), (p_zr, 0, rw, 0)]

    def rwpost_fn(i, r, f):
        return [_fn_rwpost(*r, *f)], []

    (y_rw,) = _row_call("rw_post", rwpost_fn, nt, tm, rw_in, rw_full, [(t, rw, BF16, 0)], [])

    a_hg = _mm_n_st("hg_out_mm", y_hg, w_hg_st, joined=True)
    a_rw = _mm_n_st("rw_out_mm", y_rw, w_rw_st, joined=True)
    mg_in = [(a_hg, 0, dm, 0), (a_rw, 0, dm, 0), (p_gt, 0, dm, 0), (p_gt, 1, dm, 0)]
    (merged,) = _row_call("merge", lambda i, r, f: ([_fn_merge(*r)], []), nt, tm, mg_in, [], [(t, dm, BF16, 0)], [])
    o_out = _mm_nn("out_mm", merged, w_out)

    def final_fn(i, r, f):
        gate = f[0][0][2:3]
        loss, vjp = jax.vjp(_fn_final, r[0], r[1], gate, f[1], r[2])
        dx, do, dgate, dfg, _ = vjp(jnp.ones((), F32))
        live = i >= nct
        zero = lambda a: jnp.where(live, a, 0.0)
        dmod = jnp.concatenate([jnp.concatenate([jnp.zeros((1, 2 * dm), F32), zero(dgate)], axis=1),
                                jnp.zeros((7, 3 * dm), F32)], axis=0)
        return [zero(dx), zero(do)], [jnp.broadcast_to(zero(loss), (8, LANE)), dmod, zero(dfg)]

    fin_in = [(x, 0, dm, nct), (o_out, 0, dm, 0), (tgt, 0, dm, nct)]
    dx_res, d_o, loss_acc, dmod_gate, d_final_g = _row_call(
        "final", final_fn, nt, tm, fin_in, [mod3, final_g2], [(t, dm, F32, 0), (t, dm, BF16, 0)],
        [((8, LANE), F32), ((8, 3 * dm), F32), ((1, dm), F32)])

    g_w_out = _mm_tn("d_w_out", merged, d_o)
    d_merged = _mm_nt("d_merged", d_o, w_out)

    def merge_bwd(i, r, f):
        _, vjp = jax.vjp(_fn_merge, r[0], r[1], r[2], r[3])
        da, db, dgh, dgr = vjp(r[4])
        return [da, db, jnp.concatenate([dgh, dgr], axis=1)], []

    da_hg, da_rw, dp_gt = _row_call("merge_bwd", merge_bwd, nt, tm, mg_in + [(d_merged, 0, dm, 0)], [],
                                    [(t, dm, BF16, 0), (t, dm, BF16, 0), (t, 2 * dm, BF16, 0)], [])
    g_w_hg_st = _mm_t_st("d_w_hg", y_hg, da_hg)
    g_w_rw_st = _mm_t_st("d_w_rw", y_rw, da_rw)
    dy_hg = _mm_st_t("d_y_hg", da_hg, w_hg_st)
    dy_rw = _mm_st_t("d_y_rw", da_rw, w_rw_st)

    def hgpost_bwd(i, r, f):
        _, vjp = jax.vjp(_fn_hgpost, r[0], r[1], r[2], f[0])
        dof, _, dz, dg = vjp(r[3])
        return [dof, dz], [dg]

    do_hg, dz_hg, g_hg_norm = _row_call("hg_post_bwd", hgpost_bwd, nt, tm, hg_in + [(dy_hg, 0, hw, 0)], [hg_norm_g],
                                        [(t, hw, F32, 0), (t, hw, BF16, 0)], [((1, hw), F32)])
    dq0, di0, df0, dlb0 = _hg_scan_bwd("hg_scan_bwd0", p_hg, hg_lb[0], st_hg[0], do_hg, 0, n_ctx)
    dq, di, df1, dlb1 = _hg_scan_bwd("hg_scan_bwd1", p_hg, hg_lb[1], st_hg[1], do_hg, 1, n_ctx, other=(dq0, di0))
    g_hg_lb = jnp.stack([dlb0, dlb1], axis=0)

    def rwpost_bwd(i, r, f):
        _, vjp = jax.vjp(_fn_rwpost, *r[:7], *f)
        g = vjp(r[7])
        zl = jnp.zeros((g[5].shape[0], 2 * RW_LORA), F32)
        return [g[0], jnp.concatenate([g[2], g[3], g[4], zl, g[5]], axis=1), g[6]], list(g[7:])

    dy_sum, dsh_p, dz_rw, g_a0_p, g_a2_p, g_ka_p, g_rk, g_gn_g, g_gn_b = _row_call(
        "rw_post_bwd", rwpost_bwd, nt, tm, rw_in + [(dy_rw, 0, rw, 0)], rw_full,
        [(t, rw, F32, 0), (t, n_sh_cols, F32, 0), (t, rw, BF16, 0)], [(a.shape, F32) for a in rw_full])
    dsh_dirs, hp_grads = _rw_scan_bwd_both("rw_scan_bwd", sh, hps, st_rw, dy_sum, n_ctx_rows // RW_STEP)
    dp_rs, g_mu = _shift_bwd("rw_shift_bwd", proj, rs_tile0, rw_mu, [dsh_p] + dsh_dirs, n_ctx_rows)

    def flat(a):
        if a.shape[1] == 1:
            return a.reshape(rw)
        return jnp.swapaxes(a, 0, 1).reshape(RW_LORA, rw)

    g_w0 = jnp.stack([flat(hp_grads[d][0]) for d in range(2)], axis=0)
    g_w2 = jnp.stack([flat(hp_grads[d][1]) for d in range(2)], axis=0)
    g_a0 = add("g_a0", [jnp.stack([flat(hp_grads[d][2]) for d in range(2)], axis=0), g_a0_p], (2, rw))
    g_a2 = add("g_a2", [jnp.stack([flat(hp_grads[d][3]) for d in range(2)], axis=0), g_a2_p], (2, RW_LORA, rw))
    g_kk = add("g_kk", [flat(hp_grads[0][4]).reshape(1, rw), flat(hp_grads[1][4]).reshape(1, rw)], (1, rw))
    g_ka = add("g_ka", [flat(hp_grads[0][5]).reshape(1, rw), flat(hp_grads[1][5]).reshape(1, rw), g_ka_p], (1, rw))

    dproj_st = _split_columns("dproj_split", [dq, di, df0, df1, dz_hg, dp_rs, dz_rw, dp_gt], N_SHARD)
    early = {"w_hg_out": g_w_hg_st, "w_rw_out": g_w_rw_st, "w_out": g_w_out.reshape(N_SHARD, dm // N_SHARD, dm)}
    early_chip = [_pair_exchange(f"grads_pair_sum_{n}", a, my_core, True, BF16) for n, a in early.items()]
    g_w_in_st, early_landed = _mm_t_st("d_w_in", h, dproj_st, scatter=tuple(early_chip))
    w_in_chip = _pair_exchange("grads_pair_sum_w_in", g_w_in_st, my_core, True, BF16)
    dh, (w_in_landed,) = _mm_st_t("d_h", dproj_st, w_in_st, scatter=(w_in_chip,))

    def h_bwd(i, r, f):
        shift, scale = pick(i, f[1])
        is_ctx = i < nct
        _, vjp = jax.vjp(_fn_h, jnp.where(is_ctx, r[0], r[1]), f[0], scale, shift)
        ds, dg, dscale, dshift = vjp(r[2])
        row = jnp.concatenate([dshift, dscale, jnp.zeros((1, dm), F32)], axis=1)
        z = jnp.zeros_like(row)
        dmod = jnp.concatenate([jnp.where(is_ctx, z, row), jnp.where(is_ctx, row, z), jnp.zeros((6, 3 * dm), F32)], axis=0)
        return [ds + r[3]], [dg, dmod]

    grad_x, g_norm_g, dmod_h = _row_call(
        "h_bwd", h_bwd, nt, tm, tokens + [(dh, 0, dm, 0), (dx_res, 0, dm, 0)], [norm_g, mod3],
        [(seq, dm, F32, nct)], [((1, dm), F32), ((8, 3 * dm), F32)])
    dmod = add("d_mod", [dmod_h, dmod_gate], (8, 3 * dm))
    grads = dict(
        norm_g=g_norm_g, w_in=(w_in_chip, w_in_landed), hg_lb=g_hg_lb,
        hg_norm_g=g_hg_norm, rw_mu=g_mu, rw_w0=g_w0, rw_w2=g_w2, rw_a0=g_a0, rw_a2=g_a2, rw_kk=g_kk, rw_ka=g_ka,
        rw_rk=g_rk, rw_gn_g=g_gn_g, rw_gn_b=g_gn_b, final_g=d_final_g.reshape(dm))
    grads.update(zip(early, zip(early_chip, early_landed)))
    return loss_acc[0:1, 0:1], grad_x, dmod, grads


def _my_place():
    return lax.axis_index("x"), lax.axis_index("y"), lax.axis_index("c")


MIN_CHUNK_BYTES = 1 << 18
ROW_ALIGN = 16


def _n_chunks(rows, row_bytes):
    for n in (8, 4, 2):
        if rows % (n * ROW_ALIGN) == 0 and rows // n * row_bytes >= MIN_CHUNK_BYTES:
            return n
    return 1


def _row_bytes(a, lead=1):
    n = a.dtype.itemsize
    for d in a.shape[lead:]:
        n *= d
    return n


def _rows(ref, start, size):
    return ref.at[pl.ds(start, size)]


def _chunked(make, start, size, n):
    cs = size // n
    return [make(start + j * cs, cs) for j in range(n)]


_PEER_CHIPS = 3


def _weights_gather(name, big, small):
    nb, na = len(big), len(big) + len(small)
    arrays = list(big) + list(small)
    n_ici = 6

    def body(*refs):
        outs = refs[na:2 * na]
        send_sems, recv_sems, fsend_sems, frecv_sems = refs[2 * na:]
        x, y, c = _my_place()
        me, sx, sy, sd = 2 * x + y, 2 * (1 - x) + y, 2 * x + (1 - y), 2 * (1 - x) + (1 - y)
        kx, ky, kd = (1 - x, y, c), (x, 1 - y, c), (1 - x, 1 - y, c)

        def ici(a, j, src_slot, dst_slot, to, r0, nr):
            return pltpu.make_async_remote_copy(
                src_ref=_rows(outs[a].at[src_slot], r0, nr), dst_ref=_rows(outs[a].at[dst_slot], r0, nr),
                send_sem=send_sems.at[a, j], recv_sem=recv_sems.at[a, j], device_id=to,
                device_id_type=pl.DeviceIdType.MESH)

        def to_sibling(a, k, slot, r0, nr):
            rows = _rows(outs[a].at[slot], r0, nr)
            return pltpu.make_async_remote_copy(
                src_ref=rows, dst_ref=rows, send_sem=fsend_sems.at[a, k], recv_sem=frecv_sems.at[a, k],
                device_id=(x, y, 1 - c), device_id_type=pl.DeviceIdType.MESH)

        def start(copies):
            for cp in copies:
                cp.start()

        geo = []
        for a in range(nb):
            half = arrays[a].shape[1] // 2
            geo.append((pl.multiple_of(c * half, ROW_ALIGN), pl.multiple_of((1 - c) * half, ROW_ALIGN), half // 2,
                        _n_chunks(half // 2, _row_bytes(arrays[a], 2))))
        plan = [(me, sx, kx, 0), (me, sx, kx, 1), (me, sy, ky, 0), (me, sy, ky, 1), (sx, sd, ky, 0), (sy, sd, kx, 1)]

        def piece(a, j):
            return geo[a][0] + plan[j][3] * geo[a][2]

        for a in range(nb):
            for j in range(4):
                start(_chunked(lambda r0, cs: ici(a, j, me, me, plan[j][2], r0, cs), piece(a, j), geo[a][2], geo[a][3]))
        for a in range(nb, na):
            rows = arrays[a].shape[1]
            for j, to in ((0, kx), (2, ky), (1, kd)):
                ici(a, j, me, me, to, 0, rows).start()
        for a in range(nb):
            for j, first in ((4, 0), (5, 3)):
                src_slot, _, to, _ = plan[j]
                ici(a, first, me, plan[first][1], plan[first][2], piece(a, first), geo[a][2]).wait_recv()
                start(_chunked(lambda r0, cs: ici(a, j, src_slot, src_slot, to, r0, cs), piece(a, j), geo[a][2], geo[a][3]))
        for a in range(nb):
            for j in (1, 2):
                ici(a, j, me, plan[j][1], plan[j][2], piece(a, j), geo[a][2]).wait_recv()
            for k, slot in ((0, sx), (1, sy)):
                start(_chunked(lambda r0, cs: to_sibling(a, k, slot, r0, cs), geo[a][0], 2 * geo[a][2], geo[a][3]))
        for a in range(nb):
            for j in (4, 5):
                ici(a, j, me, sd, plan[j][2], piece(a, j), geo[a][2]).wait_recv()
            start(_chunked(lambda r0, cs: to_sibling(a, 2, sd, r0, cs), geo[a][0], 2 * geo[a][2], geo[a][3]))
        for a in range(nb, na):
            rows = arrays[a].shape[1]
            for j, slot, to in ((0, sx, kx), (2, sy, ky), (1, sd, kd)):
                ici(a, j, me, slot, to, 0, rows).wait_recv()
        for a in range(nb):
            for k, slot in ((0, sx), (1, sy), (2, sd)):
                to_sibling(a, k, slot, geo[a][1], 2 * geo[a][2]).wait_recv()
        for a in range(nb):
            for j in range(n_ici):
                ici(a, j, me, me, plan[j][2], piece(a, j), geo[a][2]).wait_send()
            for k, slot in ((0, sx), (1, sy), (2, sd)):
                to_sibling(a, k, slot, geo[a][0], 2 * geo[a][2]).wait_send()
        for a in range(nb, na):
            rows = arrays[a].shape[1]
            for j, to in ((0, kx), (2, ky), (1, kd)):
                ici(a, j, me, me, to, 0, rows).wait_send()

    hbm = pl.BlockSpec(memory_space=pl.ANY)
    ici_sems = pltpu.SemaphoreType.DMA((na, n_ici))
    pair_sems = pltpu.SemaphoreType.DMA((na, _PEER_CHIPS))
    return pl.pallas_call(
        body, name=name, in_specs=[hbm] * na, out_specs=[hbm] * na,
        out_shape=[jax.ShapeDtypeStruct(a.shape, a.dtype) for a in arrays],
        input_output_aliases={a: a for a in range(na)}, scratch_shapes=[ici_sems, ici_sems, pair_sems, pair_sems],
    )(*arrays)


def _scatter_copy(arrays, ins, outs, send_sems, recv_sems, a, k, slot, r0, nr):
    x, y, c = _my_place()
    px, py = [(1 - x, y), (x, 1 - y), (1 - x, 1 - y)][k]
    return pltpu.make_async_remote_copy(
        src_ref=_rows(ins[a].at[2 * px + py], r0, nr), dst_ref=_rows(outs[a].at[slot], r0, nr),
        send_sem=send_sems.at[a, k], recv_sem=recv_sems.at[a, k], device_id=(px, py, c),
        device_id_type=pl.DeviceIdType.MESH)


def _scatter_start(arrays, ins, outs, send_sems, recv_sems):
    x, y, _ = _my_place()
    for a in range(len(arrays)):
        rows = arrays[a].shape[1]
        for k in range(_PEER_CHIPS):
            for cp in _chunked(lambda r0, cs: _scatter_copy(arrays, ins, outs, send_sems, recv_sems, a, k, 2 * x + y, r0, cs),
                               0, rows, _n_chunks(rows, _row_bytes(arrays[a], 2))):
                cp.start()


def _scatter_wait(arrays, ins, outs, send_sems, recv_sems):
    x, y, _ = _my_place()
    peer_slot = [2 * (1 - x) + y, 2 * x + (1 - y), 2 * (1 - x) + (1 - y)]
    for k in range(_PEER_CHIPS):
        for a in range(len(arrays)):
            _scatter_copy(arrays, ins, outs, send_sems, recv_sems, a, k, peer_slot[k], 0, arrays[a].shape[1]).wait_recv()
    for a in range(len(arrays)):
        for k in range(_PEER_CHIPS):
            _scatter_copy(arrays, ins, outs, send_sems, recv_sems, a, k, 2 * x + y, 0, arrays[a].shape[1]).wait_send()


PAIR_TILE_BYTES = 4 << 20


def _pair_exchange(name, a, place, reduce, out_dtype):
    rows, cols = a.shape[-2], a.shape[-1]
    half = rows // 2 if reduce else rows
    tr = _row_tile_for(half, cols, budget=PAIR_TILE_BYTES)
    nh = half // tr
    n_steps = (N_SHARD if reduce else 1) * nh

    def body(pc_ref, *refs):
        if reduce:
            keep_ref, send_ref, o_ref, land, send_sems, recv_sems, credit, wire = refs
            wire[...] = send_ref[...].astype(BF16)
            src = wire
        else:
            send_ref, o_ref, land, send_sems, recv_sems, credit = refs
            src = send_ref
        x, y, c = _my_place()
        other = (x, y, 1 - c)
        t = pl.program_id(0) * nh + pl.program_id(1) if reduce else pl.program_id(0)
        slot = t % 2

        @pl.when(t >= 2)
        def _():
            pl.semaphore_wait(credit, 1)

        copy = pltpu.make_async_remote_copy(
            src_ref=src, dst_ref=land.at[slot], send_sem=send_sems.at[slot], recv_sem=recv_sems.at[slot],
            device_id=other, device_id_type=pl.DeviceIdType.MESH)
        copy.start()
        copy.wait_recv()
        got = land[slot]
        o_ref[...] = ((keep_ref[...] + got.astype(F32)) if reduce else got).astype(out_dtype)
        copy.wait_send()

        @pl.when(t < n_steps - 2)
        def _():
            pl.semaphore_signal(credit, inc=1, device_id=other, device_id_type=pl.DeviceIdType.MESH)

    if reduce:
        grid = (N_SHARD, nh)
        in_specs = [pl.BlockSpec((None, tr, cols), lambda j, i, pc: (j, pc[0] * nh + i, 0)),
                    pl.BlockSpec((None, tr, cols), lambda j, i, pc: (j, (1 - pc[0]) * nh + i, 0))]
        out_spec = pl.BlockSpec((None, tr, cols), lambda j, i, pc: (j, i, 0))
        out_shape = jax.ShapeDtypeStruct((N_SHARD, half, cols), out_dtype)
        operands = (a, a)
        sem = ("arbitrary", "arbitrary")
    else:
        grid = (nh,)
        in_specs = [pl.BlockSpec((tr, cols), lambda i, pc: (i, 0))]
        out_spec = pl.BlockSpec((tr, cols), lambda i, pc: (i, 0))
        out_shape = jax.ShapeDtypeStruct((half, cols), out_dtype)
        operands = (a,)
        sem = ("arbitrary",)
    return pl.pallas_call(
        body, name=name,
        grid_spec=pltpu.PrefetchScalarGridSpec(
            num_scalar_prefetch=1, grid=grid, in_specs=in_specs, out_specs=out_spec,
            scratch_shapes=[pltpu.VMEM((2, tr, cols), BF16 if reduce else a.dtype), pltpu.SemaphoreType.DMA((2,)),
                            pltpu.SemaphoreType.DMA((2,)), pltpu.SemaphoreType.REGULAR] +
                           ([pltpu.VMEM((tr, cols), BF16)] if reduce else [])),
        out_shape=out_shape, compiler_params=_params(sem),
    )(place, *operands)


def _cast_into_slot(name, a, chip):
    rows, cols = a.shape
    tm = _row_tile_for(rows, cols)

    def body(pc_ref, a_ref, o_ref):
        o_ref[...] = a_ref[...].astype(BF16)

    return pl.pallas_call(
        body, name=name,
        grid_spec=pltpu.PrefetchScalarGridSpec(
            num_scalar_prefetch=1, grid=(rows // tm,), in_specs=[pl.BlockSpec((tm, cols), lambda i, pc: (i, 0))],
            out_specs=pl.BlockSpec((None, tm, cols), lambda i, pc: (pc[0], i, 0))),
        out_shape=jax.ShapeDtypeStruct((N_SHARD, rows, cols), BF16), compiler_params=_params(("parallel",)),
    )(chip, a)


def _sum_landed(name, landed, sent, chip):
    ns, rows, cols = landed.shape
    tm = _row_tile_for(rows, cols)

    def body(pc_ref, *refs):
        own_ref, o_ref = refs[ns], refs[ns + 1]
        me = pc_ref[0]
        terms = [jnp.where(me == j, own_ref[...], refs[j][...]).astype(F32) for j in range(ns)]
        o_ref[...] = _slot_sum(terms)

    def landed_spec(j):
        return pl.BlockSpec((None, tm, cols), lambda i, pc: (jnp.where(pc[0] == j, (j + 1) % ns, j), i, 0))

    return pl.pallas_call(
        body, name=name,
        grid_spec=pltpu.PrefetchScalarGridSpec(
            num_scalar_prefetch=1, grid=(rows // tm,),
            in_specs=[landed_spec(j) for j in range(ns)] + [pl.BlockSpec((None, tm, cols), lambda i, pc: (pc[0], i, 0))],
            out_specs=pl.BlockSpec((tm, cols), lambda i, pc: (i, 0))),
        out_shape=jax.ShapeDtypeStruct((rows, cols), F32), compiler_params=_params(("parallel",)),
    )(chip, *([landed] * ns), sent)


def _gather_all(name, a):
    def body(in_ref, out_ref, send_sems, recv_sems, local_sem):
        x, y, c = _my_place()
        me = 4 * x + 2 * y + c

        def peer(k):
            return (x ^ (k >> 2), y ^ ((k >> 1) & 1), c ^ (k & 1))

        def remote(k, land):
            return pltpu.make_async_remote_copy(
                src_ref=in_ref, dst_ref=out_ref.at[land], send_sem=send_sems.at[k - 1], recv_sem=recv_sems.at[k - 1],
                device_id=peer(k), device_id_type=pl.DeviceIdType.MESH)

        local = pltpu.make_async_copy(in_ref, out_ref.at[me], local_sem)
        local.start()
        for k in range(1, N_DEV):
            remote(k, me).start()
        for k in range(1, N_DEV):
            px, py, pc = peer(k)
            remote(k, 4 * px + 2 * py + pc).wait_recv()
        for k in range(1, N_DEV):
            remote(k, me).wait_send()
        local.wait()

    hbm = pl.BlockSpec(memory_space=pl.ANY)
    return pl.pallas_call(
        body, name=name, in_specs=[hbm], out_specs=hbm,
        out_shape=jax.ShapeDtypeStruct((N_DEV,) + a.shape, a.dtype),
        scratch_shapes=[pltpu.SemaphoreType.DMA((N_DEV - 1,)), pltpu.SemaphoreType.DMA((N_DEV - 1,)), pltpu.SemaphoreType.DMA],
    )(a)


def _row_tile_for(rows, cols, budget=1 << 20):
    if rows * cols * 4 <= budget:
        return rows
    for tm in (1024, 512, 256, 128, 64, 32, 16, 8):
        if rows % tm == 0 and tm * cols * 4 <= budget:
            return tm
    return rows


def _slot_sum(vals):
    g = vals[0]
    for v in vals[1:]:
        g = g + v
    return g


def _rowwise(name, fn, arrays, out_dtype):
    rows, cols = arrays[0].shape
    tm = _row_tile_for(rows, cols)

    def body(*refs):
        refs[-1][...] = fn(*[r[...] for r in refs[:-1]]).astype(out_dtype)

    blk = pl.BlockSpec((tm, cols), lambda i: (i, 0))
    return pl.pallas_call(
        body, name=name, grid=(rows // tm,), in_specs=[blk] * len(arrays), out_specs=blk,
        out_shape=jax.ShapeDtypeStruct((rows, cols), out_dtype), compiler_params=_params(("parallel",)),
    )(*arrays)


def _sum_slots(name, st):
    ns, rows, cols = st.shape
    tm = _row_tile_for(rows, cols)

    def body(s_ref, o_ref):
        o_ref[...] = _slot_sum([s_ref[j].astype(F32) for j in range(ns)])

    return pl.pallas_call(
        body, name=name, grid=(rows // tm,),
        in_specs=[pl.BlockSpec((ns, tm, cols), lambda i: (0, i, 0))],
        out_specs=pl.BlockSpec((tm, cols), lambda i: (i, 0)),
        out_shape=jax.ShapeDtypeStruct((rows, cols), F32),
        compiler_params=_params(("parallel",)),
    )(st)


ADAM_TILE_BYTES = 1 << 20


def _adam_update(g, p_ref, m_ref, v_ref, go_ref, d_ref, mo_ref, vo_ref):
    mn = ADAM_B1 * m_ref[...] + (1.0 - ADAM_B1) * g
    vn = ADAM_B2 * v_ref[...] + (1.0 - ADAM_B2) * jnp.square(g)
    m_hat = mn / (1.0 - ADAM_B1 ** ADAM_STEP)
    v_hat = vn / (1.0 - ADAM_B2 ** ADAM_STEP)
    go_ref[...] = g
    d_ref[...] = -ADAM_LR * (m_hat / (jnp.sqrt(v_hat) + ADAM_EPS) + ADAM_WD * p_ref[...])
    mo_ref[...] = mn
    vo_ref[...] = vn


def _adamw(name, p, m, v, gst):
    rows, cols = p.shape
    ns = gst.shape[0]
    tm = _row_tile_for(rows, cols, budget=ADAM_TILE_BYTES)

    def body(p_ref, m_ref, v_ref, g_ref, *outs):
        _adam_update(_slot_sum([g_ref[j] for j in range(ns)]), p_ref, m_ref, v_ref, *outs)

    blk = pl.BlockSpec((tm, cols), lambda i: (i, 0))
    return pl.pallas_call(
        body, name=name, grid=(rows // tm,),
        in_specs=[blk, blk, blk, pl.BlockSpec((ns, tm, cols), lambda i: (0, i, 0))],
        out_specs=[blk] * 4, out_shape=[jax.ShapeDtypeStruct((rows, cols), F32)] * 4,
        compiler_params=_params(("parallel",)),
    )(p, m, v, gst)


def _adamw_halves(name, p, m, v, mine, theirs, place, scatter=()):
    rows, cols = p.shape
    half = rows // 2
    tm = _row_tile_for(half, cols, budget=ADAM_TILE_BYTES)
    nh = half // tm
    ns = len(scatter)

    def body(pc_ref, p_ref, m_ref, v_ref, mine_ref, theirs_ref, *refs):
        if ns:
            sc_refs = (refs[:ns], refs[ns + 4:2 * ns + 4]) + tuple(refs[2 * ns + 4:])
            at = pl.program_id(0) * nh + pl.program_id(1)
            pl.when(at == 0)(lambda: _scatter_start(scatter, *sc_refs))
        g = jnp.where(pl.program_id(0) == pc_ref[0], mine_ref[...], theirs_ref[...])
        _adam_update(g, p_ref, m_ref, v_ref, *refs[ns:ns + 4])
        if ns:
            pl.when(at == 2 * nh - 1)(lambda: _scatter_wait(scatter, *sc_refs))

    blk = pl.BlockSpec((tm, cols), lambda h, i, pc: (h * nh + i, 0))
    hblk = pl.BlockSpec((tm, cols), lambda h, i, pc: (i, 0))
    hbm = pl.BlockSpec(memory_space=pl.ANY)
    res = pl.pallas_call(
        body, name=name,
        grid_spec=pltpu.PrefetchScalarGridSpec(
            num_scalar_prefetch=1, grid=(2, nh), in_specs=[blk, blk, blk, hblk, hblk] + [hbm] * ns,
            out_specs=[blk] * 4 + [hbm] * ns,
            scratch_shapes=[pltpu.SemaphoreType.DMA((ns, _PEER_CHIPS))] * 2 if ns else []),
        out_shape=[jax.ShapeDtypeStruct((rows, cols), F32)] * 4 + [jax.ShapeDtypeStruct(s.shape, s.dtype) for s in scatter],
        compiler_params=_params(("arbitrary", "arbitrary") if ns else ("parallel", "parallel")),
    )(place, p, m, v, mine, theirs, *scatter)
    return (list(res[:4]), list(res[4:])) if ns else res


def _pack(parts, width=LANE, mult=8):
    flat = jnp.concatenate([a.reshape(-1) for a in parts])
    n = flat.shape[0]
    per = width * mult
    total = -(-n // per) * per
    return jnp.pad(flat, (0, total - n)).reshape(total // width, width)


def _unpack(packed, shapes):
    flat = packed.reshape(-1)
    out, off = [], 0
    for s in shapes:
        n = 1
        for d in s:
            n *= d
        out.append(flat[off:off + n].reshape(s))
        off += n
    return out


_SMALL_SHARDED = ("hg_lb", "rw_mu", "rw_w0", "rw_w2", "rw_a0", "rw_a2")
_REPLICATED = ("c_ctx", "ada_b", "norm_g", "hg_norm_g", "rw_kk", "rw_ka", "rw_rk", "rw_gn_g", "rw_gn_b", "final_g")
_GATHERED = ("w_in", "w_hg_out", "w_rw_out", "w_out")
_WEIGHTS = ("c_ctx", "ada_w", "ada_b", "norm_g", "w_in", "hg_lb", "hg_norm_g", "rw_mu", "rw_w0", "rw_w2", "rw_a0", "rw_a2",
            "rw_kk", "rw_ka", "rw_rk", "rw_gn_g", "rw_gn_b", "w_hg_out", "w_rw_out", "w_out", "final_g")


def _join_shards(st):
    a = jnp.moveaxis(st, 0, -2)
    return a.reshape(a.shape[:-2] + (a.shape[-2] * a.shape[-1],))


def _split_shards(a):
    s = a.reshape(a.shape[:-1] + (N_SHARD, a.shape[-1] // N_SHARD))
    return jnp.moveaxis(s, -2, 0)


def kernel(x, c, ctx, c_ctx, ada_w, ada_b, norm_g, w_in, hg_lb, hg_norm_g, rw_mu, rw_w0, rw_w2, rw_a0, rw_a2, rw_kk, rw_ka, rw_rk, rw_gn_g, rw_gn_b, w_hg_out, w_rw_out, w_out, final_g, loss_target, m_c_ctx, m_ada_w, m_ada_b, m_norm_g, m_w_in, m_hg_lb, m_hg_norm_g, m_rw_mu, m_rw_w0, m_rw_w2, m_rw_a0, m_rw_a2, m_rw_kk, m_rw_ka, m_rw_rk, m_rw_gn_g, m_rw_gn_b, m_w_hg_out, m_w_rw_out, m_w_out, m_final_g, v_c_ctx, v_ada_w, v_ada_b, v_norm_g, v_w_in, v_hg_lb, v_hg_norm_g, v_rw_mu, v_rw_w0, v_rw_w2, v_rw_a0, v_rw_a2, v_rw_kk, v_rw_ka, v_rw_rk, v_rw_gn_g, v_rw_gn_b, v_w_hg_out, v_w_rw_out, v_w_out, v_final_g):
    w = dict(c_ctx=c_ctx, ada_w=ada_w, ada_b=ada_b, norm_g=norm_g, w_in=w_in, hg_lb=hg_lb, hg_norm_g=hg_norm_g, rw_mu=rw_mu,
             rw_w0=rw_w0, rw_w2=rw_w2, rw_a0=rw_a0, rw_a2=rw_a2, rw_kk=rw_kk, rw_ka=rw_ka, rw_rk=rw_rk, rw_gn_g=rw_gn_g,
             rw_gn_b=rw_gn_b, w_hg_out=w_hg_out, w_rw_out=w_rw_out, w_out=w_out, final_g=final_g)
    m = dict(c_ctx=m_c_ctx, ada_w=m_ada_w, ada_b=m_ada_b, norm_g=m_norm_g, w_in=m_w_in, hg_lb=m_hg_lb, hg_norm_g=m_hg_norm_g,
             rw_mu=m_rw_mu, rw_w0=m_rw_w0, rw_w2=m_rw_w2, rw_a0=m_rw_a0, rw_a2=m_rw_a2, rw_kk=m_rw_kk, rw_ka=m_rw_ka,
             rw_rk=m_rw_rk, rw_gn_g=m_rw_gn_g, rw_gn_b=m_rw_gn_b, w_hg_out=m_w_hg_out, w_rw_out=m_w_rw_out, w_out=m_w_out,
             final_g=m_final_g)
    v = dict(c_ctx=v_c_ctx, ada_w=v_ada_w, ada_b=v_ada_b, norm_g=v_norm_g, w_in=v_w_in, hg_lb=v_hg_lb, hg_norm_g=v_hg_norm_g,
             rw_mu=v_rw_mu, rw_w0=v_rw_w0, rw_w2=v_rw_w2, rw_a0=v_rw_a0, rw_a2=v_rw_a2, rw_kk=v_rw_kk, rw_ka=v_rw_ka,
             rw_rk=v_rw_rk, rw_gn_g=v_rw_gn_g, rw_gn_b=v_rw_gn_b, w_hg_out=v_w_hg_out, w_rw_out=v_w_rw_out, w_out=v_w_out,
             final_g=v_final_g)

    def mat(a):
        return a.reshape(a.shape[-2], a.shape[-1])

    def pack_small(d):
        return _pack([d[n] for n in _SMALL_SHARDED], mult=2 * ROW_ALIGN)

    my_core = lax.axis_index("c").astype(jnp.int32).reshape(1)
    my_chip = (2 * lax.axis_index("x") + lax.axis_index("y")).astype(jnp.int32).reshape(1)

    my_dev = 2 * my_chip[0] + my_core[0]
    dm = x.shape[-1]
    ada_cols = ada_w.shape[-1]

    c_all = _gather_all("cond_gather", c.reshape(1, dm)).reshape(N_DEV, dm)
    cond16 = jnp.concatenate([c_all, c_ctx.reshape(1, dm), jnp.zeros((7, dm), F32)], axis=0)
    (sc16,) = _row_call("cond_silu", lambda i, r, f: ([jax.nn.silu(r[0])], []), 1, 16, [(cond16, 0, dm, 0)], [],
                        [(16, dm, F32, 0)], [])
    mod_here = _mm_nn("mod_mm", sc16, mat(ada_w))
    mod_all = _gather_all("mod_gather", mod_here)
    mod_rows = jnp.concatenate([mod_all[2 * j] for j in range(N_SHARD)], axis=1)
    mine = lax.dynamic_slice_in_dim(mod_rows, my_dev, 1, axis=0)
    mod = _add_small("mod_bias", [jnp.concatenate([mine, mod_rows[N_DEV:N_DEV + 1], jnp.zeros((6, 3 * dm), F32)], axis=0),
                                  jnp.broadcast_to(ada_b, (8, 3 * dm))], (8, 3 * dm))

    small_shapes = [w[n].shape for n in _SMALL_SHARDED]
    big_bf = [_cast_into_slot(f"to_bf16_{n}", mat(w[n]), my_chip) for n in _GATHERED]
    small_mine = pack_small(w)
    small_slots = lax.dynamic_update_slice(jnp.zeros((N_SHARD,) + small_mine.shape, F32), small_mine[None], (my_chip[0], 0, 0))
    gathered = _weights_gather("weights_gather", big_bf, [small_slots])
    w_in_st, w_hg_st, w_rw_st, w_out_st, small_st = gathered
    full_small = {}
    per_chip = [_unpack(small_st[j], small_shapes) for j in range(N_SHARD)]
    for i, n in enumerate(_SMALL_SHARDED):
        full_small[n] = _join_shards(jnp.stack([per_chip[j][i] for j in range(N_SHARD)], axis=0))
    w_out_full = w_out_st.reshape(dm, dm)

    loss_b, grad_x, dmod, g = _local_step(
        x[0], ctx[0], mod, norm_g, w_in_st, full_small["hg_lb"], hg_norm_g, full_small["rw_mu"][0],
        full_small["rw_w0"][0], full_small["rw_w2"][0], full_small["rw_a0"][0], full_small["rw_a2"][0], rw_kk, rw_ka, rw_rk,
        rw_gn_g, rw_gn_b, w_hg_st, w_rw_st, w_out_full, final_g, loss_target[0], my_core)
    loss = lax.psum(loss_b[0, 0], ("x", "y", "c"))

    dmod_all = _gather_all("dmod_gather", dmod[0:2])
    dmod_here = lax.dynamic_slice_in_dim(dmod_all, my_chip[0] * ada_cols, ada_cols, axis=2)
    d_ctx_row = _add_small("d_mod_ctx", [dmod_here[j, 1:2] for j in range(N_DEV)], (1, ada_cols))
    dm16 = jnp.concatenate([dmod_here[:, 0], d_ctx_row, jnp.zeros((7, ada_cols), F32)], axis=0)
    g_ada_here = _mm_tn("d_ada_w", sc16, dm16)
    d_sc16 = _mm_nt("d_cond", dm16, mat(ada_w))

    def cond_bwd(i, r, f):
        _, vjp = jax.vjp(jax.nn.silu, r[0])
        return [vjp(r[1])[0]], []

    (d_cond16,) = _row_call("cond_bwd", cond_bwd, 1, 16, [(cond16, 0, dm, 0), (d_sc16, 0, dm, 0)], [], [(16, dm, F32, 0)], [])
    g["c_ctx"] = jnp.where(my_core[0] == 0, d_cond16[N_DEV], 0.0)
    g["ada_b"] = _add_small("g_ada_b", [dmod[0:1], dmod[1:2]], (1, 3 * dm))

    g_small = {"hg_lb": g["hg_lb"], "rw_mu": g["rw_mu"][None], "rw_w0": g["rw_w0"][None], "rw_w2": g["rw_w2"][None],
               "rw_a0": g["rw_a0"][None], "rw_a2": g["rw_a2"][None]}
    split = {n: _split_shards(g_small[n]) for n in _SMALL_SHARDED}
    small_parts = jnp.stack([pack_small({n: split[n][j] for n in _SMALL_SHARDED}) for j in range(N_SHARD)], axis=0)
    def finish(name, chip_sum, landed):
        half = _sum_landed(f"grads_sum_{name}", landed, chip_sum, my_chip)
        return half, _pair_exchange(f"grads_pair_swap_{name}", half, my_core, False, F32)

    res = {}
    later = {"small": small_parts}
    later_chip = [_pair_exchange(f"grads_pair_sum_{n}", a, my_core, True, BF16) for n, a in later.items()]
    outs, later_landed = _adamw_halves("adamw_w_in", mat(w["w_in"]), mat(m["w_in"]), mat(v["w_in"]),
                                       *finish("w_in", *g["w_in"]), my_core, scatter=tuple(later_chip))
    res["w_in"] = [o.reshape(w["w_in"].shape) for o in outs]
    pending = {n: g[n] for n in ("w_hg_out", "w_rw_out", "w_out")}
    pending.update(zip(later, zip(later_chip, later_landed)))
    rep_shapes = [w[n].shape for n in _REPLICATED]
    rep_all = _gather_all("grads_replicated", _pack([g[n].reshape(w[n].shape) for n in _REPLICATED]))

    outs = _adamw("adamw_ada_w", mat(ada_w), mat(m["ada_w"]), mat(v["ada_w"]), g_ada_here[None])
    res["ada_w"] = [o.reshape(ada_w.shape) for o in outs]
    for n in ("w_hg_out", "w_rw_out", "w_out"):
        outs = _adamw_halves(f"adamw_{n}", mat(w[n]), mat(m[n]), mat(v[n]), *finish(n, *pending[n]), my_core)
        res[n] = [o.reshape(w[n].shape) for o in outs]
    outs = _adamw_halves("adamw_small", small_mine, pack_small(m), pack_small(v), *finish("small", *pending["small"]), my_core)
    for i, vals in enumerate(zip(*[_unpack(o, small_shapes) for o in outs])):
        res[_SMALL_SHARDED[i]] = list(vals)
    outs = _adamw("adamw_replicated", _pack([w[n] for n in _REPLICATED]), _pack([m[n] for n in _REPLICATED]),
                  _pack([v[n] for n in _REPLICATED]), rep_all)
    for i, vals in enumerate(zip(*[_unpack(o, rep_shapes) for o in outs])):
        res[_REPLICATED[i]] = list(vals)

    return (loss, grad_x[None], *[res[n][0] for n in _WEIGHTS], *[res[n][1] for n in _WEIGHTS],
            *[res[n][2] for n in _WEIGHTS], *[res[n][3] for n in _WEIGHTS])
```

```python
import functools

import jax
import jax.numpy as jnp
from jax import lax
from jax.experimental import pallas as pl
from jax.experimental.pallas import tpu as pltpu

HI = lax.Precision.HIGHEST
F32 = jnp.float32
BF16 = jnp.bfloat16

NORM_EPS = 1e-6
HG_HEAD = 128
RW_HEAD = 64
RW_LORA = 64
RW_GN_EPS = 64e-5
GRID_W = 64
SUB = 16
RW_SUB = 16
STEP = 64
RW_STEP = 64
N_SHARD = 4
N_DEV = 8
LANE = 128

ADAM_LR = 0.001
ADAM_B1 = 0.9
ADAM_B2 = 0.999
ADAM_EPS = 1e-08
ADAM_WD = 0.01
ADAM_STEP = 10

VMEM_LIMIT = 56 * 1024 * 1024


def _params(sem=None):
    return pltpu.CompilerParams(dimension_semantics=sem, vmem_limit_bytes=VMEM_LIMIT)


def _tile(n, cands):
    for c in cands:
        if n % c == 0:
            return c
    return n


def _iota2(n, m, d):
    return lax.broadcasted_iota(jnp.int32, (n, m), d)


def _before(n, rev, strict):
    t, s = _iota2(n, n, 0), _iota2(n, n, 1)
    if rev:
        return (s > t) if strict else (s >= t)
    return (s < t) if strict else (s <= t)


def _running_sum(a, axis, rev):
    n = a.shape[axis]
    shift = 1
    while shift < n:
        pad = list(a.shape)
        pad[axis] = shift
        zeros = jnp.zeros(pad, a.dtype)
        if rev:
            moved = jnp.concatenate([lax.slice_in_dim(a, shift, n, axis=axis), zeros], axis=axis)
        else:
            moved = jnp.concatenate([zeros, lax.slice_in_dim(a, 0, n - shift, axis=axis)], axis=axis)
        a = a + moved
        shift *= 2
    return a


def _sdot(a, b, spec):
    return jnp.einsum(spec, a, b, precision=lax.Precision.DEFAULT, preferred_element_type=F32)


def _hg_step(s0, qraw, iin, fin, lb2, rev):
    c, w = qraw.shape
    h = w // HG_HEAD
    nsub = c // SUB
    lb = jax.nn.sigmoid(lb2[0:1] - lb2[1:2])
    q = jax.nn.silu(qraw)
    fg = lb + (1.0 - lb) * jax.nn.sigmoid(fin)
    kk = 1.0 - fg
    g = jnp.log(fg)
    bcum = _running_sum(g, 0, rev)
    def heads(a):
        return jnp.swapaxes(a.reshape(a.shape[0], h, HG_HEAD), 0, 1)

    def unheads(a):
        return jnp.swapaxes(a, 0, 1).reshape(a.shape[1], w)

    blocks = [slice(j * SUB, (j + 1) * SUB) for j in range(nsub)]
    outs = []
    for sl in blocks:
        qs, ks, vs, bc = [a[sl].reshape(SUB, h, HG_HEAD) for a in (q, kk, iin, bcum)]
        o = jnp.zeros((SUB, h, HG_HEAD), F32)
        for si in range(SUB):
            after = slice(0, si + 1) if rev else slice(si, SUB)
            dec = jnp.exp(jnp.minimum(bc[after] - bc[si:si + 1], 0.0))
            a = jnp.sum(qs[after] * ks[si:si + 1] * dec, axis=-1, keepdims=True)
            term = a * vs[si:si + 1]
            n_rest = SUB - 1 - si if rev else si
            if n_rest:
                rest = jnp.zeros((n_rest, h, HG_HEAD), F32)
                term = jnp.concatenate([term, rest] if rev else [rest, term], axis=0)
            o = o + term
        outs.append(o.reshape(SUB, w))
    order = list(range(nsub - 1, -1, -1)) if rev else list(range(nsub))
    for pos in range(1, nsub):
        j, before = order[pos], order[:pos]
        first = (j + 1) * SUB - 1 if rev else j * SUB
        bstart = bcum[first:first + 1] - g[first:first + 1]
        qp = heads(q[blocks[j]] * jnp.exp(bcum[blocks[j]] - bstart))
        kp = heads(jnp.concatenate([kk[blocks[p]] * jnp.exp(bstart - bcum[blocks[p]]) for p in before], axis=0))
        vp = heads(jnp.concatenate([iin[blocks[p]] for p in before], axis=0))
        outs[j] = outs[j] + unheads(_sdot(_sdot(qp, kp, 'htk,hsk->hts'), vp, 'hts,hsv->htv'))
    o_state = unheads(_sdot(heads(q * jnp.exp(bcum)), s0, 'htk,hvk->htv'))
    last = 0 if rev else c - 1
    blast = bcum[last:last + 1]
    s_new = heads(jnp.exp(blast)) * s0 + _sdot(heads(iin), heads(kk * jnp.exp(blast - bcum)), 'hsv,hsk->hvk')
    return jnp.concatenate(outs, axis=0) + o_state, s_new


def _tri_solve(lmat, rhs, rev):
    hh, c, _ = lmat.shape
    sub = RW_SUB
    nb = c // sub
    diag = jnp.concatenate([lmat[:, i * sub:(i + 1) * sub, i * sub:(i + 1) * sub] for i in range(nb)], axis=0)
    dt = jnp.transpose(diag, (1, 2, 0))
    col = lax.broadcasted_iota(jnp.int32, (sub, 1), 0)
    inv_rows = [None] * sub
    order = list(range(sub - 1, -1, -1)) if rev else list(range(sub))
    for pos, t in enumerate(order):
        row = jnp.broadcast_to((col == t).astype(F32), (sub, dt.shape[2]))
        for s in order[:pos]:
            row = row - dt[t, s:s + 1, :] * inv_rows[s]
        inv_rows[t] = row
    tinv = jnp.transpose(jnp.concatenate([r[None] for r in inv_rows], axis=0), (2, 0, 1))
    p = [None] * nb
    done = []
    for i in (range(nb - 1, -1, -1) if rev else range(nb)):
        r = rhs[:, i * sub:(i + 1) * sub]
        if done:
            lrow = jnp.concatenate([lmat[:, i * sub:(i + 1) * sub, m * sub:(m + 1) * sub] for m in done], axis=2)
            r = r - _sdot(lrow, jnp.concatenate([p[m] for m in done], axis=1), 'hts,hsv->htv')
        p[i] = _sdot(tinv[i * hh:(i + 1) * hh], r, 'hts,hsv->htv')
        done.append(i)
    return jnp.concatenate(p, axis=1)


def _rw_step(s0, r, k, v, wlo, alo, w0h, w2h, a0h, a2h, kkh, kah, rev):
    hh, c, _ = r.shape
    tl = jnp.broadcast_to(jnp.tanh(wlo)[None], (hh, c, wlo.shape[1]))
    al = jnp.broadcast_to(alo[None], (hh, c, alo.shape[1]))
    wlog = -jax.nn.softplus(-(w0h + _sdot(tl, w2h, 'hcl,hlj->hcj'))) - 0.5
    lw = -jnp.exp(wlog)
    a = jax.nn.sigmoid(a0h + _sdot(al, a2h, 'hcl,hlj->hcj'))
    kk = k * kkh
    kk = kk * lax.rsqrt(jnp.sum(kk * kk, axis=-1, keepdims=True) + 1e-12)
    kd = k * (1.0 + (a - 1.0) * kah)
    b = kk * a
    cum = _running_sum(lw, 1, rev)
    ecum, encum = jnp.exp(cum), jnp.exp(-cum)
    alpha = jnp.exp(cum - lw) * kk
    beta = b * encum
    kappa = kd * encum
    rho = r * ecum
    m_lt = _before(c, rev, True)[None]
    m_le = _before(c, rev, False)[None]
    ar = jnp.concatenate([alpha, rho], axis=1)
    kb = jnp.concatenate([kappa, beta], axis=1)
    gram = _sdot(ar, kb, 'htk,hsk->hts')
    a_kap = jnp.where(m_lt, gram[:, :c, :c], 0.0)
    a_bet = jnp.where(m_lt, gram[:, :c, c:], 0.0)
    b_kap = jnp.where(m_le, gram[:, c:, :c], 0.0)
    b_bet = jnp.where(m_le, gram[:, c:, c:], 0.0)
    from_state = _sdot(ar, s0, 'htk,hvk->htv')
    p = _tri_solve(a_bet, from_state[:, :c] + _sdot(a_kap, v, 'hts,hsv->htv'), rev)
    vp = jnp.concatenate([v, -p], axis=1)
    y = from_state[:, c:] + _sdot(jnp.concatenate([b_kap, b_bet], axis=2), vp, 'hts,hsv->htv')
    stil = s0 + _sdot(vp, kb, 'hsv,hsk->hvk')
    last = 0 if rev else c - 1
    return y, stil * ecum[:, last:last + 1, :]


def _fn_h(s, norm_g, scale, shift):
    return s * lax.rsqrt(jnp.mean(s * s, axis=-1, keepdims=True) + NORM_EPS) * norm_g * (1.0 + scale) + shift


def _fn_hgpost(of, ob, z, g):
    tm, w = of.shape
    o = (of + ob).reshape(tm, w // HG_HEAD, HG_HEAD)
    o = o * lax.rsqrt(jnp.mean(o * o, axis=-1, keepdims=True) + NORM_EPS)
    return o.reshape(tm, w) * g * jax.nn.silu(z)


def _fn_rwpost(y0, y1, r, k, v, alo, z, a0, a2, k_a, r_k, gn_g, gn_b):
    tm, w = r.shape
    nh = w // RW_HEAD
    asum = 0.0
    for d in range(2):
        asum = asum + jax.nn.sigmoid(a0[d:d + 1] + jnp.dot(alo[:, d * RW_LORA:(d + 1) * RW_LORA], a2[d],
                                                           precision=HI, preferred_element_type=F32))
    k_sum = k * (2.0 + (asum - 2.0) * k_a)
    ys = (y0 + y1).reshape(tm, nh, RW_HEAD)
    mean = jnp.mean(ys, axis=-1, keepdims=True)
    var = jnp.mean(jnp.square(ys - mean), axis=-1, keepdims=True)
    y = ((ys - mean) * lax.rsqrt(var + RW_GN_EPS)).reshape(tm, w) * gn_g + gn_b
    bonus = jnp.sum((r * k_sum * r_k).reshape(tm, nh, RW_HEAD), axis=-1, keepdims=True) * v.reshape(tm, nh, RW_HEAD)
    return (y + bonus.reshape(tm, w)) * jax.nn.silu(z)


def _fn_merge(a, b, ghg, grw):
    return jax.nn.sigmoid(ghg) * a + jax.nn.sigmoid(grw) * b


def _fn_final(xs, o, gate, final_g, tgt):
    x2 = xs + gate * o
    y = x2 * lax.rsqrt(jnp.mean(x2 * x2, axis=-1, keepdims=True) + NORM_EPS) * final_g
    return 0.5 * jnp.sum(jnp.mean(jnp.square(y - tgt), axis=-1))


def _row_call(name, fn, n_tiles, tm, row_ins, full_ins, row_outs, acc_outs):
    n_ri, n_fi, n_ro = len(row_ins), len(full_ins), len(row_outs)

    def body(*refs):
        i = pl.program_id(0)
        rvals = [r[...] for r in refs[:n_ri]]
        fvals = [r[...] for r in refs[n_ri:n_ri + n_fi]]
        outs = refs[n_ri + n_fi:]
        ro, ao = fn(i, rvals, fvals)
        for ref, val in zip(outs[:n_ro], ro):
            ref[...] = val.astype(ref.dtype)
        for ref, val in zip(outs[n_ro:], ao):
            @pl.when(i == 0)
            def _(ref=ref):
                ref[...] = jnp.zeros_like(ref)
            ref[...] += val.astype(ref.dtype)

    def rspec(width, cb, off, rows):
        return pl.BlockSpec((tm, width), lambda i: (jnp.clip(i - off, 0, rows // tm - 1), cb))

    def fspec(shape):
        nd = len(shape)
        return pl.BlockSpec(shape, lambda i: (0,) * nd)

    in_specs = [rspec(w, cb, off, a.shape[0]) for (a, cb, w, off) in row_ins] + [fspec(a.shape) for a in full_ins]
    out_specs = [rspec(w, 0, off, rows) for (rows, w, _, off) in row_outs] + [fspec(s) for (s, _) in acc_outs]
    out_shape = [jax.ShapeDtypeStruct((rows, w), dt) for (rows, w, dt, _) in row_outs] + \
                [jax.ShapeDtypeStruct(s, dt) for (s, dt) in acc_outs]
    res = pl.pallas_call(
        body, name=name, grid=(n_tiles,), in_specs=in_specs, out_specs=out_specs, out_shape=out_shape,
        compiler_params=_params(("arbitrary",)),
    )(*[a for (a, _, _, _) in row_ins], *full_ins)
    return list(res)


def _mm(name, a, b, m, n, k_steps, tm, tn, a_block, a_map, b_block, b_map, o_shape, o_block, o_map,
        contract, out_dtype=F32, scatter=()):
    ns = len(scatter)
    grid = (m // tm, n // tn, k_steps)

    def body(*refs):
        a_ref, b_ref, o_ref, acc_ref = refs[0], refs[1], refs[2 + ns], refs[3 + 2 * ns]
        kk = pl.program_id(2)
        if ns:
            sc_refs = (refs[2:2 + ns], refs[3 + ns:3 + 2 * ns]) + tuple(refs[4 + 2 * ns:])
            at = (pl.program_id(0) * grid[1] + pl.program_id(1)) * grid[2] + kk
            pl.when(at == 0)(lambda: _scatter_start(scatter, *sc_refs))

        @pl.when(kk == 0)
        def _():
            acc_ref[...] = jnp.zeros_like(acc_ref)

        acc_ref[...] += lax.dot_general(a_ref[...].astype(BF16), b_ref[...].astype(BF16),
                                        (contract, ((), ())), preferred_element_type=F32)

        @pl.when(kk == k_steps - 1)
        def _():
            o_ref[...] = acc_ref[...].astype(o_ref.dtype)

        if ns:
            pl.when(at == grid[0] * grid[1] * grid[2] - 1)(lambda: _scatter_wait(scatter, *sc_refs))

    hbm = pl.BlockSpec(memory_space=pl.ANY)
    sems = [pltpu.SemaphoreType.DMA((ns, _PEER_CHIPS))] * 2 if ns else []
    res = pl.pallas_call(
        body, name=name, grid=grid,
        in_specs=[pl.BlockSpec(a_block, a_map), pl.BlockSpec(b_block, b_map)] + [hbm] * ns,
        out_specs=[pl.BlockSpec(o_block, o_map)] + [hbm] * ns,
        out_shape=[jax.ShapeDtypeStruct(o_shape, out_dtype)] + [jax.ShapeDtypeStruct(s.shape, s.dtype) for s in scatter],
        scratch_shapes=[pltpu.VMEM((tm, tn), F32)] + sems,
        compiler_params=_params(("arbitrary",) * 3 if ns else ("parallel", "parallel", "arbitrary")),
    )(a, b, *scatter)
    return (res[0], list(res[1:])) if ns else res[0]


_TM = (768, 512, 256, 128, 64, 32, 16, 8)
_TN = (512, 256, 128)
_TK = (1024, 768, 512, 256, 128)
_TK_WIDE = (768, 512, 256, 128)
WIDE_OUT_BYTES = 32 << 20


def _tm_wide(m, ns):
    for tm in _TM:
        if m % tm == 0 and 3 * 4 * tm * ns <= WIDE_OUT_BYTES:
            return tm
    return m


def _mm_nn(name, a, b, out_dtype=F32):
    m, k = a.shape
    n = b.shape[1]
    tm, tn, tk = _tile(m, _TM), _tile(n, _TN), _tile(k, _TK)
    return _mm(name, a, b, m, n, k // tk, tm, tn, (tm, tk), lambda i, j, s: (i, s), (tk, tn), lambda i, j, s: (s, j),
               (m, n), (tm, tn), lambda i, j, s: (i, j), ((1,), (0,)), out_dtype)


def _mm_nt(name, a, b, out_dtype=F32):
    m, k = a.shape
    n = b.shape[0]
    tm, tn, tk = _tile(m, _TM), _tile(n, _TN), _tile(k, _TK)
    return _mm(name, a, b, m, n, k // tk, tm, tn, (tm, tk), lambda i, j, s: (i, s), (tn, tk), lambda i, j, s: (j, s),
               (m, n), (tm, tn), lambda i, j, s: (i, j), ((1,), (1,)), out_dtype)


def _mm_tn(name, a, b, out_dtype=F32):
    k, m = a.shape
    n = b.shape[1]
    tm, tn, tk = _tile(m, _TM), _tile(n, _TN), _tile(k, _TK)
    return _mm(name, a, b, m, n, k // tk, tm, tn, (tk, tm), lambda i, j, s: (s, i), (tk, tn), lambda i, j, s: (s, j),
               (m, n), (tm, tn), lambda i, j, s: (i, j), ((0,), (0,)), out_dtype)


def _mm_n_st(name, a, bst, out_dtype=F32, joined=False):
    m, k = a.shape
    ns_, _, ns = bst.shape
    tm, tk = _tm_wide(m, ns), _tile(k, (512, 256, 128))
    out = ((m, ns_ * ns), (tm, ns), lambda i, j, s: (i, j)) if joined else \
          ((ns_, m, ns), (None, tm, ns), lambda i, j, s: (j, i, 0))
    return _mm(name, a, bst, m, ns_ * ns, k // tk, tm, ns,
               (tm, tk), lambda i, j, s: (i, s), (None, tk, ns), lambda i, j, s: (j, s, 0), *out, ((1,), (0,)), out_dtype)


def _mm_st_t(name, ast, bst, out_dtype=F32, scatter=()):
    ns_, n, ns = bst.shape
    m = ast.shape[-2]
    tm, tn = _tile(m, _TM), _tile(n, _TN)
    a_side = ((None, tm, ns), lambda i, j, s: (s, i, 0)) if ast.ndim == 3 else ((tm, ns), lambda i, j, s: (i, s))
    return _mm(name, ast, bst, m, n, ns_, tm, tn, *a_side, (None, tn, ns), lambda i, j, s: (s, j, 0),
               (m, n), (tm, tn), lambda i, j, s: (i, j), ((1,), (1,)), out_dtype, scatter)


def _mm_t_st(name, a, bst, out_dtype=F32, scatter=(), n_shard=N_SHARD):
    k, m = a.shape
    ns = bst.shape[-1] if bst.ndim == 3 else bst.shape[-1] // n_shard
    tm, tk = _tile(m, _TN), _tile(k, _TK_WIDE)
    b_side = ((None, tk, ns), lambda i, j, s: (j, s, 0)) if bst.ndim == 3 else ((tk, ns), lambda i, j, s: (s, j))
    return _mm(name, a, bst, m, n_shard * ns, k // tk, tm, ns, (tk, tm), lambda i, j, s: (s, i), *b_side,
               (n_shard, m, ns), (None, tm, ns), lambda i, j, s: (j, i, 0), ((0,), (0,)), out_dtype, scatter)


RELAYOUT_TILE_BYTES = 12 << 20


def _join_columns(name, st):
    ns_, t, ns = st.shape
    tm = _row_tile_for(t, ns_ * ns, budget=RELAYOUT_TILE_BYTES)

    def body(s_ref, o_ref):
        o_ref[...] = jnp.concatenate([s_ref[j] for j in range(ns_)], axis=1)

    return pl.pallas_call(
        body, name=name, grid=(t // tm,), in_specs=[pl.BlockSpec((ns_, tm, ns), lambda i: (0, i, 0))],
        out_specs=pl.BlockSpec((tm, ns_ * ns), lambda i: (i, 0)),
        out_shape=jax.ShapeDtypeStruct((t, ns_ * ns), st.dtype), compiler_params=_params(("parallel",)),
    )(st)


def _split_columns(name, pieces, n_shard):
    t = pieces[0].shape[0]
    n = sum(p.shape[1] for p in pieces)
    ns = n // n_shard
    tm = _row_tile_for(t, n, budget=RELAYOUT_TILE_BYTES)
    npc = len(pieces)

    def body(*refs):
        full = jnp.concatenate([r[...] for r in refs[:npc]], axis=1)
        for j in range(n_shard):
            refs[npc][j] = full[:, j * ns:(j + 1) * ns]

    return pl.pallas_call(
        body, name=name, grid=(t // tm,),
        in_specs=[pl.BlockSpec((tm, p.shape[1]), lambda i: (i, 0)) for p in pieces],
        out_specs=pl.BlockSpec((n_shard, tm, ns), lambda i: (0, i, 0)),
        out_shape=jax.ShapeDtypeStruct((n_shard, t, ns), pieces[0].dtype), compiler_params=_params(("parallel",)),
    )(*pieces)


def _scan_order(j, n_ctx, n_all, rev):
    if not rev:
        return j
    return jnp.where(j < n_ctx, n_ctx - 1 - j, n_all - 1 - (j - n_ctx))


def _hg_scan_fwd(name, p_hg, lb2, d, n_ctx):
    t, w = p_hg.shape[0], lb2.shape[1]
    h = w // HG_HEAD
    n = t // STEP
    rev = d == 1

    def body(q_ref, i_ref, f_ref, lb_ref, o_ref, st_ref, s_ref):
        j = pl.program_id(0)

        @pl.when(j == 0)
        def _():
            s_ref[...] = jnp.zeros_like(s_ref)

        s0 = s_ref[...]
        st_ref[...] = s0
        o, s1 = _hg_step(s0, q_ref[...], i_ref[...], f_ref[...], lb_ref[...], rev)
        o_ref[...] = o
        s_ref[...] = s1

    def rows(cb):
        return pl.BlockSpec((STEP, w), lambda j: (_scan_order(j, n_ctx, n, rev), cb))

    return pl.pallas_call(
        body, name=name, grid=(n,),
        in_specs=[rows(0), rows(1), rows(2 + d), pl.BlockSpec((2, w), lambda j: (0, 0))],
        out_specs=[rows(0), pl.BlockSpec((None, h, HG_HEAD, HG_HEAD), lambda j: (j, 0, 0, 0))],
        out_shape=[jax.ShapeDtypeStruct((t, w), F32), jax.ShapeDtypeStruct((n, h, HG_HEAD, HG_HEAD), F32)],
        scratch_shapes=[pltpu.VMEM((h, HG_HEAD, HG_HEAD), F32)],
        compiler_params=_params(("arbitrary",)),
    )(p_hg, p_hg, p_hg, lb2)


def _hg_scan_bwd(name, p_hg, lb2, states, do, d, n_ctx, other=()):
    t, w = p_hg.shape[0], lb2.shape[1]
    h = w // HG_HEAD
    n = t // STEP
    rev = d == 1
    no = len(other)

    def body(q_ref, i_ref, f_ref, lb_ref, st_ref, do_ref, *refs):
        dq_ref, di_ref, df_ref, dlb_ref, ds_ref = refs[no:]
        step = pl.program_id(0)

        @pl.when(step == 0)
        def _():
            ds_ref[...] = jnp.zeros_like(ds_ref)
            dlb_ref[...] = jnp.zeros_like(dlb_ref)

        _, vjp = jax.vjp(lambda s0, q, i, f, lb: _hg_step(s0, q, i, f, lb, rev),
                         st_ref[...], q_ref[...], i_ref[...], f_ref[...], lb_ref[...])
        ds0, dq, di, df, dlb = vjp((do_ref[...], ds_ref[...]))
        if no:
            dq, di = refs[0][...] + dq, refs[1][...] + di
        dq_ref[...] = dq.astype(dq_ref.dtype)
        di_ref[...] = di.astype(di_ref.dtype)
        df_ref[...] = df.astype(df_ref.dtype)
        dlb_ref[...] += dlb
        ds_ref[...] = ds0

    def rows(cb):
        return pl.BlockSpec((STEP, w), lambda s: (_scan_order(n - 1 - s, n_ctx, n, rev), cb))

    qi = BF16 if no else F32
    return pl.pallas_call(
        body, name=name, grid=(n,),
        in_specs=[rows(0), rows(1), rows(2 + d), pl.BlockSpec((2, w), lambda s: (0, 0)),
                  pl.BlockSpec((None, h, HG_HEAD, HG_HEAD), lambda s: (n - 1 - s, 0, 0, 0)), rows(0)] + [rows(0)] * no,
        out_specs=[rows(0), rows(0), rows(0), pl.BlockSpec((2, w), lambda s: (0, 0))],
        out_shape=[jax.ShapeDtypeStruct((t, w), qi)] * 2 + [jax.ShapeDtypeStruct((t, w), BF16),
                                                            jax.ShapeDtypeStruct((2, w), F32)],
        scratch_shapes=[pltpu.VMEM((h, HG_HEAD, HG_HEAD), F32)],
        compiler_params=_params(("arbitrary",)),
    )(p_hg, p_hg, p_hg, lb2, states, do, *other)


def _to_heads(a, nh):
    return jnp.stack([a[:, i * RW_HEAD:(i + 1) * RW_HEAD] for i in range(nh)], axis=0)


def _from_heads(a):
    return jnp.concatenate([a[i] for i in range(a.shape[0])], axis=-1)


def _rw_scan_fwd(name, sh, hp, d, n_ctx):
    t = sh.shape[0]
    w = (sh.shape[1] - 4 * RW_LORA) // 3
    nh = w // RW_HEAD
    n = t // RW_STEP
    rev = d == 1
    lo = 3 * w // LANE

    def body(r_ref, k_ref, v_ref, wl_ref, al_ref, w0_ref, w2_ref, a0_ref, a2_ref, kk_ref, ka_ref,
             y_ref, st_ref, s_ref):
        j = pl.program_id(0)

        @pl.when(j == 0)
        def _():
            s_ref[...] = jnp.zeros_like(s_ref)

        s0 = s_ref[...]
        st_ref[...] = s0
        wl = wl_ref[...][:, d * RW_LORA:(d + 1) * RW_LORA]
        al = al_ref[...][:, d * RW_LORA:(d + 1) * RW_LORA]
        y, s1 = _rw_step(s0, _to_heads(r_ref[...], nh), _to_heads(k_ref[...], nh), _to_heads(v_ref[...], nh), wl, al,
                         w0_ref[...], w2_ref[...], a0_ref[...], a2_ref[...], kk_ref[...], ka_ref[...], rev)
        y_ref[...] = _from_heads(y)
        s_ref[...] = s1

    def rows(cb, width=w):
        return pl.BlockSpec((RW_STEP, width), lambda j: (_scan_order(j, n_ctx, n, rev), cb))

    def whole(a):
        nd = a.ndim
        return pl.BlockSpec(a.shape, lambda j: (0,) * nd)

    return pl.pallas_call(
        body, name=name, grid=(n,),
        in_specs=[rows(0), rows(1), rows(2), rows(lo, LANE), rows(lo + 1, LANE)] + [whole(a) for a in hp],
        out_specs=[rows(0), pl.BlockSpec((None, nh, RW_HEAD, RW_HEAD), lambda j: (j, 0, 0, 0))],
        out_shape=[jax.ShapeDtypeStruct((t, w), F32), jax.ShapeDtypeStruct((n, nh, RW_HEAD, RW_HEAD), F32)],
        scratch_shapes=[pltpu.VMEM((nh, RW_HEAD, RW_HEAD), F32)],
        compiler_params=_params(("arbitrary",)),
    )(sh, sh, sh, sh, sh, *hp)


def _rw_scan_bwd_both(name, sh, hps, states, dy, n_ctx):
    t = sh.shape[0]
    w = (sh.shape[1] - 4 * RW_LORA) // 3
    nh = w // RW_HEAD
    n = t // RW_STEP
    lo = 3 * w // LANE
    n_in, n_p = 13, 6

    def body(*refs):
        step = pl.program_id(0)
        ins = [refs[d * n_in:(d + 1) * n_in] for d in range(2)]
        outs = [refs[2 * n_in + d * (1 + n_p):2 * n_in + (d + 1) * (1 + n_p)] for d in range(2)]
        ds_refs = refs[2 * n_in + 2 * (1 + n_p):]

        @pl.when(step == 0)
        def _():
            for d in range(2):
                ds_refs[d][...] = jnp.zeros_like(ds_refs[d])
                for ref in outs[d][1:]:
                    ref[...] = jnp.zeros_like(ref)

        for d in range(2):
            r_ref, k_ref, v_ref, wl_ref, al_ref = ins[d][:5]
            hp_refs, st_ref, dy_ref = ins[d][5:11], ins[d][11], ins[d][12]
            wl = wl_ref[...][:, d * RW_LORA:(d + 1) * RW_LORA]
            al = al_ref[...][:, d * RW_LORA:(d + 1) * RW_LORA]
            _, vjp = jax.vjp(functools.partial(_rw_step, rev=d == 1),
                             st_ref[...], _to_heads(r_ref[...], nh), _to_heads(k_ref[...], nh), _to_heads(v_ref[...], nh),
                             wl, al, *[p[...] for p in hp_refs])
            g = vjp((_to_heads(dy_ref[...], nh), ds_refs[d][...]))
            ds_refs[d][...] = g[0]
            zero = jnp.zeros_like(g[4])
            lora = [zero] * 4
            lora[d], lora[2 + d] = g[4], g[5]
            outs[d][0][...] = jnp.concatenate([_from_heads(g[1]), _from_heads(g[2]), _from_heads(g[3])] + lora, axis=-1)
            for ref, val in zip(outs[d][1:], g[6:]):
                ref[...] += val

    def rows(d, cb, width=w):
        return pl.BlockSpec((RW_STEP, width), lambda s: (_scan_order(n - 1 - s, n_ctx, n, d == 1), cb))

    def whole(a):
        nd = a.ndim
        return pl.BlockSpec(a.shape, lambda s: (0,) * nd)

    in_specs, operands, out_specs, out_shape = [], [], [], []
    for d in range(2):
        in_specs += [rows(d, 0), rows(d, 1), rows(d, 2), rows(d, lo, LANE), rows(d, lo + 1, LANE)]
        in_specs += [whole(a) for a in hps[d]]
        in_specs += [pl.BlockSpec((None, nh, RW_HEAD, RW_HEAD), lambda s: (n - 1 - s, 0, 0, 0)), rows(d, 0)]
        operands += [sh] * 5 + list(hps[d]) + [states[d], dy]
        out_specs += [rows(d, 0, sh.shape[1])] + [whole(a) for a in hps[d]]
        out_shape += [jax.ShapeDtypeStruct(sh.shape, F32)] + [jax.ShapeDtypeStruct(a.shape, F32) for a in hps[d]]
    res = pl.pallas_call(
        body, name=name, grid=(n,), in_specs=in_specs, out_specs=out_specs, out_shape=out_shape,
        scratch_shapes=[pltpu.VMEM((nh, RW_HEAD, RW_HEAD), F32)] * 2, compiler_params=_params(("arbitrary",)),
    )(*operands)
    return [res[0], res[1 + n_p]], [res[1:1 + n_p], res[2 + n_p:]]


def _shift_masks(t, n_ctx_rows):
    row = lax.broadcasted_iota(jnp.int32, (t, 1), 0)
    isx = row >= n_ctx_rows
    pos = jnp.where(isx, row - n_ctx_rows, row)
    col = jnp.where(isx, jnp.bitwise_and(pos, GRID_W - 1), pos)
    ncol = jnp.where(isx, GRID_W, n_ctx_rows)
    n_x = t - n_ctx_rows
    ml = col != 0
    mr = col != ncol - 1
    mu = isx & (pos >= GRID_W)
    md = isx & (pos < n_x - GRID_W)
    return ml, mr, mu, md, isx


def _shift_fwd(name, p, col0, mu, n_ctx_rows):
    t, c = p.shape[0], mu.shape[1]
    cw = LANE

    def body(p_ref, mu_ref, o_ref):
        x = p_ref[...]
        m = mu_ref[...]
        ml, mr, mup, mdn, isx = _shift_masks(t, n_ctx_rows)
        left = jnp.where(ml, pltpu.roll(x, 1, 0), 0.0)
        right = jnp.where(mr, pltpu.roll(x, t - 1, 0), 0.0)
        up = jnp.where(mup, pltpu.roll(x, GRID_W, 0), 0.0)
        down = jnp.where(mdn, pltpu.roll(x, t - GRID_W, 0), 0.0)
        out = x + m[0:1] * (left - x) + m[1:2] * (right - x)
        vert = m[2:3] * (up - x) + m[3:4] * (down - x)
        o_ref[...] = out + jnp.where(isx, vert, 0.0)

    return pl.pallas_call(
        body, name=name, grid=(c // cw,),
        in_specs=[pl.BlockSpec((t, cw), lambda j: (0, col0 + j)), pl.BlockSpec((4, cw), lambda j: (0, j))],
        out_specs=pl.BlockSpec((t, cw), lambda j: (0, j)),
        out_shape=jax.ShapeDtypeStruct((t, c), F32),
        compiler_params=_params(("parallel",)),
    )(p, mu)


def _shift_bwd(name, p, col0, mu, dparts, n_ctx_rows):
    t, c = p.shape[0], mu.shape[1]
    cw = LANE
    npart = len(dparts)

    def body(*refs):
        p_ref, mu_ref = refs[0], refs[1]
        dp_ref, dmu_ref = refs[2 + npart], refs[3 + npart]
        x = p_ref[...]
        m = mu_ref[...]
        g = refs[2][...]
        for r in refs[3:2 + npart]:
            g = g + r[...]
        ml, mr, mup, mdn, isx = _shift_masks(t, n_ctx_rows)
        left = jnp.where(ml, pltpu.roll(x, 1, 0), 0.0)
        right = jnp.where(mr, pltpu.roll(x, t - 1, 0), 0.0)
        up = jnp.where(mup, pltpu.roll(x, GRID_W, 0), 0.0)
        down = jnp.where(mdn, pltpu.roll(x, t - GRID_W, 0), 0.0)
        gx = jnp.where(isx, g, 0.0)
        dmu_ref[...] = jnp.concatenate([
            jnp.sum(g * (left - x), axis=0, keepdims=True), jnp.sum(g * (right - x), axis=0, keepdims=True),
            jnp.sum(gx * (up - x), axis=0, keepdims=True), jnp.sum(gx * (down - x), axis=0, keepdims=True)], axis=0)
        coef = 1.0 - m[0:1] - m[1:2] - jnp.where(isx, m[2:3] + m[3:4], 0.0)
        dp = coef * g
        dp = dp + m[0:1] * pltpu.roll(jnp.where(ml, g, 0.0), t - 1, 0)
        dp = dp + m[1:2] * pltpu.roll(jnp.where(mr, g, 0.0), 1, 0)
        dp = dp + m[2:3] * pltpu.roll(jnp.where(mup, g, 0.0), t - GRID_W, 0)
        dp = dp + m[3:4] * pltpu.roll(jnp.where(mdn, g, 0.0), GRID_W, 0)
        dp_ref[...] = dp.astype(dp_ref.dtype)

    col = pl.BlockSpec((t, cw), lambda j: (0, j))
    par = pl.BlockSpec((4, cw), lambda j: (0, j))
    return pl.pallas_call(
        body, name=name, grid=(c // cw,),
        in_specs=[pl.BlockSpec((t, cw), lambda j: (0, col0 + j)), par] + [col] * npart,
        out_specs=[col, par],
        out_shape=[jax.ShapeDtypeStruct((t, c), BF16), jax.ShapeDtypeStruct((4, c), F32)],
        compiler_params=_params(("parallel",)),
    )(p, mu, *dparts)


def _add_small(name, terms, shape):
    flat2 = [a.reshape(-1, a.shape[-1]) for a in terms]
    return _rowwise(name, lambda *v: _slot_sum(list(v)), flat2, F32).reshape(shape)


def _local_step(x, ctx, mod, norm_g, w_in_st, hg_lb, hg_norm_g, rw_mu, rw_w0, rw_w2, rw_a0, rw_a2,
                rw_kk, rw_ka, rw_rk, rw_gn_g, rw_gn_b, w_hg_st, w_rw_st, w_out, final_g, tgt, my_core):
    seq, dm = x.shape
    n_ctx_rows = ctx.shape[0]
    t = seq + n_ctx_rows
    hw = hg_norm_g.shape[-1]
    rw = rw_kk.shape[-1]
    nh_rw = rw // RW_HEAD
    n_ctx = n_ctx_rows // STEP
    tm = _tile(n_ctx_rows, (256, 128, 64))
    nt = t // tm
    nct = n_ctx_rows // tm
    n_sh_cols = 3 * rw + 4 * RW_LORA

    final_g2 = final_g.reshape(1, dm)
    add = _add_small
    mod3 = mod.reshape(8, 3, dm)

    def pick(i, m3):
        r = jnp.where(i < nct, m3[1], m3[0])
        return r[0:1], r[1:2]

    tokens = [(ctx, 0, dm, 0), (x, 0, dm, nct)]

    def h_fn(i, r, f):
        shift, scale = pick(i, f[1])
        return [_fn_h(jnp.where(i < nct, r[0], r[1]), f[0], scale, shift)], []

    (h,) = _row_call("h_fwd", h_fn, nt, tm, tokens, [norm_g, mod3], [(t, dm, BF16, 0)], [])
    proj = _join_columns("proj_join", _mm_n_st("proj_mm", h, w_in_st))
    p_hg = proj
    rs_tile0 = 5 * hw // LANE
    p_zr = proj[:, 5 * hw + n_sh_cols:5 * hw + n_sh_cols + rw]
    p_gt = proj[:, 5 * hw + n_sh_cols + rw:]

    o_hg, st_hg = [], []
    for d in range(2):
        o, st = _hg_scan_fwd(f"hg_scan_fwd{d}", p_hg, hg_lb[d], d, n_ctx)
        o_hg.append(o)
        st_hg.append(st)

    def hgpost_fn(i, r, f):
        return [_fn_hgpost(r[0], r[1], r[2], f[0])], []

    hg_in = [(o_hg[0], 0, hw, 0), (o_hg[1], 0, hw, 0), (p_hg, 4, hw, 0)]
    (y_hg,) = _row_call("hg_post", hgpost_fn, nt, tm, hg_in, [hg_norm_g], [(t, hw, BF16, 0)], [])

    sh = _shift_fwd("rw_shift", proj, rs_tile0, rw_mu, n_ctx_rows)
    hps = []
    for d in range(2):
        hps.append([rw_w0[d].reshape(nh_rw, 1, RW_HEAD), jnp.swapaxes(rw_w2[d].reshape(RW_LORA, nh_rw, RW_HEAD), 0, 1),
                    rw_a0[d].reshape(nh_rw, 1, RW_HEAD), jnp.swapaxes(rw_a2[d].reshape(RW_LORA, nh_rw, RW_HEAD), 0, 1),
                    rw_kk.reshape(nh_rw, 1, RW_HEAD), rw_ka.reshape(nh_rw, 1, RW_HEAD)])
    y_rw_d, st_rw = [], []
    for d in range(2):
        y, st = _rw_scan_fwd(f"rw_scan_fwd{d}", sh, hps[d], d, n_ctx_rows // RW_STEP)
        y_rw_d.append(y)
        st_rw.append(st)

    rw_full = [rw_a0, rw_a2, rw_ka, rw_rk, rw_gn_g, rw_gn_b]
    lo = 3 * rw // LANE
    rw_in = [(y_rw_d[0], 0, rw, 0), (y_rw_d[1], 0, rw, 0), (sh, 0, rw, 0), (sh, 1, rw, 0), (sh, 2, rw, 0),
             (sh, lo + 1, LANE, 0), (p_zr, 0, rw, 0)]

    def rwpost_fn(i, r, f):
        return [_fn_rwpost(*r, *f)], []

    (y_rw,) = _row_call("rw_post", rwpost_fn, nt, tm, rw_in, rw_full, [(t, rw, BF16, 0)], [])

    a_hg = _mm_n_st("hg_out_mm", y_hg, w_hg_st, joined=True)
    a_rw = _mm_n_st("rw_out_mm", y_rw, w_rw_st, joined=True)
    mg_in = [(a_hg, 0, dm, 0), (a_rw, 0, dm, 0), (p_gt, 0, dm, 0), (p_gt, 1, dm, 0)]
    (merged,) = _row_call("merge", lambda i, r, f: ([_fn_merge(*r)], []), nt, tm, mg_in, [], [(t, dm, BF16, 0)], [])
    o_out = _mm_nn("out_mm", merged, w_out)

    def final_fn(i, r, f):
        gate = f[0][0][2:3]
        loss, vjp = jax.vjp(_fn_final, r[0], r[1], gate, f[1], r[2])
        dx, do, dgate, dfg, _ = vjp(jnp.ones((), F32))
        live = i >= nct
        zero = lambda a: jnp.where(live, a, 0.0)
        dmod = jnp.concatenate([jnp.concatenate([jnp.zeros((1, 2 * dm), F32), zero(dgate)], axis=1),
                                jnp.zeros((7, 3 * dm), F32)], axis=0)
        return [zero(dx), zero(do)], [jnp.broadcast_to(zero(loss), (8, LANE)), dmod, zero(dfg)]

    fin_in = [(x, 0, dm, nct), (o_out, 0, dm, 0), (tgt, 0, dm, nct)]
    dx_res, d_o, loss_acc, dmod_gate, d_final_g = _row_call(
        "final", final_fn, nt, tm, fin_in, [mod3, final_g2], [(t, dm, F32, 0), (t, dm, BF16, 0)],
        [((8, LANE), F32), ((8, 3 * dm), F32), ((1, dm), F32)])

    g_w_out = _mm_tn("d_w_out", merged, d_o)
    d_merged = _mm_nt("d_merged", d_o, w_out)

    def merge_bwd(i, r, f):
        _, vjp = jax.vjp(_fn_merge, r[0], r[1], r[2], r[3])
        da, db, dgh, dgr = vjp(r[4])
        return [da, db, jnp.concatenate([dgh, dgr], axis=1)], []

    da_hg, da_rw, dp_gt = _row_call("merge_bwd", merge_bwd, nt, tm, mg_in + [(d_merged, 0, dm, 0)], [],
                                    [(t, dm, BF16, 0), (t, dm, BF16, 0), (t, 2 * dm, BF16, 0)], [])
    g_w_hg_st = _mm_t_st("d_w_hg", y_hg, da_hg)
    g_w_rw_st = _mm_t_st("d_w_rw", y_rw, da_rw)
    dy_hg = _mm_st_t("d_y_hg", da_hg, w_hg_st)
    dy_rw = _mm_st_t("d_y_rw", da_rw, w_rw_st)

    def hgpost_bwd(i, r, f):
        _, vjp = jax.vjp(_fn_hgpost, r[0], r[1], r[2], f[0])
        dof, _, dz, dg = vjp(r[3])
        return [dof, dz], [dg]

    do_hg, dz_hg, g_hg_norm = _row_call("hg_post_bwd", hgpost_bwd, nt, tm, hg_in + [(dy_hg, 0, hw, 0)], [hg_norm_g],
                                        [(t, hw, F32, 0), (t, hw, BF16, 0)], [((1, hw), F32)])
    dq0, di0, df0, dlb0 = _hg_scan_bwd("hg_scan_bwd0", p_hg, hg_lb[0], st_hg[0], do_hg, 0, n_ctx)
    dq, di, df1, dlb1 = _hg_scan_bwd("hg_scan_bwd1", p_hg, hg_lb[1], st_hg[1], do_hg, 1, n_ctx, other=(dq0, di0))
    g_hg_lb = jnp.stack([dlb0, dlb1], axis=0)

    def rwpost_bwd(i, r, f):
        _, vjp = jax.vjp(_fn_rwpost, *r[:7], *f)
        g = vjp(r[7])
        zl = jnp.zeros((g[5].shape[0], 2 * RW_LORA), F32)
        return [g[0], jnp.concatenate([g[2], g[3], g[4], zl, g[5]], axis=1), g[6]], list(g[7:])

    dy_sum, dsh_p, dz_rw, g_a0_p, g_a2_p, g_ka_p, g_rk, g_gn_g, g_gn_b = _row_call(
        "rw_post_bwd", rwpost_bwd, nt, tm, rw_in + [(dy_rw, 0, rw, 0)], rw_full,
        [(t, rw, F32, 0), (t, n_sh_cols, F32, 0), (t, rw, BF16, 0)], [(a.shape, F32) for a in rw_full])
    dsh_dirs, hp_grads = _rw_scan_bwd_both("rw_scan_bwd", sh, hps, st_rw, dy_sum, n_ctx_rows // RW_STEP)
    dp_rs, g_mu = _shift_bwd("rw_shift_bwd", proj, rs_tile0, rw_mu, [dsh_p] + dsh_dirs, n_ctx_rows)

    def flat(a):
        if a.shape[1] == 1:
            return a.reshape(rw)
        return jnp.swapaxes(a, 0, 1).reshape(RW_LORA, rw)

    g_w0 = jnp.stack([flat(hp_grads[d][0]) for d in range(2)], axis=0)
    g_w2 = jnp.stack([flat(hp_grads[d][1]) for d in range(2)], axis=0)
    g_a0 = add("g_a0", [jnp.stack([flat(hp_grads[d][2]) for d in range(2)], axis=0), g_a0_p], (2, rw))
    g_a2 = add("g_a2", [jnp.stack([flat(hp_grads[d][3]) for d in range(2)], axis=0), g_a2_p], (2, RW_LORA, rw))
    g_kk = add("g_kk", [flat(hp_grads[0][4]).reshape(1, rw), flat(hp_grads[1][4]).reshape(1, rw)], (1, rw))
    g_ka = add("g_ka", [flat(hp_grads[0][5]).reshape(1, rw), flat(hp_grads[1][5]).reshape(1, rw), g_ka_p], (1, rw))

    dproj_st = _split_columns("dproj_split", [dq, di, df0, df1, dz_hg, dp_rs, dz_rw, dp_gt], N_SHARD)
    g_small = {"hg_lb": g_hg_lb, "rw_mu": g_mu, "rw_w0": g_w0, "rw_w2": g_w2, "rw_a0": g_a0, "rw_a2": g_a2}
    split = {n: _split_shards(g_small[n]) for n in _SMALL_SHARDED}
    small_parts = jnp.stack([_pack_small({n: split[n][j] for n in _SMALL_SHARDED}) for j in range(N_SHARD)], axis=0)
    early = {"w_hg_out": g_w_hg_st, "w_rw_out": g_w_rw_st, "w_out": g_w_out.reshape(N_SHARD, dm // N_SHARD, dm),
             "small": small_parts}
    early_chip = [_pair_exchange(f"grads_pair_sum_{n}", a, my_core, True, BF16) for n, a in early.items()]
    g_w_in_st, early_landed = _mm_t_st("d_w_in", h, dproj_st, scatter=tuple(early_chip))
    w_in_chip = _pair_exchange("grads_pair_sum_w_in", g_w_in_st, my_core, True, BF16)
    dh, (w_in_landed,) = _mm_st_t("d_h", dproj_st, w_in_st, scatter=(w_in_chip,))

    def h_bwd(i, r, f):
        shift, scale = pick(i, f[1])
        is_ctx = i < nct
        _, vjp = jax.vjp(_fn_h, jnp.where(is_ctx, r[0], r[1]), f[0], scale, shift)
        ds, dg, dscale, dshift = vjp(r[2])
        row = jnp.concatenate([dshift, dscale, jnp.zeros((1, dm), F32)], axis=1)
        z = jnp.zeros_like(row)
        dmod = jnp.concatenate([jnp.where(is_ctx, z, row), jnp.where(is_ctx, row, z), jnp.zeros((6, 3 * dm), F32)], axis=0)
        return [ds + r[3]], [dg, dmod]

    grad_x, g_norm_g, dmod_h = _row_call(
        "h_bwd", h_bwd, nt, tm, tokens + [(dh, 0, dm, 0), (dx_res, 0, dm, 0)], [norm_g, mod3],
        [(seq, dm, F32, nct)], [((1, dm), F32), ((8, 3 * dm), F32)])
    dmod = add("d_mod", [dmod_h, dmod_gate], (8, 3 * dm))
    grads = dict(
        norm_g=g_norm_g, w_in=(w_in_chip, w_in_landed), hg_norm_g=g_hg_norm, rw_kk=g_kk, rw_ka=g_ka,
        rw_rk=g_rk, rw_gn_g=g_gn_g, rw_gn_b=g_gn_b, final_g=d_final_g.reshape(dm))
    grads.update(zip(early, zip(early_chip, early_landed)))
    return loss_acc[0:1, 0:1], grad_x, dmod, grads


def _my_place():
    return lax.axis_index("x"), lax.axis_index("y"), lax.axis_index("c")


MIN_CHUNK_BYTES = 1 << 18
ROW_ALIGN = 16


def _n_chunks(rows, row_bytes):
    for n in (8, 4, 2):
        if rows % (n * ROW_ALIGN) == 0 and rows // n * row_bytes >= MIN_CHUNK_BYTES:
            return n
    return 1


def _row_bytes(a, lead=1):
    n = a.dtype.itemsize
    for d in a.shape[lead:]:
        n *= d
    return n


def _rows(ref, start, size):
    return ref.at[pl.ds(start, size)]


def _chunked(make, start, size, n):
    cs = size // n
    return [make(start + j * cs, cs) for j in range(n)]


_PEER_CHIPS = 3


def _weights_gather(name, big, small):
    nb, na = len(big), len(big) + len(small)
    arrays = list(big) + list(small)
    n_ici = 6

    def body(*refs):
        outs = refs[na:2 * na]
        send_sems, recv_sems, fsend_sems, frecv_sems = refs[2 * na:]
        x, y, c = _my_place()
        me, sx, sy, sd = 2 * x + y, 2 * (1 - x) + y, 2 * x + (1 - y), 2 * (1 - x) + (1 - y)
        kx, ky, kd = (1 - x, y, c), (x, 1 - y, c), (1 - x, 1 - y, c)

        def ici(a, j, src_slot, dst_slot, to, r0, nr):
            return pltpu.make_async_remote_copy(
                src_ref=_rows(outs[a].at[src_slot], r0, nr), dst_ref=_rows(outs[a].at[dst_slot], r0, nr),
                send_sem=send_sems.at[a, j], recv_sem=recv_sems.at[a, j], device_id=to,
                device_id_type=pl.DeviceIdType.MESH)

        def to_sibling(a, k, slot, r0, nr):
            rows = _rows(outs[a].at[slot], r0, nr)
            return pltpu.make_async_remote_copy(
                src_ref=rows, dst_ref=rows, send_sem=fsend_sems.at[a, k], recv_sem=frecv_sems.at[a, k],
                device_id=(x, y, 1 - c), device_id_type=pl.DeviceIdType.MESH)

        def start(copies):
            for cp in copies:
                cp.start()

        geo = []
        for a in range(nb):
            half = arrays[a].shape[1] // 2
            geo.append((pl.multiple_of(c * half, ROW_ALIGN), pl.multiple_of((1 - c) * half, ROW_ALIGN), half // 2,
                        _n_chunks(half // 2, _row_bytes(arrays[a], 2))))
        plan = [(me, sx, kx, 0), (me, sx, kx, 1), (me, sy, ky, 0), (me, sy, ky, 1), (sx, sd, ky, 0), (sy, sd, kx, 1)]

        def piece(a, j):
            return geo[a][0] + plan[j][3] * geo[a][2]

        for a in range(nb):
            for j in range(4):
                start(_chunked(lambda r0, cs: ici(a, j, me, me, plan[j][2], r0, cs), piece(a, j), geo[a][2], geo[a][3]))
        for a in range(nb, na):
            rows = arrays[a].shape[1]
            for j, to in ((0, kx), (2, ky), (1, kd)):
                ici(a, j, me, me, to, 0, rows).start()
        for a in range(nb):
            for j, first in ((4, 0), (5, 3)):
                src_slot, _, to, _ = plan[j]
                ici(a, first, me, plan[first][1], plan[first][2], piece(a, first), geo[a][2]).wait_recv()
                start(_chunked(lambda r0, cs: ici(a, j, src_slot, src_slot, to, r0, cs), piece(a, j), geo[a][2], geo[a][3]))
        for a in range(nb):
            for j in (1, 2):
                ici(a, j, me, plan[j][1], plan[j][2], piece(a, j), geo[a][2]).wait_recv()
            for k, slot in ((0, sx), (1, sy)):
                start(_chunked(lambda r0, cs: to_sibling(a, k, slot, r0, cs), geo[a][0], 2 * geo[a][2], geo[a][3]))
        for a in range(nb):
            for j in (4, 5):
                ici(a, j, me, sd, plan[j][2], piece(a, j), geo[a][2]).wait_recv()
            start(_chunked(lambda r0, cs: to_sibling(a, 2, sd, r0, cs), geo[a][0], 2 * geo[a][2], geo[a][3]))
        for a in range(nb, na):
            rows = arrays[a].shape[1]
            for j, slot, to in ((0, sx, kx), (2, sy, ky), (1, sd, kd)):
                ici(a, j, me, slot, to, 0, rows).wait_recv()
        for a in range(nb):
            for k, slot in ((0, sx), (1, sy), (2, sd)):
                to_sibling(a, k, slot, geo[a][1], 2 * geo[a][2]).wait_recv()
        for a in range(nb):
            for j in range(n_ici):
                ici(a, j, me, me, plan[j][2], piece(a, j), geo[a][2]).wait_send()
            for k, slot in ((0, sx), (1, sy), (2, sd)):
                to_sibling(a, k, slot, geo[a][0], 2 * geo[a][2]).wait_send()
        for a in range(nb, na):
            rows = arrays[a].shape[1]
            for j, to in ((0, kx), (2, ky), (1, kd)):
                ici(a, j, me, me, to, 0, rows).wait_send()

    hbm = pl.BlockSpec(memory_space=pl.ANY)
    ici_sems = pltpu.SemaphoreType.DMA((na, n_ici))
    pair_sems = pltpu.SemaphoreType.DMA((na, _PEER_CHIPS))
    return pl.pallas_call(
        body, name=name, in_specs=[hbm] * na, out_specs=[hbm] * na,
        out_shape=[jax.ShapeDtypeStruct(a.shape, a.dtype) for a in arrays],
        input_output_aliases={a: a for a in range(na)}, scratch_shapes=[ici_sems, ici_sems, pair_sems, pair_sems],
    )(*arrays)


def _scatter_copy(arrays, ins, outs, send_sems, recv_sems, a, k, slot, r0, nr):
    x, y, c = _my_place()
    px, py = [(1 - x, y), (x, 1 - y), (1 - x, 1 - y)][k]
    return pltpu.make_async_remote_copy(
        src_ref=_rows(ins[a].at[2 * px + py], r0, nr), dst_ref=_rows(outs[a].at[slot], r0, nr),
        send_sem=send_sems.at[a, k], recv_sem=recv_sems.at[a, k], device_id=(px, py, c),
        device_id_type=pl.DeviceIdType.MESH)


def _scatter_start(arrays, ins, outs, send_sems, recv_sems):
    x, y, _ = _my_place()
    for a in range(len(arrays)):
        rows = arrays[a].shape[1]
        for k in range(_PEER_CHIPS):
            for cp in _chunked(lambda r0, cs: _scatter_copy(arrays, ins, outs, send_sems, recv_sems, a, k, 2 * x + y, r0, cs),
                               0, rows, _n_chunks(rows, _row_bytes(arrays[a], 2))):
                cp.start()


def _scatter_wait(arrays, ins, outs, send_sems, recv_sems):
    x, y, _ = _my_place()
    peer_slot = [2 * (1 - x) + y, 2 * x + (1 - y), 2 * (1 - x) + (1 - y)]
    for k in range(_PEER_CHIPS):
        for a in range(len(arrays)):
            _scatter_copy(arrays, ins, outs, send_sems, recv_sems, a, k, peer_slot[k], 0, arrays[a].shape[1]).wait_recv()
    for a in range(len(arrays)):
        for k in range(_PEER_CHIPS):
            _scatter_copy(arrays, ins, outs, send_sems, recv_sems, a, k, 2 * x + y, 0, arrays[a].shape[1]).wait_send()


PAIR_TILE_BYTES = 4 << 20


def _pair_exchange(name, a, place, reduce, out_dtype):
    rows, cols = a.shape[-2], a.shape[-1]
    half = rows // 2 if reduce else rows
    tr = _row_tile_for(half, cols, budget=PAIR_TILE_BYTES)
    nh = half // tr
    n_steps = (N_SHARD if reduce else 1) * nh

    def body(pc_ref, *refs):
        if reduce:
            keep_ref, send_ref, o_ref, land, send_sems, recv_sems, credit, wire = refs
            wire[...] = send_ref[...].astype(BF16)
            src = wire
        else:
            send_ref, o_ref, land, send_sems, recv_sems, credit = refs
            src = send_ref
        x, y, c = _my_place()
        other = (x, y, 1 - c)
        t = pl.program_id(0) * nh + pl.program_id(1) if reduce else pl.program_id(0)
        slot = t % 2

        @pl.when(t >= 2)
        def _():
            pl.semaphore_wait(credit, 1)

        copy = pltpu.make_async_remote_copy(
            src_ref=src, dst_ref=land.at[slot], send_sem=send_sems.at[slot], recv_sem=recv_sems.at[slot],
            device_id=other, device_id_type=pl.DeviceIdType.MESH)
        copy.start()
        copy.wait_recv()
        got = land[slot]
        o_ref[...] = ((keep_ref[...] + got.astype(F32)) if reduce else got).astype(out_dtype)
        copy.wait_send()

        @pl.when(t < n_steps - 2)
        def _():
            pl.semaphore_signal(credit, inc=1, device_id=other, device_id_type=pl.DeviceIdType.MESH)

    if reduce:
        grid = (N_SHARD, nh)
        in_specs = [pl.BlockSpec((None, tr, cols), lambda j, i, pc: (j, pc[0] * nh + i, 0)),
                    pl.BlockSpec((None, tr, cols), lambda j, i, pc: (j, (1 - pc[0]) * nh + i, 0))]
        out_spec = pl.BlockSpec((None, tr, cols), lambda j, i, pc: (j, i, 0))
        out_shape = jax.ShapeDtypeStruct((N_SHARD, half, cols), out_dtype)
        operands = (a, a)
        sem = ("arbitrary", "arbitrary")
    else:
        grid = (nh,)
        in_specs = [pl.BlockSpec((tr, cols), lambda i, pc: (i, 0))]
        out_spec = pl.BlockSpec((tr, cols), lambda i, pc: (i, 0))
        out_shape = jax.ShapeDtypeStruct((half, cols), out_dtype)
        operands = (a,)
        sem = ("arbitrary",)
    return pl.pallas_call(
        body, name=name,
        grid_spec=pltpu.PrefetchScalarGridSpec(
            num_scalar_prefetch=1, grid=grid, in_specs=in_specs, out_specs=out_spec,
            scratch_shapes=[pltpu.VMEM((2, tr, cols), BF16 if reduce else a.dtype), pltpu.SemaphoreType.DMA((2,)),
                            pltpu.SemaphoreType.DMA((2,)), pltpu.SemaphoreType.REGULAR] +
                           ([pltpu.VMEM((tr, cols), BF16)] if reduce else [])),
        out_shape=out_shape, compiler_params=_params(sem),
    )(place, *operands)


def _cast_into_slot(name, a, chip):
    rows, cols = a.shape
    tm = _row_tile_for(rows, cols)

    def body(pc_ref, a_ref, o_ref):
        o_ref[...] = a_ref[...].astype(BF16)

    return pl.pallas_call(
        body, name=name,
        grid_spec=pltpu.PrefetchScalarGridSpec(
            num_scalar_prefetch=1, grid=(rows // tm,), in_specs=[pl.BlockSpec((tm, cols), lambda i, pc: (i, 0))],
            out_specs=pl.BlockSpec((None, tm, cols), lambda i, pc: (pc[0], i, 0))),
        out_shape=jax.ShapeDtypeStruct((N_SHARD, rows, cols), BF16), compiler_params=_params(("parallel",)),
    )(chip, a)


def _sum_landed(name, landed, sent, chip):
    ns, rows, cols = landed.shape
    tm = _row_tile_for(rows, cols)

    def body(pc_ref, *refs):
        own_ref, o_ref = refs[ns], refs[ns + 1]
        me = pc_ref[0]
        terms = [jnp.where(me == j, own_ref[...], refs[j][...]).astype(F32) for j in range(ns)]
        o_ref[...] = _slot_sum(terms)

    def landed_spec(j):
        return pl.BlockSpec((None, tm, cols), lambda i, pc: (jnp.where(pc[0] == j, (j + 1) % ns, j), i, 0))

    return pl.pallas_call(
        body, name=name,
        grid_spec=pltpu.PrefetchScalarGridSpec(
            num_scalar_prefetch=1, grid=(rows // tm,),
            in_specs=[landed_spec(j) for j in range(ns)] + [pl.BlockSpec((None, tm, cols), lambda i, pc: (pc[0], i, 0))],
            out_specs=pl.BlockSpec((tm, cols), lambda i, pc: (i, 0))),
        out_shape=jax.ShapeDtypeStruct((rows, cols), F32), compiler_params=_params(("parallel",)),
    )(chip, *([landed] * ns), sent)


def _gather_all(name, a):
    def body(in_ref, out_ref, send_sems, recv_sems, local_sem):
        x, y, c = _my_place()
        me = 4 * x + 2 * y + c

        def peer(k):
            return (x ^ (k >> 2), y ^ ((k >> 1) & 1), c ^ (k & 1))

        def remote(k, land):
            return pltpu.make_async_remote_copy(
                src_ref=in_ref, dst_ref=out_ref.at[land], send_sem=send_sems.at[k - 1], recv_sem=recv_sems.at[k - 1],
                device_id=peer(k), device_id_type=pl.DeviceIdType.MESH)

        local = pltpu.make_async_copy(in_ref, out_ref.at[me], local_sem)
        local.start()
        for k in range(1, N_DEV):
            remote(k, me).start()
        for k in range(1, N_DEV):
            px, py, pc = peer(k)
            remote(k, 4 * px + 2 * py + pc).wait_recv()
        for k in range(1, N_DEV):
            remote(k, me).wait_send()
        local.wait()

    hbm = pl.BlockSpec(memory_space=pl.ANY)
    return pl.pallas_call(
        body, name=name, in_specs=[hbm], out_specs=hbm,
        out_shape=jax.ShapeDtypeStruct((N_DEV,) + a.shape, a.dtype),
        scratch_shapes=[pltpu.SemaphoreType.DMA((N_DEV - 1,)), pltpu.SemaphoreType.DMA((N_DEV - 1,)), pltpu.SemaphoreType.DMA],
    )(a)


def _row_tile_for(rows, cols, budget=1 << 20):
    if rows * cols * 4 <= budget:
        return rows
    for tm in (1024, 512, 256, 128, 64, 32, 16, 8):
        if rows % tm == 0 and tm * cols * 4 <= budget:
            return tm
    return rows


def _slot_sum(vals):
    g = vals[0]
    for v in vals[1:]:
        g = g + v
    return g


def _rowwise(name, fn, arrays, out_dtype):
    rows, cols = arrays[0].shape
    tm = _row_tile_for(rows, cols)

    def body(*refs):
        refs[-1][...] = fn(*[r[...] for r in refs[:-1]]).astype(out_dtype)

    blk = pl.BlockSpec((tm, cols), lambda i: (i, 0))
    return pl.pallas_call(
        body, name=name, grid=(rows // tm,), in_specs=[blk] * len(arrays), out_specs=blk,
        out_shape=jax.ShapeDtypeStruct((rows, cols), out_dtype), compiler_params=_params(("parallel",)),
    )(*arrays)


def _sum_slots(name, st):
    ns, rows, cols = st.shape
    tm = _row_tile_for(rows, cols)

    def body(s_ref, o_ref):
        o_ref[...] = _slot_sum([s_ref[j].astype(F32) for j in range(ns)])

    return pl.pallas_call(
        body, name=name, grid=(rows // tm,),
        in_specs=[pl.BlockSpec((ns, tm, cols), lambda i: (0, i, 0))],
        out_specs=pl.BlockSpec((tm, cols), lambda i: (i, 0)),
        out_shape=jax.ShapeDtypeStruct((rows, cols), F32),
        compiler_params=_params(("parallel",)),
    )(st)


ADAM_TILE_BYTES = 1 << 20


def _adam_update(g, p_ref, m_ref, v_ref, go_ref, d_ref, mo_ref, vo_ref):
    mn = ADAM_B1 * m_ref[...] + (1.0 - ADAM_B1) * g
    vn = ADAM_B2 * v_ref[...] + (1.0 - ADAM_B2) * jnp.square(g)
    m_hat = mn / (1.0 - ADAM_B1 ** ADAM_STEP)
    v_hat = vn / (1.0 - ADAM_B2 ** ADAM_STEP)
    go_ref[...] = g
    d_ref[...] = -ADAM_LR * (m_hat / (jnp.sqrt(v_hat) + ADAM_EPS) + ADAM_WD * p_ref[...])
    mo_ref[...] = mn
    vo_ref[...] = vn


def _adamw(name, p, m, v, gst):
    rows, cols = p.shape
    ns = gst.shape[0]
    tm = _row_tile_for(rows, cols, budget=ADAM_TILE_BYTES)

    def body(p_ref, m_ref, v_ref, g_ref, *outs):
        _adam_update(_slot_sum([g_ref[j] for j in range(ns)]), p_ref, m_ref, v_ref, *outs)

    blk = pl.BlockSpec((tm, cols), lambda i: (i, 0))
    return pl.pallas_call(
        body, name=name, grid=(rows // tm,),
        in_specs=[blk, blk, blk, pl.BlockSpec((ns, tm, cols), lambda i: (0, i, 0))],
        out_specs=[blk] * 4, out_shape=[jax.ShapeDtypeStruct((rows, cols), F32)] * 4,
        compiler_params=_params(("parallel",)),
    )(p, m, v, gst)


def _adamw_halves(name, p, m, v, mine, theirs, place, scatter=()):
    rows, cols = p.shape
    half = rows // 2
    tm = _row_tile_for(half, cols, budget=ADAM_TILE_BYTES)
    nh = half // tm
    ns = len(scatter)

    def body(pc_ref, p_ref, m_ref, v_ref, mine_ref, theirs_ref, *refs):
        if ns:
            sc_refs = (refs[:ns], refs[ns + 4:2 * ns + 4]) + tuple(refs[2 * ns + 4:])
            at = pl.program_id(0) * nh + pl.program_id(1)
            pl.when(at == 0)(lambda: _scatter_start(scatter, *sc_refs))
        g = jnp.where(pl.program_id(0) == pc_ref[0], mine_ref[...], theirs_ref[...])
        _adam_update(g, p_ref, m_ref, v_ref, *refs[ns:ns + 4])
        if ns:
            pl.when(at == 2 * nh - 1)(lambda: _scatter_wait(scatter, *sc_refs))

    blk = pl.BlockSpec((tm, cols), lambda h, i, pc: (h * nh + i, 0))
    hblk = pl.BlockSpec((tm, cols), lambda h, i, pc: (i, 0))
    hbm = pl.BlockSpec(memory_space=pl.ANY)
    res = pl.pallas_call(
        body, name=name,
        grid_spec=pltpu.PrefetchScalarGridSpec(
            num_scalar_prefetch=1, grid=(2, nh), in_specs=[blk, blk, blk, hblk, hblk] + [hbm] * ns,
            out_specs=[blk] * 4 + [hbm] * ns,
            scratch_shapes=[pltpu.SemaphoreType.DMA((ns, _PEER_CHIPS))] * 2 if ns else []),
        out_shape=[jax.ShapeDtypeStruct((rows, cols), F32)] * 4 + [jax.ShapeDtypeStruct(s.shape, s.dtype) for s in scatter],
        compiler_params=_params(("arbitrary", "arbitrary") if ns else ("parallel", "parallel")),
    )(place, p, m, v, mine, theirs, *scatter)
    return (list(res[:4]), list(res[4:])) if ns else res


def _pack(parts, width=LANE, mult=8):
    flat = jnp.concatenate([a.reshape(-1) for a in parts])
    n = flat.shape[0]
    per = width * mult
    total = -(-n // per) * per
    return jnp.pad(flat, (0, total - n)).reshape(total // width, width)


def _unpack(packed, shapes):
    flat = packed.reshape(-1)
    out, off = [], 0
    for s in shapes:
        n = 1
        for d in s:
            n *= d
        out.append(flat[off:off + n].reshape(s))
        off += n
    return out


_SMALL_SHARDED = ("hg_lb", "rw_mu", "rw_w0", "rw_w2", "rw_a0", "rw_a2")
_REPLICATED = ("c_ctx", "ada_b", "norm_g", "hg_norm_g", "rw_kk", "rw_ka", "rw_rk", "rw_gn_g", "rw_gn_b", "final_g")
_GATHERED = ("w_in", "w_hg_out", "w_rw_out", "w_out")
_WEIGHTS = ("c_ctx", "ada_w", "ada_b", "norm_g", "w_in", "hg_lb", "hg_norm_g", "rw_mu", "rw_w0", "rw_w2", "rw_a0", "rw_a2",
            "rw_kk", "rw_ka", "rw_rk", "rw_gn_g", "rw_gn_b", "w_hg_out", "w_rw_out", "w_out", "final_g")


def _pack_small(d):
    return _pack([d[n] for n in _SMALL_SHARDED], mult=2 * ROW_ALIGN)


def _join_shards(st):
    a = jnp.moveaxis(st, 0, -2)
    return a.reshape(a.shape[:-2] + (a.shape[-2] * a.shape[-1],))


def _split_shards(a):
    s = a.reshape(a.shape[:-1] + (N_SHARD, a.shape[-1] // N_SHARD))
    return jnp.moveaxis(s, -2, 0)


def kernel(x, c, ctx, c_ctx, ada_w, ada_b, norm_g, w_in, hg_lb, hg_norm_g, rw_mu, rw_w0, rw_w2, rw_a0, rw_a2, rw_kk, rw_ka, rw_rk, rw_gn_g, rw_gn_b, w_hg_out, w_rw_out, w_out, final_g, loss_target, m_c_ctx, m_ada_w, m_ada_b, m_norm_g, m_w_in, m_hg_lb, m_hg_norm_g, m_rw_mu, m_rw_w0, m_rw_w2, m_rw_a0, m_rw_a2, m_rw_kk, m_rw_ka, m_rw_rk, m_rw_gn_g, m_rw_gn_b, m_w_hg_out, m_w_rw_out, m_w_out, m_final_g, v_c_ctx, v_ada_w, v_ada_b, v_norm_g, v_w_in, v_hg_lb, v_hg_norm_g, v_rw_mu, v_rw_w0, v_rw_w2, v_rw_a0, v_rw_a2, v_rw_kk, v_rw_ka, v_rw_rk, v_rw_gn_g, v_rw_gn_b, v_w_hg_out, v_w_rw_out, v_w_out, v_final_g):
    w = dict(c_ctx=c_ctx, ada_w=ada_w, ada_b=ada_b, norm_g=norm_g, w_in=w_in, hg_lb=hg_lb, hg_norm_g=hg_norm_g, rw_mu=rw_mu,
             rw_w0=rw_w0, rw_w2=rw_w2, rw_a0=rw_a0, rw_a2=rw_a2, rw_kk=rw_kk, rw_ka=rw_ka, rw_rk=rw_rk, rw_gn_g=rw_gn_g,
             rw_gn_b=rw_gn_b, w_hg_out=w_hg_out, w_rw_out=w_rw_out, w_out=w_out, final_g=final_g)
    m = dict(c_ctx=m_c_ctx, ada_w=m_ada_w, ada_b=m_ada_b, norm_g=m_norm_g, w_in=m_w_in, hg_lb=m_hg_lb, hg_norm_g=m_hg_norm_g,
             rw_mu=m_rw_mu, rw_w0=m_rw_w0, rw_w2=m_rw_w2, rw_a0=m_rw_a0, rw_a2=m_rw_a2, rw_kk=m_rw_kk, rw_ka=m_rw_ka,
             rw_rk=m_rw_rk, rw_gn_g=m_rw_gn_g, rw_gn_b=m_rw_gn_b, w_hg_out=m_w_hg_out, w_rw_out=m_w_rw_out, w_out=m_w_out,
             final_g=m_final_g)
    v = dict(c_ctx=v_c_ctx, ada_w=v_ada_w, ada_b=v_ada_b, norm_g=v_norm_g, w_in=v_w_in, hg_lb=v_hg_lb, hg_norm_g=v_hg_norm_g,
             rw_mu=v_rw_mu, rw_w0=v_rw_w0, rw_w2=v_rw_w2, rw_a0=v_rw_a0, rw_a2=v_rw_a2, rw_kk=v_rw_kk, rw_ka=v_rw_ka,
             rw_rk=v_rw_rk, rw_gn_g=v_rw_gn_g, rw_gn_b=v_rw_gn_b, w_hg_out=v_w_hg_out, w_rw_out=v_w_rw_out, w_out=v_w_out,
             final_g=v_final_g)

    def mat(a):
        return a.reshape(a.shape[-2], a.shape[-1])

    my_core = lax.axis_index("c").astype(jnp.int32).reshape(1)
    my_chip = (2 * lax.axis_index("x") + lax.axis_index("y")).astype(jnp.int32).reshape(1)

    my_dev = 2 * my_chip[0] + my_core[0]
    dm = x.shape[-1]
    ada_cols = ada_w.shape[-1]

    c_all = _gather_all("cond_gather", c.reshape(1, dm)).reshape(N_DEV, dm)
    cond16 = jnp.concatenate([c_all, c_ctx.reshape(1, dm), jnp.zeros((7, dm), F32)], axis=0)
    (sc16,) = _row_call("cond_silu", lambda i, r, f: ([jax.nn.silu(r[0])], []), 1, 16, [(cond16, 0, dm, 0)], [],
                        [(16, dm, F32, 0)], [])
    mod_here = _mm_nn("mod_mm", sc16, mat(ada_w))
    mod_all = _gather_all("mod_gather", mod_here)
    mod_rows = jnp.concatenate([mod_all[2 * j] for j in range(N_SHARD)], axis=1)
    mine = lax.dynamic_slice_in_dim(mod_rows, my_dev, 1, axis=0)
    mod = _add_small("mod_bias", [jnp.concatenate([mine, mod_rows[N_DEV:N_DEV + 1], jnp.zeros((6, 3 * dm), F32)], axis=0),
                                  jnp.broadcast_to(ada_b, (8, 3 * dm))], (8, 3 * dm))

    small_shapes = [w[n].shape for n in _SMALL_SHARDED]
    big_bf = [_cast_into_slot(f"to_bf16_{n}", mat(w[n]), my_chip) for n in _GATHERED]
    small_mine = _pack_small(w)
    small_slots = lax.dynamic_update_slice(jnp.zeros((N_SHARD,) + small_mine.shape, F32), small_mine[None], (my_chip[0], 0, 0))
    gathered = _weights_gather("weights_gather", big_bf, [small_slots])
    w_in_st, w_hg_st, w_rw_st, w_out_st, small_st = gathered
    full_small = {}
    per_chip = [_unpack(small_st[j], small_shapes) for j in range(N_SHARD)]
    for i, n in enumerate(_SMALL_SHARDED):
        full_small[n] = _join_shards(jnp.stack([per_chip[j][i] for j in range(N_SHARD)], axis=0))
    w_out_full = w_out_st.reshape(dm, dm)

    loss_b, grad_x, dmod, g = _local_step(
        x[0], ctx[0], mod, norm_g, w_in_st, full_small["hg_lb"], hg_norm_g, full_small["rw_mu"][0],
        full_small["rw_w0"][0], full_small["rw_w2"][0], full_small["rw_a0"][0], full_small["rw_a2"][0], rw_kk, rw_ka, rw_rk,
        rw_gn_g, rw_gn_b, w_hg_st, w_rw_st, w_out_full, final_g, loss_target[0], my_core)
    loss = lax.psum(loss_b[0, 0], ("x", "y", "c"))

    dmod_all = _gather_all("dmod_gather", dmod[0:2])
    dmod_here = lax.dynamic_slice_in_dim(dmod_all, my_chip[0] * ada_cols, ada_cols, axis=2)
    d_ctx_row = _add_small("d_mod_ctx", [dmod_here[j, 1:2] for j in range(N_DEV)], (1, ada_cols))
    dm16 = jnp.concatenate([dmod_here[:, 0], d_ctx_row, jnp.zeros((7, ada_cols), F32)], axis=0)
    g_ada_here = _mm_tn("d_ada_w", sc16, dm16)
    d_sc16 = _mm_nt("d_cond", dm16, mat(ada_w))

    def cond_bwd(i, r, f):
        _, vjp = jax.vjp(jax.nn.silu, r[0])
        return [vjp(r[1])[0]], []

    (d_cond16,) = _row_call("cond_bwd", cond_bwd, 1, 16, [(cond16, 0, dm, 0), (d_sc16, 0, dm, 0)], [], [(16, dm, F32, 0)], [])
    g["c_ctx"] = jnp.where(my_core[0] == 0, d_cond16[N_DEV], 0.0)
    g["ada_b"] = _add_small("g_ada_b", [dmod[0:1], dmod[1:2]], (1, 3 * dm))

    def finish(name, chip_sum, landed):
        half = _sum_landed(f"grads_sum_{name}", landed, chip_sum, my_chip)
        return half, _pair_exchange(f"grads_pair_swap_{name}", half, my_core, False, F32)

    res = {}
    rep_shapes = [w[n].shape for n in _REPLICATED]
    rep_all = _gather_all("grads_replicated", _pack([g[n].reshape(w[n].shape) for n in _REPLICATED]))
    outs = _adamw("adamw_ada_w", mat(ada_w), mat(m["ada_w"]), mat(v["ada_w"]), g_ada_here[None])
    res["ada_w"] = [o.reshape(ada_w.shape) for o in outs]
    for n in _GATHERED:
        outs = _adamw_halves(f"adamw_{n}", mat(w[n]), mat(m[n]), mat(v[n]), *finish(n, *g[n]), my_core)
        res[n] = [o.reshape(w[n].shape) for o in outs]
    outs = _adamw_halves("adamw_small", small_mine, _pack_small(m), _pack_small(v), *finish("small", *g["small"]), my_core)
    for i, vals in enumerate(zip(*[_unpack(o, small_shapes) for o in outs])):
        res[_SMALL_SHARDED[i]] = list(vals)
    outs = _adamw("adamw_replicated", _pack([w[n] for n in _REPLICATED]), _pack([m[n] for n in _REPLICATED]),
                  _pack([v[n] for n in _REPLICATED]), rep_all)
    for i, vals in enumerate(zip(*[_unpack(o, rep_shapes) for o in outs])):
        res[_REPLICATED[i]] = list(vals)

    return (loss, grad_x[None], *[res[n][0] for n in _WEIGHTS], *[res[n][1] for n in _WEIGHTS],
            *[res[n][2] for n in _WEIGHTS], *[res[n][3] for n in _WEIGHTS])
```

```python
import functools

import jax
import jax.numpy as jnp
from jax import lax
from jax.experimental import pallas as pl
from jax.experimental.pallas import tpu as pltpu

HI = lax.Precision.HIGHEST
F32 = jnp.float32
BF16 = jnp.bfloat16

NORM_EPS = 1e-6
HG_HEAD = 128
RW_HEAD = 64
RW_LORA = 64
RW_GN_EPS = 64e-5
GRID_W = 64
SUB = 16
RW_SUB = 16
STEP = 64
RW_STEP = 64
N_SHARD = 4
N_DEV = 8
LANE = 128

ADAM_LR = 0.001
ADAM_B1 = 0.9
ADAM_B2 = 0.999
ADAM_EPS = 1e-08
ADAM_WD = 0.01
ADAM_STEP = 10

VMEM_LIMIT = 56 * 1024 * 1024


def _params(sem=None):
    return pltpu.CompilerParams(dimension_semantics=sem, vmem_limit_bytes=VMEM_LIMIT)


def _tile(n, cands):
    for c in cands:
        if n % c == 0:
            return c
    return n


def _iota2(n, m, d):
    return lax.broadcasted_iota(jnp.int32, (n, m), d)


def _before(n, rev, strict):
    t, s = _iota2(n, n, 0), _iota2(n, n, 1)
    if rev:
        return (s > t) if strict else (s >= t)
    return (s < t) if strict else (s <= t)


def _running_sum(a, axis, rev):
    n = a.shape[axis]
    shift = 1
    while shift < n:
        pad = list(a.shape)
        pad[axis] = shift
        zeros = jnp.zeros(pad, a.dtype)
        if rev:
            moved = jnp.concatenate([lax.slice_in_dim(a, shift, n, axis=axis), zeros], axis=axis)
        else:
            moved = jnp.concatenate([zeros, lax.slice_in_dim(a, 0, n - shift, axis=axis)], axis=axis)
        a = a + moved
        shift *= 2
    return a


def _sdot(a, b, spec):
    return jnp.einsum(spec, a, b, precision=lax.Precision.DEFAULT, preferred_element_type=F32)


def _hg_step(s0, qraw, iin, fin, lb2, rev):
    c, w = qraw.shape
    h = w // HG_HEAD
    nsub = c // SUB
    lb = jax.nn.sigmoid(lb2[0:1] - lb2[1:2])
    q = jax.nn.silu(qraw)
    fg = lb + (1.0 - lb) * jax.nn.sigmoid(fin)
    kk = 1.0 - fg
    g = jnp.log(fg)
    bcum = _running_sum(g, 0, rev)
    def heads(a):
        return jnp.swapaxes(a.reshape(a.shape[0], h, HG_HEAD), 0, 1)

    def unheads(a):
        return jnp.swapaxes(a, 0, 1).reshape(a.shape[1], w)

    blocks = [slice(j * SUB, (j + 1) * SUB) for j in range(nsub)]
    outs = []
    for sl in blocks:
        qs, ks, vs, bc = [a[sl].reshape(SUB, h, HG_HEAD) for a in (q, kk, iin, bcum)]
        o = jnp.zeros((SUB, h, HG_HEAD), F32)
        for si in range(SUB):
            after = slice(0, si + 1) if rev else slice(si, SUB)
            dec = jnp.exp(jnp.minimum(bc[after] - bc[si:si + 1], 0.0))
            a = jnp.sum(qs[after] * ks[si:si + 1] * dec, axis=-1, keepdims=True)
            term = a * vs[si:si + 1]
            n_rest = SUB - 1 - si if rev else si
            if n_rest:
                rest = jnp.zeros((n_rest, h, HG_HEAD), F32)
                term = jnp.concatenate([term, rest] if rev else [rest, term], axis=0)
            o = o + term
        outs.append(o.reshape(SUB, w))
    order = list(range(nsub - 1, -1, -1)) if rev else list(range(nsub))
    for pos in range(1, nsub):
        j, before = order[pos], order[:pos]
        first = (j + 1) * SUB - 1 if rev else j * SUB
        bstart = bcum[first:first + 1] - g[first:first + 1]
        qp = heads(q[blocks[j]] * jnp.exp(bcum[blocks[j]] - bstart))
        kp = heads(jnp.concatenate([kk[blocks[p]] * jnp.exp(bstart - bcum[blocks[p]]) for p in before], axis=0))
        vp = heads(jnp.concatenate([iin[blocks[p]] for p in before], axis=0))
        outs[j] = outs[j] + unheads(_sdot(_sdot(qp, kp, 'htk,hsk->hts'), vp, 'hts,hsv->htv'))
    o_state = unheads(_sdot(heads(q * jnp.exp(bcum)), s0, 'htk,hvk->htv'))
    last = 0 if rev else c - 1
    blast = bcum[last:last + 1]
    s_new = heads(jnp.exp(blast)) * s0 + _sdot(heads(iin), heads(kk * jnp.exp(blast - bcum)), 'hsv,hsk->hvk')
    return jnp.concatenate(outs, axis=0) + o_state, s_new


def _tri_solve(lmat, rhs, rev):
    hh, c, _ = lmat.shape
    sub = RW_SUB
    nb = c // sub
    diag = jnp.concatenate([lmat[:, i * sub:(i + 1) * sub, i * sub:(i + 1) * sub] for i in range(nb)], axis=0)
    dt = jnp.transpose(diag, (1, 2, 0))
    col = lax.broadcasted_iota(jnp.int32, (sub, 1), 0)
    inv_rows = [None] * sub
    order = list(range(sub - 1, -1, -1)) if rev else list(range(sub))
    for pos, t in enumerate(order):
        row = jnp.broadcast_to((col == t).astype(F32), (sub, dt.shape[2]))
        for s in order[:pos]:
            row = row - dt[t, s:s + 1, :] * inv_rows[s]
        inv_rows[t] = row
    tinv = jnp.transpose(jnp.concatenate([r[None] for r in inv_rows], axis=0), (2, 0, 1))
    p = [None] * nb
    done = []
    for i in (range(nb - 1, -1, -1) if rev else range(nb)):
        r = rhs[:, i * sub:(i + 1) * sub]
        if done:
            lrow = jnp.concatenate([lmat[:, i * sub:(i + 1) * sub, m * sub:(m + 1) * sub] for m in done], axis=2)
            r = r - _sdot(lrow, jnp.concatenate([p[m] for m in done], axis=1), 'hts,hsv->htv')
        p[i] = _sdot(tinv[i * hh:(i + 1) * hh], r, 'hts,hsv->htv')
        done.append(i)
    return jnp.concatenate(p, axis=1)


def _rw_step(s0, r, k, v, wlo, alo, w0h, w2h, a0h, a2h, kkh, kah, rev):
    hh, c, _ = r.shape
    tl = jnp.broadcast_to(jnp.tanh(wlo)[None], (hh, c, wlo.shape[1]))
    al = jnp.broadcast_to(alo[None], (hh, c, alo.shape[1]))
    wlog = -jax.nn.softplus(-(w0h + _sdot(tl, w2h, 'hcl,hlj->hcj'))) - 0.5
    lw = -jnp.exp(wlog)
    a = jax.nn.sigmoid(a0h + _sdot(al, a2h, 'hcl,hlj->hcj'))
    kk = k * kkh
    kk = kk * lax.rsqrt(jnp.sum(kk * kk, axis=-1, keepdims=True) + 1e-12)
    kd = k * (1.0 + (a - 1.0) * kah)
    b = kk * a
    cum = _running_sum(lw, 1, rev)
    ecum, encum = jnp.exp(cum), jnp.exp(-cum)
    alpha = jnp.exp(cum - lw) * kk
    beta = b * encum
    kappa = kd * encum
    rho = r * ecum
    m_lt = _before(c, rev, True)[None]
    m_le = _before(c, rev, False)[None]
    ar = jnp.concatenate([alpha, rho], axis=1)
    kb = jnp.concatenate([kappa, beta], axis=1)
    gram = _sdot(ar, kb, 'htk,hsk->hts')
    a_kap = jnp.where(m_lt, gram[:, :c, :c], 0.0)
    a_bet = jnp.where(m_lt, gram[:, :c, c:], 0.0)
    b_kap = jnp.where(m_le, gram[:, c:, :c], 0.0)
    b_bet = jnp.where(m_le, gram[:, c:, c:], 0.0)
    from_state = _sdot(ar, s0, 'htk,hvk->htv')
    p = _tri_solve(a_bet, from_state[:, :c] + _sdot(a_kap, v, 'hts,hsv->htv'), rev)
    vp = jnp.concatenate([v, -p], axis=1)
    y = from_state[:, c:] + _sdot(jnp.concatenate([b_kap, b_bet], axis=2), vp, 'hts,hsv->htv')
    stil = s0 + _sdot(vp, kb, 'hsv,hsk->hvk')
    last = 0 if rev else c - 1
    return y, stil * ecum[:, last:last + 1, :]


def _fn_h(s, norm_g, scale, shift):
    return s * lax.rsqrt(jnp.mean(s * s, axis=-1, keepdims=True) + NORM_EPS) * norm_g * (1.0 + scale) + shift


def _fn_hgpost(of, ob, z, g):
    tm, w = of.shape
    o = (of + ob).reshape(tm, w // HG_HEAD, HG_HEAD)
    o = o * lax.rsqrt(jnp.mean(o * o, axis=-1, keepdims=True) + NORM_EPS)
    return o.reshape(tm, w) * g * jax.nn.silu(z)


def _fn_rwpost(y0, y1, r, k, v, alo, z, a0, a2, k_a, r_k, gn_g, gn_b):
    tm, w = r.shape
    nh = w // RW_HEAD
    asum = 0.0
    for d in range(2):
        asum = asum + jax.nn.sigmoid(a0[d:d + 1] + jnp.dot(alo[:, d * RW_LORA:(d + 1) * RW_LORA], a2[d],
                                                           precision=HI, preferred_element_type=F32))
    k_sum = k * (2.0 + (asum - 2.0) * k_a)
    ys = (y0 + y1).reshape(tm, nh, RW_HEAD)
    mean = jnp.mean(ys, axis=-1, keepdims=True)
    var = jnp.mean(jnp.square(ys - mean), axis=-1, keepdims=True)
    y = ((ys - mean) * lax.rsqrt(var + RW_GN_EPS)).reshape(tm, w) * gn_g + gn_b
    bonus = jnp.sum((r * k_sum * r_k).reshape(tm, nh, RW_HEAD), axis=-1, keepdims=True) * v.reshape(tm, nh, RW_HEAD)
    return (y + bonus.reshape(tm, w)) * jax.nn.silu(z)


def _fn_merge(a, b, ghg, grw):
    return jax.nn.sigmoid(ghg) * a + jax.nn.sigmoid(grw) * b


def _fn_final(xs, o, gate, final_g, tgt):
    x2 = xs + gate * o
    y = x2 * lax.rsqrt(jnp.mean(x2 * x2, axis=-1, keepdims=True) + NORM_EPS) * final_g
    return 0.5 * jnp.sum(jnp.mean(jnp.square(y - tgt), axis=-1))


def _row_call(name, fn, n_tiles, tm, row_ins, full_ins, row_outs, acc_outs):
    n_ri, n_fi, n_ro = len(row_ins), len(full_ins), len(row_outs)

    def body(*refs):
        i = pl.program_id(0)
        rvals = [r[...] for r in refs[:n_ri]]
        fvals = [r[...] for r in refs[n_ri:n_ri + n_fi]]
        outs = refs[n_ri + n_fi:]
        ro, ao = fn(i, rvals, fvals)
        for ref, val in zip(outs[:n_ro], ro):
            ref[...] = val.astype(ref.dtype)
        for ref, val in zip(outs[n_ro:], ao):
            @pl.when(i == 0)
            def _(ref=ref):
                ref[...] = jnp.zeros_like(ref)
            ref[...] += val.astype(ref.dtype)

    def rspec(width, cb, off, rows):
        return pl.BlockSpec((tm, width), lambda i: (jnp.clip(i - off, 0, rows // tm - 1), cb))

    def fspec(shape):
        nd = len(shape)
        return pl.BlockSpec(shape, lambda i: (0,) * nd)

    in_specs = [rspec(w, cb, off, a.shape[0]) for (a, cb, w, off) in row_ins] + [fspec(a.shape) for a in full_ins]
    out_specs = [rspec(w, 0, off, rows) for (rows, w, _, off) in row_outs] + [fspec(s) for (s, _) in acc_outs]
    out_shape = [jax.ShapeDtypeStruct((rows, w), dt) for (rows, w, dt, _) in row_outs] + \
                [jax.ShapeDtypeStruct(s, dt) for (s, dt) in acc_outs]
    res = pl.pallas_call(
        body, name=name, grid=(n_tiles,), in_specs=in_specs, out_specs=out_specs, out_shape=out_shape,
        compiler_params=_params(("arbitrary",)),
    )(*[a for (a, _, _, _) in row_ins], *full_ins)
    return list(res)


def _mm(name, a, b, m, n, k_steps, tm, tn, a_block, a_map, b_block, b_map, o_shape, o_block, o_map,
        contract, out_dtype=F32, scatter=()):
    ns = len(scatter)
    grid = (m // tm, n // tn, k_steps)

    def body(*refs):
        a_ref, b_ref, o_ref, acc_ref = refs[0], refs[1], refs[2 + ns], refs[3 + 2 * ns]
        kk = pl.program_id(2)
        if ns:
            sc_refs = (refs[2:2 + ns], refs[3 + ns:3 + 2 * ns]) + tuple(refs[4 + 2 * ns:])
            at = (pl.program_id(0) * grid[1] + pl.program_id(1)) * grid[2] + kk
            pl.when(at == 0)(lambda: _scatter_start(scatter, *sc_refs))

        @pl.when(kk == 0)
        def _():
            acc_ref[...] = jnp.zeros_like(acc_ref)

        acc_ref[...] += lax.dot_general(a_ref[...].astype(BF16), b_ref[...].astype(BF16),
                                        (contract, ((), ())), preferred_element_type=F32)

        @pl.when(kk == k_steps - 1)
        def _():
            o_ref[...] = acc_ref[...].astype(o_ref.dtype)

        if ns:
            pl.when(at == grid[0] * grid[1] * grid[2] - 1)(lambda: _scatter_wait(scatter, *sc_refs))

    hbm = pl.BlockSpec(memory_space=pl.ANY)
    sems = [pltpu.SemaphoreType.DMA((ns, _PEER_CHIPS))] * 2 if ns else []
    res = pl.pallas_call(
        body, name=name, grid=grid,
        in_specs=[pl.BlockSpec(a_block, a_map), pl.BlockSpec(b_block, b_map)] + [hbm] * ns,
        out_specs=[pl.BlockSpec(o_block, o_map)] + [hbm] * ns,
        out_shape=[jax.ShapeDtypeStruct(o_shape, out_dtype)] + [jax.ShapeDtypeStruct(s.shape, s.dtype) for s in scatter],
        scratch_shapes=[pltpu.VMEM((tm, tn), F32)] + sems,
        compiler_params=_params(("arbitrary",) * 3 if ns else ("parallel", "parallel", "arbitrary")),
    )(a, b, *scatter)
    return (res[0], list(res[1:])) if ns else res[0]


_TM = (768, 512, 256, 128, 64, 32, 16, 8)
_TN = (512, 256, 128)
_TK = (1024, 768, 512, 256, 128)
_TK_WIDE = (768, 512, 256, 128)
WIDE_OUT_BYTES = 32 << 20


def _tm_wide(m, ns):
    for tm in _TM:
        if m % tm == 0 and 3 * 4 * tm * ns <= WIDE_OUT_BYTES:
            return tm
    return m


def _mm_nn(name, a, b, out_dtype=F32):
    m, k = a.shape
    n = b.shape[1]
    tm, tn, tk = _tile(m, _TM), _tile(n, _TN), _tile(k, _TK)
    return _mm(name, a, b, m, n, k // tk, tm, tn, (tm, tk), lambda i, j, s: (i, s), (tk, tn), lambda i, j, s: (s, j),
               (m, n), (tm, tn), lambda i, j, s: (i, j), ((1,), (0,)), out_dtype)


def _mm_nt(name, a, b, out_dtype=F32):
    m, k = a.shape
    n = b.shape[0]
    tm, tn, tk = _tile(m, _TM), _tile(n, _TN), _tile(k, _TK)
    return _mm(name, a, b, m, n, k // tk, tm, tn, (tm, tk), lambda i, j, s: (i, s), (tn, tk), lambda i, j, s: (j, s),
               (m, n), (tm, tn), lambda i, j, s: (i, j), ((1,), (1,)), out_dtype)


def _mm_tn(name, a, b, out_dtype=F32):
    k, m = a.shape
    n = b.shape[1]
    tm, tn, tk = _tile(m, _TM), _tile(n, _TN), _tile(k, _TK)
    return _mm(name, a, b, m, n, k // tk, tm, tn, (tk, tm), lambda i, j, s: (s, i), (tk, tn), lambda i, j, s: (s, j),
               (m, n), (tm, tn), lambda i, j, s: (i, j), ((0,), (0,)), out_dtype)


def _mm_n_st(name, a, bst, out_dtype=F32, joined=False):
    m, k = a.shape
    ns_, _, ns = bst.shape
    tm, tk = _tm_wide(m, ns), _tile(k, (512, 256, 128))
    out = ((m, ns_ * ns), (tm, ns), lambda i, j, s: (i, j)) if joined else \
          ((ns_, m, ns), (None, tm, ns), lambda i, j, s: (j, i, 0))
    return _mm(name, a, bst, m, ns_ * ns, k // tk, tm, ns,
               (tm, tk), lambda i, j, s: (i, s), (None, tk, ns), lambda i, j, s: (j, s, 0), *out, ((1,), (0,)), out_dtype)


def _mm_st_t(name, ast, bst, out_dtype=F32, scatter=()):
    ns_, n, ns = bst.shape
    m = ast.shape[-2]
    tm, tn = _tile(m, _TM), _tile(n, _TN)
    a_side = ((None, tm, ns), lambda i, j, s: (s, i, 0)) if ast.ndim == 3 else ((tm, ns), lambda i, j, s: (i, s))
    return _mm(name, ast, bst, m, n, ns_, tm, tn, *a_side, (None, tn, ns), lambda i, j, s: (s, j, 0),
               (m, n), (tm, tn), lambda i, j, s: (i, j), ((1,), (1,)), out_dtype, scatter)


def _mm_t_st(name, a, bst, out_dtype=F32, scatter=(), n_shard=N_SHARD):
    k, m = a.shape
    ns = bst.shape[-1] if bst.ndim == 3 else bst.shape[-1] // n_shard
    tm, tk = _tile(m, _TN), _tile(k, _TK_WIDE)
    b_side = ((None, tk, ns), lambda i, j, s: (j, s, 0)) if bst.ndim == 3 else ((tk, ns), lambda i, j, s: (s, j))
    return _mm(name, a, bst, m, n_shard * ns, k // tk, tm, ns, (tk, tm), lambda i, j, s: (s, i), *b_side,
               (n_shard, m, ns), (None, tm, ns), lambda i, j, s: (j, i, 0), ((0,), (0,)), out_dtype, scatter)


RELAYOUT_TILE_BYTES = 12 << 20


def _join_columns(name, st):
    ns_, t, ns = st.shape
    tm = _row_tile_for(t, ns_ * ns, budget=RELAYOUT_TILE_BYTES)

    def body(s_ref, o_ref):
        o_ref[...] = jnp.concatenate([s_ref[j] for j in range(ns_)], axis=1)

    return pl.pallas_call(
        body, name=name, grid=(t // tm,), in_specs=[pl.BlockSpec((ns_, tm, ns), lambda i: (0, i, 0))],
        out_specs=pl.BlockSpec((tm, ns_ * ns), lambda i: (i, 0)),
        out_shape=jax.ShapeDtypeStruct((t, ns_ * ns), st.dtype), compiler_params=_params(("parallel",)),
    )(st)


def _split_columns(name, pieces, n_shard):
    t = pieces[0].shape[0]
    n = sum(p.shape[1] for p in pieces)
    ns = n // n_shard
    tm = _row_tile_for(t, n, budget=RELAYOUT_TILE_BYTES)
    npc = len(pieces)

    def body(*refs):
        full = jnp.concatenate([r[...] for r in refs[:npc]], axis=1)
        for j in range(n_shard):
            refs[npc][j] = full[:, j * ns:(j + 1) * ns]

    return pl.pallas_call(
        body, name=name, grid=(t // tm,),
        in_specs=[pl.BlockSpec((tm, p.shape[1]), lambda i: (i, 0)) for p in pieces],
        out_specs=pl.BlockSpec((n_shard, tm, ns), lambda i: (0, i, 0)),
        out_shape=jax.ShapeDtypeStruct((n_shard, t, ns), pieces[0].dtype), compiler_params=_params(("parallel",)),
    )(*pieces)


def _scan_order(j, n_ctx, n_all, rev):
    if not rev:
        return j
    return jnp.where(j < n_ctx, n_ctx - 1 - j, n_all - 1 - (j - n_ctx))


def _hg_scan_fwd(name, p_hg, lb2, d, n_ctx):
    t, w = p_hg.shape[0], lb2.shape[1]
    h = w // HG_HEAD
    n = t // STEP
    rev = d == 1

    def body(q_ref, i_ref, f_ref, lb_ref, o_ref, st_ref, s_ref):
        j = pl.program_id(0)

        @pl.when(j == 0)
        def _():
            s_ref[...] = jnp.zeros_like(s_ref)

        s0 = s_ref[...]
        st_ref[...] = s0
        o, s1 = _hg_step(s0, q_ref[...], i_ref[...], f_ref[...], lb_ref[...], rev)
        o_ref[...] = o
        s_ref[...] = s1

    def rows(cb):
        return pl.BlockSpec((STEP, w), lambda j: (_scan_order(j, n_ctx, n, rev), cb))

    return pl.pallas_call(
        body, name=name, grid=(n,),
        in_specs=[rows(0), rows(1), rows(2 + d), pl.BlockSpec((2, w), lambda j: (0, 0))],
        out_specs=[rows(0), pl.BlockSpec((None, h, HG_HEAD, HG_HEAD), lambda j: (j, 0, 0, 0))],
        out_shape=[jax.ShapeDtypeStruct((t, w), F32), jax.ShapeDtypeStruct((n, h, HG_HEAD, HG_HEAD), F32)],
        scratch_shapes=[pltpu.VMEM((h, HG_HEAD, HG_HEAD), F32)],
        compiler_params=_params(("arbitrary",)),
    )(p_hg, p_hg, p_hg, lb2)


def _hg_scan_bwd(name, p_hg, lb2, states, do, d, n_ctx, other=()):
    t, w = p_hg.shape[0], lb2.shape[1]
    h = w // HG_HEAD
    n = t // STEP
    rev = d == 1
    no = len(other)

    def body(q_ref, i_ref, f_ref, lb_ref, st_ref, do_ref, *refs):
        dq_ref, di_ref, df_ref, dlb_ref, ds_ref = refs[no:]
        step = pl.program_id(0)

        @pl.when(step == 0)
        def _():
            ds_ref[...] = jnp.zeros_like(ds_ref)
            dlb_ref[...] = jnp.zeros_like(dlb_ref)

        _, vjp = jax.vjp(lambda s0, q, i, f, lb: _hg_step(s0, q, i, f, lb, rev),
                         st_ref[...], q_ref[...], i_ref[...], f_ref[...], lb_ref[...])
        ds0, dq, di, df, dlb = vjp((do_ref[...], ds_ref[...]))
        if no:
            dq, di = refs[0][...] + dq, refs[1][...] + di
        dq_ref[...] = dq.astype(dq_ref.dtype)
        di_ref[...] = di.astype(di_ref.dtype)
        df_ref[...] = df.astype(df_ref.dtype)
        dlb_ref[...] += dlb
        ds_ref[...] = ds0

    def rows(cb):
        return pl.BlockSpec((STEP, w), lambda s: (_scan_order(n - 1 - s, n_ctx, n, rev), cb))

    qi = BF16 if no else F32
    return pl.pallas_call(
        body, name=name, grid=(n,),
        in_specs=[rows(0), rows(1), rows(2 + d), pl.BlockSpec((2, w), lambda s: (0, 0)),
                  pl.BlockSpec((None, h, HG_HEAD, HG_HEAD), lambda s: (n - 1 - s, 0, 0, 0)), rows(0)] + [rows(0)] * no,
        out_specs=[rows(0), rows(0), rows(0), pl.BlockSpec((2, w), lambda s: (0, 0))],
        out_shape=[jax.ShapeDtypeStruct((t, w), qi)] * 2 + [jax.ShapeDtypeStruct((t, w), BF16),
                                                            jax.ShapeDtypeStruct((2, w), F32)],
        scratch_shapes=[pltpu.VMEM((h, HG_HEAD, HG_HEAD), F32)],
        compiler_params=_params(("arbitrary",)),
    )(p_hg, p_hg, p_hg, lb2, states, do, *other)


def _to_heads(a, nh):
    return jnp.stack([a[:, i * RW_HEAD:(i + 1) * RW_HEAD] for i in range(nh)], axis=0)


def _from_heads(a):
    return jnp.concatenate([a[i] for i in range(a.shape[0])], axis=-1)


def _rw_scan_fwd(name, sh, hp, d, n_ctx):
    t = sh.shape[0]
    w = (sh.shape[1] - 4 * RW_LORA) // 3
    nh = w // RW_HEAD
    n = t // RW_STEP
    rev = d == 1
    lo = 3 * w // LANE

    def body(r_ref, k_ref, v_ref, wl_ref, al_ref, w0_ref, w2_ref, a0_ref, a2_ref, kk_ref, ka_ref,
             y_ref, st_ref, s_ref):
        j = pl.program_id(0)

        @pl.when(j == 0)
        def _():
            s_ref[...] = jnp.zeros_like(s_ref)

        s0 = s_ref[...]
        st_ref[...] = s0
        wl = wl_ref[...][:, d * RW_LORA:(d + 1) * RW_LORA]
        al = al_ref[...][:, d * RW_LORA:(d + 1) * RW_LORA]
        y, s1 = _rw_step(s0, _to_heads(r_ref[...], nh), _to_heads(k_ref[...], nh), _to_heads(v_ref[...], nh), wl, al,
                         w0_ref[...], w2_ref[...], a0_ref[...], a2_ref[...], kk_ref[...], ka_ref[...], rev)
        y_ref[...] = _from_heads(y)
        s_ref[...] = s1

    def rows(cb, width=w):
        return pl.BlockSpec((RW_STEP, width), lambda j: (_scan_order(j, n_ctx, n, rev), cb))

    def whole(a):
        nd = a.ndim
        return pl.BlockSpec(a.shape, lambda j: (0,) * nd)

    return pl.pallas_call(
        body, name=name, grid=(n,),
        in_specs=[rows(0), rows(1), rows(2), rows(lo, LANE), rows(lo + 1, LANE)] + [whole(a) for a in hp],
        out_specs=[rows(0), pl.BlockSpec((None, nh, RW_HEAD, RW_HEAD), lambda j: (j, 0, 0, 0))],
        out_shape=[jax.ShapeDtypeStruct((t, w), F32), jax.ShapeDtypeStruct((n, nh, RW_HEAD, RW_HEAD), F32)],
        scratch_shapes=[pltpu.VMEM((nh, RW_HEAD, RW_HEAD), F32)],
        compiler_params=_params(("arbitrary",)),
    )(sh, sh, sh, sh, sh, *hp)


def _rw_scan_bwd_both(name, sh, hps, states, dy, n_ctx):
    t = sh.shape[0]
    w = (sh.shape[1] - 4 * RW_LORA) // 3
    nh = w // RW_HEAD
    n = t // RW_STEP
    lo = 3 * w // LANE
    n_in, n_p = 13, 6

    def body(*refs):
        step = pl.program_id(0)
        ins = [refs[d * n_in:(d + 1) * n_in] for d in range(2)]
        outs = [refs[2 * n_in + d * (1 + n_p):2 * n_in + (d + 1) * (1 + n_p)] for d in range(2)]
        ds_refs = refs[2 * n_in + 2 * (1 + n_p):]

        @pl.when(step == 0)
        def _():
            for d in range(2):
                ds_refs[d][...] = jnp.zeros_like(ds_refs[d])
                for ref in outs[d][1:]:
                    ref[...] = jnp.zeros_like(ref)

        for d in range(2):
            r_ref, k_ref, v_ref, wl_ref, al_ref = ins[d][:5]
            hp_refs, st_ref, dy_ref = ins[d][5:11], ins[d][11], ins[d][12]
            wl = wl_ref[...][:, d * RW_LORA:(d + 1) * RW_LORA]
            al = al_ref[...][:, d * RW_LORA:(d + 1) * RW_LORA]
            _, vjp = jax.vjp(functools.partial(_rw_step, rev=d == 1),
                             st_ref[...], _to_heads(r_ref[...], nh), _to_heads(k_ref[...], nh), _to_heads(v_ref[...], nh),
                             wl, al, *[p[...] for p in hp_refs])
            g = vjp((_to_heads(dy_ref[...], nh), ds_refs[d][...]))
            ds_refs[d][...] = g[0]
            zero = jnp.zeros_like(g[4])
            lora = [zero] * 4
            lora[d], lora[2 + d] = g[4], g[5]
            outs[d][0][...] = jnp.concatenate([_from_heads(g[1]), _from_heads(g[2]), _from_heads(g[3])] + lora, axis=-1)
            for ref, val in zip(outs[d][1:], g[6:]):
                ref[...] += val

    def rows(d, cb, width=w):
        return pl.BlockSpec((RW_STEP, width), lambda s: (_scan_order(n - 1 - s, n_ctx, n, d == 1), cb))

    def whole(a):
        nd = a.ndim
        return pl.BlockSpec(a.shape, lambda s: (0,) * nd)

    in_specs, operands, out_specs, out_shape = [], [], [], []
    for d in range(2):
        in_specs += [rows(d, 0), rows(d, 1), rows(d, 2), rows(d, lo, LANE), rows(d, lo + 1, LANE)]
        in_specs += [whole(a) for a in hps[d]]
        in_specs += [pl.BlockSpec((None, nh, RW_HEAD, RW_HEAD), lambda s: (n - 1 - s, 0, 0, 0)), rows(d, 0)]
        operands += [sh] * 5 + list(hps[d]) + [states[d], dy]
        out_specs += [rows(d, 0, sh.shape[1])] + [whole(a) for a in hps[d]]
        out_shape += [jax.ShapeDtypeStruct(sh.shape, F32)] + [jax.ShapeDtypeStruct(a.shape, F32) for a in hps[d]]
    res = pl.pallas_call(
        body, name=name, grid=(n,), in_specs=in_specs, out_specs=out_specs, out_shape=out_shape,
        scratch_shapes=[pltpu.VMEM((nh, RW_HEAD, RW_HEAD), F32)] * 2, compiler_params=_params(("arbitrary",)),
    )(*operands)
    return [res[0], res[1 + n_p]], [res[1:1 + n_p], res[2 + n_p:]]


def _shift_masks(t, n_ctx_rows):
    row = lax.broadcasted_iota(jnp.int32, (t, 1), 0)
    isx = row >= n_ctx_rows
    pos = jnp.where(isx, row - n_ctx_rows, row)
    col = jnp.where(isx, jnp.bitwise_and(pos, GRID_W - 1), pos)
    ncol = jnp.where(isx, GRID_W, n_ctx_rows)
    n_x = t - n_ctx_rows
    ml = col != 0
    mr = col != ncol - 1
    mu = isx & (pos >= GRID_W)
    md = isx & (pos < n_x - GRID_W)
    return ml, mr, mu, md, isx


def _shift_fwd(name, p, col0, mu, n_ctx_rows):
    t, c = p.shape[0], mu.shape[1]
    cw = LANE

    def body(p_ref, mu_ref, o_ref):
        x = p_ref[...]
        m = mu_ref[...]
        ml, mr, mup, mdn, isx = _shift_masks(t, n_ctx_rows)
        left = jnp.where(ml, pltpu.roll(x, 1, 0), 0.0)
        right = jnp.where(mr, pltpu.roll(x, t - 1, 0), 0.0)
        up = jnp.where(mup, pltpu.roll(x, GRID_W, 0), 0.0)
        down = jnp.where(mdn, pltpu.roll(x, t - GRID_W, 0), 0.0)
        out = x + m[0:1] * (left - x) + m[1:2] * (right - x)
        vert = m[2:3] * (up - x) + m[3:4] * (down - x)
        o_ref[...] = out + jnp.where(isx, vert, 0.0)

    return pl.pallas_call(
        body, name=name, grid=(c // cw,),
        in_specs=[pl.BlockSpec((t, cw), lambda j: (0, col0 + j)), pl.BlockSpec((4, cw), lambda j: (0, j))],
        out_specs=pl.BlockSpec((t, cw), lambda j: (0, j)),
        out_shape=jax.ShapeDtypeStruct((t, c), F32),
        compiler_params=_params(("parallel",)),
    )(p, mu)


def _shift_bwd(name, p, col0, mu, dparts, n_ctx_rows):
    t, c = p.shape[0], mu.shape[1]
    cw = LANE
    npart = len(dparts)

    def body(*refs):
        p_ref, mu_ref = refs[0], refs[1]
        dp_ref, dmu_ref = refs[2 + npart], refs[3 + npart]
        x = p_ref[...]
        m = mu_ref[...]
        g = refs[2][...]
        for r in refs[3:2 + npart]:
            g = g + r[...]
        ml, mr, mup, mdn, isx = _shift_masks(t, n_ctx_rows)
        left = jnp.where(ml, pltpu.roll(x, 1, 0), 0.0)
        right = jnp.where(mr, pltpu.roll(x, t - 1, 0), 0.0)
        up = jnp.where(mup, pltpu.roll(x, GRID_W, 0), 0.0)
        down = jnp.where(mdn, pltpu.roll(x, t - GRID_W, 0), 0.0)
        gx = jnp.where(isx, g, 0.0)
        dmu_ref[...] = jnp.concatenate([
            jnp.sum(g * (left - x), axis=0, keepdims=True), jnp.sum(g * (right - x), axis=0, keepdims=True),
            jnp.sum(gx * (up - x), axis=0, keepdims=True), jnp.sum(gx * (down - x), axis=0, keepdims=True)], axis=0)
        coef = 1.0 - m[0:1] - m[1:2] - jnp.where(isx, m[2:3] + m[3:4], 0.0)
        dp = coef * g
        dp = dp + m[0:1] * pltpu.roll(jnp.where(ml, g, 0.0), t - 1, 0)
        dp = dp + m[1:2] * pltpu.roll(jnp.where(mr, g, 0.0), 1, 0)
        dp = dp + m[2:3] * pltpu.roll(jnp.where(mup, g, 0.0), t - GRID_W, 0)
        dp = dp + m[3:4] * pltpu.roll(jnp.where(mdn, g, 0.0), GRID_W, 0)
        dp_ref[...] = dp.astype(dp_ref.dtype)

    col = pl.BlockSpec((t, cw), lambda j: (0, j))
    par = pl.BlockSpec((4, cw), lambda j: (0, j))
    return pl.pallas_call(
        body, name=name, grid=(c // cw,),
        in_specs=[pl.BlockSpec((t, cw), lambda j: (0, col0 + j)), par] + [col] * npart,
        out_specs=[col, par],
        out_shape=[jax.ShapeDtypeStruct((t, c), BF16), jax.ShapeDtypeStruct((4, c), F32)],
        compiler_params=_params(("parallel",)),
    )(p, mu, *dparts)


def _add_small(name, terms, shape):
    flat2 = [a.reshape(-1, a.shape[-1]) for a in terms]
    return _rowwise(name, lambda *v: _slot_sum(list(v)), flat2, F32).reshape(shape)


def _local_step(x, ctx, mod, norm_g, w_in_st, hg_lb, hg_norm_g, rw_mu, rw_w0, rw_w2, rw_a0, rw_a2,
                rw_kk, rw_ka, rw_rk, rw_gn_g, rw_gn_b, w_hg_st, w_rw_st, w_out, final_g, tgt, my_core):
    seq, dm = x.shape
    n_ctx_rows = ctx.shape[0]
    t = seq + n_ctx_rows
    hw = hg_norm_g.shape[-1]
    rw = rw_kk.shape[-1]
    nh_rw = rw // RW_HEAD
    n_ctx = n_ctx_rows // STEP
    tm = _tile(n_ctx_rows, (256, 128, 64))
    nt = t // tm
    nct = n_ctx_rows // tm
    n_sh_cols = 3 * rw + 4 * RW_LORA

    final_g2 = final_g.reshape(1, dm)
    add = _add_small
    mod3 = mod.reshape(8, 3, dm)

    def pick(i, m3):
        r = jnp.where(i < nct, m3[1], m3[0])
        return r[0:1], r[1:2]

    tokens = [(ctx, 0, dm, 0), (x, 0, dm, nct)]

    def h_fn(i, r, f):
        shift, scale = pick(i, f[1])
        return [_fn_h(jnp.where(i < nct, r[0], r[1]), f[0], scale, shift)], []

    (h,) = _row_call("h_fwd", h_fn, nt, tm, tokens, [norm_g, mod3], [(t, dm, BF16, 0)], [])
    proj = _join_columns("proj_join", _mm_n_st("proj_mm", h, w_in_st))
    p_hg = proj
    rs_tile0 = 5 * hw // LANE
    p_zr = proj[:, 5 * hw + n_sh_cols:5 * hw + n_sh_cols + rw]
    p_gt = proj[:, 5 * hw + n_sh_cols + rw:]

    o_hg, st_hg = [], []
    for d in range(2):
        o, st = _hg_scan_fwd(f"hg_scan_fwd{d}", p_hg, hg_lb[d], d, n_ctx)
        o_hg.append(o)
        st_hg.append(st)

    def hgpost_fn(i, r, f):
        return [_fn_hgpost(r[0], r[1], r[2], f[0])], []

    hg_in = [(o_hg[0], 0, hw, 0), (o_hg[1], 0, hw, 0), (p_hg, 4, hw, 0)]
    (y_hg,) = _row_call("hg_post", hgpost_fn, nt, tm, hg_in, [hg_norm_g], [(t, hw, BF16, 0)], [])

    sh = _shift_fwd("rw_shift", proj, rs_tile0, rw_mu, n_ctx_rows)
    hps = []
    for d in range(2):
        hps.append([rw_w0[d].reshape(nh_rw, 1, RW_HEAD), jnp.swapaxes(rw_w2[d].reshape(RW_LORA, nh_rw, RW_HEAD), 0, 1),
                    rw_a0[d].reshape(nh_rw, 1, RW_HEAD), jnp.swapaxes(rw_a2[d].reshape(RW_LORA, nh_rw, RW_HEAD), 0, 1),
                    rw_kk.reshape(nh_rw, 1, RW_HEAD), rw_ka.reshape(nh_rw, 1, RW_HEAD)])
    y_rw_d, st_rw = [], []
    for d in range(2):
        y, st = _rw_scan_fwd(f"rw_scan_fwd{d}", sh, hps[d], d, n_ctx_rows // RW_STEP)
        y_rw_d.append(y)
        st_rw.append(st)

    rw_full = [rw_a0, rw_a2, rw_ka, rw_rk, rw_gn_g, rw_gn_b]
    lo = 3 * rw // LANE
    rw_in = [(y_rw_d[0], 0, rw, 0), (y_rw_d[1], 0, rw, 0), (sh, 0, rw, 0), (sh, 1, rw, 0), (sh, 2, rw, 0),
             (sh, lo + 1, LANE, 0), (p_zr, 0, rw, 0)]

    def rwpost_fn(i, r, f):
        return [_fn_rwpost(*r, *f)], []

    (y_rw,) = _row_call("rw_post", rwpost_fn, nt, tm, rw_in, rw_full, [(t, rw, BF16, 0)], [])

    a_hg = _mm_n_st("hg_out_mm", y_hg, w_hg_st, joined=True)
    a_rw = _mm_n_st("rw_out_mm", y_rw, w_rw_st, joined=True)
    mg_in = [(a_hg, 0, dm, 0), (a_rw, 0, dm, 0), (p_gt, 0, dm, 0), (p_gt, 1, dm, 0)]
    (merged,) = _row_call("merge", lambda i, r, f: ([_fn_merge(*r)], []), nt, tm, mg_in, [], [(t, dm, BF16, 0)], [])
    o_out = _mm_nn("out_mm", merged, w_out)

    def final_fn(i, r, f):
        gate = f[0][0][2:3]
        loss, vjp = jax.vjp(_fn_final, r[0], r[1], gate, f[1], r[2])
        dx, do, dgate, dfg, _ = vjp(jnp.ones((), F32))
        live = i >= nct
        zero = lambda a: jnp.where(live, a, 0.0)
        dmod = jnp.concatenate([jnp.concatenate([jnp.zeros((1, 2 * dm), F32), zero(dgate)], axis=1),
                                jnp.zeros((7, 3 * dm), F32)], axis=0)
        return [zero(dx), zero(do)], [jnp.broadcast_to(zero(loss), (8, LANE)), dmod, zero(dfg)]

    fin_in = [(x, 0, dm, nct), (o_out, 0, dm, 0), (tgt, 0, dm, nct)]
    dx_res, d_o, loss_acc, dmod_gate, d_final_g = _row_call(
        "final", final_fn, nt, tm, fin_in, [mod3, final_g2], [(t, dm, F32, 0), (t, dm, BF16, 0)],
        [((8, LANE), F32), ((8, 3 * dm), F32), ((1, dm), F32)])

    g_w_out = _mm_tn("d_w_out", merged, d_o)
    d_merged = _mm_nt("d_merged", d_o, w_out)

    def merge_bwd(i, r, f):
        _, vjp = jax.vjp(_fn_merge, r[0], r[1], r[2], r[3])
        da, db, dgh, dgr = vjp(r[4])
        return [da, db, jnp.concatenate([dgh, dgr], axis=1)], []

    da_hg, da_rw, dp_gt = _row_call("merge_bwd", merge_bwd, nt, tm, mg_in + [(d_merged, 0, dm, 0)], [],
                                    [(t, dm, BF16, 0), (t, dm, BF16, 0), (t, 2 * dm, BF16, 0)], [])
    g_w_hg_st = _mm_t_st("d_w_hg", y_hg, da_hg)
    g_w_rw_st = _mm_t_st("d_w_rw", y_rw, da_rw)
    dy_hg = _mm_st_t("d_y_hg", da_hg, w_hg_st)
    dy_rw = _mm_st_t("d_y_rw", da_rw, w_rw_st)

    def hgpost_bwd(i, r, f):
        _, vjp = jax.vjp(_fn_hgpost, r[0], r[1], r[2], f[0])
        dof, _, dz, dg = vjp(r[3])
        return [dof, dz], [dg]

    do_hg, dz_hg, g_hg_norm = _row_call("hg_post_bwd", hgpost_bwd, nt, tm, hg_in + [(dy_hg, 0, hw, 0)], [hg_norm_g],
                                        [(t, hw, F32, 0), (t, hw, BF16, 0)], [((1, hw), F32)])
    dq0, di0, df0, dlb0 = _hg_scan_bwd("hg_scan_bwd0", p_hg, hg_lb[0], st_hg[0], do_hg, 0, n_ctx)
    dq, di, df1, dlb1 = _hg_scan_bwd("hg_scan_bwd1", p_hg, hg_lb[1], st_hg[1], do_hg, 1, n_ctx, other=(dq0, di0))
    g_hg_lb = jnp.stack([dlb0, dlb1], axis=0)

    def rwpost_bwd(i, r, f):
        _, vjp = jax.vjp(_fn_rwpost, *r[:7], *f)
        g = vjp(r[7])
        zl = jnp.zeros((g[5].shape[0], 2 * RW_LORA), F32)
        return [g[0], jnp.concatenate([g[2], g[3], g[4], zl, g[5]], axis=1), g[6]], list(g[7:])

    dy_sum, dsh_p, dz_rw, g_a0_p, g_a2_p, g_ka_p, g_rk, g_gn_g, g_gn_b = _row_call(
        "rw_post_bwd", rwpost_bwd, nt, tm, rw_in + [(dy_rw, 0, rw, 0)], rw_full,
        [(t, rw, F32, 0), (t, n_sh_cols, F32, 0), (t, rw, BF16, 0)], [(a.shape, F32) for a in rw_full])
    dsh_dirs, hp_grads = _rw_scan_bwd_both("rw_scan_bwd", sh, hps, st_rw, dy_sum, n_ctx_rows // RW_STEP)
    dp_rs, g_mu = _shift_bwd("rw_shift_bwd", proj, rs_tile0, rw_mu, [dsh_p] + dsh_dirs, n_ctx_rows)

    def flat(a):
        if a.shape[1] == 1:
            return a.reshape(rw)
        return jnp.swapaxes(a, 0, 1).reshape(RW_LORA, rw)

    g_w0 = jnp.stack([flat(hp_grads[d][0]) for d in range(2)], axis=0)
    g_w2 = jnp.stack([flat(hp_grads[d][1]) for d in range(2)], axis=0)
    g_a0 = add("g_a0", [jnp.stack([flat(hp_grads[d][2]) for d in range(2)], axis=0), g_a0_p], (2, rw))
    g_a2 = add("g_a2", [jnp.stack([flat(hp_grads[d][3]) for d in range(2)], axis=0), g_a2_p], (2, RW_LORA, rw))
    g_kk = add("g_kk", [flat(hp_grads[0][4]).reshape(1, rw), flat(hp_grads[1][4]).reshape(1, rw)], (1, rw))
    g_ka = add("g_ka", [flat(hp_grads[0][5]).reshape(1, rw), flat(hp_grads[1][5]).reshape(1, rw), g_ka_p], (1, rw))

    dproj_st = _split_columns("dproj_split", [dq, di, df0, df1, dz_hg, dp_rs, dz_rw, dp_gt], N_SHARD)
    g_small = {"hg_lb": g_hg_lb, "rw_mu": g_mu, "rw_w0": g_w0, "rw_w2": g_w2, "rw_a0": g_a0, "rw_a2": g_a2}
    split = {n: _split_shards(g_small[n]) for n in _SMALL_SHARDED}
    small_parts = jnp.stack([_pack_small({n: split[n][j] for n in _SMALL_SHARDED}) for j in range(N_SHARD)], axis=0)
    early = {"w_hg_out": g_w_hg_st, "w_rw_out": g_w_rw_st, "w_out": g_w_out.reshape(N_SHARD, dm // N_SHARD, dm),
             "small": small_parts}
    early_chip = [_pair_exchange(f"grads_pair_sum_{n}", a, my_core, True, BF16) for n, a in early.items()]
    g_w_in_st, early_landed = _mm_t_st("d_w_in", h, dproj_st, scatter=tuple(early_chip))
    w_in_chip = _pair_exchange("grads_pair_sum_w_in", g_w_in_st, my_core, True, BF16)
    dh, (w_in_landed,) = _mm_st_t("d_h", dproj_st, w_in_st, scatter=(w_in_chip,))

    def h_bwd(i, r, f):
        shift, scale = pick(i, f[1])
        is_ctx = i < nct
        _, vjp = jax.vjp(_fn_h, jnp.where(is_ctx, r[0], r[1]), f[0], scale, shift)
        ds, dg, dscale, dshift = vjp(r[2])
        row = jnp.concatenate([dshift, dscale, jnp.zeros((1, dm), F32)], axis=1)
        z = jnp.zeros_like(row)
        dmod = jnp.concatenate([jnp.where(is_ctx, z, row), jnp.where(is_ctx, row, z), jnp.zeros((6, 3 * dm), F32)], axis=0)
        return [ds + r[3]], [dg, dmod]

    grad_x, g_norm_g, dmod_h = _row_call(
        "h_bwd", h_bwd, nt, tm, tokens + [(dh, 0, dm, 0), (dx_res, 0, dm, 0)], [norm_g, mod3],
        [(seq, dm, F32, nct)], [((1, dm), F32), ((8, 3 * dm), F32)])
    dmod = add("d_mod", [dmod_h, dmod_gate], (8, 3 * dm))
    grads = dict(
        norm_g=g_norm_g, w_in=(w_in_chip, w_in_landed), hg_norm_g=g_hg_norm, rw_kk=g_kk, rw_ka=g_ka,
        rw_rk=g_rk, rw_gn_g=g_gn_g, rw_gn_b=g_gn_b, final_g=d_final_g.reshape(dm))
    grads.update(zip(early, zip(early_chip, early_landed)))
    return loss_acc[0:1, 0:1], grad_x, dmod, grads


def _my_place():
    return lax.axis_index("x"), lax.axis_index("y"), lax.axis_index("c")


MIN_CHUNK_BYTES = 1 << 18
ROW_ALIGN = 16


def _n_chunks(rows, row_bytes):
    for n in (8, 4, 2):
        if rows % (n * ROW_ALIGN) == 0 and rows // n * row_bytes >= MIN_CHUNK_BYTES:
            return n
    return 1


def _row_bytes(a, lead=1):
    n = a.dtype.itemsize
    for d in a.shape[lead:]:
        n *= d
    return n


def _rows(ref, start, size):
    return ref.at[pl.ds(start, size)]


def _chunked(make, start, size, n):
    cs = size // n
    return [make(start + j * cs, cs) for j in range(n)]


_PEER_CHIPS = 3


def _weights_gather(name, big, small):
    nb, na = len(big), len(big) + len(small)
    arrays = list(big) + list(small)
    n_ici = 6

    def body(*refs):
        outs = refs[na:2 * na]
        send_sems, recv_sems, fsend_sems, frecv_sems = refs[2 * na:]
        x, y, c = _my_place()
        me, sx, sy, sd = 2 * x + y, 2 * (1 - x) + y, 2 * x + (1 - y), 2 * (1 - x) + (1 - y)
        kx, ky, kd = (1 - x, y, c), (x, 1 - y, c), (1 - x, 1 - y, c)

        def ici(a, j, src_slot, dst_slot, to, r0, nr):
            return pltpu.make_async_remote_copy(
                src_ref=_rows(outs[a].at[src_slot], r0, nr), dst_ref=_rows(outs[a].at[dst_slot], r0, nr),
                send_sem=send_sems.at[a, j], recv_sem=recv_sems.at[a, j], device_id=to,
                device_id_type=pl.DeviceIdType.MESH)

        def to_sibling(a, k, slot, r0, nr):
            rows = _rows(outs[a].at[slot], r0, nr)
            return pltpu.make_async_remote_copy(
                src_ref=rows, dst_ref=rows, send_sem=fsend_sems.at[a, k], recv_sem=frecv_sems.at[a, k],
                device_id=(x, y, 1 - c), device_id_type=pl.DeviceIdType.MESH)

        def start(copies):
            for cp in copies:
                cp.start()

        geo = []
        for a in range(nb):
            half = arrays[a].shape[1] // 2
            geo.append((pl.multiple_of(c * half, ROW_ALIGN), pl.multiple_of((1 - c) * half, ROW_ALIGN), half // 2,
                        _n_chunks(half // 2, _row_bytes(arrays[a], 2))))
        plan = [(me, sx, kx, 0), (me, sx, kx, 1), (me, sy, ky, 0), (me, sy, ky, 1), (sx, sd, ky, 0), (sy, sd, kx, 1)]

        def piece(a, j):
            return geo[a][0] + plan[j][3] * geo[a][2]

        for a in range(nb):
            for j in range(4):
                start(_chunked(lambda r0, cs: ici(a, j, me, me, plan[j][2], r0, cs), piece(a, j), geo[a][2], geo[a][3]))
        for a in range(nb, na):
            rows = arrays[a].shape[1]
            for j, to in ((0, kx), (2, ky), (1, kd)):
                ici(a, j, me, me, to, 0, rows).start()
        for a in range(nb):
            for j, first in ((4, 0), (5, 3)):
                src_slot, _, to, _ = plan[j]
                ici(a, first, me, plan[first][1], plan[first][2], piece(a, first), geo[a][2]).wait_recv()
                start(_chunked(lambda r0, cs: ici(a, j, src_slot, src_slot, to, r0, cs), piece(a, j), geo[a][2], geo[a][3]))
        for a in range(nb):
            for j in (1, 2):
                ici(a, j, me, plan[j][1], plan[j][2], piece(a, j), geo[a][2]).wait_recv()
            for k, slot in ((0, sx), (1, sy)):
                start(_chunked(lambda r0, cs: to_sibling(a, k, slot, r0, cs), geo[a][0], 2 * geo[a][2], geo[a][3]))
        for a in range(nb):
            for j in (4, 5):
                ici(a, j, me, sd, plan[j][2], piece(a, j), geo[a][2]).wait_recv()
            start(_chunked(lambda r0, cs: to_sibling(a, 2, sd, r0, cs), geo[a][0], 2 * geo[a][2], geo[a][3]))
        for a in range(nb, na):
            rows = arrays[a].shape[1]
            for j, slot, to in ((0, sx, kx), (2, sy, ky), (1, sd, kd)):
                ici(a, j, me, slot, to, 0, rows).wait_recv()
        for a in range(nb):
            for k, slot in ((0, sx), (1, sy), (2, sd)):
                to_sibling(a, k, slot, geo[a][1], 2 * geo[a][2]).wait_recv()
        for a in range(nb):
            for j in range(n_ici):
                ici(a, j, me, me, plan[j][2], piece(a, j), geo[a][2]).wait_send()
            for k, slot in ((0, sx), (1, sy), (2, sd)):
                to_sibling(a, k, slot, geo[a][0], 2 * geo[a][2]).wait_send()
        for a in range(nb, na):
            rows = arrays[a].shape[1]
            for j, to in ((0, kx), (2, ky), (1, kd)):
                ici(a, j, me, me, to, 0, rows).wait_send()

    hbm = pl.BlockSpec(memory_space=pl.ANY)
    ici_sems = pltpu.SemaphoreType.DMA((na, n_ici))
    pair_sems = pltpu.SemaphoreType.DMA((na, _PEER_CHIPS))
    return pl.pallas_call(
        body, name=name, in_specs=[hbm] * na, out_specs=[hbm] * na,
        out_shape=[jax.ShapeDtypeStruct(a.shape, a.dtype) for a in arrays],
        input_output_aliases={a: a for a in range(na)}, scratch_shapes=[ici_sems, ici_sems, pair_sems, pair_sems],
    )(*arrays)


def _scatter_copy(arrays, ins, outs, send_sems, recv_sems, a, k, slot, r0, nr):
    x, y, c = _my_place()
    px, py = [(1 - x, y), (x, 1 - y), (1 - x, 1 - y)][k]
    return pltpu.make_async_remote_copy(
        src_ref=_rows(ins[a].at[2 * px + py], r0, nr), dst_ref=_rows(outs[a].at[slot], r0, nr),
        send_sem=send_sems.at[a, k], recv_sem=recv_sems.at[a, k], device_id=(px, py, c),
        device_id_type=pl.DeviceIdType.MESH)


def _scatter_start(arrays, ins, outs, send_sems, recv_sems):
    x, y, _ = _my_place()
    for a in range(len(arrays)):
        rows = arrays[a].shape[1]
        for k in range(_PEER_CHIPS):
            for cp in _chunked(lambda r0, cs: _scatter_copy(arrays, ins, outs, send_sems, recv_sems, a, k, 2 * x + y, r0, cs),
                               0, rows, _n_chunks(rows, _row_bytes(arrays[a], 2))):
                cp.start()


def _scatter_wait(arrays, ins, outs, send_sems, recv_sems):
    x, y, _ = _my_place()
    peer_slot = [2 * (1 - x) + y, 2 * x + (1 - y), 2 * (1 - x) + (1 - y)]
    for k in range(_PEER_CHIPS):
        for a in range(len(arrays)):
            _scatter_copy(arrays, ins, outs, send_sems, recv_sems, a, k, peer_slot[k], 0, arrays[a].shape[1]).wait_recv()
    for a in range(len(arrays)):
        for k in range(_PEER_CHIPS):
            _scatter_copy(arrays, ins, outs, send_sems, recv_sems, a, k, 2 * x + y, 0, arrays[a].shape[1]).wait_send()


PAIR_TILE_BYTES = 4 << 20


def _pair_exchange(name, a, place, reduce, out_dtype):
    rows, cols = a.shape[-2], a.shape[-1]
    half = rows // 2 if reduce else rows
    tr = _row_tile_for(half, cols, budget=PAIR_TILE_BYTES)
    nh = half // tr
    n_steps = (N_SHARD if reduce else 1) * nh

    def body(pc_ref, *refs):
        if reduce:
            keep_ref, send_ref, o_ref, land, send_sems, recv_sems, credit, wire = refs
            wire[...] = send_ref[...].astype(BF16)
            src = wire
        else:
            send_ref, o_ref, land, send_sems, recv_sems, credit = refs
            src = send_ref
        x, y, c = _my_place()
        other = (x, y, 1 - c)
        t = pl.program_id(0) * nh + pl.program_id(1) if reduce else pl.program_id(0)
        slot = t % 2

        @pl.when(t >= 2)
        def _():
            pl.semaphore_wait(credit, 1)

        copy = pltpu.make_async_remote_copy(
            src_ref=src, dst_ref=land.at[slot], send_sem=send_sems.at[slot], recv_sem=recv_sems.at[slot],
            device_id=other, device_id_type=pl.DeviceIdType.MESH)
        copy.start()
        copy.wait_recv()
        got = land[slot]
        o_ref[...] = ((keep_ref[...] + got.astype(F32)) if reduce else got).astype(out_dtype)
        copy.wait_send()

        @pl.when(t < n_steps - 2)
        def _():
            pl.semaphore_signal(credit, inc=1, device_id=other, device_id_type=pl.DeviceIdType.MESH)

    if reduce:
        grid = (N_SHARD, nh)
        in_specs = [pl.BlockSpec((None, tr, cols), lambda j, i, pc: (j, pc[0] * nh + i, 0)),
                    pl.BlockSpec((None, tr, cols), lambda j, i, pc: (j, (1 - pc[0]) * nh + i, 0))]
        out_spec = pl.BlockSpec((None, tr, cols), lambda j, i, pc: (j, i, 0))
        out_shape = jax.ShapeDtypeStruct((N_SHARD, half, cols), out_dtype)
        operands = (a, a)
        sem = ("arbitrary", "arbitrary")
    else:
        grid = (nh,)
        in_specs = [pl.BlockSpec((tr, cols), lambda i, pc: (i, 0))]
        out_spec = pl.BlockSpec((tr, cols), lambda i, pc: (i, 0))
        out_shape = jax.ShapeDtypeStruct((half, cols), out_dtype)
        operands = (a,)
        sem = ("arbitrary",)
    return pl.pallas_call(
        body, name=name,
        grid_spec=pltpu.PrefetchScalarGridSpec(
            num_scalar_prefetch=1, grid=grid, in_specs=in_specs, out_specs=out_spec,
            scratch_shapes=[pltpu.VMEM((2, tr, cols), BF16 if reduce else a.dtype), pltpu.SemaphoreType.DMA((2,)),
                            pltpu.SemaphoreType.DMA((2,)), pltpu.SemaphoreType.REGULAR] +
                           ([pltpu.VMEM((tr, cols), BF16)] if reduce else [])),
        out_shape=out_shape, compiler_params=_params(sem),
    )(place, *operands)


def _cast_into_slot(name, a, chip):
    rows, cols = a.shape
    tm = _row_tile_for(rows, cols)

    def body(pc_ref, a_ref, o_ref):
        o_ref[...] = a_ref[...].astype(BF16)

    return pl.pallas_call(
        body, name=name,
        grid_spec=pltpu.PrefetchScalarGridSpec(
            num_scalar_prefetch=1, grid=(rows // tm,), in_specs=[pl.BlockSpec((tm, cols), lambda i, pc: (i, 0))],
            out_specs=pl.BlockSpec((None, tm, cols), lambda i, pc: (pc[0], i, 0))),
        out_shape=jax.ShapeDtypeStruct((N_SHARD, rows, cols), BF16), compiler_params=_params(("parallel",)),
    )(chip, a)


def _sum_landed(name, landed, sent, chip):
    ns, rows, cols = landed.shape
    tm = _row_tile_for(rows, cols)

    def body(pc_ref, *refs):
        own_ref, o_ref = refs[ns], refs[ns + 1]
        me = pc_ref[0]
        terms = [jnp.where(me == j, own_ref[...], refs[j][...]).astype(F32) for j in range(ns)]
        o_ref[...] = _slot_sum(terms)

    def landed_spec(j):
        return pl.BlockSpec((None, tm, cols), lambda i, pc: (jnp.where(pc[0] == j, (j + 1) % ns, j), i, 0))

    return pl.pallas_call(
        body, name=name,
        grid_spec=pltpu.PrefetchScalarGridSpec(
            num_scalar_prefetch=1, grid=(rows // tm,),
            in_specs=[landed_spec(j) for j in range(ns)] + [pl.BlockSpec((None, tm, cols), lambda i, pc: (pc[0], i, 0))],
            out_specs=pl.BlockSpec((tm, cols), lambda i, pc: (i, 0))),
        out_shape=jax.ShapeDtypeStruct((rows, cols), F32), compiler_params=_params(("parallel",)),
    )(chip, *([landed] * ns), sent)


def _gather_all(name, a):
    def body(in_ref, out_ref, send_sems, recv_sems, local_sem):
        x, y, c = _my_place()
        me = 4 * x + 2 * y + c

        def peer(k):
            return (x ^ (k >> 2), y ^ ((k >> 1) & 1), c ^ (k & 1))

        def remote(k, land):
            return pltpu.make_async_remote_copy(
                src_ref=in_ref, dst_ref=out_ref.at[land], send_sem=send_sems.at[k - 1], recv_sem=recv_sems.at[k - 1],
                device_id=peer(k), device_id_type=pl.DeviceIdType.MESH)

        local = pltpu.make_async_copy(in_ref, out_ref.at[me], local_sem)
        local.start()
        for k in range(1, N_DEV):
            remote(k, me).start()
        for k in range(1, N_DEV):
            px, py, pc = peer(k)
            remote(k, 4 * px + 2 * py + pc).wait_recv()
        for k in range(1, N_DEV):
            remote(k, me).wait_send()
        local.wait()

    hbm = pl.BlockSpec(memory_space=pl.ANY)
    return pl.pallas_call(
        body, name=name, in_specs=[hbm], out_specs=hbm,
        out_shape=jax.ShapeDtypeStruct((N_DEV,) + a.shape, a.dtype),
        scratch_shapes=[pltpu.SemaphoreType.DMA((N_DEV - 1,)), pltpu.SemaphoreType.DMA((N_DEV - 1,)), pltpu.SemaphoreType.DMA],
    )(a)


def _row_tile_for(rows, cols, budget=1 << 20):
    if rows * cols * 4 <= budget:
        return rows
    for tm in (1024, 512, 256, 128, 64, 32, 16, 8):
        if rows % tm == 0 and tm * cols * 4 <= budget:
            return tm
    return rows


def _slot_sum(vals):
    g = vals[0]
    for v in vals[1:]:
        g = g + v
    return g


def _rowwise(name, fn, arrays, out_dtype):
    rows, cols = arrays[0].shape
    tm = _row_tile_for(rows, cols)

    def body(*refs):
        refs[-1][...] = fn(*[r[...] for r in refs[:-1]]).astype(out_dtype)

    blk = pl.BlockSpec((tm, cols), lambda i: (i, 0))
    return pl.pallas_call(
        body, name=name, grid=(rows // tm,), in_specs=[blk] * len(arrays), out_specs=blk,
        out_shape=jax.ShapeDtypeStruct((rows, cols), out_dtype), compiler_params=_params(("parallel",)),
    )(*arrays)


def _sum_slots(name, st):
    ns, rows, cols = st.shape
    tm = _row_tile_for(rows, cols)

    def body(s_ref, o_ref):
        o_ref[...] = _slot_sum([s_ref[j].astype(F32) for j in range(ns)])

    return pl.pallas_call(
        body, name=name, grid=(rows // tm,),
        in_specs=[pl.BlockSpec((ns, tm, cols), lambda i: (0, i, 0))],
        out_specs=pl.BlockSpec((tm, cols), lambda i: (i, 0)),
        out_shape=jax.ShapeDtypeStruct((rows, cols), F32),
        compiler_params=_params(("parallel",)),
    )(st)


ADAM_TILE_BYTES = 1 << 20


def _adam_update(g, p_ref, m_ref, v_ref, go_ref, d_ref, mo_ref, vo_ref):
    mn = ADAM_B1 * m_ref[...] + (1.0 - ADAM_B1) * g
    vn = ADAM_B2 * v_ref[...] + (1.0 - ADAM_B2) * jnp.square(g)
    m_hat = mn / (1.0 - ADAM_B1 ** ADAM_STEP)
    v_hat = vn / (1.0 - ADAM_B2 ** ADAM_STEP)
    go_ref[...] = g
    d_ref[...] = -ADAM_LR * (m_hat / (jnp.sqrt(v_hat) + ADAM_EPS) + ADAM_WD * p_ref[...])
    mo_ref[...] = mn
    vo_ref[...] = vn


def _adamw(name, p, m, v, gst):
    lead = len(p.shape) - 2
    rows, cols = p.shape[-2:]
    ns = gst.shape[0]
    tm = _row_tile_for(rows, cols, budget=ADAM_TILE_BYTES)

    def body(p_ref, m_ref, v_ref, g_ref, *outs):
        _adam_update(_slot_sum([g_ref[j] for j in range(ns)]), p_ref, m_ref, v_ref, *outs)

    blk = pl.BlockSpec((None,) * lead + (tm, cols), lambda i: (0,) * lead + (i, 0))
    return pl.pallas_call(
        body, name=name, grid=(rows // tm,),
        in_specs=[blk, blk, blk, pl.BlockSpec((ns, tm, cols), lambda i: (0, i, 0))],
        out_specs=[blk] * 4, out_shape=[jax.ShapeDtypeStruct(p.shape, F32)] * 4,
        compiler_params=_params(("parallel",)),
    )(p, m, v, gst)


def _adamw_halves(name, p, m, v, mine, theirs, place):
    lead = len(p.shape) - 2
    rows, cols = p.shape[-2:]
    half = rows // 2
    tm = _row_tile_for(half, cols, budget=ADAM_TILE_BYTES)
    nh = half // tm

    def body(pc_ref, p_ref, m_ref, v_ref, mine_ref, theirs_ref, *outs):
        g = jnp.where(pl.program_id(0) == pc_ref[0], mine_ref[...], theirs_ref[...])
        _adam_update(g, p_ref, m_ref, v_ref, *outs)

    blk = pl.BlockSpec((None,) * lead + (tm, cols), lambda h, i, pc: (0,) * lead + (h * nh + i, 0))
    hblk = pl.BlockSpec((tm, cols), lambda h, i, pc: (i, 0))
    return pl.pallas_call(
        body, name=name,
        grid_spec=pltpu.PrefetchScalarGridSpec(
            num_scalar_prefetch=1, grid=(2, nh), in_specs=[blk, blk, blk, hblk, hblk], out_specs=[blk] * 4),
        out_shape=[jax.ShapeDtypeStruct(p.shape, F32)] * 4, compiler_params=_params(("parallel", "parallel")),
    )(place, p, m, v, mine, theirs)


def _pack(parts, width=LANE, mult=8):
    flat = jnp.concatenate([a.reshape(-1) for a in parts])
    n = flat.shape[0]
    per = width * mult
    total = -(-n // per) * per
    return jnp.pad(flat, (0, total - n)).reshape(total // width, width)


def _unpack(packed, shapes):
    flat = packed.reshape(-1)
    out, off = [], 0
    for s in shapes:
        n = 1
        for d in s:
            n *= d
        out.append(flat[off:off + n].reshape(s))
        off += n
    return out


_SMALL_SHARDED = ("hg_lb", "rw_mu", "rw_w0", "rw_w2", "rw_a0", "rw_a2")
_REPLICATED = ("c_ctx", "ada_b", "norm_g", "hg_norm_g", "rw_kk", "rw_ka", "rw_rk", "rw_gn_g", "rw_gn_b", "final_g")
_GATHERED = ("w_in", "w_hg_out", "w_rw_out", "w_out")
_WEIGHTS = ("c_ctx", "ada_w", "ada_b", "norm_g", "w_in", "hg_lb", "hg_norm_g", "rw_mu", "rw_w0", "rw_w2", "rw_a0", "rw_a2",
            "rw_kk", "rw_ka", "rw_rk", "rw_gn_g", "rw_gn_b", "w_hg_out", "w_rw_out", "w_out", "final_g")


def _pack_small(d):
    return _pack([d[n] for n in _SMALL_SHARDED], mult=2 * ROW_ALIGN)


def _join_shards(st):
    a = jnp.moveaxis(st, 0, -2)
    return a.reshape(a.shape[:-2] + (a.shape[-2] * a.shape[-1],))


def _split_shards(a):
    s = a.reshape(a.shape[:-1] + (N_SHARD, a.shape[-1] // N_SHARD))
    return jnp.moveaxis(s, -2, 0)


def kernel(x, c, ctx, c_ctx, ada_w, ada_b, norm_g, w_in, hg_lb, hg_norm_g, rw_mu, rw_w0, rw_w2, rw_a0, rw_a2, rw_kk, rw_ka, rw_rk, rw_gn_g, rw_gn_b, w_hg_out, w_rw_out, w_out, final_g, loss_target, m_c_ctx, m_ada_w, m_ada_b, m_norm_g, m_w_in, m_hg_lb, m_hg_norm_g, m_rw_mu, m_rw_w0, m_rw_w2, m_rw_a0, m_rw_a2, m_rw_kk, m_rw_ka, m_rw_rk, m_rw_gn_g, m_rw_gn_b, m_w_hg_out, m_w_rw_out, m_w_out, m_final_g, v_c_ctx, v_ada_w, v_ada_b, v_norm_g, v_w_in, v_hg_lb, v_hg_norm_g, v_rw_mu, v_rw_w0, v_rw_w2, v_rw_a0, v_rw_a2, v_rw_kk, v_rw_ka, v_rw_rk, v_rw_gn_g, v_rw_gn_b, v_w_hg_out, v_w_rw_out, v_w_out, v_final_g):
    w = dict(c_ctx=c_ctx, ada_w=ada_w, ada_b=ada_b, norm_g=norm_g, w_in=w_in, hg_lb=hg_lb, hg_norm_g=hg_norm_g, rw_mu=rw_mu,
             rw_w0=rw_w0, rw_w2=rw_w2, rw_a0=rw_a0, rw_a2=rw_a2, rw_kk=rw_kk, rw_ka=rw_ka, rw_rk=rw_rk, rw_gn_g=rw_gn_g,
             rw_gn_b=rw_gn_b, w_hg_out=w_hg_out, w_rw_out=w_rw_out, w_out=w_out, final_g=final_g)
    m = dict(c_ctx=m_c_ctx, ada_w=m_ada_w, ada_b=m_ada_b, norm_g=m_norm_g, w_in=m_w_in, hg_lb=m_hg_lb, hg_norm_g=m_hg_norm_g,
             rw_mu=m_rw_mu, rw_w0=m_rw_w0, rw_w2=m_rw_w2, rw_a0=m_rw_a0, rw_a2=m_rw_a2, rw_kk=m_rw_kk, rw_ka=m_rw_ka,
             rw_rk=m_rw_rk, rw_gn_g=m_rw_gn_g, rw_gn_b=m_rw_gn_b, w_hg_out=m_w_hg_out, w_rw_out=m_w_rw_out, w_out=m_w_out,
             final_g=m_final_g)
    v = dict(c_ctx=v_c_ctx, ada_w=v_ada_w, ada_b=v_ada_b, norm_g=v_norm_g, w_in=v_w_in, hg_lb=v_hg_lb, hg_norm_g=v_hg_norm_g,
             rw_mu=v_rw_mu, rw_w0=v_rw_w0, rw_w2=v_rw_w2, rw_a0=v_rw_a0, rw_a2=v_rw_a2, rw_kk=v_rw_kk, rw_ka=v_rw_ka,
             rw_rk=v_rw_rk, rw_gn_g=v_rw_gn_g, rw_gn_b=v_rw_gn_b, w_hg_out=v_w_hg_out, w_rw_out=v_w_rw_out, w_out=v_w_out,
             final_g=v_final_g)

    def mat(a):
        return a.reshape(a.shape[-2], a.shape[-1])

    my_core = lax.axis_index("c").astype(jnp.int32).reshape(1)
    my_chip = (2 * lax.axis_index("x") + lax.axis_index("y")).astype(jnp.int32).reshape(1)

    my_dev = 2 * my_chip[0] + my_core[0]
    dm = x.shape[-1]
    ada_cols = ada_w.shape[-1]

    c_all = _gather_all("cond_gather", c.reshape(1, dm)).reshape(N_DEV, dm)
    cond16 = jnp.concatenate([c_all, c_ctx.reshape(1, dm), jnp.zeros((7, dm), F32)], axis=0)
    (sc16,) = _row_call("cond_silu", lambda i, r, f: ([jax.nn.silu(r[0])], []), 1, 16, [(cond16, 0, dm, 0)], [],
                        [(16, dm, F32, 0)], [])
    mod_here = _mm_nn("mod_mm", sc16, mat(ada_w))
    mod_all = _gather_all("mod_gather", mod_here)
    mod_rows = jnp.concatenate([mod_all[2 * j] for j in range(N_SHARD)], axis=1)
    mine = lax.dynamic_slice_in_dim(mod_rows, my_dev, 1, axis=0)
    mod = _add_small("mod_bias", [jnp.concatenate([mine, mod_rows[N_DEV:N_DEV + 1], jnp.zeros((6, 3 * dm), F32)], axis=0),
                                  jnp.broadcast_to(ada_b, (8, 3 * dm))], (8, 3 * dm))

    small_shapes = [w[n].shape for n in _SMALL_SHARDED]
    big_bf = [_cast_into_slot(f"to_bf16_{n}", mat(w[n]), my_chip) for n in _GATHERED]
    small_mine = _pack_small(w)
    small_slots = lax.dynamic_update_slice(jnp.zeros((N_SHARD,) + small_mine.shape, F32), small_mine[None], (my_chip[0], 0, 0))
    gathered = _weights_gather("weights_gather", big_bf, [small_slots])
    w_in_st, w_hg_st, w_rw_st, w_out_st, small_st = gathered
    full_small = {}
    per_chip = [_unpack(small_st[j], small_shapes) for j in range(N_SHARD)]
    for i, n in enumerate(_SMALL_SHARDED):
        full_small[n] = _join_shards(jnp.stack([per_chip[j][i] for j in range(N_SHARD)], axis=0))
    w_out_full = w_out_st.reshape(dm, dm)

    loss_b, grad_x, dmod, g = _local_step(
        x[0], ctx[0], mod, norm_g, w_in_st, full_small["hg_lb"], hg_norm_g, full_small["rw_mu"][0],
        full_small["rw_w0"][0], full_small["rw_w2"][0], full_small["rw_a0"][0], full_small["rw_a2"][0], rw_kk, rw_ka, rw_rk,
        rw_gn_g, rw_gn_b, w_hg_st, w_rw_st, w_out_full, final_g, loss_target[0], my_core)
    loss = lax.psum(loss_b[0, 0], ("x", "y", "c"))

    dmod_all = _gather_all("dmod_gather", dmod[0:2])
    dmod_here = lax.dynamic_slice_in_dim(dmod_all, my_chip[0] * ada_cols, ada_cols, axis=2)
    d_ctx_row = _add_small("d_mod_ctx", [dmod_here[j, 1:2] for j in range(N_DEV)], (1, ada_cols))
    dm16 = jnp.concatenate([dmod_here[:, 0], d_ctx_row, jnp.zeros((7, ada_cols), F32)], axis=0)
    g_ada_here = _mm_tn("d_ada_w", sc16, dm16)
    d_sc16 = _mm_nt("d_cond", dm16, mat(ada_w))

    def cond_bwd(i, r, f):
        _, vjp = jax.vjp(jax.nn.silu, r[0])
        return [vjp(r[1])[0]], []

    (d_cond16,) = _row_call("cond_bwd", cond_bwd, 1, 16, [(cond16, 0, dm, 0), (d_sc16, 0, dm, 0)], [], [(16, dm, F32, 0)], [])
    g["c_ctx"] = jnp.where(my_core[0] == 0, d_cond16[N_DEV], 0.0)
    g["ada_b"] = _add_small("g_ada_b", [dmod[0:1], dmod[1:2]], (1, 3 * dm))

    def finish(name, chip_sum, landed):
        half = _sum_landed(f"grads_sum_{name}", landed, chip_sum, my_chip)
        return half, _pair_exchange(f"grads_pair_swap_{name}", half, my_core, False, F32)

    res = {}
    rep_shapes = [w[n].shape for n in _REPLICATED]
    rep_all = _gather_all("grads_replicated", _pack([g[n].reshape(w[n].shape) for n in _REPLICATED]))
    res["ada_w"] = _adamw("adamw_ada_w", ada_w, m["ada_w"], v["ada_w"], g_ada_here[None])
    for n in _GATHERED:
        res[n] = _adamw_halves(f"adamw_{n}", w[n], m[n], v[n], *finish(n, *g[n]), my_core)
    outs = _adamw_halves("adamw_small", small_mine, _pack_small(m), _pack_small(v), *finish("small", *g["small"]), my_core)
    for i, vals in enumerate(zip(*[_unpack(o, small_shapes) for o in outs])):
        res[_SMALL_SHARDED[i]] = list(vals)
    outs = _adamw("adamw_replicated", _pack([w[n] for n in _REPLICATED]), _pack([m[n] for n in _REPLICATED]),
                  _pack([v[n] for n in _REPLICATED]), rep_all)
    for i, vals in enumerate(zip(*[_unpack(o, rep_shapes) for o in outs])):
        res[_REPLICATED[i]] = list(vals)

    return (loss, grad_x[None], *[res[n][0] for n in _WEIGHTS], *[res[n][1] for n in _WEIGHTS],
            *[res[n][2] for n in _WEIGHTS], *[res[n][3] for n in _WEIGHTS])
```

```python
import functools

import jax
import jax.numpy as jnp
from jax import lax
from jax.experimental import pallas as pl
from jax.experimental.pallas import tpu as pltpu

HI = lax.Precision.HIGHEST
F32 = jnp.float32
BF16 = jnp.bfloat16

NORM_EPS = 1e-6
HG_HEAD = 128
RW_HEAD = 64
RW_LORA = 64
RW_GN_EPS = 64e-5
GRID_W = 64
SUB = 16
RW_SUB = 16
STEP = 64
RW_STEP = 64
N_SHARD = 4
N_DEV = 8
LANE = 128

ADAM_LR = 0.001
ADAM_B1 = 0.9
ADAM_B2 = 0.999
ADAM_EPS = 1e-08
ADAM_WD = 0.01
ADAM_STEP = 10

VMEM_LIMIT = 56 * 1024 * 1024


def _params(sem=None):
    return pltpu.CompilerParams(dimension_semantics=sem, vmem_limit_bytes=VMEM_LIMIT)


def _tile(n, cands):
    for c in cands:
        if n % c == 0:
            return c
    return n


def _iota2(n, m, d):
    return lax.broadcasted_iota(jnp.int32, (n, m), d)


def _before(n, rev, strict):
    t, s = _iota2(n, n, 0), _iota2(n, n, 1)
    if rev:
        return (s > t) if strict else (s >= t)
    return (s < t) if strict else (s <= t)


def _running_sum(a, axis, rev):
    n = a.shape[axis]
    shift = 1
    while shift < n:
        pad = list(a.shape)
        pad[axis] = shift
        zeros = jnp.zeros(pad, a.dtype)
        if rev:
            moved = jnp.concatenate([lax.slice_in_dim(a, shift, n, axis=axis), zeros], axis=axis)
        else:
            moved = jnp.concatenate([zeros, lax.slice_in_dim(a, 0, n - shift, axis=axis)], axis=axis)
        a = a + moved
        shift *= 2
    return a


def _sdot(a, b, spec):
    return jnp.einsum(spec, a, b, precision=lax.Precision.DEFAULT, preferred_element_type=F32)


def _hg_step(s0, qraw, iin, fin, lb2, rev):
    c, w = qraw.shape
    h = w // HG_HEAD
    nsub = c // SUB
    lb = jax.nn.sigmoid(lb2[0:1] - lb2[1:2])
    q = jax.nn.silu(qraw)
    fg = lb + (1.0 - lb) * jax.nn.sigmoid(fin)
    kk = 1.0 - fg
    g = jnp.log(fg)
    bcum = _running_sum(g, 0, rev)
    def heads(a):
        return jnp.swapaxes(a.reshape(a.shape[0], h, HG_HEAD), 0, 1)

    def unheads(a):
        return jnp.swapaxes(a, 0, 1).reshape(a.shape[1], w)

    blocks = [slice(j * SUB, (j + 1) * SUB) for j in range(nsub)]
    outs = []
    for sl in blocks:
        qs, ks, vs, bc = [a[sl].reshape(SUB, h, HG_HEAD) for a in (q, kk, iin, bcum)]
        o = jnp.zeros((SUB, h, HG_HEAD), F32)
        for si in range(SUB):
            after = slice(0, si + 1) if rev else slice(si, SUB)
            dec = jnp.exp(jnp.minimum(bc[after] - bc[si:si + 1], 0.0))
            a = jnp.sum(qs[after] * ks[si:si + 1] * dec, axis=-1, keepdims=True)
            term = a * vs[si:si + 1]
            n_rest = SUB - 1 - si if rev else si
            if n_rest:
                rest = jnp.zeros((n_rest, h, HG_HEAD), F32)
                term = jnp.concatenate([term, rest] if rev else [rest, term], axis=0)
            o = o + term
        outs.append(o.reshape(SUB, w))
    order = list(range(nsub - 1, -1, -1)) if rev else list(range(nsub))
    for pos in range(1, nsub):
        j, before = order[pos], order[:pos]
        first = (j + 1) * SUB - 1 if rev else j * SUB
        bstart = bcum[first:first + 1] - g[first:first + 1]
        qp = heads(q[blocks[j]] * jnp.exp(bcum[blocks[j]] - bstart))
        kp = heads(jnp.concatenate([kk[blocks[p]] * jnp.exp(bstart - bcum[blocks[p]]) for p in before], axis=0))
        vp = heads(jnp.concatenate([iin[blocks[p]] for p in before], axis=0))
        outs[j] = outs[j] + unheads(_sdot(_sdot(qp, kp, 'htk,hsk->hts'), vp, 'hts,hsv->htv'))
    o_state = unheads(_sdot(heads(q * jnp.exp(bcum)), s0, 'htk,hvk->htv'))
    last = 0 if rev else c - 1
    blast = bcum[last:last + 1]
    s_new = heads(jnp.exp(blast)) * s0 + _sdot(heads(iin), heads(kk * jnp.exp(blast - bcum)), 'hsv,hsk->hvk')
    return jnp.concatenate(outs, axis=0) + o_state, s_new


def _tri_solve(lmat, rhs, rev):
    hh, c, _ = lmat.shape
    sub = RW_SUB
    nb = c // sub
    diag = jnp.concatenate([lmat[:, i * sub:(i + 1) * sub, i * sub:(i + 1) * sub] for i in range(nb)], axis=0)
    dt = jnp.transpose(diag, (1, 2, 0))
    col = lax.broadcasted_iota(jnp.int32, (sub, 1), 0)
    inv_rows = [None] * sub
    order = list(range(sub - 1, -1, -1)) if rev else list(range(sub))
    for pos, t in enumerate(order):
        row = jnp.broadcast_to((col == t).astype(F32), (sub, dt.shape[2]))
        for s in order[:pos]:
            row = row - dt[t, s:s + 1, :] * inv_rows[s]
        inv_rows[t] = row
    tinv = jnp.transpose(jnp.concatenate([r[None] for r in inv_rows], axis=0), (2, 0, 1))
    p = [None] * nb
    done = []
    for i in (range(nb - 1, -1, -1) if rev else range(nb)):
        r = rhs[:, i * sub:(i + 1) * sub]
        if done:
            lrow = jnp.concatenate([lmat[:, i * sub:(i + 1) * sub, m * sub:(m + 1) * sub] for m in done], axis=2)
            r = r - _sdot(lrow, jnp.concatenate([p[m] for m in done], axis=1), 'hts,hsv->htv')
        p[i] = _sdot(tinv[i * hh:(i + 1) * hh], r, 'hts,hsv->htv')
        done.append(i)
    return jnp.concatenate(p, axis=1)


def _rw_step(s0, r, k, v, wlo, alo, w0h, w2h, a0h, a2h, kkh, kah, rev):
    hh, c, _ = r.shape
    tl = jnp.broadcast_to(jnp.tanh(wlo)[None], (hh, c, wlo.shape[1]))
    al = jnp.broadcast_to(alo[None], (hh, c, alo.shape[1]))
    wlog = -jax.nn.softplus(-(w0h + _sdot(tl, w2h, 'hcl,hlj->hcj'))) - 0.5
    lw = -jnp.exp(wlog)
    a = jax.nn.sigmoid(a0h + _sdot(al, a2h, 'hcl,hlj->hcj'))
    kk = k * kkh
    kk = kk * lax.rsqrt(jnp.sum(kk * kk, axis=-1, keepdims=True) + 1e-12)
    kd = k * (1.0 + (a - 1.0) * kah)
    b = kk * a
    cum = _running_sum(lw, 1, rev)
    ecum, encum = jnp.exp(cum), jnp.exp(-cum)
    alpha = jnp.exp(cum - lw) * kk
    beta = b * encum
    kappa = kd * encum
    rho = r * ecum
    m_lt = _before(c, rev, True)[None]
    m_le = _before(c, rev, False)[None]
    ar = jnp.concatenate([alpha, rho], axis=1)
    kb = jnp.concatenate([kappa, beta], axis=1)
    gram = _sdot(ar, kb, 'htk,hsk->hts')
    a_kap = jnp.where(m_lt, gram[:, :c, :c], 0.0)
    a_bet = jnp.where(m_lt, gram[:, :c, c:], 0.0)
    b_kap = jnp.where(m_le, gram[:, c:, :c], 0.0)
    b_bet = jnp.where(m_le, gram[:, c:, c:], 0.0)
    from_state = _sdot(ar, s0, 'htk,hvk->htv')
    p = _tri_solve(a_bet, from_state[:, :c] + _sdot(a_kap, v, 'hts,hsv->htv'), rev)
    vp = jnp.concatenate([v, -p], axis=1)
    y = from_state[:, c:] + _sdot(jnp.concatenate([b_kap, b_bet], axis=2), vp, 'hts,hsv->htv')
    stil = s0 + _sdot(vp, kb, 'hsv,hsk->hvk')
    last = 0 if rev else c - 1
    return y, stil * ecum[:, last:last + 1, :]


def _fn_h(s, norm_g, scale, shift):
    return s * lax.rsqrt(jnp.mean(s * s, axis=-1, keepdims=True) + NORM_EPS) * norm_g * (1.0 + scale) + shift


def _fn_hgpost(of, ob, z, g):
    tm, w = of.shape
    o = (of + ob).reshape(tm, w // HG_HEAD, HG_HEAD)
    o = o * lax.rsqrt(jnp.mean(o * o, axis=-1, keepdims=True) + NORM_EPS)
    return o.reshape(tm, w) * g * jax.nn.silu(z)


def _fn_rwpost(y0, y1, r, k, v, alo, z, a0, a2, k_a, r_k, gn_g, gn_b):
    tm, w = r.shape
    nh = w // RW_HEAD
    asum = 0.0
    for d in range(2):
        asum = asum + jax.nn.sigmoid(a0[d:d + 1] + jnp.dot(alo[:, d * RW_LORA:(d + 1) * RW_LORA], a2[d],
                                                           precision=HI, preferred_element_type=F32))
    k_sum = k * (2.0 + (asum - 2.0) * k_a)
    ys = (y0 + y1).reshape(tm, nh, RW_HEAD)
    mean = jnp.mean(ys, axis=-1, keepdims=True)
    var = jnp.mean(jnp.square(ys - mean), axis=-1, keepdims=True)
    y = ((ys - mean) * lax.rsqrt(var + RW_GN_EPS)).reshape(tm, w) * gn_g + gn_b
    bonus = jnp.sum((r * k_sum * r_k).reshape(tm, nh, RW_HEAD), axis=-1, keepdims=True) * v.reshape(tm, nh, RW_HEAD)
    return (y + bonus.reshape(tm, w)) * jax.nn.silu(z)


def _fn_merge(a, b, ghg, grw):
    return jax.nn.sigmoid(ghg) * a + jax.nn.sigmoid(grw) * b


def _fn_final(xs, o, gate, final_g, tgt):
    x2 = xs + gate * o
    y = x2 * lax.rsqrt(jnp.mean(x2 * x2, axis=-1, keepdims=True) + NORM_EPS) * final_g
    return 0.5 * jnp.sum(jnp.mean(jnp.square(y - tgt), axis=-1))


def _row_call(name, fn, n_tiles, tm, row_ins, full_ins, row_outs, acc_outs):
    n_ri, n_fi, n_ro = len(row_ins), len(full_ins), len(row_outs)

    def body(*refs):
        i = pl.program_id(0)
        rvals = [r[...] for r in refs[:n_ri]]
        fvals = [r[...] for r in refs[n_ri:n_ri + n_fi]]
        outs = refs[n_ri + n_fi:]
        ro, ao = fn(i, rvals, fvals)
        for ref, val in zip(outs[:n_ro], ro):
            ref[...] = val.astype(ref.dtype)
        for ref, val in zip(outs[n_ro:], ao):
            @pl.when(i == 0)
            def _(ref=ref):
                ref[...] = jnp.zeros_like(ref)
            ref[...] += val.astype(ref.dtype)

    def rspec(width, cb, off, rows):
        return pl.BlockSpec((tm, width), lambda i: (jnp.clip(i - off, 0, rows // tm - 1), cb))

    def fspec(shape):
        nd = len(shape)
        return pl.BlockSpec(shape, lambda i: (0,) * nd)

    in_specs = [rspec(w, cb, off, a.shape[0]) for (a, cb, w, off) in row_ins] + [fspec(a.shape) for a in full_ins]
    out_specs = [rspec(w, 0, off, rows) for (rows, w, _, off) in row_outs] + [fspec(s) for (s, _) in acc_outs]
    out_shape = [jax.ShapeDtypeStruct((rows, w), dt) for (rows, w, dt, _) in row_outs] + \
                [jax.ShapeDtypeStruct(s, dt) for (s, dt) in acc_outs]
    res = pl.pallas_call(
        body, name=name, grid=(n_tiles,), in_specs=in_specs, out_specs=out_specs, out_shape=out_shape,
        compiler_params=_params(("arbitrary",)),
    )(*[a for (a, _, _, _) in row_ins], *full_ins)
    return list(res)


def _mm(name, a, b, m, n, k_steps, tm, tn, a_block, a_map, b_block, b_map, o_shape, o_block, o_map,
        contract, out_dtype=F32, scatter=()):
    ns = len(scatter)
    grid = (m // tm, n // tn, k_steps)

    def body(*refs):
        a_ref, b_ref, o_ref, acc_ref = refs[0], refs[1], refs[2 + ns], refs[3 + 2 * ns]
        kk = pl.program_id(2)
        if ns:
            sc_refs = (refs[2:2 + ns], refs[3 + ns:3 + 2 * ns]) + tuple(refs[4 + 2 * ns:])
            at = (pl.program_id(0) * grid[1] + pl.program_id(1)) * grid[2] + kk
            pl.when(at == 0)(lambda: _scatter_start(scatter, *sc_refs))

        @pl.when(kk == 0)
        def _():
            acc_ref[...] = jnp.zeros_like(acc_ref)

        acc_ref[...] += lax.dot_general(a_ref[...].astype(BF16), b_ref[...].astype(BF16),
                                        (contract, ((), ())), preferred_element_type=F32)

        @pl.when(kk == k_steps - 1)
        def _():
            o_ref[...] = acc_ref[...].astype(o_ref.dtype)

        if ns:
            pl.when(at == grid[0] * grid[1] * grid[2] - 1)(lambda: _scatter_wait(scatter, *sc_refs))

    hbm = pl.BlockSpec(memory_space=pl.ANY)
    sems = [pltpu.SemaphoreType.DMA((ns, _PEER_CHIPS))] * 2 if ns else []
    res = pl.pallas_call(
        body, name=name, grid=grid,
        in_specs=[pl.BlockSpec(a_block, a_map), pl.BlockSpec(b_block, b_map)] + [hbm] * ns,
        out_specs=[pl.BlockSpec(o_block, o_map)] + [hbm] * ns,
        out_shape=[jax.ShapeDtypeStruct(o_shape, out_dtype)] + [jax.ShapeDtypeStruct(s.shape, s.dtype) for s in scatter],
        scratch_shapes=[pltpu.VMEM((tm, tn), F32)] + sems,
        compiler_params=_params(("arbitrary",) * 3 if ns else ("parallel", "parallel", "arbitrary")),
    )(a, b, *scatter)
    return (res[0], list(res[1:])) if ns else res[0]


_TM = (768, 512, 256, 128, 64, 32, 16, 8)
_TN = (512, 256, 128)
_TK = (1024, 768, 512, 256, 128)
_TK_WIDE = (768, 512, 256, 128)
WIDE_OUT_BYTES = 32 << 20


def _tm_wide(m, ns):
    for tm in _TM:
        if m % tm == 0 and 3 * 4 * tm * ns <= WIDE_OUT_BYTES:
            return tm
    return m


def _mm_nn(name, a, b, out_dtype=F32):
    m, k = a.shape
    n = b.shape[1]
    tm, tn, tk = _tile(m, _TM), _tile(n, _TN), _tile(k, _TK)
    return _mm(name, a, b, m, n, k // tk, tm, tn, (tm, tk), lambda i, j, s: (i, s), (tk, tn), lambda i, j, s: (s, j),
               (m, n), (tm, tn), lambda i, j, s: (i, j), ((1,), (0,)), out_dtype)


def _mm_nt(name, a, b, out_dtype=F32):
    m, k = a.shape
    n = b.shape[0]
    tm, tn, tk = _tile(m, _TM), _tile(n, _TN), _tile(k, _TK)
    return _mm(name, a, b, m, n, k // tk, tm, tn, (tm, tk), lambda i, j, s: (i, s), (tn, tk), lambda i, j, s: (j, s),
               (m, n), (tm, tn), lambda i, j, s: (i, j), ((1,), (1,)), out_dtype)


def _mm_tn(name, a, b, out_dtype=F32):
    k, m = a.shape
    n = b.shape[1]
    tm, tn, tk = _tile(m, _TM), _tile(n, _TN), _tile(k, _TK)
    return _mm(name, a, b, m, n, k // tk, tm, tn, (tk, tm), lambda i, j, s: (s, i), (tk, tn), lambda i, j, s: (s, j),
               (m, n), (tm, tn), lambda i, j, s: (i, j), ((0,), (0,)), out_dtype)


def _mm_n_st(name, a, bst, out_dtype=F32, joined=False):
    m, k = a.shape
    ns_, _, ns = bst.shape
    tm, tk = _tm_wide(m, ns), _tile(k, (512, 256, 128))
    out = ((m, ns_ * ns), (tm, ns), lambda i, j, s: (i, j)) if joined else \
          ((ns_, m, ns), (None, tm, ns), lambda i, j, s: (j, i, 0))
    return _mm(name, a, bst, m, ns_ * ns, k // tk, tm, ns,
               (tm, tk), lambda i, j, s: (i, s), (None, tk, ns), lambda i, j, s: (j, s, 0), *out, ((1,), (0,)), out_dtype)


def _mm_st_t(name, ast, bst, out_dtype=F32, scatter=()):
    ns_, n, ns = bst.shape
    m = ast.shape[-2]
    tm, tn = _tile(m, _TM), _tile(n, _TN)
    a_side = ((None, tm, ns), lambda i, j, s: (s, i, 0)) if ast.ndim == 3 else ((tm, ns), lambda i, j, s: (i, s))
    return _mm(name, ast, bst, m, n, ns_, tm, tn, *a_side, (None, tn, ns), lambda i, j, s: (s, j, 0),
               (m, n), (tm, tn), lambda i, j, s: (i, j), ((1,), (1,)), out_dtype, scatter)


def _mm_t_st(name, a, bst, out_dtype=F32, scatter=(), n_shard=N_SHARD):
    k, m = a.shape
    ns = bst.shape[-1] if bst.ndim == 3 else bst.shape[-1] // n_shard
    tm, tk = _tile(m, _TN), _tile(k, _TK_WIDE)
    b_side = ((None, tk, ns), lambda i, j, s: (j, s, 0)) if bst.ndim == 3 else ((tk, ns), lambda i, j, s: (s, j))
    return _mm(name, a, bst, m, n_shard * ns, k // tk, tm, ns, (tk, tm), lambda i, j, s: (s, i), *b_side,
               (n_shard, m, ns), (None, tm, ns), lambda i, j, s: (j, i, 0), ((0,), (0,)), out_dtype, scatter)


RELAYOUT_TILE_BYTES = 12 << 20


def _join_columns(name, st):
    ns_, t, ns = st.shape
    tm = _row_tile_for(t, ns_ * ns, budget=RELAYOUT_TILE_BYTES)

    def body(s_ref, o_ref):
        o_ref[...] = jnp.concatenate([s_ref[j] for j in range(ns_)], axis=1)

    return pl.pallas_call(
        body, name=name, grid=(t // tm,), in_specs=[pl.BlockSpec((ns_, tm, ns), lambda i: (0, i, 0))],
        out_specs=pl.BlockSpec((tm, ns_ * ns), lambda i: (i, 0)),
        out_shape=jax.ShapeDtypeStruct((t, ns_ * ns), st.dtype), compiler_params=_params(("parallel",)),
    )(st)


def _split_columns(name, pieces, n_shard):
    t = pieces[0].shape[0]
    n = sum(p.shape[1] for p in pieces)
    ns = n // n_shard
    tm = _row_tile_for(t, n, budget=RELAYOUT_TILE_BYTES)
    npc = len(pieces)

    def body(*refs):
        full = jnp.concatenate([r[...] for r in refs[:npc]], axis=1)
        for j in range(n_shard):
            refs[npc][j] = full[:, j * ns:(j + 1) * ns]

    return pl.pallas_call(
        body, name=name, grid=(t // tm,),
        in_specs=[pl.BlockSpec((tm, p.shape[1]), lambda i: (i, 0)) for p in pieces],
        out_specs=pl.BlockSpec((n_shard, tm, ns), lambda i: (0, i, 0)),
        out_shape=jax.ShapeDtypeStruct((n_shard, t, ns), pieces[0].dtype), compiler_params=_params(("parallel",)),
    )(*pieces)


def _scan_order(j, n_ctx, n_all, rev):
    if not rev:
        return j
    return jnp.where(j < n_ctx, n_ctx - 1 - j, n_all - 1 - (j - n_ctx))


def _hg_scan_fwd(name, p_hg, lb2, d, n_ctx):
    t, w = p_hg.shape[0], lb2.shape[1]
    h = w // HG_HEAD
    n = t // STEP
    rev = d == 1

    def body(q_ref, i_ref, f_ref, lb_ref, o_ref, st_ref, s_ref):
        j = pl.program_id(0)

        @pl.when(j == 0)
        def _():
            s_ref[...] = jnp.zeros_like(s_ref)

        s0 = s_ref[...]
        st_ref[...] = s0
        o, s1 = _hg_step(s0, q_ref[...], i_ref[...], f_ref[...], lb_ref[...], rev)
        o_ref[...] = o
        s_ref[...] = s1

    def rows(cb):
        return pl.BlockSpec((STEP, w), lambda j: (_scan_order(j, n_ctx, n, rev), cb))

    return pl.pallas_call(
        body, name=name, grid=(n,),
        in_specs=[rows(0), rows(1), rows(2 + d), pl.BlockSpec((2, w), lambda j: (0, 0))],
        out_specs=[rows(0), pl.BlockSpec((None, h, HG_HEAD, HG_HEAD), lambda j: (j, 0, 0, 0))],
        out_shape=[jax.ShapeDtypeStruct((t, w), F32), jax.ShapeDtypeStruct((n, h, HG_HEAD, HG_HEAD), F32)],
        scratch_shapes=[pltpu.VMEM((h, HG_HEAD, HG_HEAD), F32)],
        compiler_params=_params(("arbitrary",)),
    )(p_hg, p_hg, p_hg, lb2)


def _hg_scan_bwd(name, p_hg, lb2, states, do, d, n_ctx, other=()):
    t, w = p_hg.shape[0], lb2.shape[1]
    h = w // HG_HEAD
    n = t // STEP
    rev = d == 1
    no = len(other)

    def body(q_ref, i_ref, f_ref, lb_ref, st_ref, do_ref, *refs):
        dq_ref, di_ref, df_ref, dlb_ref, ds_ref = refs[no:]
        step = pl.program_id(0)

        @pl.when(step == 0)
        def _():
            ds_ref[...] = jnp.zeros_like(ds_ref)
            dlb_ref[...] = jnp.zeros_like(dlb_ref)

        _, vjp = jax.vjp(lambda s0, q, i, f, lb: _hg_step(s0, q, i, f, lb, rev),
                         st_ref[...], q_ref[...], i_ref[...], f_ref[...], lb_ref[...])
        ds0, dq, di, df, dlb = vjp((do_ref[...], ds_ref[...]))
        if no:
            dq, di = refs[0][...] + dq, refs[1][...] + di
        dq_ref[...] = dq.astype(dq_ref.dtype)
        di_ref[...] = di.astype(di_ref.dtype)
        df_ref[...] = df.astype(df_ref.dtype)
        dlb_ref[...] += dlb
        ds_ref[...] = ds0

    def rows(cb):
        return pl.BlockSpec((STEP, w), lambda s: (_scan_order(n - 1 - s, n_ctx, n, rev), cb))

    qi = BF16 if no else F32
    return pl.pallas_call(
        body, name=name, grid=(n,),
        in_specs=[rows(0), rows(1), rows(2 + d), pl.BlockSpec((2, w), lambda s: (0, 0)),
                  pl.BlockSpec((None, h, HG_HEAD, HG_HEAD), lambda s: (n - 1 - s, 0, 0, 0)), rows(0)] + [rows(0)] * no,
        out_specs=[rows(0), rows(0), rows(0), pl.BlockSpec((2, w), lambda s: (0, 0))],
        out_shape=[jax.ShapeDtypeStruct((t, w), qi)] * 2 + [jax.ShapeDtypeStruct((t, w), BF16),
                                                            jax.ShapeDtypeStruct((2, w), F32)],
        scratch_shapes=[pltpu.VMEM((h, HG_HEAD, HG_HEAD), F32)],
        compiler_params=_params(("arbitrary",)),
    )(p_hg, p_hg, p_hg, lb2, states, do, *other)


def _to_heads(a, nh):
    return jnp.stack([a[:, i * RW_HEAD:(i + 1) * RW_HEAD] for i in range(nh)], axis=0)


def _from_heads(a):
    return jnp.concatenate([a[i] for i in range(a.shape[0])], axis=-1)


def _rw_scan_fwd(name, sh, hp, d, n_ctx):
    t = sh.shape[0]
    w = (sh.shape[1] - 4 * RW_LORA) // 3
    nh = w // RW_HEAD
    n = t // RW_STEP
    rev = d == 1
    lo = 3 * w // LANE

    def body(r_ref, k_ref, v_ref, wl_ref, al_ref, w0_ref, w2_ref, a0_ref, a2_ref, kk_ref, ka_ref,
             y_ref, st_ref, s_ref):
        j = pl.program_id(0)

        @pl.when(j == 0)
        def _():
            s_ref[...] = jnp.zeros_like(s_ref)

        s0 = s_ref[...]
        st_ref[...] = s0
        wl = wl_ref[...][:, d * RW_LORA:(d + 1) * RW_LORA]
        al = al_ref[...][:, d * RW_LORA:(d + 1) * RW_LORA]
        y, s1 = _rw_step(s0, _to_heads(r_ref[...], nh), _to_heads(k_ref[...], nh), _to_heads(v_ref[...], nh), wl, al,
                         w0_ref[...], w2_ref[...], a0_ref[...], a2_ref[...], kk_ref[...], ka_ref[...], rev)
        y_ref[...] = _from_heads(y)
        s_ref[...] = s1

    def rows(cb, width=w):
        return pl.BlockSpec((RW_STEP, width), lambda j: (_scan_order(j, n_ctx, n, rev), cb))

    def whole(a):
        nd = a.ndim
        return pl.BlockSpec(a.shape, lambda j: (0,) * nd)

    return pl.pallas_call(
        body, name=name, grid=(n,),
        in_specs=[rows(0), rows(1), rows(2), rows(lo, LANE), rows(lo + 1, LANE)] + [whole(a) for a in hp],
        out_specs=[rows(0), pl.BlockSpec((None, nh, RW_HEAD, RW_HEAD), lambda j: (j, 0, 0, 0))],
        out_shape=[jax.ShapeDtypeStruct((t, w), F32), jax.ShapeDtypeStruct((n, nh, RW_HEAD, RW_HEAD), F32)],
        scratch_shapes=[pltpu.VMEM((nh, RW_HEAD, RW_HEAD), F32)],
        compiler_params=_params(("arbitrary",)),
    )(sh, sh, sh, sh, sh, *hp)


def _rw_scan_bwd_both(name, sh, hps, states, dy, n_ctx):
    t = sh.shape[0]
    w = (sh.shape[1] - 4 * RW_LORA) // 3
    nh = w // RW_HEAD
    n = t // RW_STEP
    lo = 3 * w // LANE
    n_in, n_p = 13, 6

    def body(*refs):
        step = pl.program_id(0)
        ins = [refs[d * n_in:(d + 1) * n_in] for d in range(2)]
        outs = [refs[2 * n_in + d * (1 + n_p):2 * n_in + (d + 1) * (1 + n_p)] for d in range(2)]
        ds_refs = refs[2 * n_in + 2 * (1 + n_p):]

        @pl.when(step == 0)
        def _():
            for d in range(2):
                ds_refs[d][...] = jnp.zeros_like(ds_refs[d])
                for ref in outs[d][1:]:
                    ref[...] = jnp.zeros_like(ref)

        for d in range(2):
            r_ref, k_ref, v_ref, wl_ref, al_ref = ins[d][:5]
            hp_refs, st_ref, dy_ref = ins[d][5:11], ins[d][11], ins[d][12]
            wl = wl_ref[...][:, d * RW_LORA:(d + 1) * RW_LORA]
            al = al_ref[...][:, d * RW_LORA:(d + 1) * RW_LORA]
            _, vjp = jax.vjp(functools.partial(_rw_step, rev=d == 1),
                             st_ref[...], _to_heads(r_ref[...], nh), _to_heads(k_ref[...], nh), _to_heads(v_ref[...], nh),
                             wl, al, *[p[...] for p in hp_refs])
            g = vjp((_to_heads(dy_ref[...], nh), ds_refs[d][...]))
            ds_refs[d][...] = g[0]
            zero = jnp.zeros_like(g[4])
            lora = [zero] * 4
            lora[d], lora[2 + d] = g[4], g[5]
            outs[d][0][...] = jnp.concatenate([_from_heads(g[1]), _from_heads(g[2]), _from_heads(g[3])] + lora, axis=-1)
            for ref, val in zip(outs[d][1:], g[6:]):
                ref[...] += val

    def rows(d, cb, width=w):
        return pl.BlockSpec((RW_STEP, width), lambda s: (_scan_order(n - 1 - s, n_ctx, n, d == 1), cb))

    def whole(a):
        nd = a.ndim
        return pl.BlockSpec(a.shape, lambda s: (0,) * nd)

    in_specs, operands, out_specs, out_shape = [], [], [], []
    for d in range(2):
        in_specs += [rows(d, 0), rows(d, 1), rows(d, 2), rows(d, lo, LANE), rows(d, lo + 1, LANE)]
        in_specs += [whole(a) for a in hps[d]]
        in_specs += [pl.BlockSpec((None, nh, RW_HEAD, RW_HEAD), lambda s: (n - 1 - s, 0, 0, 0)), rows(d, 0)]
        operands += [sh] * 5 + list(hps[d]) + [states[d], dy]
        out_specs += [rows(d, 0, sh.shape[1])] + [whole(a) for a in hps[d]]
        out_shape += [jax.ShapeDtypeStruct(sh.shape, F32)] + [jax.ShapeDtypeStruct(a.shape, F32) for a in hps[d]]
    res = pl.pallas_call(
        body, name=name, grid=(n,), in_specs=in_specs, out_specs=out_specs, out_shape=out_shape,
        scratch_shapes=[pltpu.VMEM((nh, RW_HEAD, RW_HEAD), F32)] * 2, compiler_params=_params(("arbitrary",)),
    )(*operands)
    return [res[0], res[1 + n_p]], [res[1:1 + n_p], res[2 + n_p:]]


def _shift_masks(t, n_ctx_rows):
    row = lax.broadcasted_iota(jnp.int32, (t, 1), 0)
    isx = row >= n_ctx_rows
    pos = jnp.where(isx, row - n_ctx_rows, row)
    col = jnp.where(isx, jnp.bitwise_and(pos, GRID_W - 1), pos)
    ncol = jnp.where(isx, GRID_W, n_ctx_rows)
    n_x = t - n_ctx_rows
    ml = col != 0
    mr = col != ncol - 1
    mu = isx & (pos >= GRID_W)
    md = isx & (pos < n_x - GRID_W)
    return ml, mr, mu, md, isx


def _shift_fwd(name, p, col0, mu, n_ctx_rows):
    t, c = p.shape[0], mu.shape[1]
    cw = LANE

    def body(p_ref, mu_ref, o_ref):
        x = p_ref[...]
        m = mu_ref[...]
        ml, mr, mup, mdn, isx = _shift_masks(t, n_ctx_rows)
        left = jnp.where(ml, pltpu.roll(x, 1, 0), 0.0)
        right = jnp.where(mr, pltpu.roll(x, t - 1, 0), 0.0)
        up = jnp.where(mup, pltpu.roll(x, GRID_W, 0), 0.0)
        down = jnp.where(mdn, pltpu.roll(x, t - GRID_W, 0), 0.0)
        out = x + m[0:1] * (left - x) + m[1:2] * (right - x)
        vert = m[2:3] * (up - x) + m[3:4] * (down - x)
        o_ref[...] = out + jnp.where(isx, vert, 0.0)

    return pl.pallas_call(
        body, name=name, grid=(c // cw,),
        in_specs=[pl.BlockSpec((t, cw), lambda j: (0, col0 + j)), pl.BlockSpec((4, cw), lambda j: (0, j))],
        out_specs=pl.BlockSpec((t, cw), lambda j: (0, j)),
        out_shape=jax.ShapeDtypeStruct((t, c), F32),
        compiler_params=_params(("parallel",)),
    )(p, mu)


def _shift_bwd(name, p, col0, mu, dparts, n_ctx_rows):
    t, c = p.shape[0], mu.shape[1]
    cw = LANE
    npart = len(dparts)

    def body(*refs):
        p_ref, mu_ref = refs[0], refs[1]
        dp_ref, dmu_ref = refs[2 + npart], refs[3 + npart]
        x = p_ref[...]
        m = mu_ref[...]
        g = refs[2][...]
        for r in refs[3:2 + npart]:
            g = g + r[...]
        ml, mr, mup, mdn, isx = _shift_masks(t, n_ctx_rows)
        left = jnp.where(ml, pltpu.roll(x, 1, 0), 0.0)
        right = jnp.where(mr, pltpu.roll(x, t - 1, 0), 0.0)
        up = jnp.where(mup, pltpu.roll(x, GRID_W, 0), 0.0)
        down = jnp.where(mdn, pltpu.roll(x, t - GRID_W, 0), 0.0)
        gx = jnp.where(isx, g, 0.0)
        dmu_ref[...] = jnp.concatenate([
            jnp.sum(g * (left - x), axis=0, keepdims=True), jnp.sum(g * (right - x), axis=0, keepdims=True),
            jnp.sum(gx * (up - x), axis=0, keepdims=True), jnp.sum(gx * (down - x), axis=0, keepdims=True)], axis=0)
        coef = 1.0 - m[0:1] - m[1:2] - jnp.where(isx, m[2:3] + m[3:4], 0.0)
        dp = coef * g
        dp = dp + m[0:1] * pltpu.roll(jnp.where(ml, g, 0.0), t - 1, 0)
        dp = dp + m[1:2] * pltpu.roll(jnp.where(mr, g, 0.0), 1, 0)
        dp = dp + m[2:3] * pltpu.roll(jnp.where(mup, g, 0.0), t - GRID_W, 0)
        dp = dp + m[3:4] * pltpu.roll(jnp.where(mdn, g, 0.0), GRID_W, 0)
        dp_ref[...] = dp.astype(dp_ref.dtype)

    col = pl.BlockSpec((t, cw), lambda j: (0, j))
    par = pl.BlockSpec((4, cw), lambda j: (0, j))
    return pl.pallas_call(
        body, name=name, grid=(c // cw,),
        in_specs=[pl.BlockSpec((t, cw), lambda j: (0, col0 + j)), par] + [col] * npart,
        out_specs=[col, par],
        out_shape=[jax.ShapeDtypeStruct((t, c), BF16), jax.ShapeDtypeStruct((4, c), F32)],
        compiler_params=_params(("parallel",)),
    )(p, mu, *dparts)


def _add_small(name, terms, shape):
    flat2 = [a.reshape(-1, a.shape[-1]) for a in terms]
    return _rowwise(name, lambda *v: _slot_sum(list(v)), flat2, F32).reshape(shape)


def _local_step(x, ctx, mod, norm_g, w_in_st, hg_lb, hg_norm_g, rw_mu, rw_w0, rw_w2, rw_a0, rw_a2,
                rw_kk, rw_ka, rw_rk, rw_gn_g, rw_gn_b, w_hg_st, w_rw_st, w_out, final_g, tgt, my_core):
    seq, dm = x.shape
    n_ctx_rows = ctx.shape[0]
    t = seq + n_ctx_rows
    hw = hg_norm_g.shape[-1]
    rw = rw_kk.shape[-1]
    nh_rw = rw // RW_HEAD
    n_ctx = n_ctx_rows // STEP
    tm = _tile(n_ctx_rows, (256, 128, 64))
    nt = t // tm
    nct = n_ctx_rows // tm
    n_sh_cols = 3 * rw + 4 * RW_LORA

    final_g2 = final_g.reshape(1, dm)
    add = _add_small
    mod3 = mod.reshape(8, 3, dm)

    def pick(i, m3):
        r = jnp.where(i < nct, m3[1], m3[0])
        return r[0:1], r[1:2]

    tokens = [(ctx, 0, dm, 0), (x, 0, dm, nct)]

    def h_fn(i, r, f):
        shift, scale = pick(i, f[1])
        return [_fn_h(jnp.where(i < nct, r[0], r[1]), f[0], scale, shift)], []

    (h,) = _row_call("h_fwd", h_fn, nt, tm, tokens, [norm_g, mod3], [(t, dm, BF16, 0)], [])
    proj = _join_columns("proj_join", _mm_n_st("proj_mm", h, w_in_st))
    p_hg = proj
    rs_tile0 = 5 * hw // LANE
    p_zr = proj[:, 5 * hw + n_sh_cols:5 * hw + n_sh_cols + rw]
    p_gt = proj[:, 5 * hw + n_sh_cols + rw:]

    o_hg, st_hg = [], []
    for d in range(2):
        o, st = _hg_scan_fwd(f"hg_scan_fwd{d}", p_hg, hg_lb[d], d, n_ctx)
        o_hg.append(o)
        st_hg.append(st)

    def hgpost_fn(i, r, f):
        return [_fn_hgpost(r[0], r[1], r[2], f[0])], []

    hg_in = [(o_hg[0], 0, hw, 0), (o_hg[1], 0, hw, 0), (p_hg, 4, hw, 0)]
    (y_hg,) = _row_call("hg_post", hgpost_fn, nt, tm, hg_in, [hg_norm_g], [(t, hw, BF16, 0)], [])

    sh = _shift_fwd("rw_shift", proj, rs_tile0, rw_mu, n_ctx_rows)
    hps = []
    for d in range(2):
        hps.append([rw_w0[d].reshape(nh_rw, 1, RW_HEAD), jnp.swapaxes(rw_w2[d].reshape(RW_LORA, nh_rw, RW_HEAD), 0, 1),
                    rw_a0[d].reshape(nh_rw, 1, RW_HEAD), jnp.swapaxes(rw_a2[d].reshape(RW_LORA, nh_rw, RW_HEAD), 0, 1),
                    rw_kk.reshape(nh_rw, 1, RW_HEAD), rw_ka.reshape(nh_rw, 1, RW_HEAD)])
    y_rw_d, st_rw = [], []
    for d in range(2):
        y, st = _rw_scan_fwd(f"rw_scan_fwd{d}", sh, hps[d], d, n_ctx_rows // RW_STEP)
        y_rw_d.append(y)
        st_rw.append(st)

    rw_full = [rw_a0, rw_a2, rw_ka, rw_rk, rw_gn_g, rw_gn_b]
    lo = 3 * rw // LANE
    rw_in = [(y_rw_d[0], 0, rw, 0), (y_rw_d[1], 0, rw, 0), (sh, 0, rw, 0), (sh, 1, rw, 0), (sh, 2, rw, 0),
             (sh, lo + 1, LANE, 0), (p_zr, 0, rw, 0)]

    def rwpost_fn(i, r, f):
        return [_fn_rwpost(*r, *f)], []

    (y_rw,) = _row_call("rw_post", rwpost_fn, nt, tm, rw_in, rw_full, [(t, rw, BF16, 0)], [])

    a_hg = _mm_n_st("hg_out_mm", y_hg, w_hg_st, joined=True)
    a_rw = _mm_n_st("rw_out_mm", y_rw, w_rw_st, joined=True)
    mg_in = [(a_hg, 0, dm, 0), (a_rw, 0, dm, 0), (p_gt, 0, dm, 0), (p_gt, 1, dm, 0)]
    (merged,) = _row_call("merge", lambda i, r, f: ([_fn_merge(*r)], []), nt, tm, mg_in, [], [(t, dm, BF16, 0)], [])
    o_out = _mm_nn("out_mm", merged, w_out)

    def final_fn(i, r, f):
        gate = f[0][0][2:3]
        loss, vjp = jax.vjp(_fn_final, r[0], r[1], gate, f[1], r[2])
        dx, do, dgate, dfg, _ = vjp(jnp.ones((), F32))
        live = i >= nct
        zero = lambda a: jnp.where(live, a, 0.0)
        dmod = jnp.concatenate([jnp.concatenate([jnp.zeros((1, 2 * dm), F32), zero(dgate)], axis=1),
                                jnp.zeros((7, 3 * dm), F32)], axis=0)
        return [zero(dx), zero(do)], [jnp.broadcast_to(zero(loss), (8, LANE)), dmod, zero(dfg)]

    fin_in = [(x, 0, dm, nct), (o_out, 0, dm, 0), (tgt, 0, dm, nct)]
    dx_res, d_o, loss_acc, dmod_gate, d_final_g = _row_call(
        "final", final_fn, nt, tm, fin_in, [mod3, final_g2], [(t, dm, F32, 0), (t, dm, BF16, 0)],
        [((8, LANE), F32), ((8, 3 * dm), F32), ((1, dm), F32)])

    g_w_out = _mm_tn("d_w_out", merged, d_o)
    d_merged = _mm_nt("d_merged", d_o, w_out)

    def merge_bwd(i, r, f):
        _, vjp = jax.vjp(_fn_merge, r[0], r[1], r[2], r[3])
        da, db, dgh, dgr = vjp(r[4])
        return [da, db, jnp.concatenate([dgh, dgr], axis=1)], []

    da_hg, da_rw, dp_gt = _row_call("merge_bwd", merge_bwd, nt, tm, mg_in + [(d_merged, 0, dm, 0)], [],
                                    [(t, dm, BF16, 0), (t, dm, BF16, 0), (t, 2 * dm, BF16, 0)], [])
    g_w_hg_st = _mm_t_st("d_w_hg", y_hg, da_hg)
    g_w_rw_st = _mm_t_st("d_w_rw", y_rw, da_rw)
    dy_hg = _mm_st_t("d_y_hg", da_hg, w_hg_st)
    dy_rw = _mm_st_t("d_y_rw", da_rw, w_rw_st)

    def hgpost_bwd(i, r, f):
        _, vjp = jax.vjp(_fn_hgpost, r[0], r[1], r[2], f[0])
        dof, _, dz, dg = vjp(r[3])
        return [dof, dz], [dg]

    do_hg, dz_hg, g_hg_norm = _row_call("hg_post_bwd", hgpost_bwd, nt, tm, hg_in + [(dy_hg, 0, hw, 0)], [hg_norm_g],
                                        [(t, hw, F32, 0), (t, hw, BF16, 0)], [((1, hw), F32)])
    dq0, di0, df0, dlb0 = _hg_scan_bwd("hg_scan_bwd0", p_hg, hg_lb[0], st_hg[0], do_hg, 0, n_ctx)
    dq, di, df1, dlb1 = _hg_scan_bwd("hg_scan_bwd1", p_hg, hg_lb[1], st_hg[1], do_hg, 1, n_ctx, other=(dq0, di0))
    g_hg_lb = jnp.stack([dlb0, dlb1], axis=0)

    def rwpost_bwd(i, r, f):
        _, vjp = jax.vjp(_fn_rwpost, *r[:7], *f)
        g = vjp(r[7])
        zl = jnp.zeros((g[5].shape[0], 2 * RW_LORA), F32)
        return [g[0], jnp.concatenate([g[2], g[3], g[4], zl, g[5]], axis=1), g[6]], list(g[7:])

    dy_sum, dsh_p, dz_rw, g_a0_p, g_a2_p, g_ka_p, g_rk, g_gn_g, g_gn_b = _row_call(
        "rw_post_bwd", rwpost_bwd, nt, tm, rw_in + [(dy_rw, 0, rw, 0)], rw_full,
        [(t, rw, F32, 0), (t, n_sh_cols, F32, 0), (t, rw, BF16, 0)], [(a.shape, F32) for a in rw_full])
    dsh_dirs, hp_grads = _rw_scan_bwd_both("rw_scan_bwd", sh, hps, st_rw, dy_sum, n_ctx_rows // RW_STEP)
    dp_rs, g_mu = _shift_bwd("rw_shift_bwd", proj, rs_tile0, rw_mu, [dsh_p] + dsh_dirs, n_ctx_rows)

    def flat(a):
        if a.shape[1] == 1:
            return a.reshape(rw)
        return jnp.swapaxes(a, 0, 1).reshape(RW_LORA, rw)

    g_w0 = jnp.stack([flat(hp_grads[d][0]) for d in range(2)], axis=0)
    g_w2 = jnp.stack([flat(hp_grads[d][1]) for d in range(2)], axis=0)
    g_a0 = add("g_a0", [jnp.stack([flat(hp_grads[d][2]) for d in range(2)], axis=0), g_a0_p], (2, rw))
    g_a2 = add("g_a2", [jnp.stack([flat(hp_grads[d][3]) for d in range(2)], axis=0), g_a2_p], (2, RW_LORA, rw))
    g_kk = add("g_kk", [flat(hp_grads[0][4]).reshape(1, rw), flat(hp_grads[1][4]).reshape(1, rw)], (1, rw))
    g_ka = add("g_ka", [flat(hp_grads[0][5]).reshape(1, rw), flat(hp_grads[1][5]).reshape(1, rw), g_ka_p], (1, rw))

    dproj_st = _split_columns("dproj_split", [dq, di, df0, df1, dz_hg, dp_rs, dz_rw, dp_gt], N_SHARD)
    g_small = {"hg_lb": g_hg_lb, "rw_mu": g_mu, "rw_w0": g_w0, "rw_w2": g_w2, "rw_a0": g_a0, "rw_a2": g_a2}
    split = {n: _split_shards(g_small[n]) for n in _SMALL_SHARDED}
    small_parts = jnp.stack([_pack_small({n: split[n][j] for n in _SMALL_SHARDED}) for j in range(N_SHARD)], axis=0)
    early = {"w_hg_out": g_w_hg_st, "w_rw_out": g_w_rw_st, "w_out": g_w_out.reshape(N_SHARD, dm // N_SHARD, dm),
             "small": small_parts}
    early_chip = [_pair_exchange(f"grads_pair_sum_{n}", a, my_core, True, BF16) for n, a in early.items()]
    g_w_in_st, early_landed = _mm_t_st("d_w_in", h, dproj_st, scatter=tuple(early_chip))
    w_in_chip = _pair_exchange("grads_pair_sum_w_in", g_w_in_st, my_core, True, BF16)
    dh, (w_in_landed,) = _mm_st_t("d_h", dproj_st, w_in_st, scatter=(w_in_chip,))

    def h_bwd(i, r, f):
        shift, scale = pick(i, f[1])
        is_ctx = i < nct
        _, vjp = jax.vjp(_fn_h, jnp.where(is_ctx, r[0], r[1]), f[0], scale, shift)
        ds, dg, dscale, dshift = vjp(r[2])
        row = jnp.concatenate([dshift, dscale, jnp.zeros((1, dm), F32)], axis=1)
        z = jnp.zeros_like(row)
        dmod = jnp.concatenate([jnp.where(is_ctx, z, row), jnp.where(is_ctx, row, z), jnp.zeros((6, 3 * dm), F32)], axis=0)
        return [ds + r[3]], [dg, dmod]

    grad_x, g_norm_g, dmod_h = _row_call(
        "h_bwd", h_bwd, nt, tm, tokens + [(dh, 0, dm, 0), (dx_res, 0, dm, 0)], [norm_g, mod3],
        [(seq, dm, F32, nct)], [((1, dm), F32), ((8, 3 * dm), F32)])
    dmod = add("d_mod", [dmod_h, dmod_gate], (8, 3 * dm))
    grads = dict(
        norm_g=g_norm_g, w_in=(w_in_chip, w_in_landed), hg_norm_g=g_hg_norm, rw_kk=g_kk, rw_ka=g_ka,
        rw_rk=g_rk, rw_gn_g=g_gn_g, rw_gn_b=g_gn_b, final_g=d_final_g.reshape(dm))
    grads.update(zip(early, zip(early_chip, early_landed)))
    return loss_acc[0:1, 0:1], grad_x, dmod, grads


def _my_place():
    return lax.axis_index("x"), lax.axis_index("y"), lax.axis_index("c")


MIN_CHUNK_BYTES = 1 << 18
ROW_ALIGN = 16


def _n_chunks(rows, row_bytes):
    for n in (8, 4, 2):
        if rows % (n * ROW_ALIGN) == 0 and rows // n * row_bytes >= MIN_CHUNK_BYTES:
            return n
    return 1


def _row_bytes(a, lead=1):
    n = a.dtype.itemsize
    for d in a.shape[lead:]:
        n *= d
    return n


def _rows(ref, start, size):
    return ref.at[pl.ds(start, size)]


def _chunked(make, start, size, n):
    cs = size // n
    return [make(start + j * cs, cs) for j in range(n)]


_PEER_CHIPS = 3


def _weights_gather(name, big, small):
    nb, na = len(big), len(big) + len(small)
    arrays = list(big) + list(small)
    n_ici = 6

    def body(*refs):
        outs = refs[na:2 * na]
        send_sems, recv_sems, fsend_sems, frecv_sems = refs[2 * na:]
        x, y, c = _my_place()
        me, sx, sy, sd = 2 * x + y, 2 * (1 - x) + y, 2 * x + (1 - y), 2 * (1 - x) + (1 - y)
        kx, ky, kd = (1 - x, y, c), (x, 1 - y, c), (1 - x, 1 - y, c)

        def ici(a, j, src_slot, dst_slot, to, r0, nr):
            return pltpu.make_async_remote_copy(
                src_ref=_rows(outs[a].at[src_slot], r0, nr), dst_ref=_rows(outs[a].at[dst_slot], r0, nr),
                send_sem=send_sems.at[a, j], recv_sem=recv_sems.at[a, j], device_id=to,
                device_id_type=pl.DeviceIdType.MESH)

        def to_sibling(a, k, slot, r0, nr):
            rows = _rows(outs[a].at[slot], r0, nr)
            return pltpu.make_async_remote_copy(
                src_ref=rows, dst_ref=rows, send_sem=fsend_sems.at[a, k], recv_sem=frecv_sems.at[a, k],
                device_id=(x, y, 1 - c), device_id_type=pl.DeviceIdType.MESH)

        def start(copies):
            for cp in copies:
                cp.start()

        geo = []
        for a in range(nb):
            half = arrays[a].shape[1] // 2
            geo.append((pl.multiple_of(c * half, ROW_ALIGN), pl.multiple_of((1 - c) * half, ROW_ALIGN), half // 2,
                        _n_chunks(half // 2, _row_bytes(arrays[a], 2))))
        plan = [(me, sx, kx, 0), (me, sx, kx, 1), (me, sy, ky, 0), (me, sy, ky, 1), (sx, sd, ky, 0), (sy, sd, kx, 1)]

        def piece(a, j):
            return geo[a][0] + plan[j][3] * geo[a][2]

        for a in range(nb):
            for j in range(4):
                start(_chunked(lambda r0, cs: ici(a, j, me, me, plan[j][2], r0, cs), piece(a, j), geo[a][2], geo[a][3]))
        for a in range(nb, na):
            rows = arrays[a].shape[1]
            for j, to in ((0, kx), (2, ky), (1, kd)):
                ici(a, j, me, me, to, 0, rows).start()
        for a in range(nb):
            for j, first in ((4, 0), (5, 3)):
                src_slot, _, to, _ = plan[j]
                ici(a, first, me, plan[first][1], plan[first][2], piece(a, first), geo[a][2]).wait_recv()
                start(_chunked(lambda r0, cs: ici(a, j, src_slot, src_slot, to, r0, cs), piece(a, j), geo[a][2], geo[a][3]))
        for a in range(nb):
            for j in (1, 2):
                ici(a, j, me, plan[j][1], plan[j][2], piece(a, j), geo[a][2]).wait_recv()
            for k, slot in ((0, sx), (1, sy)):
                start(_chunked(lambda r0, cs: to_sibling(a, k, slot, r0, cs), geo[a][0], 2 * geo[a][2], geo[a][3]))
        for a in range(nb):
            for j in (4, 5):
                ici(a, j, me, sd, plan[j][2], piece(a, j), geo[a][2]).wait_recv()
            start(_chunked(lambda r0, cs: to_sibling(a, 2, sd, r0, cs), geo[a][0], 2 * geo[a][2], geo[a][3]))
        for a in range(nb, na):
            rows = arrays[a].shape[1]
            for j, slot, to in ((0, sx, kx), (2, sy, ky), (1, sd, kd)):
                ici(a, j, me, slot, to, 0, rows).wait_recv()
        for a in range(nb):
            for k, slot in ((0, sx), (1, sy), (2, sd)):
                to_sibling(a, k, slot, geo[a][1], 2 * geo[a][2]).wait_recv()
        for a in range(nb):
            for j in range(n_ici):
                ici(a, j, me, me, plan[j][2], piece(a, j), geo[a][2]).wait_send()
            for k, slot in ((0, sx), (1, sy), (2, sd)):
                to_sibling(a, k, slot, geo[a][0], 2 * geo[a][2]).wait_send()
        for a in range(nb, na):
            rows = arrays[a].shape[1]
            for j, to in ((0, kx), (2, ky), (1, kd)):
                ici(a, j, me, me, to, 0, rows).wait_send()

    hbm = pl.BlockSpec(memory_space=pl.ANY)
    ici_sems = pltpu.SemaphoreType.DMA((na, n_ici))
    pair_sems = pltpu.SemaphoreType.DMA((na, _PEER_CHIPS))
    return pl.pallas_call(
        body, name=name, in_specs=[hbm] * na, out_specs=[hbm] * na,
        out_shape=[jax.ShapeDtypeStruct(a.shape, a.dtype) for a in arrays],
        input_output_aliases={a: a for a in range(na)}, scratch_shapes=[ici_sems, ici_sems, pair_sems, pair_sems],
    )(*arrays)


def _scatter_copy(arrays, ins, outs, send_sems, recv_sems, a, k, slot, r0, nr):
    x, y, c = _my_place()
    px, py = [(1 - x, y), (x, 1 - y), (1 - x, 1 - y)][k]
    return pltpu.make_async_remote_copy(
        src_ref=_rows(ins[a].at[2 * px + py], r0, nr), dst_ref=_rows(outs[a].at[slot], r0, nr),
        send_sem=send_sems.at[a, k], recv_sem=recv_sems.at[a, k], device_id=(px, py, c),
        device_id_type=pl.DeviceIdType.MESH)


def _scatter_start(arrays, ins, outs, send_sems, recv_sems):
    x, y, _ = _my_place()
    for a in range(len(arrays)):
        rows = arrays[a].shape[1]
        for k in range(_PEER_CHIPS):
            for cp in _chunked(lambda r0, cs: _scatter_copy(arrays, ins, outs, send_sems, recv_sems, a, k, 2 * x + y, r0, cs),
                               0, rows, _n_chunks(rows, _row_bytes(arrays[a], 2))):
                cp.start()


def _scatter_wait(arrays, ins, outs, send_sems, recv_sems):
    x, y, _ = _my_place()
    peer_slot = [2 * (1 - x) + y, 2 * x + (1 - y), 2 * (1 - x) + (1 - y)]
    for k in range(_PEER_CHIPS):
        for a in range(len(arrays)):
            _scatter_copy(arrays, ins, outs, send_sems, recv_sems, a, k, peer_slot[k], 0, arrays[a].shape[1]).wait_recv()
    for a in range(len(arrays)):
        for k in range(_PEER_CHIPS):
            _scatter_copy(arrays, ins, outs, send_sems, recv_sems, a, k, 2 * x + y, 0, arrays[a].shape[1]).wait_send()


PAIR_TILE_BYTES = 4 << 20


def _pair_exchange(name, a, place, reduce, out_dtype):
    rows, cols = a.shape[-2], a.shape[-1]
    half = rows // 2 if reduce else rows
    tr = _row_tile_for(half, cols, budget=PAIR_TILE_BYTES)
    nh = half // tr
    n_steps = (N_SHARD if reduce else 1) * nh

    def body(pc_ref, *refs):
        if reduce:
            keep_ref, send_ref, o_ref, land, send_sems, recv_sems, credit, wire = refs
            wire[...] = send_ref[...].astype(BF16)
            src = wire
        else:
            send_ref, o_ref, land, send_sems, recv_sems, credit = refs
            src = send_ref
        x, y, c = _my_place()
        other = (x, y, 1 - c)
        t = pl.program_id(0) * nh + pl.program_id(1) if reduce else pl.program_id(0)
        slot = t % 2

        @pl.when(t >= 2)
        def _():
            pl.semaphore_wait(credit, 1)

        copy = pltpu.make_async_remote_copy(
            src_ref=src, dst_ref=land.at[slot], send_sem=send_sems.at[slot], recv_sem=recv_sems.at[slot],
            device_id=other, device_id_type=pl.DeviceIdType.MESH)
        copy.start()
        copy.wait_recv()
        got = land[slot]
        o_ref[...] = ((keep_ref[...] + got.astype(F32)) if reduce else got).astype(out_dtype)
        copy.wait_send()

        @pl.when(t < n_steps - 2)
        def _():
            pl.semaphore_signal(credit, inc=1, device_id=other, device_id_type=pl.DeviceIdType.MESH)

    if reduce:
        grid = (N_SHARD, nh)
        in_specs = [pl.BlockSpec((None, tr, cols), lambda j, i, pc: (j, pc[0] * nh + i, 0)),
                    pl.BlockSpec((None, tr, cols), lambda j, i, pc: (j, (1 - pc[0]) * nh + i, 0))]
        out_spec = pl.BlockSpec((None, tr, cols), lambda j, i, pc: (j, i, 0))
        out_shape = jax.ShapeDtypeStruct((N_SHARD, half, cols), out_dtype)
        operands = (a, a)
        sem = ("arbitrary", "arbitrary")
    else:
        grid = (nh,)
        in_specs = [pl.BlockSpec((tr, cols), lambda i, pc: (i, 0))]
        out_spec = pl.BlockSpec((tr, cols), lambda i, pc: (i, 0))
        out_shape = jax.ShapeDtypeStruct((half, cols), out_dtype)
        operands = (a,)
        sem = ("arbitrary",)
    return pl.pallas_call(
        body, name=name,
        grid_spec=pltpu.PrefetchScalarGridSpec(
            num_scalar_prefetch=1, grid=grid, in_specs=in_specs, out_specs=out_spec,
            scratch_shapes=[pltpu.VMEM((2, tr, cols), BF16 if reduce else a.dtype), pltpu.SemaphoreType.DMA((2,)),
                            pltpu.SemaphoreType.DMA((2,)), pltpu.SemaphoreType.REGULAR] +
                           ([pltpu.VMEM((tr, cols), BF16)] if reduce else [])),
        out_shape=out_shape, compiler_params=_params(sem),
    )(place, *operands)


def _cast_into_slot(name, a, chip):
    rows, cols = a.shape
    tm = _row_tile_for(rows, cols)

    def body(pc_ref, a_ref, o_ref):
        o_ref[...] = a_ref[...].astype(BF16)

    return pl.pallas_call(
        body, name=name,
        grid_spec=pltpu.PrefetchScalarGridSpec(
            num_scalar_prefetch=1, grid=(rows // tm,), in_specs=[pl.BlockSpec((tm, cols), lambda i, pc: (i, 0))],
            out_specs=pl.BlockSpec((None, tm, cols), lambda i, pc: (pc[0], i, 0))),
        out_shape=jax.ShapeDtypeStruct((N_SHARD, rows, cols), BF16), compiler_params=_params(("parallel",)),
    )(chip, a)


SUM_SWAP_TILE_BYTES = 2 << 20


def _sum_and_swap(name, landed, sent, chip):
    ns, rows, cols = landed.shape
    tr = _row_tile_for(rows, cols, budget=SUM_SWAP_TILE_BYTES)
    n_steps = rows // tr

    def body(pc_ref, *refs):
        own_ref, mine_ref, theirs_ref, wire, land, send_sems, recv_sems, credit = refs[ns:]
        me = pc_ref[0]
        total = _slot_sum([jnp.where(me == j, own_ref[...], refs[j][...]).astype(F32) for j in range(ns)])
        mine_ref[...] = total
        wire[...] = total
        x, y, c = _my_place()
        other = (x, y, 1 - c)
        t = pl.program_id(0)
        slot = t % 2

        @pl.when(t >= 2)
        def _():
            pl.semaphore_wait(credit, 1)

        copy = pltpu.make_async_remote_copy(
            src_ref=wire, dst_ref=land.at[slot], send_sem=send_sems.at[slot], recv_sem=recv_sems.at[slot],
            device_id=other, device_id_type=pl.DeviceIdType.MESH)
        copy.start()
        copy.wait_recv()
        theirs_ref[...] = land[slot]
        copy.wait_send()

        @pl.when(t < n_steps - 2)
        def _():
            pl.semaphore_signal(credit, inc=1, device_id=other, device_id_type=pl.DeviceIdType.MESH)

    def landed_spec(j):
        return pl.BlockSpec((None, tr, cols), lambda i, pc: (jnp.where(pc[0] == j, (j + 1) % ns, j), i, 0))

    out = pl.BlockSpec((tr, cols), lambda i, pc: (i, 0))
    return pl.pallas_call(
        body, name=name,
        grid_spec=pltpu.PrefetchScalarGridSpec(
            num_scalar_prefetch=1, grid=(n_steps,),
            in_specs=[landed_spec(j) for j in range(ns)] + [pl.BlockSpec((None, tr, cols), lambda i, pc: (pc[0], i, 0))],
            out_specs=[out, out],
            scratch_shapes=[pltpu.VMEM((tr, cols), F32), pltpu.VMEM((2, tr, cols), F32), pltpu.SemaphoreType.DMA((2,)),
                            pltpu.SemaphoreType.DMA((2,)), pltpu.SemaphoreType.REGULAR]),
        out_shape=[jax.ShapeDtypeStruct((rows, cols), F32)] * 2, compiler_params=_params(("arbitrary",)),
    )(chip, *([landed] * ns), sent)


def _gather_all(name, a):
    def body(in_ref, out_ref, send_sems, recv_sems, local_sem):
        x, y, c = _my_place()
        me = 4 * x + 2 * y + c

        def peer(k):
            return (x ^ (k >> 2), y ^ ((k >> 1) & 1), c ^ (k & 1))

        def remote(k, land):
            return pltpu.make_async_remote_copy(
                src_ref=in_ref, dst_ref=out_ref.at[land], send_sem=send_sems.at[k - 1], recv_sem=recv_sems.at[k - 1],
                device_id=peer(k), device_id_type=pl.DeviceIdType.MESH)

        local = pltpu.make_async_copy(in_ref, out_ref.at[me], local_sem)
        local.start()
        for k in range(1, N_DEV):
            remote(k, me).start()
        for k in range(1, N_DEV):
            px, py, pc = peer(k)
            remote(k, 4 * px + 2 * py + pc).wait_recv()
        for k in range(1, N_DEV):
            remote(k, me).wait_send()
        local.wait()

    hbm = pl.BlockSpec(memory_space=pl.ANY)
    return pl.pallas_call(
        body, name=name, in_specs=[hbm], out_specs=hbm,
        out_shape=jax.ShapeDtypeStruct((N_DEV,) + a.shape, a.dtype),
        scratch_shapes=[pltpu.SemaphoreType.DMA((N_DEV - 1,)), pltpu.SemaphoreType.DMA((N_DEV - 1,)), pltpu.SemaphoreType.DMA],
    )(a)


def _row_tile_for(rows, cols, budget=1 << 20):
    if rows * cols * 4 <= budget:
        return rows
    for tm in (1024, 512, 256, 128, 64, 32, 16, 8):
        if rows % tm == 0 and tm * cols * 4 <= budget:
            return tm
    return rows


def _slot_sum(vals):
    g = vals[0]
    for v in vals[1:]:
        g = g + v
    return g


def _rowwise(name, fn, arrays, out_dtype):
    rows, cols = arrays[0].shape
    tm = _row_tile_for(rows, cols)

    def body(*refs):
        refs[-1][...] = fn(*[r[...] for r in refs[:-1]]).astype(out_dtype)

    blk = pl.BlockSpec((tm, cols), lambda i: (i, 0))
    return pl.pallas_call(
        body, name=name, grid=(rows // tm,), in_specs=[blk] * len(arrays), out_specs=blk,
        out_shape=jax.ShapeDtypeStruct((rows, cols), out_dtype), compiler_params=_params(("parallel",)),
    )(*arrays)


def _sum_slots(name, st):
    ns, rows, cols = st.shape
    tm = _row_tile_for(rows, cols)

    def body(s_ref, o_ref):
        o_ref[...] = _slot_sum([s_ref[j].astype(F32) for j in range(ns)])

    return pl.pallas_call(
        body, name=name, grid=(rows // tm,),
        in_specs=[pl.BlockSpec((ns, tm, cols), lambda i: (0, i, 0))],
        out_specs=pl.BlockSpec((tm, cols), lambda i: (i, 0)),
        out_shape=jax.ShapeDtypeStruct((rows, cols), F32),
        compiler_params=_params(("parallel",)),
    )(st)


ADAM_TILE_BYTES = 1 << 20


def _adam_update(g, p_ref, m_ref, v_ref, go_ref, d_ref, mo_ref, vo_ref):
    mn = ADAM_B1 * m_ref[...] + (1.0 - ADAM_B1) * g
    vn = ADAM_B2 * v_ref[...] + (1.0 - ADAM_B2) * jnp.square(g)
    m_hat = mn / (1.0 - ADAM_B1 ** ADAM_STEP)
    v_hat = vn / (1.0 - ADAM_B2 ** ADAM_STEP)
    go_ref[...] = g
    d_ref[...] = -ADAM_LR * (m_hat / (jnp.sqrt(v_hat) + ADAM_EPS) + ADAM_WD * p_ref[...])
    mo_ref[...] = mn
    vo_ref[...] = vn


def _adamw(name, p, m, v, gst):
    rows, cols = p.shape
    ns = gst.shape[0]
    tm = _row_tile_for(rows, cols, budget=ADAM_TILE_BYTES)

    def body(p_ref, m_ref, v_ref, g_ref, *outs):
        _adam_update(_slot_sum([g_ref[j] for j in range(ns)]), p_ref, m_ref, v_ref, *outs)

    blk = pl.BlockSpec((tm, cols), lambda i: (i, 0))
    return pl.pallas_call(
        body, name=name, grid=(rows // tm,),
        in_specs=[blk, blk, blk, pl.BlockSpec((ns, tm, cols), lambda i: (0, i, 0))],
        out_specs=[blk] * 4, out_shape=[jax.ShapeDtypeStruct((rows, cols), F32)] * 4,
        compiler_params=_params(("parallel",)),
    )(p, m, v, gst)


def _adamw_halves(name, p, m, v, mine, theirs, place, scatter=()):
    rows, cols = p.shape
    half = rows // 2
    tm = _row_tile_for(half, cols, budget=ADAM_TILE_BYTES)
    nh = half // tm
    ns = len(scatter)

    def body(pc_ref, p_ref, m_ref, v_ref, mine_ref, theirs_ref, *refs):
        if ns:
            sc_refs = (refs[:ns], refs[ns + 4:2 * ns + 4]) + tuple(refs[2 * ns + 4:])
            at = pl.program_id(0) * nh + pl.program_id(1)
            pl.when(at == 0)(lambda: _scatter_start(scatter, *sc_refs))
        g = jnp.where(pl.program_id(0) == pc_ref[0], mine_ref[...], theirs_ref[...])
        _adam_update(g, p_ref, m_ref, v_ref, *refs[ns:ns + 4])
        if ns:
            pl.when(at == 2 * nh - 1)(lambda: _scatter_wait(scatter, *sc_refs))

    blk = pl.BlockSpec((tm, cols), lambda h, i, pc: (h * nh + i, 0))
    hblk = pl.BlockSpec((tm, cols), lambda h, i, pc: (i, 0))
    hbm = pl.BlockSpec(memory_space=pl.ANY)
    res = pl.pallas_call(
        body, name=name,
        grid_spec=pltpu.PrefetchScalarGridSpec(
            num_scalar_prefetch=1, grid=(2, nh), in_specs=[blk, blk, blk, hblk, hblk] + [hbm] * ns,
            out_specs=[blk] * 4 + [hbm] * ns,
            scratch_shapes=[pltpu.SemaphoreType.DMA((ns, _PEER_CHIPS))] * 2 if ns else []),
        out_shape=[jax.ShapeDtypeStruct((rows, cols), F32)] * 4 + [jax.ShapeDtypeStruct(s.shape, s.dtype) for s in scatter],
        compiler_params=_params(("arbitrary", "arbitrary") if ns else ("parallel", "parallel")),
    )(place, p, m, v, mine, theirs, *scatter)
    return (list(res[:4]), list(res[4:])) if ns else res


def _pack(parts, width=LANE, mult=8):
    flat = jnp.concatenate([a.reshape(-1) for a in parts])
    n = flat.shape[0]
    per = width * mult
    total = -(-n // per) * per
    return jnp.pad(flat, (0, total - n)).reshape(total // width, width)


def _unpack(packed, shapes):
    flat = packed.reshape(-1)
    out, off = [], 0
    for s in shapes:
        n = 1
        for d in s:
            n *= d
        out.append(flat[off:off + n].reshape(s))
        off += n
    return out


_SMALL_SHARDED = ("hg_lb", "rw_mu", "rw_w0", "rw_w2", "rw_a0", "rw_a2")
_REPLICATED = ("c_ctx", "ada_b", "norm_g", "hg_norm_g", "rw_kk", "rw_ka", "rw_rk", "rw_gn_g", "rw_gn_b", "final_g")
_GATHERED = ("w_in", "w_hg_out", "w_rw_out", "w_out")
_WEIGHTS = ("c_ctx", "ada_w", "ada_b", "norm_g", "w_in", "hg_lb", "hg_norm_g", "rw_mu", "rw_w0", "rw_w2", "rw_a0", "rw_a2",
            "rw_kk", "rw_ka", "rw_rk", "rw_gn_g", "rw_gn_b", "w_hg_out", "w_rw_out", "w_out", "final_g")


def _pack_small(d):
    return _pack([d[n] for n in _SMALL_SHARDED], mult=2 * ROW_ALIGN)


def _join_shards(st):
    a = jnp.moveaxis(st, 0, -2)
    return a.reshape(a.shape[:-2] + (a.shape[-2] * a.shape[-1],))


def _split_shards(a):
    s = a.reshape(a.shape[:-1] + (N_SHARD, a.shape[-1] // N_SHARD))
    return jnp.moveaxis(s, -2, 0)


def kernel(x, c, ctx, c_ctx, ada_w, ada_b, norm_g, w_in, hg_lb, hg_norm_g, rw_mu, rw_w0, rw_w2, rw_a0, rw_a2, rw_kk, rw_ka, rw_rk, rw_gn_g, rw_gn_b, w_hg_out, w_rw_out, w_out, final_g, loss_target, m_c_ctx, m_ada_w, m_ada_b, m_norm_g, m_w_in, m_hg_lb, m_hg_norm_g, m_rw_mu, m_rw_w0, m_rw_w2, m_rw_a0, m_rw_a2, m_rw_kk, m_rw_ka, m_rw_rk, m_rw_gn_g, m_rw_gn_b, m_w_hg_out, m_w_rw_out, m_w_out, m_final_g, v_c_ctx, v_ada_w, v_ada_b, v_norm_g, v_w_in, v_hg_lb, v_hg_norm_g, v_rw_mu, v_rw_w0, v_rw_w2, v_rw_a0, v_rw_a2, v_rw_kk, v_rw_ka, v_rw_rk, v_rw_gn_g, v_rw_gn_b, v_w_hg_out, v_w_rw_out, v_w_out, v_final_g):
    w = dict(c_ctx=c_ctx, ada_w=ada_w, ada_b=ada_b, norm_g=norm_g, w_in=w_in, hg_lb=hg_lb, hg_norm_g=hg_norm_g, rw_mu=rw_mu,
             rw_w0=rw_w0, rw_w2=rw_w2, rw_a0=rw_a0, rw_a2=rw_a2, rw_kk=rw_kk, rw_ka=rw_ka, rw_rk=rw_rk, rw_gn_g=rw_gn_g,
             rw_gn_b=rw_gn_b, w_hg_out=w_hg_out, w_rw_out=w_rw_out, w_out=w_out, final_g=final_g)
    m = dict(c_ctx=m_c_ctx, ada_w=m_ada_w, ada_b=m_ada_b, norm_g=m_norm_g, w_in=m_w_in, hg_lb=m_hg_lb, hg_norm_g=m_hg_norm_g,
             rw_mu=m_rw_mu, rw_w0=m_rw_w0, rw_w2=m_rw_w2, rw_a0=m_rw_a0, rw_a2=m_rw_a2, rw_kk=m_rw_kk, rw_ka=m_rw_ka,
             rw_rk=m_rw_rk, rw_gn_g=m_rw_gn_g, rw_gn_b=m_rw_gn_b, w_hg_out=m_w_hg_out, w_rw_out=m_w_rw_out, w_out=m_w_out,
             final_g=m_final_g)
    v = dict(c_ctx=v_c_ctx, ada_w=v_ada_w, ada_b=v_ada_b, norm_g=v_norm_g, w_in=v_w_in, hg_lb=v_hg_lb, hg_norm_g=v_hg_norm_g,
             rw_mu=v_rw_mu, rw_w0=v_rw_w0, rw_w2=v_rw_w2, rw_a0=v_rw_a0, rw_a2=v_rw_a2, rw_kk=v_rw_kk, rw_ka=v_rw_ka,
             rw_rk=v_rw_rk, rw_gn_g=v_rw_gn_g, rw_gn_b=v_rw_gn_b, w_hg_out=v_w_hg_out, w_rw_out=v_w_rw_out, w_out=v_w_out,
             final_g=v_final_g)

    def mat(a):
        return a.reshape(a.shape[-2], a.shape[-1])

    my_core = lax.axis_index("c").astype(jnp.int32).reshape(1)
    my_chip = (2 * lax.axis_index("x") + lax.axis_index("y")).astype(jnp.int32).reshape(1)

    my_dev = 2 * my_chip[0] + my_core[0]
    dm = x.shape[-1]
    ada_cols = ada_w.shape[-1]

    c_all = _gather_all("cond_gather", c.reshape(1, dm)).reshape(N_DEV, dm)
    cond16 = jnp.concatenate([c_all, c_ctx.reshape(1, dm), jnp.zeros((7, dm), F32)], axis=0)
    (sc16,) = _row_call("cond_silu", lambda i, r, f: ([jax.nn.silu(r[0])], []), 1, 16, [(cond16, 0, dm, 0)], [],
                        [(16, dm, F32, 0)], [])
    mod_here = _mm_nn("mod_mm", sc16, mat(ada_w))
    mod_all = _gather_all("mod_gather", mod_here)
    mod_rows = jnp.concatenate([mod_all[2 * j] for j in range(N_SHARD)], axis=1)
    mine = lax.dynamic_slice_in_dim(mod_rows, my_dev, 1, axis=0)
    mod = _add_small("mod_bias", [jnp.concatenate([mine, mod_rows[N_DEV:N_DEV + 1], jnp.zeros((6, 3 * dm), F32)], axis=0),
                                  jnp.broadcast_to(ada_b, (8, 3 * dm))], (8, 3 * dm))

    small_shapes = [w[n].shape for n in _SMALL_SHARDED]
    big_bf = [_cast_into_slot(f"to_bf16_{n}", mat(w[n]), my_chip) for n in _GATHERED]
    small_mine = _pack_small(w)
    small_slots = lax.dynamic_update_slice(jnp.zeros((N_SHARD,) + small_mine.shape, F32), small_mine[None], (my_chip[0], 0, 0))
    gathered = _weights_gather("weights_gather", big_bf, [small_slots])
    w_in_st, w_hg_st, w_rw_st, w_out_st, small_st = gathered
    full_small = {}
    per_chip = [_unpack(small_st[j], small_shapes) for j in range(N_SHARD)]
    for i, n in enumerate(_SMALL_SHARDED):
        full_small[n] = _join_shards(jnp.stack([per_chip[j][i] for j in range(N_SHARD)], axis=0))
    w_out_full = w_out_st.reshape(dm, dm)

    loss_b, grad_x, dmod, g = _local_step(
        x[0], ctx[0], mod, norm_g, w_in_st, full_small["hg_lb"], hg_norm_g, full_small["rw_mu"][0],
        full_small["rw_w0"][0], full_small["rw_w2"][0], full_small["rw_a0"][0], full_small["rw_a2"][0], rw_kk, rw_ka, rw_rk,
        rw_gn_g, rw_gn_b, w_hg_st, w_rw_st, w_out_full, final_g, loss_target[0], my_core)
    loss = lax.psum(loss_b[0, 0], ("x", "y", "c"))

    dmod_all = _gather_all("dmod_gather", dmod[0:2])
    dmod_here = lax.dynamic_slice_in_dim(dmod_all, my_chip[0] * ada_cols, ada_cols, axis=2)
    d_ctx_row = _add_small("d_mod_ctx", [dmod_here[j, 1:2] for j in range(N_DEV)], (1, ada_cols))
    dm16 = jnp.concatenate([dmod_here[:, 0], d_ctx_row, jnp.zeros((7, ada_cols), F32)], axis=0)
    g_ada_here = _mm_tn("d_ada_w", sc16, dm16)
    d_sc16 = _mm_nt("d_cond", dm16, mat(ada_w))

    def cond_bwd(i, r, f):
        _, vjp = jax.vjp(jax.nn.silu, r[0])
        return [vjp(r[1])[0]], []

    (d_cond16,) = _row_call("cond_bwd", cond_bwd, 1, 16, [(cond16, 0, dm, 0), (d_sc16, 0, dm, 0)], [], [(16, dm, F32, 0)], [])
    g["c_ctx"] = jnp.where(my_core[0] == 0, d_cond16[N_DEV], 0.0)
    g["ada_b"] = _add_small("g_ada_b", [dmod[0:1], dmod[1:2]], (1, 3 * dm))

    def finish(name, chip_sum, landed):
        return _sum_and_swap(f"grads_sum_swap_{name}", landed, chip_sum, my_chip)

    res = {}
    rep_shapes = [w[n].shape for n in _REPLICATED]
    rep_all = _gather_all("grads_replicated", _pack([g[n].reshape(w[n].shape) for n in _REPLICATED]))
    outs = _adamw("adamw_ada_w", mat(ada_w), mat(m["ada_w"]), mat(v["ada_w"]), g_ada_here[None])
    res["ada_w"] = [o.reshape(ada_w.shape) for o in outs]
    for n in _GATHERED:
        outs = _adamw_halves(f"adamw_{n}", mat(w[n]), mat(m[n]), mat(v[n]), *finish(n, *g[n]), my_core)
        res[n] = [o.reshape(w[n].shape) for o in outs]
    outs = _adamw_halves("adamw_small", small_mine, _pack_small(m), _pack_small(v), *finish("small", *g["small"]), my_core)
    for i, vals in enumerate(zip(*[_unpack(o, small_shapes) for o in outs])):
        res[_SMALL_SHARDED[i]] = list(vals)
    outs = _adamw("adamw_replicated", _pack([w[n] for n in _REPLICATED]), _pack([m[n] for n in _REPLICATED]),
                  _pack([v[n] for n in _REPLICATED]), rep_all)
    for i, vals in enumerate(zip(*[_unpack(o, rep_shapes) for o in outs])):
        res[_REPLICATED[i]] = list(vals)

    return (loss, grad_x[None], *[res[n][0] for n in _WEIGHTS], *[res[n][1] for n in _WEIGHTS],
            *[res[n][2] for n in _WEIGHTS], *[res[n][3] for n in _WEIGHTS])
```

```python
import functools

import jax
import jax.numpy as jnp
from jax import lax
from jax.experimental import pallas as pl
from jax.experimental.pallas import tpu as pltpu

HI = lax.Precision.HIGHEST
F32 = jnp.float32
BF16 = jnp.bfloat16

NORM_EPS = 1e-6
HG_HEAD = 128
RW_HEAD = 64
RW_LORA = 64
RW_GN_EPS = 64e-5
GRID_W = 64
SUB = 16
RW_SUB = 16
STEP = 64
RW_STEP = 64
N_SHARD = 4
N_DEV = 8
LANE = 128

ADAM_LR = 0.001
ADAM_B1 = 0.9
ADAM_B2 = 0.999
ADAM_EPS = 1e-08
ADAM_WD = 0.01
ADAM_STEP = 10

VMEM_LIMIT = 56 * 1024 * 1024


def _params(sem=None):
    return pltpu.CompilerParams(dimension_semantics=sem, vmem_limit_bytes=VMEM_LIMIT)


def _tile(n, cands):
    for c in cands:
        if n % c == 0:
            return c
    return n


def _iota2(n, m, d):
    return lax.broadcasted_iota(jnp.int32, (n, m), d)


def _before(n, rev, strict):
    t, s = _iota2(n, n, 0), _iota2(n, n, 1)
    if rev:
        return (s > t) if strict else (s >= t)
    return (s < t) if strict else (s <= t)


def _running_sum(a, axis, rev):
    n = a.shape[axis]
    shift = 1
    while shift < n:
        pad = list(a.shape)
        pad[axis] = shift
        zeros = jnp.zeros(pad, a.dtype)
        if rev:
            moved = jnp.concatenate([lax.slice_in_dim(a, shift, n, axis=axis), zeros], axis=axis)
        else:
            moved = jnp.concatenate([zeros, lax.slice_in_dim(a, 0, n - shift, axis=axis)], axis=axis)
        a = a + moved
        shift *= 2
    return a


def _sdot(a, b, spec):
    return jnp.einsum(spec, a, b, precision=lax.Precision.DEFAULT, preferred_element_type=F32)


def _hg_step(s0, qraw, iin, fin, lb2, rev):
    c, w = qraw.shape
    h = w // HG_HEAD
    nsub = c // SUB
    lb = jax.nn.sigmoid(lb2[0:1] - lb2[1:2])
    q = jax.nn.silu(qraw)
    fg = lb + (1.0 - lb) * jax.nn.sigmoid(fin)
    kk = 1.0 - fg
    g = jnp.log(fg)
    bcum = _running_sum(g, 0, rev)
    def heads(a):
        return jnp.swapaxes(a.reshape(a.shape[0], h, HG_HEAD), 0, 1)

    def unheads(a):
        return jnp.swapaxes(a, 0, 1).reshape(a.shape[1], w)

    blocks = [slice(j * SUB, (j + 1) * SUB) for j in range(nsub)]
    outs = []
    for sl in blocks:
        qs, ks, vs, bc = [a[sl].reshape(SUB, h, HG_HEAD) for a in (q, kk, iin, bcum)]
        o = jnp.zeros((SUB, h, HG_HEAD), F32)
        for si in range(SUB):
            after = slice(0, si + 1) if rev else slice(si, SUB)
            dec = jnp.exp(jnp.minimum(bc[after] - bc[si:si + 1], 0.0))
            a = jnp.sum(qs[after] * ks[si:si + 1] * dec, axis=-1, keepdims=True)
            term = a * vs[si:si + 1]
            n_rest = SUB - 1 - si if rev else si
            if n_rest:
                rest = jnp.zeros((n_rest, h, HG_HEAD), F32)
                term = jnp.concatenate([term, rest] if rev else [rest, term], axis=0)
            o = o + term
        outs.append(o.reshape(SUB, w))
    order = list(range(nsub - 1, -1, -1)) if rev else list(range(nsub))
    for pos in range(1, nsub):
        j, before = order[pos], order[:pos]
        first = (j + 1) * SUB - 1 if rev else j * SUB
        bstart = bcum[first:first + 1] - g[first:first + 1]
        qp = heads(q[blocks[j]] * jnp.exp(bcum[blocks[j]] - bstart))
        kp = heads(jnp.concatenate([kk[blocks[p]] * jnp.exp(bstart - bcum[blocks[p]]) for p in before], axis=0))
        vp = heads(jnp.concatenate([iin[blocks[p]] for p in before], axis=0))
        outs[j] = outs[j] + unheads(_sdot(_sdot(qp, kp, 'htk,hsk->hts'), vp, 'hts,hsv->htv'))
    o_state = unheads(_sdot(heads(q * jnp.exp(bcum)), s0, 'htk,hvk->htv'))
    last = 0 if rev else c - 1
    blast = bcum[last:last + 1]
    s_new = heads(jnp.exp(blast)) * s0 + _sdot(heads(iin), heads(kk * jnp.exp(blast - bcum)), 'hsv,hsk->hvk')
    return jnp.concatenate(outs, axis=0) + o_state, s_new


def _tri_solve(lmat, rhs, rev):
    hh, c, _ = lmat.shape
    sub = RW_SUB
    nb = c // sub
    diag = jnp.concatenate([lmat[:, i * sub:(i + 1) * sub, i * sub:(i + 1) * sub] for i in range(nb)], axis=0)
    dt = jnp.transpose(diag, (1, 2, 0))
    col = lax.broadcasted_iota(jnp.int32, (sub, 1), 0)
    inv_rows = [None] * sub
    order = list(range(sub - 1, -1, -1)) if rev else list(range(sub))
    for pos, t in enumerate(order):
        row = jnp.broadcast_to((col == t).astype(F32), (sub, dt.shape[2]))
        for s in order[:pos]:
            row = row - dt[t, s:s + 1, :] * inv_rows[s]
        inv_rows[t] = row
    tinv = jnp.transpose(jnp.concatenate([r[None] for r in inv_rows], axis=0), (2, 0, 1))
    p = [None] * nb
    done = []
    for i in (range(nb - 1, -1, -1) if rev else range(nb)):
        r = rhs[:, i * sub:(i + 1) * sub]
        if done:
            lrow = jnp.concatenate([lmat[:, i * sub:(i + 1) * sub, m * sub:(m + 1) * sub] for m in done], axis=2)
            r = r - _sdot(lrow, jnp.concatenate([p[m] for m in done], axis=1), 'hts,hsv->htv')
        p[i] = _sdot(tinv[i * hh:(i + 1) * hh], r, 'hts,hsv->htv')
        done.append(i)
    return jnp.concatenate(p, axis=1)


def _rw_step(s0, r, k, v, wlo, alo, w0h, w2h, a0h, a2h, kkh, kah, rev):
    hh, c, _ = r.shape
    tl = jnp.broadcast_to(jnp.tanh(wlo)[None], (hh, c, wlo.shape[1]))
    al = jnp.broadcast_to(alo[None], (hh, c, alo.shape[1]))
    wlog = -jax.nn.softplus(-(w0h + _sdot(tl, w2h, 'hcl,hlj->hcj'))) - 0.5
    lw = -jnp.exp(wlog)
    a = jax.nn.sigmoid(a0h + _sdot(al, a2h, 'hcl,hlj->hcj'))
    kk = k * kkh
    kk = kk * lax.rsqrt(jnp.sum(kk * kk, axis=-1, keepdims=True) + 1e-12)
    kd = k * (1.0 + (a - 1.0) * kah)
    b = kk * a
    cum = _running_sum(lw, 1, rev)
    ecum, encum = jnp.exp(cum), jnp.exp(-cum)
    alpha = jnp.exp(cum - lw) * kk
    beta = b * encum
    kappa = kd * encum
    rho = r * ecum
    m_lt = _before(c, rev, True)[None]
    m_le = _before(c, rev, False)[None]
    ar = jnp.concatenate([alpha, rho], axis=1)
    kb = jnp.concatenate([kappa, beta], axis=1)
    gram = _sdot(ar, kb, 'htk,hsk->hts')
    a_kap = jnp.where(m_lt, gram[:, :c, :c], 0.0)
    a_bet = jnp.where(m_lt, gram[:, :c, c:], 0.0)
    b_kap = jnp.where(m_le, gram[:, c:, :c], 0.0)
    b_bet = jnp.where(m_le, gram[:, c:, c:], 0.0)
    from_state = _sdot(ar, s0, 'htk,hvk->htv')
    p = _tri_solve(a_bet, from_state[:, :c] + _sdot(a_kap, v, 'hts,hsv->htv'), rev)
    vp = jnp.concatenate([v, -p], axis=1)
    y = from_state[:, c:] + _sdot(jnp.concatenate([b_kap, b_bet], axis=2), vp, 'hts,hsv->htv')
    stil = s0 + _sdot(vp, kb, 'hsv,hsk->hvk')
    last = 0 if rev else c - 1
    return y, stil * ecum[:, last:last + 1, :]


def _fn_h(s, norm_g, scale, shift):
    return s * lax.rsqrt(jnp.mean(s * s, axis=-1, keepdims=True) + NORM_EPS) * norm_g * (1.0 + scale) + shift


def _fn_hgpost(of, ob, z, g):
    tm, w = of.shape
    o = (of + ob).reshape(tm, w // HG_HEAD, HG_HEAD)
    o = o * lax.rsqrt(jnp.mean(o * o, axis=-1, keepdims=True) + NORM_EPS)
    return o.reshape(tm, w) * g * jax.nn.silu(z)


def _fn_rwpost(y0, y1, r, k, v, alo, z, a0, a2, k_a, r_k, gn_g, gn_b):
    tm, w = r.shape
    nh = w // RW_HEAD
    asum = 0.0
    for d in range(2):
        asum = asum + jax.nn.sigmoid(a0[d:d + 1] + jnp.dot(alo[:, d * RW_LORA:(d + 1) * RW_LORA], a2[d],
                                                           precision=HI, preferred_element_type=F32))
    k_sum = k * (2.0 + (asum - 2.0) * k_a)
    ys = (y0 + y1).reshape(tm, nh, RW_HEAD)
    mean = jnp.mean(ys, axis=-1, keepdims=True)
    var = jnp.mean(jnp.square(ys - mean), axis=-1, keepdims=True)
    y = ((ys - mean) * lax.rsqrt(var + RW_GN_EPS)).reshape(tm, w) * gn_g + gn_b
    bonus = jnp.sum((r * k_sum * r_k).reshape(tm, nh, RW_HEAD), axis=-1, keepdims=True) * v.reshape(tm, nh, RW_HEAD)
    return (y + bonus.reshape(tm, w)) * jax.nn.silu(z)


def _fn_merge(a, b, ghg, grw):
    return jax.nn.sigmoid(ghg) * a + jax.nn.sigmoid(grw) * b


def _fn_final(xs, o, gate, final_g, tgt):
    x2 = xs + gate * o
    y = x2 * lax.rsqrt(jnp.mean(x2 * x2, axis=-1, keepdims=True) + NORM_EPS) * final_g
    return 0.5 * jnp.sum(jnp.mean(jnp.square(y - tgt), axis=-1))


def _row_call(name, fn, n_tiles, tm, row_ins, full_ins, row_outs, acc_outs):
    n_ri, n_fi, n_ro = len(row_ins), len(full_ins), len(row_outs)

    def body(*refs):
        i = pl.program_id(0)
        rvals = [r[...] for r in refs[:n_ri]]
        fvals = [r[...] for r in refs[n_ri:n_ri + n_fi]]
        outs = refs[n_ri + n_fi:]
        ro, ao = fn(i, rvals, fvals)
        for ref, val in zip(outs[:n_ro], ro):
            ref[...] = val.astype(ref.dtype)
        for ref, val in zip(outs[n_ro:], ao):
            @pl.when(i == 0)
            def _(ref=ref):
                ref[...] = jnp.zeros_like(ref)
            ref[...] += val.astype(ref.dtype)

    def rspec(width, cb, off, rows):
        return pl.BlockSpec((tm, width), lambda i: (jnp.clip(i - off, 0, rows // tm - 1), cb))

    def fspec(shape):
        nd = len(shape)
        return pl.BlockSpec(shape, lambda i: (0,) * nd)

    in_specs = [rspec(w, cb, off, a.shape[0]) for (a, cb, w, off) in row_ins] + [fspec(a.shape) for a in full_ins]
    out_specs = [rspec(w, 0, off, rows) for (rows, w, _, off) in row_outs] + [fspec(s) for (s, _) in acc_outs]
    out_shape = [jax.ShapeDtypeStruct((rows, w), dt) for (rows, w, dt, _) in row_outs] + \
                [jax.ShapeDtypeStruct(s, dt) for (s, dt) in acc_outs]
    res = pl.pallas_call(
        body, name=name, grid=(n_tiles,), in_specs=in_specs, out_specs=out_specs, out_shape=out_shape,
        compiler_params=_params(("arbitrary",)),
    )(*[a for (a, _, _, _) in row_ins], *full_ins)
    return list(res)


def _mm(name, a, b, m, n, k_steps, tm, tn, a_block, a_map, b_block, b_map, o_shape, o_block, o_map,
        contract, out_dtype=F32, scatter=()):
    ns = len(scatter)
    grid = (m // tm, n // tn, k_steps)

    def body(*refs):
        a_ref, b_ref, o_ref, acc_ref = refs[0], refs[1], refs[2 + ns], refs[3 + 2 * ns]
        kk = pl.program_id(2)
        if ns:
            sc_refs = (refs[2:2 + ns], refs[3 + ns:3 + 2 * ns]) + tuple(refs[4 + 2 * ns:])
            at = (pl.program_id(0) * grid[1] + pl.program_id(1)) * grid[2] + kk
            pl.when(at == 0)(lambda: _scatter_start(scatter, *sc_refs))

        @pl.when(kk == 0)
        def _():
            acc_ref[...] = jnp.zeros_like(acc_ref)

        acc_ref[...] += lax.dot_general(a_ref[...].astype(BF16), b_ref[...].astype(BF16),
                                        (contract, ((), ())), preferred_element_type=F32)

        @pl.when(kk == k_steps - 1)
        def _():
            o_ref[...] = acc_ref[...].astype(o_ref.dtype)

        if ns:
            pl.when(at == grid[0] * grid[1] * grid[2] - 1)(lambda: _scatter_wait(scatter, *sc_refs))

    hbm = pl.BlockSpec(memory_space=pl.ANY)
    sems = [pltpu.SemaphoreType.DMA((ns, _PEER_CHIPS))] * 2 if ns else []
    res = pl.pallas_call(
        body, name=name, grid=grid,
        in_specs=[pl.BlockSpec(a_block, a_map), pl.BlockSpec(b_block, b_map)] + [hbm] * ns,
        out_specs=[pl.BlockSpec(o_block, o_map)] + [hbm] * ns,
        out_shape=[jax.ShapeDtypeStruct(o_shape, out_dtype)] + [jax.ShapeDtypeStruct(s.shape, s.dtype) for s in scatter],
        scratch_shapes=[pltpu.VMEM((tm, tn), F32)] + sems,
        compiler_params=_params(("arbitrary",) * 3 if ns else ("parallel", "parallel", "arbitrary")),
    )(a, b, *scatter)
    return (res[0], list(res[1:])) if ns else res[0]


_TM = (768, 512, 256, 128, 64, 32, 16, 8)
_TN = (512, 256, 128)
_TK = (1024, 768, 512, 256, 128)
_TK_WIDE = (768, 512, 256, 128)
WIDE_OUT_BYTES = 32 << 20


def _tm_wide(m, ns):
    for tm in _TM:
        if m % tm == 0 and 3 * 4 * tm * ns <= WIDE_OUT_BYTES:
            return tm
    return m


def _mm_nn(name, a, b, out_dtype=F32):
    m, k = a.shape
    n = b.shape[1]
    tm, tn, tk = _tile(m, _TM), _tile(n, _TN), _tile(k, _TK)
    return _mm(name, a, b, m, n, k // tk, tm, tn, (tm, tk), lambda i, j, s: (i, s), (tk, tn), lambda i, j, s: (s, j),
               (m, n), (tm, tn), lambda i, j, s: (i, j), ((1,), (0,)), out_dtype)


def _mm_nt(name, a, b, out_dtype=F32):
    m, k = a.shape
    n = b.shape[0]
    tm, tn, tk = _tile(m, _TM), _tile(n, _TN), _tile(k, _TK)
    return _mm(name, a, b, m, n, k // tk, tm, tn, (tm, tk), lambda i, j, s: (i, s), (tn, tk), lambda i, j, s: (j, s),
               (m, n), (tm, tn), lambda i, j, s: (i, j), ((1,), (1,)), out_dtype)


def _mm_tn(name, a, b, out_dtype=F32):
    k, m = a.shape
    n = b.shape[1]
    tm, tn, tk = _tile(m, _TM), _tile(n, _TN), _tile(k, _TK)
    return _mm(name, a, b, m, n, k // tk, tm, tn, (tk, tm), lambda i, j, s: (s, i), (tk, tn), lambda i, j, s: (s, j),
               (m, n), (tm, tn), lambda i, j, s: (i, j), ((0,), (0,)), out_dtype)


def _mm_n_st(name, a, bst, out_dtype=F32, joined=False):
    m, k = a.shape
    ns_, _, ns = bst.shape
    tm, tk = _tm_wide(m, ns), _tile(k, (512, 256, 128))
    out = ((m, ns_ * ns), (tm, ns), lambda i, j, s: (i, j)) if joined else \
          ((ns_, m, ns), (None, tm, ns), lambda i, j, s: (j, i, 0))
    return _mm(name, a, bst, m, ns_ * ns, k // tk, tm, ns,
               (tm, tk), lambda i, j, s: (i, s), (None, tk, ns), lambda i, j, s: (j, s, 0), *out, ((1,), (0,)), out_dtype)


def _mm_st_t(name, ast, bst, out_dtype=F32, scatter=()):
    ns_, n, ns = bst.shape
    m = ast.shape[-2]
    tm, tn = _tile(m, _TM), _tile(n, _TN)
    a_side = ((None, tm, ns), lambda i, j, s: (s, i, 0)) if ast.ndim == 3 else ((tm, ns), lambda i, j, s: (i, s))
    return _mm(name, ast, bst, m, n, ns_, tm, tn, *a_side, (None, tn, ns), lambda i, j, s: (s, j, 0),
               (m, n), (tm, tn), lambda i, j, s: (i, j), ((1,), (1,)), out_dtype, scatter)


def _mm_t_st(name, a, bst, out_dtype=F32, scatter=(), n_shard=N_SHARD):
    k, m = a.shape
    ns = bst.shape[-1] if bst.ndim == 3 else bst.shape[-1] // n_shard
    tm, tk = _tile(m, _TN), _tile(k, _TK_WIDE)
    b_side = ((None, tk, ns), lambda i, j, s: (j, s, 0)) if bst.ndim == 3 else ((tk, ns), lambda i, j, s: (s, j))
    return _mm(name, a, bst, m, n_shard * ns, k // tk, tm, ns, (tk, tm), lambda i, j, s: (s, i), *b_side,
               (n_shard, m, ns), (None, tm, ns), lambda i, j, s: (j, i, 0), ((0,), (0,)), out_dtype, scatter)


RELAYOUT_TILE_BYTES = 12 << 20


def _join_columns(name, st):
    ns_, t, ns = st.shape
    tm = _row_tile_for(t, ns_ * ns, budget=RELAYOUT_TILE_BYTES)

    def body(s_ref, o_ref):
        o_ref[...] = jnp.concatenate([s_ref[j] for j in range(ns_)], axis=1)

    return pl.pallas_call(
        body, name=name, grid=(t // tm,), in_specs=[pl.BlockSpec((ns_, tm, ns), lambda i: (0, i, 0))],
        out_specs=pl.BlockSpec((tm, ns_ * ns), lambda i: (i, 0)),
        out_shape=jax.ShapeDtypeStruct((t, ns_ * ns), st.dtype), compiler_params=_params(("parallel",)),
    )(st)


def _split_columns(name, pieces, n_shard):
    t = pieces[0].shape[0]
    n = sum(p.shape[1] for p in pieces)
    ns = n // n_shard
    tm = _row_tile_for(t, n, budget=RELAYOUT_TILE_BYTES)
    npc = len(pieces)

    def body(*refs):
        full = jnp.concatenate([r[...] for r in refs[:npc]], axis=1)
        for j in range(n_shard):
            refs[npc][j] = full[:, j * ns:(j + 1) * ns]

    return pl.pallas_call(
        body, name=name, grid=(t // tm,),
        in_specs=[pl.BlockSpec((tm, p.shape[1]), lambda i: (i, 0)) for p in pieces],
        out_specs=pl.BlockSpec((n_shard, tm, ns), lambda i: (0, i, 0)),
        out_shape=jax.ShapeDtypeStruct((n_shard, t, ns), pieces[0].dtype), compiler_params=_params(("parallel",)),
    )(*pieces)


def _scan_order(j, n_ctx, n_all, rev):
    if not rev:
        return j
    return jnp.where(j < n_ctx, n_ctx - 1 - j, n_all - 1 - (j - n_ctx))


def _hg_scan_fwd(name, p_hg, lb2, d, n_ctx):
    t, w = p_hg.shape[0], lb2.shape[1]
    h = w // HG_HEAD
    n = t // STEP
    rev = d == 1

    def body(q_ref, i_ref, f_ref, lb_ref, o_ref, st_ref, s_ref):
        j = pl.program_id(0)

        @pl.when(j == 0)
        def _():
            s_ref[...] = jnp.zeros_like(s_ref)

        s0 = s_ref[...]
        st_ref[...] = s0
        o, s1 = _hg_step(s0, q_ref[...], i_ref[...], f_ref[...], lb_ref[...], rev)
        o_ref[...] = o
        s_ref[...] = s1

    def rows(cb):
        return pl.BlockSpec((STEP, w), lambda j: (_scan_order(j, n_ctx, n, rev), cb))

    return pl.pallas_call(
        body, name=name, grid=(n,),
        in_specs=[rows(0), rows(1), rows(2 + d), pl.BlockSpec((2, w), lambda j: (0, 0))],
        out_specs=[rows(0), pl.BlockSpec((None, h, HG_HEAD, HG_HEAD), lambda j: (j, 0, 0, 0))],
        out_shape=[jax.ShapeDtypeStruct((t, w), F32), jax.ShapeDtypeStruct((n, h, HG_HEAD, HG_HEAD), F32)],
        scratch_shapes=[pltpu.VMEM((h, HG_HEAD, HG_HEAD), F32)],
        compiler_params=_params(("arbitrary",)),
    )(p_hg, p_hg, p_hg, lb2)


def _hg_scan_bwd(name, p_hg, lb2, states, do, d, n_ctx, other=()):
    t, w = p_hg.shape[0], lb2.shape[1]
    h = w // HG_HEAD
    n = t // STEP
    rev = d == 1
    no = len(other)

    def body(q_ref, i_ref, f_ref, lb_ref, st_ref, do_ref, *refs):
        dq_ref, di_ref, df_ref, dlb_ref, ds_ref = refs[no:]
        step = pl.program_id(0)

        @pl.when(step == 0)
        def _():
            ds_ref[...] = jnp.zeros_like(ds_ref)
            dlb_ref[...] = jnp.zeros_like(dlb_ref)

        _, vjp = jax.vjp(lambda s0, q, i, f, lb: _hg_step(s0, q, i, f, lb, rev),
                         st_ref[...], q_ref[...], i_ref[...], f_ref[...], lb_ref[...])
        ds0, dq, di, df, dlb = vjp((do_ref[...], ds_ref[...]))
        if no:
            dq, di = refs[0][...] + dq, refs[1][...] + di
        dq_ref[...] = dq.astype(dq_ref.dtype)
        di_ref[...] = di.astype(di_ref.dtype)
        df_ref[...] = df.astype(df_ref.dtype)
        dlb_ref[...] += dlb
        ds_ref[...] = ds0

    def rows(cb):
        return pl.BlockSpec((STEP, w), lambda s: (_scan_order(n - 1 - s, n_ctx, n, rev), cb))

    qi = BF16 if no else F32
    return pl.pallas_call(
        body, name=name, grid=(n,),
        in_specs=[rows(0), rows(1), rows(2 + d), pl.BlockSpec((2, w), lambda s: (0, 0)),
                  pl.BlockSpec((None, h, HG_HEAD, HG_HEAD), lambda s: (n - 1 - s, 0, 0, 0)), rows(0)] + [rows(0)] * no,
        out_specs=[rows(0), rows(0), rows(0), pl.BlockSpec((2, w), lambda s: (0, 0))],
        out_shape=[jax.ShapeDtypeStruct((t, w), qi)] * 2 + [jax.ShapeDtypeStruct((t, w), BF16),
                                                            jax.ShapeDtypeStruct((2, w), F32)],
        scratch_shapes=[pltpu.VMEM((h, HG_HEAD, HG_HEAD), F32)],
        compiler_params=_params(("arbitrary",)),
    )(p_hg, p_hg, p_hg, lb2, states, do, *other)


def _to_heads(a, nh):
    return jnp.stack([a[:, i * RW_HEAD:(i + 1) * RW_HEAD] for i in range(nh)], axis=0)


def _from_heads(a):
    return jnp.concatenate([a[i] for i in range(a.shape[0])], axis=-1)


def _rw_scan_fwd(name, sh, hp, d, n_ctx):
    t = sh.shape[0]
    w = (sh.shape[1] - 4 * RW_LORA) // 3
    nh = w // RW_HEAD
    n = t // RW_STEP
    rev = d == 1
    lo = 3 * w // LANE

    def body(r_ref, k_ref, v_ref, wl_ref, al_ref, w0_ref, w2_ref, a0_ref, a2_ref, kk_ref, ka_ref,
             y_ref, st_ref, s_ref):
        j = pl.program_id(0)

        @pl.when(j == 0)
        def _():
            s_ref[...] = jnp.zeros_like(s_ref)

        s0 = s_ref[...]
        st_ref[...] = s0
        wl = wl_ref[...][:, d * RW_LORA:(d + 1) * RW_LORA]
        al = al_ref[...][:, d * RW_LORA:(d + 1) * RW_LORA]
        y, s1 = _rw_step(s0, _to_heads(r_ref[...], nh), _to_heads(k_ref[...], nh), _to_heads(v_ref[...], nh), wl, al,
                         w0_ref[...], w2_ref[...], a0_ref[...], a2_ref[...], kk_ref[...], ka_ref[...], rev)
        y_ref[...] = _from_heads(y)
        s_ref[...] = s1

    def rows(cb, width=w):
        return pl.BlockSpec((RW_STEP, width), lambda j: (_scan_order(j, n_ctx, n, rev), cb))

    def whole(a):
        nd = a.ndim
        return pl.BlockSpec(a.shape, lambda j: (0,) * nd)

    return pl.pallas_call(
        body, name=name, grid=(n,),
        in_specs=[rows(0), rows(1), rows(2), rows(lo, LANE), rows(lo + 1, LANE)] + [whole(a) for a in hp],
        out_specs=[rows(0), pl.BlockSpec((None, nh, RW_HEAD, RW_HEAD), lambda j: (j, 0, 0, 0))],
        out_shape=[jax.ShapeDtypeStruct((t, w), F32), jax.ShapeDtypeStruct((n, nh, RW_HEAD, RW_HEAD), F32)],
        scratch_shapes=[pltpu.VMEM((nh, RW_HEAD, RW_HEAD), F32)],
        compiler_params=_params(("arbitrary",)),
    )(sh, sh, sh, sh, sh, *hp)


def _rw_scan_bwd_both(name, sh, hps, states, dy, n_ctx):
    t = sh.shape[0]
    w = (sh.shape[1] - 4 * RW_LORA) // 3
    nh = w // RW_HEAD
    n = t // RW_STEP
    lo = 3 * w // LANE
    n_in, n_p = 13, 6

    def body(*refs):
        step = pl.program_id(0)
        ins = [refs[d * n_in:(d + 1) * n_in] for d in range(2)]
        outs = [refs[2 * n_in + d * (1 + n_p):2 * n_in + (d + 1) * (1 + n_p)] for d in range(2)]
        ds_refs = refs[2 * n_in + 2 * (1 + n_p):]

        @pl.when(step == 0)
        def _():
            for d in range(2):
                ds_refs[d][...] = jnp.zeros_like(ds_refs[d])
                for ref in outs[d][1:]:
                    ref[...] = jnp.zeros_like(ref)

        for d in range(2):
            r_ref, k_ref, v_ref, wl_ref, al_ref = ins[d][:5]
            hp_refs, st_ref, dy_ref = ins[d][5:11], ins[d][11], ins[d][12]
            wl = wl_ref[...][:, d * RW_LORA:(d + 1) * RW_LORA]
            al = al_ref[...][:, d * RW_LORA:(d + 1) * RW_LORA]
            _, vjp = jax.vjp(functools.partial(_rw_step, rev=d == 1),
                             st_ref[...], _to_heads(r_ref[...], nh), _to_heads(k_ref[...], nh), _to_heads(v_ref[...], nh),
                             wl, al, *[p[...] for p in hp_refs])
            g = vjp((_to_heads(dy_ref[...], nh), ds_refs[d][...]))
            ds_refs[d][...] = g[0]
            zero = jnp.zeros_like(g[4])
            lora = [zero] * 4
            lora[d], lora[2 + d] = g[4], g[5]
            outs[d][0][...] = jnp.concatenate([_from_heads(g[1]), _from_heads(g[2]), _from_heads(g[3])] + lora, axis=-1)
            for ref, val in zip(outs[d][1:], g[6:]):
                ref[...] += val

    def rows(d, cb, width=w):
        return pl.BlockSpec((RW_STEP, width), lambda s: (_scan_order(n - 1 - s, n_ctx, n, d == 1), cb))

    def whole(a):
        nd = a.ndim
        return pl.BlockSpec(a.shape, lambda s: (0,) * nd)

    in_specs, operands, out_specs, out_shape = [], [], [], []
    for d in range(2):
        in_specs += [rows(d, 0), rows(d, 1), rows(d, 2), rows(d, lo, LANE), rows(d, lo + 1, LANE)]
        in_specs += [whole(a) for a in hps[d]]
        in_specs += [pl.BlockSpec((None, nh, RW_HEAD, RW_HEAD), lambda s: (n - 1 - s, 0, 0, 0)), rows(d, 0)]
        operands += [sh] * 5 + list(hps[d]) + [states[d], dy]
        out_specs += [rows(d, 0, sh.shape[1])] + [whole(a) for a in hps[d]]
        out_shape += [jax.ShapeDtypeStruct(sh.shape, F32)] + [jax.ShapeDtypeStruct(a.shape, F32) for a in hps[d]]
    res = pl.pallas_call(
        body, name=name, grid=(n,), in_specs=in_specs, out_specs=out_specs, out_shape=out_shape,
        scratch_shapes=[pltpu.VMEM((nh, RW_HEAD, RW_HEAD), F32)] * 2, compiler_params=_params(("arbitrary",)),
    )(*operands)
    return [res[0], res[1 + n_p]], [res[1:1 + n_p], res[2 + n_p:]]


def _shift_masks(t, n_ctx_rows):
    row = lax.broadcasted_iota(jnp.int32, (t, 1), 0)
    isx = row >= n_ctx_rows
    pos = jnp.where(isx, row - n_ctx_rows, row)
    col = jnp.where(isx, jnp.bitwise_and(pos, GRID_W - 1), pos)
    ncol = jnp.where(isx, GRID_W, n_ctx_rows)
    n_x = t - n_ctx_rows
    ml = col != 0
    mr = col != ncol - 1
    mu = isx & (pos >= GRID_W)
    md = isx & (pos < n_x - GRID_W)
    return ml, mr, mu, md, isx


def _shift_fwd(name, p, col0, mu, n_ctx_rows):
    t, c = p.shape[0], mu.shape[1]
    cw = LANE

    def body(p_ref, mu_ref, o_ref):
        x = p_ref[...]
        m = mu_ref[...]
        ml, mr, mup, mdn, isx = _shift_masks(t, n_ctx_rows)
        left = jnp.where(ml, pltpu.roll(x, 1, 0), 0.0)
        right = jnp.where(mr, pltpu.roll(x, t - 1, 0), 0.0)
        up = jnp.where(mup, pltpu.roll(x, GRID_W, 0), 0.0)
        down = jnp.where(mdn, pltpu.roll(x, t - GRID_W, 0), 0.0)
        out = x + m[0:1] * (left - x) + m[1:2] * (right - x)
        vert = m[2:3] * (up - x) + m[3:4] * (down - x)
        o_ref[...] = out + jnp.where(isx, vert, 0.0)

    return pl.pallas_call(
        body, name=name, grid=(c // cw,),
        in_specs=[pl.BlockSpec((t, cw), lambda j: (0, col0 + j)), pl.BlockSpec((4, cw), lambda j: (0, j))],
        out_specs=pl.BlockSpec((t, cw), lambda j: (0, j)),
        out_shape=jax.ShapeDtypeStruct((t, c), F32),
        compiler_params=_params(("parallel",)),
    )(p, mu)


def _shift_bwd(name, p, col0, mu, dparts, n_ctx_rows):
    t, c = p.shape[0], mu.shape[1]
    cw = LANE
    npart = len(dparts)

    def body(*refs):
        p_ref, mu_ref = refs[0], refs[1]
        dp_ref, dmu_ref = refs[2 + npart], refs[3 + npart]
        x = p_ref[...]
        m = mu_ref[...]
        g = refs[2][...]
        for r in refs[3:2 + npart]:
            g = g + r[...]
        ml, mr, mup, mdn, isx = _shift_masks(t, n_ctx_rows)
        left = jnp.where(ml, pltpu.roll(x, 1, 0), 0.0)
        right = jnp.where(mr, pltpu.roll(x, t - 1, 0), 0.0)
        up = jnp.where(mup, pltpu.roll(x, GRID_W, 0), 0.0)
        down = jnp.where(mdn, pltpu.roll(x, t - GRID_W, 0), 0.0)
        gx = jnp.where(isx, g, 0.0)
        dmu_ref[...] = jnp.concatenate([
            jnp.sum(g * (left - x), axis=0, keepdims=True), jnp.sum(g * (right - x), axis=0, keepdims=True),
            jnp.sum(gx * (up - x), axis=0, keepdims=True), jnp.sum(gx * (down - x), axis=0, keepdims=True)], axis=0)
        coef = 1.0 - m[0:1] - m[1:2] - jnp.where(isx, m[2:3] + m[3:4], 0.0)
        dp = coef * g
        dp = dp + m[0:1] * pltpu.roll(jnp.where(ml, g, 0.0), t - 1, 0)
        dp = dp + m[1:2] * pltpu.roll(jnp.where(mr, g, 0.0), 1, 0)
        dp = dp + m[2:3] * pltpu.roll(jnp.where(mup, g, 0.0), t - GRID_W, 0)
        dp = dp + m[3:4] * pltpu.roll(jnp.where(mdn, g, 0.0), GRID_W, 0)
        dp_ref[...] = dp.astype(dp_ref.dtype)

    col = pl.BlockSpec((t, cw), lambda j: (0, j))
    par = pl.BlockSpec((4, cw), lambda j: (0, j))
    return pl.pallas_call(
        body, name=name, grid=(c // cw,),
        in_specs=[pl.BlockSpec((t, cw), lambda j: (0, col0 + j)), par] + [col] * npart,
        out_specs=[col, par],
        out_shape=[jax.ShapeDtypeStruct((t, c), BF16), jax.ShapeDtypeStruct((4, c), F32)],
        compiler_params=_params(("parallel",)),
    )(p, mu, *dparts)


def _add_small(name, terms, shape):
    flat2 = [a.reshape(-1, a.shape[-1]) for a in terms]
    return _rowwise(name, lambda *v: _slot_sum(list(v)), flat2, F32).reshape(shape)


def _local_step(x, ctx, mod, norm_g, w_in_st, hg_lb, hg_norm_g, rw_mu, rw_w0, rw_w2, rw_a0, rw_a2,
                rw_kk, rw_ka, rw_rk, rw_gn_g, rw_gn_b, w_hg_st, w_rw_st, w_out, final_g, tgt, my_core):
    seq, dm = x.shape
    n_ctx_rows = ctx.shape[0]
    t = seq + n_ctx_rows
    hw = hg_norm_g.shape[-1]
    rw = rw_kk.shape[-1]
    nh_rw = rw // RW_HEAD
    n_ctx = n_ctx_rows // STEP
    tm = _tile(n_ctx_rows, (256, 128, 64))
    nt = t // tm
    nct = n_ctx_rows // tm
    n_sh_cols = 3 * rw + 4 * RW_LORA

    final_g2 = final_g.reshape(1, dm)
    add = _add_small
    mod3 = mod.reshape(8, 3, dm)

    def pick(i, m3):
        r = jnp.where(i < nct, m3[1], m3[0])
        return r[0:1], r[1:2]

    tokens = [(ctx, 0, dm, 0), (x, 0, dm, nct)]

    def h_fn(i, r, f):
        shift, scale = pick(i, f[1])
        return [_fn_h(jnp.where(i < nct, r[0], r[1]), f[0], scale, shift)], []

    (h,) = _row_call("h_fwd", h_fn, nt, tm, tokens, [norm_g, mod3], [(t, dm, BF16, 0)], [])
    proj = _join_columns("proj_join", _mm_n_st("proj_mm", h, w_in_st))
    p_hg = proj
    rs_tile0 = 5 * hw // LANE
    p_zr = proj[:, 5 * hw + n_sh_cols:5 * hw + n_sh_cols + rw]
    p_gt = proj[:, 5 * hw + n_sh_cols + rw:]

    o_hg, st_hg = [], []
    for d in range(2):
        o, st = _hg_scan_fwd(f"hg_scan_fwd{d}", p_hg, hg_lb[d], d, n_ctx)
        o_hg.append(o)
        st_hg.append(st)

    def hgpost_fn(i, r, f):
        return [_fn_hgpost(r[0], r[1], r[2], f[0])], []

    hg_in = [(o_hg[0], 0, hw, 0), (o_hg[1], 0, hw, 0), (p_hg, 4, hw, 0)]
    (y_hg,) = _row_call("hg_post", hgpost_fn, nt, tm, hg_in, [hg_norm_g], [(t, hw, BF16, 0)], [])

    sh = _shift_fwd("rw_shift", proj, rs_tile0, rw_mu, n_ctx_rows)
    hps = []
    for d in range(2):
        hps.append([rw_w0[d].reshape(nh_rw, 1, RW_HEAD), jnp.swapaxes(rw_w2[d].reshape(RW_LORA, nh_rw, RW_HEAD), 0, 1),
                    rw_a0[d].reshape(nh_rw, 1, RW_HEAD), jnp.swapaxes(rw_a2[d].reshape(RW_LORA, nh_rw, RW_HEAD), 0, 1),
                    rw_kk.reshape(nh_rw, 1, RW_HEAD), rw_ka.reshape(nh_rw, 1, RW_HEAD)])
    y_rw_d, st_rw = [], []
    for d in range(2):
        y, st = _rw_scan_fwd(f"rw_scan_fwd{d}", sh, hps[d], d, n_ctx_rows // RW_STEP)
        y_rw_d.append(y)
        st_rw.append(st)

    rw_full = [rw_a0, rw_a2, rw_ka, rw_rk, rw_gn_g, rw_gn_b]
    lo = 3 * rw // LANE
    rw_in = [(y_rw_d[0], 0, rw, 0), (y_rw_d[1], 0, rw, 0), (sh, 0, rw, 0), (sh, 1, rw, 0), (sh, 2, rw, 0),
             (sh, lo + 1, LANE, 0), (p_zr, 0, rw, 0)]

    def rwpost_fn(i, r, f):
        return [_fn_rwpost(*r, *f)], []

    (y_rw,) = _row_call("rw_post", rwpost_fn, nt, tm, rw_in, rw_full, [(t, rw, BF16, 0)], [])

    a_hg = _mm_n_st("hg_out_mm", y_hg, w_hg_st, joined=True)
    a_rw = _mm_n_st("rw_out_mm", y_rw, w_rw_st, joined=True)
    mg_in = [(a_hg, 0, dm, 0), (a_rw, 0, dm, 0), (p_gt, 0, dm, 0), (p_gt, 1, dm, 0)]
    (merged,) = _row_call("merge", lambda i, r, f: ([_fn_merge(*r)], []), nt, tm, mg_in, [], [(t, dm, BF16, 0)], [])
    o_out = _mm_nn("out_mm", merged, w_out)

    def final_fn(i, r, f):
        gate = f[0][0][2:3]
        loss, vjp = jax.vjp(_fn_final, r[0], r[1], gate, f[1], r[2])
        dx, do, dgate, dfg, _ = vjp(jnp.ones((), F32))
        live = i >= nct
        zero = lambda a: jnp.where(live, a, 0.0)
        dmod = jnp.concatenate([jnp.concatenate([jnp.zeros((1, 2 * dm), F32), zero(dgate)], axis=1),
                                jnp.zeros((7, 3 * dm), F32)], axis=0)
        return [zero(dx), zero(do)], [jnp.broadcast_to(zero(loss), (8, LANE)), dmod, zero(dfg)]

    fin_in = [(x, 0, dm, nct), (o_out, 0, dm, 0), (tgt, 0, dm, nct)]
    dx_res, d_o, loss_acc, dmod_gate, d_final_g = _row_call(
        "final", final_fn, nt, tm, fin_in, [mod3, final_g2], [(t, dm, F32, 0), (t, dm, BF16, 0)],
        [((8, LANE), F32), ((8, 3 * dm), F32), ((1, dm), F32)])

    g_w_out = _mm_tn("d_w_out", merged, d_o)
    d_merged = _mm_nt("d_merged", d_o, w_out)

    def merge_bwd(i, r, f):
        _, vjp = jax.vjp(_fn_merge, r[0], r[1], r[2], r[3])
        da, db, dgh, dgr = vjp(r[4])
        return [da, db, jnp.concatenate([dgh, dgr], axis=1)], []

    da_hg, da_rw, dp_gt = _row_call("merge_bwd", merge_bwd, nt, tm, mg_in + [(d_merged, 0, dm, 0)], [],
                                    [(t, dm, BF16, 0), (t, dm, BF16, 0), (t, 2 * dm, BF16, 0)], [])
    g_w_hg_st = _mm_t_st("d_w_hg", y_hg, da_hg)
    g_w_rw_st = _mm_t_st("d_w_rw", y_rw, da_rw)
    dy_hg = _mm_st_t("d_y_hg", da_hg, w_hg_st)
    dy_rw = _mm_st_t("d_y_rw", da_rw, w_rw_st)

    def hgpost_bwd(i, r, f):
        _, vjp = jax.vjp(_fn_hgpost, r[0], r[1], r[2], f[0])
        dof, _, dz, dg = vjp(r[3])
        return [dof, dz], [dg]

    do_hg, dz_hg, g_hg_norm = _row_call("hg_post_bwd", hgpost_bwd, nt, tm, hg_in + [(dy_hg, 0, hw, 0)], [hg_norm_g],
                                        [(t, hw, F32, 0), (t, hw, BF16, 0)], [((1, hw), F32)])
    dq0, di0, df0, dlb0 = _hg_scan_bwd("hg_scan_bwd0", p_hg, hg_lb[0], st_hg[0], do_hg, 0, n_ctx)
    dq, di, df1, dlb1 = _hg_scan_bwd("hg_scan_bwd1", p_hg, hg_lb[1], st_hg[1], do_hg, 1, n_ctx, other=(dq0, di0))
    g_hg_lb = jnp.stack([dlb0, dlb1], axis=0)

    def rwpost_bwd(i, r, f):
        _, vjp = jax.vjp(_fn_rwpost, *r[:7], *f)
        g = vjp(r[7])
        zl = jnp.zeros((g[5].shape[0], 2 * RW_LORA), F32)
        return [g[0], jnp.concatenate([g[2], g[3], g[4], zl, g[5]], axis=1), g[6]], list(g[7:])

    dy_sum, dsh_p, dz_rw, g_a0_p, g_a2_p, g_ka_p, g_rk, g_gn_g, g_gn_b = _row_call(
        "rw_post_bwd", rwpost_bwd, nt, tm, rw_in + [(dy_rw, 0, rw, 0)], rw_full,
        [(t, rw, F32, 0), (t, n_sh_cols, F32, 0), (t, rw, BF16, 0)], [(a.shape, F32) for a in rw_full])
    dsh_dirs, hp_grads = _rw_scan_bwd_both("rw_scan_bwd", sh, hps, st_rw, dy_sum, n_ctx_rows // RW_STEP)
    dp_rs, g_mu = _shift_bwd("rw_shift_bwd", proj, rs_tile0, rw_mu, [dsh_p] + dsh_dirs, n_ctx_rows)

    def flat(a):
        if a.shape[1] == 1:
            return a.reshape(rw)
        return jnp.swapaxes(a, 0, 1).reshape(RW_LORA, rw)

    g_w0 = jnp.stack([flat(hp_grads[d][0]) for d in range(2)], axis=0)
    g_w2 = jnp.stack([flat(hp_grads[d][1]) for d in range(2)], axis=0)
    g_a0 = add("g_a0", [jnp.stack([flat(hp_grads[d][2]) for d in range(2)], axis=0), g_a0_p], (2, rw))
    g_a2 = add("g_a2", [jnp.stack([flat(hp_grads[d][3]) for d in range(2)], axis=0), g_a2_p], (2, RW_LORA, rw))
    g_kk = add("g_kk", [flat(hp_grads[0][4]).reshape(1, rw), flat(hp_grads[1][4]).reshape(1, rw)], (1, rw))
    g_ka = add("g_ka", [flat(hp_grads[0][5]).reshape(1, rw), flat(hp_grads[1][5]).reshape(1, rw), g_ka_p], (1, rw))

    dproj_st = _split_columns("dproj_split", [dq, di, df0, df1, dz_hg, dp_rs, dz_rw, dp_gt], N_SHARD)
    g_small = {"hg_lb": g_hg_lb, "rw_mu": g_mu, "rw_w0": g_w0, "rw_w2": g_w2, "rw_a0": g_a0, "rw_a2": g_a2}
    split = {n: _split_shards(g_small[n]) for n in _SMALL_SHARDED}
    small_parts = jnp.stack([_pack_small({n: split[n][j] for n in _SMALL_SHARDED}) for j in range(N_SHARD)], axis=0)
    early = {"w_hg_out": g_w_hg_st, "w_rw_out": g_w_rw_st, "w_out": g_w_out.reshape(N_SHARD, dm // N_SHARD, dm),
             "small": small_parts}
    early_chip = [_pair_reduce(f"grads_pair_sum_{n}", a, my_core) for n, a in early.items()]
    g_w_in_st, early_landed = _mm_t_st("d_w_in", h, dproj_st, scatter=tuple(early_chip))
    w_in_chip = _pair_reduce("grads_pair_sum_w_in", g_w_in_st, my_core)
    dh, (w_in_landed,) = _mm_st_t("d_h", dproj_st, w_in_st, scatter=(w_in_chip,))

    def h_bwd(i, r, f):
        shift, scale = pick(i, f[1])
        is_ctx = i < nct
        _, vjp = jax.vjp(_fn_h, jnp.where(is_ctx, r[0], r[1]), f[0], scale, shift)
        ds, dg, dscale, dshift = vjp(r[2])
        row = jnp.concatenate([dshift, dscale, jnp.zeros((1, dm), F32)], axis=1)
        z = jnp.zeros_like(row)
        dmod = jnp.concatenate([jnp.where(is_ctx, z, row), jnp.where(is_ctx, row, z), jnp.zeros((6, 3 * dm), F32)], axis=0)
        return [ds + r[3]], [dg, dmod]

    grad_x, g_norm_g, dmod_h = _row_call(
        "h_bwd", h_bwd, nt, tm, tokens + [(dh, 0, dm, 0), (dx_res, 0, dm, 0)], [norm_g, mod3],
        [(seq, dm, F32, nct)], [((1, dm), F32), ((8, 3 * dm), F32)])
    dmod = add("d_mod", [dmod_h, dmod_gate], (8, 3 * dm))
    grads = dict(
        norm_g=g_norm_g, w_in=(w_in_chip, w_in_landed), hg_norm_g=g_hg_norm, rw_kk=g_kk, rw_ka=g_ka,
        rw_rk=g_rk, rw_gn_g=g_gn_g, rw_gn_b=g_gn_b, final_g=d_final_g.reshape(dm))
    grads.update(zip(early, zip(early_chip, early_landed)))
    return loss_acc[0:1, 0:1], grad_x, dmod, grads


def _my_place():
    return lax.axis_index("x"), lax.axis_index("y"), lax.axis_index("c")


MIN_CHUNK_BYTES = 1 << 18
ROW_ALIGN = 16


def _n_chunks(rows, row_bytes):
    for n in (8, 4, 2):
        if rows % (n * ROW_ALIGN) == 0 and rows // n * row_bytes >= MIN_CHUNK_BYTES:
            return n
    return 1


def _row_bytes(a, lead=1):
    n = a.dtype.itemsize
    for d in a.shape[lead:]:
        n *= d
    return n


def _rows(ref, start, size):
    return ref.at[pl.ds(start, size)]


def _chunked(make, start, size, n):
    cs = size // n
    return [make(start + j * cs, cs) for j in range(n)]


_PEER_CHIPS = 3


def _weights_gather(name, big, small):
    nb, na = len(big), len(big) + len(small)
    arrays = list(big) + list(small)
    n_ici = 6

    def body(*refs):
        outs = refs[na:2 * na]
        send_sems, recv_sems, fsend_sems, frecv_sems = refs[2 * na:]
        x, y, c = _my_place()
        me, sx, sy, sd = 2 * x + y, 2 * (1 - x) + y, 2 * x + (1 - y), 2 * (1 - x) + (1 - y)
        kx, ky, kd = (1 - x, y, c), (x, 1 - y, c), (1 - x, 1 - y, c)

        def ici(a, j, src_slot, dst_slot, to, r0, nr):
            return pltpu.make_async_remote_copy(
                src_ref=_rows(outs[a].at[src_slot], r0, nr), dst_ref=_rows(outs[a].at[dst_slot], r0, nr),
                send_sem=send_sems.at[a, j], recv_sem=recv_sems.at[a, j], device_id=to,
                device_id_type=pl.DeviceIdType.MESH)

        def to_sibling(a, k, slot, r0, nr):
            rows = _rows(outs[a].at[slot], r0, nr)
            return pltpu.make_async_remote_copy(
                src_ref=rows, dst_ref=rows, send_sem=fsend_sems.at[a, k], recv_sem=frecv_sems.at[a, k],
                device_id=(x, y, 1 - c), device_id_type=pl.DeviceIdType.MESH)

        def start(copies):
            for cp in copies:
                cp.start()

        geo = []
        for a in range(nb):
            half = arrays[a].shape[1] // 2
            geo.append((pl.multiple_of(c * half, ROW_ALIGN), pl.multiple_of((1 - c) * half, ROW_ALIGN), half // 2,
                        _n_chunks(half // 2, _row_bytes(arrays[a], 2))))
        plan = [(me, sx, kx, 0), (me, sx, kx, 1), (me, sy, ky, 0), (me, sy, ky, 1), (sx, sd, ky, 0), (sy, sd, kx, 1)]

        def piece(a, j):
            return geo[a][0] + plan[j][3] * geo[a][2]

        for a in range(nb):
            for j in range(4):
                start(_chunked(lambda r0, cs: ici(a, j, me, me, plan[j][2], r0, cs), piece(a, j), geo[a][2], geo[a][3]))
        for a in range(nb, na):
            rows = arrays[a].shape[1]
            for j, to in ((0, kx), (2, ky), (1, kd)):
                ici(a, j, me, me, to, 0, rows).start()
        for a in range(nb):
            for j, first in ((4, 0), (5, 3)):
                src_slot, _, to, _ = plan[j]
                ici(a, first, me, plan[first][1], plan[first][2], piece(a, first), geo[a][2]).wait_recv()
                start(_chunked(lambda r0, cs: ici(a, j, src_slot, src_slot, to, r0, cs), piece(a, j), geo[a][2], geo[a][3]))
        for a in range(nb):
            for j in (1, 2):
                ici(a, j, me, plan[j][1], plan[j][2], piece(a, j), geo[a][2]).wait_recv()
            for k, slot in ((0, sx), (1, sy)):
                start(_chunked(lambda r0, cs: to_sibling(a, k, slot, r0, cs), geo[a][0], 2 * geo[a][2], geo[a][3]))
        for a in range(nb):
            for j in (4, 5):
                ici(a, j, me, sd, plan[j][2], piece(a, j), geo[a][2]).wait_recv()
            start(_chunked(lambda r0, cs: to_sibling(a, 2, sd, r0, cs), geo[a][0], 2 * geo[a][2], geo[a][3]))
        for a in range(nb, na):
            rows = arrays[a].shape[1]
            for j, slot, to in ((0, sx, kx), (2, sy, ky), (1, sd, kd)):
                ici(a, j, me, slot, to, 0, rows).wait_recv()
        for a in range(nb):
            for k, slot in ((0, sx), (1, sy), (2, sd)):
                to_sibling(a, k, slot, geo[a][1], 2 * geo[a][2]).wait_recv()
        for a in range(nb):
            for j in range(n_ici):
                ici(a, j, me, me, plan[j][2], piece(a, j), geo[a][2]).wait_send()
            for k, slot in ((0, sx), (1, sy), (2, sd)):
                to_sibling(a, k, slot, geo[a][0], 2 * geo[a][2]).wait_send()
        for a in range(nb, na):
            rows = arrays[a].shape[1]
            for j, to in ((0, kx), (2, ky), (1, kd)):
                ici(a, j, me, me, to, 0, rows).wait_send()

    hbm = pl.BlockSpec(memory_space=pl.ANY)
    ici_sems = pltpu.SemaphoreType.DMA((na, n_ici))
    pair_sems = pltpu.SemaphoreType.DMA((na, _PEER_CHIPS))
    return pl.pallas_call(
        body, name=name, in_specs=[hbm] * na, out_specs=[hbm] * na,
        out_shape=[jax.ShapeDtypeStruct(a.shape, a.dtype) for a in arrays],
        input_output_aliases={a: a for a in range(na)}, scratch_shapes=[ici_sems, ici_sems, pair_sems, pair_sems],
    )(*arrays)


def _scatter_copy(arrays, ins, outs, send_sems, recv_sems, a, k, slot, r0, nr):
    x, y, c = _my_place()
    px, py = [(1 - x, y), (x, 1 - y), (1 - x, 1 - y)][k]
    return pltpu.make_async_remote_copy(
        src_ref=_rows(ins[a].at[2 * px + py], r0, nr), dst_ref=_rows(outs[a].at[slot], r0, nr),
        send_sem=send_sems.at[a, k], recv_sem=recv_sems.at[a, k], device_id=(px, py, c),
        device_id_type=pl.DeviceIdType.MESH)


def _scatter_start(arrays, ins, outs, send_sems, recv_sems):
    x, y, _ = _my_place()
    for a in range(len(arrays)):
        rows = arrays[a].shape[1]
        for k in range(_PEER_CHIPS):
            for cp in _chunked(lambda r0, cs: _scatter_copy(arrays, ins, outs, send_sems, recv_sems, a, k, 2 * x + y, r0, cs),
                               0, rows, _n_chunks(rows, _row_bytes(arrays[a], 2))):
                cp.start()


def _scatter_wait(arrays, ins, outs, send_sems, recv_sems):
    x, y, _ = _my_place()
    peer_slot = [2 * (1 - x) + y, 2 * x + (1 - y), 2 * (1 - x) + (1 - y)]
    for k in range(_PEER_CHIPS):
        for a in range(len(arrays)):
            _scatter_copy(arrays, ins, outs, send_sems, recv_sems, a, k, peer_slot[k], 0, arrays[a].shape[1]).wait_recv()
    for a in range(len(arrays)):
        for k in range(_PEER_CHIPS):
            _scatter_copy(arrays, ins, outs, send_sems, recv_sems, a, k, 2 * x + y, 0, arrays[a].shape[1]).wait_send()


PAIR_TILE_BYTES = 4 << 20


def _pair_exchange(name, a, place, reduce, out_dtype):
    rows, cols = a.shape[-2], a.shape[-1]
    half = rows // 2 if reduce else rows
    tr = _row_tile_for(half, cols, budget=PAIR_TILE_BYTES)
    nh = half // tr
    n_steps = (N_SHARD if reduce else 1) * nh

    def body(pc_ref, *refs):
        if reduce:
            keep_ref, send_ref, o_ref, land, send_sems, recv_sems, credit, wire = refs
            wire[...] = send_ref[...].astype(BF16)
            src = wire
        else:
            send_ref, o_ref, land, send_sems, recv_sems, credit = refs
            src = send_ref
        x, y, c = _my_place()
        other = (x, y, 1 - c)
        t = pl.program_id(0) * nh + pl.program_id(1) if reduce else pl.program_id(0)
        slot = t % 2

        @pl.when(t >= 2)
        def _():
            pl.semaphore_wait(credit, 1)

        copy = pltpu.make_async_remote_copy(
            src_ref=src, dst_ref=land.at[slot], send_sem=send_sems.at[slot], recv_sem=recv_sems.at[slot],
            device_id=other, device_id_type=pl.DeviceIdType.MESH)
        copy.start()
        copy.wait_recv()
        got = land[slot]
        o_ref[...] = ((keep_ref[...] + got.astype(F32)) if reduce else got).astype(out_dtype)
        copy.wait_send()

        @pl.when(t < n_steps - 2)
        def _():
            pl.semaphore_signal(credit, inc=1, device_id=other, device_id_type=pl.DeviceIdType.MESH)

    if reduce:
        grid = (N_SHARD, nh)
        in_specs = [pl.BlockSpec((None, tr, cols), lambda j, i, pc: (j, pc[0] * nh + i, 0)),
                    pl.BlockSpec((None, tr, cols), lambda j, i, pc: (j, (1 - pc[0]) * nh + i, 0))]
        out_spec = pl.BlockSpec((None, tr, cols), lambda j, i, pc: (j, i, 0))
        out_shape = jax.ShapeDtypeStruct((N_SHARD, half, cols), out_dtype)
        operands = (a, a)
        sem = ("arbitrary", "arbitrary")
    else:
        grid = (nh,)
        in_specs = [pl.BlockSpec((tr, cols), lambda i, pc: (i, 0))]
        out_spec = pl.BlockSpec((tr, cols), lambda i, pc: (i, 0))
        out_shape = jax.ShapeDtypeStruct((half, cols), out_dtype)
        operands = (a,)
        sem = ("arbitrary",)
    return pl.pallas_call(
        body, name=name,
        grid_spec=pltpu.PrefetchScalarGridSpec(
            num_scalar_prefetch=1, grid=grid, in_specs=in_specs, out_specs=out_spec,
            scratch_shapes=[pltpu.VMEM((2, tr, cols), BF16 if reduce else a.dtype), pltpu.SemaphoreType.DMA((2,)),
                            pltpu.SemaphoreType.DMA((2,)), pltpu.SemaphoreType.REGULAR] +
                           ([pltpu.VMEM((tr, cols), BF16)] if reduce else [])),
        out_shape=out_shape, compiler_params=_params(sem),
    )(place, *operands)


def _cast_into_slot(name, a, chip):
    rows, cols = a.shape
    tm = _row_tile_for(rows, cols)

    def body(pc_ref, a_ref, o_ref):
        o_ref[...] = a_ref[...].astype(BF16)

    return pl.pallas_call(
        body, name=name,
        grid_spec=pltpu.PrefetchScalarGridSpec(
            num_scalar_prefetch=1, grid=(rows // tm,), in_specs=[pl.BlockSpec((tm, cols), lambda i, pc: (i, 0))],
            out_specs=pl.BlockSpec((None, tm, cols), lambda i, pc: (pc[0], i, 0))),
        out_shape=jax.ShapeDtypeStruct((N_SHARD, rows, cols), BF16), compiler_params=_params(("parallel",)),
    )(chip, a)


def _pair_reduce(name, a, place):
    rows, cols = a.shape[-2], a.shape[-1]
    half = rows // 2
    tr = _row_tile_for(half, cols, budget=PAIR_TILE_BYTES)
    nh = half // tr
    n = N_SHARD * nh

    def body(pc_ref, keep_ref, send_ref, o_ref, wire, land, send_sems, recv_sems, credit):
        x, y, c = _my_place()
        other = (x, y, 1 - c)
        t = pl.program_id(0)

        def copy(slot):
            return pltpu.make_async_remote_copy(
                src_ref=wire.at[slot], dst_ref=land.at[slot], send_sem=send_sems.at[slot], recv_sem=recv_sems.at[slot],
                device_id=other, device_id_type=pl.DeviceIdType.MESH)

        @pl.when(t < n)
        def _():
            slot = t % 2
            wire[slot] = send_ref[...].astype(BF16)

            @pl.when(t >= 2)
            def _():
                pl.semaphore_wait(credit, 1)

            copy(slot).start()

        @pl.when(t >= 1)
        def _():
            slot = (t - 1) % 2
            copy(slot).wait_recv()
            o_ref[...] = (keep_ref[...] + land[slot].astype(F32)).astype(BF16)
            copy(slot).wait_send()

            @pl.when(t - 1 < n - 2)
            def _():
                pl.semaphore_signal(credit, inc=1, device_id=other, device_id_type=pl.DeviceIdType.MESH)

    def sent(t):
        return jnp.minimum(t, n - 1)

    def summed(t):
        return jnp.maximum(t - 1, 0)

    return pl.pallas_call(
        body, name=name,
        grid_spec=pltpu.PrefetchScalarGridSpec(
            num_scalar_prefetch=1, grid=(n + 1,),
            in_specs=[pl.BlockSpec((None, tr, cols), lambda t, pc: (summed(t) // nh, pc[0] * nh + summed(t) % nh, 0)),
                      pl.BlockSpec((None, tr, cols), lambda t, pc: (sent(t) // nh, (1 - pc[0]) * nh + sent(t) % nh, 0))],
            out_specs=pl.BlockSpec((None, tr, cols), lambda t, pc: (summed(t) // nh, summed(t) % nh, 0)),
            scratch_shapes=[pltpu.VMEM((2, tr, cols), BF16), pltpu.VMEM((2, tr, cols), BF16),
                            pltpu.SemaphoreType.DMA((2,)), pltpu.SemaphoreType.DMA((2,)), pltpu.SemaphoreType.REGULAR]),
        out_shape=jax.ShapeDtypeStruct((N_SHARD, half, cols), BF16), compiler_params=_params(("arbitrary",)),
    )(place, a, a)


SUM_SWAP_TILE_BYTES = 2 << 20


def _sum_and_swap(name, landed, sent, chip):
    ns, rows, cols = landed.shape
    tr = _row_tile_for(rows, cols, budget=SUM_SWAP_TILE_BYTES)
    n_steps = rows // tr

    def body(pc_ref, *refs):
        own_ref, mine_ref, theirs_ref, wire, land, send_sems, recv_sems, credit = refs[ns:]
        me = pc_ref[0]
        total = _slot_sum([jnp.where(me == j, own_ref[...], refs[j][...]).astype(F32) for j in range(ns)])
        mine_ref[...] = total
        wire[...] = total
        x, y, c = _my_place()
        other = (x, y, 1 - c)
        t = pl.program_id(0)
        slot = t % 2

        @pl.when(t >= 2)
        def _():
            pl.semaphore_wait(credit, 1)

        copy = pltpu.make_async_remote_copy(
            src_ref=wire, dst_ref=land.at[slot], send_sem=send_sems.at[slot], recv_sem=recv_sems.at[slot],
            device_id=other, device_id_type=pl.DeviceIdType.MESH)
        copy.start()
        copy.wait_recv()
        theirs_ref[...] = land[slot]
        copy.wait_send()

        @pl.when(t < n_steps - 2)
        def _():
            pl.semaphore_signal(credit, inc=1, device_id=other, device_id_type=pl.DeviceIdType.MESH)

    def landed_spec(j):
        return pl.BlockSpec((None, tr, cols), lambda i, pc: (jnp.where(pc[0] == j, (j + 1) % ns, j), i, 0))

    out = pl.BlockSpec((tr, cols), lambda i, pc: (i, 0))
    return pl.pallas_call(
        body, name=name,
        grid_spec=pltpu.PrefetchScalarGridSpec(
            num_scalar_prefetch=1, grid=(n_steps,),
            in_specs=[landed_spec(j) for j in range(ns)] + [pl.BlockSpec((None, tr, cols), lambda i, pc: (pc[0], i, 0))],
            out_specs=[out, out],
            scratch_shapes=[pltpu.VMEM((tr, cols), F32), pltpu.VMEM((2, tr, cols), F32), pltpu.SemaphoreType.DMA((2,)),
                            pltpu.SemaphoreType.DMA((2,)), pltpu.SemaphoreType.REGULAR]),
        out_shape=[jax.ShapeDtypeStruct((rows, cols), F32)] * 2, compiler_params=_params(("arbitrary",)),
    )(chip, *([landed] * ns), sent)


def _gather_all(name, a):
    def body(in_ref, out_ref, send_sems, recv_sems, local_sem):
        x, y, c = _my_place()
        me = 4 * x + 2 * y + c

        def peer(k):
            return (x ^ (k >> 2), y ^ ((k >> 1) & 1), c ^ (k & 1))

        def remote(k, land):
            return pltpu.make_async_remote_copy(
                src_ref=in_ref, dst_ref=out_ref.at[land], send_sem=send_sems.at[k - 1], recv_sem=recv_sems.at[k - 1],
                device_id=peer(k), device_id_type=pl.DeviceIdType.MESH)

        local = pltpu.make_async_copy(in_ref, out_ref.at[me], local_sem)
        local.start()
        for k in range(1, N_DEV):
            remote(k, me).start()
        for k in range(1, N_DEV):
            px, py, pc = peer(k)
            remote(k, 4 * px + 2 * py + pc).wait_recv()
        for k in range(1, N_DEV):
            remote(k, me).wait_send()
        local.wait()

    hbm = pl.BlockSpec(memory_space=pl.ANY)
    return pl.pallas_call(
        body, name=name, in_specs=[hbm], out_specs=hbm,
        out_shape=jax.ShapeDtypeStruct((N_DEV,) + a.shape, a.dtype),
        scratch_shapes=[pltpu.SemaphoreType.DMA((N_DEV - 1,)), pltpu.SemaphoreType.DMA((N_DEV - 1,)), pltpu.SemaphoreType.DMA],
    )(a)


def _row_tile_for(rows, cols, budget=1 << 20):
    if rows * cols * 4 <= budget:
        return rows
    for tm in (1024, 512, 256, 128, 64, 32, 16, 8):
        if rows % tm == 0 and tm * cols * 4 <= budget:
            return tm
    return rows


def _slot_sum(vals):
    g = vals[0]
    for v in vals[1:]:
        g = g + v
    return g


def _rowwise(name, fn, arrays, out_dtype):
    rows, cols = arrays[0].shape
    tm = _row_tile_for(rows, cols)

    def body(*refs):
        refs[-1][...] = fn(*[r[...] for r in refs[:-1]]).astype(out_dtype)

    blk = pl.BlockSpec((tm, cols), lambda i: (i, 0))
    return pl.pallas_call(
        body, name=name, grid=(rows // tm,), in_specs=[blk] * len(arrays), out_specs=blk,
        out_shape=jax.ShapeDtypeStruct((rows, cols), out_dtype), compiler_params=_params(("parallel",)),
    )(*arrays)


def _sum_slots(name, st):
    ns, rows, cols = st.shape
    tm = _row_tile_for(rows, cols)

    def body(s_ref, o_ref):
        o_ref[...] = _slot_sum([s_ref[j].astype(F32) for j in range(ns)])

    return pl.pallas_call(
        body, name=name, grid=(rows // tm,),
        in_specs=[pl.BlockSpec((ns, tm, cols), lambda i: (0, i, 0))],
        out_specs=pl.BlockSpec((tm, cols), lambda i: (i, 0)),
        out_shape=jax.ShapeDtypeStruct((rows, cols), F32),
        compiler_params=_params(("parallel",)),
    )(st)


ADAM_TILE_BYTES = 1 << 20


def _adam_update(g, p_ref, m_ref, v_ref, go_ref, d_ref, mo_ref, vo_ref):
    mn = ADAM_B1 * m_ref[...] + (1.0 - ADAM_B1) * g
    vn = ADAM_B2 * v_ref[...] + (1.0 - ADAM_B2) * jnp.square(g)
    m_hat = mn / (1.0 - ADAM_B1 ** ADAM_STEP)
    v_hat = vn / (1.0 - ADAM_B2 ** ADAM_STEP)
    go_ref[...] = g
    d_ref[...] = -ADAM_LR * (m_hat / (jnp.sqrt(v_hat) + ADAM_EPS) + ADAM_WD * p_ref[...])
    mo_ref[...] = mn
    vo_ref[...] = vn


def _adamw(name, p, m, v, gst):
    rows, cols = p.shape
    ns = gst.shape[0]
    tm = _row_tile_for(rows, cols, budget=ADAM_TILE_BYTES)

    def body(p_ref, m_ref, v_ref, g_ref, *outs):
        _adam_update(_slot_sum([g_ref[j] for j in range(ns)]), p_ref, m_ref, v_ref, *outs)

    blk = pl.BlockSpec((tm, cols), lambda i: (i, 0))
    return pl.pallas_call(
        body, name=name, grid=(rows // tm,),
        in_specs=[blk, blk, blk, pl.BlockSpec((ns, tm, cols), lambda i: (0, i, 0))],
        out_specs=[blk] * 4, out_shape=[jax.ShapeDtypeStruct((rows, cols), F32)] * 4,
        compiler_params=_params(("parallel",)),
    )(p, m, v, gst)


def _adamw_halves(name, p, m, v, mine, theirs, place, scatter=()):
    rows, cols = p.shape
    half = rows // 2
    tm = _row_tile_for(half, cols, budget=ADAM_TILE_BYTES)
    nh = half // tm
    ns = len(scatter)

    def body(pc_ref, p_ref, m_ref, v_ref, mine_ref, theirs_ref, *refs):
        if ns:
            sc_refs = (refs[:ns], refs[ns + 4:2 * ns + 4]) + tuple(refs[2 * ns + 4:])
            at = pl.program_id(0) * nh + pl.program_id(1)
            pl.when(at == 0)(lambda: _scatter_start(scatter, *sc_refs))
        g = jnp.where(pl.program_id(0) == pc_ref[0], mine_ref[...], theirs_ref[...])
        _adam_update(g, p_ref, m_ref, v_ref, *refs[ns:ns + 4])
        if ns:
            pl.when(at == 2 * nh - 1)(lambda: _scatter_wait(scatter, *sc_refs))

    blk = pl.BlockSpec((tm, cols), lambda h, i, pc: (h * nh + i, 0))
    hblk = pl.BlockSpec((tm, cols), lambda h, i, pc: (i, 0))
    hbm = pl.BlockSpec(memory_space=pl.ANY)
    res = pl.pallas_call(
        body, name=name,
        grid_spec=pltpu.PrefetchScalarGridSpec(
            num_scalar_prefetch=1, grid=(2, nh), in_specs=[blk, blk, blk, hblk, hblk] + [hbm] * ns,
            out_specs=[blk] * 4 + [hbm] * ns,
            scratch_shapes=[pltpu.SemaphoreType.DMA((ns, _PEER_CHIPS))] * 2 if ns else []),
        out_shape=[jax.ShapeDtypeStruct((rows, cols), F32)] * 4 + [jax.ShapeDtypeStruct(s.shape, s.dtype) for s in scatter],
        compiler_params=_params(("arbitrary", "arbitrary") if ns else ("parallel", "parallel")),
    )(place, p, m, v, mine, theirs, *scatter)
    return (list(res[:4]), list(res[4:])) if ns else res


def _pack(parts, width=LANE, mult=8):
    flat = jnp.concatenate([a.reshape(-1) for a in parts])
    n = flat.shape[0]
    per = width * mult
    total = -(-n // per) * per
    return jnp.pad(flat, (0, total - n)).reshape(total // width, width)


def _unpack(packed, shapes):
    flat = packed.reshape(-1)
    out, off = [], 0
    for s in shapes:
        n = 1
        for d in s:
            n *= d
        out.append(flat[off:off + n].reshape(s))
        off += n
    return out


_SMALL_SHARDED = ("hg_lb", "rw_mu", "rw_w0", "rw_w2", "rw_a0", "rw_a2")
_REPLICATED = ("c_ctx", "ada_b", "norm_g", "hg_norm_g", "rw_kk", "rw_ka", "rw_rk", "rw_gn_g", "rw_gn_b", "final_g")
_GATHERED = ("w_in", "w_hg_out", "w_rw_out", "w_out")
_WEIGHTS = ("c_ctx", "ada_w", "ada_b", "norm_g", "w_in", "hg_lb", "hg_norm_g", "rw_mu", "rw_w0", "rw_w2", "rw_a0", "rw_a2",
            "rw_kk", "rw_ka", "rw_rk", "rw_gn_g", "rw_gn_b", "w_hg_out", "w_rw_out", "w_out", "final_g")


def _pack_small(d):
    return _pack([d[n] for n in _SMALL_SHARDED], mult=2 * ROW_ALIGN)


def _join_shards(st):
    a = jnp.moveaxis(st, 0, -2)
    return a.reshape(a.shape[:-2] + (a.shape[-2] * a.shape[-1],))


def _split_shards(a):
    s = a.reshape(a.shape[:-1] + (N_SHARD, a.shape[-1] // N_SHARD))
    return jnp.moveaxis(s, -2, 0)


def kernel(x, c, ctx, c_ctx, ada_w, ada_b, norm_g, w_in, hg_lb, hg_norm_g, rw_mu, rw_w0, rw_w2, rw_a0, rw_a2, rw_kk, rw_ka, rw_rk, rw_gn_g, rw_gn_b, w_hg_out, w_rw_out, w_out, final_g, loss_target, m_c_ctx, m_ada_w, m_ada_b, m_norm_g, m_w_in, m_hg_lb, m_hg_norm_g, m_rw_mu, m_rw_w0, m_rw_w2, m_rw_a0, m_rw_a2, m_rw_kk, m_rw_ka, m_rw_rk, m_rw_gn_g, m_rw_gn_b, m_w_hg_out, m_w_rw_out, m_w_out, m_final_g, v_c_ctx, v_ada_w, v_ada_b, v_norm_g, v_w_in, v_hg_lb, v_hg_norm_g, v_rw_mu, v_rw_w0, v_rw_w2, v_rw_a0, v_rw_a2, v_rw_kk, v_rw_ka, v_rw_rk, v_rw_gn_g, v_rw_gn_b, v_w_hg_out, v_w_rw_out, v_w_out, v_final_g):
    w = dict(c_ctx=c_ctx, ada_w=ada_w, ada_b=ada_b, norm_g=norm_g, w_in=w_in, hg_lb=hg_lb, hg_norm_g=hg_norm_g, rw_mu=rw_mu,
             rw_w0=rw_w0, rw_w2=rw_w2, rw_a0=rw_a0, rw_a2=rw_a2, rw_kk=rw_kk, rw_ka=rw_ka, rw_rk=rw_rk, rw_gn_g=rw_gn_g,
             rw_gn_b=rw_gn_b, w_hg_out=w_hg_out, w_rw_out=w_rw_out, w_out=w_out, final_g=final_g)
    m = dict(c_ctx=m_c_ctx, ada_w=m_ada_w, ada_b=m_ada_b, norm_g=m_norm_g, w_in=m_w_in, hg_lb=m_hg_lb, hg_norm_g=m_hg_norm_g,
             rw_mu=m_rw_mu, rw_w0=m_rw_w0, rw_w2=m_rw_w2, rw_a0=m_rw_a0, rw_a2=m_rw_a2, rw_kk=m_rw_kk, rw_ka=m_rw_ka,
             rw_rk=m_rw_rk, rw_gn_g=m_rw_gn_g, rw_gn_b=m_rw_gn_b, w_hg_out=m_w_hg_out, w_rw_out=m_w_rw_out, w_out=m_w_out,
             final_g=m_final_g)
    v = dict(c_ctx=v_c_ctx, ada_w=v_ada_w, ada_b=v_ada_b, norm_g=v_norm_g, w_in=v_w_in, hg_lb=v_hg_lb, hg_norm_g=v_hg_norm_g,
             rw_mu=v_rw_mu, rw_w0=v_rw_w0, rw_w2=v_rw_w2, rw_a0=v_rw_a0, rw_a2=v_rw_a2, rw_kk=v_rw_kk, rw_ka=v_rw_ka,
             rw_rk=v_rw_rk, rw_gn_g=v_rw_gn_g, rw_gn_b=v_rw_gn_b, w_hg_out=v_w_hg_out, w_rw_out=v_w_rw_out, w_out=v_w_out,
             final_g=v_final_g)

    def mat(a):
        return a.reshape(a.shape[-2], a.shape[-1])

    my_core = lax.axis_index("c").astype(jnp.int32).reshape(1)
    my_chip = (2 * lax.axis_index("x") + lax.axis_index("y")).astype(jnp.int32).reshape(1)

    my_dev = 2 * my_chip[0] + my_core[0]
    dm = x.shape[-1]
    ada_cols = ada_w.shape[-1]

    c_all = _gather_all("cond_gather", c.reshape(1, dm)).reshape(N_DEV, dm)
    cond16 = jnp.concatenate([c_all, c_ctx.reshape(1, dm), jnp.zeros((7, dm), F32)], axis=0)
    (sc16,) = _row_call("cond_silu", lambda i, r, f: ([jax.nn.silu(r[0])], []), 1, 16, [(cond16, 0, dm, 0)], [],
                        [(16, dm, F32, 0)], [])
    mod_here = _mm_nn("mod_mm", sc16, mat(ada_w))
    mod_all = _gather_all("mod_gather", mod_here)
    mod_rows = jnp.concatenate([mod_all[2 * j] for j in range(N_SHARD)], axis=1)
    mine = lax.dynamic_slice_in_dim(mod_rows, my_dev, 1, axis=0)
    mod = _add_small("mod_bias", [jnp.concatenate([mine, mod_rows[N_DEV:N_DEV + 1], jnp.zeros((6, 3 * dm), F32)], axis=0),
                                  jnp.broadcast_to(ada_b, (8, 3 * dm))], (8, 3 * dm))

    small_shapes = [w[n].shape for n in _SMALL_SHARDED]
    big_bf = [_cast_into_slot(f"to_bf16_{n}", mat(w[n]), my_chip) for n in _GATHERED]
    small_mine = _pack_small(w)
    small_slots = lax.dynamic_update_slice(jnp.zeros((N_SHARD,) + small_mine.shape, F32), small_mine[None], (my_chip[0], 0, 0))
    gathered = _weights_gather("weights_gather", big_bf, [small_slots])
    w_in_st, w_hg_st, w_rw_st, w_out_st, small_st = gathered
    full_small = {}
    per_chip = [_unpack(small_st[j], small_shapes) for j in range(N_SHARD)]
    for i, n in enumerate(_SMALL_SHARDED):
        full_small[n] = _join_shards(jnp.stack([per_chip[j][i] for j in range(N_SHARD)], axis=0))
    w_out_full = w_out_st.reshape(dm, dm)

    loss_b, grad_x, dmod, g = _local_step(
        x[0], ctx[0], mod, norm_g, w_in_st, full_small["hg_lb"], hg_norm_g, full_small["rw_mu"][0],
        full_small["rw_w0"][0], full_small["rw_w2"][0], full_small["rw_a0"][0], full_small["rw_a2"][0], rw_kk, rw_ka, rw_rk,
        rw_gn_g, rw_gn_b, w_hg_st, w_rw_st, w_out_full, final_g, loss_target[0], my_core)
    loss = lax.psum(loss_b[0, 0], ("x", "y", "c"))

    dmod_all = _gather_all("dmod_gather", dmod[0:2])
    dmod_here = lax.dynamic_slice_in_dim(dmod_all, my_chip[0] * ada_cols, ada_cols, axis=2)
    d_ctx_row = _add_small("d_mod_ctx", [dmod_here[j, 1:2] for j in range(N_DEV)], (1, ada_cols))
    dm16 = jnp.concatenate([dmod_here[:, 0], d_ctx_row, jnp.zeros((7, ada_cols), F32)], axis=0)
    g_ada_here = _mm_tn("d_ada_w", sc16, dm16)
    d_sc16 = _mm_nt("d_cond", dm16, mat(ada_w))

    def cond_bwd(i, r, f):
        _, vjp = jax.vjp(jax.nn.silu, r[0])
        return [vjp(r[1])[0]], []

    (d_cond16,) = _row_call("cond_bwd", cond_bwd, 1, 16, [(cond16, 0, dm, 0), (d_sc16, 0, dm, 0)], [], [(16, dm, F32, 0)], [])
    g["c_ctx"] = jnp.where(my_core[0] == 0, d_cond16[N_DEV], 0.0)
    g["ada_b"] = _add_small("g_ada_b", [dmod[0:1], dmod[1:2]], (1, 3 * dm))

    def finish(name, chip_sum, landed):
        return _sum_and_swap(f"grads_sum_swap_{name}", landed, chip_sum, my_chip)

    res = {}
    rep_shapes = [w[n].shape for n in _REPLICATED]
    rep_all = _gather_all("grads_replicated", _pack([g[n].reshape(w[n].shape) for n in _REPLICATED]))
    outs = _adamw("adamw_ada_w", mat(ada_w), mat(m["ada_w"]), mat(v["ada_w"]), g_ada_here[None])
    res["ada_w"] = [o.reshape(ada_w.shape) for o in outs]
    for n in _GATHERED:
        outs = _adamw_halves(f"adamw_{n}", mat(w[n]), mat(m[n]), mat(v[n]), *finish(n, *g[n]), my_core)
        res[n] = [o.reshape(w[n].shape) for o in outs]
    outs = _adamw_halves("adamw_small", small_mine, _pack_small(m), _pack_small(v), *finish("small", *g["small"]), my_core)
    for i, vals in enumerate(zip(*[_unpack(o, small_shapes) for o in outs])):
        res[_SMALL_SHARDED[i]] = list(vals)
    outs = _adamw("adamw_replicated", _pack([w[n] for n in _REPLICATED]), _pack([m[n] for n in _REPLICATED]),
                  _pack([v[n] for n in _REPLICATED]), rep_all)
    for i, vals in enumerate(zip(*[_unpack(o, rep_shapes) for o in outs])):
        res[_REPLICATED[i]] = list(vals)

    return (loss, grad_x[None], *[res[n][0] for n in _WEIGHTS], *[res[n][1] for n in _WEIGHTS],
            *[res[n][2] for n in _WEIGHTS], *[res[n][3] for n in _WEIGHTS])
```

```python
import functools

import jax
import jax.numpy as jnp
from jax import lax
from jax.experimental import pallas as pl
from jax.experimental.pallas import tpu as pltpu

HI = lax.Precision.HIGHEST
F32 = jnp.float32
BF16 = jnp.bfloat16

NORM_EPS = 1e-6
HG_HEAD = 128
RW_HEAD = 64
RW_LORA = 64
RW_GN_EPS = 64e-5
GRID_W = 64
SUB = 16
RW_SUB = 16
STEP = 64
RW_STEP = 64
N_SHARD = 4
N_DEV = 8
LANE = 128

ADAM_LR = 0.001
ADAM_B1 = 0.9
ADAM_B2 = 0.999
ADAM_EPS = 1e-08
ADAM_WD = 0.01
ADAM_STEP = 10

VMEM_LIMIT = 56 * 1024 * 1024


def _params(sem=None):
    return pltpu.CompilerParams(dimension_semantics=sem, vmem_limit_bytes=VMEM_LIMIT)


def _tile(n, cands):
    for c in cands:
        if n % c == 0:
            return c
    return n


def _iota2(n, m, d):
    return lax.broadcasted_iota(jnp.int32, (n, m), d)


def _before(n, rev, strict):
    t, s = _iota2(n, n, 0), _iota2(n, n, 1)
    if rev:
        return (s > t) if strict else (s >= t)
    return (s < t) if strict else (s <= t)


def _running_sum(a, axis, rev):
    n = a.shape[axis]
    shift = 1
    while shift < n:
        pad = list(a.shape)
        pad[axis] = shift
        zeros = jnp.zeros(pad, a.dtype)
        if rev:
            moved = jnp.concatenate([lax.slice_in_dim(a, shift, n, axis=axis), zeros], axis=axis)
        else:
            moved = jnp.concatenate([zeros, lax.slice_in_dim(a, 0, n - shift, axis=axis)], axis=axis)
        a = a + moved
        shift *= 2
    return a


def _sdot(a, b, spec):
    return jnp.einsum(spec, a, b, precision=lax.Precision.DEFAULT, preferred_element_type=F32)


def _hg_step(s0, qraw, iin, fin, lb2, rev):
    c, w = qraw.shape
    h = w // HG_HEAD
    nsub = c // SUB
    lb = jax.nn.sigmoid(lb2[0:1] - lb2[1:2])
    q = jax.nn.silu(qraw)
    fg = lb + (1.0 - lb) * jax.nn.sigmoid(fin)
    kk = 1.0 - fg
    g = jnp.log(fg)
    bcum = _running_sum(g, 0, rev)
    def heads(a):
        return jnp.swapaxes(a.reshape(a.shape[0], h, HG_HEAD), 0, 1)

    def unheads(a):
        return jnp.swapaxes(a, 0, 1).reshape(a.shape[1], w)

    blocks = [slice(j * SUB, (j + 1) * SUB) for j in range(nsub)]
    outs = []
    for sl in blocks:
        qs, ks, vs, bc = [a[sl].reshape(SUB, h, HG_HEAD) for a in (q, kk, iin, bcum)]
        o = jnp.zeros((SUB, h, HG_HEAD), F32)
        for si in range(SUB):
            after = slice(0, si + 1) if rev else slice(si, SUB)
            dec = jnp.exp(jnp.minimum(bc[after] - bc[si:si + 1], 0.0))
            a = jnp.sum(qs[after] * ks[si:si + 1] * dec, axis=-1, keepdims=True)
            term = a * vs[si:si + 1]
            n_rest = SUB - 1 - si if rev else si
            if n_rest:
                rest = jnp.zeros((n_rest, h, HG_HEAD), F32)
                term = jnp.concatenate([term, rest] if rev else [rest, term], axis=0)
            o = o + term
        outs.append(o.reshape(SUB, w))
    order = list(range(nsub - 1, -1, -1)) if rev else list(range(nsub))
    for pos in range(1, nsub):
        j, before = order[pos], order[:pos]
        first = (j + 1) * SUB - 1 if rev else j * SUB
        bstart = bcum[first:first + 1] - g[first:first + 1]
        qp = heads(q[blocks[j]] * jnp.exp(bcum[blocks[j]] - bstart))
        kp = heads(jnp.concatenate([kk[blocks[p]] * jnp.exp(bstart - bcum[blocks[p]]) for p in before], axis=0))
        vp = heads(jnp.concatenate([iin[blocks[p]] for p in before], axis=0))
        outs[j] = outs[j] + unheads(_sdot(_sdot(qp, kp, 'htk,hsk->hts'), vp, 'hts,hsv->htv'))
    o_state = unheads(_sdot(heads(q * jnp.exp(bcum)), s0, 'htk,hvk->htv'))
    last = 0 if rev else c - 1
    blast = bcum[last:last + 1]
    s_new = heads(jnp.exp(blast)) * s0 + _sdot(heads(iin), heads(kk * jnp.exp(blast - bcum)), 'hsv,hsk->hvk')
    return jnp.concatenate(outs, axis=0) + o_state, s_new


def _tri_solve(lmat, rhs, rev):
    hh, c, _ = lmat.shape
    sub = RW_SUB
    nb = c // sub
    diag = jnp.concatenate([lmat[:, i * sub:(i + 1) * sub, i * sub:(i + 1) * sub] for i in range(nb)], axis=0)
    dt = jnp.transpose(diag, (1, 2, 0))
    col = lax.broadcasted_iota(jnp.int32, (sub, 1), 0)
    inv_rows = [None] * sub
    order = list(range(sub - 1, -1, -1)) if rev else list(range(sub))
    for pos, t in enumerate(order):
        row = jnp.broadcast_to((col == t).astype(F32), (sub, dt.shape[2]))
        for s in order[:pos]:
            row = row - dt[t, s:s + 1, :] * inv_rows[s]
        inv_rows[t] = row
    tinv = jnp.transpose(jnp.concatenate([r[None] for r in inv_rows], axis=0), (2, 0, 1))
    p = [None] * nb
    done = []
    for i in (range(nb - 1, -1, -1) if rev else range(nb)):
        r = rhs[:, i * sub:(i + 1) * sub]
        if done:
            lrow = jnp.concatenate([lmat[:, i * sub:(i + 1) * sub, m * sub:(m + 1) * sub] for m in done], axis=2)
            r = r - _sdot(lrow, jnp.concatenate([p[m] for m in done], axis=1), 'hts,hsv->htv')
        p[i] = _sdot(tinv[i * hh:(i + 1) * hh], r, 'hts,hsv->htv')
        done.append(i)
    return jnp.concatenate(p, axis=1)


def _rw_step(s0, r, k, v, wlo, alo, w0h, w2h, a0h, a2h, kkh, kah, rev):
    hh, c, _ = r.shape
    tl = jnp.broadcast_to(jnp.tanh(wlo)[None], (hh, c, wlo.shape[1]))
    al = jnp.broadcast_to(alo[None], (hh, c, alo.shape[1]))
    wlog = -jax.nn.softplus(-(w0h + _sdot(tl, w2h, 'hcl,hlj->hcj'))) - 0.5
    lw = -jnp.exp(wlog)
    a = jax.nn.sigmoid(a0h + _sdot(al, a2h, 'hcl,hlj->hcj'))
    kk = k * kkh
    kk = kk * lax.rsqrt(jnp.sum(kk * kk, axis=-1, keepdims=True) + 1e-12)
    kd = k * (1.0 + (a - 1.0) * kah)
    b = kk * a
    cum = _running_sum(lw, 1, rev)
    ecum, encum = jnp.exp(cum), jnp.exp(-cum)
    alpha = jnp.exp(cum - lw) * kk
    beta = b * encum
    kappa = kd * encum
    rho = r * ecum
    m_lt = _before(c, rev, True)[None]
    m_le = _before(c, rev, False)[None]
    ar = jnp.concatenate([alpha, rho], axis=1)
    kb = jnp.concatenate([kappa, beta], axis=1)
    gram = _sdot(ar, kb, 'htk,hsk->hts')
    a_kap = jnp.where(m_lt, gram[:, :c, :c], 0.0)
    a_bet = jnp.where(m_lt, gram[:, :c, c:], 0.0)
    b_kap = jnp.where(m_le, gram[:, c:, :c], 0.0)
    b_bet = jnp.where(m_le, gram[:, c:, c:], 0.0)
    from_state = _sdot(ar, s0, 'htk,hvk->htv')
    p = _tri_solve(a_bet, from_state[:, :c] + _sdot(a_kap, v, 'hts,hsv->htv'), rev)
    vp = jnp.concatenate([v, -p], axis=1)
    y = from_state[:, c:] + _sdot(jnp.concatenate([b_kap, b_bet], axis=2), vp, 'hts,hsv->htv')
    stil = s0 + _sdot(vp, kb, 'hsv,hsk->hvk')
    last = 0 if rev else c - 1
    return y, stil * ecum[:, last:last + 1, :]


def _fn_h(s, norm_g, scale, shift):
    return s * lax.rsqrt(jnp.mean(s * s, axis=-1, keepdims=True) + NORM_EPS) * norm_g * (1.0 + scale) + shift


def _fn_hgpost(of, ob, z, g):
    tm, w = of.shape
    o = (of + ob).reshape(tm, w // HG_HEAD, HG_HEAD)
    o = o * lax.rsqrt(jnp.mean(o * o, axis=-1, keepdims=True) + NORM_EPS)
    return o.reshape(tm, w) * g * jax.nn.silu(z)


def _fn_rwpost(y0, y1, r, k, v, alo, z, a0, a2, k_a, r_k, gn_g, gn_b):
    tm, w = r.shape
    nh = w // RW_HEAD
    asum = 0.0
    for d in range(2):
        asum = asum + jax.nn.sigmoid(a0[d:d + 1] + jnp.dot(alo[:, d * RW_LORA:(d + 1) * RW_LORA], a2[d],
                                                           precision=HI, preferred_element_type=F32))
    k_sum = k * (2.0 + (asum - 2.0) * k_a)
    ys = (y0 + y1).reshape(tm, nh, RW_HEAD)
    mean = jnp.mean(ys, axis=-1, keepdims=True)
    var = jnp.mean(jnp.square(ys - mean), axis=-1, keepdims=True)
    y = ((ys - mean) * lax.rsqrt(var + RW_GN_EPS)).reshape(tm, w) * gn_g + gn_b
    bonus = jnp.sum((r * k_sum * r_k).reshape(tm, nh, RW_HEAD), axis=-1, keepdims=True) * v.reshape(tm, nh, RW_HEAD)
    return (y + bonus.reshape(tm, w)) * jax.nn.silu(z)


def _fn_merge(a, b, ghg, grw):
    return jax.nn.sigmoid(ghg) * a + jax.nn.sigmoid(grw) * b


def _fn_final(xs, o, gate, final_g, tgt):
    x2 = xs + gate * o
    y = x2 * lax.rsqrt(jnp.mean(x2 * x2, axis=-1, keepdims=True) + NORM_EPS) * final_g
    return 0.5 * jnp.sum(jnp.mean(jnp.square(y - tgt), axis=-1))


def _row_call(name, fn, n_tiles, tm, row_ins, full_ins, row_outs, acc_outs):
    n_ri, n_fi, n_ro = len(row_ins), len(full_ins), len(row_outs)

    def body(*refs):
        i = pl.program_id(0)
        rvals = [r[...] for r in refs[:n_ri]]
        fvals = [r[...] for r in refs[n_ri:n_ri + n_fi]]
        outs = refs[n_ri + n_fi:]
        ro, ao = fn(i, rvals, fvals)
        for ref, val in zip(outs[:n_ro], ro):
            ref[...] = val.astype(ref.dtype)
        for ref, val in zip(outs[n_ro:], ao):
            @pl.when(i == 0)
            def _(ref=ref):
                ref[...] = jnp.zeros_like(ref)
            ref[...] += val.astype(ref.dtype)

    def rspec(width, cb, off, rows):
        return pl.BlockSpec((tm, width), lambda i: (jnp.clip(i - off, 0, rows // tm - 1), cb))

    def fspec(shape):
        nd = len(shape)
        return pl.BlockSpec(shape, lambda i: (0,) * nd)

    in_specs = [rspec(w, cb, off, a.shape[0]) for (a, cb, w, off) in row_ins] + [fspec(a.shape) for a in full_ins]
    out_specs = [rspec(w, 0, off, rows) for (rows, w, _, off) in row_outs] + [fspec(s) for (s, _) in acc_outs]
    out_shape = [jax.ShapeDtypeStruct((rows, w), dt) for (rows, w, dt, _) in row_outs] + \
                [jax.ShapeDtypeStruct(s, dt) for (s, dt) in acc_outs]
    res = pl.pallas_call(
        body, name=name, grid=(n_tiles,), in_specs=in_specs, out_specs=out_specs, out_shape=out_shape,
        compiler_params=_params(("arbitrary",)),
    )(*[a for (a, _, _, _) in row_ins], *full_ins)
    return list(res)


def _mm(name, a, b, m, n, k_steps, tm, tn, a_block, a_map, b_block, b_map, o_shape, o_block, o_map,
        contract, out_dtype=F32, scatter=()):
    ns = len(scatter)
    grid = (m // tm, n // tn, k_steps)

    def body(*refs):
        a_ref, b_ref, o_ref, acc_ref = refs[0], refs[1], refs[2 + ns], refs[3 + 2 * ns]
        kk = pl.program_id(2)
        if ns:
            sc_refs = (refs[2:2 + ns], refs[3 + ns:3 + 2 * ns]) + tuple(refs[4 + 2 * ns:])
            at = (pl.program_id(0) * grid[1] + pl.program_id(1)) * grid[2] + kk
            pl.when(at == 0)(lambda: _scatter_start(scatter, *sc_refs))

        @pl.when(kk == 0)
        def _():
            acc_ref[...] = jnp.zeros_like(acc_ref)

        acc_ref[...] += lax.dot_general(a_ref[...].astype(BF16), b_ref[...].astype(BF16),
                                        (contract, ((), ())), preferred_element_type=F32)

        @pl.when(kk == k_steps - 1)
        def _():
            o_ref[...] = acc_ref[...].astype(o_ref.dtype)

        if ns:
            pl.when(at == grid[0] * grid[1] * grid[2] - 1)(lambda: _scatter_wait(scatter, *sc_refs))

    hbm = pl.BlockSpec(memory_space=pl.ANY)
    sems = [pltpu.SemaphoreType.DMA((ns, _PEER_CHIPS))] * 2 if ns else []
    res = pl.pallas_call(
        body, name=name, grid=grid,
        in_specs=[pl.BlockSpec(a_block, a_map), pl.BlockSpec(b_block, b_map)] + [hbm] * ns,
        out_specs=[pl.BlockSpec(o_block, o_map)] + [hbm] * ns,
        out_shape=[jax.ShapeDtypeStruct(o_shape, out_dtype)] + [jax.ShapeDtypeStruct(s.shape, s.dtype) for s in scatter],
        scratch_shapes=[pltpu.VMEM((tm, tn), F32)] + sems,
        compiler_params=_params(("arbitrary",) * 3 if ns else ("parallel", "parallel", "arbitrary")),
    )(a, b, *scatter)
    return (res[0], list(res[1:])) if ns else res[0]


_TM = (768, 512, 256, 128, 64, 32, 16, 8)
_TN = (512, 256, 128)
_TK = (1024, 768, 512, 256, 128)
_TK_WIDE = (768, 512, 256, 128)
WIDE_OUT_BYTES = 32 << 20


def _tm_wide(m, ns):
    for tm in _TM:
        if m % tm == 0 and 3 * 4 * tm * ns <= WIDE_OUT_BYTES:
            return tm
    return m


def _mm_nn(name, a, b, out_dtype=F32):
    m, k = a.shape
    n = b.shape[1]
    tm, tn, tk = _tile(m, _TM), _tile(n, _TN), _tile(k, _TK)
    return _mm(name, a, b, m, n, k // tk, tm, tn, (tm, tk), lambda i, j, s: (i, s), (tk, tn), lambda i, j, s: (s, j),
               (m, n), (tm, tn), lambda i, j, s: (i, j), ((1,), (0,)), out_dtype)


def _mm_nt(name, a, b, out_dtype=F32):
    m, k = a.shape
    n = b.shape[0]
    tm, tn, tk = _tile(m, _TM), _tile(n, _TN), _tile(k, _TK)
    return _mm(name, a, b, m, n, k // tk, tm, tn, (tm, tk), lambda i, j, s: (i, s), (tn, tk), lambda i, j, s: (j, s),
               (m, n), (tm, tn), lambda i, j, s: (i, j), ((1,), (1,)), out_dtype)


def _mm_tn(name, a, b, out_dtype=F32):
    k, m = a.shape
    n = b.shape[1]
    tm, tn, tk = _tile(m, _TM), _tile(n, _TN), _tile(k, _TK)
    return _mm(name, a, b, m, n, k // tk, tm, tn, (tk, tm), lambda i, j, s: (s, i), (tk, tn), lambda i, j, s: (s, j),
               (m, n), (tm, tn), lambda i, j, s: (i, j), ((0,), (0,)), out_dtype)


def _mm_n_st(name, a, bst, out_dtype=F32, joined=False):
    m, k = a.shape
    ns_, _, ns = bst.shape
    tm, tk = _tm_wide(m, ns), _tile(k, (512, 256, 128))
    out = ((m, ns_ * ns), (tm, ns), lambda i, j, s: (i, j)) if joined else \
          ((ns_, m, ns), (None, tm, ns), lambda i, j, s: (j, i, 0))
    return _mm(name, a, bst, m, ns_ * ns, k // tk, tm, ns,
               (tm, tk), lambda i, j, s: (i, s), (None, tk, ns), lambda i, j, s: (j, s, 0), *out, ((1,), (0,)), out_dtype)


def _mm_st_t(name, ast, bst, out_dtype=F32, scatter=()):
    ns_, n, ns = bst.shape
    m = ast.shape[-2]
    tm, tn = _tile(m, _TM), _tile(n, _TN)
    a_side = ((None, tm, ns), lambda i, j, s: (s, i, 0)) if ast.ndim == 3 else ((tm, ns), lambda i, j, s: (i, s))
    return _mm(name, ast, bst, m, n, ns_, tm, tn, *a_side, (None, tn, ns), lambda i, j, s: (s, j, 0),
               (m, n), (tm, tn), lambda i, j, s: (i, j), ((1,), (1,)), out_dtype, scatter)


def _mm_t_st(name, a, bst, out_dtype=F32, scatter=(), n_shard=N_SHARD):
    k, m = a.shape
    ns = bst.shape[-1] if bst.ndim == 3 else bst.shape[-1] // n_shard
    tm, tk = _tile(m, _TN), _tile(k, _TK_WIDE)
    b_side = ((None, tk, ns), lambda i, j, s: (j, s, 0)) if bst.ndim == 3 else ((tk, ns), lambda i, j, s: (s, j))
    return _mm(name, a, bst, m, n_shard * ns, k // tk, tm, ns, (tk, tm), lambda i, j, s: (s, i), *b_side,
               (n_shard, m, ns), (None, tm, ns), lambda i, j, s: (j, i, 0), ((0,), (0,)), out_dtype, scatter)


RELAYOUT_TILE_BYTES = 12 << 20


def _join_columns(name, st):
    ns_, t, ns = st.shape
    tm = _row_tile_for(t, ns_ * ns, budget=RELAYOUT_TILE_BYTES)

    def body(s_ref, o_ref):
        o_ref[...] = jnp.concatenate([s_ref[j] for j in range(ns_)], axis=1)

    return pl.pallas_call(
        body, name=name, grid=(t // tm,), in_specs=[pl.BlockSpec((ns_, tm, ns), lambda i: (0, i, 0))],
        out_specs=pl.BlockSpec((tm, ns_ * ns), lambda i: (i, 0)),
        out_shape=jax.ShapeDtypeStruct((t, ns_ * ns), st.dtype), compiler_params=_params(("parallel",)),
    )(st)


def _split_columns(name, pieces, n_shard):
    t = pieces[0].shape[0]
    n = sum(p.shape[1] for p in pieces)
    ns = n // n_shard
    tm = _row_tile_for(t, n, budget=RELAYOUT_TILE_BYTES)
    npc = len(pieces)

    def body(*refs):
        full = jnp.concatenate([r[...] for r in refs[:npc]], axis=1)
        for j in range(n_shard):
            refs[npc][j] = full[:, j * ns:(j + 1) * ns]

    return pl.pallas_call(
        body, name=name, grid=(t // tm,),
        in_specs=[pl.BlockSpec((tm, p.shape[1]), lambda i: (i, 0)) for p in pieces],
        out_specs=pl.BlockSpec((n_shard, tm, ns), lambda i: (0, i, 0)),
        out_shape=jax.ShapeDtypeStruct((n_shard, t, ns), pieces[0].dtype), compiler_params=_params(("parallel",)),
    )(*pieces)


def _scan_order(j, n_ctx, n_all, rev):
    if not rev:
        return j
    return jnp.where(j < n_ctx, n_ctx - 1 - j, n_all - 1 - (j - n_ctx))


def _hg_scan_fwd(name, p_hg, lb2, d, n_ctx):
    t, w = p_hg.shape[0], lb2.shape[1]
    h = w // HG_HEAD
    n = t // STEP
    rev = d == 1

    def body(q_ref, i_ref, f_ref, lb_ref, o_ref, st_ref, s_ref):
        j = pl.program_id(0)

        @pl.when(j == 0)
        def _():
            s_ref[...] = jnp.zeros_like(s_ref)

        s0 = s_ref[...]
        st_ref[...] = s0
        o, s1 = _hg_step(s0, q_ref[...], i_ref[...], f_ref[...], lb_ref[...], rev)
        o_ref[...] = o
        s_ref[...] = s1

    def rows(cb):
        return pl.BlockSpec((STEP, w), lambda j: (_scan_order(j, n_ctx, n, rev), cb))

    return pl.pallas_call(
        body, name=name, grid=(n,),
        in_specs=[rows(0), rows(1), rows(2 + d), pl.BlockSpec((2, w), lambda j: (0, 0))],
        out_specs=[rows(0), pl.BlockSpec((None, h, HG_HEAD, HG_HEAD), lambda j: (j, 0, 0, 0))],
        out_shape=[jax.ShapeDtypeStruct((t, w), F32), jax.ShapeDtypeStruct((n, h, HG_HEAD, HG_HEAD), F32)],
        scratch_shapes=[pltpu.VMEM((h, HG_HEAD, HG_HEAD), F32)],
        compiler_params=_params(("arbitrary",)),
    )(p_hg, p_hg, p_hg, lb2)


def _hg_scan_bwd(name, p_hg, lb2, states, do, d, n_ctx, other=()):
    t, w = p_hg.shape[0], lb2.shape[1]
    h = w // HG_HEAD
    n = t // STEP
    rev = d == 1
    no = len(other)

    def body(q_ref, i_ref, f_ref, lb_ref, st_ref, do_ref, *refs):
        dq_ref, di_ref, df_ref, dlb_ref, ds_ref = refs[no:]
        step = pl.program_id(0)

        @pl.when(step == 0)
        def _():
            ds_ref[...] = jnp.zeros_like(ds_ref)
            dlb_ref[...] = jnp.zeros_like(dlb_ref)

        _, vjp = jax.vjp(lambda s0, q, i, f, lb: _hg_step(s0, q, i, f, lb, rev),
                         st_ref[...], q_ref[...], i_ref[...], f_ref[...], lb_ref[...])
        ds0, dq, di, df, dlb = vjp((do_ref[...], ds_ref[...]))
        if no:
            dq, di = refs[0][...] + dq, refs[1][...] + di
        dq_ref[...] = dq.astype(dq_ref.dtype)
        di_ref[...] = di.astype(di_ref.dtype)
        df_ref[...] = df.astype(df_ref.dtype)
        dlb_ref[...] += dlb
        ds_ref[...] = ds0

    def rows(cb):
        return pl.BlockSpec((STEP, w), lambda s: (_scan_order(n - 1 - s, n_ctx, n, rev), cb))

    qi = BF16 if no else F32
    return pl.pallas_call(
        body, name=name, grid=(n,),
        in_specs=[rows(0), rows(1), rows(2 + d), pl.BlockSpec((2, w), lambda s: (0, 0)),
                  pl.BlockSpec((None, h, HG_HEAD, HG_HEAD), lambda s: (n - 1 - s, 0, 0, 0)), rows(0)] + [rows(0)] * no,
        out_specs=[rows(0), rows(0), rows(0), pl.BlockSpec((2, w), lambda s: (0, 0))],
        out_shape=[jax.ShapeDtypeStruct((t, w), qi)] * 2 + [jax.ShapeDtypeStruct((t, w), BF16),
                                                            jax.ShapeDtypeStruct((2, w), F32)],
        scratch_shapes=[pltpu.VMEM((h, HG_HEAD, HG_HEAD), F32)],
        compiler_params=_params(("arbitrary",)),
    )(p_hg, p_hg, p_hg, lb2, states, do, *other)


def _to_heads(a, nh):
    return jnp.stack([a[:, i * RW_HEAD:(i + 1) * RW_HEAD] for i in range(nh)], axis=0)


def _from_heads(a):
    return jnp.concatenate([a[i] for i in range(a.shape[0])], axis=-1)


def _rw_scan_fwd(name, sh, hp, d, n_ctx):
    t = sh.shape[0]
    w = (sh.shape[1] - 4 * RW_LORA) // 3
    nh = w // RW_HEAD
    n = t // RW_STEP
    rev = d == 1
    lo = 3 * w // LANE

    def body(r_ref, k_ref, v_ref, wl_ref, al_ref, w0_ref, w2_ref, a0_ref, a2_ref, kk_ref, ka_ref,
             y_ref, st_ref, s_ref):
        j = pl.program_id(0)

        @pl.when(j == 0)
        def _():
            s_ref[...] = jnp.zeros_like(s_ref)

        s0 = s_ref[...]
        st_ref[...] = s0
        wl = wl_ref[...][:, d * RW_LORA:(d + 1) * RW_LORA]
        al = al_ref[...][:, d * RW_LORA:(d + 1) * RW_LORA]
        y, s1 = _rw_step(s0, _to_heads(r_ref[...], nh), _to_heads(k_ref[...], nh), _to_heads(v_ref[...], nh), wl, al,
                         w0_ref[...], w2_ref[...], a0_ref[...], a2_ref[...], kk_ref[...], ka_ref[...], rev)
        y_ref[...] = _from_heads(y)
        s_ref[...] = s1

    def rows(cb, width=w):
        return pl.BlockSpec((RW_STEP, width), lambda j: (_scan_order(j, n_ctx, n, rev), cb))

    def whole(a):
        nd = a.ndim
        return pl.BlockSpec(a.shape, lambda j: (0,) * nd)

    return pl.pallas_call(
        body, name=name, grid=(n,),
        in_specs=[rows(0), rows(1), rows(2), rows(lo, LANE), rows(lo + 1, LANE)] + [whole(a) for a in hp],
        out_specs=[rows(0), pl.BlockSpec((None, nh, RW_HEAD, RW_HEAD), lambda j: (j, 0, 0, 0))],
        out_shape=[jax.ShapeDtypeStruct((t, w), F32), jax.ShapeDtypeStruct((n, nh, RW_HEAD, RW_HEAD), F32)],
        scratch_shapes=[pltpu.VMEM((nh, RW_HEAD, RW_HEAD), F32)],
        compiler_params=_params(("arbitrary",)),
    )(sh, sh, sh, sh, sh, *hp)


def _rw_scan_bwd_both(name, sh, hps, states, dy, n_ctx):
    t = sh.shape[0]
    w = (sh.shape[1] - 4 * RW_LORA) // 3
    nh = w // RW_HEAD
    n = t // RW_STEP
    lo = 3 * w // LANE
    n_in, n_p = 13, 6

    def body(*refs):
        step = pl.program_id(0)
        ins = [refs[d * n_in:(d + 1) * n_in] for d in range(2)]
        outs = [refs[2 * n_in + d * (1 + n_p):2 * n_in + (d + 1) * (1 + n_p)] for d in range(2)]
        ds_refs = refs[2 * n_in + 2 * (1 + n_p):]

        @pl.when(step == 0)
        def _():
            for d in range(2):
                ds_refs[d][...] = jnp.zeros_like(ds_refs[d])
                for ref in outs[d][1:]:
                    ref[...] = jnp.zeros_like(ref)

        for d in range(2):
            r_ref, k_ref, v_ref, wl_ref, al_ref = ins[d][:5]
            hp_refs, st_ref, dy_ref = ins[d][5:11], ins[d][11], ins[d][12]
            wl = wl_ref[...][:, d * RW_LORA:(d + 1) * RW_LORA]
            al = al_ref[...][:, d * RW_LORA:(d + 1) * RW_LORA]
            _, vjp = jax.vjp(functools.partial(_rw_step, rev=d == 1),
                             st_ref[...], _to_heads(r_ref[...], nh), _to_heads(k_ref[...], nh), _to_heads(v_ref[...], nh),
                             wl, al, *[p[...] for p in hp_refs])
            g = vjp((_to_heads(dy_ref[...], nh), ds_refs[d][...]))
            ds_refs[d][...] = g[0]
            zero = jnp.zeros_like(g[4])
            lora = [zero] * 4
            lora[d], lora[2 + d] = g[4], g[5]
            outs[d][0][...] = jnp.concatenate([_from_heads(g[1]), _from_heads(g[2]), _from_heads(g[3])] + lora, axis=-1)
            for ref, val in zip(outs[d][1:], g[6:]):
                ref[...] += val

    def rows(d, cb, width=w):
        return pl.BlockSpec((RW_STEP, width), lambda s: (_scan_order(n - 1 - s, n_ctx, n, d == 1), cb))

    def whole(a):
        nd = a.ndim
        return pl.BlockSpec(a.shape, lambda s: (0,) * nd)

    in_specs, operands, out_specs, out_shape = [], [], [], []
    for d in range(2):
        in_specs += [rows(d, 0), rows(d, 1), rows(d, 2), rows(d, lo, LANE), rows(d, lo + 1, LANE)]
        in_specs += [whole(a) for a in hps[d]]
        in_specs += [pl.BlockSpec((None, nh, RW_HEAD, RW_HEAD), lambda s: (n - 1 - s, 0, 0, 0)), rows(d, 0)]
        operands += [sh] * 5 + list(hps[d]) + [states[d], dy]
        out_specs += [rows(d, 0, sh.shape[1])] + [whole(a) for a in hps[d]]
        out_shape += [jax.ShapeDtypeStruct(sh.shape, F32)] + [jax.ShapeDtypeStruct(a.shape, F32) for a in hps[d]]
    res = pl.pallas_call(
        body, name=name, grid=(n,), in_specs=in_specs, out_specs=out_specs, out_shape=out_shape,
        scratch_shapes=[pltpu.VMEM((nh, RW_HEAD, RW_HEAD), F32)] * 2, compiler_params=_params(("arbitrary",)),
    )(*operands)
    return [res[0], res[1 + n_p]], [res[1:1 + n_p], res[2 + n_p:]]


def _shift_masks(t, n_ctx_rows):
    row = lax.broadcasted_iota(jnp.int32, (t, 1), 0)
    isx = row >= n_ctx_rows
    pos = jnp.where(isx, row - n_ctx_rows, row)
    col = jnp.where(isx, jnp.bitwise_and(pos, GRID_W - 1), pos)
    ncol = jnp.where(isx, GRID_W, n_ctx_rows)
    n_x = t - n_ctx_rows
    ml = col != 0
    mr = col != ncol - 1
    mu = isx & (pos >= GRID_W)
    md = isx & (pos < n_x - GRID_W)
    return ml, mr, mu, md, isx


def _shift_fwd(name, p, col0, mu, n_ctx_rows):
    t, c = p.shape[0], mu.shape[1]
    cw = LANE

    def body(p_ref, mu_ref, o_ref):
        x = p_ref[...]
        m = mu_ref[...]
        ml, mr, mup, mdn, isx = _shift_masks(t, n_ctx_rows)
        left = jnp.where(ml, pltpu.roll(x, 1, 0), 0.0)
        right = jnp.where(mr, pltpu.roll(x, t - 1, 0), 0.0)
        up = jnp.where(mup, pltpu.roll(x, GRID_W, 0), 0.0)
        down = jnp.where(mdn, pltpu.roll(x, t - GRID_W, 0), 0.0)
        out = x + m[0:1] * (left - x) + m[1:2] * (right - x)
        vert = m[2:3] * (up - x) + m[3:4] * (down - x)
        o_ref[...] = out + jnp.where(isx, vert, 0.0)

    return pl.pallas_call(
        body, name=name, grid=(c // cw,),
        in_specs=[pl.BlockSpec((t, cw), lambda j: (0, col0 + j)), pl.BlockSpec((4, cw), lambda j: (0, j))],
        out_specs=pl.BlockSpec((t, cw), lambda j: (0, j)),
        out_shape=jax.ShapeDtypeStruct((t, c), F32),
        compiler_params=_params(("parallel",)),
    )(p, mu)


def _shift_bwd(name, p, col0, mu, dparts, n_ctx_rows):
    t, c = p.shape[0], mu.shape[1]
    cw = LANE
    npart = len(dparts)

    def body(*refs):
        p_ref, mu_ref = refs[0], refs[1]
        dp_ref, dmu_ref = refs[2 + npart], refs[3 + npart]
        x = p_ref[...]
        m = mu_ref[...]
        g = refs[2][...]
        for r in refs[3:2 + npart]:
            g = g + r[...]
        ml, mr, mup, mdn, isx = _shift_masks(t, n_ctx_rows)
        left = jnp.where(ml, pltpu.roll(x, 1, 0), 0.0)
        right = jnp.where(mr, pltpu.roll(x, t - 1, 0), 0.0)
        up = jnp.where(mup, pltpu.roll(x, GRID_W, 0), 0.0)
        down = jnp.where(mdn, pltpu.roll(x, t - GRID_W, 0), 0.0)
        gx = jnp.where(isx, g, 0.0)
        dmu_ref[...] = jnp.concatenate([
            jnp.sum(g * (left - x), axis=0, keepdims=True), jnp.sum(g * (right - x), axis=0, keepdims=True),
            jnp.sum(gx * (up - x), axis=0, keepdims=True), jnp.sum(gx * (down - x), axis=0, keepdims=True)], axis=0)
        coef = 1.0 - m[0:1] - m[1:2] - jnp.where(isx, m[2:3] + m[3:4], 0.0)
        dp = coef * g
        dp = dp + m[0:1] * pltpu.roll(jnp.where(ml, g, 0.0), t - 1, 0)
        dp = dp + m[1:2] * pltpu.roll(jnp.where(mr, g, 0.0), 1, 0)
        dp = dp + m[2:3] * pltpu.roll(jnp.where(mup, g, 0.0), t - GRID_W, 0)
        dp = dp + m[3:4] * pltpu.roll(jnp.where(mdn, g, 0.0), GRID_W, 0)
        dp_ref[...] = dp.astype(dp_ref.dtype)

    col = pl.BlockSpec((t, cw), lambda j: (0, j))
    par = pl.BlockSpec((4, cw), lambda j: (0, j))
    return pl.pallas_call(
        body, name=name, grid=(c // cw,),
        in_specs=[pl.BlockSpec((t, cw), lambda j: (0, col0 + j)), par] + [col] * npart,
        out_specs=[col, par],
        out_shape=[jax.ShapeDtypeStruct((t, c), BF16), jax.ShapeDtypeStruct((4, c), F32)],
        compiler_params=_params(("parallel",)),
    )(p, mu, *dparts)


def _add_small(name, terms, shape):
    flat2 = [a.reshape(-1, a.shape[-1]) for a in terms]
    return _rowwise(name, lambda *v: _slot_sum(list(v)), flat2, F32).reshape(shape)


def _local_step(x, ctx, mod, norm_g, w_in_st, hg_lb, hg_norm_g, rw_mu, rw_w0, rw_w2, rw_a0, rw_a2,
                rw_kk, rw_ka, rw_rk, rw_gn_g, rw_gn_b, w_hg_st, w_rw_st, w_out, final_g, tgt, my_core):
    seq, dm = x.shape
    n_ctx_rows = ctx.shape[0]
    t = seq + n_ctx_rows
    hw = hg_norm_g.shape[-1]
    rw = rw_kk.shape[-1]
    nh_rw = rw // RW_HEAD
    n_ctx = n_ctx_rows // STEP
    tm = _tile(n_ctx_rows, (256, 128, 64))
    nt = t // tm
    nct = n_ctx_rows // tm
    n_sh_cols = 3 * rw + 4 * RW_LORA

    final_g2 = final_g.reshape(1, dm)
    add = _add_small
    mod3 = mod.reshape(8, 3, dm)

    def pick(i, m3):
        r = jnp.where(i < nct, m3[1], m3[0])
        return r[0:1], r[1:2]

    tokens = [(ctx, 0, dm, 0), (x, 0, dm, nct)]

    def h_fn(i, r, f):
        shift, scale = pick(i, f[1])
        return [_fn_h(jnp.where(i < nct, r[0], r[1]), f[0], scale, shift)], []

    (h,) = _row_call("h_fwd", h_fn, nt, tm, tokens, [norm_g, mod3], [(t, dm, BF16, 0)], [])
    proj = _join_columns("proj_join", _mm_n_st("proj_mm", h, w_in_st))
    p_hg = proj
    rs_tile0 = 5 * hw // LANE
    p_zr = proj[:, 5 * hw + n_sh_cols:5 * hw + n_sh_cols + rw]
    p_gt = proj[:, 5 * hw + n_sh_cols + rw:]

    o_hg, st_hg = [], []
    for d in range(2):
        o, st = _hg_scan_fwd(f"hg_scan_fwd{d}", p_hg, hg_lb[d], d, n_ctx)
        o_hg.append(o)
        st_hg.append(st)

    def hgpost_fn(i, r, f):
        return [_fn_hgpost(r[0], r[1], r[2], f[0])], []

    hg_in = [(o_hg[0], 0, hw, 0), (o_hg[1], 0, hw, 0), (p_hg, 4, hw, 0)]
    (y_hg,) = _row_call("hg_post", hgpost_fn, nt, tm, hg_in, [hg_norm_g], [(t, hw, BF16, 0)], [])

    sh = _shift_fwd("rw_shift", proj, rs_tile0, rw_mu, n_ctx_rows)
    hps = []
    for d in range(2):
        hps.append([rw_w0[d].reshape(nh_rw, 1, RW_HEAD), jnp.swapaxes(rw_w2[d].reshape(RW_LORA, nh_rw, RW_HEAD), 0, 1),
                    rw_a0[d].reshape(nh_rw, 1, RW_HEAD), jnp.swapaxes(rw_a2[d].reshape(RW_LORA, nh_rw, RW_HEAD), 0, 1),
                    rw_kk.reshape(nh_rw, 1, RW_HEAD), rw_ka.reshape(nh_rw, 1, RW_HEAD)])
    y_rw_d, st_rw = [], []
    for d in range(2):
        y, st = _rw_scan_fwd(f"rw_scan_fwd{d}", sh, hps[d], d, n_ctx_rows // RW_STEP)
        y_rw_d.append(y)
        st_rw.append(st)

    rw_full = [rw_a0, rw_a2, rw_ka, rw_rk, rw_gn_g, rw_gn_b]
    lo = 3 * rw // LANE
    rw_in = [(y_rw_d[0], 0, rw, 0), (y_rw_d[1], 0, rw, 0), (sh, 0, rw, 0), (sh, 1, rw, 0), (sh, 2, rw, 0),
             (sh, lo + 1, LANE, 0), (p_zr, 0, rw, 0)]

    def rwpost_fn(i, r, f):
        return [_fn_rwpost(*r, *f)], []

    (y_rw,) = _row_call("rw_post", rwpost_fn, nt, tm, rw_in, rw_full, [(t, rw, BF16, 0)], [])

    a_hg = _mm_n_st("hg_out_mm", y_hg, w_hg_st, joined=True)
    a_rw = _mm_n_st("rw_out_mm", y_rw, w_rw_st, joined=True)
    mg_in = [(a_hg, 0, dm, 0), (a_rw, 0, dm, 0), (p_gt, 0, dm, 0), (p_gt, 1, dm, 0)]
    (merged,) = _row_call("merge", lambda i, r, f: ([_fn_merge(*r)], []), nt, tm, mg_in, [], [(t, dm, BF16, 0)], [])
    o_out = _mm_nn("out_mm", merged, w_out)

    def final_fn(i, r, f):
        gate = f[0][0][2:3]
        loss, vjp = jax.vjp(_fn_final, r[0], r[1], gate, f[1], r[2])
        dx, do, dgate, dfg, _ = vjp(jnp.ones((), F32))
        live = i >= nct
        zero = lambda a: jnp.where(live, a, 0.0)
        dmod = jnp.concatenate([jnp.concatenate([jnp.zeros((1, 2 * dm), F32), zero(dgate)], axis=1),
                                jnp.zeros((7, 3 * dm), F32)], axis=0)
        return [zero(dx), zero(do)], [jnp.broadcast_to(zero(loss), (8, LANE)), dmod, zero(dfg)]

    fin_in = [(x, 0, dm, nct), (o_out, 0, dm, 0), (tgt, 0, dm, nct)]
    dx_res, d_o, loss_acc, dmod_gate, d_final_g = _row_call(
        "final", final_fn, nt, tm, fin_in, [mod3, final_g2], [(t, dm, F32, 0), (t, dm, BF16, 0)],
        [((8, LANE), F32), ((8, 3 * dm), F32), ((1, dm), F32)])

    g_w_out = _mm_tn("d_w_out", merged, d_o)
    d_merged = _mm_nt("d_merged", d_o, w_out)

    def merge_bwd(i, r, f):
        _, vjp = jax.vjp(_fn_merge, r[0], r[1], r[2], r[3])
        da, db, dgh, dgr = vjp(r[4])
        return [da, db, jnp.concatenate([dgh, dgr], axis=1)], []

    da_hg, da_rw, dp_gt = _row_call("merge_bwd", merge_bwd, nt, tm, mg_in + [(d_merged, 0, dm, 0)], [],
                                    [(t, dm, BF16, 0), (t, dm, BF16, 0), (t, 2 * dm, BF16, 0)], [])
    g_w_hg_st = _mm_t_st("d_w_hg", y_hg, da_hg)
    g_w_rw_st = _mm_t_st("d_w_rw", y_rw, da_rw)
    dy_hg = _mm_st_t("d_y_hg", da_hg, w_hg_st)
    dy_rw = _mm_st_t("d_y_rw", da_rw, w_rw_st)

    def hgpost_bwd(i, r, f):
        _, vjp = jax.vjp(_fn_hgpost, r[0], r[1], r[2], f[0])
        dof, _, dz, dg = vjp(r[3])
        return [dof, dz], [dg]

    do_hg, dz_hg, g_hg_norm = _row_call("hg_post_bwd", hgpost_bwd, nt, tm, hg_in + [(dy_hg, 0, hw, 0)], [hg_norm_g],
                                        [(t, hw, F32, 0), (t, hw, BF16, 0)], [((1, hw), F32)])
    dq0, di0, df0, dlb0 = _hg_scan_bwd("hg_scan_bwd0", p_hg, hg_lb[0], st_hg[0], do_hg, 0, n_ctx)
    dq, di, df1, dlb1 = _hg_scan_bwd("hg_scan_bwd1", p_hg, hg_lb[1], st_hg[1], do_hg, 1, n_ctx, other=(dq0, di0))
    g_hg_lb = jnp.stack([dlb0, dlb1], axis=0)

    def rwpost_bwd(i, r, f):
        _, vjp = jax.vjp(_fn_rwpost, *r[:7], *f)
        g = vjp(r[7])
        zl = jnp.zeros((g[5].shape[0], 2 * RW_LORA), F32)
        return [g[0], jnp.concatenate([g[2], g[3], g[4], zl, g[5]], axis=1), g[6]], list(g[7:])

    dy_sum, dsh_p, dz_rw, g_a0_p, g_a2_p, g_ka_p, g_rk, g_gn_g, g_gn_b = _row_call(
        "rw_post_bwd", rwpost_bwd, nt, tm, rw_in + [(dy_rw, 0, rw, 0)], rw_full,
        [(t, rw, F32, 0), (t, n_sh_cols, F32, 0), (t, rw, BF16, 0)], [(a.shape, F32) for a in rw_full])
    dsh_dirs, hp_grads = _rw_scan_bwd_both("rw_scan_bwd", sh, hps, st_rw, dy_sum, n_ctx_rows // RW_STEP)
    dp_rs, g_mu = _shift_bwd("rw_shift_bwd", proj, rs_tile0, rw_mu, [dsh_p] + dsh_dirs, n_ctx_rows)

    def flat(a):
        if a.shape[1] == 1:
            return a.reshape(rw)
        return jnp.swapaxes(a, 0, 1).reshape(RW_LORA, rw)

    g_w0 = jnp.stack([flat(hp_grads[d][0]) for d in range(2)], axis=0)
    g_w2 = jnp.stack([flat(hp_grads[d][1]) for d in range(2)], axis=0)
    g_a0 = add("g_a0", [jnp.stack([flat(hp_grads[d][2]) for d in range(2)], axis=0), g_a0_p], (2, rw))
    g_a2 = add("g_a2", [jnp.stack([flat(hp_grads[d][3]) for d in range(2)], axis=0), g_a2_p], (2, RW_LORA, rw))
    g_kk = add("g_kk", [flat(hp_grads[0][4]).reshape(1, rw), flat(hp_grads[1][4]).reshape(1, rw)], (1, rw))
    g_ka = add("g_ka", [flat(hp_grads[0][5]).reshape(1, rw), flat(hp_grads[1][5]).reshape(1, rw), g_ka_p], (1, rw))

    dproj_st = _split_columns("dproj_split", [dq, di, df0, df1, dz_hg, dp_rs, dz_rw, dp_gt], N_SHARD)
    g_small = {"hg_lb": g_hg_lb, "rw_mu": g_mu, "rw_w0": g_w0, "rw_w2": g_w2, "rw_a0": g_a0, "rw_a2": g_a2}
    split = {n: _split_shards(g_small[n]) for n in _SMALL_SHARDED}
    small_parts = jnp.stack([_pack_small({n: split[n][j] for n in _SMALL_SHARDED}) for j in range(N_SHARD)], axis=0)
    early = {"w_hg_out": g_w_hg_st, "w_rw_out": g_w_rw_st, "w_out": g_w_out.reshape(N_SHARD, dm // N_SHARD, dm),
             "small": small_parts}
    early_chip = [_pair_reduce(f"grads_pair_sum_{n}", a, my_core) for n, a in early.items()]
    g_w_in_st, early_landed = _mm_t_st("d_w_in", h, dproj_st, scatter=tuple(early_chip))
    w_in_chip = _pair_reduce("grads_pair_sum_w_in", g_w_in_st, my_core)
    dh, (w_in_landed,) = _mm_st_t("d_h", dproj_st, w_in_st, scatter=(w_in_chip,))

    def h_bwd(i, r, f):
        shift, scale = pick(i, f[1])
        is_ctx = i < nct
        _, vjp = jax.vjp(_fn_h, jnp.where(is_ctx, r[0], r[1]), f[0], scale, shift)
        ds, dg, dscale, dshift = vjp(r[2])
        row = jnp.concatenate([dshift, dscale, jnp.zeros((1, dm), F32)], axis=1)
        z = jnp.zeros_like(row)
        dmod = jnp.concatenate([jnp.where(is_ctx, z, row), jnp.where(is_ctx, row, z), jnp.zeros((6, 3 * dm), F32)], axis=0)
        return [ds + r[3]], [dg, dmod]

    grad_x, g_norm_g, dmod_h = _row_call(
        "h_bwd", h_bwd, nt, tm, tokens + [(dh, 0, dm, 0), (dx_res, 0, dm, 0)], [norm_g, mod3],
        [(seq, dm, F32, nct)], [((1, dm), F32), ((8, 3 * dm), F32)])
    dmod = add("d_mod", [dmod_h, dmod_gate], (8, 3 * dm))
    grads = dict(
        norm_g=g_norm_g, w_in=(w_in_chip, w_in_landed), hg_norm_g=g_hg_norm, rw_kk=g_kk, rw_ka=g_ka,
        rw_rk=g_rk, rw_gn_g=g_gn_g, rw_gn_b=g_gn_b, final_g=d_final_g.reshape(dm))
    grads.update(zip(early, zip(early_chip, early_landed)))
    return loss_acc[0:1, 0:1], grad_x, dmod, grads


def _my_place():
    return lax.axis_index("x"), lax.axis_index("y"), lax.axis_index("c")


MIN_CHUNK_BYTES = 1 << 18
ROW_ALIGN = 16


def _n_chunks(rows, row_bytes):
    for n in (8, 4, 2):
        if rows % (n * ROW_ALIGN) == 0 and rows // n * row_bytes >= MIN_CHUNK_BYTES:
            return n
    return 1


def _row_bytes(a, lead=1):
    n = a.dtype.itemsize
    for d in a.shape[lead:]:
        n *= d
    return n


def _rows(ref, start, size):
    return ref.at[pl.ds(start, size)]


def _chunked(make, start, size, n):
    cs = size // n
    return [make(start + j * cs, cs) for j in range(n)]


_PEER_CHIPS = 3


def _weights_gather(name, big, small):
    nb, na = len(big), len(big) + len(small)
    arrays = list(big) + list(small)
    n_ici = 6

    def body(*refs):
        outs = refs[na:2 * na]
        send_sems, recv_sems, fsend_sems, frecv_sems = refs[2 * na:]
        x, y, c = _my_place()
        me, sx, sy, sd = 2 * x + y, 2 * (1 - x) + y, 2 * x + (1 - y), 2 * (1 - x) + (1 - y)
        kx, ky, kd = (1 - x, y, c), (x, 1 - y, c), (1 - x, 1 - y, c)

        def ici(a, j, src_slot, dst_slot, to, r0, nr):
            return pltpu.make_async_remote_copy(
                src_ref=_rows(outs[a].at[src_slot], r0, nr), dst_ref=_rows(outs[a].at[dst_slot], r0, nr),
                send_sem=send_sems.at[a, j], recv_sem=recv_sems.at[a, j], device_id=to,
                device_id_type=pl.DeviceIdType.MESH)

        def to_sibling(a, k, slot, r0, nr):
            rows = _rows(outs[a].at[slot], r0, nr)
            return pltpu.make_async_remote_copy(
                src_ref=rows, dst_ref=rows, send_sem=fsend_sems.at[a, k], recv_sem=frecv_sems.at[a, k],
                device_id=(x, y, 1 - c), device_id_type=pl.DeviceIdType.MESH)

        def start(copies):
            for cp in copies:
                cp.start()

        geo = []
        for a in range(nb):
            half = arrays[a].shape[1] // 2
            geo.append((pl.multiple_of(c * half, ROW_ALIGN), pl.multiple_of((1 - c) * half, ROW_ALIGN), half // 2,
                        _n_chunks(half // 2, _row_bytes(arrays[a], 2))))
        plan = [(me, sx, kx, 0), (me, sx, kx, 1), (me, sy, ky, 0), (me, sy, ky, 1), (sx, sd, ky, 0), (sy, sd, kx, 1)]

        def piece(a, j):
            return geo[a][0] + plan[j][3] * geo[a][2]

        for a in range(nb):
            for j in range(4):
                start(_chunked(lambda r0, cs: ici(a, j, me, me, plan[j][2], r0, cs), piece(a, j), geo[a][2], geo[a][3]))
        for a in range(nb, na):
            rows = arrays[a].shape[1]
            for j, to in ((0, kx), (2, ky), (1, kd)):
                ici(a, j, me, me, to, 0, rows).start()
        for a in range(nb):
            for j, first in ((4, 0), (5, 3)):
                src_slot, _, to, _ = plan[j]
                ici(a, first, me, plan[first][1], plan[first][2], piece(a, first), geo[a][2]).wait_recv()
                start(_chunked(lambda r0, cs: ici(a, j, src_slot, src_slot, to, r0, cs), piece(a, j), geo[a][2], geo[a][3]))
        for a in range(nb):
            for j in (1, 2):
                ici(a, j, me, plan[j][1], plan[j][2], piece(a, j), geo[a][2]).wait_recv()
            for k, slot in ((0, sx), (1, sy)):
                start(_chunked(lambda r0, cs: to_sibling(a, k, slot, r0, cs), geo[a][0], 2 * geo[a][2], geo[a][3]))
        for a in range(nb):
            for j in (4, 5):
                ici(a, j, me, sd, plan[j][2], piece(a, j), geo[a][2]).wait_recv()
            start(_chunked(lambda r0, cs: to_sibling(a, 2, sd, r0, cs), geo[a][0], 2 * geo[a][2], geo[a][3]))
        for a in range(nb, na):
            rows = arrays[a].shape[1]
            for j, slot, to in ((0, sx, kx), (2, sy, ky), (1, sd, kd)):
                ici(a, j, me, slot, to, 0, rows).wait_recv()
        for a in range(nb):
            for k, slot in ((0, sx), (1, sy), (2, sd)):
                to_sibling(a, k, slot, geo[a][1], 2 * geo[a][2]).wait_recv()
        for a in range(nb):
            for j in range(n_ici):
                ici(a, j, me, me, plan[j][2], piece(a, j), geo[a][2]).wait_send()
            for k, slot in ((0, sx), (1, sy), (2, sd)):
                to_sibling(a, k, slot, geo[a][0], 2 * geo[a][2]).wait_send()
        for a in range(nb, na):
            rows = arrays[a].shape[1]
            for j, to in ((0, kx), (2, ky), (1, kd)):
                ici(a, j, me, me, to, 0, rows).wait_send()

    hbm = pl.BlockSpec(memory_space=pl.ANY)
    ici_sems = pltpu.SemaphoreType.DMA((na, n_ici))
    pair_sems = pltpu.SemaphoreType.DMA((na, _PEER_CHIPS))
    return pl.pallas_call(
        body, name=name, in_specs=[hbm] * na, out_specs=[hbm] * na,
        out_shape=[jax.ShapeDtypeStruct(a.shape, a.dtype) for a in arrays],
        input_output_aliases={a: a for a in range(na)}, scratch_shapes=[ici_sems, ici_sems, pair_sems, pair_sems],
    )(*arrays)


def _scatter_copy(arrays, ins, outs, send_sems, recv_sems, a, k, slot, r0, nr):
    x, y, c = _my_place()
    px, py = [(1 - x, y), (x, 1 - y), (1 - x, 1 - y)][k]
    return pltpu.make_async_remote_copy(
        src_ref=_rows(ins[a].at[2 * px + py], r0, nr), dst_ref=_rows(outs[a].at[slot], r0, nr),
        send_sem=send_sems.at[a, k], recv_sem=recv_sems.at[a, k], device_id=(px, py, c),
        device_id_type=pl.DeviceIdType.MESH)


def _scatter_start(arrays, ins, outs, send_sems, recv_sems):
    x, y, _ = _my_place()
    for a in range(len(arrays)):
        rows = arrays[a].shape[1]
        for k in range(_PEER_CHIPS):
            for cp in _chunked(lambda r0, cs: _scatter_copy(arrays, ins, outs, send_sems, recv_sems, a, k, 2 * x + y, r0, cs),
                               0, rows, _n_chunks(rows, _row_bytes(arrays[a], 2))):
                cp.start()


def _scatter_wait(arrays, ins, outs, send_sems, recv_sems):
    x, y, _ = _my_place()
    peer_slot = [2 * (1 - x) + y, 2 * x + (1 - y), 2 * (1 - x) + (1 - y)]
    for k in range(_PEER_CHIPS):
        for a in range(len(arrays)):
            _scatter_copy(arrays, ins, outs, send_sems, recv_sems, a, k, peer_slot[k], 0, arrays[a].shape[1]).wait_recv()
    for a in range(len(arrays)):
        for k in range(_PEER_CHIPS):
            _scatter_copy(arrays, ins, outs, send_sems, recv_sems, a, k, 2 * x + y, 0, arrays[a].shape[1]).wait_send()


PAIR_TILE_BYTES = 4 << 20


def _pair_exchange(name, a, place, reduce, out_dtype):
    rows, cols = a.shape[-2], a.shape[-1]
    half = rows // 2 if reduce else rows
    tr = _row_tile_for(half, cols, budget=PAIR_TILE_BYTES)
    nh = half // tr
    n_steps = (N_SHARD if reduce else 1) * nh

    def body(pc_ref, *refs):
        if reduce:
            keep_ref, send_ref, o_ref, land, send_sems, recv_sems, credit, wire = refs
            wire[...] = send_ref[...].astype(BF16)
            src = wire
        else:
            send_ref, o_ref, land, send_sems, recv_sems, credit = refs
            src = send_ref
        x, y, c = _my_place()
        other = (x, y, 1 - c)
        t = pl.program_id(0) * nh + pl.program_id(1) if reduce else pl.program_id(0)
        slot = t % 2

        @pl.when(t >= 2)
        def _():
            pl.semaphore_wait(credit, 1)

        copy = pltpu.make_async_remote_copy(
            src_ref=src, dst_ref=land.at[slot], send_sem=send_sems.at[slot], recv_sem=recv_sems.at[slot],
            device_id=other, device_id_type=pl.DeviceIdType.MESH)
        copy.start()
        copy.wait_recv()
        got = land[slot]
        o_ref[...] = ((keep_ref[...] + got.astype(F32)) if reduce else got).astype(out_dtype)
        copy.wait_send()

        @pl.when(t < n_steps - 2)
        def _():
            pl.semaphore_signal(credit, inc=1, device_id=other, device_id_type=pl.DeviceIdType.MESH)

    if reduce:
        grid = (N_SHARD, nh)
        in_specs = [pl.BlockSpec((None, tr, cols), lambda j, i, pc: (j, pc[0] * nh + i, 0)),
                    pl.BlockSpec((None, tr, cols), lambda j, i, pc: (j, (1 - pc[0]) * nh + i, 0))]
        out_spec = pl.BlockSpec((None, tr, cols), lambda j, i, pc: (j, i, 0))
        out_shape = jax.ShapeDtypeStruct((N_SHARD, half, cols), out_dtype)
        operands = (a, a)
        sem = ("arbitrary", "arbitrary")
    else:
        grid = (nh,)
        in_specs = [pl.BlockSpec((tr, cols), lambda i, pc: (i, 0))]
        out_spec = pl.BlockSpec((tr, cols), lambda i, pc: (i, 0))
        out_shape = jax.ShapeDtypeStruct((half, cols), out_dtype)
        operands = (a,)
        sem = ("arbitrary",)
    return pl.pallas_call(
        body, name=name,
        grid_spec=pltpu.PrefetchScalarGridSpec(
            num_scalar_prefetch=1, grid=grid, in_specs=in_specs, out_specs=out_spec,
            scratch_shapes=[pltpu.VMEM((2, tr, cols), BF16 if reduce else a.dtype), pltpu.SemaphoreType.DMA((2,)),
                            pltpu.SemaphoreType.DMA((2,)), pltpu.SemaphoreType.REGULAR] +
                           ([pltpu.VMEM((tr, cols), BF16)] if reduce else [])),
        out_shape=out_shape, compiler_params=_params(sem),
    )(place, *operands)


def _cast_into_slot(name, a, chip):
    rows, cols = a.shape
    tm = _row_tile_for(rows, cols)

    def body(pc_ref, a_ref, o_ref):
        o_ref[...] = a_ref[...].astype(BF16)

    return pl.pallas_call(
        body, name=name,
        grid_spec=pltpu.PrefetchScalarGridSpec(
            num_scalar_prefetch=1, grid=(rows // tm,), in_specs=[pl.BlockSpec((tm, cols), lambda i, pc: (i, 0))],
            out_specs=pl.BlockSpec((None, tm, cols), lambda i, pc: (pc[0], i, 0))),
        out_shape=jax.ShapeDtypeStruct((N_SHARD, rows, cols), BF16), compiler_params=_params(("parallel",)),
    )(chip, a)


def _pair_reduce(name, a, place):
    rows, cols = a.shape[-2], a.shape[-1]
    half = rows // 2
    tr = _row_tile_for(half, cols, budget=PAIR_TILE_BYTES)
    nh = half // tr
    n = N_SHARD * nh

    def body(pc_ref, keep_ref, send_ref, o_ref, wire, land, send_sems, recv_sems, credit):
        x, y, c = _my_place()
        other = (x, y, 1 - c)
        t = pl.program_id(0)

        def copy(slot):
            return pltpu.make_async_remote_copy(
                src_ref=wire.at[slot], dst_ref=land.at[slot], send_sem=send_sems.at[slot], recv_sem=recv_sems.at[slot],
                device_id=other, device_id_type=pl.DeviceIdType.MESH)

        @pl.when(t < n)
        def _():
            slot = t % 2
            wire[slot] = send_ref[...].astype(BF16)

            @pl.when(t >= 2)
            def _():
                pl.semaphore_wait(credit, 1)

            copy(slot).start()

        @pl.when(t >= 1)
        def _():
            slot = (t - 1) % 2
            copy(slot).wait_recv()
            o_ref[...] = (keep_ref[...] + land[slot].astype(F32)).astype(BF16)
            copy(slot).wait_send()

            @pl.when(t - 1 < n - 2)
            def _():
                pl.semaphore_signal(credit, inc=1, device_id=other, device_id_type=pl.DeviceIdType.MESH)

    def sent(t):
        return jnp.minimum(t, n - 1)

    def summed(t):
        return jnp.maximum(t - 1, 0)

    return pl.pallas_call(
        body, name=name,
        grid_spec=pltpu.PrefetchScalarGridSpec(
            num_scalar_prefetch=1, grid=(n + 1,),
            in_specs=[pl.BlockSpec((None, tr, cols), lambda t, pc: (summed(t) // nh, pc[0] * nh + summed(t) % nh, 0)),
                      pl.BlockSpec((None, tr, cols), lambda t, pc: (sent(t) // nh, (1 - pc[0]) * nh + sent(t) % nh, 0))],
            out_specs=pl.BlockSpec((None, tr, cols), lambda t, pc: (summed(t) // nh, summed(t) % nh, 0)),
            scratch_shapes=[pltpu.VMEM((2, tr, cols), BF16), pltpu.VMEM((2, tr, cols), BF16),
                            pltpu.SemaphoreType.DMA((2,)), pltpu.SemaphoreType.DMA((2,)), pltpu.SemaphoreType.REGULAR]),
        out_shape=jax.ShapeDtypeStruct((N_SHARD, half, cols), BF16), compiler_params=_params(("arbitrary",)),
    )(place, a, a)


SUM_SWAP_TILE_BYTES = 2 << 20


def _sum_and_swap(name, landed, sent, chip):
    ns, rows, cols = landed.shape
    tr = _row_tile_for(rows, cols, budget=SUM_SWAP_TILE_BYTES)
    n_steps = rows // tr

    def body(pc_ref, *refs):
        own_ref, mine_ref, theirs_ref, wire, land, send_sems, recv_sems, credit = refs[ns:]
        me = pc_ref[0]
        x, y, c = _my_place()
        other = (x, y, 1 - c)
        t = pl.program_id(0)

        def copy(slot):
            return pltpu.make_async_remote_copy(
                src_ref=wire.at[slot], dst_ref=land.at[slot], send_sem=send_sems.at[slot], recv_sem=recv_sems.at[slot],
                device_id=other, device_id_type=pl.DeviceIdType.MESH)

        @pl.when(t < n_steps)
        def _():
            slot = t % 2
            total = _slot_sum([jnp.where(me == j, own_ref[...], refs[j][...]).astype(F32) for j in range(ns)])
            mine_ref[...] = total
            wire[slot] = total

            @pl.when(t >= 2)
            def _():
                pl.semaphore_wait(credit, 1)

            copy(slot).start()

        @pl.when(t >= 1)
        def _():
            slot = (t - 1) % 2
            copy(slot).wait_recv()
            theirs_ref[...] = land[slot]
            copy(slot).wait_send()

            @pl.when(t - 1 < n_steps - 2)
            def _():
                pl.semaphore_signal(credit, inc=1, device_id=other, device_id_type=pl.DeviceIdType.MESH)

    def ahead(i):
        return jnp.minimum(i, n_steps - 1)

    def landed_spec(j):
        return pl.BlockSpec((None, tr, cols), lambda i, pc: (jnp.where(pc[0] == j, (j + 1) % ns, j), ahead(i), 0))

    return pl.pallas_call(
        body, name=name,
        grid_spec=pltpu.PrefetchScalarGridSpec(
            num_scalar_prefetch=1, grid=(n_steps + 1,),
            in_specs=[landed_spec(j) for j in range(ns)] +
                     [pl.BlockSpec((None, tr, cols), lambda i, pc: (pc[0], ahead(i), 0))],
            out_specs=[pl.BlockSpec((tr, cols), lambda i, pc: (ahead(i), 0)),
                       pl.BlockSpec((tr, cols), lambda i, pc: (jnp.maximum(i - 1, 0), 0))],
            scratch_shapes=[pltpu.VMEM((2, tr, cols), F32), pltpu.VMEM((2, tr, cols), F32), pltpu.SemaphoreType.DMA((2,)),
                            pltpu.SemaphoreType.DMA((2,)), pltpu.SemaphoreType.REGULAR]),
        out_shape=[jax.ShapeDtypeStruct((rows, cols), F32)] * 2, compiler_params=_params(("arbitrary",)),
    )(chip, *([landed] * ns), sent)


def _gather_all(name, a):
    def body(in_ref, out_ref, send_sems, recv_sems, local_sem):
        x, y, c = _my_place()
        me = 4 * x + 2 * y + c

        def peer(k):
            return (x ^ (k >> 2), y ^ ((k >> 1) & 1), c ^ (k & 1))

        def remote(k, land):
            return pltpu.make_async_remote_copy(
                src_ref=in_ref, dst_ref=out_ref.at[land], send_sem=send_sems.at[k - 1], recv_sem=recv_sems.at[k - 1],
                device_id=peer(k), device_id_type=pl.DeviceIdType.MESH)

        local = pltpu.make_async_copy(in_ref, out_ref.at[me], local_sem)
        local.start()
        for k in range(1, N_DEV):
            remote(k, me).start()
        for k in range(1, N_DEV):
            px, py, pc = peer(k)
            remote(k, 4 * px + 2 * py + pc).wait_recv()
        for k in range(1, N_DEV):
            remote(k, me).wait_send()
        local.wait()

    hbm = pl.BlockSpec(memory_space=pl.ANY)
    return pl.pallas_call(
        body, name=name, in_specs=[hbm], out_specs=hbm,
        out_shape=jax.ShapeDtypeStruct((N_DEV,) + a.shape, a.dtype),
        scratch_shapes=[pltpu.SemaphoreType.DMA((N_DEV - 1,)), pltpu.SemaphoreType.DMA((N_DEV - 1,)), pltpu.SemaphoreType.DMA],
    )(a)


def _row_tile_for(rows, cols, budget=1 << 20):
    if rows * cols * 4 <= budget:
        return rows
    for tm in (1024, 512, 256, 128, 64, 32, 16, 8):
        if rows % tm == 0 and tm * cols * 4 <= budget:
            return tm
    return rows


def _slot_sum(vals):
    g = vals[0]
    for v in vals[1:]:
        g = g + v
    return g


def _rowwise(name, fn, arrays, out_dtype):
    rows, cols = arrays[0].shape
    tm = _row_tile_for(rows, cols)

    def body(*refs):
        refs[-1][...] = fn(*[r[...] for r in refs[:-1]]).astype(out_dtype)

    blk = pl.BlockSpec((tm, cols), lambda i: (i, 0))
    return pl.pallas_call(
        body, name=name, grid=(rows // tm,), in_specs=[blk] * len(arrays), out_specs=blk,
        out_shape=jax.ShapeDtypeStruct((rows, cols), out_dtype), compiler_params=_params(("parallel",)),
    )(*arrays)


def _sum_slots(name, st):
    ns, rows, cols = st.shape
    tm = _row_tile_for(rows, cols)

    def body(s_ref, o_ref):
        o_ref[...] = _slot_sum([s_ref[j].astype(F32) for j in range(ns)])

    return pl.pallas_call(
        body, name=name, grid=(rows // tm,),
        in_specs=[pl.BlockSpec((ns, tm, cols), lambda i: (0, i, 0))],
        out_specs=pl.BlockSpec((tm, cols), lambda i: (i, 0)),
        out_shape=jax.ShapeDtypeStruct((rows, cols), F32),
        compiler_params=_params(("parallel",)),
    )(st)


ADAM_TILE_BYTES = 1 << 20


def _adam_update(g, p_ref, m_ref, v_ref, go_ref, d_ref, mo_ref, vo_ref):
    mn = ADAM_B1 * m_ref[...] + (1.0 - ADAM_B1) * g
    vn = ADAM_B2 * v_ref[...] + (1.0 - ADAM_B2) * jnp.square(g)
    m_hat = mn / (1.0 - ADAM_B1 ** ADAM_STEP)
    v_hat = vn / (1.0 - ADAM_B2 ** ADAM_STEP)
    go_ref[...] = g
    d_ref[...] = -ADAM_LR * (m_hat / (jnp.sqrt(v_hat) + ADAM_EPS) + ADAM_WD * p_ref[...])
    mo_ref[...] = mn
    vo_ref[...] = vn


def _adamw(name, p, m, v, gst):
    rows, cols = p.shape
    ns = gst.shape[0]
    tm = _row_tile_for(rows, cols, budget=ADAM_TILE_BYTES)

    def body(p_ref, m_ref, v_ref, g_ref, *outs):
        _adam_update(_slot_sum([g_ref[j] for j in range(ns)]), p_ref, m_ref, v_ref, *outs)

    blk = pl.BlockSpec((tm, cols), lambda i: (i, 0))
    return pl.pallas_call(
        body, name=name, grid=(rows // tm,),
        in_specs=[blk, blk, blk, pl.BlockSpec((ns, tm, cols), lambda i: (0, i, 0))],
        out_specs=[blk] * 4, out_shape=[jax.ShapeDtypeStruct((rows, cols), F32)] * 4,
        compiler_params=_params(("parallel",)),
    )(p, m, v, gst)


def _adamw_halves(name, p, m, v, mine, theirs, place, scatter=()):
    rows, cols = p.shape
    half = rows // 2
    tm = _row_tile_for(half, cols, budget=ADAM_TILE_BYTES)
    nh = half // tm
    ns = len(scatter)

    def body(pc_ref, p_ref, m_ref, v_ref, mine_ref, theirs_ref, *refs):
        if ns:
            sc_refs = (refs[:ns], refs[ns + 4:2 * ns + 4]) + tuple(refs[2 * ns + 4:])
            at = pl.program_id(0) * nh + pl.program_id(1)
            pl.when(at == 0)(lambda: _scatter_start(scatter, *sc_refs))
        g = jnp.where(pl.program_id(0) == pc_ref[0], mine_ref[...], theirs_ref[...])
        _adam_update(g, p_ref, m_ref, v_ref, *refs[ns:ns + 4])
        if ns:
            pl.when(at == 2 * nh - 1)(lambda: _scatter_wait(scatter, *sc_refs))

    blk = pl.BlockSpec((tm, cols), lambda h, i, pc: (h * nh + i, 0))
    hblk = pl.BlockSpec((tm, cols), lambda h, i, pc: (i, 0))
    hbm = pl.BlockSpec(memory_space=pl.ANY)
    res = pl.pallas_call(
        body, name=name,
        grid_spec=pltpu.PrefetchScalarGridSpec(
            num_scalar_prefetch=1, grid=(2, nh), in_specs=[blk, blk, blk, hblk, hblk] + [hbm] * ns,
            out_specs=[blk] * 4 + [hbm] * ns,
            scratch_shapes=[pltpu.SemaphoreType.DMA((ns, _PEER_CHIPS))] * 2 if ns else []),
        out_shape=[jax.ShapeDtypeStruct((rows, cols), F32)] * 4 + [jax.ShapeDtypeStruct(s.shape, s.dtype) for s in scatter],
        compiler_params=_params(("arbitrary", "arbitrary") if ns else ("parallel", "parallel")),
    )(place, p, m, v, mine, theirs, *scatter)
    return (list(res[:4]), list(res[4:])) if ns else res


def _pack(parts, width=LANE, mult=8):
    flat = jnp.concatenate([a.reshape(-1) for a in parts])
    n = flat.shape[0]
    per = width * mult
    total = -(-n // per) * per
    return jnp.pad(flat, (0, total - n)).reshape(total // width, width)


def _unpack(packed, shapes):
    flat = packed.reshape(-1)
    out, off = [], 0
    for s in shapes:
        n = 1
        for d in s:
            n *= d
        out.append(flat[off:off + n].reshape(s))
        off += n
    return out


_SMALL_SHARDED = ("hg_lb", "rw_mu", "rw_w0", "rw_w2", "rw_a0", "rw_a2")
_REPLICATED = ("c_ctx", "ada_b", "norm_g", "hg_norm_g", "rw_kk", "rw_ka", "rw_rk", "rw_gn_g", "rw_gn_b", "final_g")
_GATHERED = ("w_in", "w_hg_out", "w_rw_out", "w_out")
_WEIGHTS = ("c_ctx", "ada_w", "ada_b", "norm_g", "w_in", "hg_lb", "hg_norm_g", "rw_mu", "rw_w0", "rw_w2", "rw_a0", "rw_a2",
            "rw_kk", "rw_ka", "rw_rk", "rw_gn_g", "rw_gn_b", "w_hg_out", "w_rw_out", "w_out", "final_g")


def _pack_small(d):
    return _pack([d[n] for n in _SMALL_SHARDED], mult=2 * ROW_ALIGN)


def _join_shards(st):
    a = jnp.moveaxis(st, 0, -2)
    return a.reshape(a.shape[:-2] + (a.shape[-2] * a.shape[-1],))


def _split_shards(a):
    s = a.reshape(a.shape[:-1] + (N_SHARD, a.shape[-1] // N_SHARD))
    return jnp.moveaxis(s, -2, 0)


def kernel(x, c, ctx, c_ctx, ada_w, ada_b, norm_g, w_in, hg_lb, hg_norm_g, rw_mu, rw_w0, rw_w2, rw_a0, rw_a2, rw_kk, rw_ka, rw_rk, rw_gn_g, rw_gn_b, w_hg_out, w_rw_out, w_out, final_g, loss_target, m_c_ctx, m_ada_w, m_ada_b, m_norm_g, m_w_in, m_hg_lb, m_hg_norm_g, m_rw_mu, m_rw_w0, m_rw_w2, m_rw_a0, m_rw_a2, m_rw_kk, m_rw_ka, m_rw_rk, m_rw_gn_g, m_rw_gn_b, m_w_hg_out, m_w_rw_out, m_w_out, m_final_g, v_c_ctx, v_ada_w, v_ada_b, v_norm_g, v_w_in, v_hg_lb, v_hg_norm_g, v_rw_mu, v_rw_w0, v_rw_w2, v_rw_a0, v_rw_a2, v_rw_kk, v_rw_ka, v_rw_rk, v_rw_gn_g, v_rw_gn_b, v_w_hg_out, v_w_rw_out, v_w_out, v_final_g):
    w = dict(c_ctx=c_ctx, ada_w=ada_w, ada_b=ada_b, norm_g=norm_g, w_in=w_in, hg_lb=hg_lb, hg_norm_g=hg_norm_g, rw_mu=rw_mu,
             rw_w0=rw_w0, rw_w2=rw_w2, rw_a0=rw_a0, rw_a2=rw_a2, rw_kk=rw_kk, rw_ka=rw_ka, rw_rk=rw_rk, rw_gn_g=rw_gn_g,
             rw_gn_b=rw_gn_b, w_hg_out=w_hg_out, w_rw_out=w_rw_out, w_out=w_out, final_g=final_g)
    m = dict(c_ctx=m_c_ctx, ada_w=m_ada_w, ada_b=m_ada_b, norm_g=m_norm_g, w_in=m_w_in, hg_lb=m_hg_lb, hg_norm_g=m_hg_norm_g,
             rw_mu=m_rw_mu, rw_w0=m_rw_w0, rw_w2=m_rw_w2, rw_a0=m_rw_a0, rw_a2=m_rw_a2, rw_kk=m_rw_kk, rw_ka=m_rw_ka,
             rw_rk=m_rw_rk, rw_gn_g=m_rw_gn_g, rw_gn_b=m_rw_gn_b, w_hg_out=m_w_hg_out, w_rw_out=m_w_rw_out, w_out=m_w_out,
             final_g=m_final_g)
    v = dict(c_ctx=v_c_ctx, ada_w=v_ada_w, ada_b=v_ada_b, norm_g=v_norm_g, w_in=v_w_in, hg_lb=v_hg_lb, hg_norm_g=v_hg_norm_g,
             rw_mu=v_rw_mu, rw_w0=v_rw_w0, rw_w2=v_rw_w2, rw_a0=v_rw_a0, rw_a2=v_rw_a2, rw_kk=v_rw_kk, rw_ka=v_rw_ka,
             rw_rk=v_rw_rk, rw_gn_g=v_rw_gn_g, rw_gn_b=v_rw_gn_b, w_hg_out=v_w_hg_out, w_rw_out=v_w_rw_out, w_out=v_w_out,
             final_g=v_final_g)

    def mat(a):
        return a.reshape(a.shape[-2], a.shape[-1])

    my_core = lax.axis_index("c").astype(jnp.int32).reshape(1)
    my_chip = (2 * lax.axis_index("x") + lax.axis_index("y")).astype(jnp.int32).reshape(1)

    my_dev = 2 * my_chip[0] + my_core[0]
    dm = x.shape[-1]
    ada_cols = ada_w.shape[-1]

    c_all = _gather_all("cond_gather", c.reshape(1, dm)).reshape(N_DEV, dm)
    cond16 = jnp.concatenate([c_all, c_ctx.reshape(1, dm), jnp.zeros((7, dm), F32)], axis=0)
    (sc16,) = _row_call("cond_silu", lambda i, r, f: ([jax.nn.silu(r[0])], []), 1, 16, [(cond16, 0, dm, 0)], [],
                        [(16, dm, F32, 0)], [])
    mod_here = _mm_nn("mod_mm", sc16, mat(ada_w))
    mod_all = _gather_all("mod_gather", mod_here)
    mod_rows = jnp.concatenate([mod_all[2 * j] for j in range(N_SHARD)], axis=1)
    mine = lax.dynamic_slice_in_dim(mod_rows, my_dev, 1, axis=0)
    mod = _add_small("mod_bias", [jnp.concatenate([mine, mod_rows[N_DEV:N_DEV + 1], jnp.zeros((6, 3 * dm), F32)], axis=0),
                                  jnp.broadcast_to(ada_b, (8, 3 * dm))], (8, 3 * dm))

    small_shapes = [w[n].shape for n in _SMALL_SHARDED]
    big_bf = [_cast_into_slot(f"to_bf16_{n}", mat(w[n]), my_chip) for n in _GATHERED]
    small_mine = _pack_small(w)
    small_slots = lax.dynamic_update_slice(jnp.zeros((N_SHARD,) + small_mine.shape, F32), small_mine[None], (my_chip[0], 0, 0))
    gathered = _weights_gather("weights_gather", big_bf, [small_slots])
    w_in_st, w_hg_st, w_rw_st, w_out_st, small_st = gathered
    full_small = {}
    per_chip = [_unpack(small_st[j], small_shapes) for j in range(N_SHARD)]
    for i, n in enumerate(_SMALL_SHARDED):
        full_small[n] = _join_shards(jnp.stack([per_chip[j][i] for j in range(N_SHARD)], axis=0))
    w_out_full = w_out_st.reshape(dm, dm)

    loss_b, grad_x, dmod, g = _local_step(
        x[0], ctx[0], mod, norm_g, w_in_st, full_small["hg_lb"], hg_norm_g, full_small["rw_mu"][0],
        full_small["rw_w0"][0], full_small["rw_w2"][0], full_small["rw_a0"][0], full_small["rw_a2"][0], rw_kk, rw_ka, rw_rk,
        rw_gn_g, rw_gn_b, w_hg_st, w_rw_st, w_out_full, final_g, loss_target[0], my_core)
    loss = lax.psum(loss_b[0, 0], ("x", "y", "c"))

    dmod_all = _gather_all("dmod_gather", dmod[0:2])
    dmod_here = lax.dynamic_slice_in_dim(dmod_all, my_chip[0] * ada_cols, ada_cols, axis=2)
    d_ctx_row = _add_small("d_mod_ctx", [dmod_here[j, 1:2] for j in range(N_DEV)], (1, ada_cols))
    dm16 = jnp.concatenate([dmod_here[:, 0], d_ctx_row, jnp.zeros((7, ada_cols), F32)], axis=0)
    g_ada_here = _mm_tn("d_ada_w", sc16, dm16)
    d_sc16 = _mm_nt("d_cond", dm16, mat(ada_w))

    def cond_bwd(i, r, f):
        _, vjp = jax.vjp(jax.nn.silu, r[0])
        return [vjp(r[1])[0]], []

    (d_cond16,) = _row_call("cond_bwd", cond_bwd, 1, 16, [(cond16, 0, dm, 0), (d_sc16, 0, dm, 0)], [], [(16, dm, F32, 0)], [])
    g["c_ctx"] = jnp.where(my_core[0] == 0, d_cond16[N_DEV], 0.0)
    g["ada_b"] = _add_small("g_ada_b", [dmod[0:1], dmod[1:2]], (1, 3 * dm))

    def finish(name, chip_sum, landed):
        return _sum_and_swap(f"grads_sum_swap_{name}", landed, chip_sum, my_chip)

    res = {}
    rep_shapes = [w[n].shape for n in _REPLICATED]
    rep_all = _gather_all("grads_replicated", _pack([g[n].reshape(w[n].shape) for n in _REPLICATED]))
    outs = _adamw("adamw_ada_w", mat(ada_w), mat(m["ada_w"]), mat(v["ada_w"]), g_ada_here[None])
    res["ada_w"] = [o.reshape(ada_w.shape) for o in outs]
    for n in _GATHERED:
        outs = _adamw_halves(f"adamw_{n}", mat(w[n]), mat(m[n]), mat(v[n]), *finish(n, *g[n]), my_core)
        res[n] = [o.reshape(w[n].shape) for o in outs]
    outs = _adamw_halves("adamw_small", small_mine, _pack_small(m), _pack_small(v), *finish("small", *g["small"]), my_core)
    for i, vals in enumerate(zip(*[_unpack(o, small_shapes) for o in outs])):
        res[_SMALL_SHARDED[i]] = list(vals)
    outs = _adamw("adamw_replicated", _pack([w[n] for n in _REPLICATED]), _pack([m[n] for n in _REPLICATED]),
                  _pack([v[n] for n in _REPLICATED]), rep_all)
    for i, vals in enumerate(zip(*[_unpack(o, rep_shapes) for o in outs])):
        res[_REPLICATED[i]] = list(vals)

    return (loss, grad_x[None], *[res[n][0] for n in _WEIGHTS], *[res[n][1] for n in _WEIGHTS],
            *[res[n][2] for n in _WEIGHTS], *[res[n][3] for n in _WEIGHTS])
```
